```python
import math
import jax, jax.numpy as jnp
from jax import lax
import numpy as np

D_MODEL = 2048
BATCH = 2
SEQ = 4096
DEPTH = 1

GRID_W = 64
CTX_LEN = 256
RET_HEADS = 4
RET_DK = 256
RET_DV = 256
GLA_HEADS = 4
GLA_DK = 128
GLA_DV = 256
GLA_LOW_RANK = 16
GLA_GATE_NORM = 16.0
CHUNK = 64
D_FF = ((8 * D_MODEL // 3 + 255) // 256) * 256
ROPE_BASE = 10000.0
EPS = 1e-6

RET_QK = RET_HEADS * RET_DK
RET_VW = RET_HEADS * RET_DV
GLA_KW = GLA_HEADS * GLA_DK
GLA_VW = GLA_HEADS * GLA_DV
SPLIT_SIZES = (RET_QK, RET_VW, GLA_KW, GLA_VW, 2 * GLA_LOW_RANK,
               RET_QK, RET_VW, GLA_KW, GLA_VW, D_MODEL, D_MODEL)
N_CTX_COLS = RET_QK + RET_VW + GLA_KW + GLA_VW + 2 * GLA_LOW_RANK
D_IN = sum(SPLIT_SIZES)

kernel_name = 'hybrid_retention_gla_block'


def rms_norm(t, w):
    tf = t.astype(jnp.float32)
    return tf * lax.rsqrt(jnp.mean(tf * tf, axis=-1, keepdims=True) + EPS) * w


def split_cols(p, sizes):
    offsets = []
    acc = 0
    for s in sizes[:-1]:
        acc += s
        offsets.append(acc)
    return jnp.split(p, offsets, axis=-1)


def to_heads(t, n_heads):
    b, l, _ = t.shape
    return t.reshape(b, l, n_heads, -1).transpose(0, 2, 1, 3).astype(jnp.float32)


def merge_heads(t):
    b, h, l, d = t.shape
    return t.transpose(0, 2, 1, 3).reshape(b, l, h * d)


def axial_rope_tables(pos_r, pos_c, dk):
    n_f = dk // 4
    inv = ROPE_BASE ** (-jnp.arange(n_f, dtype=jnp.float32) / n_f)
    ang = jnp.concatenate([pos_r[:, None] * inv, pos_c[:, None] * inv], axis=-1)
    return jnp.cos(ang), jnp.sin(ang)


def apply_rope(t, cos, sin):
    half = t.shape[-1] // 2
    t1, t2 = t[..., :half], t[..., half:]
    return jnp.concatenate([t1 * cos - t2 * sin, t1 * sin + t2 * cos], axis=-1)


def head_layer_norm(o):
    mu = jnp.mean(o, axis=-1, keepdims=True)
    var = jnp.mean(jnp.square(o - mu), axis=-1, keepdims=True)
    return (o - mu) * lax.rsqrt(var + EPS)


def head_rms_norm(o):
    return o * lax.rsqrt(jnp.mean(o * o, axis=-1, keepdims=True) + EPS)


def ctx_final_state(k, v, log_a):
    b = jnp.cumsum(log_a, axis=2)
    w = jnp.exp(b[:, :, -1:, :] - b)
    return jnp.einsum('bhjd,bhjv->bhdv', k * w, v)


def chunk_recurrence(q, k, v, log_a, s0, exclusive):
    bsz, nh, length, _ = q.shape
    dv = v.shape[-1]
    n_chunks = length // CHUNK
    scalar = log_a.shape[-1] == 1
    mask = jnp.tril(jnp.ones((CHUNK, CHUNK), dtype=bool), k=-1 if exclusive else 0)

    def chunks(t):
        return jnp.moveaxis(t.reshape(bsz, nh, n_chunks, CHUNK, t.shape[-1]), 2, 0)

    def step(s, inp):
        qc, kc, vc, ac = inp
        b = jnp.cumsum(ac, axis=2)
        b_last = b[:, :, -1:, :]
        diff = b[:, :, :, None, :] - b[:, :, None, :, :]
        decay = jnp.exp(jnp.where(mask[:, :, None], diff, -jnp.inf))
        if scalar:
            scores = jnp.einsum('bhid,bhjd->bhij', qc, kc) * decay[..., 0]
        else:
            scores = jnp.einsum('bhid,bhjd,bhijd->bhij', qc, kc, decay)
        o = (jnp.einsum('bhij,bhjv->bhiv', scores, vc)
             + jnp.einsum('bhid,bhdv->bhiv', qc * jnp.exp(b), s))
        s_new = (jnp.exp(b_last[:, :, 0, :, None]) * s
                 + jnp.einsum('bhjd,bhjv->bhdv', kc * jnp.exp(b_last - b), vc))
        return s_new, o

    _, o = lax.scan(step, s0, (chunks(q), chunks(k), chunks(v), chunks(log_a)))
    return jnp.moveaxis(o, 0, 2).reshape(bsz, nh, length, dv)


def bidirectional_recurrence(q, k, v, la_f, la_b, kc, vc, lac_f, lac_b):
    flip = lambda t: jnp.flip(t, axis=2)
    s_f = ctx_final_state(kc, vc, lac_f)
    s_b = ctx_final_state(flip(kc), flip(vc), flip(lac_b))
    o_f = chunk_recurrence(q, k, v, la_f, s_f, False)
    o_b = flip(chunk_recurrence(flip(q), flip(k), flip(v), flip(la_b), s_b, True))
    return o_f + o_b


def hybrid_mixer(h, hc, w_in, ret_decay, gla_a_up, gla_a_bias, ret_gn, gla_gn,
                 w_up_ret, w_up_gla, w_out, cos, sin):
    f32 = jnp.float32
    bsz, length, _ = h.shape
    lc = hc.shape[1]
    p = h @ w_in
    pc = hc @ w_in[:, :N_CTX_COLS]
    rk, rv, gk, gv, ga, rq, rg, gq, gg, gate_a, gate_b = split_cols(p, SPLIT_SIZES)
    rkc, rvc, gkc, gvc, gac = split_cols(pc, SPLIT_SIZES[:5])

    log_gamma = -jnp.exp(ret_decay.astype(f32))

    def ret_la(n, d):
        return jnp.broadcast_to(log_gamma[d][None, :, None, None], (bsz, RET_HEADS, n, 1))

    q_r = apply_rope(to_heads(rq, RET_HEADS), cos, sin) * (RET_DK ** -0.5)
    k_r = apply_rope(to_heads(rk, RET_HEADS), cos, sin)
    o_r = bidirectional_recurrence(q_r, k_r, to_heads(rv, RET_HEADS), ret_la(length, 0), ret_la(length, 1),
                                   to_heads(rkc, RET_HEADS), to_heads(rvc, RET_HEADS),
                                   ret_la(lc, 0), ret_la(lc, 1))
    o_r = merge_heads(head_layer_norm(o_r)) * ret_gn * jax.nn.silu(rg.astype(f32))
    y_ret = o_r @ w_up_ret

    def gla_la(a_low, d):
        z = a_low[..., d * GLA_LOW_RANK:(d + 1) * GLA_LOW_RANK] @ gla_a_up[d] + gla_a_bias[d]
        return to_heads(jax.nn.log_sigmoid(z.astype(f32)) / GLA_GATE_NORM, GLA_HEADS)

    q_g = to_heads(gq, GLA_HEADS) * (GLA_DK ** -0.5)
    o_g = bidirectional_recurrence(q_g, to_heads(gk, GLA_HEADS), to_heads(gv, GLA_HEADS),
                                   gla_la(ga, 0), gla_la(ga, 1),
                                   to_heads(gkc, GLA_HEADS), to_heads(gvc, GLA_HEADS),
                                   gla_la(gac, 0), gla_la(gac, 1))
    o_g = merge_heads(head_rms_norm(o_g)) * gla_gn * jax.nn.silu(gg.astype(f32))
    y_gla = o_g @ w_up_gla

    merged = jax.nn.sigmoid(gate_a.astype(f32)) * y_ret + jax.nn.sigmoid(gate_b.astype(f32)) * y_gla
    return merged @ w_out


def setup_inputs(seed: int = 0) -> dict:
    key = jax.random.key(seed)
    ks = jax.random.split(key, 24)
    f32 = jnp.float32

    def nrm(k, shape, scale):
        return jax.random.normal(k, shape, f32) * scale

    ret_init = np.log(-np.log(1.0 - np.exp(np.linspace(np.log(1.0 / 32), np.log(1.0 / 512), RET_HEADS))))
    ret_decay = jnp.asarray(ret_init, f32)[None, None, :] + nrm(ks[10], (DEPTH, 2, RET_HEADS), 0.05)
    return {
        'x': nrm(ks[0], (BATCH, SEQ, D_MODEL), 1.0),
        'c': nrm(ks[1], (BATCH, D_MODEL), 1.0),
        'ctx': nrm(ks[2], (BATCH, CTX_LEN, D_MODEL), 1.0),
        'c_ctx': nrm(ks[3], (D_MODEL,), 1.0),
        'w_mod': nrm(ks[4], (DEPTH, D_MODEL, 6 * D_MODEL), 0.5 * D_MODEL ** -0.5),
        'b_mod': nrm(ks[5], (DEPTH, 6 * D_MODEL), 0.01),
        'norm_mix_pre': 1.0 + nrm(ks[6], (DEPTH, D_MODEL), 0.05),
        'norm_mix_post': 1.0 + nrm(ks[7], (DEPTH, D_MODEL), 0.05),
        'norm_ffn_pre': 1.0 + nrm(ks[8], (DEPTH, D_MODEL), 0.05),
        'norm_ffn_post': 1.0 + nrm(ks[9], (DEPTH, D_MODEL), 0.05),
        'w_in': nrm(ks[11], (DEPTH, D_MODEL, D_IN), D_MODEL ** -0.5),
        'ret_decay': ret_decay,
        'gla_a_up': nrm(ks[12], (DEPTH, 2, GLA_LOW_RANK, GLA_KW), GLA_LOW_RANK ** -0.5),
        'gla_a_bias': nrm(ks[13], (DEPTH, 2, GLA_KW), 0.01),
        'ret_gn': 1.0 + nrm(ks[14], (DEPTH, RET_VW), 0.05),
        'gla_gn': 1.0 + nrm(ks[15], (DEPTH, GLA_VW), 0.05),
        'w_up_ret': nrm(ks[16], (DEPTH, RET_VW, D_MODEL), RET_VW ** -0.5),
        'w_up_gla': nrm(ks[17], (DEPTH, GLA_VW, D_MODEL), GLA_VW ** -0.5),
        'w_out': nrm(ks[18], (DEPTH, D_MODEL, D_MODEL), D_MODEL ** -0.5),
        'ffn_w_gate': nrm(ks[19], (DEPTH, D_MODEL, D_FF), D_MODEL ** -0.5),
        'ffn_w_up': nrm(ks[20], (DEPTH, D_MODEL, D_FF), D_MODEL ** -0.5),
        'ffn_w_down': nrm(ks[21], (DEPTH, D_FF, D_MODEL), D_FF ** -0.5),
    }


def reference(x, c, ctx, c_ctx, w_mod, b_mod, norm_mix_pre, norm_mix_post, norm_ffn_pre, norm_ffn_post,
              w_in, ret_decay, gla_a_up, gla_a_bias, ret_gn, gla_gn, w_up_ret, w_up_gla, w_out,
              ffn_w_gate, ffn_w_up, ffn_w_down):
    f32 = jnp.float32
    out_dtype = x.dtype
    length = x.shape[1]
    rows = length // GRID_W
    pos_r = jnp.repeat(jnp.arange(rows, dtype=f32), GRID_W)
    pos_c = jnp.tile(jnp.arange(GRID_W, dtype=f32), rows)
    cos, sin = axial_rope_tables(pos_r, pos_c, RET_DK)

    h_state = x.astype(f32)
    for i in range(DEPTH):
        mod = jax.nn.silu(c.astype(f32)) @ w_mod[i] + b_mod[i]
        sh1, sc1, g1, sh2, sc2, g2 = jnp.split(mod, 6, axis=-1)
        mod_c = jax.nn.silu(c_ctx.astype(f32)) @ w_mod[i][:, :2 * D_MODEL] + b_mod[i][:2 * D_MODEL]
        shc, scc = jnp.split(mod_c, 2)

        h = rms_norm(h_state, norm_mix_pre[i]) * (1.0 + sc1[:, None]) + sh1[:, None]
        hc = rms_norm(ctx, norm_mix_pre[i]) * (1.0 + scc) + shc
        y = hybrid_mixer(h, hc, w_in[i], ret_decay[i], gla_a_up[i], gla_a_bias[i], ret_gn[i], gla_gn[i],
                         w_up_ret[i], w_up_gla[i], w_out[i], cos, sin)
        h_state = h_state + g1[:, None] * rms_norm(y, norm_mix_post[i])

        h = rms_norm(h_state, norm_ffn_pre[i]) * (1.0 + sc2[:, None]) + sh2[:, None]
        y = (jax.nn.silu(h @ ffn_w_gate[i]) * (h @ ffn_w_up[i])) @ ffn_w_down[i]
        h_state = h_state + g2[:, None] * rms_norm(y, norm_ffn_post[i])
    return h_state.astype(out_dtype)
```

```python
import functools
import math

import jax
import jax.numpy as jnp
from jax import lax
from jax.experimental import pallas as pl
from jax.experimental.pallas import tpu as pltpu

F32 = jnp.float32
BF16 = jnp.bfloat16

D_MODEL = 2048
GRID_W = 64
RET_HEADS = 4
RET_DK = 256
RET_DV = 256
GLA_HEADS = 4
GLA_DK = 128
GLA_DV = 256
GLA_LOW_RANK = 16
GLA_GATE_NORM = 16.0
ROPE_BASE = 10000.0
EPS = 1e-6

RET_QK = RET_HEADS * RET_DK
RET_VW = RET_HEADS * RET_DV
GLA_KW = GLA_HEADS * GLA_DK
GLA_VW = GLA_HEADS * GLA_DV
GA_W = 2 * GLA_LOW_RANK
LANE = 128

_SRC = {}
_off = 0
for _name, _w in (("rk", RET_QK), ("rv", RET_VW), ("gk", GLA_KW), ("gv", GLA_VW), ("ga", GA_W),
                  ("rq", RET_QK), ("rg", RET_VW), ("gq", GLA_KW), ("gg", GLA_VW),
                  ("gate_a", D_MODEL), ("gate_b", D_MODEL)):
    _SRC[_name] = (_off, _w)
    _off += _w

_ORDER = ("gate_a", "gate_b", "rq", "rg", "gg", "gq", "gk", "gv", "rk", "rv")
_DST = {}
_off = 0
for _name in _ORDER:
    _DST[_name] = _off
    _off += _SRC[_name][1]
N_MAIN = _off
CTX_COL0 = _DST["gk"]
N_CTX = N_MAIN - CTX_COL0
_CTX = {n: _DST[n] - CTX_COL0 for n in ("gk", "gv", "rk", "rv")}

VMEM_LIMIT = 56 * 1024 * 1024


def _dot(a, b):
    return jnp.dot(a, b, preferred_element_type=F32)


def _dot_nt(a, b):
    return lax.dot_general(a, b, (((1,), (1,)), ((), ())), preferred_element_type=F32)


def _dot_tn(a, b):
    return lax.dot_general(a, b, (((0,), (0,)), ((), ())), preferred_element_type=F32)


def _sigmoid(x):
    return 1.0 / (1.0 + jnp.exp(-x))


def _rms(x):
    return x * lax.rsqrt(jnp.mean(x * x, axis=-1, keepdims=True) + EPS)


def _mod_kernel(c_ref, w_ref, b_ref, o_ref):
    cf = c_ref[...]
    s = cf * _sigmoid(cf)
    o_ref[...] = jnp.dot(s, w_ref[...], preferred_element_type=F32,
                         precision=lax.Precision.HIGHEST) + b_ref[...]


def _modulation(c_rows, w_mod, b_mod):
    rows, d = c_rows.shape
    n = w_mod.shape[1]
    tn = 1024
    return pl.pallas_call(
        _mod_kernel,
        out_shape=jax.ShapeDtypeStruct((rows, n), F32),
        grid=(n // tn,),
        in_specs=[pl.BlockSpec((rows, d), lambda j: (0, 0)),
                  pl.BlockSpec((d, tn), lambda j: (0, j)),
                  pl.BlockSpec((1, tn), lambda j: (0, j))],
        out_specs=pl.BlockSpec((rows, tn), lambda j: (0, j)),
        compiler_params=pltpu.CompilerParams(dimension_semantics=("arbitrary",),
                                             vmem_limit_bytes=VMEM_LIMIT),
        name="mod",
    )(c_rows, w_mod, b_mod.reshape(1, n))


def _inproj_kernel(x_ref, nw_ref, sh_ref, sc_ref, w_ref, wga_ref, cos_ref, sin_ref,
                   o_ref, ga_ref, h_scr, *, rope_tiles, tn):
    j = pl.program_id(1)

    @pl.when(j == 0)
    def _():
        h = _rms(x_ref[...]) * nw_ref[...] * (1.0 + sc_ref[...]) + sh_ref[...]
        hb = h.astype(BF16)
        h_scr[...] = hb
        ga_ref[...] = _dot(hb, wga_ref[...])

    acc = _dot(h_scr[...], w_ref[...])

    if not rope_tiles:
        o_ref[...] = acc.astype(o_ref.dtype)
        return

    is_rope = functools.reduce(jnp.logical_or, [j == t for t in rope_tiles])

    @pl.when(is_rope)
    def _():
        cos = cos_ref[...]
        sin = sin_ref[...]
        half = RET_DK // 2
        for hd in range(tn // RET_DK):
            lo = hd * RET_DK
            t1 = acc[:, lo:lo + half]
            t2 = acc[:, lo + half:lo + RET_DK]
            o_ref[:, lo:lo + half] = (t1 * cos - t2 * sin).astype(o_ref.dtype)
            o_ref[:, lo + half:lo + RET_DK] = (t1 * sin + t2 * cos).astype(o_ref.dtype)

    @pl.when(jnp.logical_not(is_rope))
    def _():
        o_ref[...] = acc.astype(o_ref.dtype)


def _inproj(x2d, norm_w, mod3, mod_row_of_tile, w_main, w_ga, cos, sin, *, tm, tn, col0, ncols,
            rope_tiles, seq_tiles):
    m, d = x2d.shape
    jt0 = col0 // tn
    kern = functools.partial(_inproj_kernel, rope_tiles=rope_tiles, tn=tn)
    return pl.pallas_call(
        kern,
        out_shape=(jax.ShapeDtypeStruct((m, ncols), BF16),
                   jax.ShapeDtypeStruct((m, LANE), F32)),
        grid=(m // tm, ncols // tn),
        in_specs=[
            pl.BlockSpec((tm, d), lambda i, j: (i, 0)),
            pl.BlockSpec((1, d), lambda i, j: (0, 0)),
            pl.BlockSpec((None, 1, d), lambda i, j: (mod_row_of_tile(i), 0, 0)),
            pl.BlockSpec((None, 1, d), lambda i, j: (mod_row_of_tile(i), 0, 1)),
            pl.BlockSpec((d, tn), lambda i, j: (0, j + jt0)),
            pl.BlockSpec((d, LANE), lambda i, j: (0, 0)),
            pl.BlockSpec((tm, RET_DK // 2), lambda i, j: (i % seq_tiles, 0)),
            pl.BlockSpec((tm, RET_DK // 2), lambda i, j: (i % seq_tiles, 0)),
        ],
        out_specs=(pl.BlockSpec((tm, tn), lambda i, j: (i, j)),
                   pl.BlockSpec((tm, LANE), lambda i, j: (i, 0))),
        scratch_shapes=[pltpu.VMEM((tm, d), BF16)],
        compiler_params=pltpu.CompilerParams(dimension_semantics=("arbitrary", "arbitrary"),
                                             vmem_limit_bytes=VMEM_LIMIT),
        name="inproj",
    )(x2d, norm_w, mod3, mod3, w_main, w_ga, cos, sin)


RET_CHUNK = 256


def _ret_kernel(q_ref, k_ref, v_ref, g_ref, kc_ref, vc_ref, rd_ref, gn_ref, o_ref,
                of_scr, s_scr, *, n_chunks):
    c_len = RET_CHUNK
    ii = lax.broadcasted_iota(jnp.int32, (c_len, c_len), 0)
    jj = lax.broadcasted_iota(jnp.int32, (c_len, c_len), 1)
    rowi = lax.broadcasted_iota(jnp.int32, (c_len, RET_DK), 0).astype(F32)
    scale = RET_DK ** -0.5

    for direction in (0, 1):
        lg = -jnp.exp(rd_ref[direction][0:1, :])
        if direction == 0:
            dmat = jnp.where(jj <= ii, jnp.exp((ii - jj).astype(F32) * lg), 0.0)
            eq = jnp.exp((rowi + 1.0) * lg)
            wk = jnp.exp((c_len - 1.0 - rowi) * lg)
        else:
            dmat = jnp.where(jj > ii, jnp.exp((jj - ii).astype(F32) * lg), 0.0)
            eq = jnp.exp((c_len - rowi) * lg)
            wk = jnp.exp(rowi * lg)
        sdec = jnp.exp(float(c_len) * lg)

        kcw = (kc_ref[...].astype(F32) * wk).astype(BF16)
        s_scr[...] = _dot_tn(kcw, vc_ref[...])

        def body(t, carry, direction=direction, dmat=dmat, eq=eq, wk=wk, sdec=sdec):
            c = t if direction == 0 else n_chunks - 1 - t
            r0 = pl.multiple_of(c * c_len, c_len)
            q = q_ref[pl.ds(r0, c_len), :]
            k = k_ref[pl.ds(r0, c_len), :]
            v = v_ref[pl.ds(r0, c_len), :]
            s_state = s_scr[...]
            sc = (_dot_nt(q, k) * dmat).astype(BF16)
            o = _dot(sc, v) + eq * _dot(q, s_state.astype(BF16))
            kw = (k.astype(F32) * wk).astype(BF16)
            s_scr[...] = sdec * s_state + _dot_tn(kw, v)
            if direction == 0:
                of_scr[pl.ds(r0, c_len), :] = o
            else:
                tot = (of_scr[pl.ds(r0, c_len), :] + o) * scale
                mu = jnp.mean(tot, axis=-1, keepdims=True)
                cen = tot - mu
                var = jnp.mean(cen * cen, axis=-1, keepdims=True)
                nrm = cen * lax.rsqrt(var + EPS)
                g = g_ref[pl.ds(r0, c_len), :].astype(F32)
                o_ref[pl.ds(r0, c_len), :] = (nrm * gn_ref[...] * (g * _sigmoid(g))).astype(o_ref.dtype)
            return carry

        lax.fori_loop(0, n_chunks, body, 0)


def _retention(p, pc, rd_b, ret_gn, *, batch, seq, ctx_len):
    assert ctx_len == RET_CHUNK
    n_chunks = seq // RET_CHUNK
    w = RET_DK
    kern = functools.partial(_ret_kernel, n_chunks=n_chunks)

    def col(name):
        return _DST[name] // w

    def ccol(name):
        return _CTX[name] // w

    return pl.pallas_call(
        kern,
        out_shape=jax.ShapeDtypeStruct((batch * seq, RET_VW), BF16),
        grid=(batch, RET_HEADS),
        in_specs=[
            pl.BlockSpec((seq, w), lambda b, h: (b, col("rq") + h)),
            pl.BlockSpec((seq, w), lambda b, h: (b, col("rk") + h)),
            pl.BlockSpec((seq, w), lambda b, h: (b, col("rv") + h)),
            pl.BlockSpec((seq, w), lambda b, h: (b, col("rg") + h)),
            pl.BlockSpec((ctx_len, w), lambda b, h: (b, ccol("rk") + h)),
            pl.BlockSpec((ctx_len, w), lambda b, h: (b, ccol("rv") + h)),
            pl.BlockSpec((2, None, 8, w), lambda b, h: (0, h, 0, 0)),
            pl.BlockSpec((1, w), lambda b, h: (0, h)),
        ],
        out_specs=pl.BlockSpec((seq, w), lambda b, h: (b, h)),
        scratch_shapes=[pltpu.VMEM((seq, RET_DV), F32), pltpu.VMEM((RET_DK, RET_DV), F32)],
        compiler_params=pltpu.CompilerParams(dimension_semantics=("arbitrary", "arbitrary"),
                                             vmem_limit_bytes=VMEM_LIMIT),
        name="retention",
    )(p, p, p, p, pc, pc, rd_b, ret_gn)


GLA_CHUNK = 64


def _gla_kernel(q_ref, k_ref, v_ref, g_ref, ga_ref, kc_ref, vc_ref, gac_ref, up_ref, bias_ref, gn_ref,
                o_ref, of_scr, st_scr, *, n_chunks, n_ctx_chunks):
    c_len = GLA_CHUNK
    ii = lax.broadcasted_iota(jnp.int32, (c_len, c_len), 0)
    jj = lax.broadcasted_iota(jnp.int32, (c_len, c_len), 1)
    scale = GLA_DK ** -0.5

    for direction in (0, 1):
        if direction == 0:
            tri = (jj <= ii)
            vis = (jj <= ii)
            last = c_len - 1
        else:
            tri = (jj >= ii)
            vis = (jj > ii)
            last = 0
        tmat = jnp.where(tri, 1.0, 0.0).astype(BF16)
        up = up_ref[direction]
        bias = bias_ref[direction]

        def gates(ga_c, tmat=tmat, up=up, bias=bias, last=last):
            z = jnp.dot(ga_c, up, preferred_element_type=F32, precision=lax.Precision.HIGHEST) + bias
            a = (jnp.minimum(z, 0.0) - jnp.log(1.0 + jnp.exp(-jnp.abs(z)))) * (1.0 / GLA_GATE_NORM)
            a_hi = a.astype(BF16)
            a_lo = (a - a_hi.astype(F32)).astype(BF16)
            b = _dot(tmat, a_hi) + _dot(tmat, a_lo)
            return b, b[last:last + 1, :]

        def update_state(k, v, b, b_last):
            kh = (k.astype(F32) * jnp.exp(b_last - b)).astype(BF16)
            return jnp.exp(b_last) * st_scr[...] + _dot_tn(v, kh)

        st_scr[...] = jnp.zeros_like(st_scr)

        def ctx_body(t, carry, direction=direction):
            c = t if direction == 0 else n_ctx_chunks - 1 - t
            r0 = pl.multiple_of(c * c_len, c_len)
            b, b_last = gates(gac_ref[pl.ds(r0, c_len), :])
            st_scr[...] = update_state(kc_ref[pl.ds(r0, c_len), :], vc_ref[pl.ds(r0, c_len), :], b, b_last)
            return carry

        lax.fori_loop(0, n_ctx_chunks, ctx_body, 0)

        def body(t, carry, direction=direction, vis=vis):
            c = t if direction == 0 else n_chunks - 1 - t
            r0 = pl.multiple_of(c * c_len, c_len)
            b, b_last = gates(ga_ref[pl.ds(r0, c_len), :])
            q = q_ref[pl.ds(r0, c_len), :].astype(F32)
            k = k_ref[pl.ds(r0, c_len), :]
            v = v_ref[pl.ds(r0, c_len), :]
            qt = (q * jnp.exp(b)).astype(BF16)
            kt = (k.astype(F32) * jnp.exp(-b)).astype(BF16)
            sc = jnp.where(vis, _dot_nt(qt, kt), 0.0).astype(BF16)
            o = _dot(sc, v) + _dot_nt(qt, st_scr[...].astype(BF16))
            st_scr[...] = update_state(k, v, b, b_last)
            if direction == 0:
                of_scr[pl.ds(r0, c_len), :] = o
            else:
                tot = (of_scr[pl.ds(r0, c_len), :] + o) * scale
                nrm = _rms(tot)
                g = g_ref[pl.ds(r0, c_len), :].astype(F32)
                o_ref[pl.ds(r0, c_len), :] = (nrm * gn_ref[...] * (g * _sigmoid(g))).astype(o_ref.dtype)
            return carry

        lax.fori_loop(0, n_chunks, body, 0)


def _gla(p, ga, pc, gac, up_pad, bias3, gla_gn, *, batch, seq, ctx_len):
    n_chunks = seq // GLA_CHUNK
    n_ctx_chunks = ctx_len // GLA_CHUNK
    kern = functools.partial(_gla_kernel, n_chunks=n_chunks, n_ctx_chunks=n_ctx_chunks)
    dk, dv = GLA_DK, GLA_DV
    return pl.pallas_call(
        kern,
        out_shape=jax.ShapeDtypeStruct((batch * seq, GLA_VW), BF16),
        grid=(batch, GLA_HEADS),
        in_specs=[
            pl.BlockSpec((seq, dk), lambda b, h: (b, _DST["gq"] // dk + h)),
            pl.BlockSpec((seq, dk), lambda b, h: (b, _DST["gk"] // dk + h)),
            pl.BlockSpec((seq, dv), lambda b, h: (b, _DST["gv"] // dv + h)),
            pl.BlockSpec((seq, dv), lambda b, h: (b, _DST["gg"] // dv + h)),
            pl.BlockSpec((seq, LANE), lambda b, h: (b, 0)),
            pl.BlockSpec((ctx_len, dk), lambda b, h: (b, _CTX["gk"] // dk + h)),
            pl.BlockSpec((ctx_len, dv), lambda b, h: (b, _CTX["gv"] // dv + h)),
            pl.BlockSpec((ctx_len, LANE), lambda b, h: (b, 0)),
            pl.BlockSpec((2, LANE, dk), lambda b, h: (0, 0, h)),
            pl.BlockSpec((2, 1, dk), lambda b, h: (0, 0, h)),
            pl.BlockSpec((1, dv), lambda b, h: (0, h)),
        ],
        out_specs=pl.BlockSpec((seq, dv), lambda b, h: (b, h)),
        scratch_shapes=[pltpu.VMEM((seq, dv), F32), pltpu.VMEM((dv, dk), F32)],
        compiler_params=pltpu.CompilerParams(dimension_semantics=("arbitrary", "arbitrary"),
                                             vmem_limit_bytes=VMEM_LIMIT),
        name="gla",
    )(p, p, p, p, ga, pc, pc, gac, up_pad, bias3, gla_gn)


def _post_kernel(or_ref, og_ref, ga_ref, gb_ref, x_ref, wur_ref, wug_ref, wo_ref,
                 g1_ref, sh2_ref, sc2_ref, npost_ref, npre_ref, hs_ref, h2_ref):
    y_ret = _dot(or_ref[...], wur_ref[...])
    y_gla = _dot(og_ref[...], wug_ref[...])
    merged = (_sigmoid(ga_ref[...].astype(F32)) * y_ret
              + _sigmoid(gb_ref[...].astype(F32)) * y_gla)
    y = _dot(merged.astype(BF16), wo_ref[...])
    hs = x_ref[...] + g1_ref[...] * (_rms(y) * npost_ref[...])
    hs_ref[...] = hs
    h2 = _rms(hs) * npre_ref[...] * (1.0 + sc2_ref[...]) + sh2_ref[...]
    h2_ref[...] = h2.astype(h2_ref.dtype)


def _post(o_r, o_g, p, x2d, w_up_ret, w_up_gla, w_out, mod3, npost, npre, *, tm, seq):
    m, d = x2d.shape
    tiles_per_seq = seq // tm
    const = dict(pipeline_mode=pl.Buffered(1))

    def modspec(chunk):
        return pl.BlockSpec((None, 1, d), lambda i: (i // tiles_per_seq, 0, chunk))

    return pl.pallas_call(
        _post_kernel,
        out_shape=(jax.ShapeDtypeStruct((m, d), F32), jax.ShapeDtypeStruct((m, d), BF16)),
        grid=(m // tm,),
        in_specs=[
            pl.BlockSpec((tm, RET_VW), lambda i: (i, 0)),
            pl.BlockSpec((tm, GLA_VW), lambda i: (i, 0)),
            pl.BlockSpec((tm, d), lambda i: (i, _DST["gate_a"] // d)),
            pl.BlockSpec((tm, d), lambda i: (i, _DST["gate_b"] // d)),
            pl.BlockSpec((tm, d), lambda i: (i, 0)),
            pl.BlockSpec((RET_VW, d), lambda i: (0, 0), **const),
            pl.BlockSpec((GLA_VW, d), lambda i: (0, 0), **const),
            pl.BlockSpec((d, d), lambda i: (0, 0), **const),
            modspec(2), modspec(3), modspec(4),
            pl.BlockSpec((1, d), lambda i: (0, 0)),
            pl.BlockSpec((1, d), lambda i: (0, 0)),
        ],
        out_specs=(pl.BlockSpec((tm, d), lambda i: (i, 0)),
                   pl.BlockSpec((tm, d), lambda i: (i, 0))),
        compiler_params=pltpu.CompilerParams(dimension_semantics=("arbitrary",),
                                             vmem_limit_bytes=VMEM_LIMIT),
        name="post",
    )(o_r, o_g, p, p, x2d, w_up_ret, w_up_gla, w_out, mod3, mod3, mod3, npost, npre)


def _ffn_kernel(h_ref, wg_ref, wu_ref, wd_ref, hs_ref, g2_ref, nw_ref, o_ref, acc_scr, *, n_f):
    f = pl.program_id(1)
    h = h_ref[...]
    g = _dot(h, wg_ref[...])
    u = _dot(h, wu_ref[...])
    a = (g * _sigmoid(g) * u).astype(BF16)
    part = _dot(a, wd_ref[...])

    @pl.when(f == 0)
    def _():
        acc_scr[...] = part

    @pl.when(f > 0)
    def _():
        acc_scr[...] += part

    @pl.when(f == n_f - 1)
    def _():
        o_ref[...] = hs_ref[...] + g2_ref[...] * (_rms(acc_scr[...]) * nw_ref[...])


def _ffn(h2, hs, wg, wu, wd, mod3, npost, *, tm, tf, seq):
    m, d = hs.shape
    d_ff = wg.shape[1]
    n_f = d_ff // tf
    tiles_per_seq = seq // tm
    kern = functools.partial(_ffn_kernel, n_f=n_f)
    return pl.pallas_call(
        kern,
        out_shape=jax.ShapeDtypeStruct((m, d), F32),
        grid=(m // tm, n_f),
        in_specs=[
            pl.BlockSpec((tm, d), lambda i, f: (i, 0)),
            pl.BlockSpec((d, tf), lambda i, f: (0, f)),
            pl.BlockSpec((d, tf), lambda i, f: (0, f)),
            pl.BlockSpec((tf, d), lambda i, f: (f, 0)),
            pl.BlockSpec((tm, d), lambda i, f: (i, 0)),
            pl.BlockSpec((None, 1, d), lambda i, f: (i // tiles_per_seq, 0, 5)),
            pl.BlockSpec((1, d), lambda i, f: (0, 0)),
        ],
        out_specs=pl.BlockSpec((tm, d), lambda i, f: (i, 0)),
        scratch_shapes=[pltpu.VMEM((tm, d), F32)],
        compiler_params=pltpu.CompilerParams(dimension_semantics=("arbitrary", "arbitrary"),
                                             vmem_limit_bytes=VMEM_LIMIT),
        name="ffn",
    )(h2, wg, wu, wd, hs, mod3, npost)


def _rope_tables(seq):
    rows = seq // GRID_W
    pos_r = jnp.repeat(jnp.arange(rows, dtype=F32), GRID_W)
    pos_c = jnp.tile(jnp.arange(GRID_W, dtype=F32), rows)
    n_f = RET_DK // 4
    inv = ROPE_BASE ** (-jnp.arange(n_f, dtype=F32) / n_f)
    ang = jnp.concatenate([pos_r[:, None] * inv, pos_c[:, None] * inv], axis=-1)
    return jnp.cos(ang), jnp.sin(ang)


def _layer(h_state, c_rows, ctx2d, w_mod, b_mod, norm_mix_pre, norm_mix_post, norm_ffn_pre, norm_ffn_post,
           w_in, ret_decay, gla_a_up, gla_a_bias, ret_gn, gla_gn, w_up_ret, w_up_gla, w_out,
           ffn_w_gate, ffn_w_up, ffn_w_down, cos, sin, *, batch, seq, ctx_len):
    d = D_MODEL
    w_main = jnp.concatenate(
        [w_in[:, _SRC[n][0]:_SRC[n][0] + _SRC[n][1]] for n in _ORDER], axis=1).astype(BF16)
    ga0 = _SRC["ga"][0]
    w_ga = jnp.pad(w_in[:, ga0:ga0 + GA_W], ((0, 0), (0, LANE - GA_W))).astype(BF16)
    up_pad = jnp.zeros((2, LANE, GLA_KW), F32)
    for dr in range(2):
        up_pad = up_pad.at[dr, dr * GLA_LOW_RANK:(dr + 1) * GLA_LOW_RANK, :].set(gla_a_up[dr])
    bias3 = gla_a_bias.reshape(2, 1, GLA_KW)
    rd_b = jnp.broadcast_to(ret_decay.reshape(2, RET_HEADS, 1, 1), (2, RET_HEADS, 8, RET_DK))

    mod = _modulation(c_rows, w_mod, b_mod)
    mod3 = mod.reshape(mod.shape[0], 1, 6 * d)
    nw_pre = norm_mix_pre.reshape(1, d)

    tm = 1024
    p, ga = _inproj(h_state, nw_pre, mod3, lambda i: i // (seq // tm), w_main, w_ga, cos, sin,
                    tm=tm, tn=1024, col0=0, ncols=N_MAIN,
                    rope_tiles=(_DST["rq"] // 1024, _DST["rk"] // 1024), seq_tiles=seq // tm)
    tmc = batch * ctx_len
    pc, gac = _inproj(ctx2d, nw_pre, mod3, lambda i: batch, w_main, w_ga, cos, sin,
                      tm=tmc, tn=512, col0=CTX_COL0, ncols=N_CTX, rope_tiles=(), seq_tiles=1)

    o_r = _retention(p, pc, rd_b, ret_gn.reshape(1, RET_VW), batch=batch, seq=seq, ctx_len=ctx_len)
    o_g = _gla(p, ga, pc, gac, up_pad, bias3, gla_gn.reshape(1, GLA_VW), batch=batch, seq=seq, ctx_len=ctx_len)

    hs, h2 = _post(o_r, o_g, p, h_state, w_up_ret.astype(BF16), w_up_gla.astype(BF16), w_out.astype(BF16),
                   mod3, norm_mix_post.reshape(1, d), norm_ffn_pre.reshape(1, d), tm=256, seq=seq)
    return _ffn(h2, hs, ffn_w_gate.astype(BF16), ffn_w_up.astype(BF16), ffn_w_down.astype(BF16),
                mod3, norm_ffn_post.reshape(1, d), tm=512, tf=512, seq=seq)


def kernel(x, c, ctx, c_ctx, w_mod, b_mod, norm_mix_pre, norm_mix_post, norm_ffn_pre, norm_ffn_post,
           w_in, ret_decay, gla_a_up, gla_a_bias, ret_gn, gla_gn, w_up_ret, w_up_gla, w_out,
           ffn_w_gate, ffn_w_up, ffn_w_down):
    batch, seq, d = x.shape
    ctx_len = ctx.shape[1]
    depth = w_mod.shape[0]
    cos, sin = _rope_tables(seq)
    c_rows = jnp.zeros((8, d), F32).at[:batch].set(c.astype(F32)).at[batch].set(c_ctx.astype(F32))
    ctx2d = ctx.astype(F32).reshape(batch * ctx_len, d)
    h_state = x.astype(F32).reshape(batch * seq, d)
    for i in range(depth):
        h_state = _layer(h_state, c_rows, ctx2d, w_mod[i], b_mod[i], norm_mix_pre[i], norm_mix_post[i],
                         norm_ffn_pre[i], norm_ffn_post[i], w_in[i], ret_decay[i], gla_a_up[i], gla_a_bias[i],
                         ret_gn[i], gla_gn[i], w_up_ret[i], w_up_gla[i], w_out[i],
                         ffn_w_gate[i], ffn_w_up[i], ffn_w_down[i], cos, sin,
                         batch=batch, seq=seq, ctx_len=ctx_len)
    return h_state.reshape(batch, seq, d).astype(x.dtype)
```

```python
import functools

import jax
import jax.numpy as jnp
from jax import lax
from jax.experimental import pallas as pl
from jax.experimental.pallas import tpu as pltpu

F32 = jnp.float32
BF16 = jnp.bfloat16

D_MODEL = 2048
GRID_W = 64
RET_HEADS = 4
RET_DK = 256
RET_DV = 256
GLA_HEADS = 4
GLA_DK = 128
GLA_DV = 256
GLA_LOW_RANK = 16
GLA_GATE_NORM = 16.0
ROPE_BASE = 10000.0
EPS = 1e-6

RET_QK = RET_HEADS * RET_DK
RET_VW = RET_HEADS * RET_DV
GLA_KW = GLA_HEADS * GLA_DK
GLA_VW = GLA_HEADS * GLA_DV
GA_W = 2 * GLA_LOW_RANK
LANE = 128

_SRC = {}
_off = 0
for _name, _w in (("rk", RET_QK), ("rv", RET_VW), ("gk", GLA_KW), ("gv", GLA_VW), ("ga", GA_W),
                  ("rq", RET_QK), ("rg", RET_VW), ("gq", GLA_KW), ("gg", GLA_VW),
                  ("gate_a", D_MODEL), ("gate_b", D_MODEL)):
    _SRC[_name] = (_off, _w)
    _off += _w

_DST = {}
_off = 0
for _name in _SRC:
    if _name != "ga":
        _DST[_name] = _off
        _off += _SRC[_name][1]
N_MAIN = _off
N_CTX = _DST["rq"]

VMEM_LIMIT = 56 * 1024 * 1024


def _dot(a, b):
    return jnp.dot(a, b, preferred_element_type=F32)


def _dot_nt(a, b):
    return lax.dot_general(a, b, (((1,), (1,)), ((), ())), preferred_element_type=F32)


def _dot_tn(a, b):
    return lax.dot_general(a, b, (((0,), (0,)), ((), ())), preferred_element_type=F32)


def _sigmoid(x):
    return 1.0 / (1.0 + jnp.exp(-x))


def _rms(x):
    return x * lax.rsqrt(jnp.mean(x * x, axis=-1, keepdims=True) + EPS)


def _split_bf16(x):
    hi = x.astype(BF16)
    lo = (x - hi.astype(F32)).astype(BF16)
    return hi, lo


def _mod_kernel(c_ref, w_ref, b_ref, o_ref):
    cf = c_ref[...]
    s = cf * _sigmoid(cf)
    o_ref[...] = jnp.dot(s, w_ref[...], preferred_element_type=F32,
                         precision=lax.Precision.HIGHEST) + b_ref[...]


def _modulation(c_rows, w_mod, b_mod):
    rows, d = c_rows.shape
    n = w_mod.shape[1]
    tn = 1024
    return pl.pallas_call(
        _mod_kernel,
        out_shape=jax.ShapeDtypeStruct((rows, n), F32),
        grid=(n // tn,),
        in_specs=[pl.BlockSpec((rows, d), lambda j: (0, 0)),
                  pl.BlockSpec((d, tn), lambda j: (0, j)),
                  pl.BlockSpec((1, tn), lambda j: (0, j))],
        out_specs=pl.BlockSpec((rows, tn), lambda j: (0, j)),
        compiler_params=pltpu.CompilerParams(dimension_semantics=("arbitrary",),
                                             vmem_limit_bytes=VMEM_LIMIT),
        name="mod",
    )(c_rows, w_mod, b_mod.reshape(1, n))


def _inproj_kernel(x_ref, nw_ref, sh_ref, sc_ref, w_ref, wga_ref, cos_ref, sin_ref,
                   o_ref, ga_ref, h_scr, *, rope_map, tn):
    j = pl.program_id(1)

    @pl.when(j == 0)
    def _():
        h = _rms(x_ref[...]) * nw_ref[...] * (1.0 + sc_ref[...]) + sh_ref[...]
        hb = h.astype(BF16)
        h_scr[...] = hb
        g3 = _dot(hb, wga_ref[...])
        lane = lax.broadcasted_iota(jnp.int32, g3.shape, 1)
        resid = g3 - g3.astype(BF16).astype(F32)
        ga_ref[...] = jnp.where((lane >= GA_W) & (lane < 2 * GA_W), resid, g3).astype(ga_ref.dtype)

    acc = _dot(h_scr[...], w_ref[...])

    def plain(lo, hi):
        o_ref[:, lo:hi] = acc[:, lo:hi].astype(o_ref.dtype)

    if not rope_map:
        plain(0, tn)
        return

    half = RET_DK // 2
    for tile, heads in rope_map.items():
        @pl.when(j == tile)
        def _(heads=heads):
            cos = cos_ref[...]
            sin = sin_ref[...]
            lo = 0
            for hd in heads:
                if hd > lo:
                    plain(lo, hd)
                t1 = acc[:, hd:hd + half]
                t2 = acc[:, hd + half:hd + RET_DK]
                o_ref[:, hd:hd + half] = (t1 * cos - t2 * sin).astype(o_ref.dtype)
                o_ref[:, hd + half:hd + RET_DK] = (t1 * sin + t2 * cos).astype(o_ref.dtype)
                lo = hd + RET_DK
            if lo < tn:
                plain(lo, tn)

    is_rope = functools.reduce(jnp.logical_or, [j == t for t in rope_map])

    @pl.when(jnp.logical_not(is_rope))
    def _():
        plain(0, tn)


def _rope_map(tn):
    out = {}
    for name in ("rk", "rq"):
        for hd in range(RET_HEADS):
            col = _DST[name] + hd * RET_DK
            out.setdefault(col // tn, []).append(col % tn)
    return {t: tuple(sorted(v)) for t, v in out.items()}


def _inproj(x2d, norm_w, mod3, mod_row_of_tile, w_main, w_ga, cos, sin, *, tm, tn, ncols, rope, seq_tiles):
    m, d = x2d.shape
    kern = functools.partial(_inproj_kernel, rope_map=_rope_map(tn) if rope else {}, tn=tn)
    return pl.pallas_call(
        kern,
        out_shape=(jax.ShapeDtypeStruct((m, ncols), BF16),
                   jax.ShapeDtypeStruct((m, LANE), BF16)),
        grid=(m // tm, ncols // tn),
        in_specs=[
            pl.BlockSpec((tm, d), lambda i, j: (i, 0)),
            pl.BlockSpec((1, d), lambda i, j: (0, 0)),
            pl.BlockSpec((None, 1, d), lambda i, j: (mod_row_of_tile(i), 0, 0)),
            pl.BlockSpec((None, 1, d), lambda i, j: (mod_row_of_tile(i), 0, 1)),
            pl.BlockSpec((d, tn), lambda i, j: (0, j)),
            pl.BlockSpec((d, LANE), lambda i, j: (0, 0)),
            pl.BlockSpec((tm, RET_DK // 2), lambda i, j: (i % seq_tiles, 0)),
            pl.BlockSpec((tm, RET_DK // 2), lambda i, j: (i % seq_tiles, 0)),
        ],
        out_specs=(pl.BlockSpec((tm, tn), lambda i, j: (i, j)),
                   pl.BlockSpec((tm, LANE), lambda i, j: (i, 0))),
        scratch_shapes=[pltpu.VMEM((tm, d), BF16)],
        compiler_params=pltpu.CompilerParams(dimension_semantics=("arbitrary", "arbitrary"),
                                             vmem_limit_bytes=VMEM_LIMIT),
        name="inproj",
    )(x2d, norm_w, mod3, mod3, w_main, w_ga, cos, sin)


RET_CHUNK = 256


def _ret_kernel(q_ref, k_ref, v_ref, g_ref, kc_ref, vc_ref, rd_ref, gn_ref, o_ref,
                of_scr, s_scr, *, n_chunks):
    c_len = RET_CHUNK
    ii = lax.broadcasted_iota(jnp.int32, (c_len, c_len), 0)
    jj = lax.broadcasted_iota(jnp.int32, (c_len, c_len), 1)
    rowi = lax.broadcasted_iota(jnp.int32, (c_len, RET_DK), 0).astype(F32)
    scale = RET_DK ** -0.5

    for direction in (0, 1):
        lg = -jnp.exp(rd_ref[direction][0:1, :])
        if direction == 0:
            dmat = jnp.where(jj <= ii, jnp.exp((ii - jj).astype(F32) * lg), 0.0)
            eq = jnp.exp((rowi + 1.0) * lg)
            wk = jnp.exp((c_len - 1.0 - rowi) * lg)
        else:
            dmat = jnp.where(jj > ii, jnp.exp((jj - ii).astype(F32) * lg), 0.0)
            eq = jnp.exp((c_len - rowi) * lg)
            wk = jnp.exp(rowi * lg)
        sdec = jnp.exp(float(c_len) * lg)

        kcw = (kc_ref[...].astype(F32) * wk).astype(BF16)
        s_scr[...] = _dot_tn(kcw, vc_ref[...])

        def body(t, carry, direction=direction, dmat=dmat, eq=eq, wk=wk, sdec=sdec):
            c = t if direction == 0 else n_chunks - 1 - t
            r0 = pl.multiple_of(c * c_len, c_len)
            q = q_ref[pl.ds(r0, c_len), :]
            k = k_ref[pl.ds(r0, c_len), :]
            v = v_ref[pl.ds(r0, c_len), :]
            s_state = s_scr[...]
            sc = (_dot_nt(q, k) * dmat).astype(BF16)
            o = _dot(sc, v) + eq * _dot(q, s_state.astype(BF16))
            kw = (k.astype(F32) * wk).astype(BF16)
            s_scr[...] = sdec * s_state + _dot_tn(kw, v)
            if direction == 0:
                of_scr[pl.ds(r0, c_len), :] = o
            else:
                tot = (of_scr[pl.ds(r0, c_len), :] + o) * scale
                mu = jnp.mean(tot, axis=-1, keepdims=True)
                cen = tot - mu
                var = jnp.mean(cen * cen, axis=-1, keepdims=True)
                nrm = cen * lax.rsqrt(var + EPS)
                g = g_ref[pl.ds(r0, c_len), :].astype(F32)
                o_ref[pl.ds(r0, c_len), :] = (nrm * gn_ref[...] * (g * _sigmoid(g))).astype(o_ref.dtype)
            return carry

        lax.fori_loop(0, n_chunks, body, 0)


def _retention(p, pc, rd_b, ret_gn, *, batch, seq, ctx_len):
    assert ctx_len == RET_CHUNK
    n_chunks = seq // RET_CHUNK
    w = RET_DK
    kern = functools.partial(_ret_kernel, n_chunks=n_chunks)

    def col(name):
        return _DST[name] // w

    return pl.pallas_call(
        kern,
        out_shape=jax.ShapeDtypeStruct((batch * seq, RET_VW), BF16),
        grid=(batch, RET_HEADS),
        in_specs=[
            pl.BlockSpec((seq, w), lambda b, h: (b, col("rq") + h)),
            pl.BlockSpec((seq, w), lambda b, h: (b, col("rk") + h)),
            pl.BlockSpec((seq, w), lambda b, h: (b, col("rv") + h)),
            pl.BlockSpec((seq, w), lambda b, h: (b, col("rg") + h)),
            pl.BlockSpec((ctx_len, w), lambda b, h: (b, col("rk") + h)),
            pl.BlockSpec((ctx_len, w), lambda b, h: (b, col("rv") + h)),
            pl.BlockSpec((2, None, 8, w), lambda b, h: (0, h, 0, 0)),
            pl.BlockSpec((1, w), lambda b, h: (0, h)),
        ],
        out_specs=pl.BlockSpec((seq, w), lambda b, h: (b, h)),
        scratch_shapes=[pltpu.VMEM((seq, RET_DV), F32), pltpu.VMEM((RET_DK, RET_DV), F32)],
        compiler_params=pltpu.CompilerParams(dimension_semantics=("arbitrary", "arbitrary"),
                                             vmem_limit_bytes=VMEM_LIMIT),
        name="retention",
    )(p, p, p, p, pc, pc, rd_b, ret_gn)


GLA_SUB = 64
GLA_BLK = 256


def _gla_kernel(q_ref, k_ref, v_ref, g_ref, ga_ref, kc_ref, vc_ref, gac_ref, up_ref, bias_ref, gn_ref,
                o_ref, qf_scr, qb_scr, khf_scr, khb_scr, khcf_scr, khcb_scr, dec_scr, decc_scr,
                o_scr, s_scr, *, n_blk):
    blk, sub, dk = GLA_BLK, GLA_SUB, GLA_DK
    nsub = blk // sub
    shift = sub.bit_length() - 1
    scale = GLA_DK ** -0.5

    ii = lax.broadcasted_iota(jnp.int32, (blk, blk), 0)
    jj = lax.broadcasted_iota(jnp.int32, (blk, blk), 1)
    same = (ii >> shift) == (jj >> shift)
    lower = jj <= ii
    tmat = jnp.where(same & lower, 1.0, 0.0).astype(BF16)
    r8 = lax.broadcasted_iota(jnp.int32, (8, dk), 0)

    u = up_ref[...]
    u_hi, u_lo = _split_bf16(u)
    urow = lax.broadcasted_iota(jnp.int32, u.shape, 0)
    rhs = jnp.where(urow >= 2 * GA_W, u_lo, u_hi)
    bias = bias_ref[...]

    def prep(ga_blk, k_blk, q_blk, v_blk):
        z = _dot(ga_blk, rhs) + bias
        a = (jnp.minimum(z, 0.0) - jnp.log(1.0 + jnp.exp(-jnp.abs(z)))) * (1.0 / GLA_GATE_NORM)
        a_hi, a_lo = _split_bf16(a)
        pre = _dot(tmat, a_hi) + _dot(tmat, a_lo)
        lasts = [pre[i * sub + sub - 1:i * sub + sub, :] for i in range(nsub)]
        tot = jnp.concatenate([jnp.broadcast_to(l, (sub, 2 * dk)) for l in lasts], axis=0)
        b_f = pre[:, :dk]
        ex_b = pre[:, dk:] - a[:, dk:]
        b_b = tot[:, dk:] - ex_b
        kf = k_blk.astype(F32)
        khf = (kf * jnp.exp(tot[:, :dk] - b_f)).astype(BF16)
        khb = (kf * jnp.exp(ex_b)).astype(BF16)

        def tile8(half):
            rows = [jnp.broadcast_to(l[:, half * dk:(half + 1) * dk], (8, dk)) for l in lasts]
            t = rows[nsub - 1]
            for i in range(nsub - 2, -1, -1):
                t = jnp.where(r8 == i, rows[i], t)
            return t

        dec = jnp.exp(jnp.concatenate([tile8(0), tile8(1), jnp.zeros((LANE - 16, dk), F32)], axis=0))
        dec_t = dec.T
        if q_blk is None:
            return khf, khb, dec_t
        qf32 = q_blk.astype(F32)
        qf = (qf32 * jnp.exp(b_f)).astype(BF16)
        qb = (qf32 * jnp.exp(b_b)).astype(BF16)
        ktf = (kf * jnp.exp(-b_f)).astype(BF16)
        ktb = (kf * jnp.exp(-b_b)).astype(BF16)
        s = jnp.where(same, jnp.where(lower, _dot_nt(qf, ktf), _dot_nt(qb, ktb)), 0.0).astype(BF16)
        return khf, khb, dec_t, qf, qb, _dot(s, v_blk)

    khf, khb, dec_t = prep(gac_ref[...], kc_ref[...], None, None)
    khcf_scr[...] = khf
    khcb_scr[...] = khb
    decc_scr[...] = dec_t

    def prep_body(t, carry):
        r = pl.multiple_of(t * blk, blk)
        khf, khb, dec_t, qf, qb, o_intra = prep(ga_ref[pl.ds(r, blk), :], k_ref[pl.ds(r, blk), :],
                                                q_ref[pl.ds(r, blk), :], v_ref[pl.ds(r, blk), :])
        khf_scr[pl.ds(r, blk), :] = khf
        khb_scr[pl.ds(r, blk), :] = khb
        qf_scr[pl.ds(r, blk), :] = qf
        qb_scr[pl.ds(r, blk), :] = qb
        o_scr[pl.ds(r, blk), :] = o_intra
        dec_scr[t] = dec_t
        return carry

    lax.fori_loop(0, n_blk, prep_body, 0)

    def scan_blk(direction, dec_t, kh_at, v_at, q_at=None, o_rows=None):
        s = s_scr[direction]
        order = range(nsub) if direction == 0 else range(nsub - 1, -1, -1)
        for i in order:
            col = dec_t[:, direction * 8 + i:direction * 8 + i + 1]
            if q_at is not None:
                rows = o_rows(i)
                o_scr[rows, :] += _dot(q_at(i), s.astype(BF16))
            s = s * col + _dot_tn(kh_at(i), v_at(i))
        s_scr[direction] = s

    s_scr[...] = jnp.zeros(s_scr.shape, F32)
    for direction, khc in ((0, khcf_scr), (1, khcb_scr)):
        scan_blk(direction, decc_scr[...],
                 lambda i, khc=khc: khc[i * sub:(i + 1) * sub, :],
                 lambda i: vc_ref[i * sub:(i + 1) * sub, :])

    def finalize(t):
        r = pl.multiple_of(t * blk, blk)
        nrm = _rms(o_scr[pl.ds(r, blk), :] * scale)
        g = g_ref[pl.ds(r, blk), :].astype(F32)
        o_ref[pl.ds(r, blk), :] = (nrm * gn_ref[...] * (g * _sigmoid(g))).astype(o_ref.dtype)

    def scan_body(t, carry, done):
        for direction, q_scr, kh_scr in ((0, qf_scr, khf_scr), (1, qb_scr, khb_scr)):
            tb = t if direction == 0 else n_blk - 1 - t

            def rows(i, tb=tb):
                return pl.ds(pl.multiple_of(tb * blk + i * sub, sub), sub)

            scan_blk(direction, dec_scr[tb],
                     lambda i, kh_scr=kh_scr, rows=rows: kh_scr[rows(i), :],
                     lambda i, rows=rows: v_ref[rows(i), :],
                     lambda i, q_scr=q_scr, rows=rows: q_scr[rows(i), :],
                     rows)
        if done:
            finalize(t)
            finalize(n_blk - 1 - t)
        return carry

    lax.fori_loop(0, n_blk // 2, functools.partial(scan_body, done=False), 0)
    lax.fori_loop(n_blk // 2, n_blk, functools.partial(scan_body, done=True), 0)


def _gla(p, ga, pc, gac, up_stack, bias2, gla_gn, *, batch, seq, ctx_len):
    assert ctx_len == GLA_BLK and (seq // GLA_BLK) % 2 == 0
    n_blk = seq // GLA_BLK
    kern = functools.partial(_gla_kernel, n_blk=n_blk)
    dk, dv = GLA_DK, GLA_DV
    return pl.pallas_call(
        kern,
        out_shape=jax.ShapeDtypeStruct((batch * seq, GLA_VW), BF16),
        grid=(batch, GLA_HEADS),
        in_specs=[
            pl.BlockSpec((seq, dk), lambda b, h: (b, _DST["gq"] // dk + h)),
            pl.BlockSpec((seq, dk), lambda b, h: (b, _DST["gk"] // dk + h)),
            pl.BlockSpec((seq, dv), lambda b, h: (b, _DST["gv"] // dv + h)),
            pl.BlockSpec((seq, dv), lambda b, h: (b, _DST["gg"] // dv + h)),
            pl.BlockSpec((seq, LANE), lambda b, h: (b, 0)),
            pl.BlockSpec((ctx_len, dk), lambda b, h: (b, _DST["gk"] // dk + h)),
            pl.BlockSpec((ctx_len, dv), lambda b, h: (b, _DST["gv"] // dv + h)),
            pl.BlockSpec((ctx_len, LANE), lambda b, h: (b, 0)),
            pl.BlockSpec((LANE, 2 * dk), lambda b, h: (0, h)),
            pl.BlockSpec((1, 2 * dk), lambda b, h: (0, h)),
            pl.BlockSpec((1, dv), lambda b, h: (0, h)),
        ],
        out_specs=pl.BlockSpec((seq, dv), lambda b, h: (b, h)),
        scratch_shapes=[
            pltpu.VMEM((seq, dk), BF16), pltpu.VMEM((seq, dk), BF16),
            pltpu.VMEM((seq, dk), BF16), pltpu.VMEM((seq, dk), BF16),
            pltpu.VMEM((ctx_len, dk), BF16), pltpu.VMEM((ctx_len, dk), BF16),
            pltpu.VMEM((n_blk, dk, LANE), F32), pltpu.VMEM((dk, LANE), F32),
            pltpu.VMEM((seq, dv), F32),
            pltpu.VMEM((2, dk, dv), F32),
        ],
        compiler_params=pltpu.CompilerParams(dimension_semantics=("arbitrary", "arbitrary"),
                                             vmem_limit_bytes=VMEM_LIMIT),
        name="gla",
    )(p, p, p, p, ga, pc, pc, gac, up_stack, bias2, gla_gn)


def _post_kernel(or_ref, og_ref, ga0_ref, ga1_ref, gb0_ref, gb1_ref, x_ref, wur_ref, wug_ref, wo_ref,
                 g1_ref, sh2_ref, sc2_ref, npost_ref, npre_ref, hs_ref, h2_ref):
    y_ret = _dot(or_ref[...], wur_ref[...])
    y_gla = _dot(og_ref[...], wug_ref[...])
    half = ga0_ref.shape[1]

    def merge(ga_r, gb_r, lo):
        return (_sigmoid(ga_r[...].astype(F32)) * y_ret[:, lo:lo + half]
                + _sigmoid(gb_r[...].astype(F32)) * y_gla[:, lo:lo + half]).astype(BF16)

    merged = jnp.concatenate([merge(ga0_ref, gb0_ref, 0), merge(ga1_ref, gb1_ref, half)], axis=1)
    y = _dot(merged, wo_ref[...])
    hs = x_ref[...] + g1_ref[...] * (_rms(y) * npost_ref[...])
    hs_ref[...] = hs
    h2 = _rms(hs) * npre_ref[...] * (1.0 + sc2_ref[...]) + sh2_ref[...]
    h2_ref[...] = h2.astype(h2_ref.dtype)


def _post(o_r, o_g, p, x2d, w_up_ret, w_up_gla, w_out, mod3, npost, npre, *, tm, seq):
    m, d = x2d.shape
    tiles_per_seq = seq // tm
    const = dict(pipeline_mode=pl.Buffered(1))
    gw = d // 2

    def modspec(chunk):
        return pl.BlockSpec((None, 1, d), lambda i: (i // tiles_per_seq, 0, chunk))

    def gatespec(name, part):
        return pl.BlockSpec((tm, gw), lambda i: (i, _DST[name] // gw + part))

    return pl.pallas_call(
        _post_kernel,
        out_shape=(jax.ShapeDtypeStruct((m, d), F32), jax.ShapeDtypeStruct((m, d), BF16)),
        grid=(m // tm,),
        in_specs=[
            pl.BlockSpec((tm, RET_VW), lambda i: (i, 0)),
            pl.BlockSpec((tm, GLA_VW), lambda i: (i, 0)),
            gatespec("gate_a", 0), gatespec("gate_a", 1), gatespec("gate_b", 0), gatespec("gate_b", 1),
            pl.BlockSpec((tm, d), lambda i: (i, 0)),
            pl.BlockSpec((RET_VW, d), lambda i: (0, 0), **const),
            pl.BlockSpec((GLA_VW, d), lambda i: (0, 0), **const),
            pl.BlockSpec((d, d), lambda i: (0, 0), **const),
            modspec(2), modspec(3), modspec(4),
            pl.BlockSpec((1, d), lambda i: (0, 0)),
            pl.BlockSpec((1, d), lambda i: (0, 0)),
        ],
        out_specs=(pl.BlockSpec((tm, d), lambda i: (i, 0)),
                   pl.BlockSpec((tm, d), lambda i: (i, 0))),
        compiler_params=pltpu.CompilerParams(dimension_semantics=("arbitrary",),
                                             vmem_limit_bytes=VMEM_LIMIT),
        name="post",
    )(o_r, o_g, p, p, p, p, x2d, w_up_ret, w_up_gla, w_out, mod3, mod3, mod3, npost, npre)


def _ffn_kernel(h_ref, wg_ref, wu_ref, wd_ref, hs_ref, g2_ref, nw_ref, o_ref, acc_scr, *, n_f):
    f = pl.program_id(1)
    h = h_ref[...]
    g = _dot(h, wg_ref[...])
    u = _dot(h, wu_ref[...])
    a = (g * _sigmoid(g) * u).astype(BF16)
    part = _dot(a, wd_ref[...])

    @pl.when(f == 0)
    def _():
        acc_scr[...] = part

    @pl.when(f > 0)
    def _():
        acc_scr[...] += part

    @pl.when(f == n_f - 1)
    def _():
        o_ref[...] = hs_ref[...] + g2_ref[...] * (_rms(acc_scr[...]) * nw_ref[...])


def _ffn(h2, hs, wg, wu, wd, mod3, npost, *, tm, tf, seq):
    m, d = hs.shape
    d_ff = wg.shape[1]
    n_f = d_ff // tf
    tiles_per_seq = seq // tm
    kern = functools.partial(_ffn_kernel, n_f=n_f)
    return pl.pallas_call(
        kern,
        out_shape=jax.ShapeDtypeStruct((m, d), F32),
        grid=(m // tm, n_f),
        in_specs=[
            pl.BlockSpec((tm, d), lambda i, f: (i, 0)),
            pl.BlockSpec((d, tf), lambda i, f: (0, f)),
            pl.BlockSpec((d, tf), lambda i, f: (0, f)),
            pl.BlockSpec((tf, d), lambda i, f: (f, 0)),
            pl.BlockSpec((tm, d), lambda i, f: (i, 0)),
            pl.BlockSpec((None, 1, d), lambda i, f: (i // tiles_per_seq, 0, 5)),
            pl.BlockSpec((1, d), lambda i, f: (0, 0)),
        ],
        out_specs=pl.BlockSpec((tm, d), lambda i, f: (i, 0)),
        scratch_shapes=[pltpu.VMEM((tm, d), F32)],
        compiler_params=pltpu.CompilerParams(dimension_semantics=("arbitrary", "arbitrary"),
                                             vmem_limit_bytes=VMEM_LIMIT),
        name="ffn",
    )(h2, wg, wu, wd, hs, mod3, npost)


def _rope_tables(seq):
    rows = seq // GRID_W
    pos_r = jnp.repeat(jnp.arange(rows, dtype=F32), GRID_W)
    pos_c = jnp.tile(jnp.arange(GRID_W, dtype=F32), rows)
    n_f = RET_DK // 4
    inv = ROPE_BASE ** (-jnp.arange(n_f, dtype=F32) / n_f)
    ang = jnp.concatenate([pos_r[:, None] * inv, pos_c[:, None] * inv], axis=-1)
    return jnp.cos(ang), jnp.sin(ang)


def _gate_map_layout(gla_a_up, gla_a_bias):
    r, h, dk = GLA_LOW_RANK, GLA_HEADS, GLA_DK
    u = jnp.zeros((GA_W, h, 2, dk), F32)
    u = u.at[:r, :, 0, :].set(gla_a_up[0].reshape(r, h, dk))
    u = u.at[r:, :, 1, :].set(gla_a_up[1].reshape(r, h, dk))
    u = u.reshape(GA_W, h * 2 * dk)
    up_stack = jnp.concatenate([u, u, u, jnp.zeros((LANE - 3 * GA_W, h * 2 * dk), F32)], axis=0)
    bias2 = jnp.stack([gla_a_bias[0].reshape(h, dk), gla_a_bias[1].reshape(h, dk)], axis=1)
    return up_stack, bias2.reshape(1, h * 2 * dk)


def _layer(h_state, c_rows, ctx2d, w_mod, b_mod, norm_mix_pre, norm_mix_post, norm_ffn_pre, norm_ffn_post,
           w_in, ret_decay, gla_a_up, gla_a_bias, ret_gn, gla_gn, w_up_ret, w_up_gla, w_out,
           ffn_w_gate, ffn_w_up, ffn_w_down, cos, sin, *, batch, seq, ctx_len):
    d = D_MODEL
    ga0 = _SRC["ga"][0]
    w_main = jnp.concatenate([w_in[:, :ga0], w_in[:, ga0 + GA_W:]], axis=1).astype(BF16)
    w_g = w_in[:, ga0:ga0 + GA_W]
    w_ga = jnp.concatenate([w_g, w_g, w_g, jnp.zeros((d, LANE - 3 * GA_W), F32)], axis=1).astype(BF16)
    up_stack, bias2 = _gate_map_layout(gla_a_up, gla_a_bias)
    rd_b = jnp.broadcast_to(ret_decay.reshape(2, RET_HEADS, 1, 1), (2, RET_HEADS, 8, RET_DK))

    mod = _modulation(c_rows, w_mod, b_mod)
    mod3 = mod.reshape(mod.shape[0], 1, 6 * d)
    nw_pre = norm_mix_pre.reshape(1, d)

    tm = 1024
    p, ga = _inproj(h_state, nw_pre, mod3, lambda i: i // (seq // tm), w_main, w_ga, cos, sin,
                    tm=tm, tn=1024, ncols=N_MAIN, rope=True, seq_tiles=seq // tm)
    pc, gac = _inproj(ctx2d, nw_pre, mod3, lambda i: batch, w_main, w_ga, cos, sin,
                      tm=batch * ctx_len, tn=512, ncols=N_CTX, rope=False, seq_tiles=1)

    o_r = _retention(p, pc, rd_b, ret_gn.reshape(1, RET_VW), batch=batch, seq=seq, ctx_len=ctx_len)
    o_g = _gla(p, ga, pc, gac, up_stack, bias2, gla_gn.reshape(1, GLA_VW), batch=batch, seq=seq, ctx_len=ctx_len)

    hs, h2 = _post(o_r, o_g, p, h_state, w_up_ret.astype(BF16), w_up_gla.astype(BF16), w_out.astype(BF16),
                   mod3, norm_mix_post.reshape(1, d), norm_ffn_pre.reshape(1, d), tm=256, seq=seq)
    return _ffn(h2, hs, ffn_w_gate.astype(BF16), ffn_w_up.astype(BF16), ffn_w_down.astype(BF16),
                mod3, norm_ffn_post.reshape(1, d), tm=512, tf=512, seq=seq)


def kernel(x, c, ctx, c_ctx, w_mod, b_mod, norm_mix_pre, norm_mix_post, norm_ffn_pre, norm_ffn_post,
           w_in, ret_decay, gla_a_up, gla_a_bias, ret_gn, gla_gn, w_up_ret, w_up_gla, w_out,
           ffn_w_gate, ffn_w_up, ffn_w_down):
    batch, seq, d = x.shape
    ctx_len = ctx.shape[1]
    depth = w_mod.shape[0]
    cos, sin = _rope_tables(seq)
    c_rows = jnp.zeros((8, d), F32).at[:batch].set(c.astype(F32)).at[batch].set(c_ctx.astype(F32))
    ctx2d = ctx.astype(F32).reshape(batch * ctx_len, d)
    h_state = x.astype(F32).reshape(batch * seq, d)
    for i in range(depth):
        h_state = _layer(h_state, c_rows, ctx2d, w_mod[i], b_mod[i], norm_mix_pre[i], norm_mix_post[i],
                         norm_ffn_pre[i], norm_ffn_post[i], w_in[i], ret_decay[i], gla_a_up[i], gla_a_bias[i],
                         ret_gn[i], gla_gn[i], w_up_ret[i], w_up_gla[i], w_out[i],
                         ffn_w_gate[i], ffn_w_up[i], ffn_w_down[i], cos, sin,
                         batch=batch, seq=seq, ctx_len=ctx_len)
    return h_state.reshape(batch, seq, d).astype(x.dtype)
```

```python
import functools

import jax
import jax.numpy as jnp
from jax import lax
from jax.experimental import pallas as pl
from jax.experimental.pallas import tpu as pltpu

F32 = jnp.float32
BF16 = jnp.bfloat16

D_MODEL = 2048
GRID_W = 64
RET_HEADS = 4
RET_DK = 256
RET_DV = 256
GLA_HEADS = 4
GLA_DK = 128
GLA_DV = 256
GLA_LOW_RANK = 16
GLA_GATE_NORM = 16.0
ROPE_BASE = 10000.0
EPS = 1e-6

RET_QK = RET_HEADS * RET_DK
RET_VW = RET_HEADS * RET_DV
GLA_KW = GLA_HEADS * GLA_DK
GLA_VW = GLA_HEADS * GLA_DV
GA_W = 2 * GLA_LOW_RANK
LANE = 128

_SRC = {}
_off = 0
for _name, _w in (("rk", RET_QK), ("rv", RET_VW), ("gk", GLA_KW), ("gv", GLA_VW), ("ga", GA_W),
                  ("rq", RET_QK), ("rg", RET_VW), ("gq", GLA_KW), ("gg", GLA_VW),
                  ("gate_a", D_MODEL), ("gate_b", D_MODEL)):
    _SRC[_name] = (_off, _w)
    _off += _w

_ORDER = ("rk", "rv", "gv", "gk", "gq", "rq", "rg", "gg", "gate_a", "gate_b")
_DST = {}
_off = 0
for _name in _ORDER:
    _DST[_name] = _off
    _off += _SRC[_name][1]
N_MAIN = _off
N_CTX = _DST["gq"]
PACK_W = 256

VMEM_LIMIT = 56 * 1024 * 1024


def _dot(a, b):
    return jnp.dot(a, b, preferred_element_type=F32)


def _dot_nt(a, b):
    return lax.dot_general(a, b, (((1,), (1,)), ((), ())), preferred_element_type=F32)


def _dot_tn(a, b):
    return lax.dot_general(a, b, (((0,), (0,)), ((), ())), preferred_element_type=F32)


def _sigmoid(x):
    return 1.0 / (1.0 + jnp.exp(-x))


def _rms(x):
    return x * lax.rsqrt(jnp.mean(x * x, axis=-1, keepdims=True) + EPS)


def _split_bf16(x):
    hi = x.astype(BF16)
    lo = (x - hi.astype(F32)).astype(BF16)
    return hi, lo


def _mod_kernel(c_ref, w_ref, b_ref, o_ref):
    cf = c_ref[...]
    s = cf * _sigmoid(cf)
    o_ref[...] = jnp.dot(s, w_ref[...], preferred_element_type=F32,
                         precision=lax.Precision.HIGHEST) + b_ref[...]


def _modulation(c_rows, w_mod, b_mod):
    rows, d = c_rows.shape
    n = w_mod.shape[1]
    tn = 1024
    return pl.pallas_call(
        _mod_kernel,
        out_shape=jax.ShapeDtypeStruct((rows, n), F32),
        grid=(n // tn,),
        in_specs=[pl.BlockSpec((rows, d), lambda j: (0, 0)),
                  pl.BlockSpec((d, tn), lambda j: (0, j)),
                  pl.BlockSpec((1, tn), lambda j: (0, j))],
        out_specs=pl.BlockSpec((rows, tn), lambda j: (0, j)),
        compiler_params=pltpu.CompilerParams(dimension_semantics=("arbitrary",),
                                             vmem_limit_bytes=VMEM_LIMIT),
        name="mod",
    )(c_rows, w_mod, b_mod.reshape(1, n))


def _pack_kernel(a_idx_ref, n_idx_ref, shift_ref, a_ref, n_ref, o_ref):
    j = pl.program_id(0)

    @pl.when(shift_ref[j] == 0)
    def _():
        o_ref[...] = a_ref[...].astype(o_ref.dtype)

    @pl.when(shift_ref[j] != 0)
    def _():
        o_ref[...] = jnp.concatenate([a_ref[:, GA_W:], n_ref[:, :GA_W]], axis=1).astype(o_ref.dtype)


def _pack_w_in(w_in):
    d = w_in.shape[0]
    a_idx, n_idx, shift = [], [], []
    for name in _ORDER:
        src, width = _SRC[name]
        for c in range(src, src + width, PACK_W):
            base = c - c % PACK_W
            assert c - base in (0, GA_W)
            a_idx.append(base // PACK_W)
            n_idx.append((base + PACK_W) // LANE if c != base else (n_idx[-1] if n_idx else 0))
            shift.append(c - base)
    n_tiles = len(a_idx)
    as_i32 = lambda v: jnp.asarray(v, jnp.int32)
    return pl.pallas_call(
        _pack_kernel,
        out_shape=jax.ShapeDtypeStruct((d, n_tiles * PACK_W), BF16),
        grid_spec=pltpu.PrefetchScalarGridSpec(
            num_scalar_prefetch=3,
            grid=(n_tiles,),
            in_specs=[pl.BlockSpec((d, PACK_W), lambda j, a, n, s: (0, a[j])),
                      pl.BlockSpec((d, LANE), lambda j, a, n, s: (0, n[j]))],
            out_specs=pl.BlockSpec((d, PACK_W), lambda j, a, n, s: (0, j)),
        ),
        compiler_params=pltpu.CompilerParams(dimension_semantics=("arbitrary",),
                                             vmem_limit_bytes=VMEM_LIMIT),
        name="pack_w_in",
    )(as_i32(a_idx), as_i32(n_idx), as_i32(shift), w_in, w_in)


def _inproj_kernel(x_ref, nw_ref, sh_ref, sc_ref, w_ref, wga_ref, cos_ref, sin_ref,
                   o_ref, ga_ref, h_scr, *, rope_tiles, tn):
    j = pl.program_id(1)

    @pl.when(j == 0)
    def _():
        h = _rms(x_ref[...]) * nw_ref[...] * (1.0 + sc_ref[...]) + sh_ref[...]
        hb = h.astype(BF16)
        h_scr[...] = hb
        g3 = _dot(hb, wga_ref[...])
        lane = lax.broadcasted_iota(jnp.int32, g3.shape, 1)
        resid = g3 - g3.astype(BF16).astype(F32)
        ga_ref[...] = jnp.where((lane >= GA_W) & (lane < 2 * GA_W), resid, g3).astype(ga_ref.dtype)

    if not rope_tiles:
        o_ref[...] = _dot(h_scr[...], w_ref[...]).astype(o_ref.dtype)
        return

    is_rope = functools.reduce(jnp.logical_or, [j == t for t in rope_tiles])

    @pl.when(is_rope)
    def _():
        cos = cos_ref[...]
        sin = sin_ref[...]
        half = RET_DK // 2
        for hd in range(0, tn, RET_DK):
            acc = _dot(h_scr[...], w_ref[:, hd:hd + RET_DK])
            t1 = acc[:, :half]
            t2 = acc[:, half:]
            o_ref[:, hd:hd + half] = (t1 * cos - t2 * sin).astype(o_ref.dtype)
            o_ref[:, hd + half:hd + RET_DK] = (t1 * sin + t2 * cos).astype(o_ref.dtype)

    @pl.when(jnp.logical_not(is_rope))
    def _():
        o_ref[...] = _dot(h_scr[...], w_ref[...]).astype(o_ref.dtype)


def _inproj(x2d, norm_w, mod3, mod_row_of_tile, w_main, w_ga, cos, sin, *, tm, tn, ncols, rope, seq_tiles):
    m, d = x2d.shape
    rope_tiles = ()
    if rope:
        assert tn == RET_QK and _DST["rk"] % tn == 0 and _DST["rq"] % tn == 0
        rope_tiles = (_DST["rk"] // tn, _DST["rq"] // tn)
    kern = functools.partial(_inproj_kernel, rope_tiles=rope_tiles, tn=tn)
    return pl.pallas_call(
        kern,
        out_shape=(jax.ShapeDtypeStruct((m, ncols), BF16),
                   jax.ShapeDtypeStruct((m, LANE), BF16)),
        grid=(m // tm, ncols // tn),
        in_specs=[
            pl.BlockSpec((tm, d), lambda i, j: (i, 0)),
            pl.BlockSpec((1, d), lambda i, j: (0, 0)),
            pl.BlockSpec((None, 1, d), lambda i, j: (mod_row_of_tile(i), 0, 0)),
            pl.BlockSpec((None, 1, d), lambda i, j: (mod_row_of_tile(i), 0, 1)),
            pl.BlockSpec((d, tn), lambda i, j: (0, j)),
            pl.BlockSpec((d, LANE), lambda i, j: (0, 0)),
            pl.BlockSpec((tm, RET_DK // 2), lambda i, j: (i % seq_tiles, 0)),
            pl.BlockSpec((tm, RET_DK // 2), lambda i, j: (i % seq_tiles, 0)),
        ],
        out_specs=(pl.BlockSpec((tm, tn), lambda i, j: (i, j)),
                   pl.BlockSpec((tm, LANE), lambda i, j: (i, 0))),
        scratch_shapes=[pltpu.VMEM((tm, d), BF16)],
        compiler_params=pltpu.CompilerParams(dimension_semantics=("arbitrary", "arbitrary"),
                                             vmem_limit_bytes=VMEM_LIMIT),
        name="inproj",
    )(x2d, norm_w, mod3, mod3, w_main, w_ga, cos, sin)


RET_CHUNK = 256


def _ret_kernel(q_ref, k_ref, v_ref, g_ref, kc_ref, vc_ref, rd_ref, gn_ref, o_ref,
                of_scr, s_scr, *, n_chunks):
    c_len = RET_CHUNK
    ii = lax.broadcasted_iota(jnp.int32, (c_len, c_len), 0)
    jj = lax.broadcasted_iota(jnp.int32, (c_len, c_len), 1)
    rowi = lax.broadcasted_iota(jnp.int32, (c_len, RET_DK), 0).astype(F32)
    scale = RET_DK ** -0.5

    for direction in (0, 1):
        lg = -jnp.exp(rd_ref[direction][0:1, :])
        if direction == 0:
            dmat = jnp.where(jj <= ii, jnp.exp((ii - jj).astype(F32) * lg), 0.0)
            eq = jnp.exp((rowi + 1.0) * lg)
            wk = jnp.exp((c_len - 1.0 - rowi) * lg)
        else:
            dmat = jnp.where(jj > ii, jnp.exp((jj - ii).astype(F32) * lg), 0.0)
            eq = jnp.exp((c_len - rowi) * lg)
            wk = jnp.exp(rowi * lg)
        sdec = jnp.exp(float(c_len) * lg)

        kcw = (kc_ref[...].astype(F32) * wk).astype(BF16)
        s_scr[...] = _dot_tn(kcw, vc_ref[...])

        def body(t, carry, direction=direction, dmat=dmat, eq=eq, wk=wk, sdec=sdec):
            c = t if direction == 0 else n_chunks - 1 - t
            r0 = pl.multiple_of(c * c_len, c_len)
            q = q_ref[pl.ds(r0, c_len), :]
            k = k_ref[pl.ds(r0, c_len), :]
            v = v_ref[pl.ds(r0, c_len), :]
            s_state = s_scr[...]
            sc = (_dot_nt(q, k) * dmat).astype(BF16)
            o = _dot(sc, v) + eq * _dot(q, s_state.astype(BF16))
            kw = (k.astype(F32) * wk).astype(BF16)
            s_scr[...] = sdec * s_state + _dot_tn(kw, v)
            if direction == 0:
                of_scr[pl.ds(r0, c_len), :] = o
            else:
                tot = (of_scr[pl.ds(r0, c_len), :] + o) * scale
                mu = jnp.mean(tot, axis=-1, keepdims=True)
                cen = tot - mu
                var = jnp.mean(cen * cen, axis=-1, keepdims=True)
                nrm = cen * lax.rsqrt(var + EPS)
                g = g_ref[pl.ds(r0, c_len), :].astype(F32)
                o_ref[pl.ds(r0, c_len), :] = (nrm * gn_ref[...] * (g * _sigmoid(g))).astype(o_ref.dtype)
            return carry

        lax.fori_loop(0, n_chunks, body, 0)


def _retention(p, pc, rd_b, ret_gn, *, batch, seq, ctx_len):
    assert ctx_len == RET_CHUNK
    n_chunks = seq // RET_CHUNK
    w = RET_DK
    kern = functools.partial(_ret_kernel, n_chunks=n_chunks)

    def col(name):
        return _DST[name] // w

    return pl.pallas_call(
        kern,
        out_shape=jax.ShapeDtypeStruct((batch * seq, RET_VW), BF16),
        grid=(batch, RET_HEADS),
        in_specs=[
            pl.BlockSpec((seq, w), lambda b, h: (b, col("rq") + h)),
            pl.BlockSpec((seq, w), lambda b, h: (b, col("rk") + h)),
            pl.BlockSpec((seq, w), lambda b, h: (b, col("rv") + h)),
            pl.BlockSpec((seq, w), lambda b, h: (b, col("rg") + h)),
            pl.BlockSpec((ctx_len, w), lambda b, h: (b, col("rk") + h)),
            pl.BlockSpec((ctx_len, w), lambda b, h: (b, col("rv") + h)),
            pl.BlockSpec((2, None, 8, w), lambda b, h: (0, h, 0, 0)),
            pl.BlockSpec((1, w), lambda b, h: (0, h)),
        ],
        out_specs=pl.BlockSpec((seq, w), lambda b, h: (b, h)),
        scratch_shapes=[pltpu.VMEM((seq, RET_DV), F32), pltpu.VMEM((RET_DK, RET_DV), F32)],
        compiler_params=pltpu.CompilerParams(dimension_semantics=("arbitrary", "arbitrary"),
                                             vmem_limit_bytes=VMEM_LIMIT),
        name="retention",
    )(p, p, p, p, pc, pc, rd_b, ret_gn)


GLA_SUB = 64
GLA_BLK = 256


def _gla_kernel(q_ref, k_ref, v_ref, g_ref, ga_ref, kc_ref, vc_ref, gac_ref, up_ref, bias_ref, gn_ref,
                o_ref, qf_scr, qb_scr, khf_scr, khb_scr, khcf_scr, khcb_scr, dec_scr, decc_scr,
                o_scr, s_scr, *, n_blk):
    blk, sub, dk = GLA_BLK, GLA_SUB, GLA_DK
    nsub = blk // sub
    shift = sub.bit_length() - 1
    scale = GLA_DK ** -0.5

    ii = lax.broadcasted_iota(jnp.int32, (blk, blk), 0)
    jj = lax.broadcasted_iota(jnp.int32, (blk, blk), 1)
    same = (ii >> shift) == (jj >> shift)
    lower = jj <= ii
    tmat = jnp.where(same & lower, 1.0, 0.0).astype(BF16)
    r8 = lax.broadcasted_iota(jnp.int32, (8, dk), 0)

    u = up_ref[...]
    u_hi, u_lo = _split_bf16(u)
    urow = lax.broadcasted_iota(jnp.int32, u.shape, 0)
    rhs = jnp.where(urow >= 2 * GA_W, u_lo, u_hi)
    bias = bias_ref[...]

    def prep(ga_blk, k_blk, q_blk, v_blk):
        z = _dot(ga_blk, rhs) + bias
        a = (jnp.minimum(z, 0.0) - jnp.log(1.0 + jnp.exp(-jnp.abs(z)))) * (1.0 / GLA_GATE_NORM)
        a_hi, a_lo = _split_bf16(a)
        pre = _dot(tmat, a_hi) + _dot(tmat, a_lo)
        lasts = [pre[i * sub + sub - 1:i * sub + sub, :] for i in range(nsub)]
        tot = jnp.concatenate([jnp.broadcast_to(l, (sub, 2 * dk)) for l in lasts], axis=0)
        b_f = pre[:, :dk]
        ex_b = pre[:, dk:] - a[:, dk:]
        b_b = tot[:, dk:] - ex_b
        kf = k_blk.astype(F32)
        khf = (kf * jnp.exp(tot[:, :dk] - b_f)).astype(BF16)
        khb = (kf * jnp.exp(ex_b)).astype(BF16)

        def tile8(half):
            rows = [jnp.broadcast_to(l[:, half * dk:(half + 1) * dk], (8, dk)) for l in lasts]
            t = rows[nsub - 1]
            for i in range(nsub - 2, -1, -1):
                t = jnp.where(r8 == i, rows[i], t)
            return t

        dec = jnp.exp(jnp.concatenate([tile8(0), tile8(1), jnp.zeros((LANE - 16, dk), F32)], axis=0))
        dec_t = dec.T
        if q_blk is None:
            return khf, khb, dec_t
        qf32 = q_blk.astype(F32)
        qf = (qf32 * jnp.exp(b_f)).astype(BF16)
        qb = (qf32 * jnp.exp(b_b)).astype(BF16)
        ktf = (kf * jnp.exp(-b_f)).astype(BF16)
        ktb = (kf * jnp.exp(-b_b)).astype(BF16)
        s = jnp.where(same, jnp.where(lower, _dot_nt(qf, ktf), _dot_nt(qb, ktb)), 0.0).astype(BF16)
        return khf, khb, dec_t, qf, qb, _dot(s, v_blk)

    khf, khb, dec_t = prep(gac_ref[...], kc_ref[...], None, None)
    khcf_scr[...] = khf
    khcb_scr[...] = khb
    decc_scr[...] = dec_t

    def prep_body(t, carry):
        r = pl.multiple_of(t * blk, blk)
        khf, khb, dec_t, qf, qb, o_intra = prep(ga_ref[pl.ds(r, blk), :], k_ref[pl.ds(r, blk), :],
                                                q_ref[pl.ds(r, blk), :], v_ref[pl.ds(r, blk), :])
        khf_scr[pl.ds(r, blk), :] = khf
        khb_scr[pl.ds(r, blk), :] = khb
        qf_scr[pl.ds(r, blk), :] = qf
        qb_scr[pl.ds(r, blk), :] = qb
        o_scr[pl.ds(r, blk), :] = o_intra
        dec_scr[t] = dec_t
        return carry

    lax.fori_loop(0, n_blk, prep_body, 0)

    def scan_blk(direction, dec_t, kh_at, v_at, q_at=None, o_rows=None):
        s = s_scr[direction]
        order = range(nsub) if direction == 0 else range(nsub - 1, -1, -1)
        for i in order:
            col = dec_t[:, direction * 8 + i:direction * 8 + i + 1]
            if q_at is not None:
                rows = o_rows(i)
                o_scr[rows, :] += _dot(q_at(i), s.astype(BF16))
            s = s * col + _dot_tn(kh_at(i), v_at(i))
        s_scr[direction] = s

    s_scr[...] = jnp.zeros(s_scr.shape, F32)
    for direction, khc in ((0, khcf_scr), (1, khcb_scr)):
        scan_blk(direction, decc_scr[...],
                 lambda i, khc=khc: khc[i * sub:(i + 1) * sub, :],
                 lambda i: vc_ref[i * sub:(i + 1) * sub, :])

    def finalize(t):
        r = pl.multiple_of(t * blk, blk)
        nrm = _rms(o_scr[pl.ds(r, blk), :] * scale)
        g = g_ref[pl.ds(r, blk), :].astype(F32)
        o_ref[pl.ds(r, blk), :] = (nrm * gn_ref[...] * (g * _sigmoid(g))).astype(o_ref.dtype)

    def scan_body(t, carry, done):
        for direction, q_scr, kh_scr in ((0, qf_scr, khf_scr), (1, qb_scr, khb_scr)):
            tb = t if direction == 0 else n_blk - 1 - t

            def rows(i, tb=tb):
                return pl.ds(pl.multiple_of(tb * blk + i * sub, sub), sub)

            scan_blk(direction, dec_scr[tb],
                     lambda i, kh_scr=kh_scr, rows=rows: kh_scr[rows(i), :],
                     lambda i, rows=rows: v_ref[rows(i), :],
                     lambda i, q_scr=q_scr, rows=rows: q_scr[rows(i), :],
                     rows)
        if done:
            finalize(t)
            finalize(n_blk - 1 - t)
        return carry

    lax.fori_loop(0, n_blk // 2, functools.partial(scan_body, done=False), 0)
    lax.fori_loop(n_blk // 2, n_blk, functools.partial(scan_body, done=True), 0)


def _gla(p, ga, pc, gac, up_stack, bias2, gla_gn, *, batch, seq, ctx_len):
    assert ctx_len == GLA_BLK and (seq // GLA_BLK) % 2 == 0
    n_blk = seq // GLA_BLK
    kern = functools.partial(_gla_kernel, n_blk=n_blk)
    dk, dv = GLA_DK, GLA_DV
    return pl.pallas_call(
        kern,
        out_shape=jax.ShapeDtypeStruct((batch * seq, GLA_VW), BF16),
        grid=(batch, GLA_HEADS),
        in_specs=[
            pl.BlockSpec((seq, dk), lambda b, h: (b, _DST["gq"] // dk + h)),
            pl.BlockSpec((seq, dk), lambda b, h: (b, _DST["gk"] // dk + h)),
            pl.BlockSpec((seq, dv), lambda b, h: (b, _DST["gv"] // dv + h)),
            pl.BlockSpec((seq, dv), lambda b, h: (b, _DST["gg"] // dv + h)),
            pl.BlockSpec((seq, LANE), lambda b, h: (b, 0)),
            pl.BlockSpec((ctx_len, dk), lambda b, h: (b, _DST["gk"] // dk + h)),
            pl.BlockSpec((ctx_len, dv), lambda b, h: (b, _DST["gv"] // dv + h)),
            pl.BlockSpec((ctx_len, LANE), lambda b, h: (b, 0)),
            pl.BlockSpec((LANE, 2 * dk), lambda b, h: (0, h)),
            pl.BlockSpec((1, 2 * dk), lambda b, h: (0, h)),
            pl.BlockSpec((1, dv), lambda b, h: (0, h)),
        ],
        out_specs=pl.BlockSpec((seq, dv), lambda b, h: (b, h)),
        scratch_shapes=[
            pltpu.VMEM((seq, dk), BF16), pltpu.VMEM((seq, dk), BF16),
            pltpu.VMEM((seq, dk), BF16), pltpu.VMEM((seq, dk), BF16),
            pltpu.VMEM((ctx_len, dk), BF16), pltpu.VMEM((ctx_len, dk), BF16),
            pltpu.VMEM((n_blk, dk, LANE), F32), pltpu.VMEM((dk, LANE), F32),
            pltpu.VMEM((seq, dv), F32),
            pltpu.VMEM((2, dk, dv), F32),
        ],
        compiler_params=pltpu.CompilerParams(dimension_semantics=("arbitrary", "arbitrary"),
                                             vmem_limit_bytes=VMEM_LIMIT),
        name="gla",
    )(p, p, p, p, ga, pc, pc, gac, up_stack, bias2, gla_gn)


def _post_kernel(or_ref, og_ref, ga0_ref, ga1_ref, gb0_ref, gb1_ref, x_ref, wur_ref, wug_ref, wo_ref,
                 g1_ref, sh2_ref, sc2_ref, npost_ref, npre_ref, hs_ref, h2_ref):
    y_ret = _dot(or_ref[...], wur_ref[...])
    y_gla = _dot(og_ref[...], wug_ref[...])
    half = ga0_ref.shape[1]

    def merge(ga_r, gb_r, lo):
        return (_sigmoid(ga_r[...].astype(F32)) * y_ret[:, lo:lo + half]
                + _sigmoid(gb_r[...].astype(F32)) * y_gla[:, lo:lo + half]).astype(BF16)

    merged = jnp.concatenate([merge(ga0_ref, gb0_ref, 0), merge(ga1_ref, gb1_ref, half)], axis=1)
    y = _dot(merged, wo_ref[...])
    hs = x_ref[...] + g1_ref[...] * (_rms(y) * npost_ref[...])
    hs_ref[...] = hs
    h2 = _rms(hs) * npre_ref[...] * (1.0 + sc2_ref[...]) + sh2_ref[...]
    h2_ref[...] = h2.astype(h2_ref.dtype)


def _post(o_r, o_g, p, x2d, w_up_ret, w_up_gla, w_out, mod3, npost, npre, *, tm, seq):
    m, d = x2d.shape
    tiles_per_seq = seq // tm
    const = dict(pipeline_mode=pl.Buffered(1))
    gw = d // 2

    def modspec(chunk):
        return pl.BlockSpec((None, 1, d), lambda i: (i // tiles_per_seq, 0, chunk))

    def gatespec(name, part):
        return pl.BlockSpec((tm, gw), lambda i: (i, _DST[name] // gw + part))

    return pl.pallas_call(
        _post_kernel,
        out_shape=(jax.ShapeDtypeStruct((m, d), F32), jax.ShapeDtypeStruct((m, d), BF16)),
        grid=(m // tm,),
        in_specs=[
            pl.BlockSpec((tm, RET_VW), lambda i: (i, 0)),
            pl.BlockSpec((tm, GLA_VW), lambda i: (i, 0)),
            gatespec("gate_a", 0), gatespec("gate_a", 1), gatespec("gate_b", 0), gatespec("gate_b", 1),
            pl.BlockSpec((tm, d), lambda i: (i, 0)),
            pl.BlockSpec((RET_VW, d), lambda i: (0, 0), **const),
            pl.BlockSpec((GLA_VW, d), lambda i: (0, 0), **const),
            pl.BlockSpec((d, d), lambda i: (0, 0), **const),
            modspec(2), modspec(3), modspec(4),
            pl.BlockSpec((1, d), lambda i: (0, 0)),
            pl.BlockSpec((1, d), lambda i: (0, 0)),
        ],
        out_specs=(pl.BlockSpec((tm, d), lambda i: (i, 0)),
                   pl.BlockSpec((tm, d), lambda i: (i, 0))),
        compiler_params=pltpu.CompilerParams(dimension_semantics=("arbitrary",),
                                             vmem_limit_bytes=VMEM_LIMIT),
        name="post",
    )(o_r, o_g, p, p, p, p, x2d, w_up_ret, w_up_gla, w_out, mod3, mod3, mod3, npost, npre)


MXU_N = 256


def _ffn_up_kernel(h_ref, wg_ref, wu_ref, a_ref):
    h = h_ref[...]
    for lo in range(0, a_ref.shape[1], MXU_N):
        g = _dot(h, wg_ref[:, lo:lo + MXU_N])
        u = _dot(h, wu_ref[:, lo:lo + MXU_N])
        a_ref[:, lo:lo + MXU_N] = (g * _sigmoid(g) * u).astype(a_ref.dtype)


def _ffn_up(h2, wg, wu, *, tm, tf):
    m, d = h2.shape
    d_ff = wg.shape[1]
    return pl.pallas_call(
        _ffn_up_kernel,
        out_shape=jax.ShapeDtypeStruct((m, d_ff), BF16),
        grid=(m // tm, d_ff // tf),
        in_specs=[
            pl.BlockSpec((tm, d), lambda i, f: (i, 0)),
            pl.BlockSpec((d, tf), lambda i, f: (0, f)),
            pl.BlockSpec((d, tf), lambda i, f: (0, f)),
        ],
        out_specs=pl.BlockSpec((tm, tf), lambda i, f: (i, f)),
        compiler_params=pltpu.CompilerParams(dimension_semantics=("arbitrary", "arbitrary"),
                                             vmem_limit_bytes=VMEM_LIMIT),
        name="ffn_up",
    )(h2, wg, wu)


def _ffn_down_kernel(a_ref, wd_ref, hs_ref, g2_ref, nw_ref, o_ref, y_scr, *, n_split):
    tm, d = o_ref.shape
    w = d // n_split
    ss = jnp.zeros((tm, 1), F32)
    for lo in range(0, d, w):
        y = _dot(a_ref[...], wd_ref[:, lo:lo + w])
        y_scr[:, lo:lo + w] = y
        ss = ss + jnp.sum(y * y, axis=-1, keepdims=True)
    inv = lax.rsqrt(ss * (1.0 / d) + EPS)
    o_ref[...] = hs_ref[...] + g2_ref[...] * (y_scr[...] * inv * nw_ref[...])


def _ffn_down(a, hs, wd, mod3, npost, *, tm, seq):
    m, d = hs.shape
    d_ff = wd.shape[0]
    tiles_per_seq = seq // tm
    kern = functools.partial(_ffn_down_kernel, n_split=4)
    return pl.pallas_call(
        kern,
        out_shape=jax.ShapeDtypeStruct((m, d), F32),
        grid=(m // tm,),
        in_specs=[
            pl.BlockSpec((tm, d_ff), lambda i: (i, 0)),
            pl.BlockSpec((d_ff, d), lambda i: (0, 0), pipeline_mode=pl.Buffered(1)),
            pl.BlockSpec((tm, d), lambda i: (i, 0)),
            pl.BlockSpec((None, 1, d), lambda i: (i // tiles_per_seq, 0, 5)),
            pl.BlockSpec((1, d), lambda i: (0, 0)),
        ],
        out_specs=pl.BlockSpec((tm, d), lambda i: (i, 0)),
        scratch_shapes=[pltpu.VMEM((tm, d), F32)],
        compiler_params=pltpu.CompilerParams(dimension_semantics=("arbitrary",),
                                             vmem_limit_bytes=VMEM_LIMIT),
        name="ffn_down",
    )(a, wd, hs, mod3, npost)


def _rope_tables(seq):
    rows = seq // GRID_W
    pos_r = jnp.repeat(jnp.arange(rows, dtype=F32), GRID_W)
    pos_c = jnp.tile(jnp.arange(GRID_W, dtype=F32), rows)
    n_f = RET_DK // 4
    inv = ROPE_BASE ** (-jnp.arange(n_f, dtype=F32) / n_f)
    ang = jnp.concatenate([pos_r[:, None] * inv, pos_c[:, None] * inv], axis=-1)
    return jnp.cos(ang), jnp.sin(ang)


def _gate_map_layout(gla_a_up, gla_a_bias):
    r, h, dk = GLA_LOW_RANK, GLA_HEADS, GLA_DK
    u = jnp.zeros((GA_W, h, 2, dk), F32)
    u = u.at[:r, :, 0, :].set(gla_a_up[0].reshape(r, h, dk))
    u = u.at[r:, :, 1, :].set(gla_a_up[1].reshape(r, h, dk))
    u = u.reshape(GA_W, h * 2 * dk)
    up_stack = jnp.concatenate([u, u, u, jnp.zeros((LANE - 3 * GA_W, h * 2 * dk), F32)], axis=0)
    bias2 = jnp.stack([gla_a_bias[0].reshape(h, dk), gla_a_bias[1].reshape(h, dk)], axis=1)
    return up_stack, bias2.reshape(1, h * 2 * dk)


def _layer(h_state, c_rows, ctx2d, w_mod, b_mod, norm_mix_pre, norm_mix_post, norm_ffn_pre, norm_ffn_post,
           w_in, ret_decay, gla_a_up, gla_a_bias, ret_gn, gla_gn, w_up_ret, w_up_gla, w_out,
           ffn_w_gate, ffn_w_up, ffn_w_down, cos, sin, *, batch, seq, ctx_len):
    d = D_MODEL
    ga0 = _SRC["ga"][0]
    w_main = _pack_w_in(w_in)
    w_g = w_in[:, ga0:ga0 + GA_W]
    w_ga = jnp.concatenate([w_g, w_g, w_g, jnp.zeros((d, LANE - 3 * GA_W), F32)], axis=1).astype(BF16)
    up_stack, bias2 = _gate_map_layout(gla_a_up, gla_a_bias)
    rd_b = jnp.broadcast_to(ret_decay.reshape(2, RET_HEADS, 1, 1), (2, RET_HEADS, 8, RET_DK))

    mod = _modulation(c_rows, w_mod, b_mod)
    mod3 = mod.reshape(mod.shape[0], 1, 6 * d)
    nw_pre = norm_mix_pre.reshape(1, d)

    tm = 1024
    p, ga = _inproj(h_state, nw_pre, mod3, lambda i: i // (seq // tm), w_main, w_ga, cos, sin,
                    tm=tm, tn=1024, ncols=N_MAIN, rope=True, seq_tiles=seq // tm)
    pc, gac = _inproj(ctx2d, nw_pre, mod3, lambda i: batch, w_main, w_ga, cos, sin,
                      tm=batch * ctx_len, tn=512, ncols=N_CTX, rope=False, seq_tiles=1)

    o_r = _retention(p, pc, rd_b, ret_gn.reshape(1, RET_VW), batch=batch, seq=seq, ctx_len=ctx_len)
    o_g = _gla(p, ga, pc, gac, up_stack, bias2, gla_gn.reshape(1, GLA_VW), batch=batch, seq=seq, ctx_len=ctx_len)

    hs, h2 = _post(o_r, o_g, p, h_state, w_up_ret.astype(BF16), w_up_gla.astype(BF16), w_out.astype(BF16),
                   mod3, norm_mix_post.reshape(1, d), norm_ffn_pre.reshape(1, d), tm=256, seq=seq)
    act = _ffn_up(h2, ffn_w_gate.astype(BF16), ffn_w_up.astype(BF16), tm=1024, tf=512)
    return _ffn_down(act, hs, ffn_w_down.astype(BF16), mod3, norm_ffn_post.reshape(1, d), tm=256, seq=seq)


def kernel(x, c, ctx, c_ctx, w_mod, b_mod, norm_mix_pre, norm_mix_post, norm_ffn_pre, norm_ffn_post,
           w_in, ret_decay, gla_a_up, gla_a_bias, ret_gn, gla_gn, w_up_ret, w_up_gla, w_out,
           ffn_w_gate, ffn_w_up, ffn_w_down):
    batch, seq, d = x.shape
    ctx_len = ctx.shape[1]
    depth = w_mod.shape[0]
    cos, sin = _rope_tables(seq)
    c_rows = jnp.zeros((8, d), F32).at[:batch].set(c.astype(F32)).at[batch].set(c_ctx.astype(F32))
    ctx2d = ctx.astype(F32).reshape(batch * ctx_len, d)
    h_state = x.astype(F32).reshape(batch * seq, d)
    for i in range(depth):
        h_state = _layer(h_state, c_rows, ctx2d, w_mod[i], b_mod[i], norm_mix_pre[i], norm_mix_post[i],
                         norm_ffn_pre[i], norm_ffn_post[i], w_in[i], ret_decay[i], gla_a_up[i], gla_a_bias[i],
                         ret_gn[i], gla_gn[i], w_up_ret[i], w_up_gla[i], w_out[i],
                         ffn_w_gate[i], ffn_w_up[i], ffn_w_down[i], cos, sin,
                         batch=batch, seq=seq, ctx_len=ctx_len)
    return h_state.reshape(batch, seq, d).astype(x.dtype)
```

```python
import functools

import jax
import jax.numpy as jnp
from jax import lax
from jax.experimental import pallas as pl
from jax.experimental.pallas import tpu as pltpu

F32 = jnp.float32
BF16 = jnp.bfloat16

D_MODEL = 2048
GRID_W = 64
RET_HEADS = 4
RET_DK = 256
RET_DV = 256
GLA_HEADS = 4
GLA_DK = 128
GLA_DV = 256
GLA_LOW_RANK = 16
GLA_GATE_NORM = 16.0
ROPE_BASE = 10000.0
EPS = 1e-6

RET_QK = RET_HEADS * RET_DK
RET_VW = RET_HEADS * RET_DV
GLA_KW = GLA_HEADS * GLA_DK
GLA_VW = GLA_HEADS * GLA_DV
GA_W = 2 * GLA_LOW_RANK
LANE = 128

_SRC = {}
_off = 0
for _name, _w in (("rk", RET_QK), ("rv", RET_VW), ("gk", GLA_KW), ("gv", GLA_VW), ("ga", GA_W),
                  ("rq", RET_QK), ("rg", RET_VW), ("gq", GLA_KW), ("gg", GLA_VW),
                  ("gate_a", D_MODEL), ("gate_b", D_MODEL)):
    _SRC[_name] = (_off, _w)
    _off += _w

_ORDER = ("rk", "rv", "gv", "gk", "gq", "rq", "rg", "gg", "gate_a", "gate_b")
_DST = {}
_off = 0
for _name in _ORDER:
    _DST[_name] = _off
    _off += _SRC[_name][1]
N_MAIN = _off
N_CTX = _DST["gq"]
PACK_W = 512

VMEM_LIMIT = 56 * 1024 * 1024


def _dot(a, b):
    return jnp.dot(a, b, preferred_element_type=F32)


def _dot_nt(a, b):
    return lax.dot_general(a, b, (((1,), (1,)), ((), ())), preferred_element_type=F32)


def _dot_tn(a, b):
    return lax.dot_general(a, b, (((0,), (0,)), ((), ())), preferred_element_type=F32)


def _sigmoid(x):
    return 1.0 / (1.0 + jnp.exp(-x))


def _rms(x):
    return x * lax.rsqrt(jnp.mean(x * x, axis=-1, keepdims=True) + EPS)


def _split_bf16(x):
    hi = x.astype(BF16)
    lo = (x - hi.astype(F32)).astype(BF16)
    return hi, lo


def _mod_kernel(c_ref, w_ref, b_ref, o_ref):
    cf = c_ref[...]
    s = cf * _sigmoid(cf)
    o_ref[...] = jnp.dot(s, w_ref[...], preferred_element_type=F32,
                         precision=lax.Precision.HIGHEST) + b_ref[...]


def _modulation(c_rows, w_mod, b_mod):
    rows, d = c_rows.shape
    n = w_mod.shape[1]
    tn = 1024
    return pl.pallas_call(
        _mod_kernel,
        out_shape=jax.ShapeDtypeStruct((rows, n), F32),
        grid=(n // tn,),
        in_specs=[pl.BlockSpec((rows, d), lambda j: (0, 0)),
                  pl.BlockSpec((d, tn), lambda j: (0, j)),
                  pl.BlockSpec((1, tn), lambda j: (0, j))],
        out_specs=pl.BlockSpec((rows, tn), lambda j: (0, j)),
        compiler_params=pltpu.CompilerParams(dimension_semantics=("arbitrary",),
                                             vmem_limit_bytes=VMEM_LIMIT),
        name="mod",
    )(c_rows, w_mod, b_mod.reshape(1, n))


def _pack_kernel(a_idx_ref, n_idx_ref, shift_ref, a_ref, n_ref, o_ref):
    j = pl.program_id(0)

    @pl.when(shift_ref[j] == 0)
    def _():
        o_ref[...] = a_ref[...].astype(o_ref.dtype)

    @pl.when(shift_ref[j] != 0)
    def _():
        o_ref[...] = jnp.concatenate([a_ref[GA_W:, :], n_ref[...]], axis=0).astype(o_ref.dtype)


def _pack_w_in(w_t):
    d = w_t.shape[1]
    a_idx, n_idx, shift = [], [], []
    for name in _ORDER:
        src, width = _SRC[name]
        for c in range(src, src + width, PACK_W):
            base = c - c % PACK_W
            assert c - base in (0, GA_W)
            a_idx.append(base // PACK_W)
            n_idx.append((base + PACK_W) // GA_W if c != base else (n_idx[-1] if n_idx else 0))
            shift.append(c - base)
    n_tiles = len(a_idx)
    as_i32 = lambda v: jnp.asarray(v, jnp.int32)
    return pl.pallas_call(
        _pack_kernel,
        out_shape=jax.ShapeDtypeStruct((n_tiles * PACK_W, d), BF16),
        grid_spec=pltpu.PrefetchScalarGridSpec(
            num_scalar_prefetch=3,
            grid=(n_tiles,),
            in_specs=[pl.BlockSpec((PACK_W, d), lambda j, a, n, s: (a[j], 0)),
                      pl.BlockSpec((GA_W, d), lambda j, a, n, s: (n[j], 0))],
            out_specs=pl.BlockSpec((PACK_W, d), lambda j, a, n, s: (j, 0)),
        ),
        compiler_params=pltpu.CompilerParams(dimension_semantics=("arbitrary",),
                                             vmem_limit_bytes=VMEM_LIMIT),
        name="pack_w_in",
    )(as_i32(a_idx), as_i32(n_idx), as_i32(shift), w_t, w_t)


def _inproj_kernel(x_ref, nw_ref, sh_ref, sc_ref, w_ref, wga_ref, cos_ref, sin_ref,
                   o_ref, ga_ref, h_scr, *, rope_tiles, tn):
    j = pl.program_id(1)

    @pl.when(j == 0)
    def _():
        h = _rms(x_ref[...]) * nw_ref[...] * (1.0 + sc_ref[...]) + sh_ref[...]
        hb = h.astype(BF16)
        h_scr[...] = hb
        g3 = _dot_nt(hb, wga_ref[...])
        lane = lax.broadcasted_iota(jnp.int32, g3.shape, 1)
        resid = g3 - g3.astype(BF16).astype(F32)
        ga_ref[...] = jnp.where((lane >= GA_W) & (lane < 2 * GA_W), resid, g3).astype(ga_ref.dtype)

    if not rope_tiles:
        o_ref[...] = _dot_nt(h_scr[...], w_ref[...]).astype(o_ref.dtype)
        return

    is_rope = functools.reduce(jnp.logical_or, [j == t for t in rope_tiles])

    @pl.when(is_rope)
    def _():
        cos = cos_ref[...]
        sin = sin_ref[...]
        half = RET_DK // 2
        for hd in range(0, tn, RET_DK):
            acc = _dot_nt(h_scr[...], w_ref[hd:hd + RET_DK, :])
            t1 = acc[:, :half]
            t2 = acc[:, half:]
            o_ref[:, hd:hd + half] = (t1 * cos - t2 * sin).astype(o_ref.dtype)
            o_ref[:, hd + half:hd + RET_DK] = (t1 * sin + t2 * cos).astype(o_ref.dtype)

    @pl.when(jnp.logical_not(is_rope))
    def _():
        o_ref[...] = _dot_nt(h_scr[...], w_ref[...]).astype(o_ref.dtype)


def _inproj(x2d, norm_w, mod3, mod_row_of_tile, w_main, w_ga, cos, sin, *, tm, tn, ncols, rope, seq_tiles):
    m, d = x2d.shape
    rope_tiles = ()
    if rope:
        assert tn == RET_QK and _DST["rk"] % tn == 0 and _DST["rq"] % tn == 0
        rope_tiles = (_DST["rk"] // tn, _DST["rq"] // tn)
    kern = functools.partial(_inproj_kernel, rope_tiles=rope_tiles, tn=tn)
    return pl.pallas_call(
        kern,
        out_shape=(jax.ShapeDtypeStruct((m, ncols), BF16),
                   jax.ShapeDtypeStruct((m, LANE), BF16)),
        grid=(m // tm, ncols // tn),
        in_specs=[
            pl.BlockSpec((tm, d), lambda i, j: (i, 0)),
            pl.BlockSpec((1, d), lambda i, j: (0, 0)),
            pl.BlockSpec((None, 1, d), lambda i, j: (mod_row_of_tile(i), 0, 0)),
            pl.BlockSpec((None, 1, d), lambda i, j: (mod_row_of_tile(i), 0, 1)),
            pl.BlockSpec((tn, d), lambda i, j: (j, 0)),
            pl.BlockSpec((LANE, d), lambda i, j: (0, 0)),
            pl.BlockSpec((tm, RET_DK // 2), lambda i, j: (i % seq_tiles, 0)),
            pl.BlockSpec((tm, RET_DK // 2), lambda i, j: (i % seq_tiles, 0)),
        ],
        out_specs=(pl.BlockSpec((tm, tn), lambda i, j: (i, j)),
                   pl.BlockSpec((tm, LANE), lambda i, j: (i, 0))),
        scratch_shapes=[pltpu.VMEM((tm, d), BF16)],
        compiler_params=pltpu.CompilerParams(dimension_semantics=("arbitrary", "arbitrary"),
                                             vmem_limit_bytes=VMEM_LIMIT),
        name="inproj",
    )(x2d, norm_w, mod3, mod3, w_main, w_ga, cos, sin)


RET_CHUNK = 256


def _ret_kernel(q_ref, k_ref, v_ref, g_ref, kc_ref, vc_ref, rd_ref, gn_ref, o_ref,
                of_scr, s_scr, *, n_chunks):
    c_len = RET_CHUNK
    ii = lax.broadcasted_iota(jnp.int32, (c_len, c_len), 0)
    jj = lax.broadcasted_iota(jnp.int32, (c_len, c_len), 1)
    rowi = lax.broadcasted_iota(jnp.int32, (c_len, RET_DK), 0).astype(F32)
    scale = RET_DK ** -0.5

    for direction in (0, 1):
        lg = -jnp.exp(rd_ref[direction][0:1, :])
        if direction == 0:
            dmat = jnp.where(jj <= ii, jnp.exp((ii - jj).astype(F32) * lg), 0.0)
            eq = jnp.exp((rowi + 1.0) * lg)
            wk = jnp.exp((c_len - 1.0 - rowi) * lg)
        else:
            dmat = jnp.where(jj > ii, jnp.exp((jj - ii).astype(F32) * lg), 0.0)
            eq = jnp.exp((c_len - rowi) * lg)
            wk = jnp.exp(rowi * lg)
        sdec = jnp.exp(float(c_len) * lg)

        kcw = (kc_ref[...].astype(F32) * wk).astype(BF16)
        s_scr[...] = _dot_tn(kcw, vc_ref[...])

        def body(t, carry, direction=direction, dmat=dmat, eq=eq, wk=wk, sdec=sdec):
            c = t if direction == 0 else n_chunks - 1 - t
            r0 = pl.multiple_of(c * c_len, c_len)
            q = q_ref[pl.ds(r0, c_len), :]
            k = k_ref[pl.ds(r0, c_len), :]
            v = v_ref[pl.ds(r0, c_len), :]
            s_state = s_scr[...]
            sc = (_dot_nt(q, k) * dmat).astype(BF16)
            o = _dot(sc, v) + eq * _dot(q, s_state.astype(BF16))
            kw = (k.astype(F32) * wk).astype(BF16)
            s_scr[...] = sdec * s_state + _dot_tn(kw, v)
            if direction == 0:
                of_scr[pl.ds(r0, c_len), :] = o
            else:
                tot = (of_scr[pl.ds(r0, c_len), :] + o) * scale
                mu = jnp.mean(tot, axis=-1, keepdims=True)
                cen = tot - mu
                var = jnp.mean(cen * cen, axis=-1, keepdims=True)
                nrm = cen * lax.rsqrt(var + EPS)
                g = g_ref[pl.ds(r0, c_len), :].astype(F32)
                o_ref[pl.ds(r0, c_len), :] = (nrm * gn_ref[...] * (g * _sigmoid(g))).astype(o_ref.dtype)
            return carry

        lax.fori_loop(0, n_chunks, body, 0)


def _retention(p, pc, rd_b, ret_gn, *, batch, seq, ctx_len):
    assert ctx_len == RET_CHUNK
    n_chunks = seq // RET_CHUNK
    w = RET_DK
    kern = functools.partial(_ret_kernel, n_chunks=n_chunks)

    def col(name):
        return _DST[name] // w

    return pl.pallas_call(
        kern,
        out_shape=jax.ShapeDtypeStruct((batch * seq, RET_VW), BF16),
        grid=(batch, RET_HEADS),
        in_specs=[
            pl.BlockSpec((seq, w), lambda b, h: (b, col("rq") + h)),
            pl.BlockSpec((seq, w), lambda b, h: (b, col("rk") + h)),
            pl.BlockSpec((seq, w), lambda b, h: (b, col("rv") + h)),
            pl.BlockSpec((seq, w), lambda b, h: (b, col("rg") + h)),
            pl.BlockSpec((ctx_len, w), lambda b, h: (b, col("rk") + h)),
            pl.BlockSpec((ctx_len, w), lambda b, h: (b, col("rv") + h)),
            pl.BlockSpec((2, None, 8, w), lambda b, h: (0, h, 0, 0)),
            pl.BlockSpec((1, w), lambda b, h: (0, h)),
        ],
        out_specs=pl.BlockSpec((seq, w), lambda b, h: (b, h)),
        scratch_shapes=[pltpu.VMEM((seq, RET_DV), F32), pltpu.VMEM((RET_DK, RET_DV), F32)],
        compiler_params=pltpu.CompilerParams(dimension_semantics=("arbitrary", "arbitrary"),
                                             vmem_limit_bytes=VMEM_LIMIT),
        name="retention",
    )(p, p, p, p, pc, pc, rd_b, ret_gn)


GLA_SUB = 64
GLA_BLK = 256


def _gla_kernel(q_ref, k_ref, v_ref, g_ref, ga_ref, kc_ref, vc_ref, gac_ref, up_ref, bias_ref, gn_ref,
                o_ref, qf_scr, qb_scr, khf_scr, khb_scr, khcf_scr, khcb_scr, dec_scr, decc_scr,
                o_scr, s_scr, *, n_blk):
    blk, sub, dk = GLA_BLK, GLA_SUB, GLA_DK
    nsub = blk // sub
    shift = sub.bit_length() - 1
    scale = GLA_DK ** -0.5

    ii = lax.broadcasted_iota(jnp.int32, (blk, blk), 0)
    jj = lax.broadcasted_iota(jnp.int32, (blk, blk), 1)
    same = (ii >> shift) == (jj >> shift)
    lower = jj <= ii
    tmat = jnp.where(same & lower, 1.0, 0.0).astype(BF16)
    r8 = lax.broadcasted_iota(jnp.int32, (8, dk), 0)

    u = up_ref[...]
    u_hi, u_lo = _split_bf16(u)
    urow = lax.broadcasted_iota(jnp.int32, u.shape, 0)
    rhs = jnp.where(urow >= 2 * GA_W, u_lo, u_hi)
    bias = bias_ref[...]

    def prep(ga_blk, k_blk, q_blk, v_blk):
        z = _dot(ga_blk, rhs) + bias
        a = (jnp.minimum(z, 0.0) - jnp.log(1.0 + jnp.exp(-jnp.abs(z)))) * (1.0 / GLA_GATE_NORM)
        a_hi, a_lo = _split_bf16(a)
        pre = _dot(tmat, a_hi) + _dot(tmat, a_lo)
        lasts = [pre[i * sub + sub - 1:i * sub + sub, :] for i in range(nsub)]
        tot = jnp.concatenate([jnp.broadcast_to(l, (sub, 2 * dk)) for l in lasts], axis=0)
        b_f = pre[:, :dk]
        ex_b = pre[:, dk:] - a[:, dk:]
        b_b = tot[:, dk:] - ex_b
        kf = k_blk.astype(F32)
        khf = (kf * jnp.exp(tot[:, :dk] - b_f)).astype(BF16)
        khb = (kf * jnp.exp(ex_b)).astype(BF16)

        def tile8(half):
            rows = [jnp.broadcast_to(l[:, half * dk:(half + 1) * dk], (8, dk)) for l in lasts]
            t = rows[nsub - 1]
            for i in range(nsub - 2, -1, -1):
                t = jnp.where(r8 == i, rows[i], t)
            return t

        dec = jnp.exp(jnp.concatenate([tile8(0), tile8(1), jnp.zeros((LANE - 16, dk), F32)], axis=0))
        dec_t = dec.T
        if q_blk is None:
            return khf, khb, dec_t
        qf32 = q_blk.astype(F32)
        qf = (qf32 * jnp.exp(b_f)).astype(BF16)
        qb = (qf32 * jnp.exp(b_b)).astype(BF16)
        ktf = (kf * jnp.exp(-b_f)).astype(BF16)
        ktb = (kf * jnp.exp(-b_b)).astype(BF16)
        s = jnp.where(same, jnp.where(lower, _dot_nt(qf, ktf), _dot_nt(qb, ktb)), 0.0).astype(BF16)
        return khf, khb, dec_t, qf, qb, _dot(s, v_blk)

    khf, khb, dec_t = prep(gac_ref[...], kc_ref[...], None, None)
    khcf_scr[...] = khf
    khcb_scr[...] = khb
    decc_scr[...] = dec_t

    def prep_body(t, carry):
        r = pl.multiple_of(t * blk, blk)
        khf, khb, dec_t, qf, qb, o_intra = prep(ga_ref[pl.ds(r, blk), :], k_ref[pl.ds(r, blk), :],
                                                q_ref[pl.ds(r, blk), :], v_ref[pl.ds(r, blk), :])
        khf_scr[pl.ds(r, blk), :] = khf
        khb_scr[pl.ds(r, blk), :] = khb
        qf_scr[pl.ds(r, blk), :] = qf
        qb_scr[pl.ds(r, blk), :] = qb
        o_scr[pl.ds(r, blk), :] = o_intra
        dec_scr[t] = dec_t
        return carry

    lax.fori_loop(0, n_blk, prep_body, 0)

    def scan_blk(direction, dec_t, kh_at, v_at, q_at=None, o_rows=None):
        s = s_scr[direction]
        order = range(nsub) if direction == 0 else range(nsub - 1, -1, -1)
        for i in order:
            col = dec_t[:, direction * 8 + i:direction * 8 + i + 1]
            if q_at is not None:
                rows = o_rows(i)
                o_scr[rows, :] += _dot(q_at(i), s.astype(BF16))
            s = s * col + _dot_tn(kh_at(i), v_at(i))
        s_scr[direction] = s

    s_scr[...] = jnp.zeros(s_scr.shape, F32)
    for direction, khc in ((0, khcf_scr), (1, khcb_scr)):
        scan_blk(direction, decc_scr[...],
                 lambda i, khc=khc: khc[i * sub:(i + 1) * sub, :],
                 lambda i: vc_ref[i * sub:(i + 1) * sub, :])

    def finalize(t):
        r = pl.multiple_of(t * blk, blk)
        nrm = _rms(o_scr[pl.ds(r, blk), :] * scale)
        g = g_ref[pl.ds(r, blk), :].astype(F32)
        o_ref[pl.ds(r, blk), :] = (nrm * gn_ref[...] * (g * _sigmoid(g))).astype(o_ref.dtype)

    def scan_body(t, carry, done):
        for direction, q_scr, kh_scr in ((0, qf_scr, khf_scr), (1, qb_scr, khb_scr)):
            tb = t if direction == 0 else n_blk - 1 - t

            def rows(i, tb=tb):
                return pl.ds(pl.multiple_of(tb * blk + i * sub, sub), sub)

            scan_blk(direction, dec_scr[tb],
                     lambda i, kh_scr=kh_scr, rows=rows: kh_scr[rows(i), :],
                     lambda i, rows=rows: v_ref[rows(i), :],
                     lambda i, q_scr=q_scr, rows=rows: q_scr[rows(i), :],
                     rows)
        if done:
            finalize(t)
            finalize(n_blk - 1 - t)
        return carry

    lax.fori_loop(0, n_blk // 2, functools.partial(scan_body, done=False), 0)
    lax.fori_loop(n_blk // 2, n_blk, functools.partial(scan_body, done=True), 0)


def _gla(p, ga, pc, gac, up_stack, bias2, gla_gn, *, batch, seq, ctx_len):
    assert ctx_len == GLA_BLK and (seq // GLA_BLK) % 2 == 0
    n_blk = seq // GLA_BLK
    kern = functools.partial(_gla_kernel, n_blk=n_blk)
    dk, dv = GLA_DK, GLA_DV
    return pl.pallas_call(
        kern,
        out_shape=jax.ShapeDtypeStruct((batch * seq, GLA_VW), BF16),
        grid=(batch, GLA_HEADS),
        in_specs=[
            pl.BlockSpec((seq, dk), lambda b, h: (b, _DST["gq"] // dk + h)),
            pl.BlockSpec((seq, dk), lambda b, h: (b, _DST["gk"] // dk + h)),
            pl.BlockSpec((seq, dv), lambda b, h: (b, _DST["gv"] // dv + h)),
            pl.BlockSpec((seq, dv), lambda b, h: (b, _DST["gg"] // dv + h)),
            pl.BlockSpec((seq, LANE), lambda b, h: (b, 0)),
            pl.BlockSpec((ctx_len, dk), lambda b, h: (b, _DST["gk"] // dk + h)),
            pl.BlockSpec((ctx_len, dv), lambda b, h: (b, _DST["gv"] // dv + h)),
            pl.BlockSpec((ctx_len, LANE), lambda b, h: (b, 0)),
            pl.BlockSpec((LANE, 2 * dk), lambda b, h: (0, h)),
            pl.BlockSpec((1, 2 * dk), lambda b, h: (0, h)),
            pl.BlockSpec((1, dv), lambda b, h: (0, h)),
        ],
        out_specs=pl.BlockSpec((seq, dv), lambda b, h: (b, h)),
        scratch_shapes=[
            pltpu.VMEM((seq, dk), BF16), pltpu.VMEM((seq, dk), BF16),
            pltpu.VMEM((seq, dk), BF16), pltpu.VMEM((seq, dk), BF16),
            pltpu.VMEM((ctx_len, dk), BF16), pltpu.VMEM((ctx_len, dk), BF16),
            pltpu.VMEM((n_blk, dk, LANE), F32), pltpu.VMEM((dk, LANE), F32),
            pltpu.VMEM((seq, dv), F32),
            pltpu.VMEM((2, dk, dv), F32),
        ],
        compiler_params=pltpu.CompilerParams(dimension_semantics=("arbitrary", "arbitrary"),
                                             vmem_limit_bytes=VMEM_LIMIT),
        name="gla",
    )(p, p, p, p, ga, pc, pc, gac, up_stack, bias2, gla_gn)


def _post_kernel(or_ref, og_ref, ga0_ref, ga1_ref, gb0_ref, gb1_ref, x_ref, wur_ref, wug_ref, wo_ref,
                 g1_ref, sh2_ref, sc2_ref, npost_ref, npre_ref, hs_ref, h2_ref):
    y_ret = _dot(or_ref[...], wur_ref[...])
    y_gla = _dot(og_ref[...], wug_ref[...])
    half = ga0_ref.shape[1]

    def merge(ga_r, gb_r, lo):
        return (_sigmoid(ga_r[...].astype(F32)) * y_ret[:, lo:lo + half]
                + _sigmoid(gb_r[...].astype(F32)) * y_gla[:, lo:lo + half]).astype(BF16)

    merged = jnp.concatenate([merge(ga0_ref, gb0_ref, 0), merge(ga1_ref, gb1_ref, half)], axis=1)
    y = _dot(merged, wo_ref[...])
    hs = x_ref[...] + g1_ref[...] * (_rms(y) * npost_ref[...])
    hs_ref[...] = hs
    h2 = _rms(hs) * npre_ref[...] * (1.0 + sc2_ref[...]) + sh2_ref[...]
    h2_ref[...] = h2.astype(h2_ref.dtype)


def _post(o_r, o_g, p, x2d, w_up_ret, w_up_gla, w_out, mod3, npost, npre, *, tm, seq):
    m, d = x2d.shape
    tiles_per_seq = seq // tm
    const = dict(pipeline_mode=pl.Buffered(1))
    gw = d // 2

    def modspec(chunk):
        return pl.BlockSpec((None, 1, d), lambda i: (i // tiles_per_seq, 0, chunk))

    def gatespec(name, part):
        return pl.BlockSpec((tm, gw), lambda i: (i, _DST[name] // gw + part))

    return pl.pallas_call(
        _post_kernel,
        out_shape=(jax.ShapeDtypeStruct((m, d), F32), jax.ShapeDtypeStruct((m, d), BF16)),
        grid=(m // tm,),
        in_specs=[
            pl.BlockSpec((tm, RET_VW), lambda i: (i, 0)),
            pl.BlockSpec((tm, GLA_VW), lambda i: (i, 0)),
            gatespec("gate_a", 0), gatespec("gate_a", 1), gatespec("gate_b", 0), gatespec("gate_b", 1),
            pl.BlockSpec((tm, d), lambda i: (i, 0)),
            pl.BlockSpec((RET_VW, d), lambda i: (0, 0), **const),
            pl.BlockSpec((GLA_VW, d), lambda i: (0, 0), **const),
            pl.BlockSpec((d, d), lambda i: (0, 0), **const),
            modspec(2), modspec(3), modspec(4),
            pl.BlockSpec((1, d), lambda i: (0, 0)),
            pl.BlockSpec((1, d), lambda i: (0, 0)),
        ],
        out_specs=(pl.BlockSpec((tm, d), lambda i: (i, 0)),
                   pl.BlockSpec((tm, d), lambda i: (i, 0))),
        compiler_params=pltpu.CompilerParams(dimension_semantics=("arbitrary",),
                                             vmem_limit_bytes=VMEM_LIMIT),
        name="post",
    )(o_r, o_g, p, p, p, p, x2d, w_up_ret, w_up_gla, w_out, mod3, mod3, mod3, npost, npre)


MXU_N = 256


def _ffn_up_kernel(h_ref, wg_ref, wu_ref, a_ref):
    h = h_ref[...]
    for lo in range(0, a_ref.shape[1], MXU_N):
        g = _dot(h, wg_ref[:, lo:lo + MXU_N])
        u = _dot(h, wu_ref[:, lo:lo + MXU_N])
        a_ref[:, lo:lo + MXU_N] = (g * _sigmoid(g) * u).astype(a_ref.dtype)


def _ffn_up(h2, wg, wu, *, tm, tf):
    m, d = h2.shape
    d_ff = wg.shape[1]
    return pl.pallas_call(
        _ffn_up_kernel,
        out_shape=jax.ShapeDtypeStruct((m, d_ff), BF16),
        grid=(m // tm, d_ff // tf),
        in_specs=[
            pl.BlockSpec((tm, d), lambda i, f: (i, 0)),
            pl.BlockSpec((d, tf), lambda i, f: (0, f)),
            pl.BlockSpec((d, tf), lambda i, f: (0, f)),
        ],
        out_specs=pl.BlockSpec((tm, tf), lambda i, f: (i, f)),
        compiler_params=pltpu.CompilerParams(dimension_semantics=("arbitrary", "arbitrary"),
                                             vmem_limit_bytes=VMEM_LIMIT),
        name="ffn_up",
    )(h2, wg, wu)


def _ffn_down_kernel(a_ref, wd_ref, hs_ref, g2_ref, nw_ref, o_ref, y_scr, *, n_split):
    tm, d = o_ref.shape
    w = d // n_split
    ss = jnp.zeros((tm, 1), F32)
    for lo in range(0, d, w):
        y = _dot(a_ref[...], wd_ref[:, lo:lo + w])
        y_scr[:, lo:lo + w] = y
        ss = ss + jnp.sum(y * y, axis=-1, keepdims=True)
    inv = lax.rsqrt(ss * (1.0 / d) + EPS)
    o_ref[...] = hs_ref[...] + g2_ref[...] * (y_scr[...] * inv * nw_ref[...])


def _ffn_down(a, hs, wd, mod3, npost, *, tm, seq):
    m, d = hs.shape
    d_ff = wd.shape[0]
    tiles_per_seq = seq // tm
    kern = functools.partial(_ffn_down_kernel, n_split=4)
    return pl.pallas_call(
        kern,
        out_shape=jax.ShapeDtypeStruct((m, d), F32),
        grid=(m // tm,),
        in_specs=[
            pl.BlockSpec((tm, d_ff), lambda i: (i, 0)),
            pl.BlockSpec((d_ff, d), lambda i: (0, 0), pipeline_mode=pl.Buffered(1)),
            pl.BlockSpec((tm, d), lambda i: (i, 0)),
            pl.BlockSpec((None, 1, d), lambda i: (i // tiles_per_seq, 0, 5)),
            pl.BlockSpec((1, d), lambda i: (0, 0)),
        ],
        out_specs=pl.BlockSpec((tm, d), lambda i: (i, 0)),
        scratch_shapes=[pltpu.VMEM((tm, d), F32)],
        compiler_params=pltpu.CompilerParams(dimension_semantics=("arbitrary",),
                                             vmem_limit_bytes=VMEM_LIMIT),
        name="ffn_down",
    )(a, wd, hs, mod3, npost)


def _rope_tables(seq):
    rows = seq // GRID_W
    pos_r = jnp.repeat(jnp.arange(rows, dtype=F32), GRID_W)
    pos_c = jnp.tile(jnp.arange(GRID_W, dtype=F32), rows)
    n_f = RET_DK // 4
    inv = ROPE_BASE ** (-jnp.arange(n_f, dtype=F32) / n_f)
    ang = jnp.concatenate([pos_r[:, None] * inv, pos_c[:, None] * inv], axis=-1)
    return jnp.cos(ang), jnp.sin(ang)


def _gate_map_layout(gla_a_up, gla_a_bias):
    r, h, dk = GLA_LOW_RANK, GLA_HEADS, GLA_DK
    u = jnp.zeros((GA_W, h, 2, dk), F32)
    u = u.at[:r, :, 0, :].set(gla_a_up[0].reshape(r, h, dk))
    u = u.at[r:, :, 1, :].set(gla_a_up[1].reshape(r, h, dk))
    u = u.reshape(GA_W, h * 2 * dk)
    up_stack = jnp.concatenate([u, u, u, jnp.zeros((LANE - 3 * GA_W, h * 2 * dk), F32)], axis=0)
    bias2 = jnp.stack([gla_a_bias[0].reshape(h, dk), gla_a_bias[1].reshape(h, dk)], axis=1)
    return up_stack, bias2.reshape(1, h * 2 * dk)


def _layer(h_state, c_rows, ctx2d, w_mod, b_mod, norm_mix_pre, norm_mix_post, norm_ffn_pre, norm_ffn_post,
           w_in, ret_decay, gla_a_up, gla_a_bias, ret_gn, gla_gn, w_up_ret, w_up_gla, w_out,
           ffn_w_gate, ffn_w_up, ffn_w_down, cos, sin, *, batch, seq, ctx_len):
    d = D_MODEL
    ga0 = _SRC["ga"][0]
    w_t = w_in.T
    w_main = _pack_w_in(w_t)
    w_g = w_t[ga0:ga0 + GA_W]
    w_ga = jnp.concatenate([w_g, w_g, w_g, jnp.zeros((LANE - 3 * GA_W, d), F32)], axis=0).astype(BF16)
    up_stack, bias2 = _gate_map_layout(gla_a_up, gla_a_bias)
    rd_b = jnp.broadcast_to(ret_decay.reshape(2, RET_HEADS, 1, 1), (2, RET_HEADS, 8, RET_DK))

    mod = _modulation(c_rows, w_mod, b_mod)
    mod3 = mod.reshape(mod.shape[0], 1, 6 * d)
    nw_pre = norm_mix_pre.reshape(1, d)

    tm = 1024
    p, ga = _inproj(h_state, nw_pre, mod3, lambda i: i // (seq // tm), w_main, w_ga, cos, sin,
                    tm=tm, tn=1024, ncols=N_MAIN, rope=True, seq_tiles=seq // tm)
    pc, gac = _inproj(ctx2d, nw_pre, mod3, lambda i: batch, w_main, w_ga, cos, sin,
                      tm=batch * ctx_len, tn=512, ncols=N_CTX, rope=False, seq_tiles=1)

    o_r = _retention(p, pc, rd_b, ret_gn.reshape(1, RET_VW), batch=batch, seq=seq, ctx_len=ctx_len)
    o_g = _gla(p, ga, pc, gac, up_stack, bias2, gla_gn.reshape(1, GLA_VW), batch=batch, seq=seq, ctx_len=ctx_len)

    hs, h2 = _post(o_r, o_g, p, h_state, w_up_ret.astype(BF16), w_up_gla.astype(BF16), w_out.astype(BF16),
                   mod3, norm_mix_post.reshape(1, d), norm_ffn_pre.reshape(1, d), tm=256, seq=seq)
    act = _ffn_up(h2, ffn_w_gate.astype(BF16), ffn_w_up.astype(BF16), tm=1024, tf=512)
    return _ffn_down(act, hs, ffn_w_down.astype(BF16), mod3, norm_ffn_post.reshape(1, d), tm=256, seq=seq)


def kernel(x, c, ctx, c_ctx, w_mod, b_mod, norm_mix_pre, norm_mix_post, norm_ffn_pre, norm_ffn_post,
           w_in, ret_decay, gla_a_up, gla_a_bias, ret_gn, gla_gn, w_up_ret, w_up_gla, w_out,
           ffn_w_gate, ffn_w_up, ffn_w_down):
    batch, seq, d = x.shape
    ctx_len = ctx.shape[1]
    depth = w_mod.shape[0]
    cos, sin = _rope_tables(seq)
    c_rows = jnp.zeros((8, d), F32).at[:batch].set(c.astype(F32)).at[batch].set(c_ctx.astype(F32))
    ctx2d = ctx.astype(F32).reshape(batch * ctx_len, d)
    h_state = x.astype(F32).reshape(batch * seq, d)
    for i in range(depth):
        h_state = _layer(h_state, c_rows, ctx2d, w_mod[i], b_mod[i], norm_mix_pre[i], norm_mix_post[i],
                         norm_ffn_pre[i], norm_ffn_post[i], w_in[i], ret_decay[i], gla_a_up[i], gla_a_bias[i],
                         ret_gn[i], gla_gn[i], w_up_ret[i], w_up_gla[i], w_out[i],
                         ffn_w_gate[i], ffn_w_up[i], ffn_w_down[i], cos, sin,
                         batch=batch, seq=seq, ctx_len=ctx_len)
    return h_state.reshape(batch, seq, d).astype(x.dtype)
```

```python
import functools

import jax
import jax.numpy as jnp
from jax import lax
from jax.experimental import pallas as pl
from jax.experimental.pallas import tpu as pltpu

F32 = jnp.float32
BF16 = jnp.bfloat16

D_MODEL = 2048
GRID_W = 64
RET_HEADS = 4
RET_DK = 256
RET_DV = 256
GLA_HEADS = 4
GLA_DK = 128
GLA_DV = 256
GLA_LOW_RANK = 16
GLA_GATE_NORM = 16.0
ROPE_BASE = 10000.0
EPS = 1e-6

RET_QK = RET_HEADS * RET_DK
RET_VW = RET_HEADS * RET_DV
GLA_KW = GLA_HEADS * GLA_DK
GLA_VW = GLA_HEADS * GLA_DV
GA_W = 2 * GLA_LOW_RANK
LANE = 128

_SRC = {}
_off = 0
for _name, _w in (("rk", RET_QK), ("rv", RET_VW), ("gk", GLA_KW), ("gv", GLA_VW), ("ga", GA_W),
                  ("rq", RET_QK), ("rg", RET_VW), ("gq", GLA_KW), ("gg", GLA_VW),
                  ("gate_a", D_MODEL), ("gate_b", D_MODEL)):
    _SRC[_name] = (_off, _w)
    _off += _w

_ORDER = ("rk", "rv", "gv", "gk", "gq", "rq", "rg", "gg", "gate_a", "gate_b")
_DST = {}
_off = 0
for _name in _ORDER:
    _DST[_name] = _off
    _off += _SRC[_name][1]
N_MAIN = _off
N_CTX = _DST["gq"]
PACK_W = 512

VMEM_LIMIT = 56 * 1024 * 1024


def _dot(a, b):
    return jnp.dot(a, b, preferred_element_type=F32)


def _dot_nt(a, b):
    return lax.dot_general(a, b, (((1,), (1,)), ((), ())), preferred_element_type=F32)


def _dot_tn(a, b):
    return lax.dot_general(a, b, (((0,), (0,)), ((), ())), preferred_element_type=F32)


def _sigmoid(x):
    return 1.0 / (1.0 + jnp.exp(-x))


def _rms(x):
    return x * lax.rsqrt(jnp.mean(x * x, axis=-1, keepdims=True) + EPS)


def _split_bf16(x):
    hi = x.astype(BF16)
    lo = (x - hi.astype(F32)).astype(BF16)
    return hi, lo


def _mod_kernel(c_ref, w_ref, b_ref, o_ref):
    cf = c_ref[...]
    s = cf * _sigmoid(cf)
    o_ref[...] = jnp.dot(s, w_ref[...], preferred_element_type=F32,
                         precision=lax.Precision.HIGHEST) + b_ref[...]


def _modulation(c_rows, w_mod, b_mod):
    rows, d = c_rows.shape
    n = w_mod.shape[1]
    tn = 1024
    return pl.pallas_call(
        _mod_kernel,
        out_shape=jax.ShapeDtypeStruct((rows, n), F32),
        grid=(n // tn,),
        in_specs=[pl.BlockSpec((rows, d), lambda j: (0, 0)),
                  pl.BlockSpec((d, tn), lambda j: (0, j)),
                  pl.BlockSpec((1, tn), lambda j: (0, j))],
        out_specs=pl.BlockSpec((rows, tn), lambda j: (0, j)),
        compiler_params=pltpu.CompilerParams(dimension_semantics=("arbitrary",),
                                             vmem_limit_bytes=VMEM_LIMIT),
        name="mod",
    )(c_rows, w_mod, b_mod.reshape(1, n))


def _pack_kernel(a_idx_ref, n_idx_ref, shift_ref, a_ref, n_ref, o_ref):
    j = pl.program_id(0)

    @pl.when(shift_ref[j] == 0)
    def _():
        o_ref[...] = a_ref[...].astype(o_ref.dtype)

    @pl.when(shift_ref[j] != 0)
    def _():
        o_ref[...] = jnp.concatenate([a_ref[GA_W:, :], n_ref[...]], axis=0).astype(o_ref.dtype)


def _pack_w_in(w_t):
    d = w_t.shape[1]
    a_idx, n_idx, shift = [], [], []
    for name in _ORDER:
        src, width = _SRC[name]
        for c in range(src, src + width, PACK_W):
            base = c - c % PACK_W
            assert c - base in (0, GA_W)
            a_idx.append(base // PACK_W)
            n_idx.append((base + PACK_W) // GA_W if c != base else (n_idx[-1] if n_idx else 0))
            shift.append(c - base)
    n_tiles = len(a_idx)
    as_i32 = lambda v: jnp.asarray(v, jnp.int32)
    return pl.pallas_call(
        _pack_kernel,
        out_shape=jax.ShapeDtypeStruct((n_tiles * PACK_W, d), BF16),
        grid_spec=pltpu.PrefetchScalarGridSpec(
            num_scalar_prefetch=3,
            grid=(n_tiles,),
            in_specs=[pl.BlockSpec((PACK_W, d), lambda j, a, n, s: (a[j], 0)),
                      pl.BlockSpec((GA_W, d), lambda j, a, n, s: (n[j], 0))],
            out_specs=pl.BlockSpec((PACK_W, d), lambda j, a, n, s: (j, 0)),
        ),
        compiler_params=pltpu.CompilerParams(dimension_semantics=("arbitrary",),
                                             vmem_limit_bytes=VMEM_LIMIT),
        name="pack_w_in",
    )(as_i32(a_idx), as_i32(n_idx), as_i32(shift), w_t, w_t)


def _inproj_kernel(x_ref, nw_ref, sh_ref, sc_ref, w_ref, wga_ref, cos_ref, sin_ref,
                   o_ref, ga_ref, h_scr, *, rope_tiles, tn):
    j = pl.program_id(1)

    @pl.when(j == 0)
    def _():
        h = _rms(x_ref[...]) * nw_ref[...] * (1.0 + sc_ref[...]) + sh_ref[...]
        hb = h.astype(BF16)
        h_scr[...] = hb
        g3 = _dot_nt(hb, wga_ref[...])
        lane = lax.broadcasted_iota(jnp.int32, g3.shape, 1)
        resid = g3 - g3.astype(BF16).astype(F32)
        ga_ref[...] = jnp.where((lane >= GA_W) & (lane < 2 * GA_W), resid, g3).astype(ga_ref.dtype)

    if not rope_tiles:
        o_ref[...] = _dot_nt(h_scr[...], w_ref[...]).astype(o_ref.dtype)
        return

    is_rope = functools.reduce(jnp.logical_or, [j == t for t in rope_tiles])

    @pl.when(is_rope)
    def _():
        cos = cos_ref[...]
        sin = sin_ref[...]
        half = RET_DK // 2
        for hd in range(0, tn, RET_DK):
            acc = _dot_nt(h_scr[...], w_ref[hd:hd + RET_DK, :])
            t1 = acc[:, :half]
            t2 = acc[:, half:]
            o_ref[:, hd:hd + half] = (t1 * cos - t2 * sin).astype(o_ref.dtype)
            o_ref[:, hd + half:hd + RET_DK] = (t1 * sin + t2 * cos).astype(o_ref.dtype)

    @pl.when(jnp.logical_not(is_rope))
    def _():
        o_ref[...] = _dot_nt(h_scr[...], w_ref[...]).astype(o_ref.dtype)


def _inproj(x2d, norm_w, mod3, mod_row_of_tile, w_main, w_ga, cos, sin, *, tm, tn, ncols, rope, seq_tiles):
    m, d = x2d.shape
    rope_tiles = ()
    if rope:
        assert tn == RET_QK and _DST["rk"] % tn == 0 and _DST["rq"] % tn == 0
        rope_tiles = (_DST["rk"] // tn, _DST["rq"] // tn)
    kern = functools.partial(_inproj_kernel, rope_tiles=rope_tiles, tn=tn)
    return pl.pallas_call(
        kern,
        out_shape=(jax.ShapeDtypeStruct((m, ncols), BF16),
                   jax.ShapeDtypeStruct((m, LANE), BF16)),
        grid=(m // tm, ncols // tn),
        in_specs=[
            pl.BlockSpec((tm, d), lambda i, j: (i, 0)),
            pl.BlockSpec((1, d), lambda i, j: (0, 0)),
            pl.BlockSpec((None, 1, d), lambda i, j: (mod_row_of_tile(i), 0, 0)),
            pl.BlockSpec((None, 1, d), lambda i, j: (mod_row_of_tile(i), 0, 1)),
            pl.BlockSpec((tn, d), lambda i, j: (j, 0)),
            pl.BlockSpec((LANE, d), lambda i, j: (0, 0)),
            pl.BlockSpec((tm, RET_DK // 2), lambda i, j: (i % seq_tiles, 0)),
            pl.BlockSpec((tm, RET_DK // 2), lambda i, j: (i % seq_tiles, 0)),
        ],
        out_specs=(pl.BlockSpec((tm, tn), lambda i, j: (i, j)),
                   pl.BlockSpec((tm, LANE), lambda i, j: (i, 0))),
        scratch_shapes=[pltpu.VMEM((tm, d), BF16)],
        compiler_params=pltpu.CompilerParams(dimension_semantics=("arbitrary", "arbitrary"),
                                             vmem_limit_bytes=VMEM_LIMIT),
        name="inproj",
    )(x2d, norm_w, mod3, mod3, w_main, w_ga, cos, sin)


RET_CHUNK = 256


def _ret_kernel(q_ref, k_ref, v_ref, g_ref, kc_ref, vc_ref, rd_ref, gn_ref, o_ref,
                o_scr, s_scr, dm_scr, eq_scr, wk_scr, sd_scr, *, n_chunks):
    c_len = RET_CHUNK
    ii = lax.broadcasted_iota(jnp.int32, (c_len, c_len), 0)
    jj = lax.broadcasted_iota(jnp.int32, (c_len, c_len), 1)
    rowi = lax.broadcasted_iota(jnp.int32, (c_len, RET_DK), 0).astype(F32)
    scale = RET_DK ** -0.5

    for direction in (0, 1):
        lg = -jnp.exp(rd_ref[direction][0:1, :])
        if direction == 0:
            dm_scr[direction] = jnp.where(jj <= ii, jnp.exp((ii - jj).astype(F32) * lg), 0.0)
            eq_scr[direction] = jnp.exp((rowi + 1.0) * lg)
            wk = jnp.exp((c_len - 1.0 - rowi) * lg)
        else:
            dm_scr[direction] = jnp.where(jj > ii, jnp.exp((jj - ii).astype(F32) * lg), 0.0)
            eq_scr[direction] = jnp.exp((c_len - rowi) * lg)
            wk = jnp.exp(rowi * lg)
        wk_scr[direction] = wk
        sd_scr[direction] = jnp.broadcast_to(jnp.exp(float(c_len) * lg), (8, RET_DV))
        kcw = (kc_ref[...].astype(F32) * wk).astype(BF16)
        s_scr[direction] = _dot_tn(kcw, vc_ref[...])

    def chunk(direction, c, first):
        r0 = pl.multiple_of(c * c_len, c_len)
        q = q_ref[pl.ds(r0, c_len), :]
        k = k_ref[pl.ds(r0, c_len), :]
        v = v_ref[pl.ds(r0, c_len), :]
        kv = _dot_tn((k.astype(F32) * wk_scr[direction]).astype(BF16), v)
        s_state = s_scr[direction]
        sc = (_dot_nt(q, k) * dm_scr[direction]).astype(BF16)
        o = _dot(sc, v) + eq_scr[direction] * _dot(q, s_state.astype(BF16))
        s_scr[direction] = sd_scr[direction][0:1, :] * s_state + kv
        if first:
            o_scr[pl.ds(r0, c_len), :] = o
        else:
            tot = (o_scr[pl.ds(r0, c_len), :] + o) * scale
            mu = jnp.mean(tot, axis=-1, keepdims=True)
            cen = tot - mu
            var = jnp.mean(cen * cen, axis=-1, keepdims=True)
            nrm = cen * lax.rsqrt(var + EPS)
            g = g_ref[pl.ds(r0, c_len), :].astype(F32)
            o_ref[pl.ds(r0, c_len), :] = (nrm * gn_ref[...] * (g * _sigmoid(g))).astype(o_ref.dtype)

    def body(t, carry, first):
        chunk(0, t, first)
        chunk(1, n_chunks - 1 - t, first)
        return carry

    lax.fori_loop(0, n_chunks // 2, functools.partial(body, first=True), 0)
    lax.fori_loop(n_chunks // 2, n_chunks, functools.partial(body, first=False), 0)


def _retention(p, pc, rd_b, ret_gn, *, batch, seq, ctx_len):
    assert ctx_len == RET_CHUNK and (seq // RET_CHUNK) % 2 == 0
    n_chunks = seq // RET_CHUNK
    w = RET_DK
    kern = functools.partial(_ret_kernel, n_chunks=n_chunks)

    def col(name):
        return _DST[name] // w

    return pl.pallas_call(
        kern,
        out_shape=jax.ShapeDtypeStruct((batch * seq, RET_VW), BF16),
        grid=(batch, RET_HEADS),
        in_specs=[
            pl.BlockSpec((seq, w), lambda b, h: (b, col("rq") + h)),
            pl.BlockSpec((seq, w), lambda b, h: (b, col("rk") + h)),
            pl.BlockSpec((seq, w), lambda b, h: (b, col("rv") + h)),
            pl.BlockSpec((seq, w), lambda b, h: (b, col("rg") + h)),
            pl.BlockSpec((ctx_len, w), lambda b, h: (b, col("rk") + h)),
            pl.BlockSpec((ctx_len, w), lambda b, h: (b, col("rv") + h)),
            pl.BlockSpec((2, None, 8, w), lambda b, h: (0, h, 0, 0)),
            pl.BlockSpec((1, w), lambda b, h: (0, h)),
        ],
        out_specs=pl.BlockSpec((seq, w), lambda b, h: (b, h)),
        scratch_shapes=[
            pltpu.VMEM((seq, RET_DV), F32),
            pltpu.VMEM((2, RET_DK, RET_DV), F32),
            pltpu.VMEM((2, RET_CHUNK, RET_CHUNK), F32),
            pltpu.VMEM((2, RET_CHUNK, RET_DV), F32),
            pltpu.VMEM((2, RET_CHUNK, RET_DK), F32),
            pltpu.VMEM((2, 8, RET_DV), F32),
        ],
        compiler_params=pltpu.CompilerParams(dimension_semantics=("arbitrary", "arbitrary"),
                                             vmem_limit_bytes=VMEM_LIMIT),
        name="retention",
    )(p, p, p, p, pc, pc, rd_b, ret_gn)


GLA_SUB = 64
GLA_BLK = 256


def _gla_kernel(q_ref, k_ref, v_ref, g_ref, ga_ref, kc_ref, vc_ref, gac_ref, up_ref, bias_ref, gn_ref,
                o_ref, qf_scr, qb_scr, khf_scr, khb_scr, khcf_scr, khcb_scr, dec_scr, decc_scr,
                o_scr, s_scr, *, n_blk):
    blk, sub, dk = GLA_BLK, GLA_SUB, GLA_DK
    nsub = blk // sub
    shift = sub.bit_length() - 1
    scale = GLA_DK ** -0.5

    ii = lax.broadcasted_iota(jnp.int32, (blk, blk), 0)
    jj = lax.broadcasted_iota(jnp.int32, (blk, blk), 1)
    same = (ii >> shift) == (jj >> shift)
    lower = jj <= ii
    tmat = jnp.where(same & lower, 1.0, 0.0).astype(BF16)
    r8 = lax.broadcasted_iota(jnp.int32, (8, dk), 0)

    u = up_ref[...]
    u_hi, u_lo = _split_bf16(u)
    urow = lax.broadcasted_iota(jnp.int32, u.shape, 0)
    rhs = jnp.where(urow >= 2 * GA_W, u_lo, u_hi)
    bias = bias_ref[...]

    def prep(ga_blk, k_blk, q_blk, v_blk):
        z = _dot(ga_blk, rhs) + bias
        a = (jnp.minimum(z, 0.0) - jnp.log(1.0 + jnp.exp(-jnp.abs(z)))) * (1.0 / GLA_GATE_NORM)
        a_hi, a_lo = _split_bf16(a)
        pre = _dot(tmat, a_hi) + _dot(tmat, a_lo)
        lasts = [pre[i * sub + sub - 1:i * sub + sub, :] for i in range(nsub)]
        tot = jnp.concatenate([jnp.broadcast_to(l, (sub, 2 * dk)) for l in lasts], axis=0)
        b_f = pre[:, :dk]
        ex_b = pre[:, dk:] - a[:, dk:]
        b_b = tot[:, dk:] - ex_b
        kf = k_blk.astype(F32)
        khf = (kf * jnp.exp(tot[:, :dk] - b_f)).astype(BF16)
        khb = (kf * jnp.exp(ex_b)).astype(BF16)

        def tile8(half):
            rows = [jnp.broadcast_to(l[:, half * dk:(half + 1) * dk], (8, dk)) for l in lasts]
            t = rows[nsub - 1]
            for i in range(nsub - 2, -1, -1):
                t = jnp.where(r8 == i, rows[i], t)
            return t

        dec = jnp.exp(jnp.concatenate([tile8(0), tile8(1), jnp.zeros((LANE - 16, dk), F32)], axis=0))
        dec_t = dec.T
        if q_blk is None:
            return khf, khb, dec_t
        qf32 = q_blk.astype(F32)
        qf = (qf32 * jnp.exp(b_f)).astype(BF16)
        qb = (qf32 * jnp.exp(b_b)).astype(BF16)
        ktf = (kf * jnp.exp(-b_f)).astype(BF16)
        ktb = (kf * jnp.exp(-b_b)).astype(BF16)
        s = jnp.where(same, jnp.where(lower, _dot_nt(qf, ktf), _dot_nt(qb, ktb)), 0.0).astype(BF16)
        return khf, khb, dec_t, qf, qb, _dot(s, v_blk)

    khf, khb, dec_t = prep(gac_ref[...], kc_ref[...], None, None)
    khcf_scr[...] = khf
    khcb_scr[...] = khb
    decc_scr[...] = dec_t

    def prep_body(t, carry):
        r = pl.multiple_of(t * blk, blk)
        khf, khb, dec_t, qf, qb, o_intra = prep(ga_ref[pl.ds(r, blk), :], k_ref[pl.ds(r, blk), :],
                                                q_ref[pl.ds(r, blk), :], v_ref[pl.ds(r, blk), :])
        khf_scr[pl.ds(r, blk), :] = khf
        khb_scr[pl.ds(r, blk), :] = khb
        qf_scr[pl.ds(r, blk), :] = qf
        qb_scr[pl.ds(r, blk), :] = qb
        o_scr[pl.ds(r, blk), :] = o_intra
        dec_scr[t] = dec_t
        return carry

    lax.fori_loop(0, n_blk, prep_body, 0, unroll=4)

    def scan_blk(direction, dec_t, kh_at, v_at, q_at=None, o_rows=None):
        s = s_scr[direction]
        order = range(nsub) if direction == 0 else range(nsub - 1, -1, -1)
        kv = {i: _dot_tn(kh_at(i), v_at(i)) for i in order}
        for i in order:
            col = dec_t[:, direction * 8 + i:direction * 8 + i + 1]
            if q_at is not None:
                rows = o_rows(i)
                o_scr[rows, :] += _dot(q_at(i), s.astype(BF16))
            s = s * col + kv[i]
        s_scr[direction] = s

    s_scr[...] = jnp.zeros(s_scr.shape, F32)
    for direction, khc in ((0, khcf_scr), (1, khcb_scr)):
        scan_blk(direction, decc_scr[...],
                 lambda i, khc=khc: khc[i * sub:(i + 1) * sub, :],
                 lambda i: vc_ref[i * sub:(i + 1) * sub, :])

    def finalize(t):
        r = pl.multiple_of(t * blk, blk)
        nrm = _rms(o_scr[pl.ds(r, blk), :] * scale)
        g = g_ref[pl.ds(r, blk), :].astype(F32)
        o_ref[pl.ds(r, blk), :] = (nrm * gn_ref[...] * (g * _sigmoid(g))).astype(o_ref.dtype)

    def scan_body(t, carry, done):
        for direction, q_scr, kh_scr in ((0, qf_scr, khf_scr), (1, qb_scr, khb_scr)):
            tb = t if direction == 0 else n_blk - 1 - t

            def rows(i, tb=tb):
                return pl.ds(pl.multiple_of(tb * blk + i * sub, sub), sub)

            scan_blk(direction, dec_scr[tb],
                     lambda i, kh_scr=kh_scr, rows=rows: kh_scr[rows(i), :],
                     lambda i, rows=rows: v_ref[rows(i), :],
                     lambda i, q_scr=q_scr, rows=rows: q_scr[rows(i), :],
                     rows)
        if done:
            finalize(t)
            finalize(n_blk - 1 - t)
        return carry

    lax.fori_loop(0, n_blk // 2, functools.partial(scan_body, done=False), 0, unroll=2)
    lax.fori_loop(n_blk // 2, n_blk, functools.partial(scan_body, done=True), 0, unroll=2)


def _gla(p, ga, pc, gac, up_stack, bias2, gla_gn, *, batch, seq, ctx_len):
    assert ctx_len == GLA_BLK and (seq // GLA_BLK) % 2 == 0
    n_blk = seq // GLA_BLK
    kern = functools.partial(_gla_kernel, n_blk=n_blk)
    dk, dv = GLA_DK, GLA_DV
    return pl.pallas_call(
        kern,
        out_shape=jax.ShapeDtypeStruct((batch * seq, GLA_VW), BF16),
        grid=(batch, GLA_HEADS),
        in_specs=[
            pl.BlockSpec((seq, dk), lambda b, h: (b, _DST["gq"] // dk + h)),
            pl.BlockSpec((seq, dk), lambda b, h: (b, _DST["gk"] // dk + h)),
            pl.BlockSpec((seq, dv), lambda b, h: (b, _DST["gv"] // dv + h)),
            pl.BlockSpec((seq, dv), lambda b, h: (b, _DST["gg"] // dv + h)),
            pl.BlockSpec((seq, LANE), lambda b, h: (b, 0)),
            pl.BlockSpec((ctx_len, dk), lambda b, h: (b, _DST["gk"] // dk + h)),
            pl.BlockSpec((ctx_len, dv), lambda b, h: (b, _DST["gv"] // dv + h)),
            pl.BlockSpec((ctx_len, LANE), lambda b, h: (b, 0)),
            pl.BlockSpec((LANE, 2 * dk), lambda b, h: (0, h)),
            pl.BlockSpec((1, 2 * dk), lambda b, h: (0, h)),
            pl.BlockSpec((1, dv), lambda b, h: (0, h)),
        ],
        out_specs=pl.BlockSpec((seq, dv), lambda b, h: (b, h)),
        scratch_shapes=[
            pltpu.VMEM((seq, dk), BF16), pltpu.VMEM((seq, dk), BF16),
            pltpu.VMEM((seq, dk), BF16), pltpu.VMEM((seq, dk), BF16),
            pltpu.VMEM((ctx_len, dk), BF16), pltpu.VMEM((ctx_len, dk), BF16),
            pltpu.VMEM((n_blk, dk, LANE), F32), pltpu.VMEM((dk, LANE), F32),
            pltpu.VMEM((seq, dv), F32),
            pltpu.VMEM((2, dk, dv), F32),
        ],
        compiler_params=pltpu.CompilerParams(dimension_semantics=("arbitrary", "arbitrary"),
                                             vmem_limit_bytes=VMEM_LIMIT),
        name="gla",
    )(p, p, p, p, ga, pc, pc, gac, up_stack, bias2, gla_gn)


def _post_kernel(or_ref, og_ref, ga0_ref, ga1_ref, gb0_ref, gb1_ref, x_ref, wur_ref, wug_ref, wo_ref,
                 g1_ref, sh2_ref, sc2_ref, npost_ref, npre_ref, hs_ref, h2_ref):
    y_ret = _dot(or_ref[...], wur_ref[...])
    y_gla = _dot(og_ref[...], wug_ref[...])
    half = ga0_ref.shape[1]

    def merge(ga_r, gb_r, lo):
        return (_sigmoid(ga_r[...].astype(F32)) * y_ret[:, lo:lo + half]
                + _sigmoid(gb_r[...].astype(F32)) * y_gla[:, lo:lo + half]).astype(BF16)

    merged = jnp.concatenate([merge(ga0_ref, gb0_ref, 0), merge(ga1_ref, gb1_ref, half)], axis=1)
    y = _dot(merged, wo_ref[...])
    hs = x_ref[...] + g1_ref[...] * (_rms(y) * npost_ref[...])
    hs_ref[...] = hs
    h2 = _rms(hs) * npre_ref[...] * (1.0 + sc2_ref[...]) + sh2_ref[...]
    h2_ref[...] = h2.astype(h2_ref.dtype)


def _post(o_r, o_g, p, x2d, w_up_ret, w_up_gla, w_out, mod3, npost, npre, *, tm, seq):
    m, d = x2d.shape
    tiles_per_seq = seq // tm
    const = dict(pipeline_mode=pl.Buffered(1))
    gw = d // 2

    def modspec(chunk):
        return pl.BlockSpec((None, 1, d), lambda i: (i // tiles_per_seq, 0, chunk))

    def gatespec(name, part):
        return pl.BlockSpec((tm, gw), lambda i: (i, _DST[name] // gw + part))

    return pl.pallas_call(
        _post_kernel,
        out_shape=(jax.ShapeDtypeStruct((m, d), F32), jax.ShapeDtypeStruct((m, d), BF16)),
        grid=(m // tm,),
        in_specs=[
            pl.BlockSpec((tm, RET_VW), lambda i: (i, 0)),
            pl.BlockSpec((tm, GLA_VW), lambda i: (i, 0)),
            gatespec("gate_a", 0), gatespec("gate_a", 1), gatespec("gate_b", 0), gatespec("gate_b", 1),
            pl.BlockSpec((tm, d), lambda i: (i, 0)),
            pl.BlockSpec((RET_VW, d), lambda i: (0, 0), **const),
            pl.BlockSpec((GLA_VW, d), lambda i: (0, 0), **const),
            pl.BlockSpec((d, d), lambda i: (0, 0), **const),
            modspec(2), modspec(3), modspec(4),
            pl.BlockSpec((1, d), lambda i: (0, 0)),
            pl.BlockSpec((1, d), lambda i: (0, 0)),
        ],
        out_specs=(pl.BlockSpec((tm, d), lambda i: (i, 0)),
                   pl.BlockSpec((tm, d), lambda i: (i, 0))),
        compiler_params=pltpu.CompilerParams(dimension_semantics=("arbitrary",),
                                             vmem_limit_bytes=VMEM_LIMIT),
        name="post",
    )(o_r, o_g, p, p, p, p, x2d, w_up_ret, w_up_gla, w_out, mod3, mod3, mod3, npost, npre)


MXU_N = 256


def _ffn_up_kernel(h_ref, wg_ref, wu_ref, a_ref):
    h = h_ref[...]
    for lo in range(0, a_ref.shape[1], MXU_N):
        g = _dot(h, wg_ref[:, lo:lo + MXU_N])
        u = _dot(h, wu_ref[:, lo:lo + MXU_N])
        a_ref[:, lo:lo + MXU_N] = (g * _sigmoid(g) * u).astype(a_ref.dtype)


def _ffn_up(h2, wg, wu, *, tm, tf):
    m, d = h2.shape
    d_ff = wg.shape[1]
    return pl.pallas_call(
        _ffn_up_kernel,
        out_shape=jax.ShapeDtypeStruct((m, d_ff), BF16),
        grid=(m // tm, d_ff // tf),
        in_specs=[
            pl.BlockSpec((tm, d), lambda i, f: (i, 0)),
            pl.BlockSpec((d, tf), lambda i, f: (0, f)),
            pl.BlockSpec((d, tf), lambda i, f: (0, f)),
        ],
        out_specs=pl.BlockSpec((tm, tf), lambda i, f: (i, f)),
        compiler_params=pltpu.CompilerParams(dimension_semantics=("arbitrary", "arbitrary"),
                                             vmem_limit_bytes=VMEM_LIMIT),
        name="ffn_up",
    )(h2, wg, wu)


def _ffn_down_kernel(a_ref, wd_ref, hs_ref, g2_ref, nw_ref, o_ref, y_scr, *, n_split):
    tm, d = o_ref.shape
    w = d // n_split
    ss = jnp.zeros((tm, 1), F32)
    for lo in range(0, d, w):
        y = _dot(a_ref[...], wd_ref[:, lo:lo + w])
        y_scr[:, lo:lo + w] = y
        ss = ss + jnp.sum(y * y, axis=-1, keepdims=True)
    inv = lax.rsqrt(ss * (1.0 / d) + EPS)
    o_ref[...] = hs_ref[...] + g2_ref[...] * (y_scr[...] * inv * nw_ref[...])


def _ffn_down(a, hs, wd, mod3, npost, *, tm, seq):
    m, d = hs.shape
    d_ff = wd.shape[0]
    tiles_per_seq = seq // tm
    kern = functools.partial(_ffn_down_kernel, n_split=4)
    return pl.pallas_call(
        kern,
        out_shape=jax.ShapeDtypeStruct((m, d), F32),
        grid=(m // tm,),
        in_specs=[
            pl.BlockSpec((tm, d_ff), lambda i: (i, 0)),
            pl.BlockSpec((d_ff, d), lambda i: (0, 0), pipeline_mode=pl.Buffered(1)),
            pl.BlockSpec((tm, d), lambda i: (i, 0)),
            pl.BlockSpec((None, 1, d), lambda i: (i // tiles_per_seq, 0, 5)),
            pl.BlockSpec((1, d), lambda i: (0, 0)),
        ],
        out_specs=pl.BlockSpec((tm, d), lambda i: (i, 0)),
        scratch_shapes=[pltpu.VMEM((tm, d), F32)],
        compiler_params=pltpu.CompilerParams(dimension_semantics=("arbitrary",),
                                             vmem_limit_bytes=VMEM_LIMIT),
        name="ffn_down",
    )(a, wd, hs, mod3, npost)


def _rope_tables(seq):
    rows = seq // GRID_W
    pos_r = jnp.repeat(jnp.arange(rows, dtype=F32), GRID_W)
    pos_c = jnp.tile(jnp.arange(GRID_W, dtype=F32), rows)
    n_f = RET_DK // 4
    inv = ROPE_BASE ** (-jnp.arange(n_f, dtype=F32) / n_f)
    ang = jnp.concatenate([pos_r[:, None] * inv, pos_c[:, None] * inv], axis=-1)
    return jnp.cos(ang), jnp.sin(ang)


def _gate_map_layout(gla_a_up, gla_a_bias):
    r, h, dk = GLA_LOW_RANK, GLA_HEADS, GLA_DK
    u = jnp.zeros((GA_W, h, 2, dk), F32)
    u = u.at[:r, :, 0, :].set(gla_a_up[0].reshape(r, h, dk))
    u = u.at[r:, :, 1, :].set(gla_a_up[1].reshape(r, h, dk))
    u = u.reshape(GA_W, h * 2 * dk)
    up_stack = jnp.concatenate([u, u, u, jnp.zeros((LANE - 3 * GA_W, h * 2 * dk), F32)], axis=0)
    bias2 = jnp.stack([gla_a_bias[0].reshape(h, dk), gla_a_bias[1].reshape(h, dk)], axis=1)
    return up_stack, bias2.reshape(1, h * 2 * dk)


def _layer(h_state, c_rows, ctx2d, w_mod, b_mod, norm_mix_pre, norm_mix_post, norm_ffn_pre, norm_ffn_post,
           w_in, ret_decay, gla_a_up, gla_a_bias, ret_gn, gla_gn, w_up_ret, w_up_gla, w_out,
           ffn_w_gate, ffn_w_up, ffn_w_down, cos, sin, *, batch, seq, ctx_len):
    d = D_MODEL
    ga0 = _SRC["ga"][0]
    w_t = w_in.T
    w_main = _pack_w_in(w_t)
    w_g = w_t[ga0:ga0 + GA_W]
    w_ga = jnp.concatenate([w_g, w_g, w_g, jnp.zeros((LANE - 3 * GA_W, d), F32)], axis=0).astype(BF16)
    up_stack, bias2 = _gate_map_layout(gla_a_up, gla_a_bias)
    rd_b = jnp.broadcast_to(ret_decay.reshape(2, RET_HEADS, 1, 1), (2, RET_HEADS, 8, RET_DK))

    mod = _modulation(c_rows, w_mod, b_mod)
    mod3 = mod.reshape(mod.shape[0], 1, 6 * d)
    nw_pre = norm_mix_pre.reshape(1, d)

    tm = 1024
    p, ga = _inproj(h_state, nw_pre, mod3, lambda i: i // (seq // tm), w_main, w_ga, cos, sin,
                    tm=tm, tn=1024, ncols=N_MAIN, rope=True, seq_tiles=seq // tm)
    pc, gac = _inproj(ctx2d, nw_pre, mod3, lambda i: batch, w_main, w_ga, cos, sin,
                      tm=batch * ctx_len, tn=512, ncols=N_CTX, rope=False, seq_tiles=1)

    o_r = _retention(p, pc, rd_b, ret_gn.reshape(1, RET_VW), batch=batch, seq=seq, ctx_len=ctx_len)
    o_g = _gla(p, ga, pc, gac, up_stack, bias2, gla_gn.reshape(1, GLA_VW), batch=batch, seq=seq, ctx_len=ctx_len)

    hs, h2 = _post(o_r, o_g, p, h_state, w_up_ret.astype(BF16), w_up_gla.astype(BF16), w_out.astype(BF16),
                   mod3, norm_mix_post.reshape(1, d), norm_ffn_pre.reshape(1, d), tm=256, seq=seq)
    act = _ffn_up(h2, ffn_w_gate.astype(BF16), ffn_w_up.astype(BF16), tm=1024, tf=512)
    return _ffn_down(act, hs, ffn_w_down.astype(BF16), mod3, norm_ffn_post.reshape(1, d), tm=256, seq=seq)


def kernel(x, c, ctx, c_ctx, w_mod, b_mod, norm_mix_pre, norm_mix_post, norm_ffn_pre, norm_ffn_post,
           w_in, ret_decay, gla_a_up, gla_a_bias, ret_gn, gla_gn, w_up_ret, w_up_gla, w_out,
           ffn_w_gate, ffn_w_up, ffn_w_down):
    batch, seq, d = x.shape
    ctx_len = ctx.shape[1]
    depth = w_mod.shape[0]
    cos, sin = _rope_tables(seq)
    c_rows = jnp.zeros((8, d), F32).at[:batch].set(c.astype(F32)).at[batch].set(c_ctx.astype(F32))
    ctx2d = ctx.astype(F32).reshape(batch * ctx_len, d)
    h_state = x.astype(F32).reshape(batch * seq, d)
    for i in range(depth):
        h_state = _layer(h_state, c_rows, ctx2d, w_mod[i], b_mod[i], norm_mix_pre[i], norm_mix_post[i],
                         norm_ffn_pre[i], norm_ffn_post[i], w_in[i], ret_decay[i], gla_a_up[i], gla_a_bias[i],
                         ret_gn[i], gla_gn[i], w_up_ret[i], w_up_gla[i], w_out[i],
                         ffn_w_gate[i], ffn_w_up[i], ffn_w_down[i], cos, sin,
                         batch=batch, seq=seq, ctx_len=ctx_len)
    return h_state.reshape(batch, seq, d).astype(x.dtype)
```

```python
import functools

import jax
import jax.numpy as jnp
from jax import lax
from jax.experimental import pallas as pl
from jax.experimental.pallas import tpu as pltpu

F32 = jnp.float32
BF16 = jnp.bfloat16

D_MODEL = 2048
GRID_W = 64
RET_HEADS = 4
RET_DK = 256
RET_DV = 256
GLA_HEADS = 4
GLA_DK = 128
GLA_DV = 256
GLA_LOW_RANK = 16
GLA_GATE_NORM = 16.0
ROPE_BASE = 10000.0
EPS = 1e-6

RET_QK = RET_HEADS * RET_DK
RET_VW = RET_HEADS * RET_DV
GLA_KW = GLA_HEADS * GLA_DK
GLA_VW = GLA_HEADS * GLA_DV
GA_W = 2 * GLA_LOW_RANK
LANE = 128

_SRC = {}
_off = 0
for _name, _w in (("rk", RET_QK), ("rv", RET_VW), ("gk", GLA_KW), ("gv", GLA_VW), ("ga", GA_W),
                  ("rq", RET_QK), ("rg", RET_VW), ("gq", GLA_KW), ("gg", GLA_VW),
                  ("gate_a", D_MODEL), ("gate_b", D_MODEL)):
    _SRC[_name] = (_off, _w)
    _off += _w

_ORDER = ("rk", "rv", "gv", "gk", "gq", "rq", "rg", "gg", "gate_a", "gate_b")
_DST = {}
_off = 0
for _name in _ORDER:
    _DST[_name] = _off
    _off += _SRC[_name][1]
N_MAIN = _off
N_CTX = _DST["gq"]
PACK_W = 512

VMEM_LIMIT = 56 * 1024 * 1024


def _dot(a, b):
    return jnp.dot(a, b, preferred_element_type=F32)


def _dot_nt(a, b):
    return lax.dot_general(a, b, (((1,), (1,)), ((), ())), preferred_element_type=F32)


def _dot_tn(a, b):
    return lax.dot_general(a, b, (((0,), (0,)), ((), ())), preferred_element_type=F32)


def _sigmoid(x):
    return 1.0 / (1.0 + jnp.exp(-x))


def _rms(x):
    return x * lax.rsqrt(jnp.mean(x * x, axis=-1, keepdims=True) + EPS)


def _split_bf16(x):
    hi = x.astype(BF16)
    lo = (x - hi.astype(F32)).astype(BF16)
    return hi, lo


def _mod_kernel(c_ref, w_ref, b_ref, o_ref):
    cf = c_ref[...]
    s = cf * _sigmoid(cf)
    o_ref[...] = jnp.dot(s, w_ref[...], preferred_element_type=F32,
                         precision=lax.Precision.HIGHEST) + b_ref[...]


def _modulation(c_rows, w_mod, b_mod):
    rows, d = c_rows.shape
    n = w_mod.shape[1]
    tn = 1024
    return pl.pallas_call(
        _mod_kernel,
        out_shape=jax.ShapeDtypeStruct((rows, n), F32),
        grid=(n // tn,),
        in_specs=[pl.BlockSpec((rows, d), lambda j: (0, 0)),
                  pl.BlockSpec((d, tn), lambda j: (0, j)),
                  pl.BlockSpec((1, tn), lambda j: (0, j))],
        out_specs=pl.BlockSpec((rows, tn), lambda j: (0, j)),
        compiler_params=pltpu.CompilerParams(dimension_semantics=("arbitrary",),
                                             vmem_limit_bytes=VMEM_LIMIT),
        name="mod",
    )(c_rows, w_mod, b_mod.reshape(1, n))


def _pack_kernel(a_idx_ref, n_idx_ref, shift_ref, a_ref, n_ref, o_ref):
    j = pl.program_id(0)

    @pl.when(shift_ref[j] == 0)
    def _():
        o_ref[...] = a_ref[...].astype(o_ref.dtype)

    @pl.when(shift_ref[j] != 0)
    def _():
        o_ref[...] = jnp.concatenate([a_ref[GA_W:, :], n_ref[...]], axis=0).astype(o_ref.dtype)


def _pack_w_in(w_t):
    d = w_t.shape[1]
    a_idx, n_idx, shift = [], [], []
    for name in _ORDER:
        src, width = _SRC[name]
        for c in range(src, src + width, PACK_W):
            base = c - c % PACK_W
            assert c - base in (0, GA_W)
            a_idx.append(base // PACK_W)
            n_idx.append((base + PACK_W) // GA_W if c != base else (n_idx[-1] if n_idx else 0))
            shift.append(c - base)
    n_tiles = len(a_idx)
    as_i32 = lambda v: jnp.asarray(v, jnp.int32)
    return pl.pallas_call(
        _pack_kernel,
        out_shape=jax.ShapeDtypeStruct((n_tiles * PACK_W, d), BF16),
        grid_spec=pltpu.PrefetchScalarGridSpec(
            num_scalar_prefetch=3,
            grid=(n_tiles,),
            in_specs=[pl.BlockSpec((PACK_W, d), lambda j, a, n, s: (a[j], 0)),
                      pl.BlockSpec((GA_W, d), lambda j, a, n, s: (n[j], 0))],
            out_specs=pl.BlockSpec((PACK_W, d), lambda j, a, n, s: (j, 0)),
        ),
        compiler_params=pltpu.CompilerParams(dimension_semantics=("arbitrary",),
                                             vmem_limit_bytes=VMEM_LIMIT),
        name="pack_w_in",
    )(as_i32(a_idx), as_i32(n_idx), as_i32(shift), w_t, w_t)


def _inproj_kernel(x_ref, nw_ref, sh_ref, sc_ref, w_ref, wga_ref, cos_ref, sin_ref, *rest,
                   rope_tiles, tn, n_cast):
    cast_in = rest[:n_cast]
    o_ref, ga_ref = rest[n_cast:n_cast + 2]
    cast_out = rest[n_cast + 2:2 * n_cast + 2]
    h_scr = rest[2 * n_cast + 2]
    j = pl.program_id(1)

    def ride_along_casts():
        for src, dst in zip(cast_in, cast_out):
            dst[...] = src[...].astype(dst.dtype)

    @pl.when(j == 0)
    def _():
        h = _rms(x_ref[...]) * nw_ref[...] * (1.0 + sc_ref[...]) + sh_ref[...]
        hb = h.astype(BF16)
        h_scr[...] = hb
        g3 = _dot_nt(hb, wga_ref[...])
        lane = lax.broadcasted_iota(jnp.int32, g3.shape, 1)
        resid = g3 - g3.astype(BF16).astype(F32)
        ga_ref[...] = jnp.where((lane >= GA_W) & (lane < 2 * GA_W), resid, g3).astype(ga_ref.dtype)

    if not rope_tiles:
        ride_along_casts()
        o_ref[...] = _dot_nt(h_scr[...], w_ref[...]).astype(o_ref.dtype)
        return

    is_rope = functools.reduce(jnp.logical_or, [j == t for t in rope_tiles])

    @pl.when(is_rope)
    def _():
        ride_along_casts()
        cos = cos_ref[...]
        sin = sin_ref[...]
        half = RET_DK // 2
        for hd in range(0, tn, RET_DK):
            acc = _dot_nt(h_scr[...], w_ref[hd:hd + RET_DK, :])
            t1 = acc[:, :half]
            t2 = acc[:, half:]
            o_ref[:, hd:hd + half] = (t1 * cos - t2 * sin).astype(o_ref.dtype)
            o_ref[:, hd + half:hd + RET_DK] = (t1 * sin + t2 * cos).astype(o_ref.dtype)

    @pl.when(jnp.logical_not(is_rope))
    def _():
        ride_along_casts()
        o_ref[...] = _dot_nt(h_scr[...], w_ref[...]).astype(o_ref.dtype)


def _cast_plan(arr, n_i, n_j):
    rows, cols = arr.shape

    def pieces(extent, want, align):
        n = want
        while extent % n or (extent // n) % align:
            n -= 1
        return n

    if rows % n_j == 0 and (rows // n_j) % 16 == 0:
        nr, nc = n_j, pieces(cols, n_i, LANE)
        index = lambda i, j, nc=nc: (j, jnp.minimum(i, nc - 1))
    else:
        nr, nc = pieces(rows, n_i, 16), pieces(cols, n_j, LANE)
        index = lambda i, j, nr=nr, nc=nc: (jnp.minimum(i, nr - 1), jnp.minimum(j, nc - 1))
    return (rows // nr, cols // nc), index


def _inproj(x2d, norm_w, mod3, mod_row_of_tile, w_main, w_ga, cos, sin, *, tm, tn, ncols, rope, seq_tiles,
            casts=()):
    m, d = x2d.shape
    rope_tiles = ()
    if rope:
        assert tn == RET_QK and _DST["rk"] % tn == 0 and _DST["rq"] % tn == 0
        rope_tiles = (_DST["rk"] // tn, _DST["rq"] // tn)
    grid = (m // tm, ncols // tn)
    cast_specs = [pl.BlockSpec(*_cast_plan(a, *grid)) for a in casts]
    kern = functools.partial(_inproj_kernel, rope_tiles=rope_tiles, tn=tn, n_cast=len(casts))
    return pl.pallas_call(
        kern,
        out_shape=(jax.ShapeDtypeStruct((m, ncols), BF16),
                   jax.ShapeDtypeStruct((m, LANE), BF16),
                   *[jax.ShapeDtypeStruct(a.shape, BF16) for a in casts]),
        grid=grid,
        in_specs=[
            pl.BlockSpec((tm, d), lambda i, j: (i, 0)),
            pl.BlockSpec((1, d), lambda i, j: (0, 0)),
            pl.BlockSpec((None, 1, d), lambda i, j: (mod_row_of_tile(i), 0, 0)),
            pl.BlockSpec((None, 1, d), lambda i, j: (mod_row_of_tile(i), 0, 1)),
            pl.BlockSpec((tn, d), lambda i, j: (j, 0)),
            pl.BlockSpec((LANE, d), lambda i, j: (0, 0)),
            pl.BlockSpec((tm, RET_DK // 2), lambda i, j: (i % seq_tiles, 0)),
            pl.BlockSpec((tm, RET_DK // 2), lambda i, j: (i % seq_tiles, 0)),
            *cast_specs,
        ],
        out_specs=(pl.BlockSpec((tm, tn), lambda i, j: (i, j)),
                   pl.BlockSpec((tm, LANE), lambda i, j: (i, 0)),
                   *cast_specs),
        scratch_shapes=[pltpu.VMEM((tm, d), BF16)],
        compiler_params=pltpu.CompilerParams(dimension_semantics=("arbitrary", "arbitrary"),
                                             vmem_limit_bytes=VMEM_LIMIT),
        name="inproj",
    )(x2d, norm_w, mod3, mod3, w_main, w_ga, cos, sin, *casts)


RET_CHUNK = 256


def _ret_kernel(q_ref, k_ref, v_ref, g_ref, kc_ref, vc_ref, rd_ref, gn_ref, o_ref,
                o_scr, s_scr, dm_scr, eq_scr, wk_scr, sd_scr, *, n_chunks):
    c_len = RET_CHUNK
    ii = lax.broadcasted_iota(jnp.int32, (c_len, c_len), 0)
    jj = lax.broadcasted_iota(jnp.int32, (c_len, c_len), 1)
    rowi = lax.broadcasted_iota(jnp.int32, (c_len, RET_DK), 0).astype(F32)
    scale = RET_DK ** -0.5

    for direction in (0, 1):
        lg = -jnp.exp(rd_ref[direction][0:1, :])
        if direction == 0:
            dm_scr[direction] = jnp.where(jj <= ii, jnp.exp((ii - jj).astype(F32) * lg), 0.0)
            eq_scr[direction] = jnp.exp((rowi + 1.0) * lg)
            wk = jnp.exp((c_len - 1.0 - rowi) * lg)
        else:
            dm_scr[direction] = jnp.where(jj > ii, jnp.exp((jj - ii).astype(F32) * lg), 0.0)
            eq_scr[direction] = jnp.exp((c_len - rowi) * lg)
            wk = jnp.exp(rowi * lg)
        wk_scr[direction] = wk
        sd_scr[direction] = jnp.broadcast_to(jnp.exp(float(c_len) * lg), (8, RET_DV))
        kcw = (kc_ref[...].astype(F32) * wk).astype(BF16)
        s_scr[direction] = _dot_tn(kcw, vc_ref[...])

    def chunk(direction, c, first):
        r0 = pl.multiple_of(c * c_len, c_len)
        q = q_ref[pl.ds(r0, c_len), :]
        k = k_ref[pl.ds(r0, c_len), :]
        v = v_ref[pl.ds(r0, c_len), :]
        kv = _dot_tn((k.astype(F32) * wk_scr[direction]).astype(BF16), v)
        s_state = s_scr[direction]
        sc = (_dot_nt(q, k) * dm_scr[direction]).astype(BF16)
        o = _dot(sc, v) + eq_scr[direction] * _dot(q, s_state.astype(BF16))
        s_scr[direction] = sd_scr[direction][0:1, :] * s_state + kv
        if first:
            o_scr[pl.ds(r0, c_len), :] = o
        else:
            tot = (o_scr[pl.ds(r0, c_len), :] + o) * scale
            mu = jnp.mean(tot, axis=-1, keepdims=True)
            cen = tot - mu
            var = jnp.mean(cen * cen, axis=-1, keepdims=True)
            nrm = cen * lax.rsqrt(var + EPS)
            g = g_ref[pl.ds(r0, c_len), :].astype(F32)
            o_ref[pl.ds(r0, c_len), :] = (nrm * gn_ref[...] * (g * _sigmoid(g))).astype(o_ref.dtype)

    def body(t, carry, first):
        chunk(0, t, first)
        chunk(1, n_chunks - 1 - t, first)
        return carry

    lax.fori_loop(0, n_chunks // 2, functools.partial(body, first=True), 0)
    lax.fori_loop(n_chunks // 2, n_chunks, functools.partial(body, first=False), 0)


def _retention(p, pc, rd_b, ret_gn, *, batch, seq, ctx_len):
    assert ctx_len == RET_CHUNK and (seq // RET_CHUNK) % 2 == 0
    n_chunks = seq // RET_CHUNK
    w = RET_DK
    kern = functools.partial(_ret_kernel, n_chunks=n_chunks)

    def col(name):
        return _DST[name] // w

    return pl.pallas_call(
        kern,
        out_shape=jax.ShapeDtypeStruct((batch * seq, RET_VW), BF16),
        grid=(batch, RET_HEADS),
        in_specs=[
            pl.BlockSpec((seq, w), lambda b, h: (b, col("rq") + h)),
            pl.BlockSpec((seq, w), lambda b, h: (b, col("rk") + h)),
            pl.BlockSpec((seq, w), lambda b, h: (b, col("rv") + h)),
            pl.BlockSpec((seq, w), lambda b, h: (b, col("rg") + h)),
            pl.BlockSpec((ctx_len, w), lambda b, h: (b, col("rk") + h)),
            pl.BlockSpec((ctx_len, w), lambda b, h: (b, col("rv") + h)),
            pl.BlockSpec((2, None, 8, w), lambda b, h: (0, h, 0, 0)),
            pl.BlockSpec((1, w), lambda b, h: (0, h)),
        ],
        out_specs=pl.BlockSpec((seq, w), lambda b, h: (b, h)),
        scratch_shapes=[
            pltpu.VMEM((seq, RET_DV), F32),
            pltpu.VMEM((2, RET_DK, RET_DV), F32),
            pltpu.VMEM((2, RET_CHUNK, RET_CHUNK), F32),
            pltpu.VMEM((2, RET_CHUNK, RET_DV), F32),
            pltpu.VMEM((2, RET_CHUNK, RET_DK), F32),
            pltpu.VMEM((2, 8, RET_DV), F32),
        ],
        compiler_params=pltpu.CompilerParams(dimension_semantics=("arbitrary", "arbitrary"),
                                             vmem_limit_bytes=VMEM_LIMIT),
        name="retention",
    )(p, p, p, p, pc, pc, rd_b, ret_gn)


GLA_SUB = 64
GLA_BLK = 256


def _gla_kernel(q_ref, k_ref, v_ref, g_ref, ga_ref, kc_ref, vc_ref, gac_ref, up_ref, bias_ref, gn_ref,
                o_ref, qf_scr, qb_scr, khf_scr, khb_scr, khcf_scr, khcb_scr, dec_scr, decc_scr,
                o_scr, s_scr, *, n_blk):
    blk, sub, dk = GLA_BLK, GLA_SUB, GLA_DK
    nsub = blk // sub
    shift = sub.bit_length() - 1
    scale = GLA_DK ** -0.5

    ii = lax.broadcasted_iota(jnp.int32, (blk, blk), 0)
    jj = lax.broadcasted_iota(jnp.int32, (blk, blk), 1)
    same = (ii >> shift) == (jj >> shift)
    lower = jj <= ii
    tmat = jnp.where(same & lower, 1.0, 0.0).astype(BF16)
    r8 = lax.broadcasted_iota(jnp.int32, (8, dk), 0)

    u = up_ref[...]
    u_hi, u_lo = _split_bf16(u)
    urow = lax.broadcasted_iota(jnp.int32, u.shape, 0)
    rhs = jnp.where(urow >= 2 * GA_W, u_lo, u_hi)
    bias = bias_ref[...]

    def prep(ga_blk, k_blk, q_blk, v_blk):
        z = _dot(ga_blk, rhs) + bias
        a = (jnp.minimum(z, 0.0) - jnp.log(1.0 + jnp.exp(-jnp.abs(z)))) * (1.0 / GLA_GATE_NORM)
        a_hi, a_lo = _split_bf16(a)
        pre = _dot(tmat, a_hi) + _dot(tmat, a_lo)
        lasts = [pre[i * sub + sub - 1:i * sub + sub, :] for i in range(nsub)]
        tot = jnp.concatenate([jnp.broadcast_to(l, (sub, 2 * dk)) for l in lasts], axis=0)
        b_f = pre[:, :dk]
        ex_b = pre[:, dk:] - a[:, dk:]
        b_b = tot[:, dk:] - ex_b
        kf = k_blk.astype(F32)
        khf = (kf * jnp.exp(tot[:, :dk] - b_f)).astype(BF16)
        khb = (kf * jnp.exp(ex_b)).astype(BF16)

        def tile8(half):
            rows = [jnp.broadcast_to(l[:, half * dk:(half + 1) * dk], (8, dk)) for l in lasts]
            t = rows[nsub - 1]
            for i in range(nsub - 2, -1, -1):
                t = jnp.where(r8 == i, rows[i], t)
            return t

        dec = jnp.exp(jnp.concatenate([tile8(0), tile8(1), jnp.zeros((LANE - 16, dk), F32)], axis=0))
        dec_t = dec.T
        if q_blk is None:
            return khf, khb, dec_t
        qf32 = q_blk.astype(F32)
        qf = (qf32 * jnp.exp(b_f)).astype(BF16)
        qb = (qf32 * jnp.exp(b_b)).astype(BF16)
        ktf = (kf * jnp.exp(-b_f)).astype(BF16)
        ktb = (kf * jnp.exp(-b_b)).astype(BF16)
        s = jnp.where(same, jnp.where(lower, _dot_nt(qf, ktf), _dot_nt(qb, ktb)), 0.0).astype(BF16)
        return khf, khb, dec_t, qf, qb, _dot(s, v_blk)

    khf, khb, dec_t = prep(gac_ref[...], kc_ref[...], None, None)
    khcf_scr[...] = khf
    khcb_scr[...] = khb
    decc_scr[...] = dec_t

    def prep_body(t, carry):
        r = pl.multiple_of(t * blk, blk)
        khf, khb, dec_t, qf, qb, o_intra = prep(ga_ref[pl.ds(r, blk), :], k_ref[pl.ds(r, blk), :],
                                                q_ref[pl.ds(r, blk), :], v_ref[pl.ds(r, blk), :])
        khf_scr[pl.ds(r, blk), :] = khf
        khb_scr[pl.ds(r, blk), :] = khb
        qf_scr[pl.ds(r, blk), :] = qf
        qb_scr[pl.ds(r, blk), :] = qb
        o_scr[pl.ds(r, blk), :] = o_intra
        dec_scr[t] = dec_t
        return carry

    lax.fori_loop(0, n_blk, prep_body, 0, unroll=4)

    def scan_blk(direction, dec_t, kh_at, v_at, q_at=None, o_rows=None):
        s = s_scr[direction]
        order = range(nsub) if direction == 0 else range(nsub - 1, -1, -1)
        kv = {i: _dot_tn(kh_at(i), v_at(i)) for i in order}
        for i in order:
            col = dec_t[:, direction * 8 + i:direction * 8 + i + 1]
            if q_at is not None:
                rows = o_rows(i)
                o_scr[rows, :] += _dot(q_at(i), s.astype(BF16))
            s = s * col + kv[i]
        s_scr[direction] = s

    s_scr[...] = jnp.zeros(s_scr.shape, F32)
    for direction, khc in ((0, khcf_scr), (1, khcb_scr)):
        scan_blk(direction, decc_scr[...],
                 lambda i, khc=khc: khc[i * sub:(i + 1) * sub, :],
                 lambda i: vc_ref[i * sub:(i + 1) * sub, :])

    def finalize(t):
        r = pl.multiple_of(t * blk, blk)
        nrm = _rms(o_scr[pl.ds(r, blk), :] * scale)
        g = g_ref[pl.ds(r, blk), :].astype(F32)
        o_ref[pl.ds(r, blk), :] = (nrm * gn_ref[...] * (g * _sigmoid(g))).astype(o_ref.dtype)

    def scan_body(t, carry, done):
        for direction, q_scr, kh_scr in ((0, qf_scr, khf_scr), (1, qb_scr, khb_scr)):
            tb = t if direction == 0 else n_blk - 1 - t

            def rows(i, tb=tb):
                return pl.ds(pl.multiple_of(tb * blk + i * sub, sub), sub)

            scan_blk(direction, dec_scr[tb],
                     lambda i, kh_scr=kh_scr, rows=rows: kh_scr[rows(i), :],
                     lambda i, rows=rows: v_ref[rows(i), :],
                     lambda i, q_scr=q_scr, rows=rows: q_scr[rows(i), :],
                     rows)
        if done:
            finalize(t)
            finalize(n_blk - 1 - t)
        return carry

    lax.fori_loop(0, n_blk // 2, functools.partial(scan_body, done=False), 0, unroll=2)
    lax.fori_loop(n_blk // 2, n_blk, functools.partial(scan_body, done=True), 0, unroll=2)


def _gla(p, ga, pc, gac, up_stack, bias2, gla_gn, *, batch, seq, ctx_len):
    assert ctx_len == GLA_BLK and (seq // GLA_BLK) % 2 == 0
    n_blk = seq // GLA_BLK
    kern = functools.partial(_gla_kernel, n_blk=n_blk)
    dk, dv = GLA_DK, GLA_DV
    return pl.pallas_call(
        kern,
        out_shape=jax.ShapeDtypeStruct((batch * seq, GLA_VW), BF16),
        grid=(batch, GLA_HEADS),
        in_specs=[
            pl.BlockSpec((seq, dk), lambda b, h: (b, _DST["gq"] // dk + h)),
            pl.BlockSpec((seq, dk), lambda b, h: (b, _DST["gk"] // dk + h)),
            pl.BlockSpec((seq, dv), lambda b, h: (b, _DST["gv"] // dv + h)),
            pl.BlockSpec((seq, dv), lambda b, h: (b, _DST["gg"] // dv + h)),
            pl.BlockSpec((seq, LANE), lambda b, h: (b, 0)),
            pl.BlockSpec((ctx_len, dk), lambda b, h: (b, _DST["gk"] // dk + h)),
            pl.BlockSpec((ctx_len, dv), lambda b, h: (b, _DST["gv"] // dv + h)),
            pl.BlockSpec((ctx_len, LANE), lambda b, h: (b, 0)),
            pl.BlockSpec((LANE, 2 * dk), lambda b, h: (0, h)),
            pl.BlockSpec((1, 2 * dk), lambda b, h: (0, h)),
            pl.BlockSpec((1, dv), lambda b, h: (0, h)),
        ],
        out_specs=pl.BlockSpec((seq, dv), lambda b, h: (b, h)),
        scratch_shapes=[
            pltpu.VMEM((seq, dk), BF16), pltpu.VMEM((seq, dk), BF16),
            pltpu.VMEM((seq, dk), BF16), pltpu.VMEM((seq, dk), BF16),
            pltpu.VMEM((ctx_len, dk), BF16), pltpu.VMEM((ctx_len, dk), BF16),
            pltpu.VMEM((n_blk, dk, LANE), F32), pltpu.VMEM((dk, LANE), F32),
            pltpu.VMEM((seq, dv), F32),
            pltpu.VMEM((2, dk, dv), F32),
        ],
        compiler_params=pltpu.CompilerParams(dimension_semantics=("arbitrary", "arbitrary"),
                                             vmem_limit_bytes=VMEM_LIMIT),
        name="gla",
    )(p, p, p, p, ga, pc, pc, gac, up_stack, bias2, gla_gn)


def _post_kernel(or_ref, og_ref, ga0_ref, ga1_ref, gb0_ref, gb1_ref, x_ref, wur_ref, wug_ref, wo_ref,
                 g1_ref, sh2_ref, sc2_ref, npost_ref, npre_ref, hs_ref, h2_ref):
    y_ret = _dot(or_ref[...], wur_ref[...])
    y_gla = _dot(og_ref[...], wug_ref[...])
    half = ga0_ref.shape[1]

    def merge(ga_r, gb_r, lo):
        return (_sigmoid(ga_r[...].astype(F32)) * y_ret[:, lo:lo + half]
                + _sigmoid(gb_r[...].astype(F32)) * y_gla[:, lo:lo + half]).astype(BF16)

    merged = jnp.concatenate([merge(ga0_ref, gb0_ref, 0), merge(ga1_ref, gb1_ref, half)], axis=1)
    y = _dot(merged, wo_ref[...])
    hs = x_ref[...] + g1_ref[...] * (_rms(y) * npost_ref[...])
    hs_ref[...] = hs
    h2 = _rms(hs) * npre_ref[...] * (1.0 + sc2_ref[...]) + sh2_ref[...]
    h2_ref[...] = h2.astype(h2_ref.dtype)


def _post(o_r, o_g, p, x2d, w_up_ret, w_up_gla, w_out, mod3, npost, npre, *, tm, seq):
    m, d = x2d.shape
    tiles_per_seq = seq // tm
    const = dict(pipeline_mode=pl.Buffered(1))
    gw = d // 2

    def modspec(chunk):
        return pl.BlockSpec((None, 1, d), lambda i: (i // tiles_per_seq, 0, chunk))

    def gatespec(name, part):
        return pl.BlockSpec((tm, gw), lambda i: (i, _DST[name] // gw + part))

    return pl.pallas_call(
        _post_kernel,
        out_shape=(jax.ShapeDtypeStruct((m, d), F32), jax.ShapeDtypeStruct((m, d), BF16)),
        grid=(m // tm,),
        in_specs=[
            pl.BlockSpec((tm, RET_VW), lambda i: (i, 0)),
            pl.BlockSpec((tm, GLA_VW), lambda i: (i, 0)),
            gatespec("gate_a", 0), gatespec("gate_a", 1), gatespec("gate_b", 0), gatespec("gate_b", 1),
            pl.BlockSpec((tm, d), lambda i: (i, 0)),
            pl.BlockSpec((RET_VW, d), lambda i: (0, 0), **const),
            pl.BlockSpec((GLA_VW, d), lambda i: (0, 0), **const),
            pl.BlockSpec((d, d), lambda i: (0, 0), **const),
            modspec(2), modspec(3), modspec(4),
            pl.BlockSpec((1, d), lambda i: (0, 0)),
            pl.BlockSpec((1, d), lambda i: (0, 0)),
        ],
        out_specs=(pl.BlockSpec((tm, d), lambda i: (i, 0)),
                   pl.BlockSpec((tm, d), lambda i: (i, 0))),
        compiler_params=pltpu.CompilerParams(dimension_semantics=("arbitrary",),
                                             vmem_limit_bytes=VMEM_LIMIT),
        name="post",
    )(o_r, o_g, p, p, p, p, x2d, w_up_ret, w_up_gla, w_out, mod3, mod3, mod3, npost, npre)


MXU_N = 256


def _ffn_up_kernel(h_ref, wg_ref, wu_ref, a_ref):
    h = h_ref[...]
    for lo in range(0, a_ref.shape[1], MXU_N):
        g = _dot(h, wg_ref[:, lo:lo + MXU_N])
        u = _dot(h, wu_ref[:, lo:lo + MXU_N])
        a_ref[:, lo:lo + MXU_N] = (g * _sigmoid(g) * u).astype(a_ref.dtype)


def _ffn_up(h2, wg, wu, *, tm, tf):
    m, d = h2.shape
    d_ff = wg.shape[1]
    return pl.pallas_call(
        _ffn_up_kernel,
        out_shape=jax.ShapeDtypeStruct((m, d_ff), BF16),
        grid=(m // tm, d_ff // tf),
        in_specs=[
            pl.BlockSpec((tm, d), lambda i, f: (i, 0)),
            pl.BlockSpec((d, tf), lambda i, f: (0, f)),
            pl.BlockSpec((d, tf), lambda i, f: (0, f)),
        ],
        out_specs=pl.BlockSpec((tm, tf), lambda i, f: (i, f)),
        compiler_params=pltpu.CompilerParams(dimension_semantics=("arbitrary", "arbitrary"),
                                             vmem_limit_bytes=VMEM_LIMIT),
        name="ffn_up",
    )(h2, wg, wu)


def _ffn_down_kernel(a_ref, wd_ref, hs_ref, g2_ref, nw_ref, o_ref, y_scr, *, n_split):
    tm, d = o_ref.shape
    w = d // n_split
    ss = jnp.zeros((tm, 1), F32)
    for lo in range(0, d, w):
        y = _dot(a_ref[...], wd_ref[:, lo:lo + w])
        y_scr[:, lo:lo + w] = y
        ss = ss + jnp.sum(y * y, axis=-1, keepdims=True)
    inv = lax.rsqrt(ss * (1.0 / d) + EPS)
    o_ref[...] = hs_ref[...] + g2_ref[...] * (y_scr[...] * inv * nw_ref[...])


def _ffn_down(a, hs, wd, mod3, npost, *, tm, seq):
    m, d = hs.shape
    d_ff = wd.shape[0]
    tiles_per_seq = seq // tm
    kern = functools.partial(_ffn_down_kernel, n_split=4)
    return pl.pallas_call(
        kern,
        out_shape=jax.ShapeDtypeStruct((m, d), F32),
        grid=(m // tm,),
        in_specs=[
            pl.BlockSpec((tm, d_ff), lambda i: (i, 0)),
            pl.BlockSpec((d_ff, d), lambda i: (0, 0), pipeline_mode=pl.Buffered(1)),
            pl.BlockSpec((tm, d), lambda i: (i, 0)),
            pl.BlockSpec((None, 1, d), lambda i: (i // tiles_per_seq, 0, 5)),
            pl.BlockSpec((1, d), lambda i: (0, 0)),
        ],
        out_specs=pl.BlockSpec((tm, d), lambda i: (i, 0)),
        scratch_shapes=[pltpu.VMEM((tm, d), F32)],
        compiler_params=pltpu.CompilerParams(dimension_semantics=("arbitrary",),
                                             vmem_limit_bytes=VMEM_LIMIT),
        name="ffn_down",
    )(a, wd, hs, mod3, npost)


def _rope_tables(seq):
    rows = seq // GRID_W
    pos_r = jnp.repeat(jnp.arange(rows, dtype=F32), GRID_W)
    pos_c = jnp.tile(jnp.arange(GRID_W, dtype=F32), rows)
    n_f = RET_DK // 4
    inv = ROPE_BASE ** (-jnp.arange(n_f, dtype=F32) / n_f)
    ang = jnp.concatenate([pos_r[:, None] * inv, pos_c[:, None] * inv], axis=-1)
    return jnp.cos(ang), jnp.sin(ang)


def _gate_map_layout(gla_a_up, gla_a_bias):
    r, h, dk = GLA_LOW_RANK, GLA_HEADS, GLA_DK
    u = jnp.zeros((GA_W, h, 2, dk), F32)
    u = u.at[:r, :, 0, :].set(gla_a_up[0].reshape(r, h, dk))
    u = u.at[r:, :, 1, :].set(gla_a_up[1].reshape(r, h, dk))
    u = u.reshape(GA_W, h * 2 * dk)
    up_stack = jnp.concatenate([u, u, u, jnp.zeros((LANE - 3 * GA_W, h * 2 * dk), F32)], axis=0)
    bias2 = jnp.stack([gla_a_bias[0].reshape(h, dk), gla_a_bias[1].reshape(h, dk)], axis=1)
    return up_stack, bias2.reshape(1, h * 2 * dk)


def _layer(h_state, c_rows, ctx2d, w_mod, b_mod, norm_mix_pre, norm_mix_post, norm_ffn_pre, norm_ffn_post,
           w_in, ret_decay, gla_a_up, gla_a_bias, ret_gn, gla_gn, w_up_ret, w_up_gla, w_out,
           ffn_w_gate, ffn_w_up, ffn_w_down, cos, sin, *, batch, seq, ctx_len):
    d = D_MODEL
    ga0 = _SRC["ga"][0]
    w_t = w_in.T
    w_main = _pack_w_in(w_t)
    w_g = w_t[ga0:ga0 + GA_W]
    w_ga = jnp.concatenate([w_g, w_g, w_g, jnp.zeros((LANE - 3 * GA_W, d), F32)], axis=0).astype(BF16)
    up_stack, bias2 = _gate_map_layout(gla_a_up, gla_a_bias)
    rd_b = jnp.broadcast_to(ret_decay.reshape(2, RET_HEADS, 1, 1), (2, RET_HEADS, 8, RET_DK))

    mod = _modulation(c_rows, w_mod, b_mod)
    mod3 = mod.reshape(mod.shape[0], 1, 6 * d)
    nw_pre = norm_mix_pre.reshape(1, d)

    tm = 1024
    later = (w_up_ret, w_up_gla, w_out, ffn_w_gate, ffn_w_up, ffn_w_down)
    p, ga, wur, wug, wo, wg, wu, wd = _inproj(
        h_state, nw_pre, mod3, lambda i: i // (seq // tm), w_main, w_ga, cos, sin,
        tm=tm, tn=1024, ncols=N_MAIN, rope=True, seq_tiles=seq // tm, casts=later)
    pc, gac = _inproj(ctx2d, nw_pre, mod3, lambda i: batch, w_main, w_ga, cos, sin,
                      tm=batch * ctx_len, tn=512, ncols=N_CTX, rope=False, seq_tiles=1)

    o_r = _retention(p, pc, rd_b, ret_gn.reshape(1, RET_VW), batch=batch, seq=seq, ctx_len=ctx_len)
    o_g = _gla(p, ga, pc, gac, up_stack, bias2, gla_gn.reshape(1, GLA_VW), batch=batch, seq=seq, ctx_len=ctx_len)

    hs, h2 = _post(o_r, o_g, p, h_state, wur, wug, wo,
                   mod3, norm_mix_post.reshape(1, d), norm_ffn_pre.reshape(1, d), tm=256, seq=seq)
    act = _ffn_up(h2, wg, wu, tm=1024, tf=512)
    return _ffn_down(act, hs, wd, mod3, norm_ffn_post.reshape(1, d), tm=256, seq=seq)


def kernel(x, c, ctx, c_ctx, w_mod, b_mod, norm_mix_pre, norm_mix_post, norm_ffn_pre, norm_ffn_post,
           w_in, ret_decay, gla_a_up, gla_a_bias, ret_gn, gla_gn, w_up_ret, w_up_gla, w_out,
           ffn_w_gate, ffn_w_up, ffn_w_down):
    batch, seq, d = x.shape
    ctx_len = ctx.shape[1]
    depth = w_mod.shape[0]
    cos, sin = _rope_tables(seq)
    c_rows = jnp.zeros((8, d), F32).at[:batch].set(c.astype(F32)).at[batch].set(c_ctx.astype(F32))
    ctx2d = ctx.astype(F32).reshape(batch * ctx_len, d)
    h_state = x.astype(F32).reshape(batch * seq, d)
    for i in range(depth):
        h_state = _layer(h_state, c_rows, ctx2d, w_mod[i], b_mod[i], norm_mix_pre[i], norm_mix_post[i],
                         norm_ffn_pre[i], norm_ffn_post[i], w_in[i], ret_decay[i], gla_a_up[i], gla_a_bias[i],
                         ret_gn[i], gla_gn[i], w_up_ret[i], w_up_gla[i], w_out[i],
                         ffn_w_gate[i], ffn_w_up[i], ffn_w_down[i], cos, sin,
                         batch=batch, seq=seq, ctx_len=ctx_len)
    return h_state.reshape(batch, seq, d).astype(x.dtype)
```

```python
import functools

import jax
import jax.numpy as jnp
from jax import lax
from jax.experimental import pallas as pl
from jax.experimental.pallas import tpu as pltpu

F32 = jnp.float32
BF16 = jnp.bfloat16

D_MODEL = 2048
GRID_W = 64
RET_HEADS = 4
RET_DK = 256
RET_DV = 256
GLA_HEADS = 4
GLA_DK = 128
GLA_DV = 256
GLA_LOW_RANK = 16
GLA_GATE_NORM = 16.0
ROPE_BASE = 10000.0
EPS = 1e-6

RET_QK = RET_HEADS * RET_DK
RET_VW = RET_HEADS * RET_DV
GLA_KW = GLA_HEADS * GLA_DK
GLA_VW = GLA_HEADS * GLA_DV
GA_W = 2 * GLA_LOW_RANK
LANE = 128

_SRC = {}
_off = 0
for _name, _w in (("rk", RET_QK), ("rv", RET_VW), ("gk", GLA_KW), ("gv", GLA_VW), ("ga", GA_W),
                  ("rq", RET_QK), ("rg", RET_VW), ("gq", GLA_KW), ("gg", GLA_VW),
                  ("gate_a", D_MODEL), ("gate_b", D_MODEL)):
    _SRC[_name] = (_off, _w)
    _off += _w

_ORDER = ("rk", "rv", "gv", "gk", "gq", "rq", "rg", "gg", "gate_a", "gate_b")
_DST = {}
_off = 0
for _name in _ORDER:
    _DST[_name] = _off
    _off += _SRC[_name][1]
N_MAIN = _off
N_CTX = _DST["gq"]
PACK_W = 512

VMEM_LIMIT = 56 * 1024 * 1024


def _dot(a, b):
    return jnp.dot(a, b, preferred_element_type=F32)


def _dot_nt(a, b):
    return lax.dot_general(a, b, (((1,), (1,)), ((), ())), preferred_element_type=F32)


def _dot_tn(a, b):
    return lax.dot_general(a, b, (((0,), (0,)), ((), ())), preferred_element_type=F32)


def _sigmoid(x):
    return 1.0 / (1.0 + jnp.exp(-x))


def _rms(x):
    return x * lax.rsqrt(jnp.mean(x * x, axis=-1, keepdims=True) + EPS)


def _split_bf16(x):
    hi = x.astype(BF16)
    lo = (x - hi.astype(F32)).astype(BF16)
    return hi, lo


MOD_TN = 1024


def _prologue_kernel(a_idx_ref, n_idx_ref, shift_ref, a_ref, n_ref, c_ref, wm_ref, bm_ref,
                     o_ref, mod_ref, *, n_mod):
    j = pl.program_id(0)

    @pl.when(shift_ref[j] == 0)
    def _():
        o_ref[...] = a_ref[...].astype(o_ref.dtype)

    @pl.when(shift_ref[j] != 0)
    def _():
        o_ref[...] = jnp.concatenate([a_ref[GA_W:, :], n_ref[...]], axis=0).astype(o_ref.dtype)

    @pl.when(j < n_mod)
    def _():
        cf = c_ref[...]
        rows = cf.shape[0]
        s_hi, s_lo = _split_bf16(cf * _sigmoid(cf))
        w_hi, w_lo = _split_bf16(wm_ref[...])
        r = _dot(jnp.concatenate([s_hi, s_lo], axis=0), w_hi)
        mod_ref[...] = r[:rows] + r[rows:] + _dot(s_hi, w_lo) + bm_ref[...]


def _prologue(w_t, c_rows, w_mod, b_mod):
    d = w_t.shape[1]
    rows = c_rows.shape[0]
    n_modcols = w_mod.shape[1]
    n_mod = n_modcols // MOD_TN
    a_idx, n_idx, shift = [], [], []
    for name in _ORDER:
        src, width = _SRC[name]
        for c in range(src, src + width, PACK_W):
            base = c - c % PACK_W
            assert c - base in (0, GA_W)
            a_idx.append(base // PACK_W)
            n_idx.append((base + PACK_W) // GA_W if c != base else (n_idx[-1] if n_idx else 0))
            shift.append(c - base)
    n_tiles = len(a_idx)
    assert n_mod <= n_tiles
    as_i32 = lambda v: jnp.asarray(v, jnp.int32)
    mod_tile = lambda j, a, n, s: (0, jnp.minimum(j, n_mod - 1))
    return pl.pallas_call(
        functools.partial(_prologue_kernel, n_mod=n_mod),
        out_shape=(jax.ShapeDtypeStruct((n_tiles * PACK_W, d), BF16),
                   jax.ShapeDtypeStruct((rows, n_modcols), F32)),
        grid_spec=pltpu.PrefetchScalarGridSpec(
            num_scalar_prefetch=3,
            grid=(n_tiles,),
            in_specs=[pl.BlockSpec((PACK_W, d), lambda j, a, n, s: (a[j], 0)),
                      pl.BlockSpec((GA_W, d), lambda j, a, n, s: (n[j], 0)),
                      pl.BlockSpec((rows, d), lambda j, a, n, s: (0, 0)),
                      pl.BlockSpec((d, MOD_TN), mod_tile),
                      pl.BlockSpec((1, MOD_TN), mod_tile)],
            out_specs=(pl.BlockSpec((PACK_W, d), lambda j, a, n, s: (j, 0)),
                       pl.BlockSpec((rows, MOD_TN), mod_tile)),
        ),
        compiler_params=pltpu.CompilerParams(dimension_semantics=("arbitrary",),
                                             vmem_limit_bytes=VMEM_LIMIT),
        name="prologue",
    )(as_i32(a_idx), as_i32(n_idx), as_i32(shift), w_t, w_t, c_rows, w_mod, b_mod.reshape(1, n_modcols))


def _inproj_kernel(x_ref, nw_ref, sh_ref, sc_ref, w_ref, wga_ref, cos_ref, sin_ref, *rest,
                   rope_tiles, tn, n_cast):
    cast_in = rest[:n_cast]
    o_ref, ga_ref = rest[n_cast:n_cast + 2]
    cast_out = rest[n_cast + 2:2 * n_cast + 2]
    h_scr = rest[2 * n_cast + 2]
    j = pl.program_id(1)

    def ride_along_casts():
        for src, dst in zip(cast_in, cast_out):
            dst[...] = src[...].astype(dst.dtype)

    @pl.when(j == 0)
    def _():
        h = _rms(x_ref[...]) * nw_ref[...] * (1.0 + sc_ref[...]) + sh_ref[...]
        hb = h.astype(BF16)
        h_scr[...] = hb
        g3 = _dot_nt(hb, wga_ref[...])
        lane = lax.broadcasted_iota(jnp.int32, g3.shape, 1)
        resid = g3 - g3.astype(BF16).astype(F32)
        ga_ref[...] = jnp.where((lane >= GA_W) & (lane < 2 * GA_W), resid, g3).astype(ga_ref.dtype)

    if not rope_tiles:
        ride_along_casts()
        o_ref[...] = _dot_nt(h_scr[...], w_ref[...]).astype(o_ref.dtype)
        return

    is_rope = functools.reduce(jnp.logical_or, [j == t for t in rope_tiles])

    @pl.when(is_rope)
    def _():
        ride_along_casts()
        cos = cos_ref[...]
        sin = sin_ref[...]
        half = RET_DK // 2
        for hd in range(0, tn, RET_DK):
            acc = _dot_nt(h_scr[...], w_ref[hd:hd + RET_DK, :])
            t1 = acc[:, :half]
            t2 = acc[:, half:]
            o_ref[:, hd:hd + half] = (t1 * cos - t2 * sin).astype(o_ref.dtype)
            o_ref[:, hd + half:hd + RET_DK] = (t1 * sin + t2 * cos).astype(o_ref.dtype)

    @pl.when(jnp.logical_not(is_rope))
    def _():
        ride_along_casts()
        o_ref[...] = _dot_nt(h_scr[...], w_ref[...]).astype(o_ref.dtype)


def _cast_plan(arr, n_i, n_j):
    rows, cols = arr.shape

    def pieces(extent, want, align):
        n = want
        while extent % n or (extent // n) % align:
            n -= 1
        return n

    if rows % n_j == 0 and (rows // n_j) % 16 == 0:
        nr, nc = n_j, pieces(cols, n_i, LANE)
        index = lambda i, j, nc=nc: (j, jnp.minimum(i, nc - 1))
    else:
        nr, nc = pieces(rows, n_i, 16), pieces(cols, n_j, LANE)
        index = lambda i, j, nr=nr, nc=nc: (jnp.minimum(i, nr - 1), jnp.minimum(j, nc - 1))
    return (rows // nr, cols // nc), index


def _inproj(x2d, norm_w, mod3, mod_row_of_tile, w_main, w_ga, cos, sin, *, tm, tn, ncols, rope, seq_tiles,
            casts=()):
    m, d = x2d.shape
    rope_tiles = ()
    if rope:
        assert tn == RET_QK and _DST["rk"] % tn == 0 and _DST["rq"] % tn == 0
        rope_tiles = (_DST["rk"] // tn, _DST["rq"] // tn)
    grid = (m // tm, ncols // tn)
    cast_specs = [pl.BlockSpec(*_cast_plan(a, *grid)) for a in casts]
    kern = functools.partial(_inproj_kernel, rope_tiles=rope_tiles, tn=tn, n_cast=len(casts))
    return pl.pallas_call(
        kern,
        out_shape=(jax.ShapeDtypeStruct((m, ncols), BF16),
                   jax.ShapeDtypeStruct((m, LANE), BF16),
                   *[jax.ShapeDtypeStruct(a.shape, BF16) for a in casts]),
        grid=grid,
        in_specs=[
            pl.BlockSpec((tm, d), lambda i, j: (i, 0)),
            pl.BlockSpec((1, d), lambda i, j: (0, 0)),
            pl.BlockSpec((None, 1, d), lambda i, j: (mod_row_of_tile(i), 0, 0)),
            pl.BlockSpec((None, 1, d), lambda i, j: (mod_row_of_tile(i), 0, 1)),
            pl.BlockSpec((tn, d), lambda i, j: (j, 0)),
            pl.BlockSpec((LANE, d), lambda i, j: (0, 0)),
            pl.BlockSpec((tm, RET_DK // 2), lambda i, j: (i % seq_tiles, 0)),
            pl.BlockSpec((tm, RET_DK // 2), lambda i, j: (i % seq_tiles, 0)),
            *cast_specs,
        ],
        out_specs=(pl.BlockSpec((tm, tn), lambda i, j: (i, j)),
                   pl.BlockSpec((tm, LANE), lambda i, j: (i, 0)),
                   *cast_specs),
        scratch_shapes=[pltpu.VMEM((tm, d), BF16)],
        compiler_params=pltpu.CompilerParams(dimension_semantics=("arbitrary", "arbitrary"),
                                             vmem_limit_bytes=VMEM_LIMIT),
        name="inproj",
    )(x2d, norm_w, mod3, mod3, w_main, w_ga, cos, sin, *casts)


RET_CHUNK = 256


def _ret_kernel(q_ref, k_ref, v_ref, g_ref, kc_ref, vc_ref, rd_ref, gn_ref, wsrc_ref, o_ref, wdst_ref,
                o_scr, s_scr, dm_scr, eq_scr, wk_scr, sd_scr, *, n_chunks):
    c_len = RET_CHUNK
    wdst_ref[...] = wsrc_ref[...].astype(wdst_ref.dtype)
    ii = lax.broadcasted_iota(jnp.int32, (c_len, c_len), 0)
    jj = lax.broadcasted_iota(jnp.int32, (c_len, c_len), 1)
    rowi = lax.broadcasted_iota(jnp.int32, (c_len, RET_DK), 0).astype(F32)
    scale = RET_DK ** -0.5

    for direction in (0, 1):
        lg = -jnp.exp(rd_ref[direction][0:1, :])
        if direction == 0:
            dm_scr[direction] = jnp.where(jj <= ii, jnp.exp((ii - jj).astype(F32) * lg), 0.0)
            eq_scr[direction] = jnp.exp((rowi + 1.0) * lg)
            wk = jnp.exp((c_len - 1.0 - rowi) * lg)
        else:
            dm_scr[direction] = jnp.where(jj > ii, jnp.exp((jj - ii).astype(F32) * lg), 0.0)
            eq_scr[direction] = jnp.exp((c_len - rowi) * lg)
            wk = jnp.exp(rowi * lg)
        wk_scr[direction] = wk
        sd_scr[direction] = jnp.broadcast_to(jnp.exp(float(c_len) * lg), (8, RET_DV))
        kcw = (kc_ref[...].astype(F32) * wk).astype(BF16)
        s_scr[direction] = _dot_tn(kcw, vc_ref[...])

    def chunk(direction, c, first):
        r0 = pl.multiple_of(c * c_len, c_len)
        q = q_ref[pl.ds(r0, c_len), :]
        k = k_ref[pl.ds(r0, c_len), :]
        v = v_ref[pl.ds(r0, c_len), :]
        kv = _dot_tn((k.astype(F32) * wk_scr[direction]).astype(BF16), v)
        s_state = s_scr[direction]
        sc = (_dot_nt(q, k) * dm_scr[direction]).astype(BF16)
        o = _dot(sc, v) + eq_scr[direction] * _dot(q, s_state.astype(BF16))
        s_scr[direction] = sd_scr[direction][0:1, :] * s_state + kv
        if first:
            o_scr[pl.ds(r0, c_len), :] = o
        else:
            tot = (o_scr[pl.ds(r0, c_len), :] + o) * scale
            mu = jnp.mean(tot, axis=-1, keepdims=True)
            cen = tot - mu
            var = jnp.mean(cen * cen, axis=-1, keepdims=True)
            nrm = cen * lax.rsqrt(var + EPS)
            g = g_ref[pl.ds(r0, c_len), :].astype(F32)
            o_ref[pl.ds(r0, c_len), :] = (nrm * gn_ref[...] * (g * _sigmoid(g))).astype(o_ref.dtype)

    def body(t, carry, first):
        chunk(0, t, first)
        chunk(1, n_chunks - 1 - t, first)
        return carry

    lax.fori_loop(0, n_chunks // 2, functools.partial(body, first=True), 0)
    lax.fori_loop(n_chunks // 2, n_chunks, functools.partial(body, first=False), 0)


def _retention(p, pc, rd_b, ret_gn, w_later, *, batch, seq, ctx_len):
    assert ctx_len == RET_CHUNK and (seq // RET_CHUNK) % 2 == 0
    n_chunks = seq // RET_CHUNK
    w = RET_DK
    kern = functools.partial(_ret_kernel, n_chunks=n_chunks)
    cast_spec = pl.BlockSpec(*_cast_plan(w_later, batch, RET_HEADS))

    def col(name):
        return _DST[name] // w

    return pl.pallas_call(
        kern,
        out_shape=(jax.ShapeDtypeStruct((batch * seq, RET_VW), BF16),
                   jax.ShapeDtypeStruct(w_later.shape, BF16)),
        grid=(batch, RET_HEADS),
        in_specs=[
            pl.BlockSpec((seq, w), lambda b, h: (b, col("rq") + h)),
            pl.BlockSpec((seq, w), lambda b, h: (b, col("rk") + h)),
            pl.BlockSpec((seq, w), lambda b, h: (b, col("rv") + h)),
            pl.BlockSpec((seq, w), lambda b, h: (b, col("rg") + h)),
            pl.BlockSpec((ctx_len, w), lambda b, h: (b, col("rk") + h)),
            pl.BlockSpec((ctx_len, w), lambda b, h: (b, col("rv") + h)),
            pl.BlockSpec((2, None, 8, w), lambda b, h: (0, h, 0, 0)),
            pl.BlockSpec((1, w), lambda b, h: (0, h)),
            cast_spec,
        ],
        out_specs=(pl.BlockSpec((seq, w), lambda b, h: (b, h)), cast_spec),
        scratch_shapes=[
            pltpu.VMEM((seq, RET_DV), F32),
            pltpu.VMEM((2, RET_DK, RET_DV), F32),
            pltpu.VMEM((2, RET_CHUNK, RET_CHUNK), F32),
            pltpu.VMEM((2, RET_CHUNK, RET_DV), F32),
            pltpu.VMEM((2, RET_CHUNK, RET_DK), F32),
            pltpu.VMEM((2, 8, RET_DV), F32),
        ],
        compiler_params=pltpu.CompilerParams(dimension_semantics=("arbitrary", "arbitrary"),
                                             vmem_limit_bytes=VMEM_LIMIT),
        name="retention",
    )(p, p, p, p, pc, pc, rd_b, ret_gn, w_later)


GLA_SUB = 64
GLA_BLK = 256


def _gla_kernel(q_ref, k_ref, v_ref, g_ref, ga_ref, kc_ref, vc_ref, gac_ref, up_ref, bias_ref, gn_ref,
                wsrc_ref, o_ref, wdst_ref, qf_scr, qb_scr, khf_scr, khb_scr, khcf_scr, khcb_scr,
                dec_scr, decc_scr, o_scr, s_scr, *, n_blk):
    blk, sub, dk = GLA_BLK, GLA_SUB, GLA_DK
    wdst_ref[...] = wsrc_ref[...].astype(wdst_ref.dtype)
    nsub = blk // sub
    shift = sub.bit_length() - 1
    scale = GLA_DK ** -0.5

    ii = lax.broadcasted_iota(jnp.int32, (blk, blk), 0)
    jj = lax.broadcasted_iota(jnp.int32, (blk, blk), 1)
    same = (ii >> shift) == (jj >> shift)
    lower = jj <= ii
    tmat = jnp.where(same & lower, 1.0, 0.0).astype(BF16)
    r8 = lax.broadcasted_iota(jnp.int32, (8, dk), 0)

    u = up_ref[...]
    u_hi, u_lo = _split_bf16(u)
    urow = lax.broadcasted_iota(jnp.int32, u.shape, 0)
    rhs = jnp.where(urow >= 2 * GA_W, u_lo, u_hi)
    bias = bias_ref[...]

    def prep(ga_blk, k_blk, q_blk, v_blk):
        z = _dot(ga_blk, rhs) + bias
        a = (jnp.minimum(z, 0.0) - jnp.log(1.0 + jnp.exp(-jnp.abs(z)))) * (1.0 / GLA_GATE_NORM)
        a_hi, a_lo = _split_bf16(a)
        pre = _dot(tmat, a_hi) + _dot(tmat, a_lo)
        lasts = [pre[i * sub + sub - 1:i * sub + sub, :] for i in range(nsub)]
        tot = jnp.concatenate([jnp.broadcast_to(l, (sub, 2 * dk)) for l in lasts], axis=0)
        b_f = pre[:, :dk]
        ex_b = pre[:, dk:] - a[:, dk:]
        b_b = tot[:, dk:] - ex_b
        kf = k_blk.astype(F32)
        khf = (kf * jnp.exp(tot[:, :dk] - b_f)).astype(BF16)
        khb = (kf * jnp.exp(ex_b)).astype(BF16)

        def tile8(half):
            rows = [jnp.broadcast_to(l[:, half * dk:(half + 1) * dk], (8, dk)) for l in lasts]
            t = rows[nsub - 1]
            for i in range(nsub - 2, -1, -1):
                t = jnp.where(r8 == i, rows[i], t)
            return t

        dec = jnp.exp(jnp.concatenate([tile8(0), tile8(1), jnp.zeros((LANE - 16, dk), F32)], axis=0))
        dec_t = dec.T
        if q_blk is None:
            return khf, khb, dec_t
        qf32 = q_blk.astype(F32)
        qf = (qf32 * jnp.exp(b_f)).astype(BF16)
        qb = (qf32 * jnp.exp(b_b)).astype(BF16)
        ktf = (kf * jnp.exp(-b_f)).astype(BF16)
        ktb = (kf * jnp.exp(-b_b)).astype(BF16)
        s = jnp.where(same, jnp.where(lower, _dot_nt(qf, ktf), _dot_nt(qb, ktb)), 0.0).astype(BF16)
        return khf, khb, dec_t, qf, qb, _dot(s, v_blk)

    khf, khb, dec_t = prep(gac_ref[...], kc_ref[...], None, None)
    khcf_scr[...] = khf
    khcb_scr[...] = khb
    decc_scr[...] = dec_t

    def prep_body(t, carry):
        r = pl.multiple_of(t * blk, blk)
        khf, khb, dec_t, qf, qb, o_intra = prep(ga_ref[pl.ds(r, blk), :], k_ref[pl.ds(r, blk), :],
                                                q_ref[pl.ds(r, blk), :], v_ref[pl.ds(r, blk), :])
        khf_scr[pl.ds(r, blk), :] = khf
        khb_scr[pl.ds(r, blk), :] = khb
        qf_scr[pl.ds(r, blk), :] = qf
        qb_scr[pl.ds(r, blk), :] = qb
        o_scr[pl.ds(r, blk), :] = o_intra
        dec_scr[t] = dec_t
        return carry

    lax.fori_loop(0, n_blk, prep_body, 0, unroll=4)

    def scan_blk(direction, dec_t, kh_at, v_at, q_at=None, o_rows=None):
        s = s_scr[direction]
        order = range(nsub) if direction == 0 else range(nsub - 1, -1, -1)
        kv = {i: _dot_tn(kh_at(i), v_at(i)) for i in order}
        for i in order:
            col = dec_t[:, direction * 8 + i:direction * 8 + i + 1]
            if q_at is not None:
                rows = o_rows(i)
                o_scr[rows, :] += _dot(q_at(i), s.astype(BF16))
            s = s * col + kv[i]
        s_scr[direction] = s

    s_scr[...] = jnp.zeros(s_scr.shape, F32)
    for direction, khc in ((0, khcf_scr), (1, khcb_scr)):
        scan_blk(direction, decc_scr[...],
                 lambda i, khc=khc: khc[i * sub:(i + 1) * sub, :],
                 lambda i: vc_ref[i * sub:(i + 1) * sub, :])

    def finalize(t):
        r = pl.multiple_of(t * blk, blk)
        nrm = _rms(o_scr[pl.ds(r, blk), :] * scale)
        g = g_ref[pl.ds(r, blk), :].astype(F32)
        o_ref[pl.ds(r, blk), :] = (nrm * gn_ref[...] * (g * _sigmoid(g))).astype(o_ref.dtype)

    def scan_body(t, carry, done):
        for direction, q_scr, kh_scr in ((0, qf_scr, khf_scr), (1, qb_scr, khb_scr)):
            tb = t if direction == 0 else n_blk - 1 - t

            def rows(i, tb=tb):
                return pl.ds(pl.multiple_of(tb * blk + i * sub, sub), sub)

            scan_blk(direction, dec_scr[tb],
                     lambda i, kh_scr=kh_scr, rows=rows: kh_scr[rows(i), :],
                     lambda i, rows=rows: v_ref[rows(i), :],
                     lambda i, q_scr=q_scr, rows=rows: q_scr[rows(i), :],
                     rows)
        if done:
            finalize(t)
            finalize(n_blk - 1 - t)
        return carry

    lax.fori_loop(0, n_blk // 2, functools.partial(scan_body, done=False), 0, unroll=2)
    lax.fori_loop(n_blk // 2, n_blk, functools.partial(scan_body, done=True), 0, unroll=2)


def _gla(p, ga, pc, gac, up_stack, bias2, gla_gn, w_later, *, batch, seq, ctx_len):
    assert ctx_len == GLA_BLK and (seq // GLA_BLK) % 2 == 0
    n_blk = seq // GLA_BLK
    kern = functools.partial(_gla_kernel, n_blk=n_blk)
    dk, dv = GLA_DK, GLA_DV
    cast_spec = pl.BlockSpec(*_cast_plan(w_later, batch, GLA_HEADS))
    return pl.pallas_call(
        kern,
        out_shape=(jax.ShapeDtypeStruct((batch * seq, GLA_VW), BF16),
                   jax.ShapeDtypeStruct(w_later.shape, BF16)),
        grid=(batch, GLA_HEADS),
        in_specs=[
            pl.BlockSpec((seq, dk), lambda b, h: (b, _DST["gq"] // dk + h)),
            pl.BlockSpec((seq, dk), lambda b, h: (b, _DST["gk"] // dk + h)),
            pl.BlockSpec((seq, dv), lambda b, h: (b, _DST["gv"] // dv + h)),
            pl.BlockSpec((seq, dv), lambda b, h: (b, _DST["gg"] // dv + h)),
            pl.BlockSpec((seq, LANE), lambda b, h: (b, 0)),
            pl.BlockSpec((ctx_len, dk), lambda b, h: (b, _DST["gk"] // dk + h)),
            pl.BlockSpec((ctx_len, dv), lambda b, h: (b, _DST["gv"] // dv + h)),
            pl.BlockSpec((ctx_len, LANE), lambda b, h: (b, 0)),
            pl.BlockSpec((LANE, 2 * dk), lambda b, h: (0, h)),
            pl.BlockSpec((1, 2 * dk), lambda b, h: (0, h)),
            pl.BlockSpec((1, dv), lambda b, h: (0, h)),
            cast_spec,
        ],
        out_specs=(pl.BlockSpec((seq, dv), lambda b, h: (b, h)), cast_spec),
        scratch_shapes=[
            pltpu.VMEM((seq, dk), BF16), pltpu.VMEM((seq, dk), BF16),
            pltpu.VMEM((seq, dk), BF16), pltpu.VMEM((seq, dk), BF16),
            pltpu.VMEM((ctx_len, dk), BF16), pltpu.VMEM((ctx_len, dk), BF16),
            pltpu.VMEM((n_blk, dk, LANE), F32), pltpu.VMEM((dk, LANE), F32),
            pltpu.VMEM((seq, dv), F32),
            pltpu.VMEM((2, dk, dv), F32),
        ],
        compiler_params=pltpu.CompilerParams(dimension_semantics=("arbitrary", "arbitrary"),
                                             vmem_limit_bytes=VMEM_LIMIT),
        name="gla",
    )(p, p, p, p, ga, pc, pc, gac, up_stack, bias2, gla_gn, w_later)


def _post_kernel(or_ref, og_ref, ga0_ref, ga1_ref, gb0_ref, gb1_ref, x_ref, wur_ref, wug_ref, wo_ref,
                 g1_ref, sh2_ref, sc2_ref, npost_ref, npre_ref, hs_ref, h2_ref):
    y_ret = _dot(or_ref[...], wur_ref[...])
    y_gla = _dot(og_ref[...], wug_ref[...])
    half = ga0_ref.shape[1]

    def merge(ga_r, gb_r, lo):
        return (_sigmoid(ga_r[...].astype(F32)) * y_ret[:, lo:lo + half]
                + _sigmoid(gb_r[...].astype(F32)) * y_gla[:, lo:lo + half]).astype(BF16)

    merged = jnp.concatenate([merge(ga0_ref, gb0_ref, 0), merge(ga1_ref, gb1_ref, half)], axis=1)
    y = _dot(merged, wo_ref[...])
    hs = x_ref[...] + g1_ref[...] * (_rms(y) * npost_ref[...])
    hs_ref[...] = hs
    h2 = _rms(hs) * npre_ref[...] * (1.0 + sc2_ref[...]) + sh2_ref[...]
    h2_ref[...] = h2.astype(h2_ref.dtype)


def _post(o_r, o_g, p, x2d, w_up_ret, w_up_gla, w_out, mod3, npost, npre, *, tm, seq):
    m, d = x2d.shape
    tiles_per_seq = seq // tm
    const = dict(pipeline_mode=pl.Buffered(1))
    gw = d // 2

    def modspec(chunk):
        return pl.BlockSpec((None, 1, d), lambda i: (i // tiles_per_seq, 0, chunk))

    def gatespec(name, part):
        return pl.BlockSpec((tm, gw), lambda i: (i, _DST[name] // gw + part))

    return pl.pallas_call(
        _post_kernel,
        out_shape=(jax.ShapeDtypeStruct((m, d), F32), jax.ShapeDtypeStruct((m, d), BF16)),
        grid=(m // tm,),
        in_specs=[
            pl.BlockSpec((tm, RET_VW), lambda i: (i, 0)),
            pl.BlockSpec((tm, GLA_VW), lambda i: (i, 0)),
            gatespec("gate_a", 0), gatespec("gate_a", 1), gatespec("gate_b", 0), gatespec("gate_b", 1),
            pl.BlockSpec((tm, d), lambda i: (i, 0)),
            pl.BlockSpec((RET_VW, d), lambda i: (0, 0), **const),
            pl.BlockSpec((GLA_VW, d), lambda i: (0, 0), **const),
            pl.BlockSpec((d, d), lambda i: (0, 0), **const),
            modspec(2), modspec(3), modspec(4),
            pl.BlockSpec((1, d), lambda i: (0, 0)),
            pl.BlockSpec((1, d), lambda i: (0, 0)),
        ],
        out_specs=(pl.BlockSpec((tm, d), lambda i: (i, 0)),
                   pl.BlockSpec((tm, d), lambda i: (i, 0))),
        compiler_params=pltpu.CompilerParams(dimension_semantics=("arbitrary",),
                                             vmem_limit_bytes=VMEM_LIMIT),
        name="post",
    )(o_r, o_g, p, p, p, p, x2d, w_up_ret, w_up_gla, w_out, mod3, mod3, mod3, npost, npre)


MXU_N = 256


def _ffn_up_kernel(h_ref, wg_ref, wu_ref, wsrc_ref, a_ref, wdst_ref):
    wdst_ref[...] = wsrc_ref[...].astype(wdst_ref.dtype)
    h = h_ref[...]
    for lo in range(0, a_ref.shape[1], MXU_N):
        g = _dot(h, wg_ref[:, lo:lo + MXU_N])
        u = _dot(h, wu_ref[:, lo:lo + MXU_N])
        a_ref[:, lo:lo + MXU_N] = (g * _sigmoid(g) * u).astype(a_ref.dtype)


def _ffn_up(h2, wg, wu, w_later, *, tm, tf):
    m, d = h2.shape
    d_ff = wg.shape[1]
    grid = (m // tm, d_ff // tf)
    cast_spec = pl.BlockSpec(*_cast_plan(w_later, *grid))
    return pl.pallas_call(
        _ffn_up_kernel,
        out_shape=(jax.ShapeDtypeStruct((m, d_ff), BF16), jax.ShapeDtypeStruct(w_later.shape, BF16)),
        grid=grid,
        in_specs=[
            pl.BlockSpec((tm, d), lambda i, f: (i, 0)),
            pl.BlockSpec((d, tf), lambda i, f: (0, f)),
            pl.BlockSpec((d, tf), lambda i, f: (0, f)),
            cast_spec,
        ],
        out_specs=(pl.BlockSpec((tm, tf), lambda i, f: (i, f)), cast_spec),
        compiler_params=pltpu.CompilerParams(dimension_semantics=("arbitrary", "arbitrary"),
                                             vmem_limit_bytes=VMEM_LIMIT),
        name="ffn_up",
    )(h2, wg, wu, w_later)


def _ffn_down_kernel(a_ref, wd_ref, hs_ref, g2_ref, nw_ref, o_ref, y_scr, *, n_split):
    tm, d = o_ref.shape
    w = d // n_split
    ss = jnp.zeros((tm, 1), F32)
    for lo in range(0, d, w):
        y = _dot(a_ref[...], wd_ref[:, lo:lo + w])
        y_scr[:, lo:lo + w] = y
        ss = ss + jnp.sum(y * y, axis=-1, keepdims=True)
    inv = lax.rsqrt(ss * (1.0 / d) + EPS)
    o_ref[...] = hs_ref[...] + g2_ref[...] * (y_scr[...] * inv * nw_ref[...])


def _ffn_down(a, hs, wd, mod3, npost, *, tm, seq):
    m, d = hs.shape
    d_ff = wd.shape[0]
    tiles_per_seq = seq // tm
    kern = functools.partial(_ffn_down_kernel, n_split=4)
    return pl.pallas_call(
        kern,
        out_shape=jax.ShapeDtypeStruct((m, d), F32),
        grid=(m // tm,),
        in_specs=[
            pl.BlockSpec((tm, d_ff), lambda i: (i, 0)),
            pl.BlockSpec((d_ff, d), lambda i: (0, 0), pipeline_mode=pl.Buffered(1)),
            pl.BlockSpec((tm, d), lambda i: (i, 0)),
            pl.BlockSpec((None, 1, d), lambda i: (i // tiles_per_seq, 0, 5)),
            pl.BlockSpec((1, d), lambda i: (0, 0)),
        ],
        out_specs=pl.BlockSpec((tm, d), lambda i: (i, 0)),
        scratch_shapes=[pltpu.VMEM((tm, d), F32)],
        compiler_params=pltpu.CompilerParams(dimension_semantics=("arbitrary",),
                                             vmem_limit_bytes=VMEM_LIMIT),
        name="ffn_down",
    )(a, wd, hs, mod3, npost)


def _rope_tables(seq):
    rows = seq // GRID_W
    pos_r = jnp.repeat(jnp.arange(rows, dtype=F32), GRID_W)
    pos_c = jnp.tile(jnp.arange(GRID_W, dtype=F32), rows)
    n_f = RET_DK // 4
    inv = ROPE_BASE ** (-jnp.arange(n_f, dtype=F32) / n_f)
    ang = jnp.concatenate([pos_r[:, None] * inv, pos_c[:, None] * inv], axis=-1)
    return jnp.cos(ang), jnp.sin(ang)


def _gate_map_layout(gla_a_up, gla_a_bias):
    r, h, dk = GLA_LOW_RANK, GLA_HEADS, GLA_DK
    u = jnp.zeros((GA_W, h, 2, dk), F32)
    u = u.at[:r, :, 0, :].set(gla_a_up[0].reshape(r, h, dk))
    u = u.at[r:, :, 1, :].set(gla_a_up[1].reshape(r, h, dk))
    u = u.reshape(GA_W, h * 2 * dk)
    up_stack = jnp.concatenate([u, u, u, jnp.zeros((LANE - 3 * GA_W, h * 2 * dk), F32)], axis=0)
    bias2 = jnp.stack([gla_a_bias[0].reshape(h, dk), gla_a_bias[1].reshape(h, dk)], axis=1)
    return up_stack, bias2.reshape(1, h * 2 * dk)


def _layer(h_state, c_rows, ctx2d, w_mod, b_mod, norm_mix_pre, norm_mix_post, norm_ffn_pre, norm_ffn_post,
           w_in, ret_decay, gla_a_up, gla_a_bias, ret_gn, gla_gn, w_up_ret, w_up_gla, w_out,
           ffn_w_gate, ffn_w_up, ffn_w_down, cos, sin, *, batch, seq, ctx_len):
    d = D_MODEL
    ga0 = _SRC["ga"][0]
    w_t = w_in.T
    w_g = w_t[ga0:ga0 + GA_W]
    w_ga = jnp.concatenate([w_g, w_g, w_g, jnp.zeros((LANE - 3 * GA_W, d), F32)], axis=0).astype(BF16)
    up_stack, bias2 = _gate_map_layout(gla_a_up, gla_a_bias)
    rd_b = jnp.broadcast_to(ret_decay.reshape(2, RET_HEADS, 1, 1), (2, RET_HEADS, 8, RET_DK))

    w_main, mod = _prologue(w_t, c_rows, w_mod, b_mod)
    mod3 = mod.reshape(mod.shape[0], 1, 6 * d)
    nw_pre = norm_mix_pre.reshape(1, d)

    tm = 1024
    p, ga, wur, wug, wo = _inproj(
        h_state, nw_pre, mod3, lambda i: i // (seq // tm), w_main, w_ga, cos, sin,
        tm=tm, tn=1024, ncols=N_MAIN, rope=True, seq_tiles=seq // tm, casts=(w_up_ret, w_up_gla, w_out))
    pc, gac = _inproj(ctx2d, nw_pre, mod3, lambda i: batch, w_main, w_ga, cos, sin,
                      tm=batch * ctx_len, tn=512, ncols=N_CTX, rope=False, seq_tiles=1)

    o_r, wg = _retention(p, pc, rd_b, ret_gn.reshape(1, RET_VW), ffn_w_gate,
                         batch=batch, seq=seq, ctx_len=ctx_len)
    o_g, wu = _gla(p, ga, pc, gac, up_stack, bias2, gla_gn.reshape(1, GLA_VW), ffn_w_up,
                   batch=batch, seq=seq, ctx_len=ctx_len)

    hs, h2 = _post(o_r, o_g, p, h_state, wur, wug, wo,
                   mod3, norm_mix_post.reshape(1, d), norm_ffn_pre.reshape(1, d), tm=256, seq=seq)
    act, wd = _ffn_up(h2, wg, wu, ffn_w_down, tm=1024, tf=512)
    return _ffn_down(act, hs, wd, mod3, norm_ffn_post.reshape(1, d), tm=256, seq=seq)


def kernel(x, c, ctx, c_ctx, w_mod, b_mod, norm_mix_pre, norm_mix_post, norm_ffn_pre, norm_ffn_post,
           w_in, ret_decay, gla_a_up, gla_a_bias, ret_gn, gla_gn, w_up_ret, w_up_gla, w_out,
           ffn_w_gate, ffn_w_up, ffn_w_down):
    batch, seq, d = x.shape
    ctx_len = ctx.shape[1]
    depth = w_mod.shape[0]
    cos, sin = _rope_tables(seq)
    c_rows = jnp.zeros((16, d), F32).at[:batch].set(c.astype(F32)).at[batch].set(c_ctx.astype(F32))
    ctx2d = ctx.astype(F32).reshape(batch * ctx_len, d)
    h_state = x.astype(F32).reshape(batch * seq, d)
    for i in range(depth):
        h_state = _layer(h_state, c_rows, ctx2d, w_mod[i], b_mod[i], norm_mix_pre[i], norm_mix_post[i],
                         norm_ffn_pre[i], norm_ffn_post[i], w_in[i], ret_decay[i], gla_a_up[i], gla_a_bias[i],
                         ret_gn[i], gla_gn[i], w_up_ret[i], w_up_gla[i], w_out[i],
                         ffn_w_gate[i], ffn_w_up[i], ffn_w_down[i], cos, sin,
                         batch=batch, seq=seq, ctx_len=ctx_len)
    return h_state.reshape(batch, seq, d).astype(x.dtype)
```

```python
import functools

import jax
import jax.numpy as jnp
import numpy as np
from jax import lax
from jax.experimental import pallas as pl
from jax.experimental.pallas import tpu as pltpu

F32 = jnp.float32
BF16 = jnp.bfloat16

D_MODEL = 2048
GRID_W = 64
RET_HEADS = 4
RET_DK = 256
RET_DV = 256
GLA_HEADS = 4
GLA_DK = 128
GLA_DV = 256
GLA_LOW_RANK = 16
GLA_GATE_NORM = 16.0
ROPE_BASE = 10000.0
EPS = 1e-6

RET_QK = RET_HEADS * RET_DK
RET_VW = RET_HEADS * RET_DV
GLA_KW = GLA_HEADS * GLA_DK
GLA_VW = GLA_HEADS * GLA_DV
GA_W = 2 * GLA_LOW_RANK
LANE = 128

_SRC = {}
_off = 0
for _name, _w in (("rk", RET_QK), ("rv", RET_VW), ("gk", GLA_KW), ("gv", GLA_VW), ("ga", GA_W),
                  ("rq", RET_QK), ("rg", RET_VW), ("gq", GLA_KW), ("gg", GLA_VW),
                  ("gate_a", D_MODEL), ("gate_b", D_MODEL)):
    _SRC[_name] = (_off, _w)
    _off += _w

_ORDER = ("rk", "rv", "gv", "gk", "gq", "rq", "rg", "gg", "gate_a", "gate_b")
_DST = {}
_off = 0
for _name in _ORDER:
    _DST[_name] = _off
    _off += _SRC[_name][1]
N_MAIN = _off
N_CTX = _DST["gq"]
PACK_W = 512

VMEM_LIMIT = 56 * 1024 * 1024


def _dot(a, b):
    return jnp.dot(a, b, preferred_element_type=F32)


def _dot_nt(a, b):
    return lax.dot_general(a, b, (((1,), (1,)), ((), ())), preferred_element_type=F32)


def _dot_tn(a, b):
    return lax.dot_general(a, b, (((0,), (0,)), ((), ())), preferred_element_type=F32)


def _sigmoid(x):
    return 1.0 / (1.0 + jnp.exp(-x))


def _rms(x):
    return x * lax.rsqrt(jnp.mean(x * x, axis=-1, keepdims=True) + EPS)


def _split_bf16(x):
    hi = x.astype(BF16)
    lo = (x - hi.astype(F32)).astype(BF16)
    return hi, lo


MOD_TN = 1024


def _prologue_kernel(a_idx_ref, n_idx_ref, shift_ref, a_ref, n_ref, c_ref, wm_ref, bm_ref,
                     o_ref, mod_ref, *, n_mod):
    j = pl.program_id(0)

    @pl.when(shift_ref[j] == 0)
    def _():
        o_ref[...] = a_ref[...].astype(o_ref.dtype)

    @pl.when(shift_ref[j] != 0)
    def _():
        o_ref[...] = jnp.concatenate([a_ref[GA_W:, :], n_ref[...]], axis=0).astype(o_ref.dtype)

    @pl.when(j < n_mod)
    def _():
        cf = c_ref[...]
        rows = cf.shape[0]
        s_hi, s_lo = _split_bf16(cf * _sigmoid(cf))
        w_hi, w_lo = _split_bf16(wm_ref[...])
        r = _dot(jnp.concatenate([s_hi, s_lo], axis=0), w_hi)
        mod_ref[...] = r[:rows] + r[rows:] + _dot(s_hi, w_lo) + bm_ref[...]


def _prologue(w_t, c_rows, w_mod, b_mod):
    d = w_t.shape[1]
    rows = c_rows.shape[0]
    n_modcols = w_mod.shape[1]
    n_mod = n_modcols // MOD_TN
    a_idx, n_idx, shift = [], [], []
    for name in _ORDER:
        src, width = _SRC[name]
        for c in range(src, src + width, PACK_W):
            base = c - c % PACK_W
            assert c - base in (0, GA_W)
            a_idx.append(base // PACK_W)
            n_idx.append((base + PACK_W) // GA_W if c != base else (n_idx[-1] if n_idx else 0))
            shift.append(c - base)
    n_tiles = len(a_idx)
    assert n_mod <= n_tiles
    as_i32 = lambda v: jnp.asarray(v, jnp.int32)
    mod_tile = lambda j, a, n, s: (0, jnp.minimum(j, n_mod - 1))
    return pl.pallas_call(
        functools.partial(_prologue_kernel, n_mod=n_mod),
        out_shape=(jax.ShapeDtypeStruct((n_tiles * PACK_W, d), BF16),
                   jax.ShapeDtypeStruct((rows, n_modcols), F32)),
        grid_spec=pltpu.PrefetchScalarGridSpec(
            num_scalar_prefetch=3,
            grid=(n_tiles,),
            in_specs=[pl.BlockSpec((PACK_W, d), lambda j, a, n, s: (a[j], 0)),
                      pl.BlockSpec((GA_W, d), lambda j, a, n, s: (n[j], 0)),
                      pl.BlockSpec((rows, d), lambda j, a, n, s: (0, 0)),
                      pl.BlockSpec((d, MOD_TN), mod_tile),
                      pl.BlockSpec((1, MOD_TN), mod_tile)],
            out_specs=(pl.BlockSpec((PACK_W, d), lambda j, a, n, s: (j, 0)),
                       pl.BlockSpec((rows, MOD_TN), mod_tile)),
        ),
        compiler_params=pltpu.CompilerParams(dimension_semantics=("arbitrary",),
                                             vmem_limit_bytes=VMEM_LIMIT),
        name="prologue",
    )(as_i32(a_idx), as_i32(n_idx), as_i32(shift), w_t, w_t, c_rows, w_mod, b_mod.reshape(1, n_modcols))


def _inproj_kernel(x_ref, nw_ref, sh_ref, sc_ref, w_ref, wga_ref, cos_ref, sin_ref, *rest,
                   rope_tiles, tn, n_cast):
    cast_in = rest[:n_cast]
    o_ref, ga_ref = rest[n_cast:n_cast + 2]
    cast_out = rest[n_cast + 2:2 * n_cast + 2]
    h_scr = rest[2 * n_cast + 2]
    j = pl.program_id(1)

    def ride_along_casts():
        for src, dst in zip(cast_in, cast_out):
            dst[...] = src[...].astype(dst.dtype)

    @pl.when(j == 0)
    def _():
        h = _rms(x_ref[...]) * nw_ref[...] * (1.0 + sc_ref[...]) + sh_ref[...]
        hb = h.astype(BF16)
        h_scr[...] = hb
        g3 = _dot_nt(hb, wga_ref[...])
        lane = lax.broadcasted_iota(jnp.int32, g3.shape, 1)
        resid = g3 - g3.astype(BF16).astype(F32)
        ga_ref[...] = jnp.where((lane >= GA_W) & (lane < 2 * GA_W), resid, g3).astype(ga_ref.dtype)

    if not rope_tiles:
        ride_along_casts()
        o_ref[...] = _dot_nt(h_scr[...], w_ref[...]).astype(o_ref.dtype)
        return

    is_rope = functools.reduce(jnp.logical_or, [j == t for t in rope_tiles])

    @pl.when(is_rope)
    def _():
        ride_along_casts()
        cos = cos_ref[...]
        sin = sin_ref[...]
        half = RET_DK // 2
        for hd in range(0, tn, RET_DK):
            acc = _dot_nt(h_scr[...], w_ref[hd:hd + RET_DK, :])
            t1 = acc[:, :half]
            t2 = acc[:, half:]
            o_ref[:, hd:hd + half] = (t1 * cos - t2 * sin).astype(o_ref.dtype)
            o_ref[:, hd + half:hd + RET_DK] = (t1 * sin + t2 * cos).astype(o_ref.dtype)

    @pl.when(jnp.logical_not(is_rope))
    def _():
        ride_along_casts()
        o_ref[...] = _dot_nt(h_scr[...], w_ref[...]).astype(o_ref.dtype)


def _cast_plan(arr, n_i, n_j):
    rows, cols = arr.shape

    def pieces(extent, want, align):
        n = want
        while extent % n or (extent // n) % align:
            n -= 1
        return n

    if rows % n_j == 0 and (rows // n_j) % 16 == 0:
        nr, nc = n_j, pieces(cols, n_i, LANE)
        index = lambda i, j, nc=nc: (j, jnp.minimum(i, nc - 1))
    else:
        nr, nc = pieces(rows, n_i, 16), pieces(cols, n_j, LANE)
        index = lambda i, j, nr=nr, nc=nc: (jnp.minimum(i, nr - 1), jnp.minimum(j, nc - 1))
    return (rows // nr, cols // nc), index


def _inproj(x2d, norm_w, mod3, mod_row_of_tile, w_main, w_ga, cos, sin, *, tm, tn, ncols, rope, seq_tiles,
            casts=()):
    m, d = x2d.shape
    rope_tiles = ()
    if rope:
        assert tn == RET_QK and _DST["rk"] % tn == 0 and _DST["rq"] % tn == 0
        rope_tiles = (_DST["rk"] // tn, _DST["rq"] // tn)
    grid = (m // tm, ncols // tn)
    cast_specs = [pl.BlockSpec(*_cast_plan(a, *grid)) for a in casts]
    kern = functools.partial(_inproj_kernel, rope_tiles=rope_tiles, tn=tn, n_cast=len(casts))
    return pl.pallas_call(
        kern,
        out_shape=(jax.ShapeDtypeStruct((m, ncols), BF16),
                   jax.ShapeDtypeStruct((m, LANE), BF16),
                   *[jax.ShapeDtypeStruct(a.shape, BF16) for a in casts]),
        grid=grid,
        in_specs=[
            pl.BlockSpec((tm, d), lambda i, j: (i, 0)),
            pl.BlockSpec((1, d), lambda i, j: (0, 0)),
            pl.BlockSpec((None, 1, d), lambda i, j: (mod_row_of_tile(i), 0, 0)),
            pl.BlockSpec((None, 1, d), lambda i, j: (mod_row_of_tile(i), 0, 1)),
            pl.BlockSpec((tn, d), lambda i, j: (j, 0)),
            pl.BlockSpec((LANE, d), lambda i, j: (0, 0)),
            pl.BlockSpec((tm, RET_DK // 2), lambda i, j: (i % seq_tiles, 0)),
            pl.BlockSpec((tm, RET_DK // 2), lambda i, j: (i % seq_tiles, 0)),
            *cast_specs,
        ],
        out_specs=(pl.BlockSpec((tm, tn), lambda i, j: (i, j)),
                   pl.BlockSpec((tm, LANE), lambda i, j: (i, 0)),
                   *cast_specs),
        scratch_shapes=[pltpu.VMEM((tm, d), BF16)],
        compiler_params=pltpu.CompilerParams(dimension_semantics=("arbitrary", "arbitrary"),
                                             vmem_limit_bytes=VMEM_LIMIT),
        name="inproj",
    )(x2d, norm_w, mod3, mod3, w_main, w_ga, cos, sin, *casts)


RET_CHUNK = 256


def _ret_kernel(q_ref, k_ref, v_ref, g_ref, kc_ref, vc_ref, rd_ref, gn_ref, wsrc_ref, o_ref, wdst_ref,
                o_scr, s_scr, dm_scr, eq_scr, wk_scr, sd_scr, *, n_chunks):
    c_len = RET_CHUNK
    wdst_ref[...] = wsrc_ref[...].astype(wdst_ref.dtype)
    ii = lax.broadcasted_iota(jnp.int32, (c_len, c_len), 0)
    jj = lax.broadcasted_iota(jnp.int32, (c_len, c_len), 1)
    rowi = lax.broadcasted_iota(jnp.int32, (c_len, RET_DK), 0).astype(F32)
    scale = RET_DK ** -0.5

    for direction in (0, 1):
        lg = -jnp.exp(rd_ref[direction][0:1, :])
        if direction == 0:
            dm_scr[direction] = jnp.where(jj <= ii, jnp.exp((ii - jj).astype(F32) * lg), 0.0)
            eq_scr[direction] = jnp.exp((rowi + 1.0) * lg)
            wk = jnp.exp((c_len - 1.0 - rowi) * lg)
        else:
            dm_scr[direction] = jnp.where(jj > ii, jnp.exp((jj - ii).astype(F32) * lg), 0.0)
            eq_scr[direction] = jnp.exp((c_len - rowi) * lg)
            wk = jnp.exp(rowi * lg)
        wk_scr[direction] = wk
        sd_scr[direction] = jnp.broadcast_to(jnp.exp(float(c_len) * lg), (8, RET_DV))
        kcw = (kc_ref[...].astype(F32) * wk).astype(BF16)
        s_scr[direction] = _dot_tn(kcw, vc_ref[...])

    def chunk(direction, c, first):
        r0 = pl.multiple_of(c * c_len, c_len)
        q = q_ref[pl.ds(r0, c_len), :]
        k = k_ref[pl.ds(r0, c_len), :]
        v = v_ref[pl.ds(r0, c_len), :]
        kv = _dot_tn((k.astype(F32) * wk_scr[direction]).astype(BF16), v)
        s_state = s_scr[direction]
        sc = (_dot_nt(q, k) * dm_scr[direction]).astype(BF16)
        o = _dot(sc, v) + eq_scr[direction] * _dot(q, s_state.astype(BF16))
        s_scr[direction] = sd_scr[direction][0:1, :] * s_state + kv
        if first:
            o_scr[pl.ds(r0, c_len), :] = o
        else:
            tot = (o_scr[pl.ds(r0, c_len), :] + o) * scale
            mu = jnp.mean(tot, axis=-1, keepdims=True)
            cen = tot - mu
            var = jnp.mean(cen * cen, axis=-1, keepdims=True)
            nrm = cen * lax.rsqrt(var + EPS)
            g = g_ref[pl.ds(r0, c_len), :].astype(F32)
            o_ref[pl.ds(r0, c_len), :] = (nrm * gn_ref[...] * (g * _sigmoid(g))).astype(o_ref.dtype)

    def body(t, carry, first):
        chunk(0, t, first)
        chunk(1, n_chunks - 1 - t, first)
        return carry

    lax.fori_loop(0, n_chunks // 2, functools.partial(body, first=True), 0)
    lax.fori_loop(n_chunks // 2, n_chunks, functools.partial(body, first=False), 0)


def _retention(p, pc, rd_b, ret_gn, w_later, *, batch, seq, ctx_len):
    assert ctx_len == RET_CHUNK and (seq // RET_CHUNK) % 2 == 0
    n_chunks = seq // RET_CHUNK
    w = RET_DK
    kern = functools.partial(_ret_kernel, n_chunks=n_chunks)
    cast_spec = pl.BlockSpec(*_cast_plan(w_later, batch, RET_HEADS))

    def col(name):
        return _DST[name] // w

    return pl.pallas_call(
        kern,
        out_shape=(jax.ShapeDtypeStruct((batch * seq, RET_VW), BF16),
                   jax.ShapeDtypeStruct(w_later.shape, BF16)),
        grid=(batch, RET_HEADS),
        in_specs=[
            pl.BlockSpec((seq, w), lambda b, h: (b, col("rq") + h)),
            pl.BlockSpec((seq, w), lambda b, h: (b, col("rk") + h)),
            pl.BlockSpec((seq, w), lambda b, h: (b, col("rv") + h)),
            pl.BlockSpec((seq, w), lambda b, h: (b, col("rg") + h)),
            pl.BlockSpec((ctx_len, w), lambda b, h: (b, col("rk") + h)),
            pl.BlockSpec((ctx_len, w), lambda b, h: (b, col("rv") + h)),
            pl.BlockSpec((2, None, 8, w), lambda b, h: (0, h, 0, 0)),
            pl.BlockSpec((1, w), lambda b, h: (0, h)),
            cast_spec,
        ],
        out_specs=(pl.BlockSpec((seq, w), lambda b, h: (b, h)), cast_spec),
        scratch_shapes=[
            pltpu.VMEM((seq, RET_DV), F32),
            pltpu.VMEM((2, RET_DK, RET_DV), F32),
            pltpu.VMEM((2, RET_CHUNK, RET_CHUNK), F32),
            pltpu.VMEM((2, RET_CHUNK, RET_DV), F32),
            pltpu.VMEM((2, RET_CHUNK, RET_DK), F32),
            pltpu.VMEM((2, 8, RET_DV), F32),
        ],
        compiler_params=pltpu.CompilerParams(dimension_semantics=("arbitrary", "arbitrary"),
                                             vmem_limit_bytes=VMEM_LIMIT),
        name="retention",
    )(p, p, p, p, pc, pc, rd_b, ret_gn, w_later)


GLA_SUB = 64
GLA_BLK = 256


def _gla_kernel(q_ref, k_ref, v_ref, g_ref, ga_ref, kc_ref, vc_ref, gac_ref, up_ref, bias_ref, gn_ref,
                wsrc_ref, o_ref, wdst_ref, qf_scr, qb_scr, khf_scr, khb_scr, khcf_scr, khcb_scr,
                dec_scr, decc_scr, o_scr, s_scr, *, n_blk):
    blk, sub, dk = GLA_BLK, GLA_SUB, GLA_DK
    wdst_ref[...] = wsrc_ref[...].astype(wdst_ref.dtype)
    nsub = blk // sub
    shift = sub.bit_length() - 1
    scale = GLA_DK ** -0.5

    ii = lax.broadcasted_iota(jnp.int32, (blk, blk), 0)
    jj = lax.broadcasted_iota(jnp.int32, (blk, blk), 1)
    same = (ii >> shift) == (jj >> shift)
    lower = jj <= ii
    tmat = jnp.where(same & lower, 1.0, 0.0).astype(BF16)
    r8 = lax.broadcasted_iota(jnp.int32, (8, dk), 0)

    u = up_ref[...]
    u_hi, u_lo = _split_bf16(u)
    urow = lax.broadcasted_iota(jnp.int32, u.shape, 0)
    rhs = jnp.where(urow >= 2 * GA_W, u_lo, u_hi)
    bias = bias_ref[...]

    def prep(ga_blk, k_blk, q_blk, v_blk):
        z = _dot(ga_blk, rhs) + bias
        a = (jnp.minimum(z, 0.0) - jnp.log(1.0 + jnp.exp(-jnp.abs(z)))) * (1.0 / GLA_GATE_NORM)
        a_hi, a_lo = _split_bf16(a)
        pre = _dot(tmat, a_hi) + _dot(tmat, a_lo)
        lasts = [pre[i * sub + sub - 1:i * sub + sub, :] for i in range(nsub)]
        tot = jnp.concatenate([jnp.broadcast_to(l, (sub, 2 * dk)) for l in lasts], axis=0)
        b_f = pre[:, :dk]
        ex_b = pre[:, dk:] - a[:, dk:]
        b_b = tot[:, dk:] - ex_b
        kf = k_blk.astype(F32)
        khf = (kf * jnp.exp(tot[:, :dk] - b_f)).astype(BF16)
        khb = (kf * jnp.exp(ex_b)).astype(BF16)

        def tile8(half):
            rows = [jnp.broadcast_to(l[:, half * dk:(half + 1) * dk], (8, dk)) for l in lasts]
            t = rows[nsub - 1]
            for i in range(nsub - 2, -1, -1):
                t = jnp.where(r8 == i, rows[i], t)
            return t

        dec = jnp.exp(jnp.concatenate([tile8(0), tile8(1), jnp.zeros((LANE - 16, dk), F32)], axis=0))
        dec_t = dec.T
        if q_blk is None:
            return khf, khb, dec_t
        qf32 = q_blk.astype(F32)
        qf = (qf32 * jnp.exp(b_f)).astype(BF16)
        qb = (qf32 * jnp.exp(b_b)).astype(BF16)
        ktf = (kf * jnp.exp(-b_f)).astype(BF16)
        ktb = (kf * jnp.exp(-b_b)).astype(BF16)
        s = jnp.where(same, jnp.where(lower, _dot_nt(qf, ktf), _dot_nt(qb, ktb)), 0.0).astype(BF16)
        return khf, khb, dec_t, qf, qb, _dot(s, v_blk)

    khf, khb, dec_t = prep(gac_ref[...], kc_ref[...], None, None)
    khcf_scr[...] = khf
    khcb_scr[...] = khb
    decc_scr[...] = dec_t

    def prep_body(t, carry):
        r = pl.multiple_of(t * blk, blk)
        khf, khb, dec_t, qf, qb, o_intra = prep(ga_ref[pl.ds(r, blk), :], k_ref[pl.ds(r, blk), :],
                                                q_ref[pl.ds(r, blk), :], v_ref[pl.ds(r, blk), :])
        khf_scr[pl.ds(r, blk), :] = khf
        khb_scr[pl.ds(r, blk), :] = khb
        qf_scr[pl.ds(r, blk), :] = qf
        qb_scr[pl.ds(r, blk), :] = qb
        o_scr[pl.ds(r, blk), :] = o_intra
        dec_scr[t] = dec_t
        return carry

    lax.fori_loop(0, n_blk, prep_body, 0, unroll=4)

    def scan_blk(direction, dec_t, kh_at, v_at, q_at=None, o_rows=None):
        s = s_scr[direction]
        order = range(nsub) if direction == 0 else range(nsub - 1, -1, -1)
        kv = {i: _dot_tn(kh_at(i), v_at(i)) for i in order}
        for i in order:
            col = dec_t[:, direction * 8 + i:direction * 8 + i + 1]
            if q_at is not None:
                rows = o_rows(i)
                o_scr[rows, :] += _dot(q_at(i), s.astype(BF16))
            s = s * col + kv[i]
        s_scr[direction] = s

    s_scr[...] = jnp.zeros(s_scr.shape, F32)
    for direction, khc in ((0, khcf_scr), (1, khcb_scr)):
        scan_blk(direction, decc_scr[...],
                 lambda i, khc=khc: khc[i * sub:(i + 1) * sub, :],
                 lambda i: vc_ref[i * sub:(i + 1) * sub, :])

    def finalize(t):
        r = pl.multiple_of(t * blk, blk)
        nrm = _rms(o_scr[pl.ds(r, blk), :] * scale)
        g = g_ref[pl.ds(r, blk), :].astype(F32)
        o_ref[pl.ds(r, blk), :] = (nrm * gn_ref[...] * (g * _sigmoid(g))).astype(o_ref.dtype)

    def scan_body(t, carry, done):
        for direction, q_scr, kh_scr in ((0, qf_scr, khf_scr), (1, qb_scr, khb_scr)):
            tb = t if direction == 0 else n_blk - 1 - t

            def rows(i, tb=tb):
                return pl.ds(pl.multiple_of(tb * blk + i * sub, sub), sub)

            scan_blk(direction, dec_scr[tb],
                     lambda i, kh_scr=kh_scr, rows=rows: kh_scr[rows(i), :],
                     lambda i, rows=rows: v_ref[rows(i), :],
                     lambda i, q_scr=q_scr, rows=rows: q_scr[rows(i), :],
                     rows)
        if done:
            finalize(t)
            finalize(n_blk - 1 - t)
        return carry

    lax.fori_loop(0, n_blk // 2, functools.partial(scan_body, done=False), 0, unroll=2)
    lax.fori_loop(n_blk // 2, n_blk, functools.partial(scan_body, done=True), 0, unroll=2)


def _gla(p, ga, pc, gac, up_stack, bias2, gla_gn, w_later, *, batch, seq, ctx_len):
    assert ctx_len == GLA_BLK and (seq // GLA_BLK) % 2 == 0
    n_blk = seq // GLA_BLK
    kern = functools.partial(_gla_kernel, n_blk=n_blk)
    dk, dv = GLA_DK, GLA_DV
    cast_spec = pl.BlockSpec(*_cast_plan(w_later, batch, GLA_HEADS))
    return pl.pallas_call(
        kern,
        out_shape=(jax.ShapeDtypeStruct((batch * seq, GLA_VW), BF16),
                   jax.ShapeDtypeStruct(w_later.shape, BF16)),
        grid=(batch, GLA_HEADS),
        in_specs=[
            pl.BlockSpec((seq, dk), lambda b, h: (b, _DST["gq"] // dk + h)),
            pl.BlockSpec((seq, dk), lambda b, h: (b, _DST["gk"] // dk + h)),
            pl.BlockSpec((seq, dv), lambda b, h: (b, _DST["gv"] // dv + h)),
            pl.BlockSpec((seq, dv), lambda b, h: (b, _DST["gg"] // dv + h)),
            pl.BlockSpec((seq, LANE), lambda b, h: (b, 0)),
            pl.BlockSpec((ctx_len, dk), lambda b, h: (b, _DST["gk"] // dk + h)),
            pl.BlockSpec((ctx_len, dv), lambda b, h: (b, _DST["gv"] // dv + h)),
            pl.BlockSpec((ctx_len, LANE), lambda b, h: (b, 0)),
            pl.BlockSpec((LANE, 2 * dk), lambda b, h: (0, h)),
            pl.BlockSpec((1, 2 * dk), lambda b, h: (0, h)),
            pl.BlockSpec((1, dv), lambda b, h: (0, h)),
            cast_spec,
        ],
        out_specs=(pl.BlockSpec((seq, dv), lambda b, h: (b, h)), cast_spec),
        scratch_shapes=[
            pltpu.VMEM((seq, dk), BF16), pltpu.VMEM((seq, dk), BF16),
            pltpu.VMEM((seq, dk), BF16), pltpu.VMEM((seq, dk), BF16),
            pltpu.VMEM((ctx_len, dk), BF16), pltpu.VMEM((ctx_len, dk), BF16),
            pltpu.VMEM((n_blk, dk, LANE), F32), pltpu.VMEM((dk, LANE), F32),
            pltpu.VMEM((seq, dv), F32),
            pltpu.VMEM((2, dk, dv), F32),
        ],
        compiler_params=pltpu.CompilerParams(dimension_semantics=("arbitrary", "arbitrary"),
                                             vmem_limit_bytes=VMEM_LIMIT),
        name="gla",
    )(p, p, p, p, ga, pc, pc, gac, up_stack, bias2, gla_gn, w_later)


MXU_N = 256


def _post_kernel(or_ref, og_ref, ga0_ref, ga1_ref, gb0_ref, gb1_ref, x_ref, wur_ref, wug_ref, wo_ref,
                 g1_ref, sh2_ref, sc2_ref, npost_ref, npre_ref, hs_ref, h2_ref):
    y_ret = _dot(or_ref[...], wur_ref[...])
    y_gla = _dot(og_ref[...], wug_ref[...])
    half = ga0_ref.shape[1]

    def merge(ga_r, gb_r, lo):
        return (_sigmoid(ga_r[...].astype(F32)) * y_ret[:, lo:lo + half]
                + _sigmoid(gb_r[...].astype(F32)) * y_gla[:, lo:lo + half]).astype(BF16)

    merged = jnp.concatenate([merge(ga0_ref, gb0_ref, 0), merge(ga1_ref, gb1_ref, half)], axis=1)
    y = _dot(merged, wo_ref[...])
    hs = x_ref[...] + g1_ref[...] * (_rms(y) * npost_ref[...])
    hs_ref[...] = hs
    h2 = _rms(hs) * npre_ref[...] * (1.0 + sc2_ref[...]) + sh2_ref[...]
    h2_ref[...] = h2.astype(h2_ref.dtype)


def _post(o_r, o_g, p, x2d, w_up_ret, w_up_gla, w_out, mod3, npost, npre, *, tm, seq):
    m, d = x2d.shape
    tiles_per_seq = seq // tm
    const = dict(pipeline_mode=pl.Buffered(1))
    gw = d // 2

    def modspec(chunk):
        return pl.BlockSpec((None, 1, d), lambda i: (i // tiles_per_seq, 0, chunk))

    def gatespec(name, part):
        return pl.BlockSpec((tm, gw), lambda i: (i, _DST[name] // gw + part))

    return pl.pallas_call(
        _post_kernel,
        out_shape=(jax.ShapeDtypeStruct((m, d), F32), jax.ShapeDtypeStruct((m, d), BF16)),
        grid=(m // tm,),
        in_specs=[
            pl.BlockSpec((tm, RET_VW), lambda i: (i, 0)),
            pl.BlockSpec((tm, GLA_VW), lambda i: (i, 0)),
            gatespec("gate_a", 0), gatespec("gate_a", 1), gatespec("gate_b", 0), gatespec("gate_b", 1),
            pl.BlockSpec((tm, d), lambda i: (i, 0)),
            pl.BlockSpec((RET_VW, d), lambda i: (0, 0), **const),
            pl.BlockSpec((GLA_VW, d), lambda i: (0, 0), **const),
            pl.BlockSpec((d, d), lambda i: (0, 0), **const),
            modspec(2), modspec(3), modspec(4),
            pl.BlockSpec((1, d), lambda i: (0, 0)),
            pl.BlockSpec((1, d), lambda i: (0, 0)),
        ],
        out_specs=(pl.BlockSpec((tm, d), lambda i: (i, 0)),
                   pl.BlockSpec((tm, d), lambda i: (i, 0))),
        compiler_params=pltpu.CompilerParams(dimension_semantics=("arbitrary",),
                                             vmem_limit_bytes=VMEM_LIMIT),
        name="post",
    )(o_r, o_g, p, p, p, p, x2d, w_up_ret, w_up_gla, w_out, mod3, mod3, mod3, npost, npre)


def _ffn_up_kernel(h_ref, wg_ref, wu_ref, wsrc_ref, a_ref, wdst_ref):
    wdst_ref[...] = wsrc_ref[...].astype(wdst_ref.dtype)
    h = h_ref[...]
    for lo in range(0, a_ref.shape[1], MXU_N):
        g = _dot(h, wg_ref[:, lo:lo + MXU_N])
        u = _dot(h, wu_ref[:, lo:lo + MXU_N])
        a_ref[:, lo:lo + MXU_N] = (g * _sigmoid(g) * u).astype(a_ref.dtype)


def _ffn_up(h2, wg, wu, w_later, *, tm, tf):
    m, d = h2.shape
    d_ff = wg.shape[1]
    grid = (m // tm, d_ff // tf)
    cast_spec = pl.BlockSpec(*_cast_plan(w_later, *grid))
    return pl.pallas_call(
        _ffn_up_kernel,
        out_shape=(jax.ShapeDtypeStruct((m, d_ff), BF16), jax.ShapeDtypeStruct(w_later.shape, BF16)),
        grid=grid,
        in_specs=[
            pl.BlockSpec((tm, d), lambda i, f: (i, 0)),
            pl.BlockSpec((d, tf), lambda i, f: (0, f)),
            pl.BlockSpec((d, tf), lambda i, f: (0, f)),
            cast_spec,
        ],
        out_specs=(pl.BlockSpec((tm, tf), lambda i, f: (i, f)), cast_spec),
        compiler_params=pltpu.CompilerParams(dimension_semantics=("arbitrary", "arbitrary"),
                                             vmem_limit_bytes=VMEM_LIMIT),
        name="ffn_up",
    )(h2, wg, wu, w_later)


def _ffn_down_kernel(a_ref, wd_ref, hs_ref, g2_ref, nw_ref, o_ref, y_scr, *, n_split):
    tm, d = o_ref.shape
    w = d // n_split
    ss = jnp.zeros((tm, 1), F32)
    for lo in range(0, d, w):
        y = _dot(a_ref[...], wd_ref[:, lo:lo + w])
        y_scr[:, lo:lo + w] = y
        ss = ss + jnp.sum(y * y, axis=-1, keepdims=True)
    inv = lax.rsqrt(ss * (1.0 / d) + EPS)
    o_ref[...] = hs_ref[...] + g2_ref[...] * (y_scr[...] * inv * nw_ref[...])


def _ffn_down(a, hs, wd, mod3, npost, *, tm, seq):
    m, d = hs.shape
    d_ff = wd.shape[0]
    tiles_per_seq = seq // tm
    kern = functools.partial(_ffn_down_kernel, n_split=4)
    return pl.pallas_call(
        kern,
        out_shape=jax.ShapeDtypeStruct((m, d), F32),
        grid=(m // tm,),
        in_specs=[
            pl.BlockSpec((tm, d_ff), lambda i: (i, 0)),
            pl.BlockSpec((d_ff, d), lambda i: (0, 0), pipeline_mode=pl.Buffered(1)),
            pl.BlockSpec((tm, d), lambda i: (i, 0)),
            pl.BlockSpec((None, 1, d), lambda i: (i // tiles_per_seq, 0, 5)),
            pl.BlockSpec((1, d), lambda i: (0, 0)),
        ],
        out_specs=pl.BlockSpec((tm, d), lambda i: (i, 0)),
        scratch_shapes=[pltpu.VMEM((tm, d), F32)],
        compiler_params=pltpu.CompilerParams(dimension_semantics=("arbitrary",),
                                             vmem_limit_bytes=VMEM_LIMIT),
        name="ffn_down",
    )(a, wd, hs, mod3, npost)


def _rope_tables(seq):
    f32 = np.float32
    rows = seq // GRID_W
    pos_r = np.repeat(np.arange(rows, dtype=f32), GRID_W)
    pos_c = np.tile(np.arange(GRID_W, dtype=f32), rows)
    n_f = RET_DK // 4
    inv = (f32(ROPE_BASE) ** (-np.arange(n_f, dtype=f32) / f32(n_f))).astype(f32)
    ang = np.concatenate([pos_r[:, None] * inv, pos_c[:, None] * inv], axis=-1).astype(f32)
    return jnp.asarray(np.cos(ang), F32), jnp.asarray(np.sin(ang), F32)


def _gate_map_layout(gla_a_up, gla_a_bias):
    r, h, dk = GLA_LOW_RANK, GLA_HEADS, GLA_DK
    u = jnp.zeros((GA_W, h, 2, dk), F32)
    u = u.at[:r, :, 0, :].set(gla_a_up[0].reshape(r, h, dk))
    u = u.at[r:, :, 1, :].set(gla_a_up[1].reshape(r, h, dk))
    u = u.reshape(GA_W, h * 2 * dk)
    up_stack = jnp.concatenate([u, u, u, jnp.zeros((LANE - 3 * GA_W, h * 2 * dk), F32)], axis=0)
    bias2 = jnp.stack([gla_a_bias[0].reshape(h, dk), gla_a_bias[1].reshape(h, dk)], axis=1)
    return up_stack, bias2.reshape(1, h * 2 * dk)


def _layer(h_state, c_rows, ctx2d, w_mod, b_mod, norm_mix_pre, norm_mix_post, norm_ffn_pre, norm_ffn_post,
           w_in, ret_decay, gla_a_up, gla_a_bias, ret_gn, gla_gn, w_up_ret, w_up_gla, w_out,
           ffn_w_gate, ffn_w_up, ffn_w_down, cos, sin, *, batch, seq, ctx_len):
    d = D_MODEL
    ga0 = _SRC["ga"][0]
    w_t = w_in.T
    w_g = w_t[ga0:ga0 + GA_W]
    w_ga = jnp.concatenate([w_g, w_g, w_g, jnp.zeros((LANE - 3 * GA_W, d), F32)], axis=0).astype(BF16)
    up_stack, bias2 = _gate_map_layout(gla_a_up, gla_a_bias)
    rd_b = jnp.broadcast_to(ret_decay.reshape(2, RET_HEADS, 1, 1), (2, RET_HEADS, 8, RET_DK))

    w_main, mod = _prologue(w_t, c_rows, w_mod, b_mod)
    mod3 = mod.reshape(mod.shape[0], 1, 6 * d)
    nw_pre = norm_mix_pre.reshape(1, d)

    tm = 1024
    p, ga, wur, wug, wo = _inproj(
        h_state, nw_pre, mod3, lambda i: i // (seq // tm), w_main, w_ga, cos, sin,
        tm=tm, tn=1024, ncols=N_MAIN, rope=True, seq_tiles=seq // tm, casts=(w_up_ret, w_up_gla, w_out))
    pc, gac = _inproj(ctx2d, nw_pre, mod3, lambda i: batch, w_main, w_ga, cos, sin,
                      tm=batch * ctx_len, tn=512, ncols=N_CTX, rope=False, seq_tiles=1)

    o_r, wg = _retention(p, pc, rd_b, ret_gn.reshape(1, RET_VW), ffn_w_gate,
                         batch=batch, seq=seq, ctx_len=ctx_len)
    o_g, wu = _gla(p, ga, pc, gac, up_stack, bias2, gla_gn.reshape(1, GLA_VW), ffn_w_up,
                   batch=batch, seq=seq, ctx_len=ctx_len)

    hs, h2 = _post(o_r, o_g, p, h_state, wur, wug, wo,
                   mod3, norm_mix_post.reshape(1, d), norm_ffn_pre.reshape(1, d), tm=256, seq=seq)
    act, wd = _ffn_up(h2, wg, wu, ffn_w_down, tm=1024, tf=512)
    return _ffn_down(act, hs, wd, mod3, norm_ffn_post.reshape(1, d), tm=256, seq=seq)


def kernel(x, c, ctx, c_ctx, w_mod, b_mod, norm_mix_pre, norm_mix_post, norm_ffn_pre, norm_ffn_post,
           w_in, ret_decay, gla_a_up, gla_a_bias, ret_gn, gla_gn, w_up_ret, w_up_gla, w_out,
           ffn_w_gate, ffn_w_up, ffn_w_down):
    batch, seq, d = x.shape
    ctx_len = ctx.shape[1]
    depth = w_mod.shape[0]
    cos, sin = _rope_tables(seq)
    c_rows = jnp.zeros((16, d), F32).at[:batch].set(c.astype(F32)).at[batch].set(c_ctx.astype(F32))
    ctx2d = ctx.astype(F32).reshape(batch * ctx_len, d)
    h_state = x.astype(F32).reshape(batch * seq, d)
    for i in range(depth):
        h_state = _layer(h_state, c_rows, ctx2d, w_mod[i], b_mod[i], norm_mix_pre[i], norm_mix_post[i],
                         norm_ffn_pre[i], norm_ffn_post[i], w_in[i], ret_decay[i], gla_a_up[i], gla_a_bias[i],
                         ret_gn[i], gla_gn[i], w_up_ret[i], w_up_gla[i], w_out[i],
                         ffn_w_gate[i], ffn_w_up[i], ffn_w_down[i], cos, sin,
                         batch=batch, seq=seq, ctx_len=ctx_len)
    return h_state.reshape(batch, seq, d).astype(x.dtype)
```

```python
import functools

import jax
import jax.numpy as jnp
import numpy as np
from jax import lax
from jax.experimental import pallas as pl
from jax.experimental.pallas import tpu as pltpu

F32 = jnp.float32
BF16 = jnp.bfloat16

D_MODEL = 2048
GRID_W = 64
RET_HEADS = 4
RET_DK = 256
RET_DV = 256
GLA_HEADS = 4
GLA_DK = 128
GLA_DV = 256
GLA_LOW_RANK = 16
GLA_GATE_NORM = 16.0
ROPE_BASE = 10000.0
EPS = 1e-6

RET_QK = RET_HEADS * RET_DK
RET_VW = RET_HEADS * RET_DV
GLA_KW = GLA_HEADS * GLA_DK
GLA_VW = GLA_HEADS * GLA_DV
GA_W = 2 * GLA_LOW_RANK
LANE = 128

_SRC = {}
_off = 0
for _name, _w in (("rk", RET_QK), ("rv", RET_VW), ("gk", GLA_KW), ("gv", GLA_VW), ("ga", GA_W),
                  ("rq", RET_QK), ("rg", RET_VW), ("gq", GLA_KW), ("gg", GLA_VW),
                  ("gate_a", D_MODEL), ("gate_b", D_MODEL)):
    _SRC[_name] = (_off, _w)
    _off += _w

_ORDER = ("rk", "rv", "gv", "gk", "gq", "rq", "rg", "gg", "gate_a", "gate_b")
_DST = {}
_off = 0
for _name in _ORDER:
    _DST[_name] = _off
    _off += _SRC[_name][1]
N_MAIN = _off
N_CTX = _DST["gq"]
PACK_W = 512

VMEM_LIMIT = 56 * 1024 * 1024


def _dot(a, b):
    return jnp.dot(a, b, preferred_element_type=F32)


def _dot_nt(a, b):
    return lax.dot_general(a, b, (((1,), (1,)), ((), ())), preferred_element_type=F32)


def _dot_tn(a, b):
    return lax.dot_general(a, b, (((0,), (0,)), ((), ())), preferred_element_type=F32)


def _sigmoid(x):
    return 1.0 / (1.0 + jnp.exp(-x))


def _rms(x):
    return x * lax.rsqrt(jnp.mean(x * x, axis=-1, keepdims=True) + EPS)


def _split_bf16(x):
    hi = x.astype(BF16)
    lo = (x - hi.astype(F32)).astype(BF16)
    return hi, lo


MOD_TN = 1024


def _prologue_kernel(a_idx_ref, n_idx_ref, shift_ref, a_ref, n_ref, c_ref, wm_ref, bm_ref,
                     o_ref, mod_ref, *, n_mod):
    j = pl.program_id(0)

    @pl.when(shift_ref[j] == 0)
    def _():
        o_ref[...] = a_ref[...].astype(o_ref.dtype)

    @pl.when(shift_ref[j] != 0)
    def _():
        o_ref[...] = jnp.concatenate([a_ref[GA_W:, :], n_ref[...]], axis=0).astype(o_ref.dtype)

    @pl.when(j < n_mod)
    def _():
        cf = c_ref[...]
        rows = cf.shape[0]
        s_hi, s_lo = _split_bf16(cf * _sigmoid(cf))
        w_hi, w_lo = _split_bf16(wm_ref[...])
        r = _dot(jnp.concatenate([s_hi, s_lo], axis=0), w_hi)
        mod_ref[...] = r[:rows] + r[rows:] + _dot(s_hi, w_lo) + bm_ref[...]


def _prologue(w_t, c_rows, w_mod, b_mod):
    d = w_t.shape[1]
    rows = c_rows.shape[0]
    n_modcols = w_mod.shape[1]
    n_mod = n_modcols // MOD_TN
    a_idx, n_idx, shift = [], [], []
    for name in _ORDER:
        src, width = _SRC[name]
        for c in range(src, src + width, PACK_W):
            base = c - c % PACK_W
            assert c - base in (0, GA_W)
            a_idx.append(base // PACK_W)
            n_idx.append((base + PACK_W) // GA_W if c != base else (n_idx[-1] if n_idx else 0))
            shift.append(c - base)
    n_tiles = len(a_idx)
    assert n_mod <= n_tiles
    as_i32 = lambda v: jnp.asarray(v, jnp.int32)
    mod_tile = lambda j, a, n, s: (0, jnp.minimum(j, n_mod - 1))
    return pl.pallas_call(
        functools.partial(_prologue_kernel, n_mod=n_mod),
        out_shape=(jax.ShapeDtypeStruct((n_tiles * PACK_W, d), BF16),
                   jax.ShapeDtypeStruct((rows, n_modcols), F32)),
        grid_spec=pltpu.PrefetchScalarGridSpec(
            num_scalar_prefetch=3,
            grid=(n_tiles,),
            in_specs=[pl.BlockSpec((PACK_W, d), lambda j, a, n, s: (a[j], 0)),
                      pl.BlockSpec((GA_W, d), lambda j, a, n, s: (n[j], 0)),
                      pl.BlockSpec((rows, d), lambda j, a, n, s: (0, 0)),
                      pl.BlockSpec((d, MOD_TN), mod_tile),
                      pl.BlockSpec((1, MOD_TN), mod_tile)],
            out_specs=(pl.BlockSpec((PACK_W, d), lambda j, a, n, s: (j, 0)),
                       pl.BlockSpec((rows, MOD_TN), mod_tile)),
        ),
        compiler_params=pltpu.CompilerParams(dimension_semantics=("arbitrary",),
                                             vmem_limit_bytes=VMEM_LIMIT),
        name="prologue",
    )(as_i32(a_idx), as_i32(n_idx), as_i32(shift), w_t, w_t, c_rows, w_mod, b_mod.reshape(1, n_modcols))


def _inproj_kernel(x_ref, nw_ref, sh_ref, sc_ref, w_ref, wga_ref, cos_ref, sin_ref, *rest,
                   rope_tiles, tn, n_cast):
    cast_in = rest[:n_cast]
    o_ref, ga_ref = rest[n_cast:n_cast + 2]
    cast_out = rest[n_cast + 2:2 * n_cast + 2]
    h_scr = rest[2 * n_cast + 2]
    j = pl.program_id(1)

    def ride_along_casts():
        for src, dst in zip(cast_in, cast_out):
            dst[...] = src[...].astype(dst.dtype)

    @pl.when(j == 0)
    def _():
        h = _rms(x_ref[...]) * nw_ref[...] * (1.0 + sc_ref[...]) + sh_ref[...]
        hb = h.astype(BF16)
        h_scr[...] = hb
        g3 = _dot_nt(hb, wga_ref[...])
        lane = lax.broadcasted_iota(jnp.int32, g3.shape, 1)
        resid = g3 - g3.astype(BF16).astype(F32)
        ga_ref[...] = jnp.where((lane >= GA_W) & (lane < 2 * GA_W), resid, g3).astype(ga_ref.dtype)

    if not rope_tiles:
        ride_along_casts()
        o_ref[...] = _dot_nt(h_scr[...], w_ref[...]).astype(o_ref.dtype)
        return

    is_rope = functools.reduce(jnp.logical_or, [j == t for t in rope_tiles])

    @pl.when(is_rope)
    def _():
        ride_along_casts()
        cos = cos_ref[...]
        sin = sin_ref[...]
        half = RET_DK // 2
        for hd in range(0, tn, RET_DK):
            acc = _dot_nt(h_scr[...], w_ref[hd:hd + RET_DK, :])
            t1 = acc[:, :half]
            t2 = acc[:, half:]
            o_ref[:, hd:hd + half] = (t1 * cos - t2 * sin).astype(o_ref.dtype)
            o_ref[:, hd + half:hd + RET_DK] = (t1 * sin + t2 * cos).astype(o_ref.dtype)

    @pl.when(jnp.logical_not(is_rope))
    def _():
        ride_along_casts()
        o_ref[...] = _dot_nt(h_scr[...], w_ref[...]).astype(o_ref.dtype)


def _cast_plan(arr, n_i, n_j):
    rows, cols = arr.shape

    def pieces(extent, want, align):
        n = want
        while extent % n or (extent // n) % align:
            n -= 1
        return n

    if rows % n_j == 0 and (rows // n_j) % 16 == 0:
        nr, nc = n_j, pieces(cols, n_i, LANE)
        index = lambda i, j, nc=nc: (j, jnp.minimum(i, nc - 1))
    else:
        nr, nc = pieces(rows, n_i, 16), pieces(cols, n_j, LANE)
        index = lambda i, j, nr=nr, nc=nc: (jnp.minimum(i, nr - 1), jnp.minimum(j, nc - 1))
    return (rows // nr, cols // nc), index


def _inproj(x2d, norm_w, mod3, mod_row_of_tile, w_main, w_ga, cos, sin, *, tm, tn, ncols, rope, seq_tiles,
            casts=()):
    m, d = x2d.shape
    rope_tiles = ()
    if rope:
        assert tn == RET_QK and _DST["rk"] % tn == 0 and _DST["rq"] % tn == 0
        rope_tiles = (_DST["rk"] // tn, _DST["rq"] // tn)
    grid = (m // tm, ncols // tn)
    cast_specs = [pl.BlockSpec(*_cast_plan(a, *grid)) for a in casts]
    kern = functools.partial(_inproj_kernel, rope_tiles=rope_tiles, tn=tn, n_cast=len(casts))
    return pl.pallas_call(
        kern,
        out_shape=(jax.ShapeDtypeStruct((m, ncols), BF16),
                   jax.ShapeDtypeStruct((m, LANE), BF16),
                   *[jax.ShapeDtypeStruct(a.shape, BF16) for a in casts]),
        grid=grid,
        in_specs=[
            pl.BlockSpec((tm, d), lambda i, j: (i, 0)),
            pl.BlockSpec((1, d), lambda i, j: (0, 0)),
            pl.BlockSpec((None, 1, d), lambda i, j: (mod_row_of_tile(i), 0, 0)),
            pl.BlockSpec((None, 1, d), lambda i, j: (mod_row_of_tile(i), 0, 1)),
            pl.BlockSpec((tn, d), lambda i, j: (j, 0)),
            pl.BlockSpec((LANE, d), lambda i, j: (0, 0)),
            pl.BlockSpec((tm, RET_DK // 2), lambda i, j: (i % seq_tiles, 0)),
            pl.BlockSpec((tm, RET_DK // 2), lambda i, j: (i % seq_tiles, 0)),
            *cast_specs,
        ],
        out_specs=(pl.BlockSpec((tm, tn), lambda i, j: (i, j)),
                   pl.BlockSpec((tm, LANE), lambda i, j: (i, 0)),
                   *cast_specs),
        scratch_shapes=[pltpu.VMEM((tm, d), BF16)],
        compiler_params=pltpu.CompilerParams(dimension_semantics=("arbitrary", "arbitrary"),
                                             vmem_limit_bytes=VMEM_LIMIT),
        name="inproj",
    )(x2d, norm_w, mod3, mod3, w_main, w_ga, cos, sin, *casts)


RET_CHUNK = 256


def _ret_kernel(q_ref, k_ref, v_ref, g_ref, kc_ref, vc_ref, rd_ref, gn_ref, wsrc_ref, o_ref, wdst_ref,
                o_scr, s_scr, dm_scr, eq_scr, wk_scr, sd_scr, *, n_chunks):
    c_len = RET_CHUNK
    wdst_ref[...] = wsrc_ref[...].astype(wdst_ref.dtype)
    ii = lax.broadcasted_iota(jnp.int32, (c_len, c_len), 0)
    jj = lax.broadcasted_iota(jnp.int32, (c_len, c_len), 1)
    rowi = lax.broadcasted_iota(jnp.int32, (c_len, RET_DK), 0).astype(F32)
    scale = RET_DK ** -0.5

    for direction in (0, 1):
        lg = -jnp.exp(rd_ref[direction][0:1, :])
        if direction == 0:
            dm_scr[direction] = jnp.where(jj <= ii, jnp.exp((ii - jj).astype(F32) * lg), 0.0)
            eq_scr[direction] = jnp.exp((rowi + 1.0) * lg)
            wk = jnp.exp((c_len - 1.0 - rowi) * lg)
        else:
            dm_scr[direction] = jnp.where(jj > ii, jnp.exp((jj - ii).astype(F32) * lg), 0.0)
            eq_scr[direction] = jnp.exp((c_len - rowi) * lg)
            wk = jnp.exp(rowi * lg)
        wk_scr[direction] = wk
        sd_scr[direction] = jnp.broadcast_to(jnp.exp(float(c_len) * lg), (8, RET_DV))
        kcw = (kc_ref[...].astype(F32) * wk).astype(BF16)
        s_scr[direction] = _dot_tn(kcw, vc_ref[...])

    def chunk(direction, c, first):
        r0 = pl.multiple_of(c * c_len, c_len)
        q = q_ref[pl.ds(r0, c_len), :]
        k = k_ref[pl.ds(r0, c_len), :]
        v = v_ref[pl.ds(r0, c_len), :]
        kv = _dot_tn((k.astype(F32) * wk_scr[direction]).astype(BF16), v)
        s_state = s_scr[direction]
        sc = (_dot_nt(q, k) * dm_scr[direction]).astype(BF16)
        o = _dot(sc, v) + eq_scr[direction] * _dot(q, s_state.astype(BF16))
        s_scr[direction] = sd_scr[direction][0:1, :] * s_state + kv
        if first:
            o_scr[pl.ds(r0, c_len), :] = o
        else:
            tot = (o_scr[pl.ds(r0, c_len), :] + o) * scale
            mu = jnp.mean(tot, axis=-1, keepdims=True)
            cen = tot - mu
            var = jnp.mean(cen * cen, axis=-1, keepdims=True)
            nrm = cen * lax.rsqrt(var + EPS)
            g = g_ref[pl.ds(r0, c_len), :].astype(F32)
            o_ref[pl.ds(r0, c_len), :] = (nrm * gn_ref[...] * (g * _sigmoid(g))).astype(o_ref.dtype)

    def body(t, carry, first):
        chunk(0, t, first)
        chunk(1, n_chunks - 1 - t, first)
        return carry

    lax.fori_loop(0, n_chunks // 2, functools.partial(body, first=True), 0, unroll=4)
    lax.fori_loop(n_chunks // 2, n_chunks, functools.partial(body, first=False), 0, unroll=4)


def _retention(p, pc, rd_b, ret_gn, w_later, *, batch, seq, ctx_len):
    assert ctx_len == RET_CHUNK and (seq // RET_CHUNK) % 2 == 0
    n_chunks = seq // RET_CHUNK
    w = RET_DK
    kern = functools.partial(_ret_kernel, n_chunks=n_chunks)
    cast_spec = pl.BlockSpec(*_cast_plan(w_later, batch, RET_HEADS))

    def col(name):
        return _DST[name] // w

    return pl.pallas_call(
        kern,
        out_shape=(jax.ShapeDtypeStruct((batch * seq, RET_VW), BF16),
                   jax.ShapeDtypeStruct(w_later.shape, BF16)),
        grid=(batch, RET_HEADS),
        in_specs=[
            pl.BlockSpec((seq, w), lambda b, h: (b, col("rq") + h)),
            pl.BlockSpec((seq, w), lambda b, h: (b, col("rk") + h)),
            pl.BlockSpec((seq, w), lambda b, h: (b, col("rv") + h)),
            pl.BlockSpec((seq, w), lambda b, h: (b, col("rg") + h)),
            pl.BlockSpec((ctx_len, w), lambda b, h: (b, col("rk") + h)),
            pl.BlockSpec((ctx_len, w), lambda b, h: (b, col("rv") + h)),
            pl.BlockSpec((2, None, 8, w), lambda b, h: (0, h, 0, 0)),
            pl.BlockSpec((1, w), lambda b, h: (0, h)),
            cast_spec,
        ],
        out_specs=(pl.BlockSpec((seq, w), lambda b, h: (b, h)), cast_spec),
        scratch_shapes=[
            pltpu.VMEM((seq, RET_DV), F32),
            pltpu.VMEM((2, RET_DK, RET_DV), F32),
            pltpu.VMEM((2, RET_CHUNK, RET_CHUNK), F32),
            pltpu.VMEM((2, RET_CHUNK, RET_DV), F32),
            pltpu.VMEM((2, RET_CHUNK, RET_DK), F32),
            pltpu.VMEM((2, 8, RET_DV), F32),
        ],
        compiler_params=pltpu.CompilerParams(dimension_semantics=("arbitrary", "arbitrary"),
                                             vmem_limit_bytes=VMEM_LIMIT),
        name="retention",
    )(p, p, p, p, pc, pc, rd_b, ret_gn, w_later)


GLA_SUB = 64
GLA_BLK = 256
GLA_SCAN_BLK = 512


def _gla_kernel(q_ref, k_ref, v_ref, g_ref, ga_ref, kc_ref, vc_ref, gac_ref, up_ref, bias_ref, gn_ref,
                wsrc_ref, o_ref, wdst_ref, qf_scr, qb_scr, khf_scr, khb_scr, khcf_scr, khcb_scr,
                dec_scr, decc_scr, o_scr, s_scr, *, n_blk):
    blk, sub, dk = GLA_BLK, GLA_SUB, GLA_DK
    wdst_ref[...] = wsrc_ref[...].astype(wdst_ref.dtype)
    nsub = blk // sub
    shift = sub.bit_length() - 1
    scale = GLA_DK ** -0.5

    def masks(n):
        ii = lax.broadcasted_iota(jnp.int32, (n, n), 0)
        jj = lax.broadcasted_iota(jnp.int32, (n, n), 1)
        same = (ii >> shift) == (jj >> shift)
        lower = jj <= ii
        return same, lower, jnp.where(same & lower, 1.0, 0.0).astype(BF16)

    ctx_rows = gac_ref.shape[0]
    blk_masks = masks(blk)
    ctx_masks = blk_masks if ctx_rows == blk else masks(ctx_rows)
    r8 = lax.broadcasted_iota(jnp.int32, (8, dk), 0)

    u = up_ref[...]
    u_hi, u_lo = _split_bf16(u)
    urow = lax.broadcasted_iota(jnp.int32, u.shape, 0)
    rhs = jnp.where(urow >= 2 * GA_W, u_lo, u_hi)
    bias = bias_ref[...]

    def prep(ga_blk, k_blk, q_blk, v_blk, msk):
        same, lower, tmat = msk
        nsub = ga_blk.shape[0] // sub
        assert nsub <= 8
        z = _dot(ga_blk, rhs) + bias
        a = (jnp.minimum(z, 0.0) - jnp.log(1.0 + jnp.exp(-jnp.abs(z)))) * (1.0 / GLA_GATE_NORM)
        a_hi, a_lo = _split_bf16(a)
        pre = _dot(tmat, a_hi) + _dot(tmat, a_lo)
        lasts = [pre[i * sub + sub - 1:i * sub + sub, :] for i in range(nsub)]
        tot = jnp.concatenate([jnp.broadcast_to(l, (sub, 2 * dk)) for l in lasts], axis=0)
        b_f = pre[:, :dk]
        ex_b = pre[:, dk:] - a[:, dk:]
        b_b = tot[:, dk:] - ex_b
        kf = k_blk.astype(F32)
        khf = (kf * jnp.exp(tot[:, :dk] - b_f)).astype(BF16)
        khb = (kf * jnp.exp(ex_b)).astype(BF16)

        def tile8(half):
            rows = [jnp.broadcast_to(l[:, half * dk:(half + 1) * dk], (8, dk)) for l in lasts]
            t = rows[nsub - 1]
            for i in range(nsub - 2, -1, -1):
                t = jnp.where(r8 == i, rows[i], t)
            return t

        dec = jnp.exp(jnp.concatenate([tile8(0), tile8(1), jnp.zeros((LANE - 16, dk), F32)], axis=0))
        dec_t = dec.T
        if q_blk is None:
            return khf, khb, dec_t
        qf32 = q_blk.astype(F32)
        qf = (qf32 * jnp.exp(b_f)).astype(BF16)
        qb = (qf32 * jnp.exp(b_b)).astype(BF16)
        ktf = (kf * jnp.exp(-b_f)).astype(BF16)
        ktb = (kf * jnp.exp(-b_b)).astype(BF16)
        s = jnp.where(same, jnp.where(lower, _dot_nt(qf, ktf), _dot_nt(qb, ktb)), 0.0).astype(BF16)
        return khf, khb, dec_t, qf, qb, _dot(s, v_blk)

    khf, khb, dec_t = prep(gac_ref[...], kc_ref[...], None, None, ctx_masks)
    khcf_scr[...] = khf
    khcb_scr[...] = khb
    decc_scr[...] = dec_t

    def prep_body(t, carry):
        r = pl.multiple_of(t * blk, blk)
        khf, khb, dec_t, qf, qb, o_intra = prep(ga_ref[pl.ds(r, blk), :], k_ref[pl.ds(r, blk), :],
                                                q_ref[pl.ds(r, blk), :], v_ref[pl.ds(r, blk), :], blk_masks)
        khf_scr[pl.ds(r, blk), :] = khf
        khb_scr[pl.ds(r, blk), :] = khb
        qf_scr[pl.ds(r, blk), :] = qf
        qb_scr[pl.ds(r, blk), :] = qb
        o_scr[pl.ds(r, blk), :] = o_intra
        dec_scr[t] = dec_t
        return carry

    lax.fori_loop(0, n_blk, prep_body, 0, unroll=4)

    def scan_blk(direction, n_sub, dec_ts, kh_at, v_at, q_at=None, o_rows=None):
        s = s_scr[direction]
        order = range(n_sub) if direction == 0 else range(n_sub - 1, -1, -1)
        kv = {i: _dot_tn(kh_at(i), v_at(i)) for i in order}
        for i in order:
            c = direction * 8 + i % nsub
            col = dec_ts[i // nsub][:, c:c + 1]
            if q_at is not None:
                rows = o_rows(i)
                o_scr[rows, :] += _dot(q_at(i), s.astype(BF16))
            s = s * col + kv[i]
        s_scr[direction] = s

    s_scr[...] = jnp.zeros(s_scr.shape, F32)
    for direction, khc in ((0, khcf_scr), (1, khcb_scr)):
        scan_blk(direction, ctx_rows // sub, [decc_scr[...]],
                 lambda i, khc=khc: khc[i * sub:(i + 1) * sub, :],
                 lambda i: vc_ref[i * sub:(i + 1) * sub, :])

    sblk = GLA_SCAN_BLK
    per = sblk // blk
    n_sblk = n_blk // per

    def finalize(t):
        r = pl.multiple_of(t * sblk, sblk)
        nrm = _rms(o_scr[pl.ds(r, sblk), :] * scale)
        g = g_ref[pl.ds(r, sblk), :].astype(F32)
        o_ref[pl.ds(r, sblk), :] = (nrm * gn_ref[...] * (g * _sigmoid(g))).astype(o_ref.dtype)

    def scan_body(t, carry, done):
        for direction, q_scr, kh_scr in ((0, qf_scr, khf_scr), (1, qb_scr, khb_scr)):
            tb = t if direction == 0 else n_sblk - 1 - t

            def rows(i, tb=tb):
                return pl.ds(pl.multiple_of(tb * sblk + i * sub, sub), sub)

            scan_blk(direction, sblk // sub, [dec_scr[tb * per + j] for j in range(per)],
                     lambda i, kh_scr=kh_scr, rows=rows: kh_scr[rows(i), :],
                     lambda i, rows=rows: v_ref[rows(i), :],
                     lambda i, q_scr=q_scr, rows=rows: q_scr[rows(i), :],
                     rows)
        if done:
            finalize(t)
            finalize(n_sblk - 1 - t)
        return carry

    lax.fori_loop(0, n_sblk // 2, functools.partial(scan_body, done=False), 0, unroll=2)
    lax.fori_loop(n_sblk // 2, n_sblk, functools.partial(scan_body, done=True), 0, unroll=2)


def _gla(p, ga, pc, gac, up_stack, bias2, gla_gn, w_later, *, batch, seq, ctx_len):
    assert ctx_len % GLA_SUB == 0 and ctx_len <= GLA_BLK and GLA_SCAN_BLK % GLA_BLK == 0
    assert (seq // GLA_SCAN_BLK) % 2 == 0
    n_blk = seq // GLA_BLK
    kern = functools.partial(_gla_kernel, n_blk=n_blk)
    dk, dv = GLA_DK, GLA_DV
    cast_spec = pl.BlockSpec(*_cast_plan(w_later, batch, GLA_HEADS))
    return pl.pallas_call(
        kern,
        out_shape=(jax.ShapeDtypeStruct((batch * seq, GLA_VW), BF16),
                   jax.ShapeDtypeStruct(w_later.shape, BF16)),
        grid=(batch, GLA_HEADS),
        in_specs=[
            pl.BlockSpec((seq, dk), lambda b, h: (b, _DST["gq"] // dk + h)),
            pl.BlockSpec((seq, dk), lambda b, h: (b, _DST["gk"] // dk + h)),
            pl.BlockSpec((seq, dv), lambda b, h: (b, _DST["gv"] // dv + h)),
            pl.BlockSpec((seq, dv), lambda b, h: (b, _DST["gg"] // dv + h)),
            pl.BlockSpec((seq, LANE), lambda b, h: (b, 0)),
            pl.BlockSpec((ctx_len, dk), lambda b, h: (b, _DST["gk"] // dk + h)),
            pl.BlockSpec((ctx_len, dv), lambda b, h: (b, _DST["gv"] // dv + h)),
            pl.BlockSpec((ctx_len, LANE), lambda b, h: (b, 0)),
            pl.BlockSpec((LANE, 2 * dk), lambda b, h: (0, h)),
            pl.BlockSpec((1, 2 * dk), lambda b, h: (0, h)),
            pl.BlockSpec((1, dv), lambda b, h: (0, h)),
            cast_spec,
        ],
        out_specs=(pl.BlockSpec((seq, dv), lambda b, h: (b, h)), cast_spec),
        scratch_shapes=[
            pltpu.VMEM((seq, dk), BF16), pltpu.VMEM((seq, dk), BF16),
            pltpu.VMEM((seq, dk), BF16), pltpu.VMEM((seq, dk), BF16),
            pltpu.VMEM((ctx_len, dk), BF16), pltpu.VMEM((ctx_len, dk), BF16),
            pltpu.VMEM((n_blk, dk, LANE), F32), pltpu.VMEM((dk, LANE), F32),
            pltpu.VMEM((seq, dv), F32),
            pltpu.VMEM((2, dk, dv), F32),
        ],
        compiler_params=pltpu.CompilerParams(dimension_semantics=("arbitrary", "arbitrary"),
                                             vmem_limit_bytes=VMEM_LIMIT),
        name="gla",
    )(p, p, p, p, ga, pc, pc, gac, up_stack, bias2, gla_gn, w_later)


MXU_N = 256


def _post_kernel(or_ref, og_ref, ga0_ref, ga1_ref, gb0_ref, gb1_ref, x_ref, wur_ref, wug_ref, wo_ref,
                 g1_ref, sh2_ref, sc2_ref, npost_ref, npre_ref, hs_ref, h2_ref):
    y_ret = _dot(or_ref[...], wur_ref[...])
    y_gla = _dot(og_ref[...], wug_ref[...])
    half = ga0_ref.shape[1]

    def merge(ga_r, gb_r, lo):
        return (_sigmoid(ga_r[...].astype(F32)) * y_ret[:, lo:lo + half]
                + _sigmoid(gb_r[...].astype(F32)) * y_gla[:, lo:lo + half]).astype(BF16)

    merged = jnp.concatenate([merge(ga0_ref, gb0_ref, 0), merge(ga1_ref, gb1_ref, half)], axis=1)
    y = _dot(merged, wo_ref[...])
    hs = x_ref[...] + g1_ref[...] * (_rms(y) * npost_ref[...])
    hs_ref[...] = hs
    h2 = _rms(hs) * npre_ref[...] * (1.0 + sc2_ref[...]) + sh2_ref[...]
    h2_ref[...] = h2.astype(h2_ref.dtype)


def _post(o_r, o_g, p, x2d, w_up_ret, w_up_gla, w_out, mod3, npost, npre, *, tm, seq):
    m, d = x2d.shape
    tiles_per_seq = seq // tm
    const = dict(pipeline_mode=pl.Buffered(1))
    gw = d // 2

    def modspec(chunk):
        return pl.BlockSpec((None, 1, d), lambda i: (i // tiles_per_seq, 0, chunk))

    def gatespec(name, part):
        return pl.BlockSpec((tm, gw), lambda i: (i, _DST[name] // gw + part))

    return pl.pallas_call(
        _post_kernel,
        out_shape=(jax.ShapeDtypeStruct((m, d), F32), jax.ShapeDtypeStruct((m, d), BF16)),
        grid=(m // tm,),
        in_specs=[
            pl.BlockSpec((tm, RET_VW), lambda i: (i, 0)),
            pl.BlockSpec((tm, GLA_VW), lambda i: (i, 0)),
            gatespec("gate_a", 0), gatespec("gate_a", 1), gatespec("gate_b", 0), gatespec("gate_b", 1),
            pl.BlockSpec((tm, d), lambda i: (i, 0)),
            pl.BlockSpec((RET_VW, d), lambda i: (0, 0), **const),
            pl.BlockSpec((GLA_VW, d), lambda i: (0, 0), **const),
            pl.BlockSpec((d, d), lambda i: (0, 0), **const),
            modspec(2), modspec(3), modspec(4),
            pl.BlockSpec((1, d), lambda i: (0, 0)),
            pl.BlockSpec((1, d), lambda i: (0, 0)),
        ],
        out_specs=(pl.BlockSpec((tm, d), lambda i: (i, 0)),
                   pl.BlockSpec((tm, d), lambda i: (i, 0))),
        compiler_params=pltpu.CompilerParams(dimension_semantics=("arbitrary",),
                                             vmem_limit_bytes=VMEM_LIMIT),
        name="post",
    )(o_r, o_g, p, p, p, p, x2d, w_up_ret, w_up_gla, w_out, mod3, mod3, mod3, npost, npre)


def _ffn_up_kernel(h_ref, wg_ref, wu_ref, wsrc_ref, a_ref, wdst_ref):
    wdst_ref[...] = wsrc_ref[...].astype(wdst_ref.dtype)
    h = h_ref[...]
    for lo in range(0, a_ref.shape[1], MXU_N):
        g = _dot(h, wg_ref[:, lo:lo + MXU_N])
        u = _dot(h, wu_ref[:, lo:lo + MXU_N])
        a_ref[:, lo:lo + MXU_N] = (g * _sigmoid(g) * u).astype(a_ref.dtype)


def _ffn_up(h2, wg, wu, w_later, *, tm, tf):
    m, d = h2.shape
    d_ff = wg.shape[1]
    grid = (m // tm, d_ff // tf)
    cast_spec = pl.BlockSpec(*_cast_plan(w_later, *grid))
    return pl.pallas_call(
        _ffn_up_kernel,
        out_shape=(jax.ShapeDtypeStruct((m, d_ff), BF16), jax.ShapeDtypeStruct(w_later.shape, BF16)),
        grid=grid,
        in_specs=[
            pl.BlockSpec((tm, d), lambda i, f: (i, 0)),
            pl.BlockSpec((d, tf), lambda i, f: (0, f)),
            pl.BlockSpec((d, tf), lambda i, f: (0, f)),
            cast_spec,
        ],
        out_specs=(pl.BlockSpec((tm, tf), lambda i, f: (i, f)), cast_spec),
        compiler_params=pltpu.CompilerParams(dimension_semantics=("arbitrary", "arbitrary"),
                                             vmem_limit_bytes=VMEM_LIMIT),
        name="ffn_up",
    )(h2, wg, wu, w_later)


def _ffn_down_kernel(a_ref, wd_ref, hs_ref, g2_ref, nw_ref, o_ref, y_scr, *, n_split):
    tm, d = o_ref.shape
    w = d // n_split
    ss = jnp.zeros((tm, 1), F32)
    for lo in range(0, d, w):
        y = _dot(a_ref[...], wd_ref[:, lo:lo + w])
        y_scr[:, lo:lo + w] = y
        ss = ss + jnp.sum(y * y, axis=-1, keepdims=True)
    inv = lax.rsqrt(ss * (1.0 / d) + EPS)
    o_ref[...] = hs_ref[...] + g2_ref[...] * (y_scr[...] * inv * nw_ref[...])


def _ffn_down(a, hs, wd, mod3, npost, *, tm, seq):
    m, d = hs.shape
    d_ff = wd.shape[0]
    tiles_per_seq = seq // tm
    kern = functools.partial(_ffn_down_kernel, n_split=4)
    return pl.pallas_call(
        kern,
        out_shape=jax.ShapeDtypeStruct((m, d), F32),
        grid=(m // tm,),
        in_specs=[
            pl.BlockSpec((tm, d_ff), lambda i: (i, 0)),
            pl.BlockSpec((d_ff, d), lambda i: (0, 0), pipeline_mode=pl.Buffered(1)),
            pl.BlockSpec((tm, d), lambda i: (i, 0)),
            pl.BlockSpec((None, 1, d), lambda i: (i // tiles_per_seq, 0, 5)),
            pl.BlockSpec((1, d), lambda i: (0, 0)),
        ],
        out_specs=pl.BlockSpec((tm, d), lambda i: (i, 0)),
        scratch_shapes=[pltpu.VMEM((tm, d), F32)],
        compiler_params=pltpu.CompilerParams(dimension_semantics=("arbitrary",),
                                             vmem_limit_bytes=VMEM_LIMIT),
        name="ffn_down",
    )(a, wd, hs, mod3, npost)


def _rope_tables(seq):
    f32 = np.float32
    rows = seq // GRID_W
    pos_r = np.repeat(np.arange(rows, dtype=f32), GRID_W)
    pos_c = np.tile(np.arange(GRID_W, dtype=f32), rows)
    n_f = RET_DK // 4
    inv = (f32(ROPE_BASE) ** (-np.arange(n_f, dtype=f32) / f32(n_f))).astype(f32)
    ang = np.concatenate([pos_r[:, None] * inv, pos_c[:, None] * inv], axis=-1).astype(f32)
    return jnp.asarray(np.cos(ang), F32), jnp.asarray(np.sin(ang), F32)


def _gate_map_layout(gla_a_up, gla_a_bias):
    r, h, dk = GLA_LOW_RANK, GLA_HEADS, GLA_DK
    u = jnp.zeros((GA_W, h, 2, dk), F32)
    u = u.at[:r, :, 0, :].set(gla_a_up[0].reshape(r, h, dk))
    u = u.at[r:, :, 1, :].set(gla_a_up[1].reshape(r, h, dk))
    u = u.reshape(GA_W, h * 2 * dk)
    up_stack = jnp.concatenate([u, u, u, jnp.zeros((LANE - 3 * GA_W, h * 2 * dk), F32)], axis=0)
    bias2 = jnp.stack([gla_a_bias[0].reshape(h, dk), gla_a_bias[1].reshape(h, dk)], axis=1)
    return up_stack, bias2.reshape(1, h * 2 * dk)


def _layer(h_state, c_rows, ctx2d, w_mod, b_mod, norm_mix_pre, norm_mix_post, norm_ffn_pre, norm_ffn_post,
           w_in, ret_decay, gla_a_up, gla_a_bias, ret_gn, gla_gn, w_up_ret, w_up_gla, w_out,
           ffn_w_gate, ffn_w_up, ffn_w_down, cos, sin, *, batch, seq, ctx_len):
    d = D_MODEL
    ga0 = _SRC["ga"][0]
    w_t = w_in.T
    w_g = w_t[ga0:ga0 + GA_W]
    w_ga = jnp.concatenate([w_g, w_g, w_g, jnp.zeros((LANE - 3 * GA_W, d), F32)], axis=0).astype(BF16)
    up_stack, bias2 = _gate_map_layout(gla_a_up, gla_a_bias)
    rd_b = jnp.broadcast_to(ret_decay.reshape(2, RET_HEADS, 1, 1), (2, RET_HEADS, 8, RET_DK))

    w_main, mod = _prologue(w_t, c_rows, w_mod, b_mod)
    mod3 = mod.reshape(mod.shape[0], 1, 6 * d)
    nw_pre = norm_mix_pre.reshape(1, d)

    tm = 1024
    p, ga, wur, wug, wo = _inproj(
        h_state, nw_pre, mod3, lambda i: i // (seq // tm), w_main, w_ga, cos, sin,
        tm=tm, tn=1024, ncols=N_MAIN, rope=True, seq_tiles=seq // tm, casts=(w_up_ret, w_up_gla, w_out))
    pc, gac = _inproj(ctx2d, nw_pre, mod3, lambda i: batch, w_main, w_ga, cos, sin,
                      tm=batch * ctx_len, tn=512, ncols=N_CTX, rope=False, seq_tiles=1)

    o_r, wg = _retention(p, pc, rd_b, ret_gn.reshape(1, RET_VW), ffn_w_gate,
                         batch=batch, seq=seq, ctx_len=ctx_len)
    o_g, wu = _gla(p, ga, pc, gac, up_stack, bias2, gla_gn.reshape(1, GLA_VW), ffn_w_up,
                   batch=batch, seq=seq, ctx_len=ctx_len)

    hs, h2 = _post(o_r, o_g, p, h_state, wur, wug, wo,
                   mod3, norm_mix_post.reshape(1, d), norm_ffn_pre.reshape(1, d), tm=256, seq=seq)
    act, wd = _ffn_up(h2, wg, wu, ffn_w_down, tm=1024, tf=512)
    return _ffn_down(act, hs, wd, mod3, norm_ffn_post.reshape(1, d), tm=256, seq=seq)


def kernel(x, c, ctx, c_ctx, w_mod, b_mod, norm_mix_pre, norm_mix_post, norm_ffn_pre, norm_ffn_post,
           w_in, ret_decay, gla_a_up, gla_a_bias, ret_gn, gla_gn, w_up_ret, w_up_gla, w_out,
           ffn_w_gate, ffn_w_up, ffn_w_down):
    batch, seq, d = x.shape
    ctx_len = ctx.shape[1]
    depth = w_mod.shape[0]
    cos, sin = _rope_tables(seq)
    c_rows = jnp.zeros((16, d), F32).at[:batch].set(c.astype(F32)).at[batch].set(c_ctx.astype(F32))
    ctx2d = ctx.astype(F32).reshape(batch * ctx_len, d)
    h_state = x.astype(F32).reshape(batch * seq, d)
    for i in range(depth):
        h_state = _layer(h_state, c_rows, ctx2d, w_mod[i], b_mod[i], norm_mix_pre[i], norm_mix_post[i],
                         norm_ffn_pre[i], norm_ffn_post[i], w_in[i], ret_decay[i], gla_a_up[i], gla_a_bias[i],
                         ret_gn[i], gla_gn[i], w_up_ret[i], w_up_gla[i], w_out[i],
                         ffn_w_gate[i], ffn_w_up[i], ffn_w_down[i], cos, sin,
                         batch=batch, seq=seq, ctx_len=ctx_len)
    return h_state.reshape(batch, seq, d).astype(x.dtype)
```

```python
import functools

import jax
import jax.numpy as jnp
import numpy as np
from jax import lax
from jax.experimental import pallas as pl
from jax.experimental.pallas import tpu as pltpu

F32 = jnp.float32
BF16 = jnp.bfloat16

D_MODEL = 2048
GRID_W = 64
RET_HEADS = 4
RET_DK = 256
RET_DV = 256
GLA_HEADS = 4
GLA_DK = 128
GLA_DV = 256
GLA_LOW_RANK = 16
GLA_GATE_NORM = 16.0
ROPE_BASE = 10000.0
EPS = 1e-6

RET_QK = RET_HEADS * RET_DK
RET_VW = RET_HEADS * RET_DV
GLA_KW = GLA_HEADS * GLA_DK
GLA_VW = GLA_HEADS * GLA_DV
GA_W = 2 * GLA_LOW_RANK
LANE = 128

_SRC = {}
_off = 0
for _name, _w in (("rk", RET_QK), ("rv", RET_VW), ("gk", GLA_KW), ("gv", GLA_VW), ("ga", GA_W),
                  ("rq", RET_QK), ("rg", RET_VW), ("gq", GLA_KW), ("gg", GLA_VW),
                  ("gate_a", D_MODEL), ("gate_b", D_MODEL)):
    _SRC[_name] = (_off, _w)
    _off += _w

_ORDER = ("rk", "rv", "gv", "gk", "gq", "rq", "rg", "gg", "gate_a", "gate_b")
_DST = {}
_off = 0
for _name in _ORDER:
    _DST[_name] = _off
    _off += _SRC[_name][1]
N_MAIN = _off
N_CTX = _DST["gq"]
PACK_W = 512

VMEM_LIMIT = 56 * 1024 * 1024


def _dot(a, b):
    return jnp.dot(a, b, preferred_element_type=F32)


def _dot_nt(a, b):
    return lax.dot_general(a, b, (((1,), (1,)), ((), ())), preferred_element_type=F32)


def _dot_tn(a, b):
    return lax.dot_general(a, b, (((0,), (0,)), ((), ())), preferred_element_type=F32)


def _sigmoid(x):
    return 1.0 / (1.0 + jnp.exp(-x))


def _rms(x):
    return x * lax.rsqrt(jnp.mean(x * x, axis=-1, keepdims=True) + EPS)


def _split_bf16(x):
    hi = x.astype(BF16)
    lo = (x - hi.astype(F32)).astype(BF16)
    return hi, lo


MOD_TN = 1024


def _prologue_kernel(a_idx_ref, n_idx_ref, shift_ref, a_ref, n_ref, c_ref, wm_ref, bm_ref,
                     o_ref, mod_ref, *, n_mod):
    j = pl.program_id(0)

    @pl.when(shift_ref[j] == 0)
    def _():
        o_ref[...] = a_ref[...].astype(o_ref.dtype)

    @pl.when(shift_ref[j] != 0)
    def _():
        o_ref[...] = jnp.concatenate([a_ref[GA_W:, :], n_ref[...]], axis=0).astype(o_ref.dtype)

    @pl.when(j < n_mod)
    def _():
        cf = c_ref[...]
        rows = cf.shape[0]
        s_hi, s_lo = _split_bf16(cf * _sigmoid(cf))
        w_hi, w_lo = _split_bf16(wm_ref[...])
        r = _dot(jnp.concatenate([s_hi, s_lo], axis=0), w_hi)
        mod_ref[...] = r[:rows] + r[rows:] + _dot(s_hi, w_lo) + bm_ref[...]


def _prologue(w_t, c_rows, w_mod, b_mod):
    d = w_t.shape[1]
    rows = c_rows.shape[0]
    n_modcols = w_mod.shape[1]
    n_mod = n_modcols // MOD_TN
    a_idx, n_idx, shift = [], [], []
    for name in _ORDER:
        src, width = _SRC[name]
        for c in range(src, src + width, PACK_W):
            base = c - c % PACK_W
            assert c - base in (0, GA_W)
            a_idx.append(base // PACK_W)
            n_idx.append((base + PACK_W) // GA_W if c != base else (n_idx[-1] if n_idx else 0))
            shift.append(c - base)
    n_tiles = len(a_idx)
    assert n_mod <= n_tiles
    as_i32 = lambda v: jnp.asarray(v, jnp.int32)
    mod_tile = lambda j, a, n, s: (0, jnp.minimum(j, n_mod - 1))
    return pl.pallas_call(
        functools.partial(_prologue_kernel, n_mod=n_mod),
        out_shape=(jax.ShapeDtypeStruct((n_tiles * PACK_W, d), BF16),
                   jax.ShapeDtypeStruct((rows, n_modcols), F32)),
        grid_spec=pltpu.PrefetchScalarGridSpec(
            num_scalar_prefetch=3,
            grid=(n_tiles,),
            in_specs=[pl.BlockSpec((PACK_W, d), lambda j, a, n, s: (a[j], 0)),
                      pl.BlockSpec((GA_W, d), lambda j, a, n, s: (n[j], 0)),
                      pl.BlockSpec((rows, d), lambda j, a, n, s: (0, 0)),
                      pl.BlockSpec((d, MOD_TN), mod_tile),
                      pl.BlockSpec((1, MOD_TN), mod_tile)],
            out_specs=(pl.BlockSpec((PACK_W, d), lambda j, a, n, s: (j, 0)),
                       pl.BlockSpec((rows, MOD_TN), mod_tile)),
        ),
        compiler_params=pltpu.CompilerParams(dimension_semantics=("arbitrary",),
                                             vmem_limit_bytes=VMEM_LIMIT),
        name="prologue",
    )(as_i32(a_idx), as_i32(n_idx), as_i32(shift), w_t, w_t, c_rows, w_mod, b_mod.reshape(1, n_modcols))


def _inproj_kernel(x_ref, nw_ref, sh_ref, sc_ref, w_ref, wga_ref, cos_ref, sin_ref, *rest,
                   rope_tiles, tn, n_cast):
    cast_in = rest[:n_cast]
    o_ref, ga_ref = rest[n_cast:n_cast + 2]
    cast_out = rest[n_cast + 2:2 * n_cast + 2]
    h_scr = rest[2 * n_cast + 2]
    j = pl.program_id(1)

    def ride_along_casts():
        for src, dst in zip(cast_in, cast_out):
            dst[...] = src[...].astype(dst.dtype)

    @pl.when(j == 0)
    def _():
        h = _rms(x_ref[...]) * nw_ref[...] * (1.0 + sc_ref[...]) + sh_ref[...]
        hb = h.astype(BF16)
        h_scr[...] = hb
        g3 = _dot_nt(hb, wga_ref[...])
        lane = lax.broadcasted_iota(jnp.int32, g3.shape, 1)
        resid = g3 - g3.astype(BF16).astype(F32)
        ga_ref[...] = jnp.where((lane >= GA_W) & (lane < 2 * GA_W), resid, g3).astype(ga_ref.dtype)

    if not rope_tiles:
        ride_along_casts()
        o_ref[...] = _dot_nt(h_scr[...], w_ref[...]).astype(o_ref.dtype)
        return

    is_rope = functools.reduce(jnp.logical_or, [j == t for t in rope_tiles])

    @pl.when(is_rope)
    def _():
        ride_along_casts()
        cos = cos_ref[...]
        sin = sin_ref[...]
        half = RET_DK // 2
        for hd in range(0, tn, RET_DK):
            acc = _dot_nt(h_scr[...], w_ref[hd:hd + RET_DK, :])
            t1 = acc[:, :half]
            t2 = acc[:, half:]
            o_ref[:, hd:hd + half] = (t1 * cos - t2 * sin).astype(o_ref.dtype)
            o_ref[:, hd + half:hd + RET_DK] = (t1 * sin + t2 * cos).astype(o_ref.dtype)

    @pl.when(jnp.logical_not(is_rope))
    def _():
        ride_along_casts()
        o_ref[...] = _dot_nt(h_scr[...], w_ref[...]).astype(o_ref.dtype)


def _cast_plan(arr, n_i, n_j):
    rows, cols = arr.shape

    def pieces(extent, want, align):
        n = want
        while extent % n or (extent // n) % align:
            n -= 1
        return n

    if rows % n_j == 0 and (rows // n_j) % 16 == 0:
        nr, nc = n_j, pieces(cols, n_i, LANE)
        index = lambda i, j, nc=nc: (j, jnp.minimum(i, nc - 1))
    else:
        nr, nc = pieces(rows, n_i, 16), pieces(cols, n_j, LANE)
        index = lambda i, j, nr=nr, nc=nc: (jnp.minimum(i, nr - 1), jnp.minimum(j, nc - 1))
    return (rows // nr, cols // nc), index


def _inproj(x2d, norm_w, mod3, mod_row_of_tile, w_main, w_ga, cos, sin, *, tm, tn, ncols, rope, seq_tiles,
            casts=()):
    m, d = x2d.shape
    rope_tiles = ()
    if rope:
        assert tn == RET_QK and _DST["rk"] % tn == 0 and _DST["rq"] % tn == 0
        rope_tiles = (_DST["rk"] // tn, _DST["rq"] // tn)
    grid = (m // tm, ncols // tn)
    cast_specs = [pl.BlockSpec(*_cast_plan(a, *grid)) for a in casts]
    kern = functools.partial(_inproj_kernel, rope_tiles=rope_tiles, tn=tn, n_cast=len(casts))
    return pl.pallas_call(
        kern,
        out_shape=(jax.ShapeDtypeStruct((m, ncols), BF16),
                   jax.ShapeDtypeStruct((m, LANE), BF16),
                   *[jax.ShapeDtypeStruct(a.shape, BF16) for a in casts]),
        grid=grid,
        in_specs=[
            pl.BlockSpec((tm, d), lambda i, j: (i, 0)),
            pl.BlockSpec((1, d), lambda i, j: (0, 0)),
            pl.BlockSpec((None, 1, d), lambda i, j: (mod_row_of_tile(i), 0, 0)),
            pl.BlockSpec((None, 1, d), lambda i, j: (mod_row_of_tile(i), 0, 1)),
            pl.BlockSpec((tn, d), lambda i, j: (j, 0)),
            pl.BlockSpec((LANE, d), lambda i, j: (0, 0)),
            pl.BlockSpec((tm, RET_DK // 2), lambda i, j: (i % seq_tiles, 0)),
            pl.BlockSpec((tm, RET_DK // 2), lambda i, j: (i % seq_tiles, 0)),
            *cast_specs,
        ],
        out_specs=(pl.BlockSpec((tm, tn), lambda i, j: (i, j)),
                   pl.BlockSpec((tm, LANE), lambda i, j: (i, 0)),
                   *cast_specs),
        scratch_shapes=[pltpu.VMEM((tm, d), BF16)],
        compiler_params=pltpu.CompilerParams(dimension_semantics=("arbitrary", "arbitrary"),
                                             vmem_limit_bytes=VMEM_LIMIT),
        name="inproj",
    )(x2d, norm_w, mod3, mod3, w_main, w_ga, cos, sin, *casts)


RET_CHUNK = 256


def _ret_kernel(q_ref, k_ref, v_ref, g_ref, kc_ref, vc_ref, rd_ref, gn_ref, wsrc_ref, o_ref, wdst_ref,
                o_scr, s_scr, dm_scr, eq_scr, wk_scr, sd_scr, *, n_chunks):
    c_len = RET_CHUNK
    wdst_ref[...] = wsrc_ref[...].astype(wdst_ref.dtype)
    ii = lax.broadcasted_iota(jnp.int32, (c_len, c_len), 0)
    jj = lax.broadcasted_iota(jnp.int32, (c_len, c_len), 1)
    rowi = lax.broadcasted_iota(jnp.int32, (c_len, RET_DK), 0).astype(F32)
    scale = RET_DK ** -0.5

    for direction in (0, 1):
        lg = -jnp.exp(rd_ref[direction][0:1, :])
        if direction == 0:
            dm_scr[direction] = jnp.where(jj <= ii, jnp.exp((ii - jj).astype(F32) * lg), 0.0)
            eq_scr[direction] = jnp.exp((rowi + 1.0) * lg)
            wk = jnp.exp((c_len - 1.0 - rowi) * lg)
        else:
            dm_scr[direction] = jnp.where(jj > ii, jnp.exp((jj - ii).astype(F32) * lg), 0.0)
            eq_scr[direction] = jnp.exp((c_len - rowi) * lg)
            wk = jnp.exp(rowi * lg)
        wk_scr[direction] = wk
        sd_scr[direction] = jnp.broadcast_to(jnp.exp(float(c_len) * lg), (8, RET_DV))
        kcw = (kc_ref[...].astype(F32) * wk).astype(BF16)
        s_scr[direction] = _dot_tn(kcw, vc_ref[...])

    def chunk(direction, c, first):
        r0 = pl.multiple_of(c * c_len, c_len)
        q = q_ref[pl.ds(r0, c_len), :]
        k = k_ref[pl.ds(r0, c_len), :]
        v = v_ref[pl.ds(r0, c_len), :]
        s_qk = _dot_nt(q, k)
        s_state = s_scr[direction]
        o_state = _dot(q, s_state.astype(BF16))
        yield
        sc = (s_qk * dm_scr[direction]).astype(BF16)
        kw = (k.astype(F32) * wk_scr[direction]).astype(BF16)
        yield
        o = _dot(sc, v) + eq_scr[direction] * o_state
        s_scr[direction] = sd_scr[direction][0:1, :] * s_state + _dot_tn(kw, v)
        yield
        if first:
            o_scr[pl.ds(r0, c_len), :] = o
        else:
            tot = (o_scr[pl.ds(r0, c_len), :] + o) * scale
            mu = jnp.mean(tot, axis=-1, keepdims=True)
            cen = tot - mu
            var = jnp.mean(cen * cen, axis=-1, keepdims=True)
            nrm = cen * lax.rsqrt(var + EPS)
            g = g_ref[pl.ds(r0, c_len), :].astype(F32)
            o_ref[pl.ds(r0, c_len), :] = (nrm * gn_ref[...] * (g * _sigmoid(g))).astype(o_ref.dtype)
        yield

    def run(*stages):
        live = list(stages)
        while live:
            live = [g for g in live if next(g, live) is not live]

    def body(t, carry, first):
        run(chunk(0, t, first), chunk(1, n_chunks - 1 - t, first))
        return carry

    lax.fori_loop(0, n_chunks // 2, functools.partial(body, first=True), 0, unroll=4)
    lax.fori_loop(n_chunks // 2, n_chunks, functools.partial(body, first=False), 0, unroll=4)


def _retention(p, pc, rd_b, ret_gn, w_later, *, batch, seq, ctx_len):
    assert ctx_len == RET_CHUNK and (seq // RET_CHUNK) % 2 == 0
    n_chunks = seq // RET_CHUNK
    w = RET_DK
    kern = functools.partial(_ret_kernel, n_chunks=n_chunks)
    cast_spec = pl.BlockSpec(*_cast_plan(w_later, batch, RET_HEADS))

    def col(name):
        return _DST[name] // w

    return pl.pallas_call(
        kern,
        out_shape=(jax.ShapeDtypeStruct((batch * seq, RET_VW), BF16),
                   jax.ShapeDtypeStruct(w_later.shape, BF16)),
        grid=(batch, RET_HEADS),
        in_specs=[
            pl.BlockSpec((seq, w), lambda b, h: (b, col("rq") + h)),
            pl.BlockSpec((seq, w), lambda b, h: (b, col("rk") + h)),
            pl.BlockSpec((seq, w), lambda b, h: (b, col("rv") + h)),
            pl.BlockSpec((seq, w), lambda b, h: (b, col("rg") + h)),
            pl.BlockSpec((ctx_len, w), lambda b, h: (b, col("rk") + h)),
            pl.BlockSpec((ctx_len, w), lambda b, h: (b, col("rv") + h)),
            pl.BlockSpec((2, None, 8, w), lambda b, h: (0, h, 0, 0)),
            pl.BlockSpec((1, w), lambda b, h: (0, h)),
            cast_spec,
        ],
        out_specs=(pl.BlockSpec((seq, w), lambda b, h: (b, h)), cast_spec),
        scratch_shapes=[
            pltpu.VMEM((seq, RET_DV), F32),
            pltpu.VMEM((2, RET_DK, RET_DV), F32),
            pltpu.VMEM((2, RET_CHUNK, RET_CHUNK), F32),
            pltpu.VMEM((2, RET_CHUNK, RET_DV), F32),
            pltpu.VMEM((2, RET_CHUNK, RET_DK), F32),
            pltpu.VMEM((2, 8, RET_DV), F32),
        ],
        compiler_params=pltpu.CompilerParams(dimension_semantics=("arbitrary", "arbitrary"),
                                             vmem_limit_bytes=VMEM_LIMIT),
        name="retention",
    )(p, p, p, p, pc, pc, rd_b, ret_gn, w_later)


GLA_SUB = 64
GLA_BLK = 256
GLA_SCAN_BLK = 512


def _gla_kernel(q_ref, k_ref, v_ref, g_ref, ga_ref, kc_ref, vc_ref, gac_ref, up_ref, bias_ref, gn_ref,
                wsrc_ref, o_ref, wdst_ref, qf_scr, qb_scr, ktf_scr, ktb_scr, khf_scr, khb_scr,
                khcf_scr, khcb_scr, dec_scr, decc_scr, o_scr, s_scr, *, n_blk):
    blk, sub, dk = GLA_BLK, GLA_SUB, GLA_DK
    wdst_ref[...] = wsrc_ref[...].astype(wdst_ref.dtype)
    nsub = blk // sub
    shift = sub.bit_length() - 1
    scale = GLA_DK ** -0.5

    def masks(n):
        ii = lax.broadcasted_iota(jnp.int32, (n, n), 0)
        jj = lax.broadcasted_iota(jnp.int32, (n, n), 1)
        same = (ii >> shift) == (jj >> shift)
        lower = jj <= ii
        return same, lower, jnp.where(same & lower, 1.0, 0.0).astype(BF16)

    ctx_rows = gac_ref.shape[0]
    blk_masks = masks(blk)
    ctx_masks = blk_masks if ctx_rows == blk else masks(ctx_rows)
    r8 = lax.broadcasted_iota(jnp.int32, (8, dk), 0)

    u = up_ref[...]
    u_hi, u_lo = _split_bf16(u)
    urow = lax.broadcasted_iota(jnp.int32, u.shape, 0)
    rhs = jnp.where(urow >= 2 * GA_W, u_lo, u_hi)
    bias = bias_ref[...]

    def decay_stage(ga_blk, k_blk, q_blk, msk, store):
        same, lower, tmat = msk
        nsub = ga_blk.shape[0] // sub
        assert nsub <= 8
        z = _dot(ga_blk, rhs) + bias
        yield
        a = (jnp.minimum(z, 0.0) - jnp.log(1.0 + jnp.exp(-jnp.abs(z)))) * (1.0 / GLA_GATE_NORM)
        a_hi, a_lo = _split_bf16(a)
        yield
        pre = _dot(tmat, a_hi) + _dot(tmat, a_lo)
        yield
        lasts = [pre[i * sub + sub - 1:i * sub + sub, :] for i in range(nsub)]
        tot = jnp.concatenate([jnp.broadcast_to(l, (sub, 2 * dk)) for l in lasts], axis=0)
        b_f = pre[:, :dk]
        ex_b = pre[:, dk:] - a[:, dk:]
        b_b = tot[:, dk:] - ex_b
        kf = k_blk.astype(F32)
        khf = (kf * jnp.exp(tot[:, :dk] - b_f)).astype(BF16)
        khb = (kf * jnp.exp(ex_b)).astype(BF16)

        def tile8(half):
            rows = [jnp.broadcast_to(l[:, half * dk:(half + 1) * dk], (8, dk)) for l in lasts]
            t = rows[nsub - 1]
            for i in range(nsub - 2, -1, -1):
                t = jnp.where(r8 == i, rows[i], t)
            return t

        dec = jnp.exp(jnp.concatenate([tile8(0), tile8(1), jnp.zeros((LANE - 16, dk), F32)], axis=0))
        out = dict(khf=khf, khb=khb, dec_t=dec.T)
        if q_blk is not None:
            qf32 = q_blk.astype(F32)
            out.update(qf=(qf32 * jnp.exp(b_f)).astype(BF16), qb=(qf32 * jnp.exp(b_b)).astype(BF16),
                       ktf=(kf * jnp.exp(-b_f)).astype(BF16), ktb=(kf * jnp.exp(-b_b)).astype(BF16))
        store(**out)
        yield

    def score_stage(r):
        same, lower, _ = blk_masks
        rows = pl.ds(r, blk)
        s_f = _dot_nt(qf_scr[rows, :], ktf_scr[rows, :])
        s_b = _dot_nt(qb_scr[rows, :], ktb_scr[rows, :])
        yield
        s = jnp.where(same, jnp.where(lower, s_f, s_b), 0.0).astype(BF16)
        yield
        o_scr[rows, :] = _dot(s, v_ref[rows, :])
        yield

    def run(*stages):
        live = list(stages)
        while live:
            live = [g for g in live if next(g, live) is not live]

    def store_ctx(khf, khb, dec_t):
        khcf_scr[...] = khf
        khcb_scr[...] = khb
        decc_scr[...] = dec_t

    run(decay_stage(gac_ref[...], kc_ref[...], None, ctx_masks, store_ctx))

    def decay_block(t):
        rows = pl.ds(pl.multiple_of(t * blk, blk), blk)

        def store(khf, khb, dec_t, qf, qb, ktf, ktb):
            khf_scr[rows, :] = khf
            khb_scr[rows, :] = khb
            qf_scr[rows, :] = qf
            qb_scr[rows, :] = qb
            ktf_scr[rows, :] = ktf
            ktb_scr[rows, :] = ktb
            dec_scr[t] = dec_t

        return decay_stage(ga_ref[rows, :], k_ref[rows, :], q_ref[rows, :], blk_masks, store)

    def score_block(t):
        return score_stage(pl.multiple_of(t * blk, blk))

    run(decay_block(0))

    def prep_body(t, carry):
        run(decay_block(t + 1), score_block(t))
        return carry

    lax.fori_loop(0, n_blk - 1, prep_body, 0, unroll=5)
    run(score_block(n_blk - 1))

    def scan_blk(direction, n_sub, dec_ts, kh_at, v_at, q_at=None, o_rows=None):
        s = s_scr[direction]
        order = range(n_sub) if direction == 0 else range(n_sub - 1, -1, -1)
        kv = {i: _dot_tn(kh_at(i), v_at(i)) for i in order}
        yield
        for i in order:
            c = direction * 8 + i % nsub
            col = dec_ts[i // nsub][:, c:c + 1]
            if q_at is not None:
                rows = o_rows(i)
                o_scr[rows, :] += _dot(q_at(i), s.astype(BF16))
            s = s * col + kv[i]
            yield
        s_scr[direction] = s

    s_scr[...] = jnp.zeros(s_scr.shape, F32)
    run(*[scan_blk(direction, ctx_rows // sub, [decc_scr[...]],
                   lambda i, khc=khc: khc[i * sub:(i + 1) * sub, :],
                   lambda i: vc_ref[i * sub:(i + 1) * sub, :])
          for direction, khc in ((0, khcf_scr), (1, khcb_scr))])

    sblk = GLA_SCAN_BLK
    per = sblk // blk
    n_sblk = n_blk // per

    def finalize(t):
        r = pl.multiple_of(t * sblk, sblk)
        nrm = _rms(o_scr[pl.ds(r, sblk), :] * scale)
        g = g_ref[pl.ds(r, sblk), :].astype(F32)
        o_ref[pl.ds(r, sblk), :] = (nrm * gn_ref[...] * (g * _sigmoid(g))).astype(o_ref.dtype)

    def scan_body(t, carry, done):
        stages = []
        for direction, q_scr, kh_scr in ((0, qf_scr, khf_scr), (1, qb_scr, khb_scr)):
            tb = t if direction == 0 else n_sblk - 1 - t

            def rows(i, tb=tb):
                return pl.ds(pl.multiple_of(tb * sblk + i * sub, sub), sub)

            stages.append(scan_blk(direction, sblk // sub, [dec_scr[tb * per + j] for j in range(per)],
                                   lambda i, kh_scr=kh_scr, rows=rows: kh_scr[rows(i), :],
                                   lambda i, rows=rows: v_ref[rows(i), :],
                                   lambda i, q_scr=q_scr, rows=rows: q_scr[rows(i), :],
                                   rows))
        for st in stages:
            run(st)
        if done:
            finalize(t)
            finalize(n_sblk - 1 - t)
        return carry

    lax.fori_loop(0, n_sblk // 2, functools.partial(scan_body, done=False), 0, unroll=2)
    lax.fori_loop(n_sblk // 2, n_sblk, functools.partial(scan_body, done=True), 0, unroll=2)


def _gla(p, ga, pc, gac, up_stack, bias2, gla_gn, w_later, *, batch, seq, ctx_len):
    assert ctx_len % GLA_SUB == 0 and ctx_len <= GLA_BLK and GLA_SCAN_BLK % GLA_BLK == 0
    assert (seq // GLA_SCAN_BLK) % 2 == 0
    n_blk = seq // GLA_BLK
    kern = functools.partial(_gla_kernel, n_blk=n_blk)
    dk, dv = GLA_DK, GLA_DV
    cast_spec = pl.BlockSpec(*_cast_plan(w_later, batch, GLA_HEADS))
    return pl.pallas_call(
        kern,
        out_shape=(jax.ShapeDtypeStruct((batch * seq, GLA_VW), BF16),
                   jax.ShapeDtypeStruct(w_later.shape, BF16)),
        grid=(batch, GLA_HEADS),
        in_specs=[
            pl.BlockSpec((seq, dk), lambda b, h: (b, _DST["gq"] // dk + h)),
            pl.BlockSpec((seq, dk), lambda b, h: (b, _DST["gk"] // dk + h)),
            pl.BlockSpec((seq, dv), lambda b, h: (b, _DST["gv"] // dv + h)),
            pl.BlockSpec((seq, dv), lambda b, h: (b, _DST["gg"] // dv + h)),
            pl.BlockSpec((seq, LANE), lambda b, h: (b, 0)),
            pl.BlockSpec((ctx_len, dk), lambda b, h: (b, _DST["gk"] // dk + h)),
            pl.BlockSpec((ctx_len, dv), lambda b, h: (b, _DST["gv"] // dv + h)),
            pl.BlockSpec((ctx_len, LANE), lambda b, h: (b, 0)),
            pl.BlockSpec((LANE, 2 * dk), lambda b, h: (0, h)),
            pl.BlockSpec((1, 2 * dk), lambda b, h: (0, h)),
            pl.BlockSpec((1, dv), lambda b, h: (0, h)),
            cast_spec,
        ],
        out_specs=(pl.BlockSpec((seq, dv), lambda b, h: (b, h)), cast_spec),
        scratch_shapes=[
            pltpu.VMEM((seq, dk), BF16), pltpu.VMEM((seq, dk), BF16),
            pltpu.VMEM((seq, dk), BF16), pltpu.VMEM((seq, dk), BF16),
            pltpu.VMEM((seq, dk), BF16), pltpu.VMEM((seq, dk), BF16),
            pltpu.VMEM((ctx_len, dk), BF16), pltpu.VMEM((ctx_len, dk), BF16),
            pltpu.VMEM((n_blk, dk, LANE), F32), pltpu.VMEM((dk, LANE), F32),
            pltpu.VMEM((seq, dv), F32),
            pltpu.VMEM((2, dk, dv), F32),
        ],
        compiler_params=pltpu.CompilerParams(dimension_semantics=("arbitrary", "arbitrary"),
                                             vmem_limit_bytes=VMEM_LIMIT),
        name="gla",
    )(p, p, p, p, ga, pc, pc, gac, up_stack, bias2, gla_gn, w_later)


MXU_N = 256


POST_ROWS = 128


def _post_kernel(or_ref, og_ref, ga0_ref, ga1_ref, gb0_ref, gb1_ref, x_ref, wur_ref, wug_ref, wo_ref,
                 g1_ref, sh2_ref, sc2_ref, npost_ref, npre_ref, hs_ref, h2_ref):
    half = ga0_ref.shape[1]
    tm = hs_ref.shape[0]
    groups = [slice(r0, r0 + POST_ROWS) for r0 in range(0, tm, POST_ROWS)]

    def up(rows):
        return _dot(or_ref[rows, :], wur_ref[...]), _dot(og_ref[rows, :], wug_ref[...])

    def out(rows, y_ret, y_gla):
        def merge(ga_r, gb_r, lo):
            return (_sigmoid(ga_r[rows, :].astype(F32)) * y_ret[:, lo:lo + half]
                    + _sigmoid(gb_r[rows, :].astype(F32)) * y_gla[:, lo:lo + half]).astype(BF16)

        merged = jnp.concatenate([merge(ga0_ref, gb0_ref, 0), merge(ga1_ref, gb1_ref, half)], axis=1)
        return _dot(merged, wo_ref[...])

    def finish(rows, y):
        hs = x_ref[rows, :] + g1_ref[...] * (_rms(y) * npost_ref[...])
        hs_ref[rows, :] = hs
        h2 = _rms(hs) * npre_ref[...] * (1.0 + sc2_ref[...]) + sh2_ref[...]
        h2_ref[rows, :] = h2.astype(h2_ref.dtype)

    ups = [up(rows) for rows in groups]
    ys = [out(rows, *u) for rows, u in zip(groups, ups)]
    for rows, y in zip(groups, ys):
        finish(rows, y)


def _post(o_r, o_g, p, x2d, w_up_ret, w_up_gla, w_out, mod3, npost, npre, *, tm, seq):
    m, d = x2d.shape
    tiles_per_seq = seq // tm
    const = dict(pipeline_mode=pl.Buffered(1))
    gw = d // 2

    def modspec(chunk):
        return pl.BlockSpec((None, 1, d), lambda i: (i // tiles_per_seq, 0, chunk))

    def gatespec(name, part):
        return pl.BlockSpec((tm, gw), lambda i: (i, _DST[name] // gw + part))

    return pl.pallas_call(
        _post_kernel,
        out_shape=(jax.ShapeDtypeStruct((m, d), F32), jax.ShapeDtypeStruct((m, d), BF16)),
        grid=(m // tm,),
        in_specs=[
            pl.BlockSpec((tm, RET_VW), lambda i: (i, 0)),
            pl.BlockSpec((tm, GLA_VW), lambda i: (i, 0)),
            gatespec("gate_a", 0), gatespec("gate_a", 1), gatespec("gate_b", 0), gatespec("gate_b", 1),
            pl.BlockSpec((tm, d), lambda i: (i, 0)),
            pl.BlockSpec((RET_VW, d), lambda i: (0, 0), **const),
            pl.BlockSpec((GLA_VW, d), lambda i: (0, 0), **const),
            pl.BlockSpec((d, d), lambda i: (0, 0), **const),
            modspec(2), modspec(3), modspec(4),
            pl.BlockSpec((1, d), lambda i: (0, 0)),
            pl.BlockSpec((1, d), lambda i: (0, 0)),
        ],
        out_specs=(pl.BlockSpec((tm, d), lambda i: (i, 0)),
                   pl.BlockSpec((tm, d), lambda i: (i, 0))),
        compiler_params=pltpu.CompilerParams(dimension_semantics=("arbitrary",),
                                             vmem_limit_bytes=VMEM_LIMIT),
        name="post",
    )(o_r, o_g, p, p, p, p, x2d, w_up_ret, w_up_gla, w_out, mod3, mod3, mod3, npost, npre)


def _ffn_up_kernel(h_ref, wg_ref, wu_ref, wsrc_ref, a_ref, wdst_ref):
    wdst_ref[...] = wsrc_ref[...].astype(wdst_ref.dtype)
    h = h_ref[...]
    for lo in range(0, a_ref.shape[1], MXU_N):
        g = _dot(h, wg_ref[:, lo:lo + MXU_N])
        u = _dot(h, wu_ref[:, lo:lo + MXU_N])
        a_ref[:, lo:lo + MXU_N] = (g * _sigmoid(g) * u).astype(a_ref.dtype)


def _ffn_up(h2, wg, wu, w_later, *, tm, tf):
    m, d = h2.shape
    d_ff = wg.shape[1]
    grid = (m // tm, d_ff // tf)
    cast_spec = pl.BlockSpec(*_cast_plan(w_later, *grid))
    return pl.pallas_call(
        _ffn_up_kernel,
        out_shape=(jax.ShapeDtypeStruct((m, d_ff), BF16), jax.ShapeDtypeStruct(w_later.shape, BF16)),
        grid=grid,
        in_specs=[
            pl.BlockSpec((tm, d), lambda i, f: (i, 0)),
            pl.BlockSpec((d, tf), lambda i, f: (0, f)),
            pl.BlockSpec((d, tf), lambda i, f: (0, f)),
            cast_spec,
        ],
        out_specs=(pl.BlockSpec((tm, tf), lambda i, f: (i, f)), cast_spec),
        compiler_params=pltpu.CompilerParams(dimension_semantics=("arbitrary", "arbitrary"),
                                             vmem_limit_bytes=VMEM_LIMIT),
        name="ffn_up",
    )(h2, wg, wu, w_later)


def _ffn_down_kernel(a_ref, wd_ref, hs_ref, g2_ref, nw_ref, o_ref, y_scr, *, n_split):
    tm, d = o_ref.shape
    w = d // n_split
    ss = jnp.zeros((tm, 1), F32)
    for lo in range(0, d, w):
        y = _dot(a_ref[...], wd_ref[:, lo:lo + w])
        y_scr[:, lo:lo + w] = y
        ss = ss + jnp.sum(y * y, axis=-1, keepdims=True)
    inv = lax.rsqrt(ss * (1.0 / d) + EPS)
    o_ref[...] = hs_ref[...] + g2_ref[...] * (y_scr[...] * inv * nw_ref[...])


def _ffn_down(a, hs, wd, mod3, npost, *, tm, seq):
    m, d = hs.shape
    d_ff = wd.shape[0]
    tiles_per_seq = seq // tm
    kern = functools.partial(_ffn_down_kernel, n_split=4)
    return pl.pallas_call(
        kern,
        out_shape=jax.ShapeDtypeStruct((m, d), F32),
        grid=(m // tm,),
        in_specs=[
            pl.BlockSpec((tm, d_ff), lambda i: (i, 0)),
            pl.BlockSpec((d_ff, d), lambda i: (0, 0), pipeline_mode=pl.Buffered(1)),
            pl.BlockSpec((tm, d), lambda i: (i, 0)),
            pl.BlockSpec((None, 1, d), lambda i: (i // tiles_per_seq, 0, 5)),
            pl.BlockSpec((1, d), lambda i: (0, 0)),
        ],
        out_specs=pl.BlockSpec((tm, d), lambda i: (i, 0)),
        scratch_shapes=[pltpu.VMEM((tm, d), F32)],
        compiler_params=pltpu.CompilerParams(dimension_semantics=("arbitrary",),
                                             vmem_limit_bytes=VMEM_LIMIT),
        name="ffn_down",
    )(a, wd, hs, mod3, npost)


def _rope_tables(seq):
    f32 = np.float32
    rows = seq // GRID_W
    pos_r = np.repeat(np.arange(rows, dtype=f32), GRID_W)
    pos_c = np.tile(np.arange(GRID_W, dtype=f32), rows)
    n_f = RET_DK // 4
    inv = (f32(ROPE_BASE) ** (-np.arange(n_f, dtype=f32) / f32(n_f))).astype(f32)
    ang = np.concatenate([pos_r[:, None] * inv, pos_c[:, None] * inv], axis=-1).astype(f32)
    return jnp.asarray(np.cos(ang), F32), jnp.asarray(np.sin(ang), F32)


def _gate_map_layout(gla_a_up, gla_a_bias):
    r, h, dk = GLA_LOW_RANK, GLA_HEADS, GLA_DK
    u = jnp.zeros((GA_W, h, 2, dk), F32)
    u = u.at[:r, :, 0, :].set(gla_a_up[0].reshape(r, h, dk))
    u = u.at[r:, :, 1, :].set(gla_a_up[1].reshape(r, h, dk))
    u = u.reshape(GA_W, h * 2 * dk)
    up_stack = jnp.concatenate([u, u, u, jnp.zeros((LANE - 3 * GA_W, h * 2 * dk), F32)], axis=0)
    bias2 = jnp.stack([gla_a_bias[0].reshape(h, dk), gla_a_bias[1].reshape(h, dk)], axis=1)
    return up_stack, bias2.reshape(1, h * 2 * dk)


def _layer(h_state, c_rows, ctx2d, w_mod, b_mod, norm_mix_pre, norm_mix_post, norm_ffn_pre, norm_ffn_post,
           w_in, ret_decay, gla_a_up, gla_a_bias, ret_gn, gla_gn, w_up_ret, w_up_gla, w_out,
           ffn_w_gate, ffn_w_up, ffn_w_down, cos, sin, *, batch, seq, ctx_len):
    d = D_MODEL
    ga0 = _SRC["ga"][0]
    w_t = w_in.T
    w_g = w_t[ga0:ga0 + GA_W]
    w_ga = jnp.concatenate([w_g, w_g, w_g, jnp.zeros((LANE - 3 * GA_W, d), F32)], axis=0).astype(BF16)
    up_stack, bias2 = _gate_map_layout(gla_a_up, gla_a_bias)
    rd_b = jnp.broadcast_to(ret_decay.reshape(2, RET_HEADS, 1, 1), (2, RET_HEADS, 8, RET_DK))

    w_main, mod = _prologue(w_t, c_rows, w_mod, b_mod)
    mod3 = mod.reshape(mod.shape[0], 1, 6 * d)
    nw_pre = norm_mix_pre.reshape(1, d)

    tm = 1024
    p, ga, wur, wug, wo = _inproj(
        h_state, nw_pre, mod3, lambda i: i // (seq // tm), w_main, w_ga, cos, sin,
        tm=tm, tn=1024, ncols=N_MAIN, rope=True, seq_tiles=seq // tm, casts=(w_up_ret, w_up_gla, w_out))
    pc, gac = _inproj(ctx2d, nw_pre, mod3, lambda i: batch, w_main, w_ga, cos, sin,
                      tm=batch * ctx_len, tn=512, ncols=N_CTX, rope=False, seq_tiles=1)

    o_r, wg = _retention(p, pc, rd_b, ret_gn.reshape(1, RET_VW), ffn_w_gate,
                         batch=batch, seq=seq, ctx_len=ctx_len)
    o_g, wu = _gla(p, ga, pc, gac, up_stack, bias2, gla_gn.reshape(1, GLA_VW), ffn_w_up,
                   batch=batch, seq=seq, ctx_len=ctx_len)

    hs, h2 = _post(o_r, o_g, p, h_state, wur, wug, wo,
                   mod3, norm_mix_post.reshape(1, d), norm_ffn_pre.reshape(1, d), tm=256, seq=seq)
    act, wd = _ffn_up(h2, wg, wu, ffn_w_down, tm=1024, tf=512)
    return _ffn_down(act, hs, wd, mod3, norm_ffn_post.reshape(1, d), tm=256, seq=seq)


def kernel(x, c, ctx, c_ctx, w_mod, b_mod, norm_mix_pre, norm_mix_post, norm_ffn_pre, norm_ffn_post,
           w_in, ret_decay, gla_a_up, gla_a_bias, ret_gn, gla_gn, w_up_ret, w_up_gla, w_out,
           ffn_w_gate, ffn_w_up, ffn_w_down):
    batch, seq, d = x.shape
    ctx_len = ctx.shape[1]
    depth = w_mod.shape[0]
    cos, sin = _rope_tables(seq)
    c_rows = jnp.zeros((16, d), F32).at[:batch].set(c.astype(F32)).at[batch].set(c_ctx.astype(F32))
    ctx2d = ctx.astype(F32).reshape(batch * ctx_len, d)
    h_state = x.astype(F32).reshape(batch * seq, d)
    for i in range(depth):
        h_state = _layer(h_state, c_rows, ctx2d, w_mod[i], b_mod[i], norm_mix_pre[i], norm_mix_post[i],
                         norm_ffn_pre[i], norm_ffn_post[i], w_in[i], ret_decay[i], gla_a_up[i], gla_a_bias[i],
                         ret_gn[i], gla_gn[i], w_up_ret[i], w_up_gla[i], w_out[i],
                         ffn_w_gate[i], ffn_w_up[i], ffn_w_down[i], cos, sin,
                         batch=batch, seq=seq, ctx_len=ctx_len)
    return h_state.reshape(batch, seq, d).astype(x.dtype)
```

```python
import functools

import jax
import jax.numpy as jnp
import numpy as np
from jax import lax
from jax.experimental import pallas as pl
from jax.experimental.pallas import tpu as pltpu

F32 = jnp.float32
BF16 = jnp.bfloat16

D_MODEL = 2048
GRID_W = 64
RET_HEADS = 4
RET_DK = 256
RET_DV = 256
GLA_HEADS = 4
GLA_DK = 128
GLA_DV = 256
GLA_LOW_RANK = 16
GLA_GATE_NORM = 16.0
ROPE_BASE = 10000.0
EPS = 1e-6

RET_QK = RET_HEADS * RET_DK
RET_VW = RET_HEADS * RET_DV
GLA_KW = GLA_HEADS * GLA_DK
GLA_VW = GLA_HEADS * GLA_DV
GA_W = 2 * GLA_LOW_RANK
LANE = 128

_SRC = {}
_off = 0
for _name, _w in (("rk", RET_QK), ("rv", RET_VW), ("gk", GLA_KW), ("gv", GLA_VW), ("ga", GA_W),
                  ("rq", RET_QK), ("rg", RET_VW), ("gq", GLA_KW), ("gg", GLA_VW),
                  ("gate_a", D_MODEL), ("gate_b", D_MODEL)):
    _SRC[_name] = (_off, _w)
    _off += _w

_ORDER = ("rk", "rv", "gv", "gk", "gq", "rq", "rg", "gg", "gate_a", "gate_b")
_DST = {}
_off = 0
for _name in _ORDER:
    _DST[_name] = _off
    _off += _SRC[_name][1]
N_MAIN = _off
N_CTX = _DST["gq"]
PACK_W = 512

VMEM_LIMIT = 56 * 1024 * 1024


def _dot(a, b):
    return jnp.dot(a, b, preferred_element_type=F32)


def _dot_nt(a, b):
    return lax.dot_general(a, b, (((1,), (1,)), ((), ())), preferred_element_type=F32)


def _dot_tn(a, b):
    return lax.dot_general(a, b, (((0,), (0,)), ((), ())), preferred_element_type=F32)


def _sigmoid(x):
    return 1.0 / (1.0 + jnp.exp(-x))


def _rms(x):
    return x * lax.rsqrt(jnp.mean(x * x, axis=-1, keepdims=True) + EPS)


def _split_bf16(x):
    hi = x.astype(BF16)
    lo = (x - hi.astype(F32)).astype(BF16)
    return hi, lo


MOD_TN = 1024


def _prologue_kernel(a_idx_ref, n_idx_ref, shift_ref, a_ref, n_ref, c_ref, wm_ref, bm_ref,
                     o_ref, mod_ref, *, n_mod):
    j = pl.program_id(0)

    @pl.when(shift_ref[j] == 0)
    def _():
        o_ref[...] = a_ref[...].astype(o_ref.dtype)

    @pl.when(shift_ref[j] != 0)
    def _():
        o_ref[...] = jnp.concatenate([a_ref[GA_W:, :], n_ref[...]], axis=0).astype(o_ref.dtype)

    @pl.when(j < n_mod)
    def _():
        cf = c_ref[...]
        rows = cf.shape[0]
        s_hi, s_lo = _split_bf16(cf * _sigmoid(cf))
        w_hi, w_lo = _split_bf16(wm_ref[...])
        r = _dot(jnp.concatenate([s_hi, s_lo], axis=0), w_hi)
        mod_ref[...] = r[:rows] + r[rows:] + _dot(s_hi, w_lo) + bm_ref[...]


def _prologue(w_t, c_rows, w_mod, b_mod):
    d = w_t.shape[1]
    rows = c_rows.shape[0]
    n_modcols = w_mod.shape[1]
    n_mod = n_modcols // MOD_TN
    a_idx, n_idx, shift = [], [], []
    for name in _ORDER:
        src, width = _SRC[name]
        for c in range(src, src + width, PACK_W):
            base = c - c % PACK_W
            assert c - base in (0, GA_W)
            a_idx.append(base // PACK_W)
            n_idx.append((base + PACK_W) // GA_W if c != base else (n_idx[-1] if n_idx else 0))
            shift.append(c - base)
    n_tiles = len(a_idx)
    assert n_mod <= n_tiles
    as_i32 = lambda v: jnp.asarray(v, jnp.int32)
    mod_tile = lambda j, a, n, s: (0, jnp.minimum(j, n_mod - 1))
    return pl.pallas_call(
        functools.partial(_prologue_kernel, n_mod=n_mod),
        out_shape=(jax.ShapeDtypeStruct((n_tiles * PACK_W, d), BF16),
                   jax.ShapeDtypeStruct((rows, n_modcols), F32)),
        grid_spec=pltpu.PrefetchScalarGridSpec(
            num_scalar_prefetch=3,
            grid=(n_tiles,),
            in_specs=[pl.BlockSpec((PACK_W, d), lambda j, a, n, s: (a[j], 0)),
                      pl.BlockSpec((GA_W, d), lambda j, a, n, s: (n[j], 0)),
                      pl.BlockSpec((rows, d), lambda j, a, n, s: (0, 0)),
                      pl.BlockSpec((d, MOD_TN), mod_tile),
                      pl.BlockSpec((1, MOD_TN), mod_tile)],
            out_specs=(pl.BlockSpec((PACK_W, d), lambda j, a, n, s: (j, 0)),
                       pl.BlockSpec((rows, MOD_TN), mod_tile)),
        ),
        compiler_params=pltpu.CompilerParams(dimension_semantics=("arbitrary",),
                                             vmem_limit_bytes=VMEM_LIMIT),
        name="prologue",
    )(as_i32(a_idx), as_i32(n_idx), as_i32(shift), w_t, w_t, c_rows, w_mod, b_mod.reshape(1, n_modcols))


def _inproj_kernel(x_ref, nw_ref, sh_ref, sc_ref, w_ref, wga_ref, cos_ref, sin_ref, *rest,
                   rope_tiles, tn, n_cast):
    cast_in = rest[:n_cast]
    o_ref, ga_ref = rest[n_cast:n_cast + 2]
    cast_out = rest[n_cast + 2:2 * n_cast + 2]
    h_scr = rest[2 * n_cast + 2]
    j = pl.program_id(1)

    def ride_along_casts():
        for src, dst in zip(cast_in, cast_out):
            dst[...] = src[...].astype(dst.dtype)

    @pl.when(j == 0)
    def _():
        h = _rms(x_ref[...]) * (nw_ref[...] * (1.0 + sc_ref[...])) + sh_ref[...]
        hb = h.astype(BF16)
        h_scr[...] = hb
        g3 = _dot_nt(hb, wga_ref[...])
        lane = lax.broadcasted_iota(jnp.int32, g3.shape, 1)
        resid = g3 - g3.astype(BF16).astype(F32)
        ga_ref[...] = jnp.where((lane >= GA_W) & (lane < 2 * GA_W), resid, g3).astype(ga_ref.dtype)

    if not rope_tiles:
        ride_along_casts()
        o_ref[...] = _dot_nt(h_scr[...], w_ref[...]).astype(o_ref.dtype)
        return

    is_rope = functools.reduce(jnp.logical_or, [j == t for t in rope_tiles])

    @pl.when(is_rope)
    def _():
        ride_along_casts()
        cos = cos_ref[...]
        sin = sin_ref[...]
        half = RET_DK // 2
        for hd in range(0, tn, RET_DK):
            acc = _dot_nt(h_scr[...], w_ref[hd:hd + RET_DK, :])
            t1 = acc[:, :half]
            t2 = acc[:, half:]
            o_ref[:, hd:hd + half] = (t1 * cos - t2 * sin).astype(o_ref.dtype)
            o_ref[:, hd + half:hd + RET_DK] = (t1 * sin + t2 * cos).astype(o_ref.dtype)

    @pl.when(jnp.logical_not(is_rope))
    def _():
        ride_along_casts()
        o_ref[...] = _dot_nt(h_scr[...], w_ref[...]).astype(o_ref.dtype)


def _cast_plan(arr, n_i, n_j):
    rows, cols = arr.shape

    def pieces(extent, want, align):
        n = want
        while extent % n or (extent // n) % align:
            n -= 1
        return n

    if rows % n_j == 0 and (rows // n_j) % 16 == 0:
        nr, nc = n_j, pieces(cols, n_i, LANE)
        index = lambda i, j, nc=nc: (j, jnp.minimum(i, nc - 1))
    else:
        nr, nc = pieces(rows, n_i, 16), pieces(cols, n_j, LANE)
        index = lambda i, j, nr=nr, nc=nc: (jnp.minimum(i, nr - 1), jnp.minimum(j, nc - 1))
    return (rows // nr, cols // nc), index


def _inproj(x2d, norm_w, mod3, mod_row_of_tile, w_main, w_ga, cos, sin, *, tm, tn, ncols, rope, seq_tiles,
            casts=()):
    m, d = x2d.shape
    rope_tiles = ()
    if rope:
        assert tn == RET_QK and _DST["rk"] % tn == 0 and _DST["rq"] % tn == 0
        rope_tiles = (_DST["rk"] // tn, _DST["rq"] // tn)
    grid = (m // tm, ncols // tn)
    cast_specs = [pl.BlockSpec(*_cast_plan(a, *grid)) for a in casts]
    kern = functools.partial(_inproj_kernel, rope_tiles=rope_tiles, tn=tn, n_cast=len(casts))
    return pl.pallas_call(
        kern,
        out_shape=(jax.ShapeDtypeStruct((m, ncols), BF16),
                   jax.ShapeDtypeStruct((m, LANE), BF16),
                   *[jax.ShapeDtypeStruct(a.shape, BF16) for a in casts]),
        grid=grid,
        in_specs=[
            pl.BlockSpec((tm, d), lambda i, j: (i, 0)),
            pl.BlockSpec((1, d), lambda i, j: (0, 0)),
            pl.BlockSpec((None, 1, d), lambda i, j: (mod_row_of_tile(i), 0, 0)),
            pl.BlockSpec((None, 1, d), lambda i, j: (mod_row_of_tile(i), 0, 1)),
            pl.BlockSpec((tn, d), lambda i, j: (j, 0)),
            pl.BlockSpec((LANE, d), lambda i, j: (0, 0)),
            pl.BlockSpec((tm, RET_DK // 2), lambda i, j: (i % seq_tiles, 0)),
            pl.BlockSpec((tm, RET_DK // 2), lambda i, j: (i % seq_tiles, 0)),
            *cast_specs,
        ],
        out_specs=(pl.BlockSpec((tm, tn), lambda i, j: (i, j)),
                   pl.BlockSpec((tm, LANE), lambda i, j: (i, 0)),
                   *cast_specs),
        scratch_shapes=[pltpu.VMEM((tm, d), BF16)],
        compiler_params=pltpu.CompilerParams(dimension_semantics=("arbitrary", "arbitrary"),
                                             vmem_limit_bytes=VMEM_LIMIT),
        name="inproj",
    )(x2d, norm_w, mod3, mod3, w_main, w_ga, cos, sin, *casts)


RET_CHUNK = 256


def _ride_along_casts(rest, n_cast):
    for src, dst in zip(rest[:n_cast], rest[n_cast + 1:2 * n_cast + 1]):
        dst[...] = src[...].astype(dst.dtype)
    return rest[n_cast], rest[2 * n_cast + 1:]


def _ret_kernel(q_ref, k_ref, v_ref, g_ref, kc_ref, vc_ref, rd_ref, gn_ref, *rest, n_chunks, n_cast):
    o_ref, (o_scr, s_scr, dm_scr, eq_scr, wk_scr, sd_scr) = _ride_along_casts(rest, n_cast)
    c_len = RET_CHUNK
    ii = lax.broadcasted_iota(jnp.int32, (c_len, c_len), 0)
    jj = lax.broadcasted_iota(jnp.int32, (c_len, c_len), 1)
    rowi = lax.broadcasted_iota(jnp.int32, (c_len, RET_DK), 0).astype(F32)
    scale = RET_DK ** -0.5

    for direction in (0, 1):
        lg = -jnp.exp(rd_ref[direction][0:1, :])
        if direction == 0:
            dm_scr[direction] = jnp.where(jj <= ii, jnp.exp((ii - jj).astype(F32) * lg), 0.0)
            eq_scr[direction] = jnp.exp((rowi + 1.0) * lg)
            wk = jnp.exp((c_len - 1.0 - rowi) * lg)
        else:
            dm_scr[direction] = jnp.where(jj > ii, jnp.exp((jj - ii).astype(F32) * lg), 0.0)
            eq_scr[direction] = jnp.exp((c_len - rowi) * lg)
            wk = jnp.exp(rowi * lg)
        wk_scr[direction] = wk
        sd_scr[direction] = jnp.broadcast_to(jnp.exp(float(c_len) * lg), (8, RET_DV))
        kcw = (kc_ref[...].astype(F32) * wk).astype(BF16)
        s_scr[direction] = _dot_tn(kcw, vc_ref[...])

    def chunk(direction, c, first):
        r0 = pl.multiple_of(c * c_len, c_len)
        q = q_ref[pl.ds(r0, c_len), :]
        k = k_ref[pl.ds(r0, c_len), :]
        v = v_ref[pl.ds(r0, c_len), :]
        s_qk = _dot_nt(q, k)
        s_state = s_scr[direction]
        o_state = _dot(q, s_state.astype(BF16))
        yield
        sc = (s_qk * dm_scr[direction]).astype(BF16)
        kw = (k.astype(F32) * wk_scr[direction]).astype(BF16)
        yield
        o = _dot(sc, v) + eq_scr[direction] * o_state
        s_scr[direction] = sd_scr[direction][0:1, :] * s_state + _dot_tn(kw, v)
        yield
        if first:
            o_scr[pl.ds(r0, c_len), :] = o
        else:
            tot = o_scr[pl.ds(r0, c_len), :] + o
            mu = jnp.mean(tot, axis=-1, keepdims=True)
            cen = tot - mu
            var = jnp.mean(cen * cen, axis=-1, keepdims=True)
            nrm = cen * (scale * lax.rsqrt(var * (scale * scale) + EPS))
            g = g_ref[pl.ds(r0, c_len), :].astype(F32)
            o_ref[pl.ds(r0, c_len), :] = (nrm * gn_ref[...] * (g * _sigmoid(g))).astype(o_ref.dtype)
        yield

    def run(*stages):
        live = list(stages)
        while live:
            live = [g for g in live if next(g, live) is not live]

    def body(t, carry, first):
        run(chunk(0, t, first), chunk(1, n_chunks - 1 - t, first))
        return carry

    lax.fori_loop(0, n_chunks // 2, functools.partial(body, first=True), 0, unroll=4)
    lax.fori_loop(n_chunks // 2, n_chunks, functools.partial(body, first=False), 0, unroll=4)


def _retention(p, pc, rd_b, ret_gn, casts, *, batch, seq, ctx_len):
    assert ctx_len == RET_CHUNK and (seq // RET_CHUNK) % 2 == 0
    n_chunks = seq // RET_CHUNK
    w = RET_DK
    kern = functools.partial(_ret_kernel, n_chunks=n_chunks, n_cast=len(casts))
    cast_specs = [pl.BlockSpec(*_cast_plan(a, batch, RET_HEADS)) for a in casts]

    def col(name):
        return _DST[name] // w

    return pl.pallas_call(
        kern,
        out_shape=(jax.ShapeDtypeStruct((batch * seq, RET_VW), BF16),
                   *[jax.ShapeDtypeStruct(a.shape, BF16) for a in casts]),
        grid=(batch, RET_HEADS),
        in_specs=[
            pl.BlockSpec((seq, w), lambda b, h: (b, col("rq") + h)),
            pl.BlockSpec((seq, w), lambda b, h: (b, col("rk") + h)),
            pl.BlockSpec((seq, w), lambda b, h: (b, col("rv") + h)),
            pl.BlockSpec((seq, w), lambda b, h: (b, col("rg") + h)),
            pl.BlockSpec((ctx_len, w), lambda b, h: (b, col("rk") + h)),
            pl.BlockSpec((ctx_len, w), lambda b, h: (b, col("rv") + h)),
            pl.BlockSpec((2, None, 8, w), lambda b, h: (0, h, 0, 0)),
            pl.BlockSpec((1, w), lambda b, h: (0, h)),
            *cast_specs,
        ],
        out_specs=(pl.BlockSpec((seq, w), lambda b, h: (b, h)), *cast_specs),
        scratch_shapes=[
            pltpu.VMEM((seq, RET_DV), F32),
            pltpu.VMEM((2, RET_DK, RET_DV), F32),
            pltpu.VMEM((2, RET_CHUNK, RET_CHUNK), F32),
            pltpu.VMEM((2, RET_CHUNK, RET_DV), F32),
            pltpu.VMEM((2, RET_CHUNK, RET_DK), F32),
            pltpu.VMEM((2, 8, RET_DV), F32),
        ],
        compiler_params=pltpu.CompilerParams(dimension_semantics=("arbitrary", "arbitrary"),
                                             vmem_limit_bytes=VMEM_LIMIT),
        name="retention",
    )(p, p, p, p, pc, pc, rd_b, ret_gn, *casts)


GLA_SUB = 64
GLA_BLK = 256
GLA_SCAN_BLK = 512


def _gla_kernel(q_ref, k_ref, v_ref, g_ref, ga_ref, kc_ref, vc_ref, gac_ref, up_ref, bias_ref, gn_ref,
                *rest, n_blk, n_cast):
    o_ref, scratch = _ride_along_casts(rest, n_cast)
    (qf_scr, qb_scr, ktf_scr, ktb_scr, khf_scr, khb_scr, khcf_scr, khcb_scr,
     dec_scr, decc_scr, o_scr, s_scr) = scratch
    blk, sub, dk = GLA_BLK, GLA_SUB, GLA_DK
    nsub = blk // sub
    shift = sub.bit_length() - 1
    scale = GLA_DK ** -0.5

    def masks(n):
        ii = lax.broadcasted_iota(jnp.int32, (n, n), 0)
        jj = lax.broadcasted_iota(jnp.int32, (n, n), 1)
        same = (ii >> shift) == (jj >> shift)
        lower = jj <= ii
        return same, lower, jnp.where(same & lower, 1.0, 0.0).astype(BF16)

    ctx_rows = gac_ref.shape[0]
    blk_masks = masks(blk)
    ctx_masks = blk_masks if ctx_rows == blk else masks(ctx_rows)
    r8 = lax.broadcasted_iota(jnp.int32, (8, dk), 0)

    u = up_ref[...]
    u_hi, u_lo = _split_bf16(u)
    urow = lax.broadcasted_iota(jnp.int32, u.shape, 0)
    rhs = jnp.where(urow >= 2 * GA_W, u_lo, u_hi)
    bias = bias_ref[...]

    def decay_stage(ga_blk, k_blk, q_blk, msk, store):
        same, lower, tmat = msk
        nsub = ga_blk.shape[0] // sub
        assert nsub <= 8
        z = _dot(ga_blk, rhs) + bias
        yield
        a = (jnp.minimum(z, 0.0) - jnp.log(1.0 + jnp.exp(-jnp.abs(z)))) * (1.0 / GLA_GATE_NORM)
        a_hi, a_lo = _split_bf16(a)
        yield
        pre = _dot(tmat, a_hi) + _dot(tmat, a_lo)
        yield
        lasts = [pre[i * sub + sub - 1:i * sub + sub, :] for i in range(nsub)]
        tot = jnp.concatenate([jnp.broadcast_to(l, (sub, 2 * dk)) for l in lasts], axis=0)
        b_f = pre[:, :dk]
        ex_b = pre[:, dk:] - a[:, dk:]
        b_b = tot[:, dk:] - ex_b
        kf = k_blk.astype(F32)
        khf = (kf * jnp.exp(tot[:, :dk] - b_f)).astype(BF16)
        khb = (kf * jnp.exp(ex_b)).astype(BF16)

        def tile8(half):
            rows = [jnp.broadcast_to(l[:, half * dk:(half + 1) * dk], (8, dk)) for l in lasts]
            t = rows[nsub - 1]
            for i in range(nsub - 2, -1, -1):
                t = jnp.where(r8 == i, rows[i], t)
            return t

        dec = jnp.exp(jnp.concatenate([tile8(0), tile8(1), jnp.zeros((LANE - 16, dk), F32)], axis=0))
        out = dict(khf=khf, khb=khb, dec_t=dec.T)
        if q_blk is not None:
            qf32 = q_blk.astype(F32)
            out.update(qf=(qf32 * jnp.exp(b_f)).astype(BF16), qb=(qf32 * jnp.exp(b_b)).astype(BF16),
                       ktf=(kf * jnp.exp(-b_f)).astype(BF16), ktb=(kf * jnp.exp(-b_b)).astype(BF16))
        store(**out)
        yield

    def score_stage(r):
        same, lower, _ = blk_masks
        rows = pl.ds(r, blk)
        s_f = _dot_nt(qf_scr[rows, :], ktf_scr[rows, :])
        s_b = _dot_nt(qb_scr[rows, :], ktb_scr[rows, :])
        yield
        s = jnp.where(same, jnp.where(lower, s_f, s_b), 0.0).astype(BF16)
        yield
        o_scr[rows, :] = _dot(s, v_ref[rows, :])
        yield

    def run(*stages):
        live = list(stages)
        while live:
            live = [g for g in live if next(g, live) is not live]

    def store_ctx(khf, khb, dec_t):
        khcf_scr[...] = khf
        khcb_scr[...] = khb
        decc_scr[...] = dec_t

    run(decay_stage(gac_ref[...], kc_ref[...], None, ctx_masks, store_ctx))

    def decay_block(t):
        rows = pl.ds(pl.multiple_of(t * blk, blk), blk)

        def store(khf, khb, dec_t, qf, qb, ktf, ktb):
            khf_scr[rows, :] = khf
            khb_scr[rows, :] = khb
            qf_scr[rows, :] = qf
            qb_scr[rows, :] = qb
            ktf_scr[rows, :] = ktf
            ktb_scr[rows, :] = ktb
            dec_scr[t] = dec_t

        return decay_stage(ga_ref[rows, :], k_ref[rows, :], q_ref[rows, :], blk_masks, store)

    def score_block(t):
        return score_stage(pl.multiple_of(t * blk, blk))

    run(decay_block(0))

    def prep_body(t, carry):
        run(decay_block(t + 1), score_block(t))
        return carry

    lax.fori_loop(0, n_blk - 1, prep_body, 0, unroll=5)
    run(score_block(n_blk - 1))

    def scan_blk(direction, n_sub, dec_ts, kh_at, v_at, q_at=None, o_rows=None):
        s = s_scr[direction]
        order = range(n_sub) if direction == 0 else range(n_sub - 1, -1, -1)
        kv = {i: _dot_tn(kh_at(i), v_at(i)) for i in order}
        yield
        for i in order:
            c = direction * 8 + i % nsub
            col = dec_ts[i // nsub][:, c:c + 1]
            if q_at is not None:
                rows = o_rows(i)
                o_scr[rows, :] += _dot(q_at(i), s.astype(BF16))
            s = s * col + kv[i]
            yield
        s_scr[direction] = s

    s_scr[...] = jnp.zeros(s_scr.shape, F32)
    run(*[scan_blk(direction, ctx_rows // sub, [decc_scr[...]],
                   lambda i, khc=khc: khc[i * sub:(i + 1) * sub, :],
                   lambda i: vc_ref[i * sub:(i + 1) * sub, :])
          for direction, khc in ((0, khcf_scr), (1, khcb_scr))])

    sblk = GLA_SCAN_BLK
    per = sblk // blk
    n_sblk = n_blk // per

    fin_rows = 128

    def finalize(t):
        for tb in (t, n_sblk - 1 - t):
            for r0 in range(0, sblk, fin_rows):
                rows = pl.ds(pl.multiple_of(tb * sblk + r0, fin_rows), fin_rows)
                o = o_scr[rows, :]
                ms = jnp.mean(o * o, axis=-1, keepdims=True)
                nrm = o * (scale * lax.rsqrt(ms * (scale * scale) + EPS))
                g = g_ref[rows, :].astype(F32)
                o_ref[rows, :] = (nrm * gn_ref[...] * (g * _sigmoid(g))).astype(o_ref.dtype)
                yield

    def scans(t):
        for direction, q_scr, kh_scr in ((0, qf_scr, khf_scr), (1, qb_scr, khb_scr)):
            tb = t if direction == 0 else n_sblk - 1 - t

            def rows(i, tb=tb):
                return pl.ds(pl.multiple_of(tb * sblk + i * sub, sub), sub)

            yield from scan_blk(direction, sblk // sub, [dec_scr[tb * per + j] for j in range(per)],
                                lambda i, kh_scr=kh_scr, rows=rows: kh_scr[rows(i), :],
                                lambda i, rows=rows: v_ref[rows(i), :],
                                lambda i, q_scr=q_scr, rows=rows: q_scr[rows(i), :],
                                rows)

    def scan_body(t, carry, fin_prev):
        if fin_prev:
            run(scans(t), finalize(t - 1))
        else:
            run(scans(t))
        return carry

    half = n_sblk // 2
    lax.fori_loop(0, half + 1, functools.partial(scan_body, fin_prev=False), 0, unroll=half + 1)
    lax.fori_loop(half + 1, n_sblk, functools.partial(scan_body, fin_prev=True), 0, unroll=n_sblk - half - 1)
    run(finalize(n_sblk - 1))


def _gla(p, ga, pc, gac, up_stack, bias2, gla_gn, casts, *, batch, seq, ctx_len):
    assert ctx_len % GLA_SUB == 0 and ctx_len <= GLA_BLK and GLA_SCAN_BLK % GLA_BLK == 0
    assert (seq // GLA_SCAN_BLK) % 2 == 0
    n_blk = seq // GLA_BLK
    kern = functools.partial(_gla_kernel, n_blk=n_blk, n_cast=len(casts))
    dk, dv = GLA_DK, GLA_DV
    cast_specs = [pl.BlockSpec(*_cast_plan(a, batch, GLA_HEADS)) for a in casts]
    return pl.pallas_call(
        kern,
        out_shape=(jax.ShapeDtypeStruct((batch * seq, GLA_VW), BF16),
                   *[jax.ShapeDtypeStruct(a.shape, BF16) for a in casts]),
        grid=(batch, GLA_HEADS),
        in_specs=[
            pl.BlockSpec((seq, dk), lambda b, h: (b, _DST["gq"] // dk + h)),
            pl.BlockSpec((seq, dk), lambda b, h: (b, _DST["gk"] // dk + h)),
            pl.BlockSpec((seq, dv), lambda b, h: (b, _DST["gv"] // dv + h)),
            pl.BlockSpec((seq, dv), lambda b, h: (b, _DST["gg"] // dv + h)),
            pl.BlockSpec((seq, LANE), lambda b, h: (b, 0)),
            pl.BlockSpec((ctx_len, dk), lambda b, h: (b, _DST["gk"] // dk + h)),
            pl.BlockSpec((ctx_len, dv), lambda b, h: (b, _DST["gv"] // dv + h)),
            pl.BlockSpec((ctx_len, LANE), lambda b, h: (b, 0)),
            pl.BlockSpec((LANE, 2 * dk), lambda b, h: (0, h)),
            pl.BlockSpec((1, 2 * dk), lambda b, h: (0, h)),
            pl.BlockSpec((1, dv), lambda b, h: (0, h)),
            *cast_specs,
        ],
        out_specs=(pl.BlockSpec((seq, dv), lambda b, h: (b, h)), *cast_specs),
        scratch_shapes=[
            pltpu.VMEM((seq, dk), BF16), pltpu.VMEM((seq, dk), BF16),
            pltpu.VMEM((seq, dk), BF16), pltpu.VMEM((seq, dk), BF16),
            pltpu.VMEM((seq, dk), BF16), pltpu.VMEM((seq, dk), BF16),
            pltpu.VMEM((ctx_len, dk), BF16), pltpu.VMEM((ctx_len, dk), BF16),
            pltpu.VMEM((n_blk, dk, LANE), F32), pltpu.VMEM((dk, LANE), F32),
            pltpu.VMEM((seq, dv), F32),
            pltpu.VMEM((2, dk, dv), F32),
        ],
        compiler_params=pltpu.CompilerParams(dimension_semantics=("arbitrary", "arbitrary"),
                                             vmem_limit_bytes=VMEM_LIMIT),
        name="gla",
    )(p, p, p, p, ga, pc, pc, gac, up_stack, bias2, gla_gn, *casts)


MXU_N = 256


POST_ROWS = 128


def _post_kernel(or_ref, og_ref, ga0_ref, ga1_ref, gb0_ref, gb1_ref, x_ref, wur_ref, wug_ref, wo_ref,
                 g1_ref, sh2_ref, sc2_ref, npost_ref, npre_ref, hs_ref, h2_ref):
    half = ga0_ref.shape[1]
    tm = hs_ref.shape[0]
    groups = [slice(r0, r0 + POST_ROWS) for r0 in range(0, tm, POST_ROWS)]

    def up(rows):
        return _dot(or_ref[rows, :], wur_ref[...]), _dot(og_ref[rows, :], wug_ref[...])

    def out(rows, y_ret, y_gla):
        def merge(ga_r, gb_r, lo):
            return (_sigmoid(ga_r[rows, :].astype(F32)) * y_ret[:, lo:lo + half]
                    + _sigmoid(gb_r[rows, :].astype(F32)) * y_gla[:, lo:lo + half]).astype(BF16)

        merged = jnp.concatenate([merge(ga0_ref, gb0_ref, 0), merge(ga1_ref, gb1_ref, half)], axis=1)
        return _dot(merged, wo_ref[...])

    post_gain = g1_ref[...] * npost_ref[...]
    pre_gain = npre_ref[...] * (1.0 + sc2_ref[...])

    def finish(rows, y):
        hs = x_ref[rows, :] + _rms(y) * post_gain
        hs_ref[rows, :] = hs
        h2 = _rms(hs) * pre_gain + sh2_ref[...]
        h2_ref[rows, :] = h2.astype(h2_ref.dtype)

    ups = [up(rows) for rows in groups]
    ys = [out(rows, *u) for rows, u in zip(groups, ups)]
    for rows, y in zip(groups, ys):
        finish(rows, y)


def _post(o_r, o_g, p, x2d, w_up_ret, w_up_gla, w_out, mod3, npost, npre, *, tm, seq):
    m, d = x2d.shape
    tiles_per_seq = seq // tm
    const = dict(pipeline_mode=pl.Buffered(1))
    gw = d // 2

    def modspec(chunk):
        return pl.BlockSpec((None, 1, d), lambda i: (i // tiles_per_seq, 0, chunk))

    def gatespec(name, part):
        return pl.BlockSpec((tm, gw), lambda i: (i, _DST[name] // gw + part))

    return pl.pallas_call(
        _post_kernel,
        out_shape=(jax.ShapeDtypeStruct((m, d), F32), jax.ShapeDtypeStruct((m, d), BF16)),
        grid=(m // tm,),
        in_specs=[
            pl.BlockSpec((tm, RET_VW), lambda i: (i, 0)),
            pl.BlockSpec((tm, GLA_VW), lambda i: (i, 0)),
            gatespec("gate_a", 0), gatespec("gate_a", 1), gatespec("gate_b", 0), gatespec("gate_b", 1),
            pl.BlockSpec((tm, d), lambda i: (i, 0)),
            pl.BlockSpec((RET_VW, d), lambda i: (0, 0), **const),
            pl.BlockSpec((GLA_VW, d), lambda i: (0, 0), **const),
            pl.BlockSpec((d, d), lambda i: (0, 0), **const),
            modspec(2), modspec(3), modspec(4),
            pl.BlockSpec((1, d), lambda i: (0, 0)),
            pl.BlockSpec((1, d), lambda i: (0, 0)),
        ],
        out_specs=(pl.BlockSpec((tm, d), lambda i: (i, 0)),
                   pl.BlockSpec((tm, d), lambda i: (i, 0))),
        compiler_params=pltpu.CompilerParams(dimension_semantics=("arbitrary",),
                                             vmem_limit_bytes=VMEM_LIMIT),
        name="post",
    )(o_r, o_g, p, p, p, p, x2d, w_up_ret, w_up_gla, w_out, mod3, mod3, mod3, npost, npre)


def _ffn_up_kernel(h_ref, wg_ref, wu_ref, wsrc_ref, a_ref, wdst_ref):
    wdst_ref[...] = wsrc_ref[...].astype(wdst_ref.dtype)
    h = h_ref[...]
    for lo in range(0, a_ref.shape[1], MXU_N):
        g = _dot(h, wg_ref[:, lo:lo + MXU_N])
        u = _dot(h, wu_ref[:, lo:lo + MXU_N])
        a_ref[:, lo:lo + MXU_N] = (g * _sigmoid(g) * u).astype(a_ref.dtype)


def _ffn_up(h2, wg, wu, w_later, *, tm, tf):
    m, d = h2.shape
    d_ff = wg.shape[1]
    grid = (m // tm, d_ff // tf)
    cast_spec = pl.BlockSpec(*_cast_plan(w_later, *grid))
    return pl.pallas_call(
        _ffn_up_kernel,
        out_shape=(jax.ShapeDtypeStruct((m, d_ff), BF16), jax.ShapeDtypeStruct(w_later.shape, BF16)),
        grid=grid,
        in_specs=[
            pl.BlockSpec((tm, d), lambda i, f: (i, 0)),
            pl.BlockSpec((d, tf), lambda i, f: (0, f)),
            pl.BlockSpec((d, tf), lambda i, f: (0, f)),
            cast_spec,
        ],
        out_specs=(pl.BlockSpec((tm, tf), lambda i, f: (i, f)), cast_spec),
        compiler_params=pltpu.CompilerParams(dimension_semantics=("arbitrary", "arbitrary"),
                                             vmem_limit_bytes=VMEM_LIMIT),
        name="ffn_up",
    )(h2, wg, wu, w_later)


def _ffn_down_kernel(a_ref, wd_ref, hs_ref, g2_ref, nw_ref, o_ref, y_scr, *, n_split):
    tm, d = o_ref.shape
    w = d // n_split
    ss = jnp.zeros((tm, 1), F32)
    for lo in range(0, d, w):
        y = _dot(a_ref[...], wd_ref[:, lo:lo + w])
        y_scr[:, lo:lo + w] = y
        ss = ss + jnp.sum(y * y, axis=-1, keepdims=True)
    inv = lax.rsqrt(ss * (1.0 / d) + EPS)
    o_ref[...] = hs_ref[...] + y_scr[...] * inv * (g2_ref[...] * nw_ref[...])


def _ffn_down(a, hs, wd, mod3, npost, *, tm, seq):
    m, d = hs.shape
    d_ff = wd.shape[0]
    tiles_per_seq = seq // tm
    kern = functools.partial(_ffn_down_kernel, n_split=4)
    return pl.pallas_call(
        kern,
        out_shape=jax.ShapeDtypeStruct((m, d), F32),
        grid=(m // tm,),
        in_specs=[
            pl.BlockSpec((tm, d_ff), lambda i: (i, 0)),
            pl.BlockSpec((d_ff, d), lambda i: (0, 0), pipeline_mode=pl.Buffered(1)),
            pl.BlockSpec((tm, d), lambda i: (i, 0)),
            pl.BlockSpec((None, 1, d), lambda i: (i // tiles_per_seq, 0, 5)),
            pl.BlockSpec((1, d), lambda i: (0, 0)),
        ],
        out_specs=pl.BlockSpec((tm, d), lambda i: (i, 0)),
        scratch_shapes=[pltpu.VMEM((tm, d), F32)],
        compiler_params=pltpu.CompilerParams(dimension_semantics=("arbitrary",),
                                             vmem_limit_bytes=VMEM_LIMIT),
        name="ffn_down",
    )(a, wd, hs, mod3, npost)


def _rope_tables(seq):
    f32 = np.float32
    rows = seq // GRID_W
    pos_r = np.repeat(np.arange(rows, dtype=f32), GRID_W)
    pos_c = np.tile(np.arange(GRID_W, dtype=f32), rows)
    n_f = RET_DK // 4
    inv = (f32(ROPE_BASE) ** (-np.arange(n_f, dtype=f32) / f32(n_f))).astype(f32)
    ang = np.concatenate([pos_r[:, None] * inv, pos_c[:, None] * inv], axis=-1).astype(f32)
    return jnp.asarray(np.cos(ang), F32), jnp.asarray(np.sin(ang), F32)


def _gate_map_layout(gla_a_up, gla_a_bias):
    r, h, dk = GLA_LOW_RANK, GLA_HEADS, GLA_DK
    u = jnp.zeros((GA_W, h, 2, dk), F32)
    u = u.at[:r, :, 0, :].set(gla_a_up[0].reshape(r, h, dk))
    u = u.at[r:, :, 1, :].set(gla_a_up[1].reshape(r, h, dk))
    u = u.reshape(GA_W, h * 2 * dk)
    up_stack = jnp.concatenate([u, u, u, jnp.zeros((LANE - 3 * GA_W, h * 2 * dk), F32)], axis=0)
    bias2 = jnp.stack([gla_a_bias[0].reshape(h, dk), gla_a_bias[1].reshape(h, dk)], axis=1)
    return up_stack, bias2.reshape(1, h * 2 * dk)


def _layer(h_state, c_rows, ctx2d, w_mod, b_mod, norm_mix_pre, norm_mix_post, norm_ffn_pre, norm_ffn_post,
           w_in, ret_decay, gla_a_up, gla_a_bias, ret_gn, gla_gn, w_up_ret, w_up_gla, w_out,
           ffn_w_gate, ffn_w_up, ffn_w_down, cos, sin, *, batch, seq, ctx_len):
    d = D_MODEL
    ga0 = _SRC["ga"][0]
    w_t = w_in.T
    w_g = w_t[ga0:ga0 + GA_W]
    w_ga = jnp.concatenate([w_g, w_g, w_g, jnp.zeros((LANE - 3 * GA_W, d), F32)], axis=0).astype(BF16)
    up_stack, bias2 = _gate_map_layout(gla_a_up, gla_a_bias)
    rd_b = jnp.broadcast_to(ret_decay.reshape(2, RET_HEADS, 1, 1), (2, RET_HEADS, 8, RET_DK))

    w_main, mod = _prologue(w_t, c_rows, w_mod, b_mod)
    mod3 = mod.reshape(mod.shape[0], 1, 6 * d)
    nw_pre = norm_mix_pre.reshape(1, d)

    tm = 1024
    p, ga = _inproj(h_state, nw_pre, mod3, lambda i: i // (seq // tm), w_main, w_ga, cos, sin,
                    tm=tm, tn=1024, ncols=N_MAIN, rope=True, seq_tiles=seq // tm)
    pc, gac = _inproj(ctx2d, nw_pre, mod3, lambda i: batch, w_main, w_ga, cos, sin,
                      tm=batch * ctx_len, tn=512, ncols=N_CTX, rope=False, seq_tiles=1)

    o_r, wg, wo = _retention(p, pc, rd_b, ret_gn.reshape(1, RET_VW), (ffn_w_gate, w_out),
                             batch=batch, seq=seq, ctx_len=ctx_len)
    o_g, wu, wur, wug = _gla(p, ga, pc, gac, up_stack, bias2, gla_gn.reshape(1, GLA_VW),
                             (ffn_w_up, w_up_ret, w_up_gla), batch=batch, seq=seq, ctx_len=ctx_len)

    hs, h2 = _post(o_r, o_g, p, h_state, wur, wug, wo,
                   mod3, norm_mix_post.reshape(1, d), norm_ffn_pre.reshape(1, d), tm=256, seq=seq)
    act, wd = _ffn_up(h2, wg, wu, ffn_w_down, tm=1024, tf=512)
    return _ffn_down(act, hs, wd, mod3, norm_ffn_post.reshape(1, d), tm=256, seq=seq)


def kernel(x, c, ctx, c_ctx, w_mod, b_mod, norm_mix_pre, norm_mix_post, norm_ffn_pre, norm_ffn_post,
           w_in, ret_decay, gla_a_up, gla_a_bias, ret_gn, gla_gn, w_up_ret, w_up_gla, w_out,
           ffn_w_gate, ffn_w_up, ffn_w_down):
    batch, seq, d = x.shape
    ctx_len = ctx.shape[1]
    depth = w_mod.shape[0]
    cos, sin = _rope_tables(seq)
    c_rows = jnp.zeros((16, d), F32).at[:batch].set(c.astype(F32)).at[batch].set(c_ctx.astype(F32))
    ctx2d = ctx.astype(F32).reshape(batch * ctx_len, d)
    h_state = x.astype(F32).reshape(batch * seq, d)
    for i in range(depth):
        h_state = _layer(h_state, c_rows, ctx2d, w_mod[i], b_mod[i], norm_mix_pre[i], norm_mix_post[i],
                         norm_ffn_pre[i], norm_ffn_post[i], w_in[i], ret_decay[i], gla_a_up[i], gla_a_bias[i],
                         ret_gn[i], gla_gn[i], w_up_ret[i], w_up_gla[i], w_out[i],
                         ffn_w_gate[i], ffn_w_up[i], ffn_w_down[i], cos, sin,
                         batch=batch, seq=seq, ctx_len=ctx_len)
    return h_state.reshape(batch, seq, d).astype(x.dtype)
```

```python
import functools

import jax
import jax.numpy as jnp
import numpy as np
from jax import lax
from jax.experimental import pallas as pl
from jax.experimental.pallas import tpu as pltpu

F32 = jnp.float32
BF16 = jnp.bfloat16

D_MODEL = 2048
GRID_W = 64
RET_HEADS = 4
RET_DK = 256
RET_DV = 256
GLA_HEADS = 4
GLA_DK = 128
GLA_DV = 256
GLA_LOW_RANK = 16
GLA_GATE_NORM = 16.0
ROPE_BASE = 10000.0
EPS = 1e-6

RET_QK = RET_HEADS * RET_DK
RET_VW = RET_HEADS * RET_DV
GLA_KW = GLA_HEADS * GLA_DK
GLA_VW = GLA_HEADS * GLA_DV
GA_W = 2 * GLA_LOW_RANK
LANE = 128

_SRC = {}
_off = 0
for _name, _w in (("rk", RET_QK), ("rv", RET_VW), ("gk", GLA_KW), ("gv", GLA_VW), ("ga", GA_W),
                  ("rq", RET_QK), ("rg", RET_VW), ("gq", GLA_KW), ("gg", GLA_VW),
                  ("gate_a", D_MODEL), ("gate_b", D_MODEL)):
    _SRC[_name] = (_off, _w)
    _off += _w

_ORDER = ("rk", "rv", "gv", "gk", "gq", "rq", "rg", "gg", "gate_a", "gate_b")
_DST = {}
_off = 0
for _name in _ORDER:
    _DST[_name] = _off
    _off += _SRC[_name][1]
N_MAIN = _off
N_CTX = _DST["gq"]
PACK_W = 512

VMEM_LIMIT = 56 * 1024 * 1024


def _dot(a, b):
    return jnp.dot(a, b, preferred_element_type=F32)


def _dot_nt(a, b):
    return lax.dot_general(a, b, (((1,), (1,)), ((), ())), preferred_element_type=F32)


def _dot_tn(a, b):
    return lax.dot_general(a, b, (((0,), (0,)), ((), ())), preferred_element_type=F32)


def _sigmoid(x):
    return 1.0 / (1.0 + jnp.exp(-x))


def _rms(x):
    return x * lax.rsqrt(jnp.mean(x * x, axis=-1, keepdims=True) + EPS)


def _split_bf16(x):
    hi = x.astype(BF16)
    lo = (x - hi.astype(F32)).astype(BF16)
    return hi, lo


MOD_TN = 1024


def _prologue_kernel(a_idx_ref, n_idx_ref, shift_ref, a_ref, n_ref, c_ref, wm_ref, bm_ref,
                     o_ref, mod_ref, *, n_mod):
    j = pl.program_id(0)

    @pl.when(shift_ref[j] == 0)
    def _():
        o_ref[...] = a_ref[...].astype(o_ref.dtype)

    @pl.when(shift_ref[j] != 0)
    def _():
        o_ref[...] = jnp.concatenate([a_ref[GA_W:, :], n_ref[...]], axis=0).astype(o_ref.dtype)

    @pl.when(j < n_mod)
    def _():
        cf = c_ref[...]
        rows = cf.shape[0]
        s_hi, s_lo = _split_bf16(cf * _sigmoid(cf))
        w_hi, w_lo = _split_bf16(wm_ref[...])
        r = _dot(jnp.concatenate([s_hi, s_lo], axis=0), w_hi)
        mod_ref[...] = r[:rows] + r[rows:] + _dot(s_hi, w_lo) + bm_ref[...]


def _prologue(w_t, c_rows, w_mod, b_mod):
    d = w_t.shape[1]
    rows = c_rows.shape[0]
    n_modcols = w_mod.shape[1]
    n_mod = n_modcols // MOD_TN
    a_idx, n_idx, shift = [], [], []
    for name in _ORDER:
        src, width = _SRC[name]
        for c in range(src, src + width, PACK_W):
            base = c - c % PACK_W
            assert c - base in (0, GA_W)
            a_idx.append(base // PACK_W)
            n_idx.append((base + PACK_W) // GA_W if c != base else (n_idx[-1] if n_idx else 0))
            shift.append(c - base)
    n_tiles = len(a_idx)
    assert n_mod <= n_tiles
    as_i32 = lambda v: jnp.asarray(v, jnp.int32)
    mod_tile = lambda j, a, n, s: (0, jnp.minimum(j, n_mod - 1))
    return pl.pallas_call(
        functools.partial(_prologue_kernel, n_mod=n_mod),
        out_shape=(jax.ShapeDtypeStruct((n_tiles * PACK_W, d), BF16),
                   jax.ShapeDtypeStruct((rows, n_modcols), F32)),
        grid_spec=pltpu.PrefetchScalarGridSpec(
            num_scalar_prefetch=3,
            grid=(n_tiles,),
            in_specs=[pl.BlockSpec((PACK_W, d), lambda j, a, n, s: (a[j], 0)),
                      pl.BlockSpec((GA_W, d), lambda j, a, n, s: (n[j], 0)),
                      pl.BlockSpec((rows, d), lambda j, a, n, s: (0, 0)),
                      pl.BlockSpec((d, MOD_TN), mod_tile),
                      pl.BlockSpec((1, MOD_TN), mod_tile)],
            out_specs=(pl.BlockSpec((PACK_W, d), lambda j, a, n, s: (j, 0)),
                       pl.BlockSpec((rows, MOD_TN), mod_tile)),
        ),
        compiler_params=pltpu.CompilerParams(dimension_semantics=("arbitrary",),
                                             vmem_limit_bytes=VMEM_LIMIT),
        name="prologue",
    )(as_i32(a_idx), as_i32(n_idx), as_i32(shift), w_t, w_t, c_rows, w_mod, b_mod.reshape(1, n_modcols))


def _inproj_kernel(x_ref, nw_ref, sh_ref, sc_ref, w_ref, wga_ref, cos_ref, sin_ref, *rest,
                   rope_tiles, tn, n_cast):
    cast_in = rest[:n_cast]
    o_ref, ga_ref = rest[n_cast:n_cast + 2]
    cast_out = rest[n_cast + 2:2 * n_cast + 2]
    h_scr = rest[2 * n_cast + 2]
    j = pl.program_id(1)

    def ride_along_casts():
        for src, dst in zip(cast_in, cast_out):
            dst[...] = src[...].astype(dst.dtype)

    @pl.when(j == 0)
    def _():
        h = _rms(x_ref[...]) * (nw_ref[...] * (1.0 + sc_ref[...])) + sh_ref[...]
        hb = h.astype(BF16)
        h_scr[...] = hb
        g3 = _dot_nt(hb, wga_ref[...])
        lane = lax.broadcasted_iota(jnp.int32, g3.shape, 1)
        resid = g3 - g3.astype(BF16).astype(F32)
        ga_ref[...] = jnp.where((lane >= GA_W) & (lane < 2 * GA_W), resid, g3).astype(ga_ref.dtype)

    if not rope_tiles:
        ride_along_casts()
        o_ref[...] = _dot_nt(h_scr[...], w_ref[...]).astype(o_ref.dtype)
        return

    is_rope = functools.reduce(jnp.logical_or, [j == t for t in rope_tiles])

    @pl.when(is_rope)
    def _():
        ride_along_casts()
        cos = cos_ref[...]
        sin = sin_ref[...]
        half = RET_DK // 2
        for hd in range(0, tn, RET_DK):
            acc = _dot_nt(h_scr[...], w_ref[hd:hd + RET_DK, :])
            t1 = acc[:, :half]
            t2 = acc[:, half:]
            o_ref[:, hd:hd + half] = (t1 * cos - t2 * sin).astype(o_ref.dtype)
            o_ref[:, hd + half:hd + RET_DK] = (t1 * sin + t2 * cos).astype(o_ref.dtype)

    @pl.when(jnp.logical_not(is_rope))
    def _():
        ride_along_casts()
        o_ref[...] = _dot_nt(h_scr[...], w_ref[...]).astype(o_ref.dtype)


def _cast_plan(arr, n_i, n_j):
    rows, cols = arr.shape

    def pieces(extent, want, align):
        n = want
        while extent % n or (extent // n) % align:
            n -= 1
        return n

    if rows % n_j == 0 and (rows // n_j) % 16 == 0:
        nr, nc = n_j, pieces(cols, n_i, LANE)
        index = lambda i, j, nc=nc: (j, jnp.minimum(i, nc - 1))
    else:
        nr, nc = pieces(rows, n_i, 16), pieces(cols, n_j, LANE)
        index = lambda i, j, nr=nr, nc=nc: (jnp.minimum(i, nr - 1), jnp.minimum(j, nc - 1))
    return (rows // nr, cols // nc), index


def _inproj(x2d, norm_w, mod3, mod_row_of_tile, w_main, w_ga, cos, sin, *, tm, tn, ncols, rope, seq_tiles,
            casts=()):
    m, d = x2d.shape
    rope_tiles = ()
    if rope:
        assert tn == RET_QK and _DST["rk"] % tn == 0 and _DST["rq"] % tn == 0
        rope_tiles = (_DST["rk"] // tn, _DST["rq"] // tn)
    grid = (m // tm, ncols // tn)
    cast_specs = [pl.BlockSpec(*_cast_plan(a, *grid)) for a in casts]
    kern = functools.partial(_inproj_kernel, rope_tiles=rope_tiles, tn=tn, n_cast=len(casts))
    return pl.pallas_call(
        kern,
        out_shape=(jax.ShapeDtypeStruct((m, ncols), BF16),
                   jax.ShapeDtypeStruct((m, LANE), BF16),
                   *[jax.ShapeDtypeStruct(a.shape, BF16) for a in casts]),
        grid=grid,
        in_specs=[
            pl.BlockSpec((tm, d), lambda i, j: (i, 0)),
            pl.BlockSpec((1, d), lambda i, j: (0, 0)),
            pl.BlockSpec((None, 1, d), lambda i, j: (mod_row_of_tile(i), 0, 0)),
            pl.BlockSpec((None, 1, d), lambda i, j: (mod_row_of_tile(i), 0, 1)),
            pl.BlockSpec((tn, d), lambda i, j: (j, 0)),
            pl.BlockSpec((LANE, d), lambda i, j: (0, 0)),
            pl.BlockSpec((tm, RET_DK // 2), lambda i, j: (i % seq_tiles, 0)),
            pl.BlockSpec((tm, RET_DK // 2), lambda i, j: (i % seq_tiles, 0)),
            *cast_specs,
        ],
        out_specs=(pl.BlockSpec((tm, tn), lambda i, j: (i, j)),
                   pl.BlockSpec((tm, LANE), lambda i, j: (i, 0)),
                   *cast_specs),
        scratch_shapes=[pltpu.VMEM((tm, d), BF16)],
        compiler_params=pltpu.CompilerParams(dimension_semantics=("arbitrary", "arbitrary"),
                                             vmem_limit_bytes=VMEM_LIMIT),
        name="inproj",
    )(x2d, norm_w, mod3, mod3, w_main, w_ga, cos, sin, *casts)


RET_CHUNK = 256


def _ride_along_casts(rest, n_cast):
    for src, dst in zip(rest[:n_cast], rest[n_cast + 1:2 * n_cast + 1]):
        dst[...] = src[...].astype(dst.dtype)
    return rest[n_cast], rest[2 * n_cast + 1:]


def _ret_kernel(q_ref, k_ref, v_ref, g_ref, kc_ref, vc_ref, rd_ref, gn_ref, *rest, n_chunks, n_cast):
    o_ref, (o_scr, s_scr, dm_scr, eq_scr, wk_scr, sd_scr) = _ride_along_casts(rest, n_cast)
    c_len = RET_CHUNK
    ii = lax.broadcasted_iota(jnp.int32, (c_len, c_len), 0)
    jj = lax.broadcasted_iota(jnp.int32, (c_len, c_len), 1)
    rowi = lax.broadcasted_iota(jnp.int32, (c_len, RET_DK), 0).astype(F32)
    scale = RET_DK ** -0.5

    for direction in (0, 1):
        lg = -jnp.exp(rd_ref[direction][0:1, :])
        if direction == 0:
            dm_scr[direction] = jnp.where(jj <= ii, jnp.exp((ii - jj).astype(F32) * lg), 0.0)
            eq_scr[direction] = jnp.exp((rowi + 1.0) * lg)
            wk = jnp.exp((c_len - 1.0 - rowi) * lg)
        else:
            dm_scr[direction] = jnp.where(jj > ii, jnp.exp((jj - ii).astype(F32) * lg), 0.0)
            eq_scr[direction] = jnp.exp((c_len - rowi) * lg)
            wk = jnp.exp(rowi * lg)
        wk_scr[direction] = wk
        sd_scr[direction] = jnp.broadcast_to(jnp.exp(float(c_len) * lg), (8, RET_DV))
        kcw = (kc_ref[...].astype(F32) * wk).astype(BF16)
        s_scr[direction] = _dot_tn(kcw, vc_ref[...])

    def chunk(direction, c, first):
        r0 = pl.multiple_of(c * c_len, c_len)
        q = q_ref[pl.ds(r0, c_len), :]
        k = k_ref[pl.ds(r0, c_len), :]
        v = v_ref[pl.ds(r0, c_len), :]
        s_qk = _dot_nt(q, k)
        s_state = s_scr[direction]
        o_state = _dot(q, s_state.astype(BF16))
        yield
        sc = (s_qk * dm_scr[direction]).astype(BF16)
        kw = (k.astype(F32) * wk_scr[direction]).astype(BF16)
        yield
        o = _dot(sc, v) + eq_scr[direction] * o_state
        s_scr[direction] = sd_scr[direction][0:1, :] * s_state + _dot_tn(kw, v)
        yield
        if first:
            o_scr[pl.ds(r0, c_len), :] = o
        else:
            tot = o_scr[pl.ds(r0, c_len), :] + o
            mu = jnp.mean(tot, axis=-1, keepdims=True)
            cen = tot - mu
            var = jnp.mean(cen * cen, axis=-1, keepdims=True)
            nrm = cen * (scale * lax.rsqrt(var * (scale * scale) + EPS))
            g = g_ref[pl.ds(r0, c_len), :].astype(F32)
            o_ref[pl.ds(r0, c_len), :] = (nrm * gn_ref[...] * (g * _sigmoid(g))).astype(o_ref.dtype)
        yield

    def run(*stages):
        live = list(stages)
        while live:
            live = [g for g in live if next(g, live) is not live]

    def body(t, carry, first):
        run(chunk(0, t, first), chunk(1, n_chunks - 1 - t, first))
        return carry

    lax.fori_loop(0, n_chunks // 2, functools.partial(body, first=True), 0, unroll=4)
    lax.fori_loop(n_chunks // 2, n_chunks, functools.partial(body, first=False), 0, unroll=4)


def _retention(p, pc, rd_b, ret_gn, casts, *, batch, seq, ctx_len):
    assert ctx_len == RET_CHUNK and (seq // RET_CHUNK) % 2 == 0
    n_chunks = seq // RET_CHUNK
    w = RET_DK
    kern = functools.partial(_ret_kernel, n_chunks=n_chunks, n_cast=len(casts))
    cast_specs = [pl.BlockSpec(*_cast_plan(a, batch, RET_HEADS)) for a in casts]

    def col(name):
        return _DST[name] // w

    return pl.pallas_call(
        kern,
        out_shape=(jax.ShapeDtypeStruct((batch * seq, RET_VW), BF16),
                   *[jax.ShapeDtypeStruct(a.shape, BF16) for a in casts]),
        grid=(batch, RET_HEADS),
        in_specs=[
            pl.BlockSpec((seq, w), lambda b, h: (b, col("rq") + h)),
            pl.BlockSpec((seq, w), lambda b, h: (b, col("rk") + h)),
            pl.BlockSpec((seq, w), lambda b, h: (b, col("rv") + h)),
            pl.BlockSpec((seq, w), lambda b, h: (b, col("rg") + h)),
            pl.BlockSpec((ctx_len, w), lambda b, h: (b, col("rk") + h)),
            pl.BlockSpec((ctx_len, w), lambda b, h: (b, col("rv") + h)),
            pl.BlockSpec((2, None, 8, w), lambda b, h: (0, h, 0, 0)),
            pl.BlockSpec((1, w), lambda b, h: (0, h)),
            *cast_specs,
        ],
        out_specs=(pl.BlockSpec((seq, w), lambda b, h: (b, h)), *cast_specs),
        scratch_shapes=[
            pltpu.VMEM((seq, RET_DV), F32),
            pltpu.VMEM((2, RET_DK, RET_DV), F32),
            pltpu.VMEM((2, RET_CHUNK, RET_CHUNK), F32),
            pltpu.VMEM((2, RET_CHUNK, RET_DV), F32),
            pltpu.VMEM((2, RET_CHUNK, RET_DK), F32),
            pltpu.VMEM((2, 8, RET_DV), F32),
        ],
        compiler_params=pltpu.CompilerParams(dimension_semantics=("arbitrary", "arbitrary"),
                                             vmem_limit_bytes=VMEM_LIMIT),
        name="retention",
    )(p, p, p, p, pc, pc, rd_b, ret_gn, *casts)


GLA_SUB = 64
GLA_BLK = 256
GLA_SCAN_BLK = 512
PREP_UNROLL = 7


def _gla_kernel(q_ref, k_ref, v_ref, g_ref, ga_ref, kc_ref, vc_ref, gac_ref, up_ref, bias_ref, gn_ref,
                *rest, n_blk, n_cast):
    o_ref, scratch = _ride_along_casts(rest, n_cast)
    (ahi_scr, alo_scr, qf_scr, qb_scr, ktf_scr, ktb_scr, khf_scr, khb_scr, khcf_scr, khcb_scr,
     dec_scr, decc_scr, o_scr, s_scr) = scratch
    blk, sub, dk = GLA_BLK, GLA_SUB, GLA_DK
    nsub = blk // sub
    shift = sub.bit_length() - 1
    scale = GLA_DK ** -0.5

    def masks(n):
        ii = lax.broadcasted_iota(jnp.int32, (n, n), 0)
        jj = lax.broadcasted_iota(jnp.int32, (n, n), 1)
        same = (ii >> shift) == (jj >> shift)
        lower = jj <= ii
        return same, lower, jnp.where(same & lower, 1.0, 0.0).astype(BF16)

    ctx_rows = gac_ref.shape[0]
    blk_masks = masks(blk)
    ctx_masks = blk_masks if ctx_rows == blk else masks(ctx_rows)
    r8 = lax.broadcasted_iota(jnp.int32, (8, dk), 0)

    u = up_ref[...]
    u_hi, u_lo = _split_bf16(u)
    urow = lax.broadcasted_iota(jnp.int32, u.shape, 0)
    rhs = jnp.where(urow >= 2 * GA_W, u_lo, u_hi)
    bias = bias_ref[...]

    def gate_stage(ga_blk, put):
        z = _dot(ga_blk, rhs) + bias
        yield
        a = (jnp.minimum(z, 0.0) - jnp.log(1.0 + jnp.exp(-jnp.abs(z)))) * (1.0 / GLA_GATE_NORM)
        put(*_split_bf16(a))
        yield

    def decay_stage(a_hi, a_lo, k_blk, q_blk, msk, store):
        same, lower, tmat = msk
        nsub = a_hi.shape[0] // sub
        assert nsub <= 8
        pre = _dot(tmat, a_hi) + _dot(tmat, a_lo)
        yield
        a = a_hi.astype(F32) + a_lo.astype(F32)
        lasts = [pre[i * sub + sub - 1:i * sub + sub, :] for i in range(nsub)]
        tot = jnp.concatenate([jnp.broadcast_to(l, (sub, 2 * dk)) for l in lasts], axis=0)
        b_f = pre[:, :dk]
        ex_b = pre[:, dk:] - a[:, dk:]
        b_b = tot[:, dk:] - ex_b
        kf = k_blk.astype(F32)
        khf = (kf * jnp.exp(tot[:, :dk] - b_f)).astype(BF16)
        khb = (kf * jnp.exp(ex_b)).astype(BF16)

        def tile8(half):
            rows = [jnp.broadcast_to(l[:, half * dk:(half + 1) * dk], (8, dk)) for l in lasts]
            t = rows[nsub - 1]
            for i in range(nsub - 2, -1, -1):
                t = jnp.where(r8 == i, rows[i], t)
            return t

        dec = jnp.exp(jnp.concatenate([tile8(0), tile8(1), jnp.zeros((LANE - 16, dk), F32)], axis=0))
        out = dict(khf=khf, khb=khb, dec_t=dec.T)
        if q_blk is not None:
            qf32 = q_blk.astype(F32)
            out.update(qf=(qf32 * jnp.exp(b_f)).astype(BF16), qb=(qf32 * jnp.exp(b_b)).astype(BF16),
                       ktf=(kf * jnp.exp(-b_f)).astype(BF16), ktb=(kf * jnp.exp(-b_b)).astype(BF16))
        store(**out)
        yield

    def score_stage(r):
        same, lower, _ = blk_masks
        rows = pl.ds(r, blk)
        s_f = _dot_nt(qf_scr[rows, :], ktf_scr[rows, :])
        s_b = _dot_nt(qb_scr[rows, :], ktb_scr[rows, :])
        yield
        s = jnp.where(same, jnp.where(lower, s_f, s_b), 0.0).astype(BF16)
        yield
        o_scr[rows, :] = _dot(s, v_ref[rows, :])
        yield

    def run(*stages):
        live = list(stages)
        while live:
            live = [g for g in live if next(g, live) is not live]

    def store_ctx(khf, khb, dec_t):
        khcf_scr[...] = khf
        khcb_scr[...] = khb
        decc_scr[...] = dec_t

    ctx_gate = []
    run(gate_stage(gac_ref[...], lambda hi, lo: ctx_gate.extend((hi, lo))))
    run(decay_stage(*ctx_gate, kc_ref[...], None, ctx_masks, store_ctx))

    def gate_block(t):
        rows = pl.ds(pl.multiple_of(t * blk, blk), blk)

        def put(a_hi, a_lo):
            ahi_scr[rows, :] = a_hi
            alo_scr[rows, :] = a_lo

        return gate_stage(ga_ref[rows, :], put)

    def decay_block(t):
        rows = pl.ds(pl.multiple_of(t * blk, blk), blk)

        def store(khf, khb, dec_t, qf, qb, ktf, ktb):
            khf_scr[rows, :] = khf
            khb_scr[rows, :] = khb
            qf_scr[rows, :] = qf
            qb_scr[rows, :] = qb
            ktf_scr[rows, :] = ktf
            ktb_scr[rows, :] = ktb
            dec_scr[t] = dec_t

        return decay_stage(ahi_scr[rows, :], alo_scr[rows, :], k_ref[rows, :], q_ref[rows, :],
                           blk_masks, store)

    def score_block(t):
        return score_stage(pl.multiple_of(t * blk, blk))

    run(gate_block(0))
    run(gate_block(1), decay_block(0))

    def prep_body(t, carry):
        run(gate_block(t + 2), decay_block(t + 1), score_block(t))
        return carry

    lax.fori_loop(0, n_blk - 2, prep_body, 0, unroll=PREP_UNROLL)
    run(decay_block(n_blk - 1), score_block(n_blk - 2))
    run(score_block(n_blk - 1))

    def scan_blk(direction, n_sub, dec_ts, kh_at, v_at, q_at=None, o_rows=None):
        s = s_scr[direction]
        order = range(n_sub) if direction == 0 else range(n_sub - 1, -1, -1)
        kv = {i: _dot_tn(kh_at(i), v_at(i)) for i in order}
        yield
        for i in order:
            c = direction * 8 + i % nsub
            col = dec_ts[i // nsub][:, c:c + 1]
            if q_at is not None:
                rows = o_rows(i)
                o_scr[rows, :] += _dot(q_at(i), s.astype(BF16))
            s = s * col + kv[i]
            yield
        s_scr[direction] = s

    s_scr[...] = jnp.zeros(s_scr.shape, F32)
    run(*[scan_blk(direction, ctx_rows // sub, [decc_scr[...]],
                   lambda i, khc=khc: khc[i * sub:(i + 1) * sub, :],
                   lambda i: vc_ref[i * sub:(i + 1) * sub, :])
          for direction, khc in ((0, khcf_scr), (1, khcb_scr))])

    sblk = GLA_SCAN_BLK
    per = sblk // blk
    n_sblk = n_blk // per

    fin_rows = 128

    def finalize(t):
        for tb in (t, n_sblk - 1 - t):
            for r0 in range(0, sblk, fin_rows):
                rows = pl.ds(pl.multiple_of(tb * sblk + r0, fin_rows), fin_rows)
                o = o_scr[rows, :]
                ms = jnp.mean(o * o, axis=-1, keepdims=True)
                nrm = o * (scale * lax.rsqrt(ms * (scale * scale) + EPS))
                g = g_ref[rows, :].astype(F32)
                o_ref[rows, :] = (nrm * gn_ref[...] * (g * _sigmoid(g))).astype(o_ref.dtype)
                yield

    def scans(t):
        for direction, q_scr, kh_scr in ((0, qf_scr, khf_scr), (1, qb_scr, khb_scr)):
            tb = t if direction == 0 else n_sblk - 1 - t

            def rows(i, tb=tb):
                return pl.ds(pl.multiple_of(tb * sblk + i * sub, sub), sub)

            yield from scan_blk(direction, sblk // sub, [dec_scr[tb * per + j] for j in range(per)],
                                lambda i, kh_scr=kh_scr, rows=rows: kh_scr[rows(i), :],
                                lambda i, rows=rows: v_ref[rows(i), :],
                                lambda i, q_scr=q_scr, rows=rows: q_scr[rows(i), :],
                                rows)

    def scan_body(t, carry, fin_prev):
        if fin_prev:
            run(scans(t), finalize(t - 1))
        else:
            run(scans(t))
        return carry

    half = n_sblk // 2
    lax.fori_loop(0, half + 1, functools.partial(scan_body, fin_prev=False), 0, unroll=half + 1)
    lax.fori_loop(half + 1, n_sblk, functools.partial(scan_body, fin_prev=True), 0, unroll=n_sblk - half - 1)
    run(finalize(n_sblk - 1))


def _gla(p, ga, pc, gac, up_stack, bias2, gla_gn, casts, *, batch, seq, ctx_len):
    assert ctx_len % GLA_SUB == 0 and ctx_len <= GLA_BLK and GLA_SCAN_BLK % GLA_BLK == 0
    assert (seq // GLA_SCAN_BLK) % 2 == 0
    n_blk = seq // GLA_BLK
    kern = functools.partial(_gla_kernel, n_blk=n_blk, n_cast=len(casts))
    dk, dv = GLA_DK, GLA_DV
    cast_specs = [pl.BlockSpec(*_cast_plan(a, batch, GLA_HEADS)) for a in casts]
    return pl.pallas_call(
        kern,
        out_shape=(jax.ShapeDtypeStruct((batch * seq, GLA_VW), BF16),
                   *[jax.ShapeDtypeStruct(a.shape, BF16) for a in casts]),
        grid=(batch, GLA_HEADS),
        in_specs=[
            pl.BlockSpec((seq, dk), lambda b, h: (b, _DST["gq"] // dk + h)),
            pl.BlockSpec((seq, dk), lambda b, h: (b, _DST["gk"] // dk + h)),
            pl.BlockSpec((seq, dv), lambda b, h: (b, _DST["gv"] // dv + h)),
            pl.BlockSpec((seq, dv), lambda b, h: (b, _DST["gg"] // dv + h)),
            pl.BlockSpec((seq, LANE), lambda b, h: (b, 0)),
            pl.BlockSpec((ctx_len, dk), lambda b, h: (b, _DST["gk"] // dk + h)),
            pl.BlockSpec((ctx_len, dv), lambda b, h: (b, _DST["gv"] // dv + h)),
            pl.BlockSpec((ctx_len, LANE), lambda b, h: (b, 0)),
            pl.BlockSpec((LANE, 2 * dk), lambda b, h: (0, h)),
            pl.BlockSpec((1, 2 * dk), lambda b, h: (0, h)),
            pl.BlockSpec((1, dv), lambda b, h: (0, h)),
            *cast_specs,
        ],
        out_specs=(pl.BlockSpec((seq, dv), lambda b, h: (b, h)), *cast_specs),
        scratch_shapes=[
            pltpu.VMEM((seq, 2 * dk), BF16), pltpu.VMEM((seq, 2 * dk), BF16),
            pltpu.VMEM((seq, dk), BF16), pltpu.VMEM((seq, dk), BF16),
            pltpu.VMEM((seq, dk), BF16), pltpu.VMEM((seq, dk), BF16),
            pltpu.VMEM((seq, dk), BF16), pltpu.VMEM((seq, dk), BF16),
            pltpu.VMEM((ctx_len, dk), BF16), pltpu.VMEM((ctx_len, dk), BF16),
            pltpu.VMEM((n_blk, dk, LANE), F32), pltpu.VMEM((dk, LANE), F32),
            pltpu.VMEM((seq, dv), F32),
            pltpu.VMEM((2, dk, dv), F32),
        ],
        compiler_params=pltpu.CompilerParams(dimension_semantics=("arbitrary", "arbitrary"),
                                             vmem_limit_bytes=VMEM_LIMIT),
        name="gla",
    )(p, p, p, p, ga, pc, pc, gac, up_stack, bias2, gla_gn, *casts)


MXU_N = 256


POST_ROWS = 128


def _post_kernel(or_ref, og_ref, ga0_ref, ga1_ref, gb0_ref, gb1_ref, x_ref, wur_ref, wug_ref, wo_ref,
                 g1_ref, sh2_ref, sc2_ref, npost_ref, npre_ref, hs_ref, h2_ref):
    half = ga0_ref.shape[1]
    tm = hs_ref.shape[0]
    groups = [slice(r0, r0 + POST_ROWS) for r0 in range(0, tm, POST_ROWS)]

    def up(rows):
        return _dot(or_ref[rows, :], wur_ref[...]), _dot(og_ref[rows, :], wug_ref[...])

    def out(rows, y_ret, y_gla):
        def merge(ga_r, gb_r, lo):
            return (_sigmoid(ga_r[rows, :].astype(F32)) * y_ret[:, lo:lo + half]
                    + _sigmoid(gb_r[rows, :].astype(F32)) * y_gla[:, lo:lo + half]).astype(BF16)

        merged = jnp.concatenate([merge(ga0_ref, gb0_ref, 0), merge(ga1_ref, gb1_ref, half)], axis=1)
        return _dot(merged, wo_ref[...])

    post_gain = g1_ref[...] * npost_ref[...]
    pre_gain = npre_ref[...] * (1.0 + sc2_ref[...])

    def finish(rows, y):
        hs = x_ref[rows, :] + _rms(y) * post_gain
        hs_ref[rows, :] = hs
        h2 = _rms(hs) * pre_gain + sh2_ref[...]
        h2_ref[rows, :] = h2.astype(h2_ref.dtype)

    ups = [up(rows) for rows in groups]
    ys = [out(rows, *u) for rows, u in zip(groups, ups)]
    for rows, y in zip(groups, ys):
        finish(rows, y)


def _post(o_r, o_g, p, x2d, w_up_ret, w_up_gla, w_out, mod3, npost, npre, *, tm, seq):
    m, d = x2d.shape
    tiles_per_seq = seq // tm
    const = dict(pipeline_mode=pl.Buffered(1))
    gw = d // 2

    def modspec(chunk):
        return pl.BlockSpec((None, 1, d), lambda i: (i // tiles_per_seq, 0, chunk))

    def gatespec(name, part):
        return pl.BlockSpec((tm, gw), lambda i: (i, _DST[name] // gw + part))

    return pl.pallas_call(
        _post_kernel,
        out_shape=(jax.ShapeDtypeStruct((m, d), F32), jax.ShapeDtypeStruct((m, d), BF16)),
        grid=(m // tm,),
        in_specs=[
            pl.BlockSpec((tm, RET_VW), lambda i: (i, 0)),
            pl.BlockSpec((tm, GLA_VW), lambda i: (i, 0)),
            gatespec("gate_a", 0), gatespec("gate_a", 1), gatespec("gate_b", 0), gatespec("gate_b", 1),
            pl.BlockSpec((tm, d), lambda i: (i, 0)),
            pl.BlockSpec((RET_VW, d), lambda i: (0, 0), **const),
            pl.BlockSpec((GLA_VW, d), lambda i: (0, 0), **const),
            pl.BlockSpec((d, d), lambda i: (0, 0), **const),
            modspec(2), modspec(3), modspec(4),
            pl.BlockSpec((1, d), lambda i: (0, 0)),
            pl.BlockSpec((1, d), lambda i: (0, 0)),
        ],
        out_specs=(pl.BlockSpec((tm, d), lambda i: (i, 0)),
                   pl.BlockSpec((tm, d), lambda i: (i, 0))),
        compiler_params=pltpu.CompilerParams(dimension_semantics=("arbitrary",),
                                             vmem_limit_bytes=VMEM_LIMIT),
        name="post",
    )(o_r, o_g, p, p, p, p, x2d, w_up_ret, w_up_gla, w_out, mod3, mod3, mod3, npost, npre)


def _ffn_up_kernel(h_ref, wg_ref, wu_ref, wsrc_ref, a_ref, wdst_ref):
    wdst_ref[...] = wsrc_ref[...].astype(wdst_ref.dtype)
    h = h_ref[...]
    for lo in range(0, a_ref.shape[1], MXU_N):
        g = _dot(h, wg_ref[:, lo:lo + MXU_N])
        u = _dot(h, wu_ref[:, lo:lo + MXU_N])
        a_ref[:, lo:lo + MXU_N] = (g * _sigmoid(g) * u).astype(a_ref.dtype)


def _ffn_up(h2, wg, wu, w_later, *, tm, tf):
    m, d = h2.shape
    d_ff = wg.shape[1]
    grid = (m // tm, d_ff // tf)
    cast_spec = pl.BlockSpec(*_cast_plan(w_later, *grid))
    return pl.pallas_call(
        _ffn_up_kernel,
        out_shape=(jax.ShapeDtypeStruct((m, d_ff), BF16), jax.ShapeDtypeStruct(w_later.shape, BF16)),
        grid=grid,
        in_specs=[
            pl.BlockSpec((tm, d), lambda i, f: (i, 0)),
            pl.BlockSpec((d, tf), lambda i, f: (0, f)),
            pl.BlockSpec((d, tf), lambda i, f: (0, f)),
            cast_spec,
        ],
        out_specs=(pl.BlockSpec((tm, tf), lambda i, f: (i, f)), cast_spec),
        compiler_params=pltpu.CompilerParams(dimension_semantics=("arbitrary", "arbitrary"),
                                             vmem_limit_bytes=VMEM_LIMIT),
        name="ffn_up",
    )(h2, wg, wu, w_later)


def _ffn_down_kernel(a_ref, wd_ref, hs_ref, g2_ref, nw_ref, o_ref, y_scr, *, n_split):
    tm, d = o_ref.shape
    w = d // n_split
    ss = jnp.zeros((tm, 1), F32)
    for lo in range(0, d, w):
        y = _dot(a_ref[...], wd_ref[:, lo:lo + w])
        y_scr[:, lo:lo + w] = y
        ss = ss + jnp.sum(y * y, axis=-1, keepdims=True)
    inv = lax.rsqrt(ss * (1.0 / d) + EPS)
    o_ref[...] = hs_ref[...] + y_scr[...] * inv * (g2_ref[...] * nw_ref[...])


def _ffn_down(a, hs, wd, mod3, npost, *, tm, seq):
    m, d = hs.shape
    d_ff = wd.shape[0]
    tiles_per_seq = seq // tm
    kern = functools.partial(_ffn_down_kernel, n_split=4)
    return pl.pallas_call(
        kern,
        out_shape=jax.ShapeDtypeStruct((m, d), F32),
        grid=(m // tm,),
        in_specs=[
            pl.BlockSpec((tm, d_ff), lambda i: (i, 0)),
            pl.BlockSpec((d_ff, d), lambda i: (0, 0), pipeline_mode=pl.Buffered(1)),
            pl.BlockSpec((tm, d), lambda i: (i, 0)),
            pl.BlockSpec((None, 1, d), lambda i: (i // tiles_per_seq, 0, 5)),
            pl.BlockSpec((1, d), lambda i: (0, 0)),
        ],
        out_specs=pl.BlockSpec((tm, d), lambda i: (i, 0)),
        scratch_shapes=[pltpu.VMEM((tm, d), F32)],
        compiler_params=pltpu.CompilerParams(dimension_semantics=("arbitrary",),
                                             vmem_limit_bytes=VMEM_LIMIT),
        name="ffn_down",
    )(a, wd, hs, mod3, npost)


def _rope_tables(seq):
    f32 = np.float32
    rows = seq // GRID_W
    pos_r = np.repeat(np.arange(rows, dtype=f32), GRID_W)
    pos_c = np.tile(np.arange(GRID_W, dtype=f32), rows)
    n_f = RET_DK // 4
    inv = (f32(ROPE_BASE) ** (-np.arange(n_f, dtype=f32) / f32(n_f))).astype(f32)
    ang = np.concatenate([pos_r[:, None] * inv, pos_c[:, None] * inv], axis=-1).astype(f32)
    return jnp.asarray(np.cos(ang), F32), jnp.asarray(np.sin(ang), F32)


def _gate_map_layout(gla_a_up, gla_a_bias):
    r, h, dk = GLA_LOW_RANK, GLA_HEADS, GLA_DK
    u = jnp.zeros((GA_W, h, 2, dk), F32)
    u = u.at[:r, :, 0, :].set(gla_a_up[0].reshape(r, h, dk))
    u = u.at[r:, :, 1, :].set(gla_a_up[1].reshape(r, h, dk))
    u = u.reshape(GA_W, h * 2 * dk)
    up_stack = jnp.concatenate([u, u, u, jnp.zeros((LANE - 3 * GA_W, h * 2 * dk), F32)], axis=0)
    bias2 = jnp.stack([gla_a_bias[0].reshape(h, dk), gla_a_bias[1].reshape(h, dk)], axis=1)
    return up_stack, bias2.reshape(1, h * 2 * dk)


def _layer(h_state, c_rows, ctx2d, w_mod, b_mod, norm_mix_pre, norm_mix_post, norm_ffn_pre, norm_ffn_post,
           w_in, ret_decay, gla_a_up, gla_a_bias, ret_gn, gla_gn, w_up_ret, w_up_gla, w_out,
           ffn_w_gate, ffn_w_up, ffn_w_down, cos, sin, *, batch, seq, ctx_len):
    d = D_MODEL
    ga0 = _SRC["ga"][0]
    w_t = w_in.T
    w_g = w_t[ga0:ga0 + GA_W]
    w_ga = jnp.concatenate([w_g, w_g, w_g, jnp.zeros((LANE - 3 * GA_W, d), F32)], axis=0).astype(BF16)
    up_stack, bias2 = _gate_map_layout(gla_a_up, gla_a_bias)
    rd_b = jnp.broadcast_to(ret_decay.reshape(2, RET_HEADS, 1, 1), (2, RET_HEADS, 8, RET_DK))

    w_main, mod = _prologue(w_t, c_rows, w_mod, b_mod)
    mod3 = mod.reshape(mod.shape[0], 1, 6 * d)
    nw_pre = norm_mix_pre.reshape(1, d)

    tm = 1024
    p, ga, wur, wug = _inproj(h_state, nw_pre, mod3, lambda i: i // (seq // tm), w_main, w_ga, cos, sin,
                              tm=tm, tn=1024, ncols=N_MAIN, rope=True, seq_tiles=seq // tm,
                              casts=(w_up_ret, w_up_gla))
    pc, gac = _inproj(ctx2d, nw_pre, mod3, lambda i: batch, w_main, w_ga, cos, sin,
                      tm=batch * ctx_len, tn=512, ncols=N_CTX, rope=False, seq_tiles=1)

    o_r, wg, wo = _retention(p, pc, rd_b, ret_gn.reshape(1, RET_VW), (ffn_w_gate, w_out),
                             batch=batch, seq=seq, ctx_len=ctx_len)
    o_g, wu = _gla(p, ga, pc, gac, up_stack, bias2, gla_gn.reshape(1, GLA_VW), (ffn_w_up,),
                   batch=batch, seq=seq, ctx_len=ctx_len)

    hs, h2 = _post(o_r, o_g, p, h_state, wur, wug, wo,
                   mod3, norm_mix_post.reshape(1, d), norm_ffn_pre.reshape(1, d), tm=256, seq=seq)
    act, wd = _ffn_up(h2, wg, wu, ffn_w_down, tm=1024, tf=512)
    return _ffn_down(act, hs, wd, mod3, norm_ffn_post.reshape(1, d), tm=256, seq=seq)


def kernel(x, c, ctx, c_ctx, w_mod, b_mod, norm_mix_pre, norm_mix_post, norm_ffn_pre, norm_ffn_post,
           w_in, ret_decay, gla_a_up, gla_a_bias, ret_gn, gla_gn, w_up_ret, w_up_gla, w_out,
           ffn_w_gate, ffn_w_up, ffn_w_down):
    batch, seq, d = x.shape
    ctx_len = ctx.shape[1]
    depth = w_mod.shape[0]
    cos, sin = _rope_tables(seq)
    c_rows = jnp.zeros((16, d), F32).at[:batch].set(c.astype(F32)).at[batch].set(c_ctx.astype(F32))
    ctx2d = ctx.astype(F32).reshape(batch * ctx_len, d)
    h_state = x.astype(F32).reshape(batch * seq, d)
    for i in range(depth):
        h_state = _layer(h_state, c_rows, ctx2d, w_mod[i], b_mod[i], norm_mix_pre[i], norm_mix_post[i],
                         norm_ffn_pre[i], norm_ffn_post[i], w_in[i], ret_decay[i], gla_a_up[i], gla_a_bias[i],
                         ret_gn[i], gla_gn[i], w_up_ret[i], w_up_gla[i], w_out[i],
                         ffn_w_gate[i], ffn_w_up[i], ffn_w_down[i], cos, sin,
                         batch=batch, seq=seq, ctx_len=ctx_len)
    return h_state.reshape(batch, seq, d).astype(x.dtype)
```

```python
import functools

import jax
import jax.numpy as jnp
import numpy as np
from jax import lax
from jax.experimental import pallas as pl
from jax.experimental.pallas import tpu as pltpu

F32 = jnp.float32
BF16 = jnp.bfloat16

D_MODEL = 2048
GRID_W = 64
RET_HEADS = 4
RET_DK = 256
RET_DV = 256
GLA_HEADS = 4
GLA_DK = 128
GLA_DV = 256
GLA_LOW_RANK = 16
GLA_GATE_NORM = 16.0
ROPE_BASE = 10000.0
EPS = 1e-6

RET_QK = RET_HEADS * RET_DK
RET_VW = RET_HEADS * RET_DV
GLA_KW = GLA_HEADS * GLA_DK
GLA_VW = GLA_HEADS * GLA_DV
GA_W = 2 * GLA_LOW_RANK
LANE = 128

_SRC = {}
_off = 0
for _name, _w in (("rk", RET_QK), ("rv", RET_VW), ("gk", GLA_KW), ("gv", GLA_VW), ("ga", GA_W),
                  ("rq", RET_QK), ("rg", RET_VW), ("gq", GLA_KW), ("gg", GLA_VW),
                  ("gate_a", D_MODEL), ("gate_b", D_MODEL)):
    _SRC[_name] = (_off, _w)
    _off += _w

_ORDER = ("rk", "rv", "gv", "gk", "gq", "rq", "rg", "gg", "gate_a", "gate_b")
_DST = {}
_off = 0
for _name in _ORDER:
    _DST[_name] = _off
    _off += _SRC[_name][1]
N_MAIN = _off
N_CTX = _DST["gq"]
PACK_W = 512

VMEM_LIMIT = 56 * 1024 * 1024


def _dot(a, b):
    return jnp.dot(a, b, preferred_element_type=F32)


def _dot_nt(a, b):
    return lax.dot_general(a, b, (((1,), (1,)), ((), ())), preferred_element_type=F32)


def _dot_tn(a, b):
    return lax.dot_general(a, b, (((0,), (0,)), ((), ())), preferred_element_type=F32)


def _sigmoid(x):
    return 1.0 / (1.0 + jnp.exp(-x))


def _rms(x):
    return x * lax.rsqrt(jnp.mean(x * x, axis=-1, keepdims=True) + EPS)


def _run_staged(*stages):
    live = list(stages)
    while live:
        live = [g for g in live if next(g, live) is not live]


def _split_bf16(x):
    hi = x.astype(BF16)
    lo = (x - hi.astype(F32)).astype(BF16)
    return hi, lo


MOD_TN = 1024


def _prologue_kernel(a_idx_ref, n_idx_ref, shift_ref, a_ref, n_ref, c_ref, wm_ref, bm_ref,
                     o_ref, mod_ref, *, n_mod):
    j = pl.program_id(0)

    @pl.when(shift_ref[j] == 0)
    def _():
        o_ref[...] = a_ref[...].astype(o_ref.dtype)

    @pl.when(shift_ref[j] != 0)
    def _():
        o_ref[...] = jnp.concatenate([a_ref[GA_W:, :], n_ref[...]], axis=0).astype(o_ref.dtype)

    @pl.when(j < n_mod)
    def _():
        cf = c_ref[...]
        rows = cf.shape[0]
        s_hi, s_lo = _split_bf16(cf * _sigmoid(cf))
        w_hi, w_lo = _split_bf16(wm_ref[...])
        r = _dot(jnp.concatenate([s_hi, s_lo], axis=0), w_hi)
        mod_ref[...] = r[:rows] + r[rows:] + _dot(s_hi, w_lo) + bm_ref[...]


def _prologue(w_t, c_rows, w_mod, b_mod):
    d = w_t.shape[1]
    rows = c_rows.shape[0]
    n_modcols = w_mod.shape[1]
    n_mod = n_modcols // MOD_TN
    a_idx, n_idx, shift = [], [], []
    for name in _ORDER:
        src, width = _SRC[name]
        for c in range(src, src + width, PACK_W):
            base = c - c % PACK_W
            assert c - base in (0, GA_W)
            a_idx.append(base // PACK_W)
            n_idx.append((base + PACK_W) // GA_W if c != base else (n_idx[-1] if n_idx else 0))
            shift.append(c - base)
    n_tiles = len(a_idx)
    assert n_mod <= n_tiles
    as_i32 = lambda v: jnp.asarray(v, jnp.int32)
    mod_tile = lambda j, a, n, s: (0, jnp.minimum(j, n_mod - 1))
    return pl.pallas_call(
        functools.partial(_prologue_kernel, n_mod=n_mod),
        out_shape=(jax.ShapeDtypeStruct((n_tiles * PACK_W, d), BF16),
                   jax.ShapeDtypeStruct((rows, n_modcols), F32)),
        grid_spec=pltpu.PrefetchScalarGridSpec(
            num_scalar_prefetch=3,
            grid=(n_tiles,),
            in_specs=[pl.BlockSpec((PACK_W, d), lambda j, a, n, s: (a[j], 0)),
                      pl.BlockSpec((GA_W, d), lambda j, a, n, s: (n[j], 0)),
                      pl.BlockSpec((rows, d), lambda j, a, n, s: (0, 0)),
                      pl.BlockSpec((d, MOD_TN), mod_tile),
                      pl.BlockSpec((1, MOD_TN), mod_tile)],
            out_specs=(pl.BlockSpec((PACK_W, d), lambda j, a, n, s: (j, 0)),
                       pl.BlockSpec((rows, MOD_TN), mod_tile)),
        ),
        compiler_params=pltpu.CompilerParams(dimension_semantics=("arbitrary",),
                                             vmem_limit_bytes=VMEM_LIMIT),
        name="prologue",
    )(as_i32(a_idx), as_i32(n_idx), as_i32(shift), w_t, w_t, c_rows, w_mod, b_mod.reshape(1, n_modcols))


def _inproj_kernel(x_ref, nw_ref, sh_ref, sc_ref, w_ref, wga_ref, cos_ref, sin_ref, *rest,
                   rope_tiles, tn, n_cast):
    cast_in = rest[:n_cast]
    o_ref, ga_ref = rest[n_cast:n_cast + 2]
    cast_out = rest[n_cast + 2:2 * n_cast + 2]
    h_scr = rest[2 * n_cast + 2]
    j = pl.program_id(1)

    def ride_along_casts():
        for src, dst in zip(cast_in, cast_out):
            dst[...] = src[...].astype(dst.dtype)

    @pl.when(j == 0)
    def _():
        h = _rms(x_ref[...]) * (nw_ref[...] * (1.0 + sc_ref[...])) + sh_ref[...]
        hb = h.astype(BF16)
        h_scr[...] = hb
        g3 = _dot_nt(hb, wga_ref[...])
        lane = lax.broadcasted_iota(jnp.int32, g3.shape, 1)
        resid = g3 - g3.astype(BF16).astype(F32)
        ga_ref[...] = jnp.where((lane >= GA_W) & (lane < 2 * GA_W), resid, g3).astype(ga_ref.dtype)

    if not rope_tiles:
        ride_along_casts()
        o_ref[...] = _dot_nt(h_scr[...], w_ref[...]).astype(o_ref.dtype)
        return

    is_rope = functools.reduce(jnp.logical_or, [j == t for t in rope_tiles])

    @pl.when(is_rope)
    def _():
        ride_along_casts()
        cos = cos_ref[...]
        sin = sin_ref[...]
        half = RET_DK // 2
        for hd in range(0, tn, RET_DK):
            acc = _dot_nt(h_scr[...], w_ref[hd:hd + RET_DK, :])
            t1 = acc[:, :half]
            t2 = acc[:, half:]
            o_ref[:, hd:hd + half] = (t1 * cos - t2 * sin).astype(o_ref.dtype)
            o_ref[:, hd + half:hd + RET_DK] = (t1 * sin + t2 * cos).astype(o_ref.dtype)

    @pl.when(jnp.logical_not(is_rope))
    def _():
        ride_along_casts()
        o_ref[...] = _dot_nt(h_scr[...], w_ref[...]).astype(o_ref.dtype)


def _cast_plan(arr, n_i, n_j):
    rows, cols = arr.shape

    def pieces(extent, want, align):
        n = want
        while extent % n or (extent // n) % align:
            n -= 1
        return n

    if rows % n_j == 0 and (rows // n_j) % 16 == 0:
        nr, nc = n_j, pieces(cols, n_i, LANE)
        index = lambda i, j, nc=nc: (j, jnp.minimum(i, nc - 1))
    else:
        nr, nc = pieces(rows, n_i, 16), pieces(cols, n_j, LANE)
        index = lambda i, j, nr=nr, nc=nc: (jnp.minimum(i, nr - 1), jnp.minimum(j, nc - 1))
    return (rows // nr, cols // nc), index


def _inproj(x2d, norm_w, mod3, mod_row_of_tile, w_main, w_ga, cos, sin, *, tm, tn, ncols, rope, seq_tiles,
            casts=()):
    m, d = x2d.shape
    rope_tiles = ()
    if rope:
        assert tn == RET_QK and _DST["rk"] % tn == 0 and _DST["rq"] % tn == 0
        rope_tiles = (_DST["rk"] // tn, _DST["rq"] // tn)
    grid = (m // tm, ncols // tn)
    cast_specs = [pl.BlockSpec(*_cast_plan(a, *grid)) for a in casts]
    kern = functools.partial(_inproj_kernel, rope_tiles=rope_tiles, tn=tn, n_cast=len(casts))
    return pl.pallas_call(
        kern,
        out_shape=(jax.ShapeDtypeStruct((m, ncols), BF16),
                   jax.ShapeDtypeStruct((m, LANE), BF16),
                   *[jax.ShapeDtypeStruct(a.shape, BF16) for a in casts]),
        grid=grid,
        in_specs=[
            pl.BlockSpec((tm, d), lambda i, j: (i, 0)),
            pl.BlockSpec((1, d), lambda i, j: (0, 0)),
            pl.BlockSpec((None, 1, d), lambda i, j: (mod_row_of_tile(i), 0, 0)),
            pl.BlockSpec((None, 1, d), lambda i, j: (mod_row_of_tile(i), 0, 1)),
            pl.BlockSpec((tn, d), lambda i, j: (j, 0)),
            pl.BlockSpec((LANE, d), lambda i, j: (0, 0)),
            pl.BlockSpec((tm, RET_DK // 2), lambda i, j: (i % seq_tiles, 0)),
            pl.BlockSpec((tm, RET_DK // 2), lambda i, j: (i % seq_tiles, 0)),
            *cast_specs,
        ],
        out_specs=(pl.BlockSpec((tm, tn), lambda i, j: (i, j)),
                   pl.BlockSpec((tm, LANE), lambda i, j: (i, 0)),
                   *cast_specs),
        scratch_shapes=[pltpu.VMEM((tm, d), BF16)],
        compiler_params=pltpu.CompilerParams(dimension_semantics=("arbitrary", "arbitrary"),
                                             vmem_limit_bytes=VMEM_LIMIT),
        name="inproj",
    )(x2d, norm_w, mod3, mod3, w_main, w_ga, cos, sin, *casts)


RET_CHUNK = 256


def _ride_along_casts(rest, n_cast):
    for src, dst in zip(rest[:n_cast], rest[n_cast + 1:2 * n_cast + 1]):
        dst[...] = src[...].astype(dst.dtype)
    return rest[n_cast], rest[2 * n_cast + 1:]


def _ret_kernel(q_ref, k_ref, v_ref, g_ref, kc_ref, vc_ref, rd_ref, gn_ref, *rest, n_chunks, n_cast):
    o_ref, (o_scr, sc_scr, s_scr, dm_scr, eq_scr, wk_scr, sd_scr) = _ride_along_casts(rest, n_cast)
    c_len = RET_CHUNK
    ii = lax.broadcasted_iota(jnp.int32, (c_len, c_len), 0)
    jj = lax.broadcasted_iota(jnp.int32, (c_len, c_len), 1)
    rowi = lax.broadcasted_iota(jnp.int32, (c_len, RET_DK), 0).astype(F32)
    scale = RET_DK ** -0.5

    lg_f = -jnp.exp(rd_ref[0][0:1, :])
    lg_b = -jnp.exp(rd_ref[1][0:1, :])
    dm_scr[...] = jnp.where(jj <= ii, jnp.exp((ii - jj).astype(F32) * lg_f),
                            jnp.exp((jj - ii).astype(F32) * lg_b))
    for direction, lg in ((0, lg_f), (1, lg_b)):
        if direction == 0:
            eq_scr[direction] = jnp.exp((rowi + 1.0) * lg)
            wk = jnp.exp((c_len - 1.0 - rowi) * lg)
        else:
            eq_scr[direction] = jnp.exp((c_len - rowi) * lg)
            wk = jnp.exp(rowi * lg)
        wk_scr[direction] = wk
        sd_scr[direction] = jnp.broadcast_to(jnp.exp(float(c_len) * lg), (8, RET_DV))
        kcw = (kc_ref[...].astype(F32) * wk).astype(BF16)
        s_scr[direction] = _dot_tn(kcw, vc_ref[...])

    def rows_of(c):
        return pl.ds(pl.multiple_of(c * c_len, c_len), c_len)

    def score_stage(c):
        rows = rows_of(c)
        s_qk = _dot_nt(q_ref[rows, :], k_ref[rows, :])
        yield
        sc_scr[rows, :] = (s_qk * dm_scr[...]).astype(BF16)
        yield

    def value_stage(c):
        rows = rows_of(c)
        o_scr[rows, :] = _dot(sc_scr[rows, :], v_ref[rows, :])
        yield

    _run_staged(score_stage(0))

    def intra_body(t, carry):
        _run_staged(score_stage(t + 1), value_stage(t))
        return carry

    lax.fori_loop(0, n_chunks - 1, intra_body, 0, unroll=5)
    _run_staged(value_stage(n_chunks - 1))

    def scan_chunk(direction, c, done):
        rows = rows_of(c)
        q = q_ref[rows, :]
        k = k_ref[rows, :]
        v = v_ref[rows, :]
        s_state = s_scr[direction]
        kv = _dot_tn((k.astype(F32) * wk_scr[direction]).astype(BF16), v)
        o_state = _dot(q, s_state.astype(BF16))
        yield
        s_scr[direction] = sd_scr[direction][0:1, :] * s_state + kv
        tot = o_scr[rows, :] + eq_scr[direction] * o_state
        if not done:
            o_scr[rows, :] = tot
        else:
            mu = jnp.mean(tot, axis=-1, keepdims=True)
            cen = tot - mu
            var = jnp.mean(cen * cen, axis=-1, keepdims=True)
            nrm = cen * (scale * lax.rsqrt(var * (scale * scale) + EPS))
            g = g_ref[rows, :]
            o_ref[rows, :] = (nrm * gn_ref[...] * (g * _sigmoid(g)).astype(F32)).astype(o_ref.dtype)
        yield

    def body(t, carry, done):
        _run_staged(scan_chunk(0, t, done), scan_chunk(1, n_chunks - 1 - t, done))
        return carry

    lax.fori_loop(0, n_chunks // 2, functools.partial(body, done=False), 0, unroll=4)
    lax.fori_loop(n_chunks // 2, n_chunks, functools.partial(body, done=True), 0, unroll=4)


def _retention(p, pc, rd_b, ret_gn, casts, *, batch, seq, ctx_len):
    assert ctx_len == RET_CHUNK and (seq // RET_CHUNK) % 2 == 0
    n_chunks = seq // RET_CHUNK
    w = RET_DK
    kern = functools.partial(_ret_kernel, n_chunks=n_chunks, n_cast=len(casts))
    cast_specs = [pl.BlockSpec(*_cast_plan(a, batch, RET_HEADS)) for a in casts]

    def col(name):
        return _DST[name] // w

    return pl.pallas_call(
        kern,
        out_shape=(jax.ShapeDtypeStruct((batch * seq, RET_VW), BF16),
                   *[jax.ShapeDtypeStruct(a.shape, BF16) for a in casts]),
        grid=(batch, RET_HEADS),
        in_specs=[
            pl.BlockSpec((seq, w), lambda b, h: (b, col("rq") + h)),
            pl.BlockSpec((seq, w), lambda b, h: (b, col("rk") + h)),
            pl.BlockSpec((seq, w), lambda b, h: (b, col("rv") + h)),
            pl.BlockSpec((seq, w), lambda b, h: (b, col("rg") + h)),
            pl.BlockSpec((ctx_len, w), lambda b, h: (b, col("rk") + h)),
            pl.BlockSpec((ctx_len, w), lambda b, h: (b, col("rv") + h)),
            pl.BlockSpec((2, None, 8, w), lambda b, h: (0, h, 0, 0)),
            pl.BlockSpec((1, w), lambda b, h: (0, h)),
            *cast_specs,
        ],
        out_specs=(pl.BlockSpec((seq, w), lambda b, h: (b, h)), *cast_specs),
        scratch_shapes=[
            pltpu.VMEM((seq, RET_DV), F32),
            pltpu.VMEM((seq, RET_CHUNK), BF16),
            pltpu.VMEM((2, RET_DK, RET_DV), F32),
            pltpu.VMEM((RET_CHUNK, RET_CHUNK), F32),
            pltpu.VMEM((2, RET_CHUNK, RET_DV), F32),
            pltpu.VMEM((2, RET_CHUNK, RET_DK), F32),
            pltpu.VMEM((2, 8, RET_DV), F32),
        ],
        compiler_params=pltpu.CompilerParams(dimension_semantics=("arbitrary", "arbitrary"),
                                             vmem_limit_bytes=VMEM_LIMIT),
        name="retention",
    )(p, p, p, p, pc, pc, rd_b, ret_gn, *casts)


GLA_SUB = 64
GLA_BLK = 256
GLA_SCAN_BLK = 512
PREP_UNROLL = 7


def _gla_kernel(q_ref, k_ref, v_ref, g_ref, ga_ref, kc_ref, vc_ref, gac_ref, up_ref, bias_ref, gn_ref,
                *rest, n_blk, n_cast):
    o_ref, scratch = _ride_along_casts(rest, n_cast)
    (ahi_scr, alo_scr, qf_scr, qb_scr, ktf_scr, ktb_scr, khf_scr, khb_scr, khcf_scr, khcb_scr,
     dec_scr, decc_scr, o_scr, s_scr) = scratch
    blk, sub, dk = GLA_BLK, GLA_SUB, GLA_DK
    nsub = blk // sub
    shift = sub.bit_length() - 1
    scale = GLA_DK ** -0.5

    def masks(n):
        ii = lax.broadcasted_iota(jnp.int32, (n, n), 0)
        jj = lax.broadcasted_iota(jnp.int32, (n, n), 1)
        same = (ii >> shift) == (jj >> shift)
        lower = jj <= ii
        return same, lower, jnp.where(same & lower, 1.0, 0.0).astype(BF16)

    ctx_rows = gac_ref.shape[0]
    blk_masks = masks(blk)
    ctx_masks = blk_masks if ctx_rows == blk else masks(ctx_rows)
    r8 = lax.broadcasted_iota(jnp.int32, (8, dk), 0)

    u = up_ref[...]
    u_hi, u_lo = _split_bf16(u)
    urow = lax.broadcasted_iota(jnp.int32, u.shape, 0)
    rhs = jnp.where(urow >= 2 * GA_W, u_lo, u_hi)
    bias = bias_ref[...]

    def gate_stage(ga_blk, put):
        z = _dot(ga_blk, rhs) + bias
        yield
        a = (jnp.minimum(z, 0.0) - jnp.log(1.0 + jnp.exp(-jnp.abs(z)))) * (1.0 / GLA_GATE_NORM)
        put(*_split_bf16(a))
        yield

    def decay_stage(a_hi, a_lo, k_blk, q_blk, msk, store):
        same, lower, tmat = msk
        nsub = a_hi.shape[0] // sub
        assert nsub <= 8
        pre = _dot(tmat, a_hi) + _dot(tmat, a_lo)
        yield
        a = a_hi.astype(F32) + a_lo.astype(F32)
        lasts = [pre[i * sub + sub - 1:i * sub + sub, :] for i in range(nsub)]
        tot = jnp.concatenate([jnp.broadcast_to(l, (sub, 2 * dk)) for l in lasts], axis=0)
        b_f = pre[:, :dk]
        ex_b = pre[:, dk:] - a[:, dk:]
        b_b = tot[:, dk:] - ex_b
        kf = k_blk.astype(F32)
        khf = (kf * jnp.exp(tot[:, :dk] - b_f)).astype(BF16)
        khb = (kf * jnp.exp(ex_b)).astype(BF16)

        def tile8(half):
            rows = [jnp.broadcast_to(l[:, half * dk:(half + 1) * dk], (8, dk)) for l in lasts]
            t = rows[nsub - 1]
            for i in range(nsub - 2, -1, -1):
                t = jnp.where(r8 == i, rows[i], t)
            return t

        dec = jnp.exp(jnp.concatenate([tile8(0), tile8(1), jnp.zeros((LANE - 16, dk), F32)], axis=0))
        out = dict(khf=khf, khb=khb, dec_t=dec.T)
        if q_blk is not None:
            qf32 = q_blk.astype(F32)
            out.update(qf=(qf32 * jnp.exp(b_f)).astype(BF16), qb=(qf32 * jnp.exp(b_b)).astype(BF16),
                       ktf=(kf * jnp.exp(-b_f)).astype(BF16), ktb=(kf * jnp.exp(-b_b)).astype(BF16))
        store(**out)
        yield

    def score_stage(r):
        same, lower, _ = blk_masks
        rows = pl.ds(r, blk)
        s_f = _dot_nt(qf_scr[rows, :], ktf_scr[rows, :])
        s_b = _dot_nt(qb_scr[rows, :], ktb_scr[rows, :])
        yield
        s = jnp.where(same, jnp.where(lower, s_f, s_b), 0.0).astype(BF16)
        yield
        o_scr[rows, :] = _dot(s, v_ref[rows, :])
        yield

    run = _run_staged

    def store_ctx(khf, khb, dec_t):
        khcf_scr[...] = khf
        khcb_scr[...] = khb
        decc_scr[...] = dec_t

    ctx_gate = []
    run(gate_stage(gac_ref[...], lambda hi, lo: ctx_gate.extend((hi, lo))))
    run(decay_stage(*ctx_gate, kc_ref[...], None, ctx_masks, store_ctx))

    def gate_block(t):
        rows = pl.ds(pl.multiple_of(t * blk, blk), blk)

        def put(a_hi, a_lo):
            ahi_scr[rows, :] = a_hi
            alo_scr[rows, :] = a_lo

        return gate_stage(ga_ref[rows, :], put)

    def decay_block(t):
        rows = pl.ds(pl.multiple_of(t * blk, blk), blk)

        def store(khf, khb, dec_t, qf, qb, ktf, ktb):
            khf_scr[rows, :] = khf
            khb_scr[rows, :] = khb
            qf_scr[rows, :] = qf
            qb_scr[rows, :] = qb
            ktf_scr[rows, :] = ktf
            ktb_scr[rows, :] = ktb
            dec_scr[t] = dec_t

        return decay_stage(ahi_scr[rows, :], alo_scr[rows, :], k_ref[rows, :], q_ref[rows, :],
                           blk_masks, store)

    def score_block(t):
        return score_stage(pl.multiple_of(t * blk, blk))

    run(gate_block(0))
    run(gate_block(1), decay_block(0))

    def prep_body(t, carry):
        run(gate_block(t + 2), decay_block(t + 1), score_block(t))
        return carry

    lax.fori_loop(0, n_blk - 2, prep_body, 0, unroll=PREP_UNROLL)
    run(decay_block(n_blk - 1), score_block(n_blk - 2))
    run(score_block(n_blk - 1))

    def scan_blk(direction, n_sub, dec_ts, kh_at, v_at, q_at=None, o_rows=None):
        s = s_scr[direction]
        order = range(n_sub) if direction == 0 else range(n_sub - 1, -1, -1)
        kv = {i: _dot_tn(kh_at(i), v_at(i)) for i in order}
        yield
        for i in order:
            c = direction * 8 + i % nsub
            col = dec_ts[i // nsub][:, c:c + 1]
            if q_at is not None:
                rows = o_rows(i)
                o_scr[rows, :] += _dot(q_at(i), s.astype(BF16))
            s = s * col + kv[i]
            yield
        s_scr[direction] = s

    s_scr[...] = jnp.zeros(s_scr.shape, F32)
    run(*[scan_blk(direction, ctx_rows // sub, [decc_scr[...]],
                   lambda i, khc=khc: khc[i * sub:(i + 1) * sub, :],
                   lambda i: vc_ref[i * sub:(i + 1) * sub, :])
          for direction, khc in ((0, khcf_scr), (1, khcb_scr))])

    sblk = GLA_SCAN_BLK
    per = sblk // blk
    n_sblk = n_blk // per

    fin_rows = 128

    def finalize(t):
        for tb in (t, n_sblk - 1 - t):
            for r0 in range(0, sblk, fin_rows):
                rows = pl.ds(pl.multiple_of(tb * sblk + r0, fin_rows), fin_rows)
                o = o_scr[rows, :]
                ms = jnp.mean(o * o, axis=-1, keepdims=True)
                nrm = o * (scale * lax.rsqrt(ms * (scale * scale) + EPS))
                g = g_ref[rows, :]
                o_ref[rows, :] = (nrm * gn_ref[...] * (g * _sigmoid(g)).astype(F32)).astype(o_ref.dtype)
                yield

    def scans(t):
        for direction, q_scr, kh_scr in ((0, qf_scr, khf_scr), (1, qb_scr, khb_scr)):
            tb = t if direction == 0 else n_sblk - 1 - t

            def rows(i, tb=tb):
                return pl.ds(pl.multiple_of(tb * sblk + i * sub, sub), sub)

            yield from scan_blk(direction, sblk // sub, [dec_scr[tb * per + j] for j in range(per)],
                                lambda i, kh_scr=kh_scr, rows=rows: kh_scr[rows(i), :],
                                lambda i, rows=rows: v_ref[rows(i), :],
                                lambda i, q_scr=q_scr, rows=rows: q_scr[rows(i), :],
                                rows)

    def scan_body(t, carry, fin_prev):
        if fin_prev:
            run(scans(t), finalize(t - 1))
        else:
            run(scans(t))
        return carry

    half = n_sblk // 2
    lax.fori_loop(0, half + 1, functools.partial(scan_body, fin_prev=False), 0, unroll=half + 1)
    lax.fori_loop(half + 1, n_sblk, functools.partial(scan_body, fin_prev=True), 0, unroll=n_sblk - half - 1)
    run(finalize(n_sblk - 1))


def _gla(p, ga, pc, gac, up_stack, bias2, gla_gn, casts, *, batch, seq, ctx_len):
    assert ctx_len % GLA_SUB == 0 and ctx_len <= GLA_BLK and GLA_SCAN_BLK % GLA_BLK == 0
    assert (seq // GLA_SCAN_BLK) % 2 == 0
    n_blk = seq // GLA_BLK
    kern = functools.partial(_gla_kernel, n_blk=n_blk, n_cast=len(casts))
    dk, dv = GLA_DK, GLA_DV
    cast_specs = [pl.BlockSpec(*_cast_plan(a, batch, GLA_HEADS)) for a in casts]
    return pl.pallas_call(
        kern,
        out_shape=(jax.ShapeDtypeStruct((batch * seq, GLA_VW), BF16),
                   *[jax.ShapeDtypeStruct(a.shape, BF16) for a in casts]),
        grid=(batch, GLA_HEADS),
        in_specs=[
            pl.BlockSpec((seq, dk), lambda b, h: (b, _DST["gq"] // dk + h)),
            pl.BlockSpec((seq, dk), lambda b, h: (b, _DST["gk"] // dk + h)),
            pl.BlockSpec((seq, dv), lambda b, h: (b, _DST["gv"] // dv + h)),
            pl.BlockSpec((seq, dv), lambda b, h: (b, _DST["gg"] // dv + h)),
            pl.BlockSpec((seq, LANE), lambda b, h: (b, 0)),
            pl.BlockSpec((ctx_len, dk), lambda b, h: (b, _DST["gk"] // dk + h)),
            pl.BlockSpec((ctx_len, dv), lambda b, h: (b, _DST["gv"] // dv + h)),
            pl.BlockSpec((ctx_len, LANE), lambda b, h: (b, 0)),
            pl.BlockSpec((LANE, 2 * dk), lambda b, h: (0, h)),
            pl.BlockSpec((1, 2 * dk), lambda b, h: (0, h)),
            pl.BlockSpec((1, dv), lambda b, h: (0, h)),
            *cast_specs,
        ],
        out_specs=(pl.BlockSpec((seq, dv), lambda b, h: (b, h)), *cast_specs),
        scratch_shapes=[
            pltpu.VMEM((seq, 2 * dk), BF16), pltpu.VMEM((seq, 2 * dk), BF16),
            pltpu.VMEM((seq, dk), BF16), pltpu.VMEM((seq, dk), BF16),
            pltpu.VMEM((seq, dk), BF16), pltpu.VMEM((seq, dk), BF16),
            pltpu.VMEM((seq, dk), BF16), pltpu.VMEM((seq, dk), BF16),
            pltpu.VMEM((ctx_len, dk), BF16), pltpu.VMEM((ctx_len, dk), BF16),
            pltpu.VMEM((n_blk, dk, LANE), F32), pltpu.VMEM((dk, LANE), F32),
            pltpu.VMEM((seq, dv), F32),
            pltpu.VMEM((2, dk, dv), F32),
        ],
        compiler_params=pltpu.CompilerParams(dimension_semantics=("arbitrary", "arbitrary"),
                                             vmem_limit_bytes=VMEM_LIMIT),
        name="gla",
    )(p, p, p, p, ga, pc, pc, gac, up_stack, bias2, gla_gn, *casts)


MXU_N = 256


POST_ROWS = 128


def _post_kernel(or_ref, og_ref, ga0_ref, ga1_ref, gb0_ref, gb1_ref, x_ref, wur_ref, wug_ref, wo_ref,
                 g1_ref, sh2_ref, sc2_ref, npost_ref, npre_ref, hs_ref, h2_ref):
    half = ga0_ref.shape[1]
    tm = hs_ref.shape[0]
    groups = [slice(r0, r0 + POST_ROWS) for r0 in range(0, tm, POST_ROWS)]

    def up(rows):
        return _dot(or_ref[rows, :], wur_ref[...]), _dot(og_ref[rows, :], wug_ref[...])

    def out(rows, y_ret, y_gla):
        def merge(ga_r, gb_r, lo):
            return (_sigmoid(ga_r[rows, :].astype(F32)) * y_ret[:, lo:lo + half]
                    + _sigmoid(gb_r[rows, :].astype(F32)) * y_gla[:, lo:lo + half]).astype(BF16)

        merged = jnp.concatenate([merge(ga0_ref, gb0_ref, 0), merge(ga1_ref, gb1_ref, half)], axis=1)
        return _dot(merged, wo_ref[...])

    post_gain = g1_ref[...] * npost_ref[...]
    pre_gain = npre_ref[...] * (1.0 + sc2_ref[...])

    def finish(rows, y):
        hs = x_ref[rows, :] + _rms(y) * post_gain
        hs_ref[rows, :] = hs
        h2 = _rms(hs) * pre_gain + sh2_ref[...]
        h2_ref[rows, :] = h2.astype(h2_ref.dtype)

    ups = [up(rows) for rows in groups]
    ys = [out(rows, *u) for rows, u in zip(groups, ups)]
    for rows, y in zip(groups, ys):
        finish(rows, y)


def _post(o_r, o_g, p, x2d, w_up_ret, w_up_gla, w_out, mod3, npost, npre, *, tm, seq):
    m, d = x2d.shape
    tiles_per_seq = seq // tm
    const = dict(pipeline_mode=pl.Buffered(1))
    gw = d // 2

    def modspec(chunk):
        return pl.BlockSpec((None, 1, d), lambda i: (i // tiles_per_seq, 0, chunk))

    def gatespec(name, part):
        return pl.BlockSpec((tm, gw), lambda i: (i, _DST[name] // gw + part))

    return pl.pallas_call(
        _post_kernel,
        out_shape=(jax.ShapeDtypeStruct((m, d), F32), jax.ShapeDtypeStruct((m, d), BF16)),
        grid=(m // tm,),
        in_specs=[
            pl.BlockSpec((tm, RET_VW), lambda i: (i, 0)),
            pl.BlockSpec((tm, GLA_VW), lambda i: (i, 0)),
            gatespec("gate_a", 0), gatespec("gate_a", 1), gatespec("gate_b", 0), gatespec("gate_b", 1),
            pl.BlockSpec((tm, d), lambda i: (i, 0)),
            pl.BlockSpec((RET_VW, d), lambda i: (0, 0), **const),
            pl.BlockSpec((GLA_VW, d), lambda i: (0, 0), **const),
            pl.BlockSpec((d, d), lambda i: (0, 0), **const),
            modspec(2), modspec(3), modspec(4),
            pl.BlockSpec((1, d), lambda i: (0, 0)),
            pl.BlockSpec((1, d), lambda i: (0, 0)),
        ],
        out_specs=(pl.BlockSpec((tm, d), lambda i: (i, 0)),
                   pl.BlockSpec((tm, d), lambda i: (i, 0))),
        compiler_params=pltpu.CompilerParams(dimension_semantics=("arbitrary",),
                                             vmem_limit_bytes=VMEM_LIMIT),
        name="post",
    )(o_r, o_g, p, p, p, p, x2d, w_up_ret, w_up_gla, w_out, mod3, mod3, mod3, npost, npre)


def _ffn_up_kernel(h_ref, wg_ref, wu_ref, wsrc_ref, a_ref, wdst_ref):
    wdst_ref[...] = wsrc_ref[...].astype(wdst_ref.dtype)
    h = h_ref[...]
    for lo in range(0, a_ref.shape[1], MXU_N):
        g = _dot(h, wg_ref[:, lo:lo + MXU_N])
        u = _dot(h, wu_ref[:, lo:lo + MXU_N])
        a_ref[:, lo:lo + MXU_N] = (g * _sigmoid(g) * u).astype(a_ref.dtype)


def _ffn_up(h2, wg, wu, w_later, *, tm, tf):
    m, d = h2.shape
    d_ff = wg.shape[1]
    grid = (m // tm, d_ff // tf)
    cast_spec = pl.BlockSpec(*_cast_plan(w_later, *grid))
    return pl.pallas_call(
        _ffn_up_kernel,
        out_shape=(jax.ShapeDtypeStruct((m, d_ff), BF16), jax.ShapeDtypeStruct(w_later.shape, BF16)),
        grid=grid,
        in_specs=[
            pl.BlockSpec((tm, d), lambda i, f: (i, 0)),
            pl.BlockSpec((d, tf), lambda i, f: (0, f)),
            pl.BlockSpec((d, tf), lambda i, f: (0, f)),
            cast_spec,
        ],
        out_specs=(pl.BlockSpec((tm, tf), lambda i, f: (i, f)), cast_spec),
        compiler_params=pltpu.CompilerParams(dimension_semantics=("arbitrary", "arbitrary"),
                                             vmem_limit_bytes=VMEM_LIMIT),
        name="ffn_up",
    )(h2, wg, wu, w_later)


def _ffn_down_kernel(a_ref, wd_ref, hs_ref, g2_ref, nw_ref, o_ref, y_scr, *, n_split):
    tm, d = o_ref.shape
    w = d // n_split
    ss = jnp.zeros((tm, 1), F32)
    for lo in range(0, d, w):
        y = _dot(a_ref[...], wd_ref[:, lo:lo + w])
        y_scr[:, lo:lo + w] = y
        ss = ss + jnp.sum(y * y, axis=-1, keepdims=True)
    inv = lax.rsqrt(ss * (1.0 / d) + EPS)
    o_ref[...] = hs_ref[...] + y_scr[...] * inv * (g2_ref[...] * nw_ref[...])


def _ffn_down(a, hs, wd, mod3, npost, *, tm, seq):
    m, d = hs.shape
    d_ff = wd.shape[0]
    tiles_per_seq = seq // tm
    kern = functools.partial(_ffn_down_kernel, n_split=4)
    return pl.pallas_call(
        kern,
        out_shape=jax.ShapeDtypeStruct((m, d), F32),
        grid=(m // tm,),
        in_specs=[
            pl.BlockSpec((tm, d_ff), lambda i: (i, 0)),
            pl.BlockSpec((d_ff, d), lambda i: (0, 0), pipeline_mode=pl.Buffered(1)),
            pl.BlockSpec((tm, d), lambda i: (i, 0)),
            pl.BlockSpec((None, 1, d), lambda i: (i // tiles_per_seq, 0, 5)),
            pl.BlockSpec((1, d), lambda i: (0, 0)),
        ],
        out_specs=pl.BlockSpec((tm, d), lambda i: (i, 0)),
        scratch_shapes=[pltpu.VMEM((tm, d), F32)],
        compiler_params=pltpu.CompilerParams(dimension_semantics=("arbitrary",),
                                             vmem_limit_bytes=VMEM_LIMIT),
        name="ffn_down",
    )(a, wd, hs, mod3, npost)


def _rope_tables(seq):
    f32 = np.float32
    rows = seq // GRID_W
    pos_r = np.repeat(np.arange(rows, dtype=f32), GRID_W)
    pos_c = np.tile(np.arange(GRID_W, dtype=f32), rows)
    n_f = RET_DK // 4
    inv = (f32(ROPE_BASE) ** (-np.arange(n_f, dtype=f32) / f32(n_f))).astype(f32)
    ang = np.concatenate([pos_r[:, None] * inv, pos_c[:, None] * inv], axis=-1).astype(f32)
    return jnp.asarray(np.cos(ang), F32), jnp.asarray(np.sin(ang), F32)


def _gate_map_layout(gla_a_up, gla_a_bias):
    r, h, dk = GLA_LOW_RANK, GLA_HEADS, GLA_DK
    u = jnp.zeros((GA_W, h, 2, dk), F32)
    u = u.at[:r, :, 0, :].set(gla_a_up[0].reshape(r, h, dk))
    u = u.at[r:, :, 1, :].set(gla_a_up[1].reshape(r, h, dk))
    u = u.reshape(GA_W, h * 2 * dk)
    up_stack = jnp.concatenate([u, u, u, jnp.zeros((LANE - 3 * GA_W, h * 2 * dk), F32)], axis=0)
    bias2 = jnp.stack([gla_a_bias[0].reshape(h, dk), gla_a_bias[1].reshape(h, dk)], axis=1)
    return up_stack, bias2.reshape(1, h * 2 * dk)


def _layer(h_state, c_rows, ctx2d, w_mod, b_mod, norm_mix_pre, norm_mix_post, norm_ffn_pre, norm_ffn_post,
           w_in, ret_decay, gla_a_up, gla_a_bias, ret_gn, gla_gn, w_up_ret, w_up_gla, w_out,
           ffn_w_gate, ffn_w_up, ffn_w_down, cos, sin, *, batch, seq, ctx_len):
    d = D_MODEL
    ga0 = _SRC["ga"][0]
    w_t = w_in.T
    w_g = w_t[ga0:ga0 + GA_W]
    w_ga = jnp.concatenate([w_g, w_g, w_g, jnp.zeros((LANE - 3 * GA_W, d), F32)], axis=0).astype(BF16)
    up_stack, bias2 = _gate_map_layout(gla_a_up, gla_a_bias)
    rd_b = jnp.broadcast_to(ret_decay.reshape(2, RET_HEADS, 1, 1), (2, RET_HEADS, 8, RET_DK))

    w_main, mod = _prologue(w_t, c_rows, w_mod, b_mod)
    mod3 = mod.reshape(mod.shape[0], 1, 6 * d)
    nw_pre = norm_mix_pre.reshape(1, d)

    tm = 1024
    p, ga, wur, wug = _inproj(h_state, nw_pre, mod3, lambda i: i // (seq // tm), w_main, w_ga, cos, sin,
                              tm=tm, tn=1024, ncols=N_MAIN, rope=True, seq_tiles=seq // tm,
                              casts=(w_up_ret, w_up_gla))
    pc, gac = _inproj(ctx2d, nw_pre, mod3, lambda i: batch, w_main, w_ga, cos, sin,
                      tm=batch * ctx_len, tn=512, ncols=N_CTX, rope=False, seq_tiles=1)

    o_r, wg, wo = _retention(p, pc, rd_b, ret_gn.reshape(1, RET_VW), (ffn_w_gate, w_out),
                             batch=batch, seq=seq, ctx_len=ctx_len)
    o_g, wu = _gla(p, ga, pc, gac, up_stack, bias2, gla_gn.reshape(1, GLA_VW), (ffn_w_up,),
                   batch=batch, seq=seq, ctx_len=ctx_len)

    hs, h2 = _post(o_r, o_g, p, h_state, wur, wug, wo,
                   mod3, norm_mix_post.reshape(1, d), norm_ffn_pre.reshape(1, d), tm=256, seq=seq)
    act, wd = _ffn_up(h2, wg, wu, ffn_w_down, tm=1024, tf=512)
    return _ffn_down(act, hs, wd, mod3, norm_ffn_post.reshape(1, d), tm=256, seq=seq)


def kernel(x, c, ctx, c_ctx, w_mod, b_mod, norm_mix_pre, norm_mix_post, norm_ffn_pre, norm_ffn_post,
           w_in, ret_decay, gla_a_up, gla_a_bias, ret_gn, gla_gn, w_up_ret, w_up_gla, w_out,
           ffn_w_gate, ffn_w_up, ffn_w_down):
    batch, seq, d = x.shape
    ctx_len = ctx.shape[1]
    depth = w_mod.shape[0]
    cos, sin = _rope_tables(seq)
    c_rows = jnp.zeros((16, d), F32).at[:batch].set(c.astype(F32)).at[batch].set(c_ctx.astype(F32))
    ctx2d = ctx.astype(F32).reshape(batch * ctx_len, d)
    h_state = x.astype(F32).reshape(batch * seq, d)
    for i in range(depth):
        h_state = _layer(h_state, c_rows, ctx2d, w_mod[i], b_mod[i], norm_mix_pre[i], norm_mix_post[i],
                         norm_ffn_pre[i], norm_ffn_post[i], w_in[i], ret_decay[i], gla_a_up[i], gla_a_bias[i],
                         ret_gn[i], gla_gn[i], w_up_ret[i], w_up_gla[i], w_out[i],
                         ffn_w_gate[i], ffn_w_up[i], ffn_w_down[i], cos, sin,
                         batch=batch, seq=seq, ctx_len=ctx_len)
    return h_state.reshape(batch, seq, d).astype(x.dtype)
```

```python
import functools

import jax
import jax.numpy as jnp
import numpy as np
from jax import lax
from jax.experimental import pallas as pl
from jax.experimental.pallas import tpu as pltpu

F32 = jnp.float32
BF16 = jnp.bfloat16

D_MODEL = 2048
GRID_W = 64
RET_HEADS = 4
RET_DK = 256
RET_DV = 256
GLA_HEADS = 4
GLA_DK = 128
GLA_DV = 256
GLA_LOW_RANK = 16
GLA_GATE_NORM = 16.0
ROPE_BASE = 10000.0
EPS = 1e-6

RET_QK = RET_HEADS * RET_DK
RET_VW = RET_HEADS * RET_DV
GLA_KW = GLA_HEADS * GLA_DK
GLA_VW = GLA_HEADS * GLA_DV
GA_W = 2 * GLA_LOW_RANK
LANE = 128

_SRC = {}
_off = 0
for _name, _w in (("rk", RET_QK), ("rv", RET_VW), ("gk", GLA_KW), ("gv", GLA_VW), ("ga", GA_W),
                  ("rq", RET_QK), ("rg", RET_VW), ("gq", GLA_KW), ("gg", GLA_VW),
                  ("gate_a", D_MODEL), ("gate_b", D_MODEL)):
    _SRC[_name] = (_off, _w)
    _off += _w

_ORDER = ("rk", "rv", "gv", "gk", "gq", "rq", "rg", "gg", "gate_a", "gate_b")
_DST = {}
_off = 0
for _name in _ORDER:
    _DST[_name] = _off
    _off += _SRC[_name][1]
N_MAIN = _off
N_CTX = _DST["gq"]
PACK_W = 512

VMEM_LIMIT = 56 * 1024 * 1024


def _dot(a, b):
    return jnp.dot(a, b, preferred_element_type=F32)


def _dot_nt(a, b):
    return lax.dot_general(a, b, (((1,), (1,)), ((), ())), preferred_element_type=F32)


def _dot_tn(a, b):
    return lax.dot_general(a, b, (((0,), (0,)), ((), ())), preferred_element_type=F32)


def _sigmoid(x):
    return 1.0 / (1.0 + jnp.exp(-x))


def _rms(x):
    return x * lax.rsqrt(jnp.mean(x * x, axis=-1, keepdims=True) + EPS)


def _run_staged(*stages):
    live = list(stages)
    while live:
        live = [g for g in live if next(g, live) is not live]


def _split_bf16(x):
    hi = x.astype(BF16)
    lo = (x - hi.astype(F32)).astype(BF16)
    return hi, lo


MOD_TN = 1024


def _prologue_kernel(a_idx_ref, n_idx_ref, shift_ref, a_ref, n_ref, c_ref, wm_ref, bm_ref,
                     o_ref, mod_ref, *, n_mod):
    j = pl.program_id(0)

    @pl.when(shift_ref[j] == 0)
    def _():
        o_ref[...] = a_ref[...].astype(o_ref.dtype)

    @pl.when(shift_ref[j] != 0)
    def _():
        o_ref[...] = jnp.concatenate([a_ref[GA_W:, :], n_ref[...]], axis=0).astype(o_ref.dtype)

    @pl.when(j < n_mod)
    def _():
        cf = c_ref[...]
        rows = cf.shape[0]
        s_hi, s_lo = _split_bf16(cf * _sigmoid(cf))
        w_hi, w_lo = _split_bf16(wm_ref[...])
        r = _dot(jnp.concatenate([s_hi, s_lo], axis=0), w_hi)
        mod_ref[...] = r[:rows] + r[rows:] + _dot(s_hi, w_lo) + bm_ref[...]


def _prologue(w_t, c_rows, w_mod, b_mod):
    d = w_t.shape[1]
    rows = c_rows.shape[0]
    n_modcols = w_mod.shape[1]
    n_mod = n_modcols // MOD_TN
    a_idx, n_idx, shift = [], [], []
    for name in _ORDER:
        src, width = _SRC[name]
        for c in range(src, src + width, PACK_W):
            base = c - c % PACK_W
            assert c - base in (0, GA_W)
            a_idx.append(base // PACK_W)
            n_idx.append((base + PACK_W) // GA_W if c != base else (n_idx[-1] if n_idx else 0))
            shift.append(c - base)
    n_tiles = len(a_idx)
    assert n_mod <= n_tiles
    as_i32 = lambda v: jnp.asarray(v, jnp.int32)
    mod_tile = lambda j, a, n, s: (0, jnp.minimum(j, n_mod - 1))
    return pl.pallas_call(
        functools.partial(_prologue_kernel, n_mod=n_mod),
        out_shape=(jax.ShapeDtypeStruct((n_tiles * PACK_W, d), BF16),
                   jax.ShapeDtypeStruct((rows, n_modcols), F32)),
        grid_spec=pltpu.PrefetchScalarGridSpec(
            num_scalar_prefetch=3,
            grid=(n_tiles,),
            in_specs=[pl.BlockSpec((PACK_W, d), lambda j, a, n, s: (a[j], 0)),
                      pl.BlockSpec((GA_W, d), lambda j, a, n, s: (n[j], 0)),
                      pl.BlockSpec((rows, d), lambda j, a, n, s: (0, 0)),
                      pl.BlockSpec((d, MOD_TN), mod_tile),
                      pl.BlockSpec((1, MOD_TN), mod_tile)],
            out_specs=(pl.BlockSpec((PACK_W, d), lambda j, a, n, s: (j, 0)),
                       pl.BlockSpec((rows, MOD_TN), mod_tile)),
        ),
        compiler_params=pltpu.CompilerParams(dimension_semantics=("arbitrary",),
                                             vmem_limit_bytes=VMEM_LIMIT),
        name="prologue",
    )(as_i32(a_idx), as_i32(n_idx), as_i32(shift), w_t, w_t, c_rows, w_mod, b_mod.reshape(1, n_modcols))


def _inproj_kernel(x_ref, nw_ref, sh_ref, sc_ref, w_ref, wga_ref, cos_ref, sin_ref, *rest,
                   rope_tiles, tn, n_cast):
    cast_in = rest[:n_cast]
    o_ref, ga_ref = rest[n_cast:n_cast + 2]
    cast_out = rest[n_cast + 2:2 * n_cast + 2]
    h_scr = rest[2 * n_cast + 2]
    j = pl.program_id(1)

    def ride_along_casts():
        for src, dst in zip(cast_in, cast_out):
            dst[...] = src[...].astype(dst.dtype)

    @pl.when(j == 0)
    def _():
        h = _rms(x_ref[...]) * (nw_ref[...] * (1.0 + sc_ref[...])) + sh_ref[...]
        hb = h.astype(BF16)
        h_scr[...] = hb
        g3 = _dot_nt(hb, wga_ref[...])
        lane = lax.broadcasted_iota(jnp.int32, g3.shape, 1)
        resid = g3 - g3.astype(BF16).astype(F32)
        ga_ref[...] = jnp.where((lane >= GA_W) & (lane < 2 * GA_W), resid, g3).astype(ga_ref.dtype)

    if not rope_tiles:
        ride_along_casts()
        o_ref[...] = _dot_nt(h_scr[...], w_ref[...]).astype(o_ref.dtype)
        return

    is_rope = functools.reduce(jnp.logical_or, [j == t for t in rope_tiles])

    @pl.when(is_rope)
    def _():
        ride_along_casts()
        cos = cos_ref[...]
        sin = sin_ref[...]
        half = RET_DK // 2
        for hd in range(0, tn, RET_DK):
            acc = _dot_nt(h_scr[...], w_ref[hd:hd + RET_DK, :])
            t1 = acc[:, :half]
            t2 = acc[:, half:]
            o_ref[:, hd:hd + half] = (t1 * cos - t2 * sin).astype(o_ref.dtype)
            o_ref[:, hd + half:hd + RET_DK] = (t1 * sin + t2 * cos).astype(o_ref.dtype)

    @pl.when(jnp.logical_not(is_rope))
    def _():
        ride_along_casts()
        o_ref[...] = _dot_nt(h_scr[...], w_ref[...]).astype(o_ref.dtype)


def _cast_plan(arr, n_i, n_j):
    rows, cols = arr.shape

    def pieces(extent, want, align):
        n = want
        while extent % n or (extent // n) % align:
            n -= 1
        return n

    if rows % n_j == 0 and (rows // n_j) % 16 == 0:
        nr, nc = n_j, pieces(cols, n_i, LANE)
        index = lambda i, j, nc=nc: (j, jnp.minimum(i, nc - 1))
    else:
        nr, nc = pieces(rows, n_i, 16), pieces(cols, n_j, LANE)
        index = lambda i, j, nr=nr, nc=nc: (jnp.minimum(i, nr - 1), jnp.minimum(j, nc - 1))
    return (rows // nr, cols // nc), index


def _inproj(x2d, norm_w, mod3, mod_row_of_tile, w_main, w_ga, cos, sin, *, tm, tn, ncols, rope, seq_tiles,
            casts=()):
    m, d = x2d.shape
    rope_tiles = ()
    if rope:
        assert tn == RET_QK and _DST["rk"] % tn == 0 and _DST["rq"] % tn == 0
        rope_tiles = (_DST["rk"] // tn, _DST["rq"] // tn)
    grid = (m // tm, ncols // tn)
    cast_specs = [pl.BlockSpec(*_cast_plan(a, *grid)) for a in casts]
    kern = functools.partial(_inproj_kernel, rope_tiles=rope_tiles, tn=tn, n_cast=len(casts))
    return pl.pallas_call(
        kern,
        out_shape=(jax.ShapeDtypeStruct((m, ncols), BF16),
                   jax.ShapeDtypeStruct((m, LANE), BF16),
                   *[jax.ShapeDtypeStruct(a.shape, BF16) for a in casts]),
        grid=grid,
        in_specs=[
            pl.BlockSpec((tm, d), lambda i, j: (i, 0)),
            pl.BlockSpec((1, d), lambda i, j: (0, 0)),
            pl.BlockSpec((None, 1, d), lambda i, j: (mod_row_of_tile(i), 0, 0)),
            pl.BlockSpec((None, 1, d), lambda i, j: (mod_row_of_tile(i), 0, 1)),
            pl.BlockSpec((tn, d), lambda i, j: (j, 0)),
            pl.BlockSpec((LANE, d), lambda i, j: (0, 0)),
            pl.BlockSpec((tm, RET_DK // 2), lambda i, j: (i % seq_tiles, 0)),
            pl.BlockSpec((tm, RET_DK // 2), lambda i, j: (i % seq_tiles, 0)),
            *cast_specs,
        ],
        out_specs=(pl.BlockSpec((tm, tn), lambda i, j: (i, j)),
                   pl.BlockSpec((tm, LANE), lambda i, j: (i, 0)),
                   *cast_specs),
        scratch_shapes=[pltpu.VMEM((tm, d), BF16)],
        compiler_params=pltpu.CompilerParams(dimension_semantics=("arbitrary", "arbitrary"),
                                             vmem_limit_bytes=VMEM_LIMIT),
        name="inproj",
    )(x2d, norm_w, mod3, mod3, w_main, w_ga, cos, sin, *casts)


RET_CHUNK = 256


def _ride_along_casts(rest, n_cast):
    for src, dst in zip(rest[:n_cast], rest[n_cast + 1:2 * n_cast + 1]):
        dst[...] = src[...].astype(dst.dtype)
    return rest[n_cast], rest[2 * n_cast + 1:]


def _ret_kernel(q_ref, k_ref, v_ref, g_ref, kc_ref, vc_ref, rd_ref, gn_ref, *rest, n_chunks, n_cast):
    o_ref, (o_scr, sc_scr, s_scr, dm_scr, eq_scr, wk_scr, sd_scr) = _ride_along_casts(rest, n_cast)
    c_len = RET_CHUNK
    ii = lax.broadcasted_iota(jnp.int32, (c_len, c_len), 0)
    jj = lax.broadcasted_iota(jnp.int32, (c_len, c_len), 1)
    rowi = lax.broadcasted_iota(jnp.int32, (c_len, RET_DK), 0).astype(F32)
    scale = RET_DK ** -0.5

    lg_f = -jnp.exp(rd_ref[0][0:1, :])
    lg_b = -jnp.exp(rd_ref[1][0:1, :])
    dm_scr[...] = jnp.where(jj <= ii, jnp.exp((ii - jj).astype(F32) * lg_f),
                            jnp.exp((jj - ii).astype(F32) * lg_b))
    for direction, lg in ((0, lg_f), (1, lg_b)):
        if direction == 0:
            eq_scr[direction] = jnp.exp((rowi + 1.0) * lg)
            wk = jnp.exp((c_len - 1.0 - rowi) * lg)
        else:
            eq_scr[direction] = jnp.exp((c_len - rowi) * lg)
            wk = jnp.exp(rowi * lg)
        wk_scr[direction] = wk
        sd_scr[direction] = jnp.broadcast_to(jnp.exp(float(c_len) * lg), (8, RET_DV))
        kcw = (kc_ref[...].astype(F32) * wk).astype(BF16)
        s_scr[direction] = _dot_tn(kcw, vc_ref[...])

    def rows_of(c):
        return pl.ds(pl.multiple_of(c * c_len, c_len), c_len)

    def score_stage(c):
        rows = rows_of(c)
        s_qk = _dot_nt(q_ref[rows, :], k_ref[rows, :])
        yield
        sc_scr[rows, :] = (s_qk * dm_scr[...]).astype(BF16)
        yield

    def value_stage(c):
        rows = rows_of(c)
        o_scr[rows, :] = _dot(sc_scr[rows, :], v_ref[rows, :])
        yield

    _run_staged(score_stage(0))

    def intra_body(t, carry):
        _run_staged(score_stage(t + 1), value_stage(t))
        return carry

    lax.fori_loop(0, n_chunks - 1, intra_body, 0, unroll=5)
    _run_staged(value_stage(n_chunks - 1))

    def scan_chunk(direction, c, done):
        rows = rows_of(c)
        q = q_ref[rows, :]
        k = k_ref[rows, :]
        v = v_ref[rows, :]
        s_state = s_scr[direction]
        kv = _dot_tn((k.astype(F32) * wk_scr[direction]).astype(BF16), v)
        o_state = _dot(q, s_state.astype(BF16))
        yield
        s_scr[direction] = sd_scr[direction][0:1, :] * s_state + kv
        tot = o_scr[rows, :] + eq_scr[direction] * o_state
        if not done:
            o_scr[rows, :] = tot
        else:
            mu = jnp.mean(tot, axis=-1, keepdims=True)
            cen = tot - mu
            var = jnp.mean(cen * cen, axis=-1, keepdims=True)
            nrm = cen * (scale * lax.rsqrt(var * (scale * scale) + EPS))
            g = g_ref[rows, :]
            o_ref[rows, :] = (nrm * gn_ref[...] * (g * _sigmoid(g)).astype(F32)).astype(o_ref.dtype)
        yield

    def body(t, carry, done):
        _run_staged(scan_chunk(0, t, done), scan_chunk(1, n_chunks - 1 - t, done))
        return carry

    lax.fori_loop(0, n_chunks // 2, functools.partial(body, done=False), 0, unroll=4)
    lax.fori_loop(n_chunks // 2, n_chunks, functools.partial(body, done=True), 0, unroll=4)


def _retention(p, pc, rd_b, ret_gn, casts, *, batch, seq, ctx_len):
    assert ctx_len == RET_CHUNK and (seq // RET_CHUNK) % 2 == 0
    n_chunks = seq // RET_CHUNK
    w = RET_DK
    kern = functools.partial(_ret_kernel, n_chunks=n_chunks, n_cast=len(casts))
    cast_specs = [pl.BlockSpec(*_cast_plan(a, batch, RET_HEADS)) for a in casts]

    def col(name):
        return _DST[name] // w

    return pl.pallas_call(
        kern,
        out_shape=(jax.ShapeDtypeStruct((batch * seq, RET_VW), BF16),
                   *[jax.ShapeDtypeStruct(a.shape, BF16) for a in casts]),
        grid=(batch, RET_HEADS),
        in_specs=[
            pl.BlockSpec((seq, w), lambda b, h: (b, col("rq") + h)),
            pl.BlockSpec((seq, w), lambda b, h: (b, col("rk") + h)),
            pl.BlockSpec((seq, w), lambda b, h: (b, col("rv") + h)),
            pl.BlockSpec((seq, w), lambda b, h: (b, col("rg") + h)),
            pl.BlockSpec((ctx_len, w), lambda b, h: (b, col("rk") + h)),
            pl.BlockSpec((ctx_len, w), lambda b, h: (b, col("rv") + h)),
            pl.BlockSpec((2, None, 8, w), lambda b, h: (0, h, 0, 0)),
            pl.BlockSpec((1, w), lambda b, h: (0, h)),
            *cast_specs,
        ],
        out_specs=(pl.BlockSpec((seq, w), lambda b, h: (b, h)), *cast_specs),
        scratch_shapes=[
            pltpu.VMEM((seq, RET_DV), F32),
            pltpu.VMEM((seq, RET_CHUNK), BF16),
            pltpu.VMEM((2, RET_DK, RET_DV), F32),
            pltpu.VMEM((RET_CHUNK, RET_CHUNK), F32),
            pltpu.VMEM((2, RET_CHUNK, RET_DV), F32),
            pltpu.VMEM((2, RET_CHUNK, RET_DK), F32),
            pltpu.VMEM((2, 8, RET_DV), F32),
        ],
        compiler_params=pltpu.CompilerParams(dimension_semantics=("arbitrary", "arbitrary"),
                                             vmem_limit_bytes=VMEM_LIMIT),
        name="retention",
    )(p, p, p, p, pc, pc, rd_b, ret_gn, *casts)


GLA_SUB = 64
GLA_BLK = 256
GLA_SCAN_BLK = 512
PREP_UNROLL = 7


def _gla_kernel(q_ref, k_ref, v_ref, g_ref, ga_ref, kc_ref, vc_ref, gac_ref, up_ref, bias_ref, gn_ref,
                *rest, n_blk, n_cast):
    o_ref, scratch = _ride_along_casts(rest, n_cast)
    (ahi_scr, alo_scr, qf_scr, qb_scr, ktf_scr, ktb_scr, khf_scr, khb_scr, khcf_scr, khcb_scr,
     dec_scr, decc_scr, o_scr, s_scr) = scratch
    blk, sub, dk = GLA_BLK, GLA_SUB, GLA_DK
    nsub = blk // sub
    shift = sub.bit_length() - 1
    scale = GLA_DK ** -0.5

    def masks(n):
        ii = lax.broadcasted_iota(jnp.int32, (n, n), 0)
        jj = lax.broadcasted_iota(jnp.int32, (n, n), 1)
        same = (ii >> shift) == (jj >> shift)
        lower = jj <= ii
        return same, lower, jnp.where(same & lower, 1.0, 0.0).astype(BF16)

    ctx_rows = gac_ref.shape[0]
    blk_masks = masks(blk)
    ctx_masks = blk_masks if ctx_rows == blk else masks(ctx_rows)
    r8 = lax.broadcasted_iota(jnp.int32, (8, dk), 0)

    u = up_ref[...]
    u_hi, u_lo = _split_bf16(u)
    urow = lax.broadcasted_iota(jnp.int32, u.shape, 0)
    rhs = jnp.where(urow >= 2 * GA_W, u_lo, u_hi)
    bias = bias_ref[...]

    def gate_stage(ga_blk, put):
        z = _dot(ga_blk, rhs) + bias
        yield
        a = (jnp.minimum(z, 0.0) - jnp.log(1.0 + jnp.exp(-jnp.abs(z)))) * (1.0 / GLA_GATE_NORM)
        put(*_split_bf16(a))
        yield

    def decay_stage(a_hi, a_lo, k_blk, q_blk, msk, store):
        same, lower, tmat = msk
        nsub = a_hi.shape[0] // sub
        assert nsub <= 8
        pre = _dot(tmat, a_hi) + _dot(tmat, a_lo)
        yield
        a = a_hi.astype(F32) + a_lo.astype(F32)
        lasts = [pre[i * sub + sub - 1:i * sub + sub, :] for i in range(nsub)]
        tot = jnp.concatenate([jnp.broadcast_to(l, (sub, 2 * dk)) for l in lasts], axis=0)
        b_f = pre[:, :dk]
        ex_b = pre[:, dk:] - a[:, dk:]
        b_b = tot[:, dk:] - ex_b
        kf = k_blk.astype(F32)
        khf = (kf * jnp.exp(tot[:, :dk] - b_f)).astype(BF16)
        khb = (kf * jnp.exp(ex_b)).astype(BF16)

        def tile8(half):
            rows = [jnp.broadcast_to(l[:, half * dk:(half + 1) * dk], (8, dk)) for l in lasts]
            t = rows[nsub - 1]
            for i in range(nsub - 2, -1, -1):
                t = jnp.where(r8 == i, rows[i], t)
            return t

        dec = jnp.exp(jnp.concatenate([tile8(0), tile8(1), jnp.zeros((LANE - 16, dk), F32)], axis=0))
        out = dict(khf=khf, khb=khb, dec_t=dec.T)
        if q_blk is not None:
            qf32 = q_blk.astype(F32)
            out.update(qf=(qf32 * jnp.exp(b_f)).astype(BF16), qb=(qf32 * jnp.exp(b_b)).astype(BF16),
                       ktf=(kf * jnp.exp(-b_f)).astype(BF16), ktb=(kf * jnp.exp(-b_b)).astype(BF16))
        store(**out)
        yield

    def score_stage(r):
        same, lower, _ = blk_masks
        rows = pl.ds(r, blk)
        s_f = _dot_nt(qf_scr[rows, :], ktf_scr[rows, :])
        s_b = _dot_nt(qb_scr[rows, :], ktb_scr[rows, :])
        yield
        s = jnp.where(same, jnp.where(lower, s_f, s_b), 0.0).astype(BF16)
        yield
        o_scr[rows, :] = _dot(s, v_ref[rows, :])
        yield

    run = _run_staged

    def store_ctx(khf, khb, dec_t):
        khcf_scr[...] = khf
        khcb_scr[...] = khb
        decc_scr[...] = dec_t

    ctx_gate = []
    run(gate_stage(gac_ref[...], lambda hi, lo: ctx_gate.extend((hi, lo))))
    run(decay_stage(*ctx_gate, kc_ref[...], None, ctx_masks, store_ctx))

    def gate_block(t):
        rows = pl.ds(pl.multiple_of(t * blk, blk), blk)

        def put(a_hi, a_lo):
            ahi_scr[rows, :] = a_hi
            alo_scr[rows, :] = a_lo

        return gate_stage(ga_ref[rows, :], put)

    def decay_block(t):
        rows = pl.ds(pl.multiple_of(t * blk, blk), blk)

        def store(khf, khb, dec_t, qf, qb, ktf, ktb):
            khf_scr[rows, :] = khf
            khb_scr[rows, :] = khb
            qf_scr[rows, :] = qf
            qb_scr[rows, :] = qb
            ktf_scr[rows, :] = ktf
            ktb_scr[rows, :] = ktb
            dec_scr[t] = dec_t

        return decay_stage(ahi_scr[rows, :], alo_scr[rows, :], k_ref[rows, :], q_ref[rows, :],
                           blk_masks, store)

    def score_block(t):
        return score_stage(pl.multiple_of(t * blk, blk))

    run(gate_block(0))
    run(gate_block(1), decay_block(0))

    def prep_body(t, carry):
        run(gate_block(t + 2), decay_block(t + 1), score_block(t))
        return carry

    lax.fori_loop(0, n_blk - 2, prep_body, 0, unroll=PREP_UNROLL)
    run(decay_block(n_blk - 1), score_block(n_blk - 2))
    run(score_block(n_blk - 1))

    def scan_blk(direction, n_sub, dec_ts, kh_at, v_at, q_at=None, o_rows=None):
        s = s_scr[direction]
        order = range(n_sub) if direction == 0 else range(n_sub - 1, -1, -1)
        kv = {i: _dot_tn(kh_at(i), v_at(i)) for i in order}
        yield
        for i in order:
            c = direction * 8 + i % nsub
            col = dec_ts[i // nsub][:, c:c + 1]
            if q_at is not None:
                rows = o_rows(i)
                o_scr[rows, :] += _dot(q_at(i), s.astype(BF16))
            s = s * col + kv[i]
            yield
        s_scr[direction] = s

    s_scr[...] = jnp.zeros(s_scr.shape, F32)
    run(*[scan_blk(direction, ctx_rows // sub, [decc_scr[...]],
                   lambda i, khc=khc: khc[i * sub:(i + 1) * sub, :],
                   lambda i: vc_ref[i * sub:(i + 1) * sub, :])
          for direction, khc in ((0, khcf_scr), (1, khcb_scr))])

    sblk = GLA_SCAN_BLK
    per = sblk // blk
    n_sblk = n_blk // per

    fin_rows = 128

    def finalize(t):
        for tb in (t, n_sblk - 1 - t):
            for r0 in range(0, sblk, fin_rows):
                rows = pl.ds(pl.multiple_of(tb * sblk + r0, fin_rows), fin_rows)
                o = o_scr[rows, :]
                ms = jnp.mean(o * o, axis=-1, keepdims=True)
                nrm = o * (scale * lax.rsqrt(ms * (scale * scale) + EPS))
                g = g_ref[rows, :]
                o_ref[rows, :] = (nrm * gn_ref[...] * (g * _sigmoid(g)).astype(F32)).astype(o_ref.dtype)
                yield

    def scans(t):
        for direction, q_scr, kh_scr in ((0, qf_scr, khf_scr), (1, qb_scr, khb_scr)):
            tb = t if direction == 0 else n_sblk - 1 - t

            def rows(i, tb=tb):
                return pl.ds(pl.multiple_of(tb * sblk + i * sub, sub), sub)

            yield from scan_blk(direction, sblk // sub, [dec_scr[tb * per + j] for j in range(per)],
                                lambda i, kh_scr=kh_scr, rows=rows: kh_scr[rows(i), :],
                                lambda i, rows=rows: v_ref[rows(i), :],
                                lambda i, q_scr=q_scr, rows=rows: q_scr[rows(i), :],
                                rows)

    def scan_body(t, carry, fin_prev):
        if fin_prev:
            run(scans(t), finalize(t - 1))
        else:
            run(scans(t))
        return carry

    half = n_sblk // 2
    lax.fori_loop(0, half + 1, functools.partial(scan_body, fin_prev=False), 0, unroll=half + 1)
    lax.fori_loop(half + 1, n_sblk, functools.partial(scan_body, fin_prev=True), 0, unroll=n_sblk - half - 1)
    run(finalize(n_sblk - 1))


def _gla(p, ga, pc, gac, up_stack, bias2, gla_gn, casts, *, batch, seq, ctx_len):
    assert ctx_len % GLA_SUB == 0 and ctx_len <= GLA_BLK and GLA_SCAN_BLK % GLA_BLK == 0
    assert (seq // GLA_SCAN_BLK) % 2 == 0
    n_blk = seq // GLA_BLK
    kern = functools.partial(_gla_kernel, n_blk=n_blk, n_cast=len(casts))
    dk, dv = GLA_DK, GLA_DV
    cast_specs = [pl.BlockSpec(*_cast_plan(a, batch, GLA_HEADS)) for a in casts]
    return pl.pallas_call(
        kern,
        out_shape=(jax.ShapeDtypeStruct((batch * seq, GLA_VW), BF16),
                   *[jax.ShapeDtypeStruct(a.shape, BF16) for a in casts]),
        grid=(batch, GLA_HEADS),
        in_specs=[
            pl.BlockSpec((seq, dk), lambda b, h: (b, _DST["gq"] // dk + h)),
            pl.BlockSpec((seq, dk), lambda b, h: (b, _DST["gk"] // dk + h)),
            pl.BlockSpec((seq, dv), lambda b, h: (b, _DST["gv"] // dv + h)),
            pl.BlockSpec((seq, dv), lambda b, h: (b, _DST["gg"] // dv + h)),
            pl.BlockSpec((seq, LANE), lambda b, h: (b, 0)),
            pl.BlockSpec((ctx_len, dk), lambda b, h: (b, _DST["gk"] // dk + h)),
            pl.BlockSpec((ctx_len, dv), lambda b, h: (b, _DST["gv"] // dv + h)),
            pl.BlockSpec((ctx_len, LANE), lambda b, h: (b, 0)),
            pl.BlockSpec((LANE, 2 * dk), lambda b, h: (0, h)),
            pl.BlockSpec((1, 2 * dk), lambda b, h: (0, h)),
            pl.BlockSpec((1, dv), lambda b, h: (0, h)),
            *cast_specs,
        ],
        out_specs=(pl.BlockSpec((seq, dv), lambda b, h: (b, h)), *cast_specs),
        scratch_shapes=[
            pltpu.VMEM((seq, 2 * dk), BF16), pltpu.VMEM((seq, 2 * dk), BF16),
            pltpu.VMEM((seq, dk), BF16), pltpu.VMEM((seq, dk), BF16),
            pltpu.VMEM((seq, dk), BF16), pltpu.VMEM((seq, dk), BF16),
            pltpu.VMEM((seq, dk), BF16), pltpu.VMEM((seq, dk), BF16),
            pltpu.VMEM((ctx_len, dk), BF16), pltpu.VMEM((ctx_len, dk), BF16),
            pltpu.VMEM((n_blk, dk, LANE), F32), pltpu.VMEM((dk, LANE), F32),
            pltpu.VMEM((seq, dv), F32),
            pltpu.VMEM((2, dk, dv), F32),
        ],
        compiler_params=pltpu.CompilerParams(dimension_semantics=("arbitrary", "arbitrary"),
                                             vmem_limit_bytes=VMEM_LIMIT),
        name="gla",
    )(p, p, p, p, ga, pc, pc, gac, up_stack, bias2, gla_gn, *casts)


MXU_N = 256


POST_ROWS = 128


def _post_kernel(or_ref, og_ref, ga0_ref, ga1_ref, gb0_ref, gb1_ref, x_ref, wur_ref, wug_ref, wo_ref,
                 g1_ref, sh2_ref, sc2_ref, npost_ref, npre_ref, hs_ref, h2_ref):
    half = ga0_ref.shape[1]
    tm = hs_ref.shape[0]
    groups = [slice(r0, r0 + POST_ROWS) for r0 in range(0, tm, POST_ROWS)]

    def up(rows):
        return _dot(or_ref[rows, :], wur_ref[...]), _dot(og_ref[rows, :], wug_ref[...])

    def out(rows, y_ret, y_gla):
        def merge(ga_r, gb_r, lo):
            return (_sigmoid(ga_r[rows, :].astype(F32)) * y_ret[:, lo:lo + half]
                    + _sigmoid(gb_r[rows, :].astype(F32)) * y_gla[:, lo:lo + half]).astype(BF16)

        merged = jnp.concatenate([merge(ga0_ref, gb0_ref, 0), merge(ga1_ref, gb1_ref, half)], axis=1)
        return _dot(merged, wo_ref[...])

    post_gain = g1_ref[...] * npost_ref[...]
    pre_gain = npre_ref[...] * (1.0 + sc2_ref[...])

    def finish(rows, y):
        hs = x_ref[rows, :] + _rms(y) * post_gain
        hs_ref[rows, :] = hs
        h2 = _rms(hs) * pre_gain + sh2_ref[...]
        h2_ref[rows, :] = h2.astype(h2_ref.dtype)

    ups = [up(rows) for rows in groups]
    ys = [out(rows, *u) for rows, u in zip(groups, ups)]
    for rows, y in zip(groups, ys):
        finish(rows, y)


def _post(o_r, o_g, p, x2d, w_up_ret, w_up_gla, w_out, mod3, npost, npre, *, tm, seq):
    m, d = x2d.shape
    tiles_per_seq = seq // tm
    const = dict(pipeline_mode=pl.Buffered(1))
    gw = d // 2

    def modspec(chunk):
        return pl.BlockSpec((None, 1, d), lambda i: (i // tiles_per_seq, 0, chunk))

    def gatespec(name, part):
        return pl.BlockSpec((tm, gw), lambda i: (i, _DST[name] // gw + part))

    return pl.pallas_call(
        _post_kernel,
        out_shape=(jax.ShapeDtypeStruct((m, d), F32), jax.ShapeDtypeStruct((m, d), BF16)),
        grid=(m // tm,),
        in_specs=[
            pl.BlockSpec((tm, RET_VW), lambda i: (i, 0)),
            pl.BlockSpec((tm, GLA_VW), lambda i: (i, 0)),
            gatespec("gate_a", 0), gatespec("gate_a", 1), gatespec("gate_b", 0), gatespec("gate_b", 1),
            pl.BlockSpec((tm, d), lambda i: (i, 0)),
            pl.BlockSpec((RET_VW, d), lambda i: (0, 0), **const),
            pl.BlockSpec((GLA_VW, d), lambda i: (0, 0), **const),
            pl.BlockSpec((d, d), lambda i: (0, 0), **const),
            modspec(2), modspec(3), modspec(4),
            pl.BlockSpec((1, d), lambda i: (0, 0)),
            pl.BlockSpec((1, d), lambda i: (0, 0)),
        ],
        out_specs=(pl.BlockSpec((tm, d), lambda i: (i, 0)),
                   pl.BlockSpec((tm, d), lambda i: (i, 0))),
        compiler_params=pltpu.CompilerParams(dimension_semantics=("arbitrary",),
                                             vmem_limit_bytes=VMEM_LIMIT),
        name="post",
    )(o_r, o_g, p, p, p, p, x2d, w_up_ret, w_up_gla, w_out, mod3, mod3, mod3, npost, npre)


def _ffn_up_kernel(h_ref, wg_ref, wu_ref, wsrc_ref, a_ref, wdst_ref):
    wdst_ref[...] = wsrc_ref[...].astype(wdst_ref.dtype)
    h = h_ref[...]
    for lo in range(0, a_ref.shape[1], MXU_N):
        g = _dot(h, wg_ref[:, lo:lo + MXU_N])
        u = _dot(h, wu_ref[:, lo:lo + MXU_N])
        a_ref[:, lo:lo + MXU_N] = (g * _sigmoid(g) * u).astype(a_ref.dtype)


def _ffn_up(h2, wg, wu, w_later, *, tm, tf):
    m, d = h2.shape
    d_ff = wg.shape[1]
    grid = (m // tm, d_ff // tf)
    cast_spec = pl.BlockSpec(*_cast_plan(w_later, *grid))
    return pl.pallas_call(
        _ffn_up_kernel,
        out_shape=(jax.ShapeDtypeStruct((m, d_ff), BF16), jax.ShapeDtypeStruct(w_later.shape, BF16)),
        grid=grid,
        in_specs=[
            pl.BlockSpec((tm, d), lambda i, f: (i, 0)),
            pl.BlockSpec((d, tf), lambda i, f: (0, f)),
            pl.BlockSpec((d, tf), lambda i, f: (0, f)),
            cast_spec,
        ],
        out_specs=(pl.BlockSpec((tm, tf), lambda i, f: (i, f)), cast_spec),
        compiler_params=pltpu.CompilerParams(dimension_semantics=("arbitrary", "arbitrary"),
                                             vmem_limit_bytes=VMEM_LIMIT),
        name="ffn_up",
    )(h2, wg, wu, w_later)


def _ffn_down_kernel(a_ref, wd_ref, hs_ref, g2_ref, nw_ref, o_ref, y_scr, *, n_split):
    tm, d = o_ref.shape
    w = d // n_split
    ss = jnp.zeros((tm, 1), F32)
    for lo in range(0, d, w):
        y = _dot(a_ref[...], wd_ref[:, lo:lo + w])
        y_scr[:, lo:lo + w] = y
        ss = ss + jnp.sum(y * y, axis=-1, keepdims=True)
    inv = lax.rsqrt(ss * (1.0 / d) + EPS)
    o_ref[...] = hs_ref[...] + y_scr[...] * inv * (g2_ref[...] * nw_ref[...])


def _ffn_down(a, hs, wd, mod3, npost, *, tm, seq):
    m, d = hs.shape
    d_ff = wd.shape[0]
    tiles_per_seq = seq // tm
    kern = functools.partial(_ffn_down_kernel, n_split=4)
    return pl.pallas_call(
        kern,
        out_shape=jax.ShapeDtypeStruct((m, d), F32),
        grid=(m // tm,),
        in_specs=[
            pl.BlockSpec((tm, d_ff), lambda i: (i, 0)),
            pl.BlockSpec((d_ff, d), lambda i: (0, 0), pipeline_mode=pl.Buffered(1)),
            pl.BlockSpec((tm, d), lambda i: (i, 0)),
            pl.BlockSpec((None, 1, d), lambda i: (i // tiles_per_seq, 0, 5)),
            pl.BlockSpec((1, d), lambda i: (0, 0)),
        ],
        out_specs=pl.BlockSpec((tm, d), lambda i: (i, 0)),
        scratch_shapes=[pltpu.VMEM((tm, d), F32)],
        compiler_params=pltpu.CompilerParams(dimension_semantics=("arbitrary",),
                                             vmem_limit_bytes=VMEM_LIMIT),
        name="ffn_down",
    )(a, wd, hs, mod3, npost)


def _rope_tables(seq):
    f32 = np.float32
    rows = seq // GRID_W
    pos_r = np.repeat(np.arange(rows, dtype=f32), GRID_W)
    pos_c = np.tile(np.arange(GRID_W, dtype=f32), rows)
    n_f = RET_DK // 4
    inv = (f32(ROPE_BASE) ** (-np.arange(n_f, dtype=f32) / f32(n_f))).astype(f32)
    ang = np.concatenate([pos_r[:, None] * inv, pos_c[:, None] * inv], axis=-1).astype(f32)
    return jnp.asarray(np.cos(ang), F32), jnp.asarray(np.sin(ang), F32)


def _gate_map_layout(gla_a_up, gla_a_bias):
    r, h, dk = GLA_LOW_RANK, GLA_HEADS, GLA_DK
    u = jnp.zeros((GA_W, h, 2, dk), F32)
    u = u.at[:r, :, 0, :].set(gla_a_up[0].reshape(r, h, dk))
    u = u.at[r:, :, 1, :].set(gla_a_up[1].reshape(r, h, dk))
    u = u.reshape(GA_W, h * 2 * dk)
    up_stack = jnp.concatenate([u, u, u, jnp.zeros((LANE - 3 * GA_W, h * 2 * dk), F32)], axis=0)
    bias2 = jnp.stack([gla_a_bias[0].reshape(h, dk), gla_a_bias[1].reshape(h, dk)], axis=1)
    return up_stack, bias2.reshape(1, h * 2 * dk)


def _layer(h_state, c_rows, ctx2d, w_mod, b_mod, norm_mix_pre, norm_mix_post, norm_ffn_pre, norm_ffn_post,
           w_in, ret_decay, gla_a_up, gla_a_bias, ret_gn, gla_gn, w_up_ret, w_up_gla, w_out,
           ffn_w_gate, ffn_w_up, ffn_w_down, cos, sin, *, batch, seq, ctx_len):
    d = D_MODEL
    ga0 = _SRC["ga"][0]
    w_t = w_in.T
    w_g = w_t[ga0:ga0 + GA_W]
    w_ga = jnp.concatenate([w_g, w_g, w_g, jnp.zeros((LANE - 3 * GA_W, d), F32)], axis=0).astype(BF16)
    up_stack, bias2 = _gate_map_layout(gla_a_up, gla_a_bias)
    rd_b = jnp.broadcast_to(ret_decay.reshape(2, RET_HEADS, 1, 1), (2, RET_HEADS, 8, RET_DK))

    w_main, mod = _prologue(w_t, c_rows, w_mod, b_mod)
    mod3 = mod.reshape(mod.shape[0], 1, 6 * d)
    nw_pre = norm_mix_pre.reshape(1, d)

    tm = 1024
    p, ga, wur, wug, wo, wg = _inproj(
        h_state, nw_pre, mod3, lambda i: i // (seq // tm), w_main, w_ga, cos, sin,
        tm=tm, tn=1024, ncols=N_MAIN, rope=True, seq_tiles=seq // tm,
        casts=(w_up_ret, w_up_gla, w_out, ffn_w_gate))
    pc, gac = _inproj(ctx2d, nw_pre, mod3, lambda i: batch, w_main, w_ga, cos, sin,
                      tm=batch * ctx_len, tn=512, ncols=N_CTX, rope=False, seq_tiles=1)

    (o_r,) = _retention(p, pc, rd_b, ret_gn.reshape(1, RET_VW), (), batch=batch, seq=seq, ctx_len=ctx_len)
    o_g, wu = _gla(p, ga, pc, gac, up_stack, bias2, gla_gn.reshape(1, GLA_VW), (ffn_w_up,),
                   batch=batch, seq=seq, ctx_len=ctx_len)

    hs, h2 = _post(o_r, o_g, p, h_state, wur, wug, wo,
                   mod3, norm_mix_post.reshape(1, d), norm_ffn_pre.reshape(1, d), tm=256, seq=seq)
    act, wd = _ffn_up(h2, wg, wu, ffn_w_down, tm=1024, tf=512)
    return _ffn_down(act, hs, wd, mod3, norm_ffn_post.reshape(1, d), tm=256, seq=seq)


def kernel(x, c, ctx, c_ctx, w_mod, b_mod, norm_mix_pre, norm_mix_post, norm_ffn_pre, norm_ffn_post,
           w_in, ret_decay, gla_a_up, gla_a_bias, ret_gn, gla_gn, w_up_ret, w_up_gla, w_out,
           ffn_w_gate, ffn_w_up, ffn_w_down):
    batch, seq, d = x.shape
    ctx_len = ctx.shape[1]
    depth = w_mod.shape[0]
    cos, sin = _rope_tables(seq)
    c_rows = jnp.zeros((16, d), F32).at[:batch].set(c.astype(F32)).at[batch].set(c_ctx.astype(F32))
    ctx2d = ctx.astype(F32).reshape(batch * ctx_len, d)
    h_state = x.astype(F32).reshape(batch * seq, d)
    for i in range(depth):
        h_state = _layer(h_state, c_rows, ctx2d, w_mod[i], b_mod[i], norm_mix_pre[i], norm_mix_post[i],
                         norm_ffn_pre[i], norm_ffn_post[i], w_in[i], ret_decay[i], gla_a_up[i], gla_a_bias[i],
                         ret_gn[i], gla_gn[i], w_up_ret[i], w_up_gla[i], w_out[i],
                         ffn_w_gate[i], ffn_w_up[i], ffn_w_down[i], cos, sin,
                         batch=batch, seq=seq, ctx_len=ctx_len)
    return h_state.reshape(batch, seq, d).astype(x.dtype)
```

```python
import functools

import jax
import jax.numpy as jnp
import numpy as np
from jax import lax
from jax.experimental import pallas as pl
from jax.experimental.pallas import tpu as pltpu

F32 = jnp.float32
BF16 = jnp.bfloat16

D_MODEL = 2048
GRID_W = 64
RET_HEADS = 4
RET_DK = 256
RET_DV = 256
GLA_HEADS = 4
GLA_DK = 128
GLA_DV = 256
GLA_LOW_RANK = 16
GLA_GATE_NORM = 16.0
ROPE_BASE = 10000.0
EPS = 1e-6

RET_QK = RET_HEADS * RET_DK
RET_VW = RET_HEADS * RET_DV
GLA_KW = GLA_HEADS * GLA_DK
GLA_VW = GLA_HEADS * GLA_DV
GA_W = 2 * GLA_LOW_RANK
LANE = 128

_SRC = {}
_off = 0
for _name, _w in (("rk", RET_QK), ("rv", RET_VW), ("gk", GLA_KW), ("gv", GLA_VW), ("ga", GA_W),
                  ("rq", RET_QK), ("rg", RET_VW), ("gq", GLA_KW), ("gg", GLA_VW),
                  ("gate_a", D_MODEL), ("gate_b", D_MODEL)):
    _SRC[_name] = (_off, _w)
    _off += _w

_ORDER = ("rk", "rv", "gv", "gk", "gq", "rq", "rg", "gg", "gate_a", "gate_b")
_DST = {}
_off = 0
for _name in _ORDER:
    _DST[_name] = _off
    _off += _SRC[_name][1]
N_MAIN = _off
N_CTX = _DST["gq"]
PACK_W = 512

VMEM_LIMIT = 56 * 1024 * 1024


def _dot(a, b):
    return jnp.dot(a, b, preferred_element_type=F32)


def _dot_nt(a, b):
    return lax.dot_general(a, b, (((1,), (1,)), ((), ())), preferred_element_type=F32)


def _dot_tn(a, b):
    return lax.dot_general(a, b, (((0,), (0,)), ((), ())), preferred_element_type=F32)


def _sigmoid(x):
    return 1.0 / (1.0 + jnp.exp(-x))


def _rms(x):
    return x * lax.rsqrt(jnp.mean(x * x, axis=-1, keepdims=True) + EPS)


def _run_staged(*stages):
    live = list(stages)
    while live:
        live = [g for g in live if next(g, live) is not live]


def _split_bf16(x):
    hi = x.astype(BF16)
    lo = (x - hi.astype(F32)).astype(BF16)
    return hi, lo


MOD_TN = 1024


def _prologue_kernel(a_idx_ref, n_idx_ref, shift_ref, a_ref, n_ref, c_ref, wm_ref, bm_ref,
                     o_ref, mod_ref, *, n_mod):
    j = pl.program_id(0)

    @pl.when(shift_ref[j] == 0)
    def _():
        o_ref[...] = a_ref[...].astype(o_ref.dtype)

    @pl.when(shift_ref[j] != 0)
    def _():
        o_ref[...] = jnp.concatenate([a_ref[GA_W:, :], n_ref[...]], axis=0).astype(o_ref.dtype)

    @pl.when(j < n_mod)
    def _():
        cf = c_ref[...]
        rows = cf.shape[0]
        s_hi, s_lo = _split_bf16(cf * _sigmoid(cf))
        w_hi, w_lo = _split_bf16(wm_ref[...])
        r = _dot(jnp.concatenate([s_hi, s_lo], axis=0), w_hi)
        mod_ref[...] = r[:rows] + r[rows:] + _dot(s_hi, w_lo) + bm_ref[...]


def _prologue(w_t, c_rows, w_mod, b_mod):
    d = w_t.shape[1]
    rows = c_rows.shape[0]
    n_modcols = w_mod.shape[1]
    n_mod = n_modcols // MOD_TN
    a_idx, n_idx, shift = [], [], []
    for name in _ORDER:
        src, width = _SRC[name]
        for c in range(src, src + width, PACK_W):
            base = c - c % PACK_W
            assert c - base in (0, GA_W)
            a_idx.append(base // PACK_W)
            n_idx.append((base + PACK_W) // GA_W if c != base else (n_idx[-1] if n_idx else 0))
            shift.append(c - base)
    n_tiles = len(a_idx)
    assert n_mod <= n_tiles
    as_i32 = lambda v: jnp.asarray(v, jnp.int32)
    mod_tile = lambda j, a, n, s: (0, jnp.minimum(j, n_mod - 1))
    return pl.pallas_call(
        functools.partial(_prologue_kernel, n_mod=n_mod),
        out_shape=(jax.ShapeDtypeStruct((n_tiles * PACK_W, d), BF16),
                   jax.ShapeDtypeStruct((rows, n_modcols), F32)),
        grid_spec=pltpu.PrefetchScalarGridSpec(
            num_scalar_prefetch=3,
            grid=(n_tiles,),
            in_specs=[pl.BlockSpec((PACK_W, d), lambda j, a, n, s: (a[j], 0)),
                      pl.BlockSpec((GA_W, d), lambda j, a, n, s: (n[j], 0)),
                      pl.BlockSpec((rows, d), lambda j, a, n, s: (0, 0)),
                      pl.BlockSpec((d, MOD_TN), mod_tile),
                      pl.BlockSpec((1, MOD_TN), mod_tile)],
            out_specs=(pl.BlockSpec((PACK_W, d), lambda j, a, n, s: (j, 0)),
                       pl.BlockSpec((rows, MOD_TN), mod_tile)),
        ),
        compiler_params=pltpu.CompilerParams(dimension_semantics=("arbitrary",),
                                             vmem_limit_bytes=VMEM_LIMIT),
        name="prologue",
    )(as_i32(a_idx), as_i32(n_idx), as_i32(shift), w_t, w_t, c_rows, w_mod, b_mod.reshape(1, n_modcols))


def _inproj_kernel(x_ref, nw_ref, sh_ref, sc_ref, w_ref, wga_ref, cos_ref, sin_ref, *rest,
                   rope_tiles, tn, n_cast):
    cast_in = rest[:n_cast]
    o_ref, ga_ref = rest[n_cast:n_cast + 2]
    cast_out = rest[n_cast + 2:2 * n_cast + 2]
    h_scr = rest[2 * n_cast + 2]
    j = pl.program_id(1)

    def ride_along_casts():
        for src, dst in zip(cast_in, cast_out):
            dst[...] = src[...].astype(dst.dtype)

    @pl.when(j == 0)
    def _():
        h = _rms(x_ref[...]) * (nw_ref[...] * (1.0 + sc_ref[...])) + sh_ref[...]
        hb = h.astype(BF16)
        h_scr[...] = hb
        g3 = _dot_nt(hb, wga_ref[...])
        lane = lax.broadcasted_iota(jnp.int32, g3.shape, 1)
        resid = g3 - g3.astype(BF16).astype(F32)
        ga_ref[...] = jnp.where((lane >= GA_W) & (lane < 2 * GA_W), resid, g3).astype(ga_ref.dtype)

    if not rope_tiles:
        ride_along_casts()
        o_ref[...] = _dot_nt(h_scr[...], w_ref[...]).astype(o_ref.dtype)
        return

    is_rope = functools.reduce(jnp.logical_or, [j == t for t in rope_tiles])

    @pl.when(is_rope)
    def _():
        ride_along_casts()
        cos = cos_ref[...]
        sin = sin_ref[...]
        half = RET_DK // 2
        for hd in range(0, tn, RET_DK):
            acc = _dot_nt(h_scr[...], w_ref[hd:hd + RET_DK, :])
            t1 = acc[:, :half]
            t2 = acc[:, half:]
            o_ref[:, hd:hd + half] = (t1 * cos - t2 * sin).astype(o_ref.dtype)
            o_ref[:, hd + half:hd + RET_DK] = (t1 * sin + t2 * cos).astype(o_ref.dtype)

    @pl.when(jnp.logical_not(is_rope))
    def _():
        ride_along_casts()
        o_ref[...] = _dot_nt(h_scr[...], w_ref[...]).astype(o_ref.dtype)


def _cast_plan(arr, n_i, n_j):
    rows, cols = arr.shape

    def pieces(extent, want, align):
        n = want
        while extent % n or (extent // n) % align:
            n -= 1
        return n

    if rows % n_j == 0 and (rows // n_j) % 16 == 0:
        nr, nc = n_j, pieces(cols, n_i, LANE)
        index = lambda i, j, nc=nc: (j, jnp.minimum(i, nc - 1))
    else:
        nr, nc = pieces(rows, n_i, 16), pieces(cols, n_j, LANE)
        index = lambda i, j, nr=nr, nc=nc: (jnp.minimum(i, nr - 1), jnp.minimum(j, nc - 1))
    return (rows // nr, cols // nc), index


def _inproj(x2d, norm_w, mod3, mod_row_of_tile, w_main, w_ga, cos, sin, *, tm, tn, ncols, rope, seq_tiles,
            casts=()):
    m, d = x2d.shape
    rope_tiles = ()
    if rope:
        assert tn == RET_QK and _DST["rk"] % tn == 0 and _DST["rq"] % tn == 0
        rope_tiles = (_DST["rk"] // tn, _DST["rq"] // tn)
    grid = (m // tm, ncols // tn)
    cast_specs = [pl.BlockSpec(*_cast_plan(a, *grid)) for a in casts]
    kern = functools.partial(_inproj_kernel, rope_tiles=rope_tiles, tn=tn, n_cast=len(casts))
    return pl.pallas_call(
        kern,
        out_shape=(jax.ShapeDtypeStruct((m, ncols), BF16),
                   jax.ShapeDtypeStruct((m, LANE), BF16),
                   *[jax.ShapeDtypeStruct(a.shape, BF16) for a in casts]),
        grid=grid,
        in_specs=[
            pl.BlockSpec((tm, d), lambda i, j: (i, 0)),
            pl.BlockSpec((1, d), lambda i, j: (0, 0)),
            pl.BlockSpec((None, 1, d), lambda i, j: (mod_row_of_tile(i), 0, 0)),
            pl.BlockSpec((None, 1, d), lambda i, j: (mod_row_of_tile(i), 0, 1)),
            pl.BlockSpec((tn, d), lambda i, j: (j, 0)),
            pl.BlockSpec((LANE, d), lambda i, j: (0, 0)),
            pl.BlockSpec((tm, RET_DK // 2), lambda i, j: (i % seq_tiles, 0)),
            pl.BlockSpec((tm, RET_DK // 2), lambda i, j: (i % seq_tiles, 0)),
            *cast_specs,
        ],
        out_specs=(pl.BlockSpec((tm, tn), lambda i, j: (i, j)),
                   pl.BlockSpec((tm, LANE), lambda i, j: (i, 0)),
                   *cast_specs),
        scratch_shapes=[pltpu.VMEM((tm, d), BF16)],
        compiler_params=pltpu.CompilerParams(dimension_semantics=("arbitrary", "arbitrary"),
                                             vmem_limit_bytes=VMEM_LIMIT),
        name="inproj",
    )(x2d, norm_w, mod3, mod3, w_main, w_ga, cos, sin, *casts)


RET_CHUNK = 256


def _ride_along_casts(rest, n_cast):
    for src, dst in zip(rest[:n_cast], rest[n_cast + 1:2 * n_cast + 1]):
        dst[...] = src[...].astype(dst.dtype)
    return rest[n_cast], rest[2 * n_cast + 1:]


def _ret_kernel(q_ref, k_ref, v_ref, g_ref, kc_ref, vc_ref, rd_ref, gn_ref, *rest, n_chunks, n_cast):
    o_ref, (o_scr, sc_scr, s_scr, dm_scr, eq_scr, wk_scr, sd_scr) = _ride_along_casts(rest, n_cast)
    c_len = RET_CHUNK
    ii = lax.broadcasted_iota(jnp.int32, (c_len, c_len), 0)
    jj = lax.broadcasted_iota(jnp.int32, (c_len, c_len), 1)
    rowi = lax.broadcasted_iota(jnp.int32, (c_len, RET_DK), 0).astype(F32)
    scale = RET_DK ** -0.5

    lg_f = -jnp.exp(rd_ref[0][0:1, :])
    lg_b = -jnp.exp(rd_ref[1][0:1, :])
    dm_scr[...] = jnp.where(jj <= ii, jnp.exp((ii - jj).astype(F32) * lg_f),
                            jnp.exp((jj - ii).astype(F32) * lg_b))
    for direction, lg in ((0, lg_f), (1, lg_b)):
        if direction == 0:
            eq_scr[direction] = jnp.exp((rowi + 1.0) * lg)
            wk = jnp.exp((c_len - 1.0 - rowi) * lg)
        else:
            eq_scr[direction] = jnp.exp((c_len - rowi) * lg)
            wk = jnp.exp(rowi * lg)
        wk_scr[direction] = wk
        sd_scr[direction] = jnp.broadcast_to(jnp.exp(float(c_len) * lg), (8, RET_DV))
        kcw = (kc_ref[...].astype(F32) * wk).astype(BF16)
        s_scr[direction] = _dot_tn(kcw, vc_ref[...])

    def rows_of(c):
        return pl.ds(pl.multiple_of(c * c_len, c_len), c_len)

    def score_stage(c):
        rows = rows_of(c)
        s_qk = _dot_nt(q_ref[rows, :], k_ref[rows, :])
        yield
        sc_scr[rows, :] = (s_qk * dm_scr[...]).astype(BF16)
        yield

    def value_stage(c):
        rows = rows_of(c)
        o_scr[rows, :] = _dot(sc_scr[rows, :], v_ref[rows, :])
        yield

    _run_staged(score_stage(0))

    def intra_body(t, carry):
        _run_staged(score_stage(t + 1), value_stage(t))
        return carry

    lax.fori_loop(0, n_chunks - 1, intra_body, 0, unroll=5)
    _run_staged(value_stage(n_chunks - 1))

    def scan_chunk(direction, c, done):
        rows = rows_of(c)
        q = q_ref[rows, :]
        k = k_ref[rows, :]
        v = v_ref[rows, :]
        s_state = s_scr[direction]
        kv = _dot_tn((k.astype(F32) * wk_scr[direction]).astype(BF16), v)
        o_state = _dot(q, s_state.astype(BF16))
        yield
        s_scr[direction] = sd_scr[direction][0:1, :] * s_state + kv
        tot = o_scr[rows, :] + eq_scr[direction] * o_state
        if not done:
            o_scr[rows, :] = tot
        else:
            mu = jnp.mean(tot, axis=-1, keepdims=True)
            cen = tot - mu
            var = jnp.mean(cen * cen, axis=-1, keepdims=True)
            nrm = cen * (scale * lax.rsqrt(var * (scale * scale) + EPS))
            g = g_ref[rows, :]
            o_ref[rows, :] = (nrm * gn_ref[...] * (g * _sigmoid(g)).astype(F32)).astype(o_ref.dtype)
        yield

    def body(t, carry, done):
        _run_staged(scan_chunk(0, t, done), scan_chunk(1, n_chunks - 1 - t, done))
        return carry

    lax.fori_loop(0, n_chunks // 2, functools.partial(body, done=False), 0, unroll=4)
    lax.fori_loop(n_chunks // 2, n_chunks, functools.partial(body, done=True), 0, unroll=4)


def _retention(p, pc, rd_b, ret_gn, casts, *, batch, seq, ctx_len):
    assert ctx_len == RET_CHUNK and (seq // RET_CHUNK) % 2 == 0
    n_chunks = seq // RET_CHUNK
    w = RET_DK
    kern = functools.partial(_ret_kernel, n_chunks=n_chunks, n_cast=len(casts))
    cast_specs = [pl.BlockSpec(*_cast_plan(a, batch, RET_HEADS)) for a in casts]

    def col(name):
        return _DST[name] // w

    return pl.pallas_call(
        kern,
        out_shape=(jax.ShapeDtypeStruct((batch * seq, RET_VW), BF16),
                   *[jax.ShapeDtypeStruct(a.shape, BF16) for a in casts]),
        grid=(batch, RET_HEADS),
        in_specs=[
            pl.BlockSpec((seq, w), lambda b, h: (b, col("rq") + h)),
            pl.BlockSpec((seq, w), lambda b, h: (b, col("rk") + h)),
            pl.BlockSpec((seq, w), lambda b, h: (b, col("rv") + h)),
            pl.BlockSpec((seq, w), lambda b, h: (b, col("rg") + h)),
            pl.BlockSpec((ctx_len, w), lambda b, h: (b, col("rk") + h)),
            pl.BlockSpec((ctx_len, w), lambda b, h: (b, col("rv") + h)),
            pl.BlockSpec((2, None, 8, w), lambda b, h: (0, h, 0, 0)),
            pl.BlockSpec((1, w), lambda b, h: (0, h)),
            *cast_specs,
        ],
        out_specs=(pl.BlockSpec((seq, w), lambda b, h: (b, h)), *cast_specs),
        scratch_shapes=[
            pltpu.VMEM((seq, RET_DV), F32),
            pltpu.VMEM((seq, RET_CHUNK), BF16),
            pltpu.VMEM((2, RET_DK, RET_DV), F32),
            pltpu.VMEM((RET_CHUNK, RET_CHUNK), F32),
            pltpu.VMEM((2, RET_CHUNK, RET_DV), F32),
            pltpu.VMEM((2, RET_CHUNK, RET_DK), F32),
            pltpu.VMEM((2, 8, RET_DV), F32),
        ],
        compiler_params=pltpu.CompilerParams(dimension_semantics=("arbitrary", "arbitrary"),
                                             vmem_limit_bytes=VMEM_LIMIT),
        name="retention",
    )(p, p, p, p, pc, pc, rd_b, ret_gn, *casts)


GLA_SUB = 64
GLA_BLK = 256
GLA_SCAN_BLK = 512
PREP_UNROLL = 7


def _gla_kernel(q_ref, k_ref, v_ref, g_ref, ga_ref, kc_ref, vc_ref, gac_ref, up_ref, bias_ref, gn_ref,
                *rest, n_blk, n_cast):
    o_ref, scratch = _ride_along_casts(rest, n_cast)
    (ahi_scr, alo_scr, qf_scr, qb_scr, ktf_scr, ktb_scr, khf_scr, khb_scr, khcf_scr, khcb_scr,
     dec_scr, decc_scr, o_scr, s_scr) = scratch
    blk, sub, dk = GLA_BLK, GLA_SUB, GLA_DK
    nsub = blk // sub
    shift = sub.bit_length() - 1
    scale = GLA_DK ** -0.5

    def masks(n):
        ii = lax.broadcasted_iota(jnp.int32, (n, n), 0)
        jj = lax.broadcasted_iota(jnp.int32, (n, n), 1)
        same = (ii >> shift) == (jj >> shift)
        lower = jj <= ii
        return same, lower, jnp.where(same & lower, 1.0, 0.0).astype(BF16)

    ctx_rows = gac_ref.shape[0]
    blk_masks = masks(blk)
    ctx_masks = blk_masks if ctx_rows == blk else masks(ctx_rows)
    r8 = lax.broadcasted_iota(jnp.int32, (8, dk), 0)

    u = up_ref[...]
    u_hi, u_lo = _split_bf16(u)
    urow = lax.broadcasted_iota(jnp.int32, u.shape, 0)
    rhs = jnp.where(urow >= 2 * GA_W, u_lo, u_hi)
    bias = bias_ref[...]

    def gate_stage(ga_blk, put):
        z = _dot(ga_blk, rhs) + bias
        yield
        a = (jnp.minimum(z, 0.0) - jnp.log(1.0 + jnp.exp(-jnp.abs(z)))) * (1.0 / GLA_GATE_NORM)
        put(*_split_bf16(a))
        yield

    def decay_stage(a_hi, a_lo, k_blk, q_blk, msk, store):
        same, lower, tmat = msk
        nsub = a_hi.shape[0] // sub
        assert nsub <= 8
        pre = _dot(tmat, a_hi) + _dot(tmat, a_lo)
        yield
        a = a_hi.astype(F32) + a_lo.astype(F32)
        lasts = [pre[i * sub + sub - 1:i * sub + sub, :] for i in range(nsub)]
        tot = jnp.concatenate([jnp.broadcast_to(l, (sub, 2 * dk)) for l in lasts], axis=0)
        b_f = pre[:, :dk]
        ex_b = pre[:, dk:] - a[:, dk:]
        b_b = tot[:, dk:] - ex_b
        kf = k_blk.astype(F32)
        khf = (kf * jnp.exp(tot[:, :dk] - b_f)).astype(BF16)
        khb = (kf * jnp.exp(ex_b)).astype(BF16)

        def tile8(half):
            rows = [jnp.broadcast_to(l[:, half * dk:(half + 1) * dk], (8, dk)) for l in lasts]
            t = rows[nsub - 1]
            for i in range(nsub - 2, -1, -1):
                t = jnp.where(r8 == i, rows[i], t)
            return t

        dec = jnp.exp(jnp.concatenate([tile8(0), tile8(1), jnp.zeros((LANE - 16, dk), F32)], axis=0))
        out = dict(khf=khf, khb=khb, dec_t=dec.T)
        if q_blk is not None:
            qf32 = q_blk.astype(F32)
            out.update(qf=(qf32 * jnp.exp(b_f)).astype(BF16), qb=(qf32 * jnp.exp(b_b)).astype(BF16),
                       ktf=(kf * jnp.exp(-b_f)).astype(BF16), ktb=(kf * jnp.exp(-b_b)).astype(BF16))
        store(**out)
        yield

    def score_stage(r):
        same, lower, _ = blk_masks
        rows = pl.ds(r, blk)
        s_f = _dot_nt(qf_scr[rows, :], ktf_scr[rows, :])
        s_b = _dot_nt(qb_scr[rows, :], ktb_scr[rows, :])
        yield
        s = jnp.where(same, jnp.where(lower, s_f, s_b), 0.0).astype(BF16)
        yield
        o_scr[rows, :] = _dot(s, v_ref[rows, :])
        yield

    run = _run_staged

    def store_ctx(khf, khb, dec_t):
        khcf_scr[...] = khf
        khcb_scr[...] = khb
        decc_scr[...] = dec_t

    ctx_gate = []
    run(gate_stage(gac_ref[...], lambda hi, lo: ctx_gate.extend((hi, lo))))
    run(decay_stage(*ctx_gate, kc_ref[...], None, ctx_masks, store_ctx))

    def gate_block(t):
        rows = pl.ds(pl.multiple_of(t * blk, blk), blk)

        def put(a_hi, a_lo):
            ahi_scr[rows, :] = a_hi
            alo_scr[rows, :] = a_lo

        return gate_stage(ga_ref[rows, :], put)

    def decay_block(t):
        rows = pl.ds(pl.multiple_of(t * blk, blk), blk)

        def store(khf, khb, dec_t, qf, qb, ktf, ktb):
            khf_scr[rows, :] = khf
            khb_scr[rows, :] = khb
            qf_scr[rows, :] = qf
            qb_scr[rows, :] = qb
            ktf_scr[rows, :] = ktf
            ktb_scr[rows, :] = ktb
            dec_scr[t] = dec_t

        return decay_stage(ahi_scr[rows, :], alo_scr[rows, :], k_ref[rows, :], q_ref[rows, :],
                           blk_masks, store)

    def score_block(t):
        return score_stage(pl.multiple_of(t * blk, blk))

    run(gate_block(0))
    run(gate_block(1), decay_block(0))

    def prep_body(t, carry):
        run(gate_block(t + 2), decay_block(t + 1), score_block(t))
        return carry

    lax.fori_loop(0, n_blk - 2, prep_body, 0, unroll=PREP_UNROLL)
    run(decay_block(n_blk - 1), score_block(n_blk - 2))
    run(score_block(n_blk - 1))

    def scan_blk(direction, n_sub, dec_ts, kh_at, v_at, q_at=None, o_rows=None):
        s = s_scr[direction]
        order = range(n_sub) if direction == 0 else range(n_sub - 1, -1, -1)
        kv = {i: _dot_tn(kh_at(i), v_at(i)) for i in order}
        yield
        for i in order:
            c = direction * 8 + i % nsub
            col = dec_ts[i // nsub][:, c:c + 1]
            if q_at is not None:
                rows = o_rows(i)
                o_scr[rows, :] += _dot(q_at(i), s.astype(BF16))
            s = s * col + kv[i]
            yield
        s_scr[direction] = s

    s_scr[...] = jnp.zeros(s_scr.shape, F32)
    run(*[scan_blk(direction, ctx_rows // sub, [decc_scr[...]],
                   lambda i, khc=khc: khc[i * sub:(i + 1) * sub, :],
                   lambda i: vc_ref[i * sub:(i + 1) * sub, :])
          for direction, khc in ((0, khcf_scr), (1, khcb_scr))])

    sblk = GLA_SCAN_BLK
    per = sblk // blk
    n_sblk = n_blk // per

    fin_rows = 128

    def finalize(t):
        for tb in (t, n_sblk - 1 - t):
            for r0 in range(0, sblk, fin_rows):
                rows = pl.ds(pl.multiple_of(tb * sblk + r0, fin_rows), fin_rows)
                o = o_scr[rows, :]
                ms = jnp.mean(o * o, axis=-1, keepdims=True)
                nrm = o * (scale * lax.rsqrt(ms * (scale * scale) + EPS))
                g = g_ref[rows, :]
                o_ref[rows, :] = (nrm * gn_ref[...] * (g * _sigmoid(g)).astype(F32)).astype(o_ref.dtype)
                yield

    def scans(t):
        for direction, q_scr, kh_scr in ((0, qf_scr, khf_scr), (1, qb_scr, khb_scr)):
            tb = t if direction == 0 else n_sblk - 1 - t

            def rows(i, tb=tb):
                return pl.ds(pl.multiple_of(tb * sblk + i * sub, sub), sub)

            yield from scan_blk(direction, sblk // sub, [dec_scr[tb * per + j] for j in range(per)],
                                lambda i, kh_scr=kh_scr, rows=rows: kh_scr[rows(i), :],
                                lambda i, rows=rows: v_ref[rows(i), :],
                                lambda i, q_scr=q_scr, rows=rows: q_scr[rows(i), :],
                                rows)

    def scan_body(t, carry, fin_prev):
        if fin_prev:
            run(scans(t), finalize(t - 1))
        else:
            run(scans(t))
        return carry

    half = n_sblk // 2
    lax.fori_loop(0, half + 1, functools.partial(scan_body, fin_prev=False), 0, unroll=half + 1)
    lax.fori_loop(half + 1, n_sblk, functools.partial(scan_body, fin_prev=True), 0, unroll=n_sblk - half - 1)
    run(finalize(n_sblk - 1))


def _gla(p, ga, pc, gac, up_stack, bias2, gla_gn, casts, *, batch, seq, ctx_len):
    assert ctx_len % GLA_SUB == 0 and ctx_len <= GLA_BLK and GLA_SCAN_BLK % GLA_BLK == 0
    assert (seq // GLA_SCAN_BLK) % 2 == 0
    n_blk = seq // GLA_BLK
    kern = functools.partial(_gla_kernel, n_blk=n_blk, n_cast=len(casts))
    dk, dv = GLA_DK, GLA_DV
    cast_specs = [pl.BlockSpec(*_cast_plan(a, batch, GLA_HEADS)) for a in casts]
    return pl.pallas_call(
        kern,
        out_shape=(jax.ShapeDtypeStruct((batch * seq, GLA_VW), BF16),
                   *[jax.ShapeDtypeStruct(a.shape, BF16) for a in casts]),
        grid=(batch, GLA_HEADS),
        in_specs=[
            pl.BlockSpec((seq, dk), lambda b, h: (b, _DST["gq"] // dk + h)),
            pl.BlockSpec((seq, dk), lambda b, h: (b, _DST["gk"] // dk + h)),
            pl.BlockSpec((seq, dv), lambda b, h: (b, _DST["gv"] // dv + h)),
            pl.BlockSpec((seq, dv), lambda b, h: (b, _DST["gg"] // dv + h)),
            pl.BlockSpec((seq, LANE), lambda b, h: (b, 0)),
            pl.BlockSpec((ctx_len, dk), lambda b, h: (b, _DST["gk"] // dk + h)),
            pl.BlockSpec((ctx_len, dv), lambda b, h: (b, _DST["gv"] // dv + h)),
            pl.BlockSpec((ctx_len, LANE), lambda b, h: (b, 0)),
            pl.BlockSpec((LANE, 2 * dk), lambda b, h: (0, h)),
            pl.BlockSpec((1, 2 * dk), lambda b, h: (0, h)),
            pl.BlockSpec((1, dv), lambda b, h: (0, h)),
            *cast_specs,
        ],
        out_specs=(pl.BlockSpec((seq, dv), lambda b, h: (b, h)), *cast_specs),
        scratch_shapes=[
            pltpu.VMEM((seq, 2 * dk), BF16), pltpu.VMEM((seq, 2 * dk), BF16),
            pltpu.VMEM((seq, dk), BF16), pltpu.VMEM((seq, dk), BF16),
            pltpu.VMEM((seq, dk), BF16), pltpu.VMEM((seq, dk), BF16),
            pltpu.VMEM((seq, dk), BF16), pltpu.VMEM((seq, dk), BF16),
            pltpu.VMEM((ctx_len, dk), BF16), pltpu.VMEM((ctx_len, dk), BF16),
            pltpu.VMEM((n_blk, dk, LANE), F32), pltpu.VMEM((dk, LANE), F32),
            pltpu.VMEM((seq, dv), F32),
            pltpu.VMEM((2, dk, dv), F32),
        ],
        compiler_params=pltpu.CompilerParams(dimension_semantics=("arbitrary", "arbitrary"),
                                             vmem_limit_bytes=VMEM_LIMIT),
        name="gla",
    )(p, p, p, p, ga, pc, pc, gac, up_stack, bias2, gla_gn, *casts)


MXU_N = 256


POST_ROWS = 128


def _post_kernel(or_ref, og_ref, ga0_ref, ga1_ref, gb0_ref, gb1_ref, x_ref, wur_ref, wug_ref, wo_ref,
                 g1_ref, sh2_ref, sc2_ref, npost_ref, npre_ref, wsrc_ref, hs_ref, h2_ref, wdst_ref):
    wdst_ref[...] = wsrc_ref[...].astype(wdst_ref.dtype)
    half = ga0_ref.shape[1]
    tm = hs_ref.shape[0]
    groups = [slice(r0, r0 + POST_ROWS) for r0 in range(0, tm, POST_ROWS)]

    def up(rows):
        return _dot(or_ref[rows, :], wur_ref[...]), _dot(og_ref[rows, :], wug_ref[...])

    def out(rows, y_ret, y_gla):
        def merge(ga_r, gb_r, lo):
            return (_sigmoid(ga_r[rows, :].astype(F32)) * y_ret[:, lo:lo + half]
                    + _sigmoid(gb_r[rows, :].astype(F32)) * y_gla[:, lo:lo + half]).astype(BF16)

        merged = jnp.concatenate([merge(ga0_ref, gb0_ref, 0), merge(ga1_ref, gb1_ref, half)], axis=1)
        return _dot(merged, wo_ref[...])

    post_gain = g1_ref[...] * npost_ref[...]
    pre_gain = npre_ref[...] * (1.0 + sc2_ref[...])

    def finish(rows, y):
        hs = x_ref[rows, :] + _rms(y) * post_gain
        hs_ref[rows, :] = hs
        h2 = _rms(hs) * pre_gain + sh2_ref[...]
        h2_ref[rows, :] = h2.astype(h2_ref.dtype)

    ups = [up(rows) for rows in groups]
    ys = [out(rows, *u) for rows, u in zip(groups, ups)]
    for rows, y in zip(groups, ys):
        finish(rows, y)


def _post(o_r, o_g, p, x2d, w_up_ret, w_up_gla, w_out, mod3, npost, npre, w_later, *, tm, seq):
    m, d = x2d.shape
    tiles_per_seq = seq // tm
    const = dict(pipeline_mode=pl.Buffered(1))
    gw = d // 2
    n_steps = m // tm
    assert w_later.shape[0] % (16 * n_steps) == 0
    cast_spec = pl.BlockSpec((w_later.shape[0] // n_steps, w_later.shape[1]), lambda i: (i, 0))

    def modspec(chunk):
        return pl.BlockSpec((None, 1, d), lambda i: (i // tiles_per_seq, 0, chunk))

    def gatespec(name, part):
        return pl.BlockSpec((tm, gw), lambda i: (i, _DST[name] // gw + part))

    return pl.pallas_call(
        _post_kernel,
        out_shape=(jax.ShapeDtypeStruct((m, d), F32), jax.ShapeDtypeStruct((m, d), BF16),
                   jax.ShapeDtypeStruct(w_later.shape, BF16)),
        grid=(n_steps,),
        in_specs=[
            pl.BlockSpec((tm, RET_VW), lambda i: (i, 0)),
            pl.BlockSpec((tm, GLA_VW), lambda i: (i, 0)),
            gatespec("gate_a", 0), gatespec("gate_a", 1), gatespec("gate_b", 0), gatespec("gate_b", 1),
            pl.BlockSpec((tm, d), lambda i: (i, 0)),
            pl.BlockSpec((RET_VW, d), lambda i: (0, 0), **const),
            pl.BlockSpec((GLA_VW, d), lambda i: (0, 0), **const),
            pl.BlockSpec((d, d), lambda i: (0, 0), **const),
            modspec(2), modspec(3), modspec(4),
            pl.BlockSpec((1, d), lambda i: (0, 0)),
            pl.BlockSpec((1, d), lambda i: (0, 0)),
            cast_spec,
        ],
        out_specs=(pl.BlockSpec((tm, d), lambda i: (i, 0)),
                   pl.BlockSpec((tm, d), lambda i: (i, 0)),
                   cast_spec),
        compiler_params=pltpu.CompilerParams(dimension_semantics=("arbitrary",),
                                             vmem_limit_bytes=VMEM_LIMIT),
        name="post",
    )(o_r, o_g, p, p, p, p, x2d, w_up_ret, w_up_gla, w_out, mod3, mod3, mod3, npost, npre, w_later)


def _ffn_up_kernel(h_ref, wg_ref, wu_ref, wsrc_ref, a_ref, wdst_ref):
    wdst_ref[...] = wsrc_ref[...].astype(wdst_ref.dtype)
    h = h_ref[...]
    for lo in range(0, a_ref.shape[1], MXU_N):
        g = _dot(h, wg_ref[:, lo:lo + MXU_N])
        u = _dot(h, wu_ref[:, lo:lo + MXU_N])
        a_ref[:, lo:lo + MXU_N] = (g * _sigmoid(g) * u).astype(a_ref.dtype)


def _ffn_up(h2, wg, wu, w_later, *, tm, tf):
    m, d = h2.shape
    d_ff = wg.shape[1]
    grid = (m // tm, d_ff // tf)
    cast_spec = pl.BlockSpec(*_cast_plan(w_later, *grid))
    return pl.pallas_call(
        _ffn_up_kernel,
        out_shape=(jax.ShapeDtypeStruct((m, d_ff), BF16), jax.ShapeDtypeStruct(w_later.shape, BF16)),
        grid=grid,
        in_specs=[
            pl.BlockSpec((tm, d), lambda i, f: (i, 0)),
            pl.BlockSpec((d, tf), lambda i, f: (0, f)),
            pl.BlockSpec((d, tf), lambda i, f: (0, f)),
            cast_spec,
        ],
        out_specs=(pl.BlockSpec((tm, tf), lambda i, f: (i, f)), cast_spec),
        compiler_params=pltpu.CompilerParams(dimension_semantics=("arbitrary", "arbitrary"),
                                             vmem_limit_bytes=VMEM_LIMIT),
        name="ffn_up",
    )(h2, wg, wu, w_later)


def _ffn_down_kernel(a_ref, wd_ref, hs_ref, g2_ref, nw_ref, o_ref, y_scr, *, n_split):
    tm, d = o_ref.shape
    w = d // n_split
    ss = jnp.zeros((tm, 1), F32)
    for lo in range(0, d, w):
        y = _dot(a_ref[...], wd_ref[:, lo:lo + w])
        y_scr[:, lo:lo + w] = y
        ss = ss + jnp.sum(y * y, axis=-1, keepdims=True)
    inv = lax.rsqrt(ss * (1.0 / d) + EPS)
    o_ref[...] = hs_ref[...] + y_scr[...] * inv * (g2_ref[...] * nw_ref[...])


def _ffn_down(a, hs, wd, mod3, npost, *, tm, seq):
    m, d = hs.shape
    d_ff = wd.shape[0]
    tiles_per_seq = seq // tm
    kern = functools.partial(_ffn_down_kernel, n_split=4)
    return pl.pallas_call(
        kern,
        out_shape=jax.ShapeDtypeStruct((m, d), F32),
        grid=(m // tm,),
        in_specs=[
            pl.BlockSpec((tm, d_ff), lambda i: (i, 0)),
            pl.BlockSpec((d_ff, d), lambda i: (0, 0), pipeline_mode=pl.Buffered(1)),
            pl.BlockSpec((tm, d), lambda i: (i, 0)),
            pl.BlockSpec((None, 1, d), lambda i: (i // tiles_per_seq, 0, 5)),
            pl.BlockSpec((1, d), lambda i: (0, 0)),
        ],
        out_specs=pl.BlockSpec((tm, d), lambda i: (i, 0)),
        scratch_shapes=[pltpu.VMEM((tm, d), F32)],
        compiler_params=pltpu.CompilerParams(dimension_semantics=("arbitrary",),
                                             vmem_limit_bytes=VMEM_LIMIT),
        name="ffn_down",
    )(a, wd, hs, mod3, npost)


def _rope_tables(seq):
    f32 = np.float32
    rows = seq // GRID_W
    pos_r = np.repeat(np.arange(rows, dtype=f32), GRID_W)
    pos_c = np.tile(np.arange(GRID_W, dtype=f32), rows)
    n_f = RET_DK // 4
    inv = (f32(ROPE_BASE) ** (-np.arange(n_f, dtype=f32) / f32(n_f))).astype(f32)
    ang = np.concatenate([pos_r[:, None] * inv, pos_c[:, None] * inv], axis=-1).astype(f32)
    return jnp.asarray(np.cos(ang), F32), jnp.asarray(np.sin(ang), F32)


def _gate_map_layout(gla_a_up, gla_a_bias):
    r, h, dk = GLA_LOW_RANK, GLA_HEADS, GLA_DK
    u = jnp.zeros((GA_W, h, 2, dk), F32)
    u = u.at[:r, :, 0, :].set(gla_a_up[0].reshape(r, h, dk))
    u = u.at[r:, :, 1, :].set(gla_a_up[1].reshape(r, h, dk))
    u = u.reshape(GA_W, h * 2 * dk)
    up_stack = jnp.concatenate([u, u, u, jnp.zeros((LANE - 3 * GA_W, h * 2 * dk), F32)], axis=0)
    bias2 = jnp.stack([gla_a_bias[0].reshape(h, dk), gla_a_bias[1].reshape(h, dk)], axis=1)
    return up_stack, bias2.reshape(1, h * 2 * dk)


def _layer(h_state, c_rows, ctx2d, w_mod, b_mod, norm_mix_pre, norm_mix_post, norm_ffn_pre, norm_ffn_post,
           w_in, ret_decay, gla_a_up, gla_a_bias, ret_gn, gla_gn, w_up_ret, w_up_gla, w_out,
           ffn_w_gate, ffn_w_up, ffn_w_down, cos, sin, *, batch, seq, ctx_len):
    d = D_MODEL
    ga0 = _SRC["ga"][0]
    w_t = w_in.T
    w_g = w_t[ga0:ga0 + GA_W]
    w_ga = jnp.concatenate([w_g, w_g, w_g, jnp.zeros((LANE - 3 * GA_W, d), F32)], axis=0).astype(BF16)
    up_stack, bias2 = _gate_map_layout(gla_a_up, gla_a_bias)
    rd_b = jnp.broadcast_to(ret_decay.reshape(2, RET_HEADS, 1, 1), (2, RET_HEADS, 8, RET_DK))

    w_main, mod = _prologue(w_t, c_rows, w_mod, b_mod)
    mod3 = mod.reshape(mod.shape[0], 1, 6 * d)
    nw_pre = norm_mix_pre.reshape(1, d)

    tm = 1024
    p, ga, wur, wug, wo = _inproj(
        h_state, nw_pre, mod3, lambda i: i // (seq // tm), w_main, w_ga, cos, sin,
        tm=tm, tn=1024, ncols=N_MAIN, rope=True, seq_tiles=seq // tm,
        casts=(w_up_ret, w_up_gla, w_out))
    pc, gac = _inproj(ctx2d, nw_pre, mod3, lambda i: batch, w_main, w_ga, cos, sin,
                      tm=batch * ctx_len, tn=512, ncols=N_CTX, rope=False, seq_tiles=1)

    (o_r,) = _retention(p, pc, rd_b, ret_gn.reshape(1, RET_VW), (), batch=batch, seq=seq, ctx_len=ctx_len)
    o_g, wu = _gla(p, ga, pc, gac, up_stack, bias2, gla_gn.reshape(1, GLA_VW), (ffn_w_up,),
                   batch=batch, seq=seq, ctx_len=ctx_len)

    hs, h2, wg = _post(o_r, o_g, p, h_state, wur, wug, wo,
                       mod3, norm_mix_post.reshape(1, d), norm_ffn_pre.reshape(1, d), ffn_w_gate,
                       tm=256, seq=seq)
    act, wd = _ffn_up(h2, wg, wu, ffn_w_down, tm=1024, tf=512)
    return _ffn_down(act, hs, wd, mod3, norm_ffn_post.reshape(1, d), tm=256, seq=seq)


def kernel(x, c, ctx, c_ctx, w_mod, b_mod, norm_mix_pre, norm_mix_post, norm_ffn_pre, norm_ffn_post,
           w_in, ret_decay, gla_a_up, gla_a_bias, ret_gn, gla_gn, w_up_ret, w_up_gla, w_out,
           ffn_w_gate, ffn_w_up, ffn_w_down):
    batch, seq, d = x.shape
    ctx_len = ctx.shape[1]
    depth = w_mod.shape[0]
    cos, sin = _rope_tables(seq)
    c_rows = jnp.zeros((16, d), F32).at[:batch].set(c.astype(F32)).at[batch].set(c_ctx.astype(F32))
    ctx2d = ctx.astype(F32).reshape(batch * ctx_len, d)
    h_state = x.astype(F32).reshape(batch * seq, d)
    for i in range(depth):
        h_state = _layer(h_state, c_rows, ctx2d, w_mod[i], b_mod[i], norm_mix_pre[i], norm_mix_post[i],
                         norm_ffn_pre[i], norm_ffn_post[i], w_in[i], ret_decay[i], gla_a_up[i], gla_a_bias[i],
                         ret_gn[i], gla_gn[i], w_up_ret[i], w_up_gla[i], w_out[i],
                         ffn_w_gate[i], ffn_w_up[i], ffn_w_down[i], cos, sin,
                         batch=batch, seq=seq, ctx_len=ctx_len)
    return h_state.reshape(batch, seq, d).astype(x.dtype)
```

```python
import functools

import jax
import jax.numpy as jnp
import numpy as np
from jax import lax
from jax.experimental import pallas as pl
from jax.experimental.pallas import tpu as pltpu

F32 = jnp.float32
BF16 = jnp.bfloat16

D_MODEL = 2048
GRID_W = 64
RET_HEADS = 4
RET_DK = 256
RET_DV = 256
GLA_HEADS = 4
GLA_DK = 128
GLA_DV = 256
GLA_LOW_RANK = 16
GLA_GATE_NORM = 16.0
ROPE_BASE = 10000.0
EPS = 1e-6

RET_QK = RET_HEADS * RET_DK
RET_VW = RET_HEADS * RET_DV
GLA_KW = GLA_HEADS * GLA_DK
GLA_VW = GLA_HEADS * GLA_DV
GA_W = 2 * GLA_LOW_RANK
LANE = 128

_SRC = {}
_off = 0
for _name, _w in (("rk", RET_QK), ("rv", RET_VW), ("gk", GLA_KW), ("gv", GLA_VW), ("ga", GA_W),
                  ("rq", RET_QK), ("rg", RET_VW), ("gq", GLA_KW), ("gg", GLA_VW),
                  ("gate_a", D_MODEL), ("gate_b", D_MODEL)):
    _SRC[_name] = (_off, _w)
    _off += _w

_ORDER = ("rk", "rv", "gv", "gk", "gq", "rq", "rg", "gg", "gate_a", "gate_b")
_DST = {}
_off = 0
for _name in _ORDER:
    _DST[_name] = _off
    _off += _SRC[_name][1]
N_MAIN = _off
N_CTX = _DST["gq"]
PACK_W = 512

VMEM_LIMIT = 56 * 1024 * 1024


def _dot(a, b):
    return jnp.dot(a, b, preferred_element_type=F32)


def _dot_nt(a, b):
    return lax.dot_general(a, b, (((1,), (1,)), ((), ())), preferred_element_type=F32)


def _dot_tn(a, b):
    return lax.dot_general(a, b, (((0,), (0,)), ((), ())), preferred_element_type=F32)


def _sigmoid(x):
    return 1.0 / (1.0 + jnp.exp(-x))


def _rms(x):
    return x * lax.rsqrt(jnp.mean(x * x, axis=-1, keepdims=True) + EPS)


def _run_staged(*stages):
    live = list(stages)
    while live:
        live = [g for g in live if next(g, live) is not live]


def _split_bf16(x):
    hi = x.astype(BF16)
    lo = (x - hi.astype(F32)).astype(BF16)
    return hi, lo


MOD_TN = 1024


def _prologue_kernel(a_idx_ref, n_idx_ref, shift_ref, a_ref, n_ref, c_ref, wm_ref, bm_ref,
                     o_ref, mod_ref, *, n_mod):
    j = pl.program_id(0)

    @pl.when(shift_ref[j] == 0)
    def _():
        o_ref[...] = a_ref[...].astype(o_ref.dtype)

    @pl.when(shift_ref[j] != 0)
    def _():
        o_ref[...] = jnp.concatenate([a_ref[GA_W:, :], n_ref[...]], axis=0).astype(o_ref.dtype)

    @pl.when(j < n_mod)
    def _():
        cf = c_ref[...]
        rows = cf.shape[0]
        s_hi, s_lo = _split_bf16(cf * _sigmoid(cf))
        w_hi, w_lo = _split_bf16(wm_ref[...])
        r = _dot(jnp.concatenate([s_hi, s_lo], axis=0), w_hi)
        mod_ref[...] = r[:rows] + r[rows:] + _dot(s_hi, w_lo) + bm_ref[...]


def _prologue(w_t, c_rows, w_mod, b_mod):
    d = w_t.shape[1]
    rows = c_rows.shape[0]
    n_modcols = w_mod.shape[1]
    n_mod = n_modcols // MOD_TN
    a_idx, n_idx, shift = [], [], []
    for name in _ORDER:
        src, width = _SRC[name]
        for c in range(src, src + width, PACK_W):
            base = c - c % PACK_W
            assert c - base in (0, GA_W)
            a_idx.append(base // PACK_W)
            n_idx.append((base + PACK_W) // GA_W if c != base else (n_idx[-1] if n_idx else 0))
            shift.append(c - base)
    n_tiles = len(a_idx)
    assert n_mod <= n_tiles
    as_i32 = lambda v: jnp.asarray(v, jnp.int32)
    mod_tile = lambda j, a, n, s: (0, jnp.minimum(j, n_mod - 1))
    return pl.pallas_call(
        functools.partial(_prologue_kernel, n_mod=n_mod),
        out_shape=(jax.ShapeDtypeStruct((n_tiles * PACK_W, d), BF16),
                   jax.ShapeDtypeStruct((rows, n_modcols), F32)),
        grid_spec=pltpu.PrefetchScalarGridSpec(
            num_scalar_prefetch=3,
            grid=(n_tiles,),
            in_specs=[pl.BlockSpec((PACK_W, d), lambda j, a, n, s: (a[j], 0)),
                      pl.BlockSpec((GA_W, d), lambda j, a, n, s: (n[j], 0)),
                      pl.BlockSpec((rows, d), lambda j, a, n, s: (0, 0)),
                      pl.BlockSpec((d, MOD_TN), mod_tile),
                      pl.BlockSpec((1, MOD_TN), mod_tile)],
            out_specs=(pl.BlockSpec((PACK_W, d), lambda j, a, n, s: (j, 0)),
                       pl.BlockSpec((rows, MOD_TN), mod_tile)),
        ),
        compiler_params=pltpu.CompilerParams(dimension_semantics=("arbitrary",),
                                             vmem_limit_bytes=VMEM_LIMIT),
        name="prologue",
    )(as_i32(a_idx), as_i32(n_idx), as_i32(shift), w_t, w_t, c_rows, w_mod, b_mod.reshape(1, n_modcols))


def _inproj_kernel(x_ref, nw_ref, sh_ref, sc_ref, w_ref, wga_ref, cos_ref, sin_ref, *rest,
                   rope_tiles, tn, n_cast):
    cast_in = rest[:n_cast]
    o_ref, ga_ref = rest[n_cast:n_cast + 2]
    cast_out = rest[n_cast + 2:2 * n_cast + 2]
    h_scr = rest[2 * n_cast + 2]
    j = pl.program_id(1)

    def ride_along_casts():
        for src, dst in zip(cast_in, cast_out):
            dst[...] = src[...].astype(dst.dtype)

    @pl.when(j == 0)
    def _():
        h = _rms(x_ref[...]) * (nw_ref[...] * (1.0 + sc_ref[...])) + sh_ref[...]
        hb = h.astype(BF16)
        h_scr[...] = hb
        g3 = _dot_nt(hb, wga_ref[...])
        lane = lax.broadcasted_iota(jnp.int32, g3.shape, 1)
        resid = g3 - g3.astype(BF16).astype(F32)
        ga_ref[...] = jnp.where((lane >= GA_W) & (lane < 2 * GA_W), resid, g3).astype(ga_ref.dtype)

    if not rope_tiles:
        ride_along_casts()
        o_ref[...] = _dot_nt(h_scr[...], w_ref[...]).astype(o_ref.dtype)
        return

    is_rope = functools.reduce(jnp.logical_or, [j == t for t in rope_tiles])

    @pl.when(is_rope)
    def _():
        ride_along_casts()
        cos = cos_ref[...]
        sin = sin_ref[...]
        half = RET_DK // 2
        for hd in range(0, tn, RET_DK):
            acc = _dot_nt(h_scr[...], w_ref[hd:hd + RET_DK, :])
            t1 = acc[:, :half]
            t2 = acc[:, half:]
            o_ref[:, hd:hd + half] = (t1 * cos - t2 * sin).astype(o_ref.dtype)
            o_ref[:, hd + half:hd + RET_DK] = (t1 * sin + t2 * cos).astype(o_ref.dtype)

    @pl.when(jnp.logical_not(is_rope))
    def _():
        ride_along_casts()
        o_ref[...] = _dot_nt(h_scr[...], w_ref[...]).astype(o_ref.dtype)


def _cast_plan(arr, n_i, n_j):
    rows, cols = arr.shape

    def pieces(extent, want, align):
        n = want
        while extent % n or (extent // n) % align:
            n -= 1
        return n

    if rows % n_j == 0 and (rows // n_j) % 16 == 0:
        nr, nc = n_j, pieces(cols, n_i, LANE)
        index = lambda i, j, nc=nc: (j, jnp.minimum(i, nc - 1))
    else:
        nr, nc = pieces(rows, n_i, 16), pieces(cols, n_j, LANE)
        index = lambda i, j, nr=nr, nc=nc: (jnp.minimum(i, nr - 1), jnp.minimum(j, nc - 1))
    return (rows // nr, cols // nc), index


def _inproj(x2d, norm_w, mod3, mod_row_of_tile, w_main, w_ga, cos, sin, *, tm, tn, ncols, rope, seq_tiles,
            casts=()):
    m, d = x2d.shape
    rope_tiles = ()
    if rope:
        assert tn == RET_QK and _DST["rk"] % tn == 0 and _DST["rq"] % tn == 0
        rope_tiles = (_DST["rk"] // tn, _DST["rq"] // tn)
    grid = (m // tm, ncols // tn)
    cast_specs = [pl.BlockSpec(*_cast_plan(a, *grid)) for a in casts]
    kern = functools.partial(_inproj_kernel, rope_tiles=rope_tiles, tn=tn, n_cast=len(casts))
    return pl.pallas_call(
        kern,
        out_shape=(jax.ShapeDtypeStruct((m, ncols), BF16),
                   jax.ShapeDtypeStruct((m, LANE), BF16),
                   *[jax.ShapeDtypeStruct(a.shape, BF16) for a in casts]),
        grid=grid,
        in_specs=[
            pl.BlockSpec((tm, d), lambda i, j: (i, 0)),
            pl.BlockSpec((1, d), lambda i, j: (0, 0)),
            pl.BlockSpec((None, 1, d), lambda i, j: (mod_row_of_tile(i), 0, 0)),
            pl.BlockSpec((None, 1, d), lambda i, j: (mod_row_of_tile(i), 0, 1)),
            pl.BlockSpec((tn, d), lambda i, j: (j, 0)),
            pl.BlockSpec((LANE, d), lambda i, j: (0, 0)),
            pl.BlockSpec((tm, RET_DK // 2), lambda i, j: (i % seq_tiles, 0)),
            pl.BlockSpec((tm, RET_DK // 2), lambda i, j: (i % seq_tiles, 0)),
            *cast_specs,
        ],
        out_specs=(pl.BlockSpec((tm, tn), lambda i, j: (i, j)),
                   pl.BlockSpec((tm, LANE), lambda i, j: (i, 0)),
                   *cast_specs),
        scratch_shapes=[pltpu.VMEM((tm, d), BF16)],
        compiler_params=pltpu.CompilerParams(dimension_semantics=("arbitrary", "arbitrary"),
                                             vmem_limit_bytes=VMEM_LIMIT),
        name="inproj",
    )(x2d, norm_w, mod3, mod3, w_main, w_ga, cos, sin, *casts)


RET_CHUNK = 256


def _ride_along_casts(rest, n_cast):
    for src, dst in zip(rest[:n_cast], rest[n_cast + 1:2 * n_cast + 1]):
        dst[...] = src[...].astype(dst.dtype)
    return rest[n_cast], rest[2 * n_cast + 1:]


def _ret_kernel(q_ref, k_ref, v_ref, g_ref, kc_ref, vc_ref, rd_ref, gn_ref, *rest, n_chunks, n_cast):
    o_ref, (o_scr, sc_scr, s_scr, dm_scr, eq_scr, wk_scr, sd_scr) = _ride_along_casts(rest, n_cast)
    c_len = RET_CHUNK
    ii = lax.broadcasted_iota(jnp.int32, (c_len, c_len), 0)
    jj = lax.broadcasted_iota(jnp.int32, (c_len, c_len), 1)
    rowi = lax.broadcasted_iota(jnp.int32, (c_len, RET_DK), 0).astype(F32)
    scale = RET_DK ** -0.5

    lg_f = -jnp.exp(rd_ref[0][0:1, :])
    lg_b = -jnp.exp(rd_ref[1][0:1, :])
    dm_scr[...] = jnp.where(jj <= ii, jnp.exp((ii - jj).astype(F32) * lg_f),
                            jnp.exp((jj - ii).astype(F32) * lg_b))
    for direction, lg in ((0, lg_f), (1, lg_b)):
        if direction == 0:
            eq_scr[direction] = jnp.exp((rowi + 1.0) * lg)
            wk = jnp.exp((c_len - 1.0 - rowi) * lg)
        else:
            eq_scr[direction] = jnp.exp((c_len - rowi) * lg)
            wk = jnp.exp(rowi * lg)
        wk_scr[direction] = wk
        sd_scr[direction] = jnp.broadcast_to(jnp.exp(float(c_len) * lg), (8, RET_DV))
        kcw = (kc_ref[...].astype(F32) * wk).astype(BF16)
        s_scr[direction] = _dot_tn(kcw, vc_ref[...])

    def rows_of(c):
        return pl.ds(pl.multiple_of(c * c_len, c_len), c_len)

    def score_stage(c):
        rows = rows_of(c)
        s_qk = _dot_nt(q_ref[rows, :], k_ref[rows, :])
        yield
        sc_scr[rows, :] = (s_qk * dm_scr[...]).astype(BF16)
        yield

    def value_stage(c):
        rows = rows_of(c)
        o_scr[rows, :] = _dot(sc_scr[rows, :], v_ref[rows, :])
        yield

    _run_staged(score_stage(0))

    def intra_body(t, carry):
        _run_staged(score_stage(t + 1), value_stage(t))
        return carry

    lax.fori_loop(0, n_chunks - 1, intra_body, 0, unroll=5)
    _run_staged(value_stage(n_chunks - 1))

    def scan_chunk(direction, c, done):
        rows = rows_of(c)
        q = q_ref[rows, :]
        k = k_ref[rows, :]
        v = v_ref[rows, :]
        s_state = s_scr[direction]
        kv = _dot_tn((k.astype(F32) * wk_scr[direction]).astype(BF16), v)
        o_state = _dot(q, s_state.astype(BF16))
        yield
        s_scr[direction] = sd_scr[direction][0:1, :] * s_state + kv
        tot = o_scr[rows, :] + eq_scr[direction] * o_state
        if not done:
            o_scr[rows, :] = tot
        else:
            mu = jnp.mean(tot, axis=-1, keepdims=True)
            cen = tot - mu
            var = jnp.mean(cen * cen, axis=-1, keepdims=True)
            nrm = cen * (scale * lax.rsqrt(var * (scale * scale) + EPS))
            g = g_ref[rows, :]
            o_ref[rows, :] = (nrm * gn_ref[...] * (g * _sigmoid(g)).astype(F32)).astype(o_ref.dtype)
        yield

    def body(t, carry, done):
        _run_staged(scan_chunk(0, t, done), scan_chunk(1, n_chunks - 1 - t, done))
        return carry

    lax.fori_loop(0, n_chunks // 2, functools.partial(body, done=False), 0, unroll=4)
    lax.fori_loop(n_chunks // 2, n_chunks, functools.partial(body, done=True), 0, unroll=4)


def _retention(p, pc, rd_b, ret_gn, casts, *, batch, seq, ctx_len):
    assert ctx_len == RET_CHUNK and (seq // RET_CHUNK) % 2 == 0
    n_chunks = seq // RET_CHUNK
    w = RET_DK
    kern = functools.partial(_ret_kernel, n_chunks=n_chunks, n_cast=len(casts))
    cast_specs = [pl.BlockSpec(*_cast_plan(a, batch, RET_HEADS)) for a in casts]

    def col(name):
        return _DST[name] // w

    return pl.pallas_call(
        kern,
        out_shape=(jax.ShapeDtypeStruct((batch * seq, RET_VW), BF16),
                   *[jax.ShapeDtypeStruct(a.shape, BF16) for a in casts]),
        grid=(batch, RET_HEADS),
        in_specs=[
            pl.BlockSpec((seq, w), lambda b, h: (b, col("rq") + h)),
            pl.BlockSpec((seq, w), lambda b, h: (b, col("rk") + h)),
            pl.BlockSpec((seq, w), lambda b, h: (b, col("rv") + h)),
            pl.BlockSpec((seq, w), lambda b, h: (b, col("rg") + h)),
            pl.BlockSpec((ctx_len, w), lambda b, h: (b, col("rk") + h)),
            pl.BlockSpec((ctx_len, w), lambda b, h: (b, col("rv") + h)),
            pl.BlockSpec((2, None, 8, w), lambda b, h: (0, h, 0, 0)),
            pl.BlockSpec((1, w), lambda b, h: (0, h)),
            *cast_specs,
        ],
        out_specs=(pl.BlockSpec((seq, w), lambda b, h: (b, h)), *cast_specs),
        scratch_shapes=[
            pltpu.VMEM((seq, RET_DV), F32),
            pltpu.VMEM((seq, RET_CHUNK), BF16),
            pltpu.VMEM((2, RET_DK, RET_DV), F32),
            pltpu.VMEM((RET_CHUNK, RET_CHUNK), F32),
            pltpu.VMEM((2, RET_CHUNK, RET_DV), F32),
            pltpu.VMEM((2, RET_CHUNK, RET_DK), F32),
            pltpu.VMEM((2, 8, RET_DV), F32),
        ],
        compiler_params=pltpu.CompilerParams(dimension_semantics=("arbitrary", "arbitrary"),
                                             vmem_limit_bytes=VMEM_LIMIT),
        name="retention",
    )(p, p, p, p, pc, pc, rd_b, ret_gn, *casts)


GLA_SUB = 64
GLA_BLK = 256
GLA_SCAN_BLK = 512
PREP_UNROLL = 7


def _gla_kernel(q_ref, k_ref, v_ref, g_ref, ga_ref, kc_ref, vc_ref, gac_ref, up_ref, bias_ref, gn_ref,
                *rest, n_blk, n_cast):
    o_ref, scratch = _ride_along_casts(rest, n_cast)
    (ahi_scr, alo_scr, qf_scr, qb_scr, ktf_scr, ktb_scr, khf_scr, khb_scr, khcf_scr, khcb_scr,
     dec_scr, decc_scr, o_scr, s_scr) = scratch
    blk, sub, dk = GLA_BLK, GLA_SUB, GLA_DK
    nsub = blk // sub
    shift = sub.bit_length() - 1
    scale = GLA_DK ** -0.5

    def masks(n):
        ii = lax.broadcasted_iota(jnp.int32, (n, n), 0)
        jj = lax.broadcasted_iota(jnp.int32, (n, n), 1)
        same = (ii >> shift) == (jj >> shift)
        lower = jj <= ii
        return same, lower, jnp.where(same & lower, 1.0, 0.0).astype(BF16)

    ctx_rows = gac_ref.shape[0]
    blk_masks = masks(blk)
    ctx_masks = blk_masks if ctx_rows == blk else masks(ctx_rows)
    r8 = lax.broadcasted_iota(jnp.int32, (8, dk), 0)

    u = up_ref[...]
    u_hi, u_lo = _split_bf16(u)
    urow = lax.broadcasted_iota(jnp.int32, u.shape, 0)
    rhs = jnp.where(urow >= 2 * GA_W, u_lo, u_hi)
    bias = bias_ref[...]

    def gate_stage(ga_blk, put):
        z = _dot(ga_blk, rhs) + bias
        yield
        a = (jnp.minimum(z, 0.0) - jnp.log(1.0 + jnp.exp(-jnp.abs(z)))) * (1.0 / GLA_GATE_NORM)
        put(*_split_bf16(a))
        yield

    def decay_stage(a_hi, a_lo, k_blk, q_blk, msk, store):
        same, lower, tmat = msk
        nsub = a_hi.shape[0] // sub
        assert nsub <= 8
        pre = _dot(tmat, a_hi) + _dot(tmat, a_lo)
        yield
        a = a_hi.astype(F32) + a_lo.astype(F32)
        lasts = [pre[i * sub + sub - 1:i * sub + sub, :] for i in range(nsub)]
        tot = jnp.concatenate([jnp.broadcast_to(l, (sub, 2 * dk)) for l in lasts], axis=0)
        b_f = pre[:, :dk]
        ex_b = pre[:, dk:] - a[:, dk:]
        b_b = tot[:, dk:] - ex_b
        kf = k_blk.astype(F32)
        khf = (kf * jnp.exp(tot[:, :dk] - b_f)).astype(BF16)
        khb = (kf * jnp.exp(ex_b)).astype(BF16)

        def tile8(half):
            rows = [jnp.broadcast_to(l[:, half * dk:(half + 1) * dk], (8, dk)) for l in lasts]
            t = rows[nsub - 1]
            for i in range(nsub - 2, -1, -1):
                t = jnp.where(r8 == i, rows[i], t)
            return t

        dec = jnp.exp(jnp.concatenate([tile8(0), tile8(1), jnp.zeros((LANE - 16, dk), F32)], axis=0))
        out = dict(khf=khf, khb=khb, dec_t=dec.T)
        if q_blk is not None:
            qf32 = q_blk.astype(F32)
            out.update(qf=(qf32 * jnp.exp(b_f)).astype(BF16), qb=(qf32 * jnp.exp(b_b)).astype(BF16),
                       ktf=(kf * jnp.exp(-b_f)).astype(BF16), ktb=(kf * jnp.exp(-b_b)).astype(BF16))
        store(**out)
        yield

    def score_stage(r):
        same, lower, _ = blk_masks
        rows = pl.ds(r, blk)
        s_f = _dot_nt(qf_scr[rows, :], ktf_scr[rows, :])
        s_b = _dot_nt(qb_scr[rows, :], ktb_scr[rows, :])
        yield
        s = jnp.where(same, jnp.where(lower, s_f, s_b), 0.0).astype(BF16)
        yield
        o_scr[rows, :] = _dot(s, v_ref[rows, :])
        yield

    run = _run_staged

    def store_ctx(khf, khb, dec_t):
        khcf_scr[...] = khf
        khcb_scr[...] = khb
        decc_scr[...] = dec_t

    ctx_gate = []
    run(gate_stage(gac_ref[...], lambda hi, lo: ctx_gate.extend((hi, lo))))
    run(decay_stage(*ctx_gate, kc_ref[...], None, ctx_masks, store_ctx))

    def gate_block(t):
        rows = pl.ds(pl.multiple_of(t * blk, blk), blk)

        def put(a_hi, a_lo):
            ahi_scr[rows, :] = a_hi
            alo_scr[rows, :] = a_lo

        return gate_stage(ga_ref[rows, :], put)

    def decay_block(t):
        rows = pl.ds(pl.multiple_of(t * blk, blk), blk)

        def store(khf, khb, dec_t, qf, qb, ktf, ktb):
            khf_scr[rows, :] = khf
            khb_scr[rows, :] = khb
            qf_scr[rows, :] = qf
            qb_scr[rows, :] = qb
            ktf_scr[rows, :] = ktf
            ktb_scr[rows, :] = ktb
            dec_scr[t] = dec_t

        return decay_stage(ahi_scr[rows, :], alo_scr[rows, :], k_ref[rows, :], q_ref[rows, :],
                           blk_masks, store)

    def score_block(t):
        return score_stage(pl.multiple_of(t * blk, blk))

    run(gate_block(0))
    run(gate_block(1), decay_block(0))

    def prep_body(t, carry):
        run(gate_block(t + 2), decay_block(t + 1), score_block(t))
        return carry

    lax.fori_loop(0, n_blk - 2, prep_body, 0, unroll=PREP_UNROLL)
    run(decay_block(n_blk - 1), score_block(n_blk - 2))
    run(score_block(n_blk - 1))

    def scan_blk(direction, n_sub, dec_ts, kh_at, v_at, q_at=None, o_rows=None):
        s = s_scr[direction]
        order = range(n_sub) if direction == 0 else range(n_sub - 1, -1, -1)
        kv = {i: _dot_tn(kh_at(i), v_at(i)) for i in order}
        yield
        for i in order:
            c = direction * 8 + i % nsub
            col = dec_ts[i // nsub][:, c:c + 1]
            if q_at is not None:
                rows = o_rows(i)
                o_scr[rows, :] += _dot(q_at(i), s.astype(BF16))
            s = s * col + kv[i]
            yield
        s_scr[direction] = s

    s_scr[...] = jnp.zeros(s_scr.shape, F32)
    run(*[scan_blk(direction, ctx_rows // sub, [decc_scr[...]],
                   lambda i, khc=khc: khc[i * sub:(i + 1) * sub, :],
                   lambda i: vc_ref[i * sub:(i + 1) * sub, :])
          for direction, khc in ((0, khcf_scr), (1, khcb_scr))])

    sblk = GLA_SCAN_BLK
    per = sblk // blk
    n_sblk = n_blk // per

    fin_rows = 128

    def finalize(t):
        for tb in (t, n_sblk - 1 - t):
            for r0 in range(0, sblk, fin_rows):
                rows = pl.ds(pl.multiple_of(tb * sblk + r0, fin_rows), fin_rows)
                o = o_scr[rows, :]
                ms = jnp.mean(o * o, axis=-1, keepdims=True)
                nrm = o * (scale * lax.rsqrt(ms * (scale * scale) + EPS))
                g = g_ref[rows, :]
                o_ref[rows, :] = (nrm * gn_ref[...] * (g * _sigmoid(g)).astype(F32)).astype(o_ref.dtype)
                yield

    def scans(t):
        for direction, q_scr, kh_scr in ((0, qf_scr, khf_scr), (1, qb_scr, khb_scr)):
            tb = t if direction == 0 else n_sblk - 1 - t

            def rows(i, tb=tb):
                return pl.ds(pl.multiple_of(tb * sblk + i * sub, sub), sub)

            yield from scan_blk(direction, sblk // sub, [dec_scr[tb * per + j] for j in range(per)],
                                lambda i, kh_scr=kh_scr, rows=rows: kh_scr[rows(i), :],
                                lambda i, rows=rows: v_ref[rows(i), :],
                                lambda i, q_scr=q_scr, rows=rows: q_scr[rows(i), :],
                                rows)

    def scan_body(t, carry, fin_prev):
        if fin_prev:
            run(scans(t), finalize(t - 1))
        else:
            run(scans(t))
        return carry

    half = n_sblk // 2
    lax.fori_loop(0, half + 1, functools.partial(scan_body, fin_prev=False), 0, unroll=half + 1)
    lax.fori_loop(half + 1, n_sblk, functools.partial(scan_body, fin_prev=True), 0, unroll=n_sblk - half - 1)
    run(finalize(n_sblk - 1))


def _gla(p, ga, pc, gac, up_stack, bias2, gla_gn, casts, *, batch, seq, ctx_len):
    assert ctx_len % GLA_SUB == 0 and ctx_len <= GLA_BLK and GLA_SCAN_BLK % GLA_BLK == 0
    assert (seq // GLA_SCAN_BLK) % 2 == 0
    n_blk = seq // GLA_BLK
    kern = functools.partial(_gla_kernel, n_blk=n_blk, n_cast=len(casts))
    dk, dv = GLA_DK, GLA_DV
    cast_specs = [pl.BlockSpec(*_cast_plan(a, batch, GLA_HEADS)) for a in casts]
    return pl.pallas_call(
        kern,
        out_shape=(jax.ShapeDtypeStruct((batch * seq, GLA_VW), BF16),
                   *[jax.ShapeDtypeStruct(a.shape, BF16) for a in casts]),
        grid=(batch, GLA_HEADS),
        in_specs=[
            pl.BlockSpec((seq, dk), lambda b, h: (b, _DST["gq"] // dk + h)),
            pl.BlockSpec((seq, dk), lambda b, h: (b, _DST["gk"] // dk + h)),
            pl.BlockSpec((seq, dv), lambda b, h: (b, _DST["gv"] // dv + h)),
            pl.BlockSpec((seq, dv), lambda b, h: (b, _DST["gg"] // dv + h)),
            pl.BlockSpec((seq, LANE), lambda b, h: (b, 0)),
            pl.BlockSpec((ctx_len, dk), lambda b, h: (b, _DST["gk"] // dk + h)),
            pl.BlockSpec((ctx_len, dv), lambda b, h: (b, _DST["gv"] // dv + h)),
            pl.BlockSpec((ctx_len, LANE), lambda b, h: (b, 0)),
            pl.BlockSpec((LANE, 2 * dk), lambda b, h: (0, h)),
            pl.BlockSpec((1, 2 * dk), lambda b, h: (0, h)),
            pl.BlockSpec((1, dv), lambda b, h: (0, h)),
            *cast_specs,
        ],
        out_specs=(pl.BlockSpec((seq, dv), lambda b, h: (b, h)), *cast_specs),
        scratch_shapes=[
            pltpu.VMEM((seq, 2 * dk), BF16), pltpu.VMEM((seq, 2 * dk), BF16),
            pltpu.VMEM((seq, dk), BF16), pltpu.VMEM((seq, dk), BF16),
            pltpu.VMEM((seq, dk), BF16), pltpu.VMEM((seq, dk), BF16),
            pltpu.VMEM((seq, dk), BF16), pltpu.VMEM((seq, dk), BF16),
            pltpu.VMEM((ctx_len, dk), BF16), pltpu.VMEM((ctx_len, dk), BF16),
            pltpu.VMEM((n_blk, dk, LANE), F32), pltpu.VMEM((dk, LANE), F32),
            pltpu.VMEM((seq, dv), F32),
            pltpu.VMEM((2, dk, dv), F32),
        ],
        compiler_params=pltpu.CompilerParams(dimension_semantics=("arbitrary", "arbitrary"),
                                             vmem_limit_bytes=VMEM_LIMIT),
        name="gla",
    )(p, p, p, p, ga, pc, pc, gac, up_stack, bias2, gla_gn, *casts)


MXU_N = 256


POST_ROWS = 128


def _post_kernel(or_ref, og_ref, ga0_ref, ga1_ref, gb0_ref, gb1_ref, x_ref, wur_ref, wug_ref, wo_ref,
                 g1_ref, sh2_ref, sc2_ref, npost_ref, npre_ref, wsrc_ref, hs_ref, h2_ref, wdst_ref):
    wdst_ref[...] = wsrc_ref[...].astype(wdst_ref.dtype)
    half = ga0_ref.shape[1]
    tm = hs_ref.shape[0]
    groups = [slice(r0, r0 + POST_ROWS) for r0 in range(0, tm, POST_ROWS)]

    def up(rows):
        return _dot(or_ref[rows, :], wur_ref[...]), _dot(og_ref[rows, :], wug_ref[...])

    def out(rows, y_ret, y_gla):
        def merge(ga_r, gb_r, lo):
            return (_sigmoid(ga_r[rows, :].astype(F32)) * y_ret[:, lo:lo + half]
                    + _sigmoid(gb_r[rows, :].astype(F32)) * y_gla[:, lo:lo + half]).astype(BF16)

        merged = jnp.concatenate([merge(ga0_ref, gb0_ref, 0), merge(ga1_ref, gb1_ref, half)], axis=1)
        return _dot(merged, wo_ref[...])

    post_gain = g1_ref[...] * npost_ref[...]
    pre_gain = npre_ref[...] * (1.0 + sc2_ref[...])

    def finish(rows, y):
        hs = x_ref[rows, :] + _rms(y) * post_gain
        hs_ref[rows, :] = hs
        h2 = _rms(hs) * pre_gain + sh2_ref[...]
        h2_ref[rows, :] = h2.astype(h2_ref.dtype)

    ups = [up(rows) for rows in groups]
    ys = [out(rows, *u) for rows, u in zip(groups, ups)]
    for rows, y in zip(groups, ys):
        finish(rows, y)


def _post(o_r, o_g, p, x2d, w_up_ret, w_up_gla, w_out, mod3, npost, npre, w_later, *, tm, seq):
    m, d = x2d.shape
    tiles_per_seq = seq // tm
    const = dict(pipeline_mode=pl.Buffered(1))
    gw = d // 2
    n_steps = m // tm
    assert w_later.shape[0] % (16 * n_steps) == 0
    cast_spec = pl.BlockSpec((w_later.shape[0] // n_steps, w_later.shape[1]), lambda i: (i, 0))

    def modspec(chunk):
        return pl.BlockSpec((None, 1, d), lambda i: (i // tiles_per_seq, 0, chunk))

    def gatespec(name, part):
        return pl.BlockSpec((tm, gw), lambda i: (i, _DST[name] // gw + part))

    return pl.pallas_call(
        _post_kernel,
        out_shape=(jax.ShapeDtypeStruct((m, d), F32), jax.ShapeDtypeStruct((m, d), BF16),
                   jax.ShapeDtypeStruct(w_later.shape, BF16)),
        grid=(n_steps,),
        in_specs=[
            pl.BlockSpec((tm, RET_VW), lambda i: (i, 0)),
            pl.BlockSpec((tm, GLA_VW), lambda i: (i, 0)),
            gatespec("gate_a", 0), gatespec("gate_a", 1), gatespec("gate_b", 0), gatespec("gate_b", 1),
            pl.BlockSpec((tm, d), lambda i: (i, 0)),
            pl.BlockSpec((RET_VW, d), lambda i: (0, 0), **const),
            pl.BlockSpec((GLA_VW, d), lambda i: (0, 0), **const),
            pl.BlockSpec((d, d), lambda i: (0, 0), **const),
            modspec(2), modspec(3), modspec(4),
            pl.BlockSpec((1, d), lambda i: (0, 0)),
            pl.BlockSpec((1, d), lambda i: (0, 0)),
            cast_spec,
        ],
        out_specs=(pl.BlockSpec((tm, d), lambda i: (i, 0)),
                   pl.BlockSpec((tm, d), lambda i: (i, 0)),
                   cast_spec),
        compiler_params=pltpu.CompilerParams(dimension_semantics=("arbitrary",),
                                             vmem_limit_bytes=VMEM_LIMIT),
        name="post",
    )(o_r, o_g, p, p, p, p, x2d, w_up_ret, w_up_gla, w_out, mod3, mod3, mod3, npost, npre, w_later)


def _ffn_up_kernel(h_ref, wg_ref, wu_ref, wsrc_ref, a_ref, wdst_ref):
    wdst_ref[...] = wsrc_ref[...].astype(wdst_ref.dtype)
    h = h_ref[...]
    for lo in range(0, a_ref.shape[1], MXU_N):
        g = _dot(h, wg_ref[:, lo:lo + MXU_N])
        u = _dot(h, wu_ref[:, lo:lo + MXU_N])
        a_ref[:, lo:lo + MXU_N] = (g * _sigmoid(g) * u).astype(a_ref.dtype)


def _ffn_up(h2, wg, wu, w_later, *, tm, tf):
    m, d = h2.shape
    d_ff = wg.shape[1]
    grid = (m // tm, d_ff // tf)
    cast_spec = pl.BlockSpec(*_cast_plan(w_later, *grid))
    return pl.pallas_call(
        _ffn_up_kernel,
        out_shape=(jax.ShapeDtypeStruct((m, d_ff), BF16), jax.ShapeDtypeStruct(w_later.shape, BF16)),
        grid=grid,
        in_specs=[
            pl.BlockSpec((tm, d), lambda i, f: (i, 0)),
            pl.BlockSpec((d, tf), lambda i, f: (0, f)),
            pl.BlockSpec((d, tf), lambda i, f: (0, f)),
            cast_spec,
        ],
        out_specs=(pl.BlockSpec((tm, tf), lambda i, f: (i, f)), cast_spec),
        compiler_params=pltpu.CompilerParams(dimension_semantics=("arbitrary", "arbitrary"),
                                             vmem_limit_bytes=VMEM_LIMIT),
        name="ffn_up",
    )(h2, wg, wu, w_later)


def _ffn_down_kernel(a_ref, wd_ref, hs_ref, g2_ref, nw_ref, o_ref, y_scr, *, n_split):
    tm, d = o_ref.shape
    w = d // n_split
    ss = jnp.zeros((tm, 1), F32)
    for lo in range(0, d, w):
        y = _dot(a_ref[...], wd_ref[:, lo:lo + w])
        y_scr[:, lo:lo + w] = y
        ss = ss + jnp.sum(y * y, axis=-1, keepdims=True)
    inv = lax.rsqrt(ss * (1.0 / d) + EPS)
    o_ref[...] = hs_ref[...] + y_scr[...] * inv * (g2_ref[...] * nw_ref[...])


def _ffn_down(a, hs, wd, mod3, npost, *, tm, seq):
    m, d = hs.shape
    d_ff = wd.shape[0]
    tiles_per_seq = seq // tm
    kern = functools.partial(_ffn_down_kernel, n_split=4)
    return pl.pallas_call(
        kern,
        out_shape=jax.ShapeDtypeStruct((m, d), F32),
        grid=(m // tm,),
        in_specs=[
            pl.BlockSpec((tm, d_ff), lambda i: (i, 0)),
            pl.BlockSpec((d_ff, d), lambda i: (0, 0), pipeline_mode=pl.Buffered(1)),
            pl.BlockSpec((tm, d), lambda i: (i, 0)),
            pl.BlockSpec((None, 1, d), lambda i: (i // tiles_per_seq, 0, 5)),
            pl.BlockSpec((1, d), lambda i: (0, 0)),
        ],
        out_specs=pl.BlockSpec((tm, d), lambda i: (i, 0)),
        scratch_shapes=[pltpu.VMEM((tm, d), F32)],
        compiler_params=pltpu.CompilerParams(dimension_semantics=("arbitrary",),
                                             vmem_limit_bytes=VMEM_LIMIT),
        name="ffn_down",
    )(a, wd, hs, mod3, npost)


def _rope_tables(seq):
    f32 = np.float32
    rows = seq // GRID_W
    pos_r = np.repeat(np.arange(rows, dtype=f32), GRID_W)
    pos_c = np.tile(np.arange(GRID_W, dtype=f32), rows)
    n_f = RET_DK // 4
    inv = (f32(ROPE_BASE) ** (-np.arange(n_f, dtype=f32) / f32(n_f))).astype(f32)
    ang = np.concatenate([pos_r[:, None] * inv, pos_c[:, None] * inv], axis=-1).astype(f32)
    return jnp.asarray(np.cos(ang), F32), jnp.asarray(np.sin(ang), F32)


def _gate_map_layout(gla_a_up, gla_a_bias):
    r, h, dk = GLA_LOW_RANK, GLA_HEADS, GLA_DK
    u = jnp.zeros((GA_W, h, 2, dk), F32)
    u = u.at[:r, :, 0, :].set(gla_a_up[0].reshape(r, h, dk))
    u = u.at[r:, :, 1, :].set(gla_a_up[1].reshape(r, h, dk))
    u = u.reshape(GA_W, h * 2 * dk)
    up_stack = jnp.concatenate([u, u, u, jnp.zeros((LANE - 3 * GA_W, h * 2 * dk), F32)], axis=0)
    bias2 = jnp.stack([gla_a_bias[0].reshape(h, dk), gla_a_bias[1].reshape(h, dk)], axis=1)
    return up_stack, bias2.reshape(1, h * 2 * dk)


def _layer(h_state, c_rows, ctx2d, w_mod, b_mod, norm_mix_pre, norm_mix_post, norm_ffn_pre, norm_ffn_post,
           w_in, ret_decay, gla_a_up, gla_a_bias, ret_gn, gla_gn, w_up_ret, w_up_gla, w_out,
           ffn_w_gate, ffn_w_up, ffn_w_down, cos, sin, *, batch, seq, ctx_len):
    d = D_MODEL
    ga0 = _SRC["ga"][0]
    w_t = w_in.T
    w_g = w_t[ga0:ga0 + GA_W]
    w_ga = jnp.concatenate([w_g, w_g, w_g, jnp.zeros((LANE - 3 * GA_W, d), F32)], axis=0).astype(BF16)
    up_stack, bias2 = _gate_map_layout(gla_a_up, gla_a_bias)
    rd_b = jnp.broadcast_to(ret_decay.reshape(2, RET_HEADS, 1, 1), (2, RET_HEADS, 8, RET_DK))

    w_main, mod = _prologue(w_t, c_rows, w_mod, b_mod)
    mod3 = mod.reshape(mod.shape[0], 1, 6 * d)
    nw_pre = norm_mix_pre.reshape(1, d)

    tm = 1024
    p, ga, wur, wug, wo = _inproj(
        h_state, nw_pre, mod3, lambda i: i // (seq // tm), w_main, w_ga, cos, sin,
        tm=tm, tn=1024, ncols=N_MAIN, rope=True, seq_tiles=seq // tm,
        casts=(w_up_ret, w_up_gla, w_out))
    pc, gac = _inproj(ctx2d, nw_pre, mod3, lambda i: batch, w_main, w_ga, cos, sin,
                      tm=batch * ctx_len, tn=N_CTX // 2, ncols=N_CTX, rope=False, seq_tiles=1)

    (o_r,) = _retention(p, pc, rd_b, ret_gn.reshape(1, RET_VW), (), batch=batch, seq=seq, ctx_len=ctx_len)
    o_g, wu = _gla(p, ga, pc, gac, up_stack, bias2, gla_gn.reshape(1, GLA_VW), (ffn_w_up,),
                   batch=batch, seq=seq, ctx_len=ctx_len)

    hs, h2, wg = _post(o_r, o_g, p, h_state, wur, wug, wo,
                       mod3, norm_mix_post.reshape(1, d), norm_ffn_pre.reshape(1, d), ffn_w_gate,
                       tm=256, seq=seq)
    act, wd = _ffn_up(h2, wg, wu, ffn_w_down, tm=1024, tf=512)
    return _ffn_down(act, hs, wd, mod3, norm_ffn_post.reshape(1, d), tm=256, seq=seq)


def kernel(x, c, ctx, c_ctx, w_mod, b_mod, norm_mix_pre, norm_mix_post, norm_ffn_pre, norm_ffn_post,
           w_in, ret_decay, gla_a_up, gla_a_bias, ret_gn, gla_gn, w_up_ret, w_up_gla, w_out,
           ffn_w_gate, ffn_w_up, ffn_w_down):
    batch, seq, d = x.shape
    ctx_len = ctx.shape[1]
    depth = w_mod.shape[0]
    cos, sin = _rope_tables(seq)
    c_rows = jnp.zeros((16, d), F32).at[:batch].set(c.astype(F32)).at[batch].set(c_ctx.astype(F32))
    ctx2d = ctx.astype(F32).reshape(batch * ctx_len, d)
    h_state = x.astype(F32).reshape(batch * seq, d)
    for i in range(depth):
        h_state = _layer(h_state, c_rows, ctx2d, w_mod[i], b_mod[i], norm_mix_pre[i], norm_mix_post[i],
                         norm_ffn_pre[i], norm_ffn_post[i], w_in[i], ret_decay[i], gla_a_up[i], gla_a_bias[i],
                         ret_gn[i], gla_gn[i], w_up_ret[i], w_up_gla[i], w_out[i],
                         ffn_w_gate[i], ffn_w_up[i], ffn_w_down[i], cos, sin,
                         batch=batch, seq=seq, ctx_len=ctx_len)
    return h_state.reshape(batch, seq, d).astype(x.dtype)
```

```python
import functools

import jax
import jax.numpy as jnp
import numpy as np
from jax import lax
from jax.experimental import pallas as pl
from jax.experimental.pallas import tpu as pltpu

F32 = jnp.float32
BF16 = jnp.bfloat16

D_MODEL = 2048
GRID_W = 64
RET_HEADS = 4
RET_DK = 256
RET_DV = 256
GLA_HEADS = 4
GLA_DK = 128
GLA_DV = 256
GLA_LOW_RANK = 16
GLA_GATE_NORM = 16.0
ROPE_BASE = 10000.0
EPS = 1e-6

RET_QK = RET_HEADS * RET_DK
RET_VW = RET_HEADS * RET_DV
GLA_KW = GLA_HEADS * GLA_DK
GLA_VW = GLA_HEADS * GLA_DV
GA_W = 2 * GLA_LOW_RANK
LANE = 128

_SRC = {}
_off = 0
for _name, _w in (("rk", RET_QK), ("rv", RET_VW), ("gk", GLA_KW), ("gv", GLA_VW), ("ga", GA_W),
                  ("rq", RET_QK), ("rg", RET_VW), ("gq", GLA_KW), ("gg", GLA_VW),
                  ("gate_a", D_MODEL), ("gate_b", D_MODEL)):
    _SRC[_name] = (_off, _w)
    _off += _w

_ORDER = ("rk", "rv", "gv", "gk", "gq", "rq", "rg", "gg", "gate_a", "gate_b")
_DST = {}
_off = 0
for _name in _ORDER:
    _DST[_name] = _off
    _off += _SRC[_name][1]
N_MAIN = _off
N_CTX = _DST["gq"]
PACK_W = 512

VMEM_LIMIT = 56 * 1024 * 1024


def _dot(a, b):
    return jnp.dot(a, b, preferred_element_type=F32)


def _dot_nt(a, b):
    return lax.dot_general(a, b, (((1,), (1,)), ((), ())), preferred_element_type=F32)


def _dot_tn(a, b):
    return lax.dot_general(a, b, (((0,), (0,)), ((), ())), preferred_element_type=F32)


def _sigmoid(x):
    return 1.0 / (1.0 + jnp.exp(-x))


def _rms(x):
    return x * lax.rsqrt(jnp.mean(x * x, axis=-1, keepdims=True) + EPS)


def _run_staged(*stages):
    live = list(stages)
    while live:
        live = [g for g in live if next(g, live) is not live]


def _split_bf16(x):
    hi = x.astype(BF16)
    lo = (x - hi.astype(F32)).astype(BF16)
    return hi, lo


MOD_TN = 1024


def _prologue_kernel(a_idx_ref, n_idx_ref, shift_ref, a_ref, n_ref, c_ref, wm_ref, bm_ref,
                     o_ref, mod_ref, *, n_mod):
    j = pl.program_id(0)

    @pl.when(shift_ref[j] == 0)
    def _():
        o_ref[...] = a_ref[...].astype(o_ref.dtype)

    @pl.when(shift_ref[j] != 0)
    def _():
        o_ref[...] = jnp.concatenate([a_ref[GA_W:, :], n_ref[...]], axis=0).astype(o_ref.dtype)

    @pl.when(j < n_mod)
    def _():
        cf = c_ref[...]
        rows = cf.shape[0]
        s_hi, s_lo = _split_bf16(cf * _sigmoid(cf))
        w_hi, w_lo = _split_bf16(wm_ref[...])
        r = _dot(jnp.concatenate([s_hi, s_lo], axis=0), w_hi)
        mod_ref[...] = r[:rows] + r[rows:] + _dot(s_hi, w_lo) + bm_ref[...]


def _prologue(w_t, c_rows, w_mod, b_mod):
    d = w_t.shape[1]
    rows = c_rows.shape[0]
    n_modcols = w_mod.shape[1]
    n_mod = n_modcols // MOD_TN
    a_idx, n_idx, shift = [], [], []
    for name in _ORDER:
        src, width = _SRC[name]
        for c in range(src, src + width, PACK_W):
            base = c - c % PACK_W
            assert c - base in (0, GA_W)
            a_idx.append(base // PACK_W)
            n_idx.append((base + PACK_W) // GA_W if c != base else (n_idx[-1] if n_idx else 0))
            shift.append(c - base)
    n_tiles = len(a_idx)
    assert n_mod <= n_tiles
    as_i32 = lambda v: jnp.asarray(v, jnp.int32)
    mod_tile = lambda j, a, n, s: (0, jnp.minimum(j, n_mod - 1))
    return pl.pallas_call(
        functools.partial(_prologue_kernel, n_mod=n_mod),
        out_shape=(jax.ShapeDtypeStruct((n_tiles * PACK_W, d), BF16),
                   jax.ShapeDtypeStruct((rows, n_modcols), F32)),
        grid_spec=pltpu.PrefetchScalarGridSpec(
            num_scalar_prefetch=3,
            grid=(n_tiles,),
            in_specs=[pl.BlockSpec((PACK_W, d), lambda j, a, n, s: (a[j], 0)),
                      pl.BlockSpec((GA_W, d), lambda j, a, n, s: (n[j], 0)),
                      pl.BlockSpec((rows, d), lambda j, a, n, s: (0, 0)),
                      pl.BlockSpec((d, MOD_TN), mod_tile),
                      pl.BlockSpec((1, MOD_TN), mod_tile)],
            out_specs=(pl.BlockSpec((PACK_W, d), lambda j, a, n, s: (j, 0)),
                       pl.BlockSpec((rows, MOD_TN), mod_tile)),
        ),
        compiler_params=pltpu.CompilerParams(dimension_semantics=("arbitrary",),
                                             vmem_limit_bytes=VMEM_LIMIT),
        name="prologue",
    )(as_i32(a_idx), as_i32(n_idx), as_i32(shift), w_t, w_t, c_rows, w_mod, b_mod.reshape(1, n_modcols))


def _inproj_kernel(x_ref, nw_ref, sh_ref, sc_ref, w_ref, wga_ref, cos_ref, sin_ref, *rest,
                   rope_tiles, tn, n_cast):
    cast_in = rest[:n_cast]
    o_ref, ga_ref = rest[n_cast:n_cast + 2]
    cast_out = rest[n_cast + 2:2 * n_cast + 2]
    h_scr = rest[2 * n_cast + 2]
    j = pl.program_id(1)

    def ride_along_casts():
        for src, dst in zip(cast_in, cast_out):
            dst[...] = src[...].astype(dst.dtype)

    @pl.when(j == 0)
    def _():
        h = _rms(x_ref[...]) * (nw_ref[...] * (1.0 + sc_ref[...])) + sh_ref[...]
        hb = h.astype(BF16)
        h_scr[...] = hb
        g3 = _dot_nt(hb, wga_ref[...])
        lane = lax.broadcasted_iota(jnp.int32, g3.shape, 1)
        resid = g3 - g3.astype(BF16).astype(F32)
        ga_ref[...] = jnp.where((lane >= GA_W) & (lane < 2 * GA_W), resid, g3).astype(ga_ref.dtype)

    if not rope_tiles:
        ride_along_casts()
        o_ref[...] = _dot_nt(h_scr[...], w_ref[...]).astype(o_ref.dtype)
        return

    is_rope = functools.reduce(jnp.logical_or, [j == t for t in rope_tiles])

    @pl.when(is_rope)
    def _():
        ride_along_casts()
        cos = cos_ref[...]
        sin = sin_ref[...]
        half = RET_DK // 2
        for hd in range(0, tn, RET_DK):
            acc = _dot_nt(h_scr[...], w_ref[hd:hd + RET_DK, :])
            t1 = acc[:, :half]
            t2 = acc[:, half:]
            o_ref[:, hd:hd + half] = (t1 * cos - t2 * sin).astype(o_ref.dtype)
            o_ref[:, hd + half:hd + RET_DK] = (t1 * sin + t2 * cos).astype(o_ref.dtype)

    @pl.when(jnp.logical_not(is_rope))
    def _():
        ride_along_casts()
        o_ref[...] = _dot_nt(h_scr[...], w_ref[...]).astype(o_ref.dtype)


def _cast_plan(arr, n_i, n_j):
    rows, cols = arr.shape

    def pieces(extent, want, align):
        n = want
        while extent % n or (extent // n) % align:
            n -= 1
        return n

    if rows % n_j == 0 and (rows // n_j) % 16 == 0:
        nr, nc = n_j, pieces(cols, n_i, LANE)
        index = lambda i, j, nc=nc: (j, jnp.minimum(i, nc - 1))
    else:
        nr, nc = pieces(rows, n_i, 16), pieces(cols, n_j, LANE)
        index = lambda i, j, nr=nr, nc=nc: (jnp.minimum(i, nr - 1), jnp.minimum(j, nc - 1))
    return (rows // nr, cols // nc), index


def _inproj(x2d, norm_w, mod3, mod_row_of_tile, w_main, w_ga, cos, sin, *, tm, tn, ncols, rope, seq_tiles,
            casts=()):
    m, d = x2d.shape
    rope_tiles = ()
    if rope:
        assert tn == RET_QK and _DST["rk"] % tn == 0 and _DST["rq"] % tn == 0
        rope_tiles = (_DST["rk"] // tn, _DST["rq"] // tn)
    grid = (m // tm, ncols // tn)
    cast_specs = [pl.BlockSpec(*_cast_plan(a, *grid)) for a in casts]
    kern = functools.partial(_inproj_kernel, rope_tiles=rope_tiles, tn=tn, n_cast=len(casts))
    return pl.pallas_call(
        kern,
        out_shape=(jax.ShapeDtypeStruct((m, ncols), BF16),
                   jax.ShapeDtypeStruct((m, LANE), BF16),
                   *[jax.ShapeDtypeStruct(a.shape, BF16) for a in casts]),
        grid=grid,
        in_specs=[
            pl.BlockSpec((tm, d), lambda i, j: (i, 0)),
            pl.BlockSpec((1, d), lambda i, j: (0, 0)),
            pl.BlockSpec((None, 1, d), lambda i, j: (mod_row_of_tile(i), 0, 0)),
            pl.BlockSpec((None, 1, d), lambda i, j: (mod_row_of_tile(i), 0, 1)),
            pl.BlockSpec((tn, d), lambda i, j: (j, 0)),
            pl.BlockSpec((LANE, d), lambda i, j: (0, 0)),
            pl.BlockSpec((tm, RET_DK // 2), lambda i, j: (i % seq_tiles, 0)),
            pl.BlockSpec((tm, RET_DK // 2), lambda i, j: (i % seq_tiles, 0)),
            *cast_specs,
        ],
        out_specs=(pl.BlockSpec((tm, tn), lambda i, j: (i, j)),
                   pl.BlockSpec((tm, LANE), lambda i, j: (i, 0)),
                   *cast_specs),
        scratch_shapes=[pltpu.VMEM((tm, d), BF16)],
        compiler_params=pltpu.CompilerParams(dimension_semantics=("arbitrary", "arbitrary"),
                                             vmem_limit_bytes=VMEM_LIMIT),
        name="inproj",
    )(x2d, norm_w, mod3, mod3, w_main, w_ga, cos, sin, *casts)


RET_CHUNK = 256


def _ride_along_casts(rest, n_cast):
    for src, dst in zip(rest[:n_cast], rest[n_cast + 1:2 * n_cast + 1]):
        dst[...] = src[...].astype(dst.dtype)
    return rest[n_cast], rest[2 * n_cast + 1:]


def _ret_kernel(q_ref, k_ref, v_ref, g_ref, kc_ref, vc_ref, rd_ref, gn_ref, *rest, n_chunks, n_cast):
    o_ref, (o_scr, sc_scr, s_scr, dm_scr, eq_scr, wk_scr, sd_scr) = _ride_along_casts(rest, n_cast)
    c_len = RET_CHUNK
    ii = lax.broadcasted_iota(jnp.int32, (c_len, c_len), 0)
    jj = lax.broadcasted_iota(jnp.int32, (c_len, c_len), 1)
    rowi = lax.broadcasted_iota(jnp.int32, (c_len, RET_DK), 0).astype(F32)
    scale = RET_DK ** -0.5

    lg_f = -jnp.exp(rd_ref[0][0:1, :])
    lg_b = -jnp.exp(rd_ref[1][0:1, :])
    dm_scr[...] = jnp.where(jj <= ii, jnp.exp((ii - jj).astype(F32) * lg_f),
                            jnp.exp((jj - ii).astype(F32) * lg_b))
    for direction, lg in ((0, lg_f), (1, lg_b)):
        if direction == 0:
            eq_scr[direction] = jnp.exp((rowi + 1.0) * lg)
            wk = jnp.exp((c_len - 1.0 - rowi) * lg)
        else:
            eq_scr[direction] = jnp.exp((c_len - rowi) * lg)
            wk = jnp.exp(rowi * lg)
        wk_scr[direction] = wk
        sd_scr[direction] = jnp.broadcast_to(jnp.exp(float(c_len) * lg), (8, RET_DV))
        kcw = (kc_ref[...].astype(F32) * wk).astype(BF16)
        s_scr[direction] = _dot_tn(kcw, vc_ref[...])

    def rows_of(c):
        return pl.ds(pl.multiple_of(c * c_len, c_len), c_len)

    def score_stage(c):
        rows = rows_of(c)
        s_qk = _dot_nt(q_ref[rows, :], k_ref[rows, :])
        yield
        sc_scr[rows, :] = (s_qk * dm_scr[...]).astype(BF16)
        yield

    def value_stage(c):
        rows = rows_of(c)
        o_scr[rows, :] = _dot(sc_scr[rows, :], v_ref[rows, :])
        yield

    _run_staged(score_stage(0))

    def intra_body(t, carry):
        _run_staged(score_stage(t + 1), value_stage(t))
        return carry

    lax.fori_loop(0, n_chunks - 1, intra_body, 0, unroll=n_chunks - 1)
    _run_staged(value_stage(n_chunks - 1))

    def scan_chunk(direction, c, done):
        rows = rows_of(c)
        q = q_ref[rows, :]
        k = k_ref[rows, :]
        v = v_ref[rows, :]
        s_state = s_scr[direction]
        kv = _dot_tn((k.astype(F32) * wk_scr[direction]).astype(BF16), v)
        o_state = _dot(q, s_state.astype(BF16))
        yield
        s_scr[direction] = sd_scr[direction][0:1, :] * s_state + kv
        tot = o_scr[rows, :] + eq_scr[direction] * o_state
        if not done:
            o_scr[rows, :] = tot
        else:
            mu = jnp.mean(tot, axis=-1, keepdims=True)
            cen = tot - mu
            var = jnp.mean(cen * cen, axis=-1, keepdims=True)
            nrm = cen * (scale * lax.rsqrt(var * (scale * scale) + EPS))
            g = g_ref[rows, :]
            o_ref[rows, :] = (nrm * gn_ref[...] * (g * _sigmoid(g)).astype(F32)).astype(o_ref.dtype)
        yield

    def body(t, carry, done):
        _run_staged(scan_chunk(0, t, done), scan_chunk(1, n_chunks - 1 - t, done))
        return carry

    lax.fori_loop(0, n_chunks // 2, functools.partial(body, done=False), 0, unroll=n_chunks // 2)
    lax.fori_loop(n_chunks // 2, n_chunks, functools.partial(body, done=True), 0, unroll=n_chunks // 2)


def _retention(p, pc, rd_b, ret_gn, casts, *, batch, seq, ctx_len):
    assert ctx_len == RET_CHUNK and (seq // RET_CHUNK) % 2 == 0
    n_chunks = seq // RET_CHUNK
    w = RET_DK
    kern = functools.partial(_ret_kernel, n_chunks=n_chunks, n_cast=len(casts))
    cast_specs = [pl.BlockSpec(*_cast_plan(a, batch, RET_HEADS)) for a in casts]

    def col(name):
        return _DST[name] // w

    return pl.pallas_call(
        kern,
        out_shape=(jax.ShapeDtypeStruct((batch * seq, RET_VW), BF16),
                   *[jax.ShapeDtypeStruct(a.shape, BF16) for a in casts]),
        grid=(batch, RET_HEADS),
        in_specs=[
            pl.BlockSpec((seq, w), lambda b, h: (b, col("rq") + h)),
            pl.BlockSpec((seq, w), lambda b, h: (b, col("rk") + h)),
            pl.BlockSpec((seq, w), lambda b, h: (b, col("rv") + h)),
            pl.BlockSpec((seq, w), lambda b, h: (b, col("rg") + h)),
            pl.BlockSpec((ctx_len, w), lambda b, h: (b, col("rk") + h)),
            pl.BlockSpec((ctx_len, w), lambda b, h: (b, col("rv") + h)),
            pl.BlockSpec((2, None, 8, w), lambda b, h: (0, h, 0, 0)),
            pl.BlockSpec((1, w), lambda b, h: (0, h)),
            *cast_specs,
        ],
        out_specs=(pl.BlockSpec((seq, w), lambda b, h: (b, h)), *cast_specs),
        scratch_shapes=[
            pltpu.VMEM((seq, RET_DV), F32),
            pltpu.VMEM((seq, RET_CHUNK), BF16),
            pltpu.VMEM((2, RET_DK, RET_DV), F32),
            pltpu.VMEM((RET_CHUNK, RET_CHUNK), F32),
            pltpu.VMEM((2, RET_CHUNK, RET_DV), F32),
            pltpu.VMEM((2, RET_CHUNK, RET_DK), F32),
            pltpu.VMEM((2, 8, RET_DV), F32),
        ],
        compiler_params=pltpu.CompilerParams(dimension_semantics=("arbitrary", "arbitrary"),
                                             vmem_limit_bytes=VMEM_LIMIT),
        name="retention",
    )(p, p, p, p, pc, pc, rd_b, ret_gn, *casts)


GLA_SUB = 64
GLA_BLK = 256
GLA_SCAN_BLK = 512
PREP_UNROLL = 14


def _gla_kernel(q_ref, k_ref, v_ref, g_ref, ga_ref, kc_ref, vc_ref, gac_ref, up_ref, bias_ref, gn_ref,
                *rest, n_blk, n_cast):
    o_ref, scratch = _ride_along_casts(rest, n_cast)
    (ahi_scr, alo_scr, qf_scr, qb_scr, ktf_scr, ktb_scr, khf_scr, khb_scr, khcf_scr, khcb_scr,
     dec_scr, decc_scr, o_scr, s_scr) = scratch
    blk, sub, dk = GLA_BLK, GLA_SUB, GLA_DK
    nsub = blk // sub
    shift = sub.bit_length() - 1
    scale = GLA_DK ** -0.5

    def masks(n):
        ii = lax.broadcasted_iota(jnp.int32, (n, n), 0)
        jj = lax.broadcasted_iota(jnp.int32, (n, n), 1)
        same = (ii >> shift) == (jj >> shift)
        lower = jj <= ii
        return same, lower, jnp.where(same & lower, 1.0, 0.0).astype(BF16)

    ctx_rows = gac_ref.shape[0]
    blk_masks = masks(blk)
    ctx_masks = blk_masks if ctx_rows == blk else masks(ctx_rows)
    r8 = lax.broadcasted_iota(jnp.int32, (8, dk), 0)

    u = up_ref[...]
    u_hi, u_lo = _split_bf16(u)
    urow = lax.broadcasted_iota(jnp.int32, u.shape, 0)
    rhs = jnp.where(urow >= 2 * GA_W, u_lo, u_hi)
    bias = bias_ref[...]

    def gate_stage(ga_blk, put):
        z = _dot(ga_blk, rhs) + bias
        yield
        a = (jnp.minimum(z, 0.0) - jnp.log(1.0 + jnp.exp(-jnp.abs(z)))) * (1.0 / GLA_GATE_NORM)
        put(*_split_bf16(a))
        yield

    def decay_stage(a_hi, a_lo, k_blk, q_blk, msk, store):
        same, lower, tmat = msk
        nsub = a_hi.shape[0] // sub
        assert nsub <= 8
        pre = _dot(tmat, a_hi) + _dot(tmat, a_lo)
        yield
        a = a_hi.astype(F32) + a_lo.astype(F32)
        lasts = [pre[i * sub + sub - 1:i * sub + sub, :] for i in range(nsub)]
        tot = jnp.concatenate([jnp.broadcast_to(l, (sub, 2 * dk)) for l in lasts], axis=0)
        b_f = pre[:, :dk]
        ex_b = pre[:, dk:] - a[:, dk:]
        b_b = tot[:, dk:] - ex_b
        kf = k_blk.astype(F32)
        khf = (kf * jnp.exp(tot[:, :dk] - b_f)).astype(BF16)
        khb = (kf * jnp.exp(ex_b)).astype(BF16)

        def tile8(half):
            rows = [jnp.broadcast_to(l[:, half * dk:(half + 1) * dk], (8, dk)) for l in lasts]
            t = rows[nsub - 1]
            for i in range(nsub - 2, -1, -1):
                t = jnp.where(r8 == i, rows[i], t)
            return t

        dec = jnp.exp(jnp.concatenate([tile8(0), tile8(1), jnp.zeros((LANE - 16, dk), F32)], axis=0))
        out = dict(khf=khf, khb=khb, dec_t=dec.T)
        if q_blk is not None:
            qf32 = q_blk.astype(F32)
            out.update(qf=(qf32 * jnp.exp(b_f)).astype(BF16), qb=(qf32 * jnp.exp(b_b)).astype(BF16),
                       ktf=(kf * jnp.exp(-b_f)).astype(BF16), ktb=(kf * jnp.exp(-b_b)).astype(BF16))
        store(**out)
        yield

    def score_stage(r):
        same, lower, _ = blk_masks
        rows = pl.ds(r, blk)
        s_f = _dot_nt(qf_scr[rows, :], ktf_scr[rows, :])
        s_b = _dot_nt(qb_scr[rows, :], ktb_scr[rows, :])
        yield
        s = jnp.where(same, jnp.where(lower, s_f, s_b), 0.0).astype(BF16)
        yield
        o_scr[rows, :] = _dot(s, v_ref[rows, :])
        yield

    run = _run_staged

    def store_ctx(khf, khb, dec_t):
        khcf_scr[...] = khf
        khcb_scr[...] = khb
        decc_scr[...] = dec_t

    ctx_gate = []
    run(gate_stage(gac_ref[...], lambda hi, lo: ctx_gate.extend((hi, lo))))
    run(decay_stage(*ctx_gate, kc_ref[...], None, ctx_masks, store_ctx))

    def gate_block(t):
        rows = pl.ds(pl.multiple_of(t * blk, blk), blk)

        def put(a_hi, a_lo):
            ahi_scr[rows, :] = a_hi
            alo_scr[rows, :] = a_lo

        return gate_stage(ga_ref[rows, :], put)

    def decay_block(t):
        rows = pl.ds(pl.multiple_of(t * blk, blk), blk)

        def store(khf, khb, dec_t, qf, qb, ktf, ktb):
            khf_scr[rows, :] = khf
            khb_scr[rows, :] = khb
            qf_scr[rows, :] = qf
            qb_scr[rows, :] = qb
            ktf_scr[rows, :] = ktf
            ktb_scr[rows, :] = ktb
            dec_scr[t] = dec_t

        return decay_stage(ahi_scr[rows, :], alo_scr[rows, :], k_ref[rows, :], q_ref[rows, :],
                           blk_masks, store)

    def score_block(t):
        return score_stage(pl.multiple_of(t * blk, blk))

    run(gate_block(0))
    run(gate_block(1), decay_block(0))

    def prep_body(t, carry):
        run(gate_block(t + 2), decay_block(t + 1), score_block(t))
        return carry

    lax.fori_loop(0, n_blk - 2, prep_body, 0, unroll=PREP_UNROLL)
    run(decay_block(n_blk - 1), score_block(n_blk - 2))
    run(score_block(n_blk - 1))

    def scan_blk(direction, n_sub, dec_ts, kh_at, v_at, q_at=None, o_rows=None):
        s = s_scr[direction]
        order = range(n_sub) if direction == 0 else range(n_sub - 1, -1, -1)
        kv = {i: _dot_tn(kh_at(i), v_at(i)) for i in order}
        yield
        for i in order:
            c = direction * 8 + i % nsub
            col = dec_ts[i // nsub][:, c:c + 1]
            if q_at is not None:
                rows = o_rows(i)
                o_scr[rows, :] += _dot(q_at(i), s.astype(BF16))
            s = s * col + kv[i]
            yield
        s_scr[direction] = s

    s_scr[...] = jnp.zeros(s_scr.shape, F32)
    run(*[scan_blk(direction, ctx_rows // sub, [decc_scr[...]],
                   lambda i, khc=khc: khc[i * sub:(i + 1) * sub, :],
                   lambda i: vc_ref[i * sub:(i + 1) * sub, :])
          for direction, khc in ((0, khcf_scr), (1, khcb_scr))])

    sblk = GLA_SCAN_BLK
    per = sblk // blk
    n_sblk = n_blk // per

    fin_rows = 128

    def finalize(t):
        for tb in (t, n_sblk - 1 - t):
            for r0 in range(0, sblk, fin_rows):
                rows = pl.ds(pl.multiple_of(tb * sblk + r0, fin_rows), fin_rows)
                o = o_scr[rows, :]
                ms = jnp.mean(o * o, axis=-1, keepdims=True)
                nrm = o * (scale * lax.rsqrt(ms * (scale * scale) + EPS))
                g = g_ref[rows, :]
                o_ref[rows, :] = (nrm * gn_ref[...] * (g * _sigmoid(g)).astype(F32)).astype(o_ref.dtype)
                yield

    def scans(t):
        for direction, q_scr, kh_scr in ((0, qf_scr, khf_scr), (1, qb_scr, khb_scr)):
            tb = t if direction == 0 else n_sblk - 1 - t

            def rows(i, tb=tb):
                return pl.ds(pl.multiple_of(tb * sblk + i * sub, sub), sub)

            yield from scan_blk(direction, sblk // sub, [dec_scr[tb * per + j] for j in range(per)],
                                lambda i, kh_scr=kh_scr, rows=rows: kh_scr[rows(i), :],
                                lambda i, rows=rows: v_ref[rows(i), :],
                                lambda i, q_scr=q_scr, rows=rows: q_scr[rows(i), :],
                                rows)

    def scan_body(t, carry, fin_prev):
        if fin_prev:
            run(scans(t), finalize(t - 1))
        else:
            run(scans(t))
        return carry

    half = n_sblk // 2
    lax.fori_loop(0, half + 1, functools.partial(scan_body, fin_prev=False), 0, unroll=half + 1)
    lax.fori_loop(half + 1, n_sblk, functools.partial(scan_body, fin_prev=True), 0, unroll=n_sblk - half - 1)
    run(finalize(n_sblk - 1))


def _gla(p, ga, pc, gac, up_stack, bias2, gla_gn, casts, *, batch, seq, ctx_len):
    assert ctx_len % GLA_SUB == 0 and ctx_len <= GLA_BLK and GLA_SCAN_BLK % GLA_BLK == 0
    assert (seq // GLA_SCAN_BLK) % 2 == 0
    n_blk = seq // GLA_BLK
    kern = functools.partial(_gla_kernel, n_blk=n_blk, n_cast=len(casts))
    dk, dv = GLA_DK, GLA_DV
    cast_specs = [pl.BlockSpec(*_cast_plan(a, batch, GLA_HEADS)) for a in casts]
    return pl.pallas_call(
        kern,
        out_shape=(jax.ShapeDtypeStruct((batch * seq, GLA_VW), BF16),
                   *[jax.ShapeDtypeStruct(a.shape, BF16) for a in casts]),
        grid=(batch, GLA_HEADS),
        in_specs=[
            pl.BlockSpec((seq, dk), lambda b, h: (b, _DST["gq"] // dk + h)),
            pl.BlockSpec((seq, dk), lambda b, h: (b, _DST["gk"] // dk + h)),
            pl.BlockSpec((seq, dv), lambda b, h: (b, _DST["gv"] // dv + h)),
            pl.BlockSpec((seq, dv), lambda b, h: (b, _DST["gg"] // dv + h)),
            pl.BlockSpec((seq, LANE), lambda b, h: (b, 0)),
            pl.BlockSpec((ctx_len, dk), lambda b, h: (b, _DST["gk"] // dk + h)),
            pl.BlockSpec((ctx_len, dv), lambda b, h: (b, _DST["gv"] // dv + h)),
            pl.BlockSpec((ctx_len, LANE), lambda b, h: (b, 0)),
            pl.BlockSpec((LANE, 2 * dk), lambda b, h: (0, h)),
            pl.BlockSpec((1, 2 * dk), lambda b, h: (0, h)),
            pl.BlockSpec((1, dv), lambda b, h: (0, h)),
            *cast_specs,
        ],
        out_specs=(pl.BlockSpec((seq, dv), lambda b, h: (b, h)), *cast_specs),
        scratch_shapes=[
            pltpu.VMEM((seq, 2 * dk), BF16), pltpu.VMEM((seq, 2 * dk), BF16),
            pltpu.VMEM((seq, dk), BF16), pltpu.VMEM((seq, dk), BF16),
            pltpu.VMEM((seq, dk), BF16), pltpu.VMEM((seq, dk), BF16),
            pltpu.VMEM((seq, dk), BF16), pltpu.VMEM((seq, dk), BF16),
            pltpu.VMEM((ctx_len, dk), BF16), pltpu.VMEM((ctx_len, dk), BF16),
            pltpu.VMEM((n_blk, dk, LANE), F32), pltpu.VMEM((dk, LANE), F32),
            pltpu.VMEM((seq, dv), F32),
            pltpu.VMEM((2, dk, dv), F32),
        ],
        compiler_params=pltpu.CompilerParams(dimension_semantics=("arbitrary", "arbitrary"),
                                             vmem_limit_bytes=VMEM_LIMIT),
        name="gla",
    )(p, p, p, p, ga, pc, pc, gac, up_stack, bias2, gla_gn, *casts)


MXU_N = 256


POST_ROWS = 128


def _post_kernel(or_ref, og_ref, ga0_ref, ga1_ref, gb0_ref, gb1_ref, x_ref, wur_ref, wug_ref, wo_ref,
                 g1_ref, sh2_ref, sc2_ref, npost_ref, npre_ref, wsrc_ref, hs_ref, h2_ref, wdst_ref):
    wdst_ref[...] = wsrc_ref[...].astype(wdst_ref.dtype)
    half = ga0_ref.shape[1]
    tm = hs_ref.shape[0]
    groups = [slice(r0, r0 + POST_ROWS) for r0 in range(0, tm, POST_ROWS)]

    def up(rows):
        return _dot(or_ref[rows, :], wur_ref[...]), _dot(og_ref[rows, :], wug_ref[...])

    def out(rows, y_ret, y_gla):
        def merge(ga_r, gb_r, lo):
            return (_sigmoid(ga_r[rows, :].astype(F32)) * y_ret[:, lo:lo + half]
                    + _sigmoid(gb_r[rows, :].astype(F32)) * y_gla[:, lo:lo + half]).astype(BF16)

        merged = jnp.concatenate([merge(ga0_ref, gb0_ref, 0), merge(ga1_ref, gb1_ref, half)], axis=1)
        return _dot(merged, wo_ref[...])

    post_gain = g1_ref[...] * npost_ref[...]
    pre_gain = npre_ref[...] * (1.0 + sc2_ref[...])

    def finish(rows, y):
        hs = x_ref[rows, :] + _rms(y) * post_gain
        hs_ref[rows, :] = hs
        h2 = _rms(hs) * pre_gain + sh2_ref[...]
        h2_ref[rows, :] = h2.astype(h2_ref.dtype)

    ups = [up(rows) for rows in groups]
    ys = [out(rows, *u) for rows, u in zip(groups, ups)]
    for rows, y in zip(groups, ys):
        finish(rows, y)


def _post(o_r, o_g, p, x2d, w_up_ret, w_up_gla, w_out, mod3, npost, npre, w_later, *, tm, seq):
    m, d = x2d.shape
    tiles_per_seq = seq // tm
    const = dict(pipeline_mode=pl.Buffered(1))
    gw = d // 2
    n_steps = m // tm
    assert w_later.shape[0] % (16 * n_steps) == 0
    cast_spec = pl.BlockSpec((w_later.shape[0] // n_steps, w_later.shape[1]), lambda i: (i, 0))

    def modspec(chunk):
        return pl.BlockSpec((None, 1, d), lambda i: (i // tiles_per_seq, 0, chunk))

    def gatespec(name, part):
        return pl.BlockSpec((tm, gw), lambda i: (i, _DST[name] // gw + part))

    return pl.pallas_call(
        _post_kernel,
        out_shape=(jax.ShapeDtypeStruct((m, d), F32), jax.ShapeDtypeStruct((m, d), BF16),
                   jax.ShapeDtypeStruct(w_later.shape, BF16)),
        grid=(n_steps,),
        in_specs=[
            pl.BlockSpec((tm, RET_VW), lambda i: (i, 0)),
            pl.BlockSpec((tm, GLA_VW), lambda i: (i, 0)),
            gatespec("gate_a", 0), gatespec("gate_a", 1), gatespec("gate_b", 0), gatespec("gate_b", 1),
            pl.BlockSpec((tm, d), lambda i: (i, 0)),
            pl.BlockSpec((RET_VW, d), lambda i: (0, 0), **const),
            pl.BlockSpec((GLA_VW, d), lambda i: (0, 0), **const),
            pl.BlockSpec((d, d), lambda i: (0, 0), **const),
            modspec(2), modspec(3), modspec(4),
            pl.BlockSpec((1, d), lambda i: (0, 0)),
            pl.BlockSpec((1, d), lambda i: (0, 0)),
            cast_spec,
        ],
        out_specs=(pl.BlockSpec((tm, d), lambda i: (i, 0)),
                   pl.BlockSpec((tm, d), lambda i: (i, 0)),
                   cast_spec),
        compiler_params=pltpu.CompilerParams(dimension_semantics=("arbitrary",),
                                             vmem_limit_bytes=VMEM_LIMIT),
        name="post",
    )(o_r, o_g, p, p, p, p, x2d, w_up_ret, w_up_gla, w_out, mod3, mod3, mod3, npost, npre, w_later)


def _ffn_up_kernel(h_ref, wg_ref, wu_ref, wsrc_ref, a_ref, wdst_ref):
    wdst_ref[...] = wsrc_ref[...].astype(wdst_ref.dtype)
    h = h_ref[...]
    for lo in range(0, a_ref.shape[1], MXU_N):
        g = _dot(h, wg_ref[:, lo:lo + MXU_N])
        u = _dot(h, wu_ref[:, lo:lo + MXU_N])
        a_ref[:, lo:lo + MXU_N] = (g * _sigmoid(g) * u).astype(a_ref.dtype)


def _ffn_up(h2, wg, wu, w_later, *, tm, tf):
    m, d = h2.shape
    d_ff = wg.shape[1]
    grid = (m // tm, d_ff // tf)
    cast_spec = pl.BlockSpec(*_cast_plan(w_later, *grid))
    return pl.pallas_call(
        _ffn_up_kernel,
        out_shape=(jax.ShapeDtypeStruct((m, d_ff), BF16), jax.ShapeDtypeStruct(w_later.shape, BF16)),
        grid=grid,
        in_specs=[
            pl.BlockSpec((tm, d), lambda i, f: (i, 0)),
            pl.BlockSpec((d, tf), lambda i, f: (0, f)),
            pl.BlockSpec((d, tf), lambda i, f: (0, f)),
            cast_spec,
        ],
        out_specs=(pl.BlockSpec((tm, tf), lambda i, f: (i, f)), cast_spec),
        compiler_params=pltpu.CompilerParams(dimension_semantics=("arbitrary", "arbitrary"),
                                             vmem_limit_bytes=VMEM_LIMIT),
        name="ffn_up",
    )(h2, wg, wu, w_later)


def _ffn_down_kernel(a_ref, wd_ref, hs_ref, g2_ref, nw_ref, o_ref, y_scr, *, n_split):
    tm, d = o_ref.shape
    w = d // n_split
    ss = jnp.zeros((tm, 1), F32)
    for lo in range(0, d, w):
        y = _dot(a_ref[...], wd_ref[:, lo:lo + w])
        y_scr[:, lo:lo + w] = y
        ss = ss + jnp.sum(y * y, axis=-1, keepdims=True)
    inv = lax.rsqrt(ss * (1.0 / d) + EPS)
    o_ref[...] = hs_ref[...] + y_scr[...] * inv * (g2_ref[...] * nw_ref[...])


def _ffn_down(a, hs, wd, mod3, npost, *, tm, seq):
    m, d = hs.shape
    d_ff = wd.shape[0]
    tiles_per_seq = seq // tm
    kern = functools.partial(_ffn_down_kernel, n_split=4)
    return pl.pallas_call(
        kern,
        out_shape=jax.ShapeDtypeStruct((m, d), F32),
        grid=(m // tm,),
        in_specs=[
            pl.BlockSpec((tm, d_ff), lambda i: (i, 0)),
            pl.BlockSpec((d_ff, d), lambda i: (0, 0), pipeline_mode=pl.Buffered(1)),
            pl.BlockSpec((tm, d), lambda i: (i, 0)),
            pl.BlockSpec((None, 1, d), lambda i: (i // tiles_per_seq, 0, 5)),
            pl.BlockSpec((1, d), lambda i: (0, 0)),
        ],
        out_specs=pl.BlockSpec((tm, d), lambda i: (i, 0)),
        scratch_shapes=[pltpu.VMEM((tm, d), F32)],
        compiler_params=pltpu.CompilerParams(dimension_semantics=("arbitrary",),
                                             vmem_limit_bytes=VMEM_LIMIT),
        name="ffn_down",
    )(a, wd, hs, mod3, npost)


def _rope_tables(seq):
    f32 = np.float32
    rows = seq // GRID_W
    pos_r = np.repeat(np.arange(rows, dtype=f32), GRID_W)
    pos_c = np.tile(np.arange(GRID_W, dtype=f32), rows)
    n_f = RET_DK // 4
    inv = (f32(ROPE_BASE) ** (-np.arange(n_f, dtype=f32) / f32(n_f))).astype(f32)
    ang = np.concatenate([pos_r[:, None] * inv, pos_c[:, None] * inv], axis=-1).astype(f32)
    return jnp.asarray(np.cos(ang), F32), jnp.asarray(np.sin(ang), F32)


def _gate_map_layout(gla_a_up, gla_a_bias):
    r, h, dk = GLA_LOW_RANK, GLA_HEADS, GLA_DK
    u = jnp.zeros((GA_W, h, 2, dk), F32)
    u = u.at[:r, :, 0, :].set(gla_a_up[0].reshape(r, h, dk))
    u = u.at[r:, :, 1, :].set(gla_a_up[1].reshape(r, h, dk))
    u = u.reshape(GA_W, h * 2 * dk)
    up_stack = jnp.concatenate([u, u, u, jnp.zeros((LANE - 3 * GA_W, h * 2 * dk), F32)], axis=0)
    bias2 = jnp.stack([gla_a_bias[0].reshape(h, dk), gla_a_bias[1].reshape(h, dk)], axis=1)
    return up_stack, bias2.reshape(1, h * 2 * dk)


def _layer(h_state, c_rows, ctx2d, w_mod, b_mod, norm_mix_pre, norm_mix_post, norm_ffn_pre, norm_ffn_post,
           w_in, ret_decay, gla_a_up, gla_a_bias, ret_gn, gla_gn, w_up_ret, w_up_gla, w_out,
           ffn_w_gate, ffn_w_up, ffn_w_down, cos, sin, *, batch, seq, ctx_len):
    d = D_MODEL
    ga0 = _SRC["ga"][0]
    w_t = w_in.T
    w_g = w_t[ga0:ga0 + GA_W]
    w_ga = jnp.concatenate([w_g, w_g, w_g, jnp.zeros((LANE - 3 * GA_W, d), F32)], axis=0).astype(BF16)
    up_stack, bias2 = _gate_map_layout(gla_a_up, gla_a_bias)
    rd_b = jnp.broadcast_to(ret_decay.reshape(2, RET_HEADS, 1, 1), (2, RET_HEADS, 8, RET_DK))

    w_main, mod = _prologue(w_t, c_rows, w_mod, b_mod)
    mod3 = mod.reshape(mod.shape[0], 1, 6 * d)
    nw_pre = norm_mix_pre.reshape(1, d)

    tm = 1024
    p, ga, wur, wug, wo = _inproj(
        h_state, nw_pre, mod3, lambda i: i // (seq // tm), w_main, w_ga, cos, sin,
        tm=tm, tn=1024, ncols=N_MAIN, rope=True, seq_tiles=seq // tm,
        casts=(w_up_ret, w_up_gla, w_out))
    pc, gac = _inproj(ctx2d, nw_pre, mod3, lambda i: batch, w_main, w_ga, cos, sin,
                      tm=batch * ctx_len, tn=N_CTX // 2, ncols=N_CTX, rope=False, seq_tiles=1)

    (o_r,) = _retention(p, pc, rd_b, ret_gn.reshape(1, RET_VW), (), batch=batch, seq=seq, ctx_len=ctx_len)
    o_g, wu = _gla(p, ga, pc, gac, up_stack, bias2, gla_gn.reshape(1, GLA_VW), (ffn_w_up,),
                   batch=batch, seq=seq, ctx_len=ctx_len)

    hs, h2, wg = _post(o_r, o_g, p, h_state, wur, wug, wo,
                       mod3, norm_mix_post.reshape(1, d), norm_ffn_pre.reshape(1, d), ffn_w_gate,
                       tm=256, seq=seq)
    act, wd = _ffn_up(h2, wg, wu, ffn_w_down, tm=1024, tf=512)
    return _ffn_down(act, hs, wd, mod3, norm_ffn_post.reshape(1, d), tm=256, seq=seq)


def kernel(x, c, ctx, c_ctx, w_mod, b_mod, norm_mix_pre, norm_mix_post, norm_ffn_pre, norm_ffn_post,
           w_in, ret_decay, gla_a_up, gla_a_bias, ret_gn, gla_gn, w_up_ret, w_up_gla, w_out,
           ffn_w_gate, ffn_w_up, ffn_w_down):
    batch, seq, d = x.shape
    ctx_len = ctx.shape[1]
    depth = w_mod.shape[0]
    cos, sin = _rope_tables(seq)
    c_rows = jnp.zeros((16, d), F32).at[:batch].set(c.astype(F32)).at[batch].set(c_ctx.astype(F32))
    ctx2d = ctx.astype(F32).reshape(batch * ctx_len, d)
    h_state = x.astype(F32).reshape(batch * seq, d)
    for i in range(depth):
        h_state = _layer(h_state, c_rows, ctx2d, w_mod[i], b_mod[i], norm_mix_pre[i], norm_mix_post[i],
                         norm_ffn_pre[i], norm_ffn_post[i], w_in[i], ret_decay[i], gla_a_up[i], gla_a_bias[i],
                         ret_gn[i], gla_gn[i], w_up_ret[i], w_up_gla[i], w_out[i],
                         ffn_w_gate[i], ffn_w_up[i], ffn_w_down[i], cos, sin,
                         batch=batch, seq=seq, ctx_len=ctx_len)
    return h_state.reshape(batch, seq, d).astype(x.dtype)
```

```python
import functools

import jax
import jax.numpy as jnp
import numpy as np
from jax import lax
from jax.experimental import pallas as pl
from jax.experimental.pallas import tpu as pltpu

F32 = jnp.float32
BF16 = jnp.bfloat16

D_MODEL = 2048
GRID_W = 64
RET_HEADS = 4
RET_DK = 256
RET_DV = 256
GLA_HEADS = 4
GLA_DK = 128
GLA_DV = 256
GLA_LOW_RANK = 16
GLA_GATE_NORM = 16.0
ROPE_BASE = 10000.0
EPS = 1e-6

RET_QK = RET_HEADS * RET_DK
RET_VW = RET_HEADS * RET_DV
GLA_KW = GLA_HEADS * GLA_DK
GLA_VW = GLA_HEADS * GLA_DV
GA_W = 2 * GLA_LOW_RANK
LANE = 128

_SRC = {}
_off = 0
for _name, _w in (("rk", RET_QK), ("rv", RET_VW), ("gk", GLA_KW), ("gv", GLA_VW), ("ga", GA_W),
                  ("rq", RET_QK), ("rg", RET_VW), ("gq", GLA_KW), ("gg", GLA_VW),
                  ("gate_a", D_MODEL), ("gate_b", D_MODEL)):
    _SRC[_name] = (_off, _w)
    _off += _w

_ORDER = ("rk", "rv", "gv", "gk", "gq", "rq", "rg", "gg", "gate_a", "gate_b")
_DST = {}
_off = 0
for _name in _ORDER:
    _DST[_name] = _off
    _off += _SRC[_name][1]
N_MAIN = _off
N_CTX = _DST["gq"]
PACK_W = 512

VMEM_LIMIT = 56 * 1024 * 1024


def _dot(a, b):
    return jnp.dot(a, b, preferred_element_type=F32)


def _dot_nt(a, b):
    return lax.dot_general(a, b, (((1,), (1,)), ((), ())), preferred_element_type=F32)


def _dot_tn(a, b):
    return lax.dot_general(a, b, (((0,), (0,)), ((), ())), preferred_element_type=F32)


def _sigmoid(x):
    return 1.0 / (1.0 + jnp.exp(-x))


def _rms(x):
    return x * lax.rsqrt(jnp.mean(x * x, axis=-1, keepdims=True) + EPS)


def _run_staged(*stages):
    live = list(stages)
    while live:
        live = [g for g in live if next(g, live) is not live]


def _split_bf16(x):
    hi = x.astype(BF16)
    lo = (x - hi.astype(F32)).astype(BF16)
    return hi, lo


MOD_TN = 1024


def _mod_kernel(c_ref, wm_ref, bm_ref, mod_ref):
    cf = c_ref[...]
    rows = cf.shape[0]
    s_hi, s_lo = _split_bf16(cf * _sigmoid(cf))
    w_hi, w_lo = _split_bf16(wm_ref[...])
    r = _dot(jnp.concatenate([s_hi, s_lo], axis=0), w_hi)
    mod_ref[...] = r[:rows] + r[rows:] + _dot(s_hi, w_lo) + bm_ref[...]


def _modulation(c_rows, w_mod, b_mod):
    rows, d = c_rows.shape
    n = w_mod.shape[1]
    return pl.pallas_call(
        _mod_kernel,
        out_shape=jax.ShapeDtypeStruct((rows, n), F32),
        grid=(n // MOD_TN,),
        in_specs=[pl.BlockSpec((rows, d), lambda j: (0, 0)),
                  pl.BlockSpec((d, MOD_TN), lambda j: (0, j)),
                  pl.BlockSpec((1, MOD_TN), lambda j: (0, j))],
        out_specs=pl.BlockSpec((rows, MOD_TN), lambda j: (0, j)),
        compiler_params=pltpu.CompilerParams(dimension_semantics=("arbitrary",),
                                             vmem_limit_bytes=VMEM_LIMIT),
        name="mod",
    )(c_rows, w_mod, b_mod.reshape(1, n))


def _pack_tables():
    a_idx, n_idx, shift = [], [], []
    for name in _ORDER:
        src, width = _SRC[name]
        for c in range(src, src + width, PACK_W):
            base = c - c % PACK_W
            assert c - base in (0, GA_W)
            a_idx.append(base // PACK_W)
            n_idx.append((base + PACK_W) // GA_W if c != base else (n_idx[-1] if n_idx else 0))
            shift.append(c - base)
    return tuple(jnp.asarray(v, jnp.int32) for v in (a_idx, n_idx, shift))


def _inproj_kernel(x_ref, nw_ref, sh_ref, sc_ref, w_ref, wga_ref, cos_ref, sin_ref, *rest,
                   rope_tiles, tn, n_cast, n_alias=0):
    rest = rest[n_alias:]
    cast_in = rest[:n_cast]
    o_ref, ga_ref = rest[n_cast:n_cast + 2]
    cast_out = rest[n_cast + 2:2 * n_cast + 2]
    h_scr = rest[2 * n_cast + 2]

    def ride_along_casts():
        for src, dst in zip(cast_in, cast_out):
            dst[...] = src[...].astype(dst.dtype)

    _inproj_body(x_ref, nw_ref, sh_ref, sc_ref, w_ref, wga_ref, cos_ref, sin_ref, o_ref, ga_ref, h_scr,
                 ride_along_casts, rope_tiles, tn)


def _inproj_first_kernel(a_idx_ref, n_idx_ref, shift_ref, x_ref, nw_ref, sh_ref, sc_ref,
                         a1_ref, n1_ref, a2_ref, n2_ref, wga_ref, cos_ref, sin_ref,
                         o_ref, ga_ref, wm_ref, h_scr, w_scr, *, rope_tiles, tn):
    j = pl.program_id(1)
    for part, (a_ref, n_ref) in enumerate(((a1_ref, n1_ref), (a2_ref, n2_ref))):
        t = 2 * j + part
        rows = slice(part * PACK_W, (part + 1) * PACK_W)

        @pl.when(shift_ref[t] == 0)
        def _(a_ref=a_ref, rows=rows):
            w_scr[rows, :] = a_ref[...].astype(w_scr.dtype)

        @pl.when(shift_ref[t] != 0)
        def _(a_ref=a_ref, n_ref=n_ref, rows=rows):
            w_scr[rows, :] = jnp.concatenate([a_ref[GA_W:, :], n_ref[...]], axis=0).astype(w_scr.dtype)

    wm_ref[...] = w_scr[...]
    _inproj_body(x_ref, nw_ref, sh_ref, sc_ref, w_scr, wga_ref, cos_ref, sin_ref, o_ref, ga_ref, h_scr,
                 lambda: None, rope_tiles, tn)


def _inproj_body(x_ref, nw_ref, sh_ref, sc_ref, w_ref, wga_ref, cos_ref, sin_ref, o_ref, ga_ref, h_scr,
                 ride_along_casts, rope_tiles, tn):
    j = pl.program_id(1)

    @pl.when(j == 0)
    def _():
        h = _rms(x_ref[...]) * (nw_ref[...] * (1.0 + sc_ref[...])) + sh_ref[...]
        hb = h.astype(BF16)
        h_scr[...] = hb
        g3 = _dot_nt(hb, wga_ref[...])
        lane = lax.broadcasted_iota(jnp.int32, g3.shape, 1)
        resid = g3 - g3.astype(BF16).astype(F32)
        ga_ref[...] = jnp.where((lane >= GA_W) & (lane < 2 * GA_W), resid, g3).astype(ga_ref.dtype)

    if not rope_tiles:
        ride_along_casts()
        o_ref[...] = _dot_nt(h_scr[...], w_ref[...]).astype(o_ref.dtype)
        return

    is_rope = functools.reduce(jnp.logical_or, [j == t for t in rope_tiles])

    @pl.when(is_rope)
    def _():
        ride_along_casts()
        cos = cos_ref[...]
        sin = sin_ref[...]
        half = RET_DK // 2
        for hd in range(0, tn, RET_DK):
            acc = _dot_nt(h_scr[...], w_ref[hd:hd + RET_DK, :])
            t1 = acc[:, :half]
            t2 = acc[:, half:]
            o_ref[:, hd:hd + half] = (t1 * cos - t2 * sin).astype(o_ref.dtype)
            o_ref[:, hd + half:hd + RET_DK] = (t1 * sin + t2 * cos).astype(o_ref.dtype)

    @pl.when(jnp.logical_not(is_rope))
    def _():
        ride_along_casts()
        o_ref[...] = _dot_nt(h_scr[...], w_ref[...]).astype(o_ref.dtype)


def _cast_plan(arr, n_i, n_j):
    rows, cols = arr.shape

    def pieces(extent, want, align):
        n = want
        while extent % n or (extent // n) % align:
            n -= 1
        return n

    if rows % n_j == 0 and (rows // n_j) % 16 == 0:
        nr, nc = n_j, pieces(cols, n_i, LANE)
        index = lambda i, j, nc=nc: (jnp.where(i >= nc, n_j - 1, j), jnp.minimum(i, nc - 1))
    else:
        nr, nc = pieces(rows, n_i, 16), pieces(cols, n_j, LANE)
        index = lambda i, j, nr=nr, nc=nc: (jnp.minimum(i, nr - 1),
                                            jnp.where(i >= nr, nc - 1, jnp.minimum(j, nc - 1)))
    return (rows // nr, cols // nc), index


def _rope_tiles(tn):
    assert tn == RET_QK and _DST["rk"] % tn == 0 and _DST["rq"] % tn == 0
    return (_DST["rk"] // tn, _DST["rq"] // tn)


def _inproj_first(x2d, norm_w, mod3, w_t, w_ga, cos, sin, *, tm, tn):
    m, d = x2d.shape
    assert tn == 2 * PACK_W
    kern = functools.partial(_inproj_first_kernel, rope_tiles=_rope_tiles(tn), tn=tn)

    def src(part, table):
        return lambda i, j, a, n, s: ((a, n)[table][2 * j + part], 0)

    return pl.pallas_call(
        kern,
        out_shape=(jax.ShapeDtypeStruct((m, N_MAIN), BF16),
                   jax.ShapeDtypeStruct((m, LANE), BF16),
                   jax.ShapeDtypeStruct((N_MAIN, d), BF16)),
        grid_spec=pltpu.PrefetchScalarGridSpec(
            num_scalar_prefetch=3,
            grid=(1, N_MAIN // tn),
            in_specs=[
                pl.BlockSpec((tm, d), lambda i, j, *_: (0, 0)),
                pl.BlockSpec((1, d), lambda i, j, *_: (0, 0)),
                pl.BlockSpec((None, 1, d), lambda i, j, *_: (0, 0, 0)),
                pl.BlockSpec((None, 1, d), lambda i, j, *_: (0, 0, 1)),
                pl.BlockSpec((PACK_W, d), src(0, 0)),
                pl.BlockSpec((GA_W, d), src(0, 1)),
                pl.BlockSpec((PACK_W, d), src(1, 0)),
                pl.BlockSpec((GA_W, d), src(1, 1)),
                pl.BlockSpec((LANE, d), lambda i, j, *_: (0, 0)),
                pl.BlockSpec((tm, RET_DK // 2), lambda i, j, *_: (0, 0)),
                pl.BlockSpec((tm, RET_DK // 2), lambda i, j, *_: (0, 0)),
            ],
            out_specs=(pl.BlockSpec((tm, tn), lambda i, j, *_: (0, j)),
                       pl.BlockSpec((tm, LANE), lambda i, j, *_: (0, 0)),
                       pl.BlockSpec((tn, d), lambda i, j, *_: (j, 0))),
            scratch_shapes=[pltpu.VMEM((tm, d), BF16), pltpu.VMEM((tn, d), BF16)],
        ),
        compiler_params=pltpu.CompilerParams(dimension_semantics=("arbitrary", "arbitrary"),
                                             vmem_limit_bytes=VMEM_LIMIT),
        name="inproj_first",
    )(*_pack_tables(), x2d, norm_w, mod3, mod3, w_t, w_t, w_t, w_t, w_ga, cos, sin)


def _inproj(x2d, norm_w, mod3, mod_row_of_tile, w_main, w_ga, cos, sin, *, tm, tn, ncols, rope, seq_tiles,
            casts=(), row_tile0=0, write_into=()):
    m, d = x2d.shape
    rope_tiles = _rope_tiles(tn) if rope else ()
    grid = (m // tm - row_tile0, ncols // tn)
    cast_specs = [pl.BlockSpec(*_cast_plan(a, *grid)) for a in casts]
    kern = functools.partial(_inproj_kernel, rope_tiles=rope_tiles, tn=tn, n_cast=len(casts),
                             n_alias=len(write_into))
    row = lambda i: i + row_tile0
    n_fixed = 8
    return pl.pallas_call(
        kern,
        out_shape=(jax.ShapeDtypeStruct((m, ncols), BF16),
                   jax.ShapeDtypeStruct((m, LANE), BF16),
                   *[jax.ShapeDtypeStruct(a.shape, BF16) for a in casts]),
        grid=grid,
        in_specs=[
            pl.BlockSpec((tm, d), lambda i, j: (row(i), 0)),
            pl.BlockSpec((1, d), lambda i, j: (0, 0)),
            pl.BlockSpec((None, 1, d), lambda i, j: (mod_row_of_tile(row(i)), 0, 0)),
            pl.BlockSpec((None, 1, d), lambda i, j: (mod_row_of_tile(row(i)), 0, 1)),
            pl.BlockSpec((tn, d), lambda i, j: (j, 0)),
            pl.BlockSpec((LANE, d), lambda i, j: (0, 0)),
            pl.BlockSpec((tm, RET_DK // 2), lambda i, j: (row(i) % seq_tiles, 0)),
            pl.BlockSpec((tm, RET_DK // 2), lambda i, j: (row(i) % seq_tiles, 0)),
            *[pl.BlockSpec(memory_space=pl.ANY) for _ in write_into],
            *cast_specs,
        ],
        out_specs=(pl.BlockSpec((tm, tn), lambda i, j: (row(i), j)),
                   pl.BlockSpec((tm, LANE), lambda i, j: (row(i), 0)),
                   *cast_specs),
        scratch_shapes=[pltpu.VMEM((tm, d), BF16)],
        input_output_aliases={n_fixed + k: k for k in range(len(write_into))},
        compiler_params=pltpu.CompilerParams(dimension_semantics=("arbitrary", "arbitrary"),
                                             vmem_limit_bytes=VMEM_LIMIT),
        name="inproj",
    )(x2d, norm_w, mod3, mod3, w_main, w_ga, cos, sin, *write_into, *casts)


RET_CHUNK = 256


def _ride_along_casts(rest, n_cast):
    for src, dst in zip(rest[:n_cast], rest[n_cast + 1:2 * n_cast + 1]):
        dst[...] = src[...].astype(dst.dtype)
    return rest[n_cast], rest[2 * n_cast + 1:]


def _ret_kernel(q_ref, k_ref, v_ref, g_ref, kc_ref, vc_ref, rd_ref, gn_ref, *rest, n_chunks, n_cast):
    o_ref, (o_scr, sc_scr, s_scr, dm_scr, eq_scr, wk_scr, sd_scr) = _ride_along_casts(rest, n_cast)
    c_len = RET_CHUNK
    ii = lax.broadcasted_iota(jnp.int32, (c_len, c_len), 0)
    jj = lax.broadcasted_iota(jnp.int32, (c_len, c_len), 1)
    rowi = lax.broadcasted_iota(jnp.int32, (c_len, RET_DK), 0).astype(F32)
    scale = RET_DK ** -0.5

    lg_f = -jnp.exp(rd_ref[0][0:1, :])
    lg_b = -jnp.exp(rd_ref[1][0:1, :])
    dm_scr[...] = jnp.where(jj <= ii, jnp.exp((ii - jj).astype(F32) * lg_f),
                            jnp.exp((jj - ii).astype(F32) * lg_b))
    for direction, lg in ((0, lg_f), (1, lg_b)):
        if direction == 0:
            eq_scr[direction] = jnp.exp((rowi + 1.0) * lg)
            wk = jnp.exp((c_len - 1.0 - rowi) * lg)
        else:
            eq_scr[direction] = jnp.exp((c_len - rowi) * lg)
            wk = jnp.exp(rowi * lg)
        wk_scr[direction] = wk
        sd_scr[direction] = jnp.broadcast_to(jnp.exp(float(c_len) * lg), (8, RET_DV))
        kcw = (kc_ref[...].astype(F32) * wk).astype(BF16)
        s_scr[direction] = _dot_tn(kcw, vc_ref[...])

    def rows_of(c):
        return pl.ds(pl.multiple_of(c * c_len, c_len), c_len)

    def score_stage(c):
        rows = rows_of(c)
        s_qk = _dot_nt(q_ref[rows, :], k_ref[rows, :])
        yield
        sc_scr[rows, :] = (s_qk * dm_scr[...]).astype(BF16)
        yield

    def value_stage(c):
        rows = rows_of(c)
        o_scr[rows, :] = _dot(sc_scr[rows, :], v_ref[rows, :])
        yield

    _run_staged(score_stage(0))

    def intra_body(t, carry):
        _run_staged(score_stage(t + 1), value_stage(t))
        return carry

    lax.fori_loop(0, n_chunks - 1, intra_body, 0, unroll=n_chunks - 1)
    _run_staged(value_stage(n_chunks - 1))

    def scan_chunk(direction, c, done):
        rows = rows_of(c)
        q = q_ref[rows, :]
        k = k_ref[rows, :]
        v = v_ref[rows, :]
        s_state = s_scr[direction]
        kv = _dot_tn((k.astype(F32) * wk_scr[direction]).astype(BF16), v)
        o_state = _dot(q, s_state.astype(BF16))
        yield
        s_scr[direction] = sd_scr[direction][0:1, :] * s_state + kv
        tot = o_scr[rows, :] + eq_scr[direction] * o_state
        if not done:
            o_scr[rows, :] = tot
        else:
            mu = jnp.mean(tot, axis=-1, keepdims=True)
            cen = tot - mu
            var = jnp.mean(cen * cen, axis=-1, keepdims=True)
            nrm = cen * (scale * lax.rsqrt(var * (scale * scale) + EPS))
            g = g_ref[rows, :]
            o_ref[rows, :] = (nrm * gn_ref[...] * (g * _sigmoid(g)).astype(F32)).astype(o_ref.dtype)
        yield

    def body(t, carry, done):
        _run_staged(scan_chunk(0, t, done), scan_chunk(1, n_chunks - 1 - t, done))
        return carry

    lax.fori_loop(0, n_chunks // 2, functools.partial(body, done=False), 0, unroll=n_chunks // 2)
    lax.fori_loop(n_chunks // 2, n_chunks, functools.partial(body, done=True), 0, unroll=n_chunks // 2)


def _retention(p, pc, rd_b, ret_gn, casts, *, batch, seq, ctx_len):
    assert ctx_len == RET_CHUNK and (seq // RET_CHUNK) % 2 == 0
    n_chunks = seq // RET_CHUNK
    w = RET_DK
    kern = functools.partial(_ret_kernel, n_chunks=n_chunks, n_cast=len(casts))
    cast_specs = [pl.BlockSpec(*_cast_plan(a, batch, RET_HEADS)) for a in casts]

    def col(name):
        return _DST[name] // w

    return pl.pallas_call(
        kern,
        out_shape=(jax.ShapeDtypeStruct((batch * seq, RET_VW), BF16),
                   *[jax.ShapeDtypeStruct(a.shape, BF16) for a in casts]),
        grid=(batch, RET_HEADS),
        in_specs=[
            pl.BlockSpec((seq, w), lambda b, h: (b, col("rq") + h)),
            pl.BlockSpec((seq, w), lambda b, h: (b, col("rk") + h)),
            pl.BlockSpec((seq, w), lambda b, h: (b, col("rv") + h)),
            pl.BlockSpec((seq, w), lambda b, h: (b, col("rg") + h)),
            pl.BlockSpec((ctx_len, w), lambda b, h: (b, col("rk") + h)),
            pl.BlockSpec((ctx_len, w), lambda b, h: (b, col("rv") + h)),
            pl.BlockSpec((2, None, 8, w), lambda b, h: (0, h, 0, 0)),
            pl.BlockSpec((1, w), lambda b, h: (0, h)),
            *cast_specs,
        ],
        out_specs=(pl.BlockSpec((seq, w), lambda b, h: (b, h)), *cast_specs),
        scratch_shapes=[
            pltpu.VMEM((seq, RET_DV), F32),
            pltpu.VMEM((seq, RET_CHUNK), BF16),
            pltpu.VMEM((2, RET_DK, RET_DV), F32),
            pltpu.VMEM((RET_CHUNK, RET_CHUNK), F32),
            pltpu.VMEM((2, RET_CHUNK, RET_DV), F32),
            pltpu.VMEM((2, RET_CHUNK, RET_DK), F32),
            pltpu.VMEM((2, 8, RET_DV), F32),
        ],
        compiler_params=pltpu.CompilerParams(dimension_semantics=("arbitrary", "arbitrary"),
                                             vmem_limit_bytes=VMEM_LIMIT),
        name="retention",
    )(p, p, p, p, pc, pc, rd_b, ret_gn, *casts)


GLA_SUB = 64
GLA_BLK = 256
GLA_SCAN_BLK = 512
PREP_UNROLL = 14


def _gla_kernel(q_ref, k_ref, v_ref, g_ref, ga_ref, kc_ref, vc_ref, gac_ref, up_ref, bias_ref, gn_ref,
                *rest, n_blk, n_cast):
    o_ref, scratch = _ride_along_casts(rest, n_cast)
    (ahi_scr, alo_scr, qf_scr, qb_scr, ktf_scr, ktb_scr, khf_scr, khb_scr, khcf_scr, khcb_scr,
     dec_scr, decc_scr, o_scr, s_scr) = scratch
    blk, sub, dk = GLA_BLK, GLA_SUB, GLA_DK
    nsub = blk // sub
    shift = sub.bit_length() - 1
    scale = GLA_DK ** -0.5

    def masks(n):
        ii = lax.broadcasted_iota(jnp.int32, (n, n), 0)
        jj = lax.broadcasted_iota(jnp.int32, (n, n), 1)
        same = (ii >> shift) == (jj >> shift)
        lower = jj <= ii
        return same, lower, jnp.where(same & lower, 1.0, 0.0).astype(BF16)

    ctx_rows = gac_ref.shape[0]
    blk_masks = masks(blk)
    ctx_masks = blk_masks if ctx_rows == blk else masks(ctx_rows)
    r8 = lax.broadcasted_iota(jnp.int32, (8, dk), 0)

    u = up_ref[...]
    u_hi, u_lo = _split_bf16(u)
    urow = lax.broadcasted_iota(jnp.int32, u.shape, 0)
    rhs = jnp.where(urow >= 2 * GA_W, u_lo, u_hi)
    bias = bias_ref[...]

    def gate_stage(ga_blk, put):
        z = _dot(ga_blk, rhs) + bias
        yield
        a = (jnp.minimum(z, 0.0) - jnp.log(1.0 + jnp.exp(-jnp.abs(z)))) * (1.0 / GLA_GATE_NORM)
        put(*_split_bf16(a))
        yield

    def decay_stage(a_hi, a_lo, k_blk, q_blk, msk, store):
        same, lower, tmat = msk
        nsub = a_hi.shape[0] // sub
        assert nsub <= 8
        pre = _dot(tmat, a_hi) + _dot(tmat, a_lo)
        yield
        a = a_hi.astype(F32) + a_lo.astype(F32)
        lasts = [pre[i * sub + sub - 1:i * sub + sub, :] for i in range(nsub)]
        tot = jnp.concatenate([jnp.broadcast_to(l, (sub, 2 * dk)) for l in lasts], axis=0)
        b_f = pre[:, :dk]
        ex_b = pre[:, dk:] - a[:, dk:]
        b_b = tot[:, dk:] - ex_b
        kf = k_blk.astype(F32)
        khf = (kf * jnp.exp(tot[:, :dk] - b_f)).astype(BF16)
        khb = (kf * jnp.exp(ex_b)).astype(BF16)

        def tile8(half):
            rows = [jnp.broadcast_to(l[:, half * dk:(half + 1) * dk], (8, dk)) for l in lasts]
            t = rows[nsub - 1]
            for i in range(nsub - 2, -1, -1):
                t = jnp.where(r8 == i, rows[i], t)
            return t

        dec = jnp.exp(jnp.concatenate([tile8(0), tile8(1), jnp.zeros((LANE - 16, dk), F32)], axis=0))
        out = dict(khf=khf, khb=khb, dec_t=dec.T)
        if q_blk is not None:
            qf32 = q_blk.astype(F32)
            out.update(qf=(qf32 * jnp.exp(b_f)).astype(BF16), qb=(qf32 * jnp.exp(b_b)).astype(BF16),
                       ktf=(kf * jnp.exp(-b_f)).astype(BF16), ktb=(kf * jnp.exp(-b_b)).astype(BF16))
        store(**out)
        yield

    def score_stage(r):
        same, lower, _ = blk_masks
        rows = pl.ds(r, blk)
        s_f = _dot_nt(qf_scr[rows, :], ktf_scr[rows, :])
        s_b = _dot_nt(qb_scr[rows, :], ktb_scr[rows, :])
        yield
        s = jnp.where(same, jnp.where(lower, s_f, s_b), 0.0).astype(BF16)
        yield
        o_scr[rows, :] = _dot(s, v_ref[rows, :])
        yield

    run = _run_staged

    def store_ctx(khf, khb, dec_t):
        khcf_scr[...] = khf
        khcb_scr[...] = khb
        decc_scr[...] = dec_t

    ctx_gate = []
    run(gate_stage(gac_ref[...], lambda hi, lo: ctx_gate.extend((hi, lo))))
    run(decay_stage(*ctx_gate, kc_ref[...], None, ctx_masks, store_ctx))

    def gate_block(t):
        rows = pl.ds(pl.multiple_of(t * blk, blk), blk)

        def put(a_hi, a_lo):
            ahi_scr[rows, :] = a_hi
            alo_scr[rows, :] = a_lo

        return gate_stage(ga_ref[rows, :], put)

    def decay_block(t):
        rows = pl.ds(pl.multiple_of(t * blk, blk), blk)

        def store(khf, khb, dec_t, qf, qb, ktf, ktb):
            khf_scr[rows, :] = khf
            khb_scr[rows, :] = khb
            qf_scr[rows, :] = qf
            qb_scr[rows, :] = qb
            ktf_scr[rows, :] = ktf
            ktb_scr[rows, :] = ktb
            dec_scr[t] = dec_t

        return decay_stage(ahi_scr[rows, :], alo_scr[rows, :], k_ref[rows, :], q_ref[rows, :],
                           blk_masks, store)

    def score_block(t):
        return score_stage(pl.multiple_of(t * blk, blk))

    run(gate_block(0))
    run(gate_block(1), decay_block(0))

    def prep_body(t, carry):
        run(gate_block(t + 2), decay_block(t + 1), score_block(t))
        return carry

    lax.fori_loop(0, n_blk - 2, prep_body, 0, unroll=PREP_UNROLL)
    run(decay_block(n_blk - 1), score_block(n_blk - 2))
    run(score_block(n_blk - 1))

    def scan_blk(direction, n_sub, dec_ts, kh_at, v_at, q_at=None, o_rows=None):
        s = s_scr[direction]
        order = range(n_sub) if direction == 0 else range(n_sub - 1, -1, -1)
        kv = {i: _dot_tn(kh_at(i), v_at(i)) for i in order}
        yield
        for i in order:
            c = direction * 8 + i % nsub
            col = dec_ts[i // nsub][:, c:c + 1]
            if q_at is not None:
                rows = o_rows(i)
                o_scr[rows, :] += _dot(q_at(i), s.astype(BF16))
            s = s * col + kv[i]
            yield
        s_scr[direction] = s

    s_scr[...] = jnp.zeros(s_scr.shape, F32)
    run(*[scan_blk(direction, ctx_rows // sub, [decc_scr[...]],
                   lambda i, khc=khc: khc[i * sub:(i + 1) * sub, :],
                   lambda i: vc_ref[i * sub:(i + 1) * sub, :])
          for direction, khc in ((0, khcf_scr), (1, khcb_scr))])

    sblk = GLA_SCAN_BLK
    per = sblk // blk
    n_sblk = n_blk // per

    fin_rows = 128

    def finalize(t):
        for tb in (t, n_sblk - 1 - t):
            for r0 in range(0, sblk, fin_rows):
                rows = pl.ds(pl.multiple_of(tb * sblk + r0, fin_rows), fin_rows)
                o = o_scr[rows, :]
                ms = jnp.mean(o * o, axis=-1, keepdims=True)
                nrm = o * (scale * lax.rsqrt(ms * (scale * scale) + EPS))
                g = g_ref[rows, :]
                o_ref[rows, :] = (nrm * gn_ref[...] * (g * _sigmoid(g)).astype(F32)).astype(o_ref.dtype)
                yield

    def scans(t):
        for direction, q_scr, kh_scr in ((0, qf_scr, khf_scr), (1, qb_scr, khb_scr)):
            tb = t if direction == 0 else n_sblk - 1 - t

            def rows(i, tb=tb):
                return pl.ds(pl.multiple_of(tb * sblk + i * sub, sub), sub)

            yield from scan_blk(direction, sblk // sub, [dec_scr[tb * per + j] for j in range(per)],
                                lambda i, kh_scr=kh_scr, rows=rows: kh_scr[rows(i), :],
                                lambda i, rows=rows: v_ref[rows(i), :],
                                lambda i, q_scr=q_scr, rows=rows: q_scr[rows(i), :],
                                rows)

    def scan_body(t, carry, fin_prev):
        if fin_prev:
            run(scans(t), finalize(t - 1))
        else:
            run(scans(t))
        return carry

    half = n_sblk // 2
    lax.fori_loop(0, half + 1, functools.partial(scan_body, fin_prev=False), 0, unroll=half + 1)
    lax.fori_loop(half + 1, n_sblk, functools.partial(scan_body, fin_prev=True), 0, unroll=n_sblk - half - 1)
    run(finalize(n_sblk - 1))


def _gla(p, ga, pc, gac, up_stack, bias2, gla_gn, casts, *, batch, seq, ctx_len):
    assert ctx_len % GLA_SUB == 0 and ctx_len <= GLA_BLK and GLA_SCAN_BLK % GLA_BLK == 0
    assert (seq // GLA_SCAN_BLK) % 2 == 0
    n_blk = seq // GLA_BLK
    kern = functools.partial(_gla_kernel, n_blk=n_blk, n_cast=len(casts))
    dk, dv = GLA_DK, GLA_DV
    cast_specs = [pl.BlockSpec(*_cast_plan(a, batch, GLA_HEADS)) for a in casts]
    return pl.pallas_call(
        kern,
        out_shape=(jax.ShapeDtypeStruct((batch * seq, GLA_VW), BF16),
                   *[jax.ShapeDtypeStruct(a.shape, BF16) for a in casts]),
        grid=(batch, GLA_HEADS),
        in_specs=[
            pl.BlockSpec((seq, dk), lambda b, h: (b, _DST["gq"] // dk + h)),
            pl.BlockSpec((seq, dk), lambda b, h: (b, _DST["gk"] // dk + h)),
            pl.BlockSpec((seq, dv), lambda b, h: (b, _DST["gv"] // dv + h)),
            pl.BlockSpec((seq, dv), lambda b, h: (b, _DST["gg"] // dv + h)),
            pl.BlockSpec((seq, LANE), lambda b, h: (b, 0)),
            pl.BlockSpec((ctx_len, dk), lambda b, h: (b, _DST["gk"] // dk + h)),
            pl.BlockSpec((ctx_len, dv), lambda b, h: (b, _DST["gv"] // dv + h)),
            pl.BlockSpec((ctx_len, LANE), lambda b, h: (b, 0)),
            pl.BlockSpec((LANE, 2 * dk), lambda b, h: (0, h)),
            pl.BlockSpec((1, 2 * dk), lambda b, h: (0, h)),
            pl.BlockSpec((1, dv), lambda b, h: (0, h)),
            *cast_specs,
        ],
        out_specs=(pl.BlockSpec((seq, dv), lambda b, h: (b, h)), *cast_specs),
        scratch_shapes=[
            pltpu.VMEM((seq, 2 * dk), BF16), pltpu.VMEM((seq, 2 * dk), BF16),
            pltpu.VMEM((seq, dk), BF16), pltpu.VMEM((seq, dk), BF16),
            pltpu.VMEM((seq, dk), BF16), pltpu.VMEM((seq, dk), BF16),
            pltpu.VMEM((seq, dk), BF16), pltpu.VMEM((seq, dk), BF16),
            pltpu.VMEM((ctx_len, dk), BF16), pltpu.VMEM((ctx_len, dk), BF16),
            pltpu.VMEM((n_blk, dk, LANE), F32), pltpu.VMEM((dk, LANE), F32),
            pltpu.VMEM((seq, dv), F32),
            pltpu.VMEM((2, dk, dv), F32),
        ],
        compiler_params=pltpu.CompilerParams(dimension_semantics=("arbitrary", "arbitrary"),
                                             vmem_limit_bytes=VMEM_LIMIT),
        name="gla",
    )(p, p, p, p, ga, pc, pc, gac, up_stack, bias2, gla_gn, *casts)


MXU_N = 256


POST_ROWS = 128


def _post_kernel(or_ref, og_ref, ga0_ref, ga1_ref, gb0_ref, gb1_ref, x_ref, wur_ref, wug_ref, wo_ref,
                 g1_ref, sh2_ref, sc2_ref, npost_ref, npre_ref, wsrc_ref, hs_ref, h2_ref, wdst_ref):
    wdst_ref[...] = wsrc_ref[...].astype(wdst_ref.dtype)
    half = ga0_ref.shape[1]
    tm = hs_ref.shape[0]
    groups = [slice(r0, r0 + POST_ROWS) for r0 in range(0, tm, POST_ROWS)]

    def up(rows):
        return _dot(or_ref[rows, :], wur_ref[...]), _dot(og_ref[rows, :], wug_ref[...])

    def out(rows, y_ret, y_gla):
        def merge(ga_r, gb_r, lo):
            return (_sigmoid(ga_r[rows, :].astype(F32)) * y_ret[:, lo:lo + half]
                    + _sigmoid(gb_r[rows, :].astype(F32)) * y_gla[:, lo:lo + half]).astype(BF16)

        merged = jnp.concatenate([merge(ga0_ref, gb0_ref, 0), merge(ga1_ref, gb1_ref, half)], axis=1)
        return _dot(merged, wo_ref[...])

    post_gain = g1_ref[...] * npost_ref[...]
    pre_gain = npre_ref[...] * (1.0 + sc2_ref[...])

    def finish(rows, y):
        hs = x_ref[rows, :] + _rms(y) * post_gain
        hs_ref[rows, :] = hs
        h2 = _rms(hs) * pre_gain + sh2_ref[...]
        h2_ref[rows, :] = h2.astype(h2_ref.dtype)

    ups = [up(rows) for rows in groups]
    ys = [out(rows, *u) for rows, u in zip(groups, ups)]
    for rows, y in zip(groups, ys):
        finish(rows, y)


def _post(o_r, o_g, p, x2d, w_up_ret, w_up_gla, w_out, mod3, npost, npre, w_later, *, tm, seq):
    m, d = x2d.shape
    tiles_per_seq = seq // tm
    const = dict(pipeline_mode=pl.Buffered(1))
    gw = d // 2
    n_steps = m // tm
    assert w_later.shape[0] % (16 * n_steps) == 0
    cast_spec = pl.BlockSpec((w_later.shape[0] // n_steps, w_later.shape[1]), lambda i: (i, 0))

    def modspec(chunk):
        return pl.BlockSpec((None, 1, d), lambda i: (i // tiles_per_seq, 0, chunk))

    def gatespec(name, part):
        return pl.BlockSpec((tm, gw), lambda i: (i, _DST[name] // gw + part))

    return pl.pallas_call(
        _post_kernel,
        out_shape=(jax.ShapeDtypeStruct((m, d), F32), jax.ShapeDtypeStruct((m, d), BF16),
                   jax.ShapeDtypeStruct(w_later.shape, BF16)),
        grid=(n_steps,),
        in_specs=[
            pl.BlockSpec((tm, RET_VW), lambda i: (i, 0)),
            pl.BlockSpec((tm, GLA_VW), lambda i: (i, 0)),
            gatespec("gate_a", 0), gatespec("gate_a", 1), gatespec("gate_b", 0), gatespec("gate_b", 1),
            pl.BlockSpec((tm, d), lambda i: (i, 0)),
            pl.BlockSpec((RET_VW, d), lambda i: (0, 0), **const),
            pl.BlockSpec((GLA_VW, d), lambda i: (0, 0), **const),
            pl.BlockSpec((d, d), lambda i: (0, 0), **const),
            modspec(2), modspec(3), modspec(4),
            pl.BlockSpec((1, d), lambda i: (0, 0)),
            pl.BlockSpec((1, d), lambda i: (0, 0)),
            cast_spec,
        ],
        out_specs=(pl.BlockSpec((tm, d), lambda i: (i, 0)),
                   pl.BlockSpec((tm, d), lambda i: (i, 0)),
                   cast_spec),
        compiler_params=pltpu.CompilerParams(dimension_semantics=("arbitrary",),
                                             vmem_limit_bytes=VMEM_LIMIT),
        name="post",
    )(o_r, o_g, p, p, p, p, x2d, w_up_ret, w_up_gla, w_out, mod3, mod3, mod3, npost, npre, w_later)


def _ffn_up_kernel(h_ref, wg_ref, wu_ref, wsrc_ref, a_ref, wdst_ref):
    wdst_ref[...] = wsrc_ref[...].astype(wdst_ref.dtype)
    h = h_ref[...]
    for lo in range(0, a_ref.shape[1], MXU_N):
        g = _dot(h, wg_ref[:, lo:lo + MXU_N])
        u = _dot(h, wu_ref[:, lo:lo + MXU_N])
        a_ref[:, lo:lo + MXU_N] = (g * _sigmoid(g) * u).astype(a_ref.dtype)


def _ffn_up(h2, wg, wu, w_later, *, tm, tf):
    m, d = h2.shape
    d_ff = wg.shape[1]
    grid = (m // tm, d_ff // tf)
    cast_spec = pl.BlockSpec(*_cast_plan(w_later, *grid))
    return pl.pallas_call(
        _ffn_up_kernel,
        out_shape=(jax.ShapeDtypeStruct((m, d_ff), BF16), jax.ShapeDtypeStruct(w_later.shape, BF16)),
        grid=grid,
        in_specs=[
            pl.BlockSpec((tm, d), lambda i, f: (i, 0)),
            pl.BlockSpec((d, tf), lambda i, f: (0, f)),
            pl.BlockSpec((d, tf), lambda i, f: (0, f)),
            cast_spec,
        ],
        out_specs=(pl.BlockSpec((tm, tf), lambda i, f: (i, f)), cast_spec),
        compiler_params=pltpu.CompilerParams(dimension_semantics=("arbitrary", "arbitrary"),
                                             vmem_limit_bytes=VMEM_LIMIT),
        name="ffn_up",
    )(h2, wg, wu, w_later)


def _ffn_down_kernel(a_ref, wd_ref, hs_ref, g2_ref, nw_ref, o_ref, y_scr, *, n_split):
    tm, d = o_ref.shape
    w = d // n_split
    ss = jnp.zeros((tm, 1), F32)
    for lo in range(0, d, w):
        y = _dot(a_ref[...], wd_ref[:, lo:lo + w])
        y_scr[:, lo:lo + w] = y
        ss = ss + jnp.sum(y * y, axis=-1, keepdims=True)
    inv = lax.rsqrt(ss * (1.0 / d) + EPS)
    o_ref[...] = hs_ref[...] + y_scr[...] * inv * (g2_ref[...] * nw_ref[...])


def _ffn_down(a, hs, wd, mod3, npost, *, tm, seq):
    m, d = hs.shape
    d_ff = wd.shape[0]
    tiles_per_seq = seq // tm
    kern = functools.partial(_ffn_down_kernel, n_split=4)
    return pl.pallas_call(
        kern,
        out_shape=jax.ShapeDtypeStruct((m, d), F32),
        grid=(m // tm,),
        in_specs=[
            pl.BlockSpec((tm, d_ff), lambda i: (i, 0)),
            pl.BlockSpec((d_ff, d), lambda i: (0, 0), pipeline_mode=pl.Buffered(1)),
            pl.BlockSpec((tm, d), lambda i: (i, 0)),
            pl.BlockSpec((None, 1, d), lambda i: (i // tiles_per_seq, 0, 5)),
            pl.BlockSpec((1, d), lambda i: (0, 0)),
        ],
        out_specs=pl.BlockSpec((tm, d), lambda i: (i, 0)),
        scratch_shapes=[pltpu.VMEM((tm, d), F32)],
        compiler_params=pltpu.CompilerParams(dimension_semantics=("arbitrary",),
                                             vmem_limit_bytes=VMEM_LIMIT),
        name="ffn_down",
    )(a, wd, hs, mod3, npost)


def _rope_tables(seq):
    f32 = np.float32
    rows = seq // GRID_W
    pos_r = np.repeat(np.arange(rows, dtype=f32), GRID_W)
    pos_c = np.tile(np.arange(GRID_W, dtype=f32), rows)
    n_f = RET_DK // 4
    inv = (f32(ROPE_BASE) ** (-np.arange(n_f, dtype=f32) / f32(n_f))).astype(f32)
    ang = np.concatenate([pos_r[:, None] * inv, pos_c[:, None] * inv], axis=-1).astype(f32)
    return jnp.asarray(np.cos(ang), F32), jnp.asarray(np.sin(ang), F32)


def _gate_map_layout(gla_a_up, gla_a_bias):
    r, h, dk = GLA_LOW_RANK, GLA_HEADS, GLA_DK
    u = jnp.zeros((GA_W, h, 2, dk), F32)
    u = u.at[:r, :, 0, :].set(gla_a_up[0].reshape(r, h, dk))
    u = u.at[r:, :, 1, :].set(gla_a_up[1].reshape(r, h, dk))
    u = u.reshape(GA_W, h * 2 * dk)
    up_stack = jnp.concatenate([u, u, u, jnp.zeros((LANE - 3 * GA_W, h * 2 * dk), F32)], axis=0)
    bias2 = jnp.stack([gla_a_bias[0].reshape(h, dk), gla_a_bias[1].reshape(h, dk)], axis=1)
    return up_stack, bias2.reshape(1, h * 2 * dk)


def _layer(h_state, c_rows, ctx2d, w_mod, b_mod, norm_mix_pre, norm_mix_post, norm_ffn_pre, norm_ffn_post,
           w_in, ret_decay, gla_a_up, gla_a_bias, ret_gn, gla_gn, w_up_ret, w_up_gla, w_out,
           ffn_w_gate, ffn_w_up, ffn_w_down, cos, sin, *, batch, seq, ctx_len):
    d = D_MODEL
    ga0 = _SRC["ga"][0]
    w_t = w_in.T
    w_g = w_t[ga0:ga0 + GA_W]
    w_ga = jnp.concatenate([w_g, w_g, w_g, jnp.zeros((LANE - 3 * GA_W, d), F32)], axis=0).astype(BF16)
    up_stack, bias2 = _gate_map_layout(gla_a_up, gla_a_bias)
    rd_b = jnp.broadcast_to(ret_decay.reshape(2, RET_HEADS, 1, 1), (2, RET_HEADS, 8, RET_DK))

    mod = _modulation(c_rows, w_mod, b_mod)
    mod3 = mod.reshape(mod.shape[0], 1, 6 * d)
    nw_pre = norm_mix_pre.reshape(1, d)

    tm = 1024
    p, ga, w_main = _inproj_first(h_state, nw_pre, mod3, w_t, w_ga, cos, sin, tm=tm, tn=1024)
    p, ga, wur, wug, wo = _inproj(
        h_state, nw_pre, mod3, lambda i: i // (seq // tm), w_main, w_ga, cos, sin,
        tm=tm, tn=1024, ncols=N_MAIN, rope=True, seq_tiles=seq // tm,
        casts=(w_up_ret, w_up_gla, w_out), row_tile0=1, write_into=(p, ga))
    pc, gac = _inproj(ctx2d, nw_pre, mod3, lambda i: batch, w_main, w_ga, cos, sin,
                      tm=batch * ctx_len, tn=N_CTX // 2, ncols=N_CTX, rope=False, seq_tiles=1)

    (o_r,) = _retention(p, pc, rd_b, ret_gn.reshape(1, RET_VW), (), batch=batch, seq=seq, ctx_len=ctx_len)
    o_g, wu = _gla(p, ga, pc, gac, up_stack, bias2, gla_gn.reshape(1, GLA_VW), (ffn_w_up,),
                   batch=batch, seq=seq, ctx_len=ctx_len)

    hs, h2, wg = _post(o_r, o_g, p, h_state, wur, wug, wo,
                       mod3, norm_mix_post.reshape(1, d), norm_ffn_pre.reshape(1, d), ffn_w_gate,
                       tm=256, seq=seq)
    act, wd = _ffn_up(h2, wg, wu, ffn_w_down, tm=1024, tf=512)
    return _ffn_down(act, hs, wd, mod3, norm_ffn_post.reshape(1, d), tm=256, seq=seq)


def kernel(x, c, ctx, c_ctx, w_mod, b_mod, norm_mix_pre, norm_mix_post, norm_ffn_pre, norm_ffn_post,
           w_in, ret_decay, gla_a_up, gla_a_bias, ret_gn, gla_gn, w_up_ret, w_up_gla, w_out,
           ffn_w_gate, ffn_w_up, ffn_w_down):
    batch, seq, d = x.shape
    ctx_len = ctx.shape[1]
    depth = w_mod.shape[0]
    cos, sin = _rope_tables(seq)
    c_rows = jnp.zeros((16, d), F32).at[:batch].set(c.astype(F32)).at[batch].set(c_ctx.astype(F32))
    ctx2d = ctx.astype(F32).reshape(batch * ctx_len, d)
    h_state = x.astype(F32).reshape(batch * seq, d)
    for i in range(depth):
        h_state = _layer(h_state, c_rows, ctx2d, w_mod[i], b_mod[i], norm_mix_pre[i], norm_mix_post[i],
                         norm_ffn_pre[i], norm_ffn_post[i], w_in[i], ret_decay[i], gla_a_up[i], gla_a_bias[i],
                         ret_gn[i], gla_gn[i], w_up_ret[i], w_up_gla[i], w_out[i],
                         ffn_w_gate[i], ffn_w_up[i], ffn_w_down[i], cos, sin,
                         batch=batch, seq=seq, ctx_len=ctx_len)
    return h_state.reshape(batch, seq, d).astype(x.dtype)
```

```python
import functools

import jax
import jax.numpy as jnp
import numpy as np
from jax import lax
from jax.experimental import pallas as pl
from jax.experimental.pallas import tpu as pltpu

F32 = jnp.float32
BF16 = jnp.bfloat16

D_MODEL = 2048
GRID_W = 64
RET_HEADS = 4
RET_DK = 256
RET_DV = 256
GLA_HEADS = 4
GLA_DK = 128
GLA_DV = 256
GLA_LOW_RANK = 16
GLA_GATE_NORM = 16.0
ROPE_BASE = 10000.0
EPS = 1e-6

RET_QK = RET_HEADS * RET_DK
RET_VW = RET_HEADS * RET_DV
GLA_KW = GLA_HEADS * GLA_DK
GLA_VW = GLA_HEADS * GLA_DV
GA_W = 2 * GLA_LOW_RANK
LANE = 128

_SRC = {}
_off = 0
for _name, _w in (("rk", RET_QK), ("rv", RET_VW), ("gk", GLA_KW), ("gv", GLA_VW), ("ga", GA_W),
                  ("rq", RET_QK), ("rg", RET_VW), ("gq", GLA_KW), ("gg", GLA_VW),
                  ("gate_a", D_MODEL), ("gate_b", D_MODEL)):
    _SRC[_name] = (_off, _w)
    _off += _w

_ORDER = ("rk", "rv", "gv", "gk", "gq", "rq", "rg", "gg", "gate_a", "gate_b")
_DST = {}
_off = 0
for _name in _ORDER:
    _DST[_name] = _off
    _off += _SRC[_name][1]
N_MAIN = _off
N_CTX = _DST["gq"]
PACK_W = 512

VMEM_LIMIT = 56 * 1024 * 1024


def _dot(a, b):
    return jnp.dot(a, b, preferred_element_type=F32)


def _dot_nt(a, b):
    return lax.dot_general(a, b, (((1,), (1,)), ((), ())), preferred_element_type=F32)


def _dot_tn(a, b):
    return lax.dot_general(a, b, (((0,), (0,)), ((), ())), preferred_element_type=F32)


def _sigmoid(x):
    return 1.0 / (1.0 + jnp.exp(-x))


def _rms(x):
    return x * lax.rsqrt(jnp.mean(x * x, axis=-1, keepdims=True) + EPS)


def _run_staged(*stages):
    live = list(stages)
    while live:
        live = [g for g in live if next(g, live) is not live]


def _split_bf16(x):
    hi = x.astype(BF16)
    lo = (x - hi.astype(F32)).astype(BF16)
    return hi, lo


MOD_TN = 1024


def _mod_kernel(c_ref, wm_ref, bm_ref, mod_ref):
    cf = c_ref[...]
    rows = cf.shape[0]
    s_hi, s_lo = _split_bf16(cf * _sigmoid(cf))
    w_hi, w_lo = _split_bf16(wm_ref[...])
    r = _dot(jnp.concatenate([s_hi, s_lo], axis=0), w_hi)
    mod_ref[...] = r[:rows] + r[rows:] + _dot(s_hi, w_lo) + bm_ref[...]


def _modulation(c_rows, w_mod, b_mod):
    rows, d = c_rows.shape
    n = w_mod.shape[1]
    return pl.pallas_call(
        _mod_kernel,
        out_shape=jax.ShapeDtypeStruct((rows, n), F32),
        grid=(n // MOD_TN,),
        in_specs=[pl.BlockSpec((rows, d), lambda j: (0, 0)),
                  pl.BlockSpec((d, MOD_TN), lambda j: (0, j)),
                  pl.BlockSpec((1, MOD_TN), lambda j: (0, j))],
        out_specs=pl.BlockSpec((rows, MOD_TN), lambda j: (0, j)),
        compiler_params=pltpu.CompilerParams(dimension_semantics=("arbitrary",),
                                             vmem_limit_bytes=VMEM_LIMIT),
        name="mod",
    )(c_rows, w_mod, b_mod.reshape(1, n))


def _pack_tables():
    a_idx, n_idx, shift = [], [], []
    for name in _ORDER:
        src, width = _SRC[name]
        for c in range(src, src + width, PACK_W):
            base = c - c % PACK_W
            assert c - base in (0, GA_W)
            a_idx.append(base // PACK_W)
            n_idx.append((base + PACK_W) // GA_W if c != base else (n_idx[-1] if n_idx else 0))
            shift.append(c - base)
    return tuple(jnp.asarray(v, jnp.int32) for v in (a_idx, n_idx, shift))


def _inproj_kernel(x_ref, nw_ref, sh_ref, sc_ref, w_ref, wga_ref, cos_ref, sin_ref, *rest,
                   rope_tiles, tn, n_cast, first_given=False):
    first = rest[:2] if first_given else ()
    rest = rest[len(first):]
    cast_in = rest[:n_cast]
    o_ref, ga_ref = rest[n_cast:n_cast + 2]
    cast_out = rest[n_cast + 2:2 * n_cast + 2]
    h_scr = rest[2 * n_cast + 2]

    def ride_along_casts():
        for src, dst in zip(cast_in, cast_out):
            dst[...] = src[...].astype(dst.dtype)

    def body():
        _inproj_body(x_ref, nw_ref, sh_ref, sc_ref, w_ref, wga_ref, cos_ref, sin_ref, o_ref, ga_ref, h_scr,
                     ride_along_casts, rope_tiles, tn)

    if not first:
        body()
        return

    p0_ref, ga0_ref = first
    i = pl.program_id(0)

    @pl.when(i == 0)
    def _():
        ride_along_casts()
        o_ref[...] = p0_ref[...]
        ga_ref[...] = ga0_ref[...]

    pl.when(i > 0)(body)


def _inproj_first_kernel(a_idx_ref, n_idx_ref, shift_ref, x_ref, nw_ref, sh_ref, sc_ref,
                         a1_ref, n1_ref, a2_ref, n2_ref, wga_ref, cos_ref, sin_ref,
                         o_ref, ga_ref, wm_ref, h_scr, w_scr, *, rope_tiles, tn):
    j = pl.program_id(1)
    for part, (a_ref, n_ref) in enumerate(((a1_ref, n1_ref), (a2_ref, n2_ref))):
        t = 2 * j + part
        rows = slice(part * PACK_W, (part + 1) * PACK_W)

        @pl.when(shift_ref[t] == 0)
        def _(a_ref=a_ref, rows=rows):
            w_scr[rows, :] = a_ref[...].astype(w_scr.dtype)

        @pl.when(shift_ref[t] != 0)
        def _(a_ref=a_ref, n_ref=n_ref, rows=rows):
            w_scr[rows, :] = jnp.concatenate([a_ref[GA_W:, :], n_ref[...]], axis=0).astype(w_scr.dtype)

    wm_ref[...] = w_scr[...]
    _inproj_body(x_ref, nw_ref, sh_ref, sc_ref, w_scr, wga_ref, cos_ref, sin_ref, o_ref, ga_ref, h_scr,
                 lambda: None, rope_tiles, tn)


def _inproj_body(x_ref, nw_ref, sh_ref, sc_ref, w_ref, wga_ref, cos_ref, sin_ref, o_ref, ga_ref, h_scr,
                 ride_along_casts, rope_tiles, tn):
    j = pl.program_id(1)

    @pl.when(j == 0)
    def _():
        h = _rms(x_ref[...]) * (nw_ref[...] * (1.0 + sc_ref[...])) + sh_ref[...]
        hb = h.astype(BF16)
        h_scr[...] = hb
        g3 = _dot_nt(hb, wga_ref[...])
        lane = lax.broadcasted_iota(jnp.int32, g3.shape, 1)
        resid = g3 - g3.astype(BF16).astype(F32)
        ga_ref[...] = jnp.where((lane >= GA_W) & (lane < 2 * GA_W), resid, g3).astype(ga_ref.dtype)

    if not rope_tiles:
        ride_along_casts()
        o_ref[...] = _dot_nt(h_scr[...], w_ref[...]).astype(o_ref.dtype)
        return

    is_rope = functools.reduce(jnp.logical_or, [j == t for t in rope_tiles])

    @pl.when(is_rope)
    def _():
        ride_along_casts()
        cos = cos_ref[...]
        sin = sin_ref[...]
        half = RET_DK // 2
        for hd in range(0, tn, RET_DK):
            acc = _dot_nt(h_scr[...], w_ref[hd:hd + RET_DK, :])
            t1 = acc[:, :half]
            t2 = acc[:, half:]
            o_ref[:, hd:hd + half] = (t1 * cos - t2 * sin).astype(o_ref.dtype)
            o_ref[:, hd + half:hd + RET_DK] = (t1 * sin + t2 * cos).astype(o_ref.dtype)

    @pl.when(jnp.logical_not(is_rope))
    def _():
        ride_along_casts()
        o_ref[...] = _dot_nt(h_scr[...], w_ref[...]).astype(o_ref.dtype)


def _cast_plan(arr, n_i, n_j):
    rows, cols = arr.shape

    def pieces(extent, want, align):
        n = want
        while extent % n or (extent // n) % align:
            n -= 1
        return n

    if rows % n_j == 0 and (rows // n_j) % 16 == 0:
        nr, nc = n_j, pieces(cols, n_i, LANE)
        index = lambda i, j, nc=nc: (jnp.where(i >= nc, n_j - 1, j), jnp.minimum(i, nc - 1))
    else:
        nr, nc = pieces(rows, n_i, 16), pieces(cols, n_j, LANE)
        index = lambda i, j, nr=nr, nc=nc: (jnp.minimum(i, nr - 1),
                                            jnp.where(i >= nr, nc - 1, jnp.minimum(j, nc - 1)))
    return (rows // nr, cols // nc), index


def _rope_tiles(tn):
    assert tn == RET_QK and _DST["rk"] % tn == 0 and _DST["rq"] % tn == 0
    return (_DST["rk"] // tn, _DST["rq"] // tn)


def _inproj_first(x2d, norm_w, mod3, w_t, w_ga, cos, sin, *, tm, tn):
    m, d = x2d.shape
    assert tn == 2 * PACK_W
    kern = functools.partial(_inproj_first_kernel, rope_tiles=_rope_tiles(tn), tn=tn)

    def src(part, table):
        return lambda i, j, a, n, s: ((a, n)[table][2 * j + part], 0)

    return pl.pallas_call(
        kern,
        out_shape=(jax.ShapeDtypeStruct((tm, N_MAIN), BF16),
                   jax.ShapeDtypeStruct((tm, LANE), BF16),
                   jax.ShapeDtypeStruct((N_MAIN, d), BF16)),
        grid_spec=pltpu.PrefetchScalarGridSpec(
            num_scalar_prefetch=3,
            grid=(1, N_MAIN // tn),
            in_specs=[
                pl.BlockSpec((tm, d), lambda i, j, *_: (0, 0)),
                pl.BlockSpec((1, d), lambda i, j, *_: (0, 0)),
                pl.BlockSpec((None, 1, d), lambda i, j, *_: (0, 0, 0)),
                pl.BlockSpec((None, 1, d), lambda i, j, *_: (0, 0, 1)),
                pl.BlockSpec((PACK_W, d), src(0, 0)),
                pl.BlockSpec((GA_W, d), src(0, 1)),
                pl.BlockSpec((PACK_W, d), src(1, 0)),
                pl.BlockSpec((GA_W, d), src(1, 1)),
                pl.BlockSpec((LANE, d), lambda i, j, *_: (0, 0)),
                pl.BlockSpec((tm, RET_DK // 2), lambda i, j, *_: (0, 0)),
                pl.BlockSpec((tm, RET_DK // 2), lambda i, j, *_: (0, 0)),
            ],
            out_specs=(pl.BlockSpec((tm, tn), lambda i, j, *_: (0, j)),
                       pl.BlockSpec((tm, LANE), lambda i, j, *_: (0, 0)),
                       pl.BlockSpec((tn, d), lambda i, j, *_: (j, 0))),
            scratch_shapes=[pltpu.VMEM((tm, d), BF16), pltpu.VMEM((tn, d), BF16)],
        ),
        compiler_params=pltpu.CompilerParams(dimension_semantics=("arbitrary", "arbitrary"),
                                             vmem_limit_bytes=VMEM_LIMIT),
        name="inproj_first",
    )(*_pack_tables(), x2d, norm_w, mod3, mod3, w_t, w_t, w_t, w_t, w_ga, cos, sin)


def _inproj(x2d, norm_w, mod3, mod_row_of_tile, w_main, w_ga, cos, sin, *, tm, tn, ncols, rope, seq_tiles,
            casts=(), first=()):
    m, d = x2d.shape
    rope_tiles = _rope_tiles(tn) if rope else ()
    grid = (m // tm, ncols // tn)
    cast_specs = [pl.BlockSpec(*_cast_plan(a, *grid)) for a in casts]
    kern = functools.partial(_inproj_kernel, rope_tiles=rope_tiles, tn=tn, n_cast=len(casts),
                             first_given=bool(first))
    if first:
        row = lambda i: jnp.maximum(i, 1)
        wcol = lambda i, j: jnp.where(i == 0, 0, j)
        first_specs = [pl.BlockSpec((tm, tn), lambda i, j: (0, jnp.where(i == 0, j, ncols // tn - 1))),
                       pl.BlockSpec((tm, LANE), lambda i, j: (0, 0))]
    else:
        row = lambda i: i
        wcol = lambda i, j: j
        first_specs = []
    return pl.pallas_call(
        kern,
        out_shape=(jax.ShapeDtypeStruct((m, ncols), BF16),
                   jax.ShapeDtypeStruct((m, LANE), BF16),
                   *[jax.ShapeDtypeStruct(a.shape, BF16) for a in casts]),
        grid=grid,
        in_specs=[
            pl.BlockSpec((tm, d), lambda i, j: (row(i), 0)),
            pl.BlockSpec((1, d), lambda i, j: (0, 0)),
            pl.BlockSpec((None, 1, d), lambda i, j: (mod_row_of_tile(row(i)), 0, 0)),
            pl.BlockSpec((None, 1, d), lambda i, j: (mod_row_of_tile(row(i)), 0, 1)),
            pl.BlockSpec((tn, d), lambda i, j: (wcol(i, j), 0)),
            pl.BlockSpec((LANE, d), lambda i, j: (0, 0)),
            pl.BlockSpec((tm, RET_DK // 2), lambda i, j: (row(i) % seq_tiles, 0)),
            pl.BlockSpec((tm, RET_DK // 2), lambda i, j: (row(i) % seq_tiles, 0)),
            *first_specs,
            *cast_specs,
        ],
        out_specs=(pl.BlockSpec((tm, tn), lambda i, j: (i, j)),
                   pl.BlockSpec((tm, LANE), lambda i, j: (i, 0)),
                   *cast_specs),
        scratch_shapes=[pltpu.VMEM((tm, d), BF16)],
        compiler_params=pltpu.CompilerParams(dimension_semantics=("arbitrary", "arbitrary"),
                                             vmem_limit_bytes=VMEM_LIMIT),
        name="inproj",
    )(x2d, norm_w, mod3, mod3, w_main, w_ga, cos, sin, *first, *casts)


RET_CHUNK = 256


def _ride_along_casts(rest, n_cast):
    for src, dst in zip(rest[:n_cast], rest[n_cast + 1:2 * n_cast + 1]):
        dst[...] = src[...].astype(dst.dtype)
    return rest[n_cast], rest[2 * n_cast + 1:]


def _ret_kernel(q_ref, k_ref, v_ref, g_ref, kc_ref, vc_ref, rd_ref, gn_ref, *rest, n_chunks, n_cast):
    o_ref, (o_scr, sc_scr, s_scr, dm_scr, eq_scr, wk_scr, sd_scr) = _ride_along_casts(rest, n_cast)
    c_len = RET_CHUNK
    ii = lax.broadcasted_iota(jnp.int32, (c_len, c_len), 0)
    jj = lax.broadcasted_iota(jnp.int32, (c_len, c_len), 1)
    rowi = lax.broadcasted_iota(jnp.int32, (c_len, RET_DK), 0).astype(F32)
    scale = RET_DK ** -0.5

    lg_f = -jnp.exp(rd_ref[0][0:1, :])
    lg_b = -jnp.exp(rd_ref[1][0:1, :])
    dm_scr[...] = jnp.where(jj <= ii, jnp.exp((ii - jj).astype(F32) * lg_f),
                            jnp.exp((jj - ii).astype(F32) * lg_b))
    for direction, lg in ((0, lg_f), (1, lg_b)):
        if direction == 0:
            eq_scr[direction] = jnp.exp((rowi + 1.0) * lg)
            wk = jnp.exp((c_len - 1.0 - rowi) * lg)
        else:
            eq_scr[direction] = jnp.exp((c_len - rowi) * lg)
            wk = jnp.exp(rowi * lg)
        wk_scr[direction] = wk
        sd_scr[direction] = jnp.broadcast_to(jnp.exp(float(c_len) * lg), (8, RET_DV))
        kcw = (kc_ref[...].astype(F32) * wk).astype(BF16)
        s_scr[direction] = _dot_tn(kcw, vc_ref[...])

    def rows_of(c):
        return pl.ds(pl.multiple_of(c * c_len, c_len), c_len)

    def score_stage(c):
        rows = rows_of(c)
        s_qk = _dot_nt(q_ref[rows, :], k_ref[rows, :])
        yield
        sc_scr[rows, :] = (s_qk * dm_scr[...]).astype(BF16)
        yield

    def value_stage(c):
        rows = rows_of(c)
        o_scr[rows, :] = _dot(sc_scr[rows, :], v_ref[rows, :])
        yield

    _run_staged(score_stage(0))

    def intra_body(t, carry):
        _run_staged(score_stage(t + 1), value_stage(t))
        return carry

    lax.fori_loop(0, n_chunks - 1, intra_body, 0, unroll=n_chunks - 1)
    _run_staged(value_stage(n_chunks - 1))

    def scan_chunk(direction, c, done):
        rows = rows_of(c)
        q = q_ref[rows, :]
        k = k_ref[rows, :]
        v = v_ref[rows, :]
        s_state = s_scr[direction]
        kv = _dot_tn((k.astype(F32) * wk_scr[direction]).astype(BF16), v)
        o_state = _dot(q, s_state.astype(BF16))
        yield
        s_scr[direction] = sd_scr[direction][0:1, :] * s_state + kv
        tot = o_scr[rows, :] + eq_scr[direction] * o_state
        if not done:
            o_scr[rows, :] = tot
        else:
            mu = jnp.mean(tot, axis=-1, keepdims=True)
            cen = tot - mu
            var = jnp.mean(cen * cen, axis=-1, keepdims=True)
            nrm = cen * (scale * lax.rsqrt(var * (scale * scale) + EPS))
            g = g_ref[rows, :]
            o_ref[rows, :] = (nrm * gn_ref[...] * (g * _sigmoid(g)).astype(F32)).astype(o_ref.dtype)
        yield

    def body(t, carry, done):
        _run_staged(scan_chunk(0, t, done), scan_chunk(1, n_chunks - 1 - t, done))
        return carry

    lax.fori_loop(0, n_chunks // 2, functools.partial(body, done=False), 0, unroll=n_chunks // 2)
    lax.fori_loop(n_chunks // 2, n_chunks, functools.partial(body, done=True), 0, unroll=n_chunks // 2)


def _retention(p, pc, rd_b, ret_gn, casts, *, batch, seq, ctx_len):
    assert ctx_len == RET_CHUNK and (seq // RET_CHUNK) % 2 == 0
    n_chunks = seq // RET_CHUNK
    w = RET_DK
    kern = functools.partial(_ret_kernel, n_chunks=n_chunks, n_cast=len(casts))
    cast_specs = [pl.BlockSpec(*_cast_plan(a, batch, RET_HEADS)) for a in casts]

    def col(name):
        return _DST[name] // w

    return pl.pallas_call(
        kern,
        out_shape=(jax.ShapeDtypeStruct((batch * seq, RET_VW), BF16),
                   *[jax.ShapeDtypeStruct(a.shape, BF16) for a in casts]),
        grid=(batch, RET_HEADS),
        in_specs=[
            pl.BlockSpec((seq, w), lambda b, h: (b, col("rq") + h)),
            pl.BlockSpec((seq, w), lambda b, h: (b, col("rk") + h)),
            pl.BlockSpec((seq, w), lambda b, h: (b, col("rv") + h)),
            pl.BlockSpec((seq, w), lambda b, h: (b, col("rg") + h)),
            pl.BlockSpec((ctx_len, w), lambda b, h: (b, col("rk") + h)),
            pl.BlockSpec((ctx_len, w), lambda b, h: (b, col("rv") + h)),
            pl.BlockSpec((2, None, 8, w), lambda b, h: (0, h, 0, 0)),
            pl.BlockSpec((1, w), lambda b, h: (0, h)),
            *cast_specs,
        ],
        out_specs=(pl.BlockSpec((seq, w), lambda b, h: (b, h)), *cast_specs),
        scratch_shapes=[
            pltpu.VMEM((seq, RET_DV), F32),
            pltpu.VMEM((seq, RET_CHUNK), BF16),
            pltpu.VMEM((2, RET_DK, RET_DV), F32),
            pltpu.VMEM((RET_CHUNK, RET_CHUNK), F32),
            pltpu.VMEM((2, RET_CHUNK, RET_DV), F32),
            pltpu.VMEM((2, RET_CHUNK, RET_DK), F32),
            pltpu.VMEM((2, 8, RET_DV), F32),
        ],
        compiler_params=pltpu.CompilerParams(dimension_semantics=("arbitrary", "arbitrary"),
                                             vmem_limit_bytes=VMEM_LIMIT),
        name="retention",
    )(p, p, p, p, pc, pc, rd_b, ret_gn, *casts)


GLA_SUB = 64
GLA_BLK = 256
GLA_SCAN_BLK = 512
PREP_UNROLL = 14


def _gla_kernel(q_ref, k_ref, v_ref, g_ref, ga_ref, kc_ref, vc_ref, gac_ref, up_ref, bias_ref, gn_ref,
                *rest, n_blk, n_cast):
    o_ref, scratch = _ride_along_casts(rest, n_cast)
    (ahi_scr, alo_scr, qf_scr, qb_scr, ktf_scr, ktb_scr, khf_scr, khb_scr, khcf_scr, khcb_scr,
     dec_scr, decc_scr, o_scr, s_scr) = scratch
    blk, sub, dk = GLA_BLK, GLA_SUB, GLA_DK
    nsub = blk // sub
    shift = sub.bit_length() - 1
    scale = GLA_DK ** -0.5

    def masks(n):
        ii = lax.broadcasted_iota(jnp.int32, (n, n), 0)
        jj = lax.broadcasted_iota(jnp.int32, (n, n), 1)
        same = (ii >> shift) == (jj >> shift)
        lower = jj <= ii
        return same, lower, jnp.where(same & lower, 1.0, 0.0).astype(BF16)

    ctx_rows = gac_ref.shape[0]
    blk_masks = masks(blk)
    ctx_masks = blk_masks if ctx_rows == blk else masks(ctx_rows)
    r8 = lax.broadcasted_iota(jnp.int32, (8, dk), 0)

    u = up_ref[...]
    u_hi, u_lo = _split_bf16(u)
    urow = lax.broadcasted_iota(jnp.int32, u.shape, 0)
    rhs = jnp.where(urow >= 2 * GA_W, u_lo, u_hi)
    bias = bias_ref[...]

    def gate_stage(ga_blk, put):
        z = _dot(ga_blk, rhs) + bias
        yield
        a = (jnp.minimum(z, 0.0) - jnp.log(1.0 + jnp.exp(-jnp.abs(z)))) * (1.0 / GLA_GATE_NORM)
        put(*_split_bf16(a))
        yield

    def decay_stage(a_hi, a_lo, k_blk, q_blk, msk, store):
        same, lower, tmat = msk
        nsub = a_hi.shape[0] // sub
        assert nsub <= 8
        pre = _dot(tmat, a_hi) + _dot(tmat, a_lo)
        yield
        a = a_hi.astype(F32) + a_lo.astype(F32)
        lasts = [pre[i * sub + sub - 1:i * sub + sub, :] for i in range(nsub)]
        tot = jnp.concatenate([jnp.broadcast_to(l, (sub, 2 * dk)) for l in lasts], axis=0)
        b_f = pre[:, :dk]
        ex_b = pre[:, dk:] - a[:, dk:]
        b_b = tot[:, dk:] - ex_b
        kf = k_blk.astype(F32)
        khf = (kf * jnp.exp(tot[:, :dk] - b_f)).astype(BF16)
        khb = (kf * jnp.exp(ex_b)).astype(BF16)

        def tile8(half):
            rows = [jnp.broadcast_to(l[:, half * dk:(half + 1) * dk], (8, dk)) for l in lasts]
            t = rows[nsub - 1]
            for i in range(nsub - 2, -1, -1):
                t = jnp.where(r8 == i, rows[i], t)
            return t

        dec = jnp.exp(jnp.concatenate([tile8(0), tile8(1), jnp.zeros((LANE - 16, dk), F32)], axis=0))
        out = dict(khf=khf, khb=khb, dec_t=dec.T)
        if q_blk is not None:
            qf32 = q_blk.astype(F32)
            out.update(qf=(qf32 * jnp.exp(b_f)).astype(BF16), qb=(qf32 * jnp.exp(b_b)).astype(BF16),
                       ktf=(kf * jnp.exp(-b_f)).astype(BF16), ktb=(kf * jnp.exp(-b_b)).astype(BF16))
        store(**out)
        yield

    def score_stage(r):
        same, lower, _ = blk_masks
        rows = pl.ds(r, blk)
        s_f = _dot_nt(qf_scr[rows, :], ktf_scr[rows, :])
        s_b = _dot_nt(qb_scr[rows, :], ktb_scr[rows, :])
        yield
        s = jnp.where(same, jnp.where(lower, s_f, s_b), 0.0).astype(BF16)
        yield
        o_scr[rows, :] = _dot(s, v_ref[rows, :])
        yield

    run = _run_staged

    def store_ctx(khf, khb, dec_t):
        khcf_scr[...] = khf
        khcb_scr[...] = khb
        decc_scr[...] = dec_t

    ctx_gate = []
    run(gate_stage(gac_ref[...], lambda hi, lo: ctx_gate.extend((hi, lo))))
    run(decay_stage(*ctx_gate, kc_ref[...], None, ctx_masks, store_ctx))

    def gate_block(t):
        rows = pl.ds(pl.multiple_of(t * blk, blk), blk)

        def put(a_hi, a_lo):
            ahi_scr[rows, :] = a_hi
            alo_scr[rows, :] = a_lo

        return gate_stage(ga_ref[rows, :], put)

    def decay_block(t):
        rows = pl.ds(pl.multiple_of(t * blk, blk), blk)

        def store(khf, khb, dec_t, qf, qb, ktf, ktb):
            khf_scr[rows, :] = khf
            khb_scr[rows, :] = khb
            qf_scr[rows, :] = qf
            qb_scr[rows, :] = qb
            ktf_scr[rows, :] = ktf
            ktb_scr[rows, :] = ktb
            dec_scr[t] = dec_t

        return decay_stage(ahi_scr[rows, :], alo_scr[rows, :], k_ref[rows, :], q_ref[rows, :],
                           blk_masks, store)

    def score_block(t):
        return score_stage(pl.multiple_of(t * blk, blk))

    run(gate_block(0))
    run(gate_block(1), decay_block(0))

    def prep_body(t, carry):
        run(gate_block(t + 2), decay_block(t + 1), score_block(t))
        return carry

    lax.fori_loop(0, n_blk - 2, prep_body, 0, unroll=PREP_UNROLL)
    run(decay_block(n_blk - 1), score_block(n_blk - 2))
    run(score_block(n_blk - 1))

    def scan_blk(direction, n_sub, dec_ts, kh_at, v_at, q_at=None, o_rows=None):
        s = s_scr[direction]
        order = range(n_sub) if direction == 0 else range(n_sub - 1, -1, -1)
        kv = {i: _dot_tn(kh_at(i), v_at(i)) for i in order}
        yield
        for i in order:
            c = direction * 8 + i % nsub
            col = dec_ts[i // nsub][:, c:c + 1]
            if q_at is not None:
                rows = o_rows(i)
                o_scr[rows, :] += _dot(q_at(i), s.astype(BF16))
            s = s * col + kv[i]
            yield
        s_scr[direction] = s

    s_scr[...] = jnp.zeros(s_scr.shape, F32)
    run(*[scan_blk(direction, ctx_rows // sub, [decc_scr[...]],
                   lambda i, khc=khc: khc[i * sub:(i + 1) * sub, :],
                   lambda i: vc_ref[i * sub:(i + 1) * sub, :])
          for direction, khc in ((0, khcf_scr), (1, khcb_scr))])

    sblk = GLA_SCAN_BLK
    per = sblk // blk
    n_sblk = n_blk // per

    fin_rows = 128

    def finalize(t):
        for tb in (t, n_sblk - 1 - t):
            for r0 in range(0, sblk, fin_rows):
                rows = pl.ds(pl.multiple_of(tb * sblk + r0, fin_rows), fin_rows)
                o = o_scr[rows, :]
                ms = jnp.mean(o * o, axis=-1, keepdims=True)
                nrm = o * (scale * lax.rsqrt(ms * (scale * scale) + EPS))
                g = g_ref[rows, :]
                o_ref[rows, :] = (nrm * gn_ref[...] * (g * _sigmoid(g)).astype(F32)).astype(o_ref.dtype)
                yield

    def scans(t):
        for direction, q_scr, kh_scr in ((0, qf_scr, khf_scr), (1, qb_scr, khb_scr)):
            tb = t if direction == 0 else n_sblk - 1 - t

            def rows(i, tb=tb):
                return pl.ds(pl.multiple_of(tb * sblk + i * sub, sub), sub)

            yield from scan_blk(direction, sblk // sub, [dec_scr[tb * per + j] for j in range(per)],
                                lambda i, kh_scr=kh_scr, rows=rows: kh_scr[rows(i), :],
                                lambda i, rows=rows: v_ref[rows(i), :],
                                lambda i, q_scr=q_scr, rows=rows: q_scr[rows(i), :],
                                rows)

    def scan_body(t, carry, fin_prev):
        if fin_prev:
            run(scans(t), finalize(t - 1))
        else:
            run(scans(t))
        return carry

    half = n_sblk // 2
    lax.fori_loop(0, half + 1, functools.partial(scan_body, fin_prev=False), 0, unroll=half + 1)
    lax.fori_loop(half + 1, n_sblk, functools.partial(scan_body, fin_prev=True), 0, unroll=n_sblk - half - 1)
    run(finalize(n_sblk - 1))


def _gla(p, ga, pc, gac, up_stack, bias2, gla_gn, casts, *, batch, seq, ctx_len):
    assert ctx_len % GLA_SUB == 0 and ctx_len <= GLA_BLK and GLA_SCAN_BLK % GLA_BLK == 0
    assert (seq // GLA_SCAN_BLK) % 2 == 0
    n_blk = seq // GLA_BLK
    kern = functools.partial(_gla_kernel, n_blk=n_blk, n_cast=len(casts))
    dk, dv = GLA_DK, GLA_DV
    cast_specs = [pl.BlockSpec(*_cast_plan(a, batch, GLA_HEADS)) for a in casts]
    return pl.pallas_call(
        kern,
        out_shape=(jax.ShapeDtypeStruct((batch * seq, GLA_VW), BF16),
                   *[jax.ShapeDtypeStruct(a.shape, BF16) for a in casts]),
        grid=(batch, GLA_HEADS),
        in_specs=[
            pl.BlockSpec((seq, dk), lambda b, h: (b, _DST["gq"] // dk + h)),
            pl.BlockSpec((seq, dk), lambda b, h: (b, _DST["gk"] // dk + h)),
            pl.BlockSpec((seq, dv), lambda b, h: (b, _DST["gv"] // dv + h)),
            pl.BlockSpec((seq, dv), lambda b, h: (b, _DST["gg"] // dv + h)),
            pl.BlockSpec((seq, LANE), lambda b, h: (b, 0)),
            pl.BlockSpec((ctx_len, dk), lambda b, h: (b, _DST["gk"] // dk + h)),
            pl.BlockSpec((ctx_len, dv), lambda b, h: (b, _DST["gv"] // dv + h)),
            pl.BlockSpec((ctx_len, LANE), lambda b, h: (b, 0)),
            pl.BlockSpec((LANE, 2 * dk), lambda b, h: (0, h)),
            pl.BlockSpec((1, 2 * dk), lambda b, h: (0, h)),
            pl.BlockSpec((1, dv), lambda b, h: (0, h)),
            *cast_specs,
        ],
        out_specs=(pl.BlockSpec((seq, dv), lambda b, h: (b, h)), *cast_specs),
        scratch_shapes=[
            pltpu.VMEM((seq, 2 * dk), BF16), pltpu.VMEM((seq, 2 * dk), BF16),
            pltpu.VMEM((seq, dk), BF16), pltpu.VMEM((seq, dk), BF16),
            pltpu.VMEM((seq, dk), BF16), pltpu.VMEM((seq, dk), BF16),
            pltpu.VMEM((seq, dk), BF16), pltpu.VMEM((seq, dk), BF16),
            pltpu.VMEM((ctx_len, dk), BF16), pltpu.VMEM((ctx_len, dk), BF16),
            pltpu.VMEM((n_blk, dk, LANE), F32), pltpu.VMEM((dk, LANE), F32),
            pltpu.VMEM((seq, dv), F32),
            pltpu.VMEM((2, dk, dv), F32),
        ],
        compiler_params=pltpu.CompilerParams(dimension_semantics=("arbitrary", "arbitrary"),
                                             vmem_limit_bytes=VMEM_LIMIT),
        name="gla",
    )(p, p, p, p, ga, pc, pc, gac, up_stack, bias2, gla_gn, *casts)


MXU_N = 256


POST_ROWS = 128


def _post_kernel(or_ref, og_ref, ga0_ref, ga1_ref, gb0_ref, gb1_ref, x_ref, wur_ref, wug_ref, wo_ref,
                 g1_ref, sh2_ref, sc2_ref, npost_ref, npre_ref, wsrc_ref, hs_ref, h2_ref, wdst_ref):
    wdst_ref[...] = wsrc_ref[...].astype(wdst_ref.dtype)
    half = ga0_ref.shape[1]
    tm = hs_ref.shape[0]
    groups = [slice(r0, r0 + POST_ROWS) for r0 in range(0, tm, POST_ROWS)]

    def up(rows):
        return _dot(or_ref[rows, :], wur_ref[...]), _dot(og_ref[rows, :], wug_ref[...])

    def out(rows, y_ret, y_gla):
        def merge(ga_r, gb_r, lo):
            return (_sigmoid(ga_r[rows, :].astype(F32)) * y_ret[:, lo:lo + half]
                    + _sigmoid(gb_r[rows, :].astype(F32)) * y_gla[:, lo:lo + half]).astype(BF16)

        merged = jnp.concatenate([merge(ga0_ref, gb0_ref, 0), merge(ga1_ref, gb1_ref, half)], axis=1)
        return _dot(merged, wo_ref[...])

    post_gain = g1_ref[...] * npost_ref[...]
    pre_gain = npre_ref[...] * (1.0 + sc2_ref[...])

    def finish(rows, y):
        hs = x_ref[rows, :] + _rms(y) * post_gain
        hs_ref[rows, :] = hs
        h2 = _rms(hs) * pre_gain + sh2_ref[...]
        h2_ref[rows, :] = h2.astype(h2_ref.dtype)

    ups = [up(rows) for rows in groups]
    ys = [out(rows, *u) for rows, u in zip(groups, ups)]
    for rows, y in zip(groups, ys):
        finish(rows, y)


def _post(o_r, o_g, p, x2d, w_up_ret, w_up_gla, w_out, mod3, npost, npre, w_later, *, tm, seq):
    m, d = x2d.shape
    tiles_per_seq = seq // tm
    const = dict(pipeline_mode=pl.Buffered(1))
    gw = d // 2
    n_steps = m // tm
    assert w_later.shape[0] % (16 * n_steps) == 0
    cast_spec = pl.BlockSpec((w_later.shape[0] // n_steps, w_later.shape[1]), lambda i: (i, 0))

    def modspec(chunk):
        return pl.BlockSpec((None, 1, d), lambda i: (i // tiles_per_seq, 0, chunk))

    def gatespec(name, part):
        return pl.BlockSpec((tm, gw), lambda i: (i, _DST[name] // gw + part))

    return pl.pallas_call(
        _post_kernel,
        out_shape=(jax.ShapeDtypeStruct((m, d), F32), jax.ShapeDtypeStruct((m, d), BF16),
                   jax.ShapeDtypeStruct(w_later.shape, BF16)),
        grid=(n_steps,),
        in_specs=[
            pl.BlockSpec((tm, RET_VW), lambda i: (i, 0)),
            pl.BlockSpec((tm, GLA_VW), lambda i: (i, 0)),
            gatespec("gate_a", 0), gatespec("gate_a", 1), gatespec("gate_b", 0), gatespec("gate_b", 1),
            pl.BlockSpec((tm, d), lambda i: (i, 0)),
            pl.BlockSpec((RET_VW, d), lambda i: (0, 0), **const),
            pl.BlockSpec((GLA_VW, d), lambda i: (0, 0), **const),
            pl.BlockSpec((d, d), lambda i: (0, 0), **const),
            modspec(2), modspec(3), modspec(4),
            pl.BlockSpec((1, d), lambda i: (0, 0)),
            pl.BlockSpec((1, d), lambda i: (0, 0)),
            cast_spec,
        ],
        out_specs=(pl.BlockSpec((tm, d), lambda i: (i, 0)),
                   pl.BlockSpec((tm, d), lambda i: (i, 0)),
                   cast_spec),
        compiler_params=pltpu.CompilerParams(dimension_semantics=("arbitrary",),
                                             vmem_limit_bytes=VMEM_LIMIT),
        name="post",
    )(o_r, o_g, p, p, p, p, x2d, w_up_ret, w_up_gla, w_out, mod3, mod3, mod3, npost, npre, w_later)


def _ffn_up_kernel(h_ref, wg_ref, wu_ref, wsrc_ref, a_ref, wdst_ref):
    wdst_ref[...] = wsrc_ref[...].astype(wdst_ref.dtype)
    h = h_ref[...]
    for lo in range(0, a_ref.shape[1], MXU_N):
        g = _dot(h, wg_ref[:, lo:lo + MXU_N])
        u = _dot(h, wu_ref[:, lo:lo + MXU_N])
        a_ref[:, lo:lo + MXU_N] = (g * _sigmoid(g) * u).astype(a_ref.dtype)


def _ffn_up(h2, wg, wu, w_later, *, tm, tf):
    m, d = h2.shape
    d_ff = wg.shape[1]
    grid = (m // tm, d_ff // tf)
    cast_spec = pl.BlockSpec(*_cast_plan(w_later, *grid))
    return pl.pallas_call(
        _ffn_up_kernel,
        out_shape=(jax.ShapeDtypeStruct((m, d_ff), BF16), jax.ShapeDtypeStruct(w_later.shape, BF16)),
        grid=grid,
        in_specs=[
            pl.BlockSpec((tm, d), lambda i, f: (i, 0)),
            pl.BlockSpec((d, tf), lambda i, f: (0, f)),
            pl.BlockSpec((d, tf), lambda i, f: (0, f)),
            cast_spec,
        ],
        out_specs=(pl.BlockSpec((tm, tf), lambda i, f: (i, f)), cast_spec),
        compiler_params=pltpu.CompilerParams(dimension_semantics=("arbitrary", "arbitrary"),
                                             vmem_limit_bytes=VMEM_LIMIT),
        name="ffn_up",
    )(h2, wg, wu, w_later)


def _ffn_down_kernel(a_ref, wd_ref, hs_ref, g2_ref, nw_ref, o_ref, y_scr, *, n_split):
    tm, d = o_ref.shape
    w = d // n_split
    ss = jnp.zeros((tm, 1), F32)
    for lo in range(0, d, w):
        y = _dot(a_ref[...], wd_ref[:, lo:lo + w])
        y_scr[:, lo:lo + w] = y
        ss = ss + jnp.sum(y * y, axis=-1, keepdims=True)
    inv = lax.rsqrt(ss * (1.0 / d) + EPS)
    o_ref[...] = hs_ref[...] + y_scr[...] * inv * (g2_ref[...] * nw_ref[...])


def _ffn_down(a, hs, wd, mod3, npost, *, tm, seq):
    m, d = hs.shape
    d_ff = wd.shape[0]
    tiles_per_seq = seq // tm
    kern = functools.partial(_ffn_down_kernel, n_split=4)
    return pl.pallas_call(
        kern,
        out_shape=jax.ShapeDtypeStruct((m, d), F32),
        grid=(m // tm,),
        in_specs=[
            pl.BlockSpec((tm, d_ff), lambda i: (i, 0)),
            pl.BlockSpec((d_ff, d), lambda i: (0, 0), pipeline_mode=pl.Buffered(1)),
            pl.BlockSpec((tm, d), lambda i: (i, 0)),
            pl.BlockSpec((None, 1, d), lambda i: (i // tiles_per_seq, 0, 5)),
            pl.BlockSpec((1, d), lambda i: (0, 0)),
        ],
        out_specs=pl.BlockSpec((tm, d), lambda i: (i, 0)),
        scratch_shapes=[pltpu.VMEM((tm, d), F32)],
        compiler_params=pltpu.CompilerParams(dimension_semantics=("arbitrary",),
                                             vmem_limit_bytes=VMEM_LIMIT),
        name="ffn_down",
    )(a, wd, hs, mod3, npost)


def _rope_tables(seq):
    f32 = np.float32
    rows = seq // GRID_W
    pos_r = np.repeat(np.arange(rows, dtype=f32), GRID_W)
    pos_c = np.tile(np.arange(GRID_W, dtype=f32), rows)
    n_f = RET_DK // 4
    inv = (f32(ROPE_BASE) ** (-np.arange(n_f, dtype=f32) / f32(n_f))).astype(f32)
    ang = np.concatenate([pos_r[:, None] * inv, pos_c[:, None] * inv], axis=-1).astype(f32)
    return jnp.asarray(np.cos(ang), F32), jnp.asarray(np.sin(ang), F32)


def _gate_map_layout(gla_a_up, gla_a_bias):
    r, h, dk = GLA_LOW_RANK, GLA_HEADS, GLA_DK
    u = jnp.zeros((GA_W, h, 2, dk), F32)
    u = u.at[:r, :, 0, :].set(gla_a_up[0].reshape(r, h, dk))
    u = u.at[r:, :, 1, :].set(gla_a_up[1].reshape(r, h, dk))
    u = u.reshape(GA_W, h * 2 * dk)
    up_stack = jnp.concatenate([u, u, u, jnp.zeros((LANE - 3 * GA_W, h * 2 * dk), F32)], axis=0)
    bias2 = jnp.stack([gla_a_bias[0].reshape(h, dk), gla_a_bias[1].reshape(h, dk)], axis=1)
    return up_stack, bias2.reshape(1, h * 2 * dk)


def _layer(h_state, c_rows, ctx2d, w_mod, b_mod, norm_mix_pre, norm_mix_post, norm_ffn_pre, norm_ffn_post,
           w_in, ret_decay, gla_a_up, gla_a_bias, ret_gn, gla_gn, w_up_ret, w_up_gla, w_out,
           ffn_w_gate, ffn_w_up, ffn_w_down, cos, sin, *, batch, seq, ctx_len):
    d = D_MODEL
    ga0 = _SRC["ga"][0]
    w_t = w_in.T
    w_g = w_t[ga0:ga0 + GA_W]
    w_ga = jnp.concatenate([w_g, w_g, w_g, jnp.zeros((LANE - 3 * GA_W, d), F32)], axis=0).astype(BF16)
    up_stack, bias2 = _gate_map_layout(gla_a_up, gla_a_bias)
    rd_b = jnp.broadcast_to(ret_decay.reshape(2, RET_HEADS, 1, 1), (2, RET_HEADS, 8, RET_DK))

    mod = _modulation(c_rows, w_mod, b_mod)
    mod3 = mod.reshape(mod.shape[0], 1, 6 * d)
    nw_pre = norm_mix_pre.reshape(1, d)

    tm = 1024
    p0, ga0, w_main = _inproj_first(h_state, nw_pre, mod3, w_t, w_ga, cos, sin, tm=tm, tn=1024)
    p, ga, wur, wug, wo = _inproj(
        h_state, nw_pre, mod3, lambda i: i // (seq // tm), w_main, w_ga, cos, sin,
        tm=tm, tn=1024, ncols=N_MAIN, rope=True, seq_tiles=seq // tm,
        casts=(w_up_ret, w_up_gla, w_out), first=(p0, ga0))
    pc, gac = _inproj(ctx2d, nw_pre, mod3, lambda i: batch, w_main, w_ga, cos, sin,
                      tm=batch * ctx_len, tn=N_CTX // 2, ncols=N_CTX, rope=False, seq_tiles=1)

    (o_r,) = _retention(p, pc, rd_b, ret_gn.reshape(1, RET_VW), (), batch=batch, seq=seq, ctx_len=ctx_len)
    o_g, wu = _gla(p, ga, pc, gac, up_stack, bias2, gla_gn.reshape(1, GLA_VW), (ffn_w_up,),
                   batch=batch, seq=seq, ctx_len=ctx_len)

    hs, h2, wg = _post(o_r, o_g, p, h_state, wur, wug, wo,
                       mod3, norm_mix_post.reshape(1, d), norm_ffn_pre.reshape(1, d), ffn_w_gate,
                       tm=256, seq=seq)
    act, wd = _ffn_up(h2, wg, wu, ffn_w_down, tm=1024, tf=512)
    return _ffn_down(act, hs, wd, mod3, norm_ffn_post.reshape(1, d), tm=256, seq=seq)


def kernel(x, c, ctx, c_ctx, w_mod, b_mod, norm_mix_pre, norm_mix_post, norm_ffn_pre, norm_ffn_post,
           w_in, ret_decay, gla_a_up, gla_a_bias, ret_gn, gla_gn, w_up_ret, w_up_gla, w_out,
           ffn_w_gate, ffn_w_up, ffn_w_down):
    batch, seq, d = x.shape
    ctx_len = ctx.shape[1]
    depth = w_mod.shape[0]
    cos, sin = _rope_tables(seq)
    c_rows = jnp.zeros((16, d), F32).at[:batch].set(c.astype(F32)).at[batch].set(c_ctx.astype(F32))
    ctx2d = ctx.astype(F32).reshape(batch * ctx_len, d)
    h_state = x.astype(F32).reshape(batch * seq, d)
    for i in range(depth):
        h_state = _layer(h_state, c_rows, ctx2d, w_mod[i], b_mod[i], norm_mix_pre[i], norm_mix_post[i],
                         norm_ffn_pre[i], norm_ffn_post[i], w_in[i], ret_decay[i], gla_a_up[i], gla_a_bias[i],
                         ret_gn[i], gla_gn[i], w_up_ret[i], w_up_gla[i], w_out[i],
                         ffn_w_gate[i], ffn_w_up[i], ffn_w_down[i], cos, sin,
                         batch=batch, seq=seq, ctx_len=ctx_len)
    return h_state.reshape(batch, seq, d).astype(x.dtype)
```

```python
import functools

import jax
import jax.numpy as jnp
import numpy as np
from jax import lax
from jax.experimental import pallas as pl
from jax.experimental.pallas import tpu as pltpu

F32 = jnp.float32
BF16 = jnp.bfloat16

D_MODEL = 2048
GRID_W = 64
RET_HEADS = 4
RET_DK = 256
RET_DV = 256
GLA_HEADS = 4
GLA_DK = 128
GLA_DV = 256
GLA_LOW_RANK = 16
GLA_GATE_NORM = 16.0
ROPE_BASE = 10000.0
EPS = 1e-6

RET_QK = RET_HEADS * RET_DK
RET_VW = RET_HEADS * RET_DV
GLA_KW = GLA_HEADS * GLA_DK
GLA_VW = GLA_HEADS * GLA_DV
GA_W = 2 * GLA_LOW_RANK
LANE = 128

_SRC = {}
_off = 0
for _name, _w in (("rk", RET_QK), ("rv", RET_VW), ("gk", GLA_KW), ("gv", GLA_VW), ("ga", GA_W),
                  ("rq", RET_QK), ("rg", RET_VW), ("gq", GLA_KW), ("gg", GLA_VW),
                  ("gate_a", D_MODEL), ("gate_b", D_MODEL)):
    _SRC[_name] = (_off, _w)
    _off += _w

_ORDER = ("rk", "rv", "gv", "gk", "gq", "rq", "rg", "gg", "gate_a", "gate_b")
_DST = {}
_off = 0
for _name in _ORDER:
    _DST[_name] = _off
    _off += _SRC[_name][1]
N_MAIN = _off
N_CTX = _DST["gq"]
PACK_W = 512

VMEM_LIMIT = 56 * 1024 * 1024


def _dot(a, b):
    return jnp.dot(a, b, preferred_element_type=F32)


def _dot_nt(a, b):
    return lax.dot_general(a, b, (((1,), (1,)), ((), ())), preferred_element_type=F32)


def _dot_tn(a, b):
    return lax.dot_general(a, b, (((0,), (0,)), ((), ())), preferred_element_type=F32)


def _sigmoid(x):
    return 1.0 / (1.0 + jnp.exp(-x))


def _rms(x):
    return x * lax.rsqrt(jnp.mean(x * x, axis=-1, keepdims=True) + EPS)


def _run_staged(*stages):
    live = list(stages)
    while live:
        live = [g for g in live if next(g, live) is not live]


def _split_bf16(x):
    hi = x.astype(BF16)
    lo = (x - hi.astype(F32)).astype(BF16)
    return hi, lo


MOD_TN = 1024


def _mod_kernel(c_ref, wm_ref, bm_ref, mod_ref):
    cf = c_ref[...]
    rows = cf.shape[0]
    s_hi, s_lo = _split_bf16(cf * _sigmoid(cf))
    w_hi, w_lo = _split_bf16(wm_ref[...])
    r = _dot(jnp.concatenate([s_hi, s_lo], axis=0), w_hi)
    mod_ref[...] = r[:rows] + r[rows:] + _dot(s_hi, w_lo) + bm_ref[...]


def _modulation(c_rows, w_mod, b_mod):
    rows, d = c_rows.shape
    n = w_mod.shape[1]
    return pl.pallas_call(
        _mod_kernel,
        out_shape=jax.ShapeDtypeStruct((rows, n), F32),
        grid=(n // MOD_TN,),
        in_specs=[pl.BlockSpec((rows, d), lambda j: (0, 0)),
                  pl.BlockSpec((d, MOD_TN), lambda j: (0, j)),
                  pl.BlockSpec((1, MOD_TN), lambda j: (0, j))],
        out_specs=pl.BlockSpec((rows, MOD_TN), lambda j: (0, j)),
        compiler_params=pltpu.CompilerParams(dimension_semantics=("arbitrary",),
                                             vmem_limit_bytes=VMEM_LIMIT),
        name="mod",
    )(c_rows, w_mod, b_mod.reshape(1, n))


def _pack_tables():
    a_idx, n_idx, shift = [], [], []
    for name in _ORDER:
        src, width = _SRC[name]
        for c in range(src, src + width, PACK_W):
            base = c - c % PACK_W
            assert c - base in (0, GA_W)
            a_idx.append(base // PACK_W)
            n_idx.append((base + PACK_W) // GA_W if c != base else (n_idx[-1] if n_idx else 0))
            shift.append(c - base)
    return tuple(jnp.asarray(v, jnp.int32) for v in (a_idx, n_idx, shift))


def _inproj_kernel(x_ref, nw_ref, sh_ref, sc_ref, w_ref, wga_ref, cos_ref, sin_ref, *rest,
                   rope_tiles, tn, n_cast):
    cast_in = rest[:n_cast]
    o_ref, ga_ref = rest[n_cast:n_cast + 2]
    cast_out = rest[n_cast + 2:2 * n_cast + 2]
    h_scr = rest[2 * n_cast + 2]

    def ride_along_casts():
        for src, dst in zip(cast_in, cast_out):
            dst[...] = src[...].astype(dst.dtype)

    _inproj_body(x_ref, nw_ref, sh_ref, sc_ref, w_ref, wga_ref, cos_ref, sin_ref, o_ref, ga_ref, h_scr,
                 ride_along_casts, rope_tiles, tn)


def _ctx_pack_kernel(a_idx_ref, n_idx_ref, shift_ref, x_ref, nw_ref, sh_ref, sc_ref, a_ref, n_ref, wga_ref,
                     o_ref, ga_ref, wm_ref, h_scr, w_scr, *, n_ctx_tiles):
    j = pl.program_id(0)

    @pl.when(shift_ref[j] == 0)
    def _():
        w_scr[...] = a_ref[...].astype(w_scr.dtype)

    @pl.when(shift_ref[j] != 0)
    def _():
        w_scr[...] = jnp.concatenate([a_ref[GA_W:, :], n_ref[...]], axis=0).astype(w_scr.dtype)

    wm_ref[...] = w_scr[...]

    @pl.when(j == 0)
    def _():
        h = _rms(x_ref[...]) * (nw_ref[...] * (1.0 + sc_ref[...])) + sh_ref[...]
        hb = h.astype(BF16)
        h_scr[...] = hb
        ga_ref[...] = _pack_gate_input(_dot_nt(hb, wga_ref[...])).astype(ga_ref.dtype)

    @pl.when(j < n_ctx_tiles)
    def _():
        o_ref[...] = _dot_nt(h_scr[...], w_scr[...]).astype(o_ref.dtype)


def _pack_gate_input(g3):
    lane = lax.broadcasted_iota(jnp.int32, g3.shape, 1)
    resid = g3 - g3.astype(BF16).astype(F32)
    return jnp.where((lane >= GA_W) & (lane < 2 * GA_W), resid, g3)


def _inproj_body(x_ref, nw_ref, sh_ref, sc_ref, w_ref, wga_ref, cos_ref, sin_ref, o_ref, ga_ref, h_scr,
                 ride_along_casts, rope_tiles, tn):
    j = pl.program_id(1)

    @pl.when(j == 0)
    def _():
        h = _rms(x_ref[...]) * (nw_ref[...] * (1.0 + sc_ref[...])) + sh_ref[...]
        hb = h.astype(BF16)
        h_scr[...] = hb
        ga_ref[...] = _pack_gate_input(_dot_nt(hb, wga_ref[...])).astype(ga_ref.dtype)

    if not rope_tiles:
        ride_along_casts()
        o_ref[...] = _dot_nt(h_scr[...], w_ref[...]).astype(o_ref.dtype)
        return

    is_rope = functools.reduce(jnp.logical_or, [j == t for t in rope_tiles])

    @pl.when(is_rope)
    def _():
        ride_along_casts()
        cos = cos_ref[...]
        sin = sin_ref[...]
        half = RET_DK // 2
        for hd in range(0, tn, RET_DK):
            acc = _dot_nt(h_scr[...], w_ref[hd:hd + RET_DK, :])
            t1 = acc[:, :half]
            t2 = acc[:, half:]
            o_ref[:, hd:hd + half] = (t1 * cos - t2 * sin).astype(o_ref.dtype)
            o_ref[:, hd + half:hd + RET_DK] = (t1 * sin + t2 * cos).astype(o_ref.dtype)

    @pl.when(jnp.logical_not(is_rope))
    def _():
        ride_along_casts()
        o_ref[...] = _dot_nt(h_scr[...], w_ref[...]).astype(o_ref.dtype)


def _cast_plan(arr, n_i, n_j):
    rows, cols = arr.shape

    def pieces(extent, want, align):
        n = want
        while extent % n or (extent // n) % align:
            n -= 1
        return n

    if rows % n_j == 0 and (rows // n_j) % 16 == 0:
        nr, nc = n_j, pieces(cols, n_i, LANE)
        index = lambda i, j, nc=nc: (jnp.where(i >= nc, n_j - 1, j), jnp.minimum(i, nc - 1))
    else:
        nr, nc = pieces(rows, n_i, 16), pieces(cols, n_j, LANE)
        index = lambda i, j, nr=nr, nc=nc: (jnp.minimum(i, nr - 1),
                                            jnp.where(i >= nr, nc - 1, jnp.minimum(j, nc - 1)))
    return (rows // nr, cols // nc), index


def _rope_tiles(tn):
    assert tn == RET_QK and _DST["rk"] % tn == 0 and _DST["rq"] % tn == 0
    return (_DST["rk"] // tn, _DST["rq"] // tn)


def _ctx_inproj_and_pack(ctx2d, norm_w, mod3, mod_row, w_t, w_ga):
    m, d = ctx2d.shape
    a_idx, n_idx, shift = _pack_tables()
    n_tiles = a_idx.shape[0]
    n_ctx_tiles = N_CTX // PACK_W
    assert n_ctx_tiles * PACK_W == N_CTX and n_tiles * PACK_W == N_MAIN
    kern = functools.partial(_ctx_pack_kernel, n_ctx_tiles=n_ctx_tiles)
    return pl.pallas_call(
        kern,
        out_shape=(jax.ShapeDtypeStruct((m, N_CTX), BF16),
                   jax.ShapeDtypeStruct((m, LANE), BF16),
                   jax.ShapeDtypeStruct((N_MAIN, d), BF16)),
        grid_spec=pltpu.PrefetchScalarGridSpec(
            num_scalar_prefetch=3,
            grid=(n_tiles,),
            in_specs=[
                pl.BlockSpec((m, d), lambda j, *_: (0, 0)),
                pl.BlockSpec((1, d), lambda j, *_: (0, 0)),
                pl.BlockSpec((None, 1, d), lambda j, *_: (mod_row, 0, 0)),
                pl.BlockSpec((None, 1, d), lambda j, *_: (mod_row, 0, 1)),
                pl.BlockSpec((PACK_W, d), lambda j, a, n, s: (a[j], 0)),
                pl.BlockSpec((GA_W, d), lambda j, a, n, s: (n[j], 0)),
                pl.BlockSpec((LANE, d), lambda j, *_: (0, 0)),
            ],
            out_specs=(pl.BlockSpec((m, PACK_W), lambda j, *_: (0, jnp.minimum(j, n_ctx_tiles - 1))),
                       pl.BlockSpec((m, LANE), lambda j, *_: (0, 0)),
                       pl.BlockSpec((PACK_W, d), lambda j, *_: (j, 0))),
            scratch_shapes=[pltpu.VMEM((m, d), BF16), pltpu.VMEM((PACK_W, d), BF16)],
        ),
        compiler_params=pltpu.CompilerParams(dimension_semantics=("arbitrary",),
                                             vmem_limit_bytes=VMEM_LIMIT),
        name="ctx_inproj_pack",
    )(a_idx, n_idx, shift, ctx2d, norm_w, mod3, mod3, w_t, w_t, w_ga)


def _inproj(x2d, norm_w, mod3, mod_row_of_tile, w_main, w_ga, cos, sin, *, tm, tn, ncols, rope, seq_tiles,
            casts=()):
    m, d = x2d.shape
    rope_tiles = _rope_tiles(tn) if rope else ()
    grid = (m // tm, ncols // tn)
    cast_specs = [pl.BlockSpec(*_cast_plan(a, *grid)) for a in casts]
    kern = functools.partial(_inproj_kernel, rope_tiles=rope_tiles, tn=tn, n_cast=len(casts))
    return pl.pallas_call(
        kern,
        out_shape=(jax.ShapeDtypeStruct((m, ncols), BF16),
                   jax.ShapeDtypeStruct((m, LANE), BF16),
                   *[jax.ShapeDtypeStruct(a.shape, BF16) for a in casts]),
        grid=grid,
        in_specs=[
            pl.BlockSpec((tm, d), lambda i, j: (i, 0)),
            pl.BlockSpec((1, d), lambda i, j: (0, 0)),
            pl.BlockSpec((None, 1, d), lambda i, j: (mod_row_of_tile(i), 0, 0)),
            pl.BlockSpec((None, 1, d), lambda i, j: (mod_row_of_tile(i), 0, 1)),
            pl.BlockSpec((tn, d), lambda i, j: (j, 0)),
            pl.BlockSpec((LANE, d), lambda i, j: (0, 0)),
            pl.BlockSpec((tm, RET_DK // 2), lambda i, j: (i % seq_tiles, 0)),
            pl.BlockSpec((tm, RET_DK // 2), lambda i, j: (i % seq_tiles, 0)),
            *cast_specs,
        ],
        out_specs=(pl.BlockSpec((tm, tn), lambda i, j: (i, j)),
                   pl.BlockSpec((tm, LANE), lambda i, j: (i, 0)),
                   *cast_specs),
        scratch_shapes=[pltpu.VMEM((tm, d), BF16)],
        compiler_params=pltpu.CompilerParams(dimension_semantics=("arbitrary", "arbitrary"),
                                             vmem_limit_bytes=VMEM_LIMIT),
        name="inproj",
    )(x2d, norm_w, mod3, mod3, w_main, w_ga, cos, sin, *casts)


RET_CHUNK = 256


def _ride_along_casts(rest, n_cast):
    for src, dst in zip(rest[:n_cast], rest[n_cast + 1:2 * n_cast + 1]):
        dst[...] = src[...].astype(dst.dtype)
    return rest[n_cast], rest[2 * n_cast + 1:]


def _ret_kernel(q_ref, k_ref, v_ref, g_ref, kc_ref, vc_ref, rd_ref, gn_ref, *rest, n_chunks, n_cast):
    o_ref, (o_scr, sc_scr, s_scr, dm_scr, eq_scr, wk_scr, sd_scr) = _ride_along_casts(rest, n_cast)
    c_len = RET_CHUNK
    ii = lax.broadcasted_iota(jnp.int32, (c_len, c_len), 0)
    jj = lax.broadcasted_iota(jnp.int32, (c_len, c_len), 1)
    rowi = lax.broadcasted_iota(jnp.int32, (c_len, RET_DK), 0).astype(F32)
    scale = RET_DK ** -0.5

    lg_f = -jnp.exp(rd_ref[0][0:1, :])
    lg_b = -jnp.exp(rd_ref[1][0:1, :])
    dm_scr[...] = jnp.where(jj <= ii, jnp.exp((ii - jj).astype(F32) * lg_f),
                            jnp.exp((jj - ii).astype(F32) * lg_b))
    for direction, lg in ((0, lg_f), (1, lg_b)):
        if direction == 0:
            eq_scr[direction] = jnp.exp((rowi + 1.0) * lg)
            wk = jnp.exp((c_len - 1.0 - rowi) * lg)
        else:
            eq_scr[direction] = jnp.exp((c_len - rowi) * lg)
            wk = jnp.exp(rowi * lg)
        wk_scr[direction] = wk
        sd_scr[direction] = jnp.broadcast_to(jnp.exp(float(c_len) * lg), (8, RET_DV))
        kcw = (kc_ref[...].astype(F32) * wk).astype(BF16)
        s_scr[direction] = _dot_tn(kcw, vc_ref[...])

    def rows_of(c):
        return pl.ds(pl.multiple_of(c * c_len, c_len), c_len)

    def score_stage(c):
        rows = rows_of(c)
        s_qk = _dot_nt(q_ref[rows, :], k_ref[rows, :])
        yield
        sc_scr[rows, :] = (s_qk * dm_scr[...]).astype(BF16)
        yield

    def value_stage(c):
        rows = rows_of(c)
        o_scr[rows, :] = _dot(sc_scr[rows, :], v_ref[rows, :])
        yield

    _run_staged(score_stage(0))

    def intra_body(t, carry):
        _run_staged(score_stage(t + 1), value_stage(t))
        return carry

    lax.fori_loop(0, n_chunks - 1, intra_body, 0, unroll=n_chunks - 1)
    _run_staged(value_stage(n_chunks - 1))

    def scan_chunk(direction, c, done):
        rows = rows_of(c)
        q = q_ref[rows, :]
        k = k_ref[rows, :]
        v = v_ref[rows, :]
        s_state = s_scr[direction]
        kv = _dot_tn((k.astype(F32) * wk_scr[direction]).astype(BF16), v)
        o_state = _dot(q, s_state.astype(BF16))
        yield
        s_scr[direction] = sd_scr[direction][0:1, :] * s_state + kv
        tot = o_scr[rows, :] + eq_scr[direction] * o_state
        if not done:
            o_scr[rows, :] = tot
        else:
            mu = jnp.mean(tot, axis=-1, keepdims=True)
            cen = tot - mu
            var = jnp.mean(cen * cen, axis=-1, keepdims=True)
            nrm = cen * (scale * lax.rsqrt(var * (scale * scale) + EPS))
            g = g_ref[rows, :]
            o_ref[rows, :] = (nrm * gn_ref[...] * (g * _sigmoid(g)).astype(F32)).astype(o_ref.dtype)
        yield

    def body(t, carry, done):
        _run_staged(scan_chunk(0, t, done), scan_chunk(1, n_chunks - 1 - t, done))
        return carry

    lax.fori_loop(0, n_chunks // 2, functools.partial(body, done=False), 0, unroll=n_chunks // 2)
    lax.fori_loop(n_chunks // 2, n_chunks, functools.partial(body, done=True), 0, unroll=n_chunks // 2)


def _retention(p, pc, rd_b, ret_gn, casts, *, batch, seq, ctx_len):
    assert ctx_len == RET_CHUNK and (seq // RET_CHUNK) % 2 == 0
    n_chunks = seq // RET_CHUNK
    w = RET_DK
    kern = functools.partial(_ret_kernel, n_chunks=n_chunks, n_cast=len(casts))
    cast_specs = [pl.BlockSpec(*_cast_plan(a, batch, RET_HEADS)) for a in casts]

    def col(name):
        return _DST[name] // w

    return pl.pallas_call(
        kern,
        out_shape=(jax.ShapeDtypeStruct((batch * seq, RET_VW), BF16),
                   *[jax.ShapeDtypeStruct(a.shape, BF16) for a in casts]),
        grid=(batch, RET_HEADS),
        in_specs=[
            pl.BlockSpec((seq, w), lambda b, h: (b, col("rq") + h)),
            pl.BlockSpec((seq, w), lambda b, h: (b, col("rk") + h)),
            pl.BlockSpec((seq, w), lambda b, h: (b, col("rv") + h)),
            pl.BlockSpec((seq, w), lambda b, h: (b, col("rg") + h)),
            pl.BlockSpec((ctx_len, w), lambda b, h: (b, col("rk") + h)),
            pl.BlockSpec((ctx_len, w), lambda b, h: (b, col("rv") + h)),
            pl.BlockSpec((2, None, 8, w), lambda b, h: (0, h, 0, 0)),
            pl.BlockSpec((1, w), lambda b, h: (0, h)),
            *cast_specs,
        ],
        out_specs=(pl.BlockSpec((seq, w), lambda b, h: (b, h)), *cast_specs),
        scratch_shapes=[
            pltpu.VMEM((seq, RET_DV), F32),
            pltpu.VMEM((seq, RET_CHUNK), BF16),
            pltpu.VMEM((2, RET_DK, RET_DV), F32),
            pltpu.VMEM((RET_CHUNK, RET_CHUNK), F32),
            pltpu.VMEM((2, RET_CHUNK, RET_DV), F32),
            pltpu.VMEM((2, RET_CHUNK, RET_DK), F32),
            pltpu.VMEM((2, 8, RET_DV), F32),
        ],
        compiler_params=pltpu.CompilerParams(dimension_semantics=("arbitrary", "arbitrary"),
                                             vmem_limit_bytes=VMEM_LIMIT),
        name="retention",
    )(p, p, p, p, pc, pc, rd_b, ret_gn, *casts)


GLA_SUB = 64
GLA_BLK = 256
GLA_SCAN_BLK = 512
PREP_UNROLL = 14


def _gla_kernel(q_ref, k_ref, v_ref, g_ref, ga_ref, kc_ref, vc_ref, gac_ref, up_ref, bias_ref, gn_ref,
                *rest, n_blk, n_cast):
    o_ref, scratch = _ride_along_casts(rest, n_cast)
    (ahi_scr, alo_scr, qf_scr, qb_scr, ktf_scr, ktb_scr, khf_scr, khb_scr, khcf_scr, khcb_scr,
     dec_scr, decc_scr, o_scr, s_scr) = scratch
    blk, sub, dk = GLA_BLK, GLA_SUB, GLA_DK
    nsub = blk // sub
    shift = sub.bit_length() - 1
    scale = GLA_DK ** -0.5

    def masks(n):
        ii = lax.broadcasted_iota(jnp.int32, (n, n), 0)
        jj = lax.broadcasted_iota(jnp.int32, (n, n), 1)
        same = (ii >> shift) == (jj >> shift)
        lower = jj <= ii
        return same, lower, jnp.where(same & lower, 1.0, 0.0).astype(BF16)

    ctx_rows = gac_ref.shape[0]
    blk_masks = masks(blk)
    ctx_masks = blk_masks if ctx_rows == blk else masks(ctx_rows)
    r8 = lax.broadcasted_iota(jnp.int32, (8, dk), 0)

    u = up_ref[...]
    u_hi, u_lo = _split_bf16(u)
    urow = lax.broadcasted_iota(jnp.int32, u.shape, 0)
    rhs = jnp.where(urow >= 2 * GA_W, u_lo, u_hi)
    bias = bias_ref[...]

    def gate_stage(ga_blk, put):
        z = _dot(ga_blk, rhs) + bias
        yield
        a = (jnp.minimum(z, 0.0) - jnp.log(1.0 + jnp.exp(-jnp.abs(z)))) * (1.0 / GLA_GATE_NORM)
        put(*_split_bf16(a))
        yield

    def decay_stage(a_hi, a_lo, k_blk, q_blk, msk, store):
        same, lower, tmat = msk
        nsub = a_hi.shape[0] // sub
        assert nsub <= 8
        pre = _dot(tmat, a_hi) + _dot(tmat, a_lo)
        yield
        a = a_hi.astype(F32) + a_lo.astype(F32)
        lasts = [pre[i * sub + sub - 1:i * sub + sub, :] for i in range(nsub)]
        tot = jnp.concatenate([jnp.broadcast_to(l, (sub, 2 * dk)) for l in lasts], axis=0)
        b_f = pre[:, :dk]
        ex_b = pre[:, dk:] - a[:, dk:]
        b_b = tot[:, dk:] - ex_b
        kf = k_blk.astype(F32)
        khf = (kf * jnp.exp(tot[:, :dk] - b_f)).astype(BF16)
        khb = (kf * jnp.exp(ex_b)).astype(BF16)

        def tile8(half):
            rows = [jnp.broadcast_to(l[:, half * dk:(half + 1) * dk], (8, dk)) for l in lasts]
            t = rows[nsub - 1]
            for i in range(nsub - 2, -1, -1):
                t = jnp.where(r8 == i, rows[i], t)
            return t

        dec = jnp.exp(jnp.concatenate([tile8(0), tile8(1), jnp.zeros((LANE - 16, dk), F32)], axis=0))
        out = dict(khf=khf, khb=khb, dec_t=dec.T)
        if q_blk is not None:
            qf32 = q_blk.astype(F32)
            out.update(qf=(qf32 * jnp.exp(b_f)).astype(BF16), qb=(qf32 * jnp.exp(b_b)).astype(BF16),
                       ktf=(kf * jnp.exp(-b_f)).astype(BF16), ktb=(kf * jnp.exp(-b_b)).astype(BF16))
        store(**out)
        yield

    def score_stage(r):
        same, lower, _ = blk_masks
        rows = pl.ds(r, blk)
        s_f = _dot_nt(qf_scr[rows, :], ktf_scr[rows, :])
        s_b = _dot_nt(qb_scr[rows, :], ktb_scr[rows, :])
        yield
        s = jnp.where(same, jnp.where(lower, s_f, s_b), 0.0).astype(BF16)
        yield
        o_scr[rows, :] = _dot(s, v_ref[rows, :])
        yield

    run = _run_staged

    def store_ctx(khf, khb, dec_t):
        khcf_scr[...] = khf
        khcb_scr[...] = khb
        decc_scr[...] = dec_t

    ctx_gate = []
    run(gate_stage(gac_ref[...], lambda hi, lo: ctx_gate.extend((hi, lo))))
    run(decay_stage(*ctx_gate, kc_ref[...], None, ctx_masks, store_ctx))

    def gate_block(t):
        rows = pl.ds(pl.multiple_of(t * blk, blk), blk)

        def put(a_hi, a_lo):
            ahi_scr[rows, :] = a_hi
            alo_scr[rows, :] = a_lo

        return gate_stage(ga_ref[rows, :], put)

    def decay_block(t):
        rows = pl.ds(pl.multiple_of(t * blk, blk), blk)

        def store(khf, khb, dec_t, qf, qb, ktf, ktb):
            khf_scr[rows, :] = khf
            khb_scr[rows, :] = khb
            qf_scr[rows, :] = qf
            qb_scr[rows, :] = qb
            ktf_scr[rows, :] = ktf
            ktb_scr[rows, :] = ktb
            dec_scr[t] = dec_t

        return decay_stage(ahi_scr[rows, :], alo_scr[rows, :], k_ref[rows, :], q_ref[rows, :],
                           blk_masks, store)

    def score_block(t):
        return score_stage(pl.multiple_of(t * blk, blk))

    run(gate_block(0))
    run(gate_block(1), decay_block(0))

    def prep_body(t, carry):
        run(gate_block(t + 2), decay_block(t + 1), score_block(t))
        return carry

    lax.fori_loop(0, n_blk - 2, prep_body, 0, unroll=PREP_UNROLL)
    run(decay_block(n_blk - 1), score_block(n_blk - 2))
    run(score_block(n_blk - 1))

    def scan_blk(direction, n_sub, dec_ts, kh_at, v_at, q_at=None, o_rows=None):
        s = s_scr[direction]
        order = range(n_sub) if direction == 0 else range(n_sub - 1, -1, -1)
        kv = {i: _dot_tn(kh_at(i), v_at(i)) for i in order}
        yield
        for i in order:
            c = direction * 8 + i % nsub
            col = dec_ts[i // nsub][:, c:c + 1]
            if q_at is not None:
                rows = o_rows(i)
                o_scr[rows, :] += _dot(q_at(i), s.astype(BF16))
            s = s * col + kv[i]
            yield
        s_scr[direction] = s

    s_scr[...] = jnp.zeros(s_scr.shape, F32)
    run(*[scan_blk(direction, ctx_rows // sub, [decc_scr[...]],
                   lambda i, khc=khc: khc[i * sub:(i + 1) * sub, :],
                   lambda i: vc_ref[i * sub:(i + 1) * sub, :])
          for direction, khc in ((0, khcf_scr), (1, khcb_scr))])

    sblk = GLA_SCAN_BLK
    per = sblk // blk
    n_sblk = n_blk // per

    fin_rows = 128

    def finalize(t):
        for tb in (t, n_sblk - 1 - t):
            for r0 in range(0, sblk, fin_rows):
                rows = pl.ds(pl.multiple_of(tb * sblk + r0, fin_rows), fin_rows)
                o = o_scr[rows, :]
                ms = jnp.mean(o * o, axis=-1, keepdims=True)
                nrm = o * (scale * lax.rsqrt(ms * (scale * scale) + EPS))
                g = g_ref[rows, :]
                o_ref[rows, :] = (nrm * gn_ref[...] * (g * _sigmoid(g)).astype(F32)).astype(o_ref.dtype)
                yield

    def scans(t):
        for direction, q_scr, kh_scr in ((0, qf_scr, khf_scr), (1, qb_scr, khb_scr)):
            tb = t if direction == 0 else n_sblk - 1 - t

            def rows(i, tb=tb):
                return pl.ds(pl.multiple_of(tb * sblk + i * sub, sub), sub)

            yield from scan_blk(direction, sblk // sub, [dec_scr[tb * per + j] for j in range(per)],
                                lambda i, kh_scr=kh_scr, rows=rows: kh_scr[rows(i), :],
                                lambda i, rows=rows: v_ref[rows(i), :],
                                lambda i, q_scr=q_scr, rows=rows: q_scr[rows(i), :],
                                rows)

    def scan_body(t, carry, fin_prev):
        if fin_prev:
            run(scans(t), finalize(t - 1))
        else:
            run(scans(t))
        return carry

    half = n_sblk // 2
    lax.fori_loop(0, half + 1, functools.partial(scan_body, fin_prev=False), 0, unroll=half + 1)
    lax.fori_loop(half + 1, n_sblk, functools.partial(scan_body, fin_prev=True), 0, unroll=n_sblk - half - 1)
    run(finalize(n_sblk - 1))


def _gla(p, ga, pc, gac, up_stack, bias2, gla_gn, casts, *, batch, seq, ctx_len):
    assert ctx_len % GLA_SUB == 0 and ctx_len <= GLA_BLK and GLA_SCAN_BLK % GLA_BLK == 0
    assert (seq // GLA_SCAN_BLK) % 2 == 0
    n_blk = seq // GLA_BLK
    kern = functools.partial(_gla_kernel, n_blk=n_blk, n_cast=len(casts))
    dk, dv = GLA_DK, GLA_DV
    cast_specs = [pl.BlockSpec(*_cast_plan(a, batch, GLA_HEADS)) for a in casts]
    return pl.pallas_call(
        kern,
        out_shape=(jax.ShapeDtypeStruct((batch * seq, GLA_VW), BF16),
                   *[jax.ShapeDtypeStruct(a.shape, BF16) for a in casts]),
        grid=(batch, GLA_HEADS),
        in_specs=[
            pl.BlockSpec((seq, dk), lambda b, h: (b, _DST["gq"] // dk + h)),
            pl.BlockSpec((seq, dk), lambda b, h: (b, _DST["gk"] // dk + h)),
            pl.BlockSpec((seq, dv), lambda b, h: (b, _DST["gv"] // dv + h)),
            pl.BlockSpec((seq, dv), lambda b, h: (b, _DST["gg"] // dv + h)),
            pl.BlockSpec((seq, LANE), lambda b, h: (b, 0)),
            pl.BlockSpec((ctx_len, dk), lambda b, h: (b, _DST["gk"] // dk + h)),
            pl.BlockSpec((ctx_len, dv), lambda b, h: (b, _DST["gv"] // dv + h)),
            pl.BlockSpec((ctx_len, LANE), lambda b, h: (b, 0)),
            pl.BlockSpec((LANE, 2 * dk), lambda b, h: (0, h)),
            pl.BlockSpec((1, 2 * dk), lambda b, h: (0, h)),
            pl.BlockSpec((1, dv), lambda b, h: (0, h)),
            *cast_specs,
        ],
        out_specs=(pl.BlockSpec((seq, dv), lambda b, h: (b, h)), *cast_specs),
        scratch_shapes=[
            pltpu.VMEM((seq, 2 * dk), BF16), pltpu.VMEM((seq, 2 * dk), BF16),
            pltpu.VMEM((seq, dk), BF16), pltpu.VMEM((seq, dk), BF16),
            pltpu.VMEM((seq, dk), BF16), pltpu.VMEM((seq, dk), BF16),
            pltpu.VMEM((seq, dk), BF16), pltpu.VMEM((seq, dk), BF16),
            pltpu.VMEM((ctx_len, dk), BF16), pltpu.VMEM((ctx_len, dk), BF16),
            pltpu.VMEM((n_blk, dk, LANE), F32), pltpu.VMEM((dk, LANE), F32),
            pltpu.VMEM((seq, dv), F32),
            pltpu.VMEM((2, dk, dv), F32),
        ],
        compiler_params=pltpu.CompilerParams(dimension_semantics=("arbitrary", "arbitrary"),
                                             vmem_limit_bytes=VMEM_LIMIT),
        name="gla",
    )(p, p, p, p, ga, pc, pc, gac, up_stack, bias2, gla_gn, *casts)


MXU_N = 256


POST_ROWS = 128


def _post_kernel(or_ref, og_ref, ga0_ref, ga1_ref, gb0_ref, gb1_ref, x_ref, wur_ref, wug_ref, wo_ref,
                 g1_ref, sh2_ref, sc2_ref, npost_ref, npre_ref, wsrc_ref, hs_ref, h2_ref, wdst_ref):
    wdst_ref[...] = wsrc_ref[...].astype(wdst_ref.dtype)
    half = ga0_ref.shape[1]
    tm = hs_ref.shape[0]
    groups = [slice(r0, r0 + POST_ROWS) for r0 in range(0, tm, POST_ROWS)]

    def up(rows):
        return _dot(or_ref[rows, :], wur_ref[...]), _dot(og_ref[rows, :], wug_ref[...])

    def out(rows, y_ret, y_gla):
        def merge(ga_r, gb_r, lo):
            return (_sigmoid(ga_r[rows, :].astype(F32)) * y_ret[:, lo:lo + half]
                    + _sigmoid(gb_r[rows, :].astype(F32)) * y_gla[:, lo:lo + half]).astype(BF16)

        merged = jnp.concatenate([merge(ga0_ref, gb0_ref, 0), merge(ga1_ref, gb1_ref, half)], axis=1)
        return _dot(merged, wo_ref[...])

    post_gain = g1_ref[...] * npost_ref[...]
    pre_gain = npre_ref[...] * (1.0 + sc2_ref[...])

    def finish(rows, y):
        hs = x_ref[rows, :] + _rms(y) * post_gain
        hs_ref[rows, :] = hs
        h2 = _rms(hs) * pre_gain + sh2_ref[...]
        h2_ref[rows, :] = h2.astype(h2_ref.dtype)

    ups = [up(rows) for rows in groups]
    ys = [out(rows, *u) for rows, u in zip(groups, ups)]
    for rows, y in zip(groups, ys):
        finish(rows, y)


def _post(o_r, o_g, p, x2d, w_up_ret, w_up_gla, w_out, mod3, npost, npre, w_later, *, tm, seq):
    m, d = x2d.shape
    tiles_per_seq = seq // tm
    const = dict(pipeline_mode=pl.Buffered(1))
    gw = d // 2
    n_steps = m // tm
    assert w_later.shape[0] % (16 * n_steps) == 0
    cast_spec = pl.BlockSpec((w_later.shape[0] // n_steps, w_later.shape[1]), lambda i: (i, 0))

    def modspec(chunk):
        return pl.BlockSpec((None, 1, d), lambda i: (i // tiles_per_seq, 0, chunk))

    def gatespec(name, part):
        return pl.BlockSpec((tm, gw), lambda i: (i, _DST[name] // gw + part))

    return pl.pallas_call(
        _post_kernel,
        out_shape=(jax.ShapeDtypeStruct((m, d), F32), jax.ShapeDtypeStruct((m, d), BF16),
                   jax.ShapeDtypeStruct(w_later.shape, BF16)),
        grid=(n_steps,),
        in_specs=[
            pl.BlockSpec((tm, RET_VW), lambda i: (i, 0)),
            pl.BlockSpec((tm, GLA_VW), lambda i: (i, 0)),
            gatespec("gate_a", 0), gatespec("gate_a", 1), gatespec("gate_b", 0), gatespec("gate_b", 1),
            pl.BlockSpec((tm, d), lambda i: (i, 0)),
            pl.BlockSpec((RET_VW, d), lambda i: (0, 0), **const),
            pl.BlockSpec((GLA_VW, d), lambda i: (0, 0), **const),
            pl.BlockSpec((d, d), lambda i: (0, 0), **const),
            modspec(2), modspec(3), modspec(4),
            pl.BlockSpec((1, d), lambda i: (0, 0)),
            pl.BlockSpec((1, d), lambda i: (0, 0)),
            cast_spec,
        ],
        out_specs=(pl.BlockSpec((tm, d), lambda i: (i, 0)),
                   pl.BlockSpec((tm, d), lambda i: (i, 0)),
                   cast_spec),
        compiler_params=pltpu.CompilerParams(dimension_semantics=("arbitrary",),
                                             vmem_limit_bytes=VMEM_LIMIT),
        name="post",
    )(o_r, o_g, p, p, p, p, x2d, w_up_ret, w_up_gla, w_out, mod3, mod3, mod3, npost, npre, w_later)


def _ffn_up_kernel(h_ref, wg_ref, wu_ref, wsrc_ref, a_ref, wdst_ref):
    wdst_ref[...] = wsrc_ref[...].astype(wdst_ref.dtype)
    h = h_ref[...]
    for lo in range(0, a_ref.shape[1], MXU_N):
        g = _dot(h, wg_ref[:, lo:lo + MXU_N])
        u = _dot(h, wu_ref[:, lo:lo + MXU_N])
        a_ref[:, lo:lo + MXU_N] = (g * _sigmoid(g) * u).astype(a_ref.dtype)


def _ffn_up(h2, wg, wu, w_later, *, tm, tf):
    m, d = h2.shape
    d_ff = wg.shape[1]
    grid = (m // tm, d_ff // tf)
    cast_spec = pl.BlockSpec(*_cast_plan(w_later, *grid))
    return pl.pallas_call(
        _ffn_up_kernel,
        out_shape=(jax.ShapeDtypeStruct((m, d_ff), BF16), jax.ShapeDtypeStruct(w_later.shape, BF16)),
        grid=grid,
        in_specs=[
            pl.BlockSpec((tm, d), lambda i, f: (i, 0)),
            pl.BlockSpec((d, tf), lambda i, f: (0, f)),
            pl.BlockSpec((d, tf), lambda i, f: (0, f)),
            cast_spec,
        ],
        out_specs=(pl.BlockSpec((tm, tf), lambda i, f: (i, f)), cast_spec),
        compiler_params=pltpu.CompilerParams(dimension_semantics=("arbitrary", "arbitrary"),
                                             vmem_limit_bytes=VMEM_LIMIT),
        name="ffn_up",
    )(h2, wg, wu, w_later)


def _ffn_down_kernel(a_ref, wd_ref, hs_ref, g2_ref, nw_ref, o_ref, y_scr, *, n_split):
    tm, d = o_ref.shape
    w = d // n_split
    ss = jnp.zeros((tm, 1), F32)
    for lo in range(0, d, w):
        y = _dot(a_ref[...], wd_ref[:, lo:lo + w])
        y_scr[:, lo:lo + w] = y
        ss = ss + jnp.sum(y * y, axis=-1, keepdims=True)
    inv = lax.rsqrt(ss * (1.0 / d) + EPS)
    o_ref[...] = hs_ref[...] + y_scr[...] * inv * (g2_ref[...] * nw_ref[...])


def _ffn_down(a, hs, wd, mod3, npost, *, tm, seq):
    m, d = hs.shape
    d_ff = wd.shape[0]
    tiles_per_seq = seq // tm
    kern = functools.partial(_ffn_down_kernel, n_split=4)
    return pl.pallas_call(
        kern,
        out_shape=jax.ShapeDtypeStruct((m, d), F32),
        grid=(m // tm,),
        in_specs=[
            pl.BlockSpec((tm, d_ff), lambda i: (i, 0)),
            pl.BlockSpec((d_ff, d), lambda i: (0, 0), pipeline_mode=pl.Buffered(1)),
            pl.BlockSpec((tm, d), lambda i: (i, 0)),
            pl.BlockSpec((None, 1, d), lambda i: (i // tiles_per_seq, 0, 5)),
            pl.BlockSpec((1, d), lambda i: (0, 0)),
        ],
        out_specs=pl.BlockSpec((tm, d), lambda i: (i, 0)),
        scratch_shapes=[pltpu.VMEM((tm, d), F32)],
        compiler_params=pltpu.CompilerParams(dimension_semantics=("arbitrary",),
                                             vmem_limit_bytes=VMEM_LIMIT),
        name="ffn_down",
    )(a, wd, hs, mod3, npost)


def _rope_tables(seq):
    f32 = np.float32
    rows = seq // GRID_W
    pos_r = np.repeat(np.arange(rows, dtype=f32), GRID_W)
    pos_c = np.tile(np.arange(GRID_W, dtype=f32), rows)
    n_f = RET_DK // 4
    inv = (f32(ROPE_BASE) ** (-np.arange(n_f, dtype=f32) / f32(n_f))).astype(f32)
    ang = np.concatenate([pos_r[:, None] * inv, pos_c[:, None] * inv], axis=-1).astype(f32)
    return jnp.asarray(np.cos(ang), F32), jnp.asarray(np.sin(ang), F32)


def _gate_map_layout(gla_a_up, gla_a_bias):
    r, h, dk = GLA_LOW_RANK, GLA_HEADS, GLA_DK
    u = jnp.zeros((GA_W, h, 2, dk), F32)
    u = u.at[:r, :, 0, :].set(gla_a_up[0].reshape(r, h, dk))
    u = u.at[r:, :, 1, :].set(gla_a_up[1].reshape(r, h, dk))
    u = u.reshape(GA_W, h * 2 * dk)
    up_stack = jnp.concatenate([u, u, u, jnp.zeros((LANE - 3 * GA_W, h * 2 * dk), F32)], axis=0)
    bias2 = jnp.stack([gla_a_bias[0].reshape(h, dk), gla_a_bias[1].reshape(h, dk)], axis=1)
    return up_stack, bias2.reshape(1, h * 2 * dk)


def _layer(h_state, c_rows, ctx2d, w_mod, b_mod, norm_mix_pre, norm_mix_post, norm_ffn_pre, norm_ffn_post,
           w_in, ret_decay, gla_a_up, gla_a_bias, ret_gn, gla_gn, w_up_ret, w_up_gla, w_out,
           ffn_w_gate, ffn_w_up, ffn_w_down, cos, sin, *, batch, seq, ctx_len):
    d = D_MODEL
    ga0 = _SRC["ga"][0]
    w_t = w_in.T
    w_g = w_t[ga0:ga0 + GA_W]
    w_ga = jnp.concatenate([w_g, w_g, w_g, jnp.zeros((LANE - 3 * GA_W, d), F32)], axis=0).astype(BF16)
    up_stack, bias2 = _gate_map_layout(gla_a_up, gla_a_bias)
    rd_b = jnp.broadcast_to(ret_decay.reshape(2, RET_HEADS, 1, 1), (2, RET_HEADS, 8, RET_DK))

    mod = _modulation(c_rows, w_mod, b_mod)
    mod3 = mod.reshape(mod.shape[0], 1, 6 * d)
    nw_pre = norm_mix_pre.reshape(1, d)

    pc, gac, w_main = _ctx_inproj_and_pack(ctx2d, nw_pre, mod3, batch, w_t, w_ga)
    tm = 1024
    p, ga, wur, wug, wo = _inproj(
        h_state, nw_pre, mod3, lambda i: i // (seq // tm), w_main, w_ga, cos, sin,
        tm=tm, tn=1024, ncols=N_MAIN, rope=True, seq_tiles=seq // tm,
        casts=(w_up_ret, w_up_gla, w_out))

    (o_r,) = _retention(p, pc, rd_b, ret_gn.reshape(1, RET_VW), (), batch=batch, seq=seq, ctx_len=ctx_len)
    o_g, wu = _gla(p, ga, pc, gac, up_stack, bias2, gla_gn.reshape(1, GLA_VW), (ffn_w_up,),
                   batch=batch, seq=seq, ctx_len=ctx_len)

    hs, h2, wg = _post(o_r, o_g, p, h_state, wur, wug, wo,
                       mod3, norm_mix_post.reshape(1, d), norm_ffn_pre.reshape(1, d), ffn_w_gate,
                       tm=256, seq=seq)
    act, wd = _ffn_up(h2, wg, wu, ffn_w_down, tm=1024, tf=512)
    return _ffn_down(act, hs, wd, mod3, norm_ffn_post.reshape(1, d), tm=256, seq=seq)


def kernel(x, c, ctx, c_ctx, w_mod, b_mod, norm_mix_pre, norm_mix_post, norm_ffn_pre, norm_ffn_post,
           w_in, ret_decay, gla_a_up, gla_a_bias, ret_gn, gla_gn, w_up_ret, w_up_gla, w_out,
           ffn_w_gate, ffn_w_up, ffn_w_down):
    batch, seq, d = x.shape
    ctx_len = ctx.shape[1]
    depth = w_mod.shape[0]
    cos, sin = _rope_tables(seq)
    c_rows = jnp.zeros((16, d), F32).at[:batch].set(c.astype(F32)).at[batch].set(c_ctx.astype(F32))
    ctx2d = ctx.astype(F32).reshape(batch * ctx_len, d)
    h_state = x.astype(F32).reshape(batch * seq, d)
    for i in range(depth):
        h_state = _layer(h_state, c_rows, ctx2d, w_mod[i], b_mod[i], norm_mix_pre[i], norm_mix_post[i],
                         norm_ffn_pre[i], norm_ffn_post[i], w_in[i], ret_decay[i], gla_a_up[i], gla_a_bias[i],
                         ret_gn[i], gla_gn[i], w_up_ret[i], w_up_gla[i], w_out[i],
                         ffn_w_gate[i], ffn_w_up[i], ffn_w_down[i], cos, sin,
                         batch=batch, seq=seq, ctx_len=ctx_len)
    return h_state.reshape(batch, seq, d).astype(x.dtype)
```

```python
import functools

import jax
import jax.numpy as jnp
import numpy as np
from jax import lax
from jax.experimental import pallas as pl
from jax.experimental.pallas import tpu as pltpu

F32 = jnp.float32
BF16 = jnp.bfloat16

D_MODEL = 2048
GRID_W = 64
RET_HEADS = 4
RET_DK = 256
RET_DV = 256
GLA_HEADS = 4
GLA_DK = 128
GLA_DV = 256
GLA_LOW_RANK = 16
GLA_GATE_NORM = 16.0
ROPE_BASE = 10000.0
EPS = 1e-6

RET_QK = RET_HEADS * RET_DK
RET_VW = RET_HEADS * RET_DV
GLA_KW = GLA_HEADS * GLA_DK
GLA_VW = GLA_HEADS * GLA_DV
GA_W = 2 * GLA_LOW_RANK
LANE = 128

_SRC = {}
_off = 0
for _name, _w in (("rk", RET_QK), ("rv", RET_VW), ("gk", GLA_KW), ("gv", GLA_VW), ("ga", GA_W),
                  ("rq", RET_QK), ("rg", RET_VW), ("gq", GLA_KW), ("gg", GLA_VW),
                  ("gate_a", D_MODEL), ("gate_b", D_MODEL)):
    _SRC[_name] = (_off, _w)
    _off += _w

_ORDER = ("rk", "rv", "gv", "gk", "gq", "rq", "rg", "gg", "gate_a", "gate_b")
_DST = {}
_off = 0
for _name in _ORDER:
    _DST[_name] = _off
    _off += _SRC[_name][1]
N_MAIN = _off
N_CTX = _DST["gq"]
PACK_W = 512

VMEM_LIMIT = 56 * 1024 * 1024


def _dot(a, b):
    return jnp.dot(a, b, preferred_element_type=F32)


def _dot_nt(a, b):
    return lax.dot_general(a, b, (((1,), (1,)), ((), ())), preferred_element_type=F32)


def _dot_tn(a, b):
    return lax.dot_general(a, b, (((0,), (0,)), ((), ())), preferred_element_type=F32)


def _sigmoid(x):
    return 1.0 / (1.0 + jnp.exp(-x))


def _rms(x):
    return x * lax.rsqrt(jnp.mean(x * x, axis=-1, keepdims=True) + EPS)


def _run_staged(*stages):
    live = list(stages)
    while live:
        live = [g for g in live if next(g, live) is not live]


def _split_bf16(x):
    hi = x.astype(BF16)
    lo = (x - hi.astype(F32)).astype(BF16)
    return hi, lo


MOD_TN = 1024


def _prologue_kernel(a_idx_ref, n_idx_ref, shift_ref, a_ref, n_ref, c_ref, wm_ref, bm_ref,
                     o_ref, mod_ref, *, n_mod):
    j = pl.program_id(0)

    @pl.when(shift_ref[j] == 0)
    def _():
        o_ref[...] = a_ref[...].astype(o_ref.dtype)

    @pl.when(shift_ref[j] != 0)
    def _():
        o_ref[...] = jnp.concatenate([a_ref[GA_W:, :], n_ref[...]], axis=0).astype(o_ref.dtype)

    @pl.when(j < n_mod)
    def _():
        cf = c_ref[...]
        rows = cf.shape[0]
        s_hi, s_lo = _split_bf16(cf * _sigmoid(cf))
        w_hi, w_lo = _split_bf16(wm_ref[...])
        r = _dot(jnp.concatenate([s_hi, s_lo], axis=0), w_hi)
        mod_ref[...] = r[:rows] + r[rows:] + _dot(s_hi, w_lo) + bm_ref[...]


def _prologue(w_t, c_rows, w_mod, b_mod):
    d = w_t.shape[1]
    rows = c_rows.shape[0]
    n_modcols = w_mod.shape[1]
    n_mod = n_modcols // MOD_TN
    a_idx, n_idx, shift = [], [], []
    for name in _ORDER:
        src, width = _SRC[name]
        for c in range(src, src + width, PACK_W):
            base = c - c % PACK_W
            assert c - base in (0, GA_W)
            a_idx.append(base // PACK_W)
            n_idx.append((base + PACK_W) // GA_W if c != base else (n_idx[-1] if n_idx else 0))
            shift.append(c - base)
    n_tiles = len(a_idx)
    assert n_mod <= n_tiles
    as_i32 = lambda v: jnp.asarray(v, jnp.int32)
    mod_tile = lambda j, a, n, s: (0, jnp.minimum(j, n_mod - 1))
    return pl.pallas_call(
        functools.partial(_prologue_kernel, n_mod=n_mod),
        out_shape=(jax.ShapeDtypeStruct((n_tiles * PACK_W, d), BF16),
                   jax.ShapeDtypeStruct((rows, n_modcols), F32)),
        grid_spec=pltpu.PrefetchScalarGridSpec(
            num_scalar_prefetch=3,
            grid=(n_tiles,),
            in_specs=[pl.BlockSpec((PACK_W, d), lambda j, a, n, s: (a[j], 0)),
                      pl.BlockSpec((GA_W, d), lambda j, a, n, s: (n[j], 0)),
                      pl.BlockSpec((rows, d), lambda j, a, n, s: (0, 0)),
                      pl.BlockSpec((d, MOD_TN), mod_tile),
                      pl.BlockSpec((1, MOD_TN), mod_tile)],
            out_specs=(pl.BlockSpec((PACK_W, d), lambda j, a, n, s: (j, 0)),
                       pl.BlockSpec((rows, MOD_TN), mod_tile)),
        ),
        compiler_params=pltpu.CompilerParams(dimension_semantics=("arbitrary",),
                                             vmem_limit_bytes=VMEM_LIMIT),
        name="prologue",
    )(as_i32(a_idx), as_i32(n_idx), as_i32(shift), w_t, w_t, c_rows, w_mod, b_mod.reshape(1, n_modcols))


def _inproj_kernel(x_ref, nw_ref, sh_ref, sc_ref, w_ref, wga_ref, cos_ref, sin_ref, *rest,
                   rope_tiles, tn, n_cast):
    cast_in = rest[:n_cast]
    o_ref, ga_ref = rest[n_cast:n_cast + 2]
    cast_out = rest[n_cast + 2:2 * n_cast + 2]
    h_scr = rest[2 * n_cast + 2]
    j = pl.program_id(1)

    def ride_along_casts():
        for src, dst in zip(cast_in, cast_out):
            dst[...] = src[...].astype(dst.dtype)

    @pl.when(j == 0)
    def _():
        h = _rms(x_ref[...]) * (nw_ref[...] * (1.0 + sc_ref[...])) + sh_ref[...]
        hb = h.astype(BF16)
        h_scr[...] = hb
        g3 = _dot_nt(hb, wga_ref[...])
        lane = lax.broadcasted_iota(jnp.int32, g3.shape, 1)
        resid = g3 - g3.astype(BF16).astype(F32)
        ga_ref[...] = jnp.where((lane >= GA_W) & (lane < 2 * GA_W), resid, g3).astype(ga_ref.dtype)

    if not rope_tiles:
        ride_along_casts()
        o_ref[...] = _dot_nt(h_scr[...], w_ref[...]).astype(o_ref.dtype)
        return

    is_rope = functools.reduce(jnp.logical_or, [j == t for t in rope_tiles])

    @pl.when(is_rope)
    def _():
        ride_along_casts()
        cos = cos_ref[...]
        sin = sin_ref[...]
        half = RET_DK // 2
        for hd in range(0, tn, RET_DK):
            acc = _dot_nt(h_scr[...], w_ref[hd:hd + RET_DK, :])
            t1 = acc[:, :half]
            t2 = acc[:, half:]
            o_ref[:, hd:hd + half] = (t1 * cos - t2 * sin).astype(o_ref.dtype)
            o_ref[:, hd + half:hd + RET_DK] = (t1 * sin + t2 * cos).astype(o_ref.dtype)

    @pl.when(jnp.logical_not(is_rope))
    def _():
        ride_along_casts()
        o_ref[...] = _dot_nt(h_scr[...], w_ref[...]).astype(o_ref.dtype)


def _cast_plan(arr, n_i, n_j):
    rows, cols = arr.shape

    def pieces(extent, want, align):
        n = want
        while extent % n or (extent // n) % align:
            n -= 1
        return n

    if rows % n_j == 0 and (rows // n_j) % 16 == 0:
        nr, nc = n_j, pieces(cols, n_i, LANE)
        index = lambda i, j, nc=nc: (j, jnp.minimum(i, nc - 1))
    else:
        nr, nc = pieces(rows, n_i, 16), pieces(cols, n_j, LANE)
        index = lambda i, j, nr=nr, nc=nc: (jnp.minimum(i, nr - 1), jnp.minimum(j, nc - 1))
    return (rows // nr, cols // nc), index


def _inproj(x2d, norm_w, mod3, mod_row_of_tile, w_main, w_ga, cos, sin, *, tm, tn, ncols, rope, seq_tiles,
            casts=()):
    m, d = x2d.shape
    rope_tiles = ()
    if rope:
        assert tn == RET_QK and _DST["rk"] % tn == 0 and _DST["rq"] % tn == 0
        rope_tiles = (_DST["rk"] // tn, _DST["rq"] // tn)
    grid = (m // tm, ncols // tn)
    cast_specs = [pl.BlockSpec(*_cast_plan(a, *grid)) for a in casts]
    kern = functools.partial(_inproj_kernel, rope_tiles=rope_tiles, tn=tn, n_cast=len(casts))
    return pl.pallas_call(
        kern,
        out_shape=(jax.ShapeDtypeStruct((m, ncols), BF16),
                   jax.ShapeDtypeStruct((m, LANE), BF16),
                   *[jax.ShapeDtypeStruct(a.shape, BF16) for a in casts]),
        grid=grid,
        in_specs=[
            pl.BlockSpec((tm, d), lambda i, j: (i, 0)),
            pl.BlockSpec((1, d), lambda i, j: (0, 0)),
            pl.BlockSpec((None, 1, d), lambda i, j: (mod_row_of_tile(i), 0, 0)),
            pl.BlockSpec((None, 1, d), lambda i, j: (mod_row_of_tile(i), 0, 1)),
            pl.BlockSpec((tn, d), lambda i, j: (j, 0)),
            pl.BlockSpec((LANE, d), lambda i, j: (0, 0)),
            pl.BlockSpec((tm, RET_DK // 2), lambda i, j: (i % seq_tiles, 0)),
            pl.BlockSpec((tm, RET_DK // 2), lambda i, j: (i % seq_tiles, 0)),
            *cast_specs,
        ],
        out_specs=(pl.BlockSpec((tm, tn), lambda i, j: (i, j)),
                   pl.BlockSpec((tm, LANE), lambda i, j: (i, 0)),
                   *cast_specs),
        scratch_shapes=[pltpu.VMEM((tm, d), BF16)],
        compiler_params=pltpu.CompilerParams(dimension_semantics=("arbitrary", "arbitrary"),
                                             vmem_limit_bytes=VMEM_LIMIT),
        name="inproj",
    )(x2d, norm_w, mod3, mod3, w_main, w_ga, cos, sin, *casts)


RET_CHUNK = 256


def _ride_along_casts(rest, n_cast):
    for src, dst in zip(rest[:n_cast], rest[n_cast + 1:2 * n_cast + 1]):
        dst[...] = src[...].astype(dst.dtype)
    return rest[n_cast], rest[2 * n_cast + 1:]


def _ret_kernel(q_ref, k_ref, v_ref, g_ref, kc_ref, vc_ref, rd_ref, gn_ref, *rest, n_chunks, n_cast):
    o_ref, (o_scr, sc_scr, s_scr, dm_scr, eq_scr, wk_scr, sd_scr) = _ride_along_casts(rest, n_cast)
    c_len = RET_CHUNK
    ii = lax.broadcasted_iota(jnp.int32, (c_len, c_len), 0)
    jj = lax.broadcasted_iota(jnp.int32, (c_len, c_len), 1)
    rowi = lax.broadcasted_iota(jnp.int32, (c_len, RET_DK), 0).astype(F32)
    scale = RET_DK ** -0.5

    lg_f = -jnp.exp(rd_ref[0][0:1, :])
    lg_b = -jnp.exp(rd_ref[1][0:1, :])
    dm_scr[...] = jnp.where(jj <= ii, jnp.exp((ii - jj).astype(F32) * lg_f),
                            jnp.exp((jj - ii).astype(F32) * lg_b))
    for direction, lg in ((0, lg_f), (1, lg_b)):
        if direction == 0:
            eq_scr[direction] = jnp.exp((rowi + 1.0) * lg)
            wk = jnp.exp((c_len - 1.0 - rowi) * lg)
        else:
            eq_scr[direction] = jnp.exp((c_len - rowi) * lg)
            wk = jnp.exp(rowi * lg)
        wk_scr[direction] = wk
        sd_scr[direction] = jnp.broadcast_to(jnp.exp(float(c_len) * lg), (8, RET_DV))
        kcw = (kc_ref[...].astype(F32) * wk).astype(BF16)
        s_scr[direction] = _dot_tn(kcw, vc_ref[...])

    def rows_of(c):
        return pl.ds(pl.multiple_of(c * c_len, c_len), c_len)

    def score_stage(c):
        rows = rows_of(c)
        s_qk = _dot_nt(q_ref[rows, :], k_ref[rows, :])
        yield
        sc_scr[rows, :] = (s_qk * dm_scr[...]).astype(BF16)
        yield

    def value_stage(c):
        rows = rows_of(c)
        o_scr[rows, :] = _dot(sc_scr[rows, :], v_ref[rows, :])
        yield

    _run_staged(score_stage(0))

    def intra_body(t, carry):
        _run_staged(score_stage(t + 1), value_stage(t))
        return carry

    lax.fori_loop(0, n_chunks - 1, intra_body, 0, unroll=n_chunks - 1)
    _run_staged(value_stage(n_chunks - 1))

    def scan_chunk(direction, c, done):
        rows = rows_of(c)
        q = q_ref[rows, :]
        k = k_ref[rows, :]
        v = v_ref[rows, :]
        s_state = s_scr[direction]
        kv = _dot_tn((k.astype(F32) * wk_scr[direction]).astype(BF16), v)
        o_state = _dot(q, s_state.astype(BF16))
        yield
        s_scr[direction] = sd_scr[direction][0:1, :] * s_state + kv
        tot = o_scr[rows, :] + eq_scr[direction] * o_state
        if not done:
            o_scr[rows, :] = tot
        else:
            mu = jnp.mean(tot, axis=-1, keepdims=True)
            cen = tot - mu
            var = jnp.mean(cen * cen, axis=-1, keepdims=True)
            nrm = cen * (scale * lax.rsqrt(var * (scale * scale) + EPS))
            g = g_ref[rows, :]
            o_ref[rows, :] = (nrm * gn_ref[...] * (g * _sigmoid(g)).astype(F32)).astype(o_ref.dtype)
        yield

    def body(t, carry, done):
        _run_staged(scan_chunk(0, t, done), scan_chunk(1, n_chunks - 1 - t, done))
        return carry

    lax.fori_loop(0, n_chunks // 2, functools.partial(body, done=False), 0, unroll=n_chunks // 2)
    lax.fori_loop(n_chunks // 2, n_chunks, functools.partial(body, done=True), 0, unroll=n_chunks // 2)


def _retention(p, pc, rd_b, ret_gn, casts, *, batch, seq, ctx_len):
    assert ctx_len == RET_CHUNK and (seq // RET_CHUNK) % 2 == 0
    n_chunks = seq // RET_CHUNK
    w = RET_DK
    kern = functools.partial(_ret_kernel, n_chunks=n_chunks, n_cast=len(casts))
    cast_specs = [pl.BlockSpec(*_cast_plan(a, batch, RET_HEADS)) for a in casts]

    def col(name):
        return _DST[name] // w

    return pl.pallas_call(
        kern,
        out_shape=(jax.ShapeDtypeStruct((batch * seq, RET_VW), BF16),
                   *[jax.ShapeDtypeStruct(a.shape, BF16) for a in casts]),
        grid=(batch, RET_HEADS),
        in_specs=[
            pl.BlockSpec((seq, w), lambda b, h: (b, col("rq") + h)),
            pl.BlockSpec((seq, w), lambda b, h: (b, col("rk") + h)),
            pl.BlockSpec((seq, w), lambda b, h: (b, col("rv") + h)),
            pl.BlockSpec((seq, w), lambda b, h: (b, col("rg") + h)),
            pl.BlockSpec((ctx_len, w), lambda b, h: (b, col("rk") + h)),
            pl.BlockSpec((ctx_len, w), lambda b, h: (b, col("rv") + h)),
            pl.BlockSpec((2, None, 8, w), lambda b, h: (0, h, 0, 0)),
            pl.BlockSpec((1, w), lambda b, h: (0, h)),
            *cast_specs,
        ],
        out_specs=(pl.BlockSpec((seq, w), lambda b, h: (b, h)), *cast_specs),
        scratch_shapes=[
            pltpu.VMEM((seq, RET_DV), F32),
            pltpu.VMEM((seq, RET_CHUNK), BF16),
            pltpu.VMEM((2, RET_DK, RET_DV), F32),
            pltpu.VMEM((RET_CHUNK, RET_CHUNK), F32),
            pltpu.VMEM((2, RET_CHUNK, RET_DV), F32),
            pltpu.VMEM((2, RET_CHUNK, RET_DK), F32),
            pltpu.VMEM((2, 8, RET_DV), F32),
        ],
        compiler_params=pltpu.CompilerParams(dimension_semantics=("arbitrary", "arbitrary"),
                                             vmem_limit_bytes=VMEM_LIMIT),
        name="retention",
    )(p, p, p, p, pc, pc, rd_b, ret_gn, *casts)


GLA_SUB = 64
GLA_BLK = 256
GLA_SCAN_BLK = 512
PREP_UNROLL = 14


def _gla_kernel(q_ref, k_ref, v_ref, g_ref, ga_ref, kc_ref, vc_ref, gac_ref, up_ref, bias_ref, gn_ref,
                *rest, n_blk, n_cast):
    o_ref, scratch = _ride_along_casts(rest, n_cast)
    (ahi_scr, alo_scr, qf_scr, qb_scr, ktf_scr, ktb_scr, khf_scr, khb_scr, khcf_scr, khcb_scr,
     dec_scr, decc_scr, o_scr, s_scr) = scratch
    blk, sub, dk = GLA_BLK, GLA_SUB, GLA_DK
    nsub = blk // sub
    shift = sub.bit_length() - 1
    scale = GLA_DK ** -0.5

    def masks(n):
        ii = lax.broadcasted_iota(jnp.int32, (n, n), 0)
        jj = lax.broadcasted_iota(jnp.int32, (n, n), 1)
        same = (ii >> shift) == (jj >> shift)
        lower = jj <= ii
        return same, lower, jnp.where(same & lower, 1.0, 0.0).astype(BF16)

    ctx_rows = gac_ref.shape[0]
    blk_masks = masks(blk)
    ctx_masks = blk_masks if ctx_rows == blk else masks(ctx_rows)
    r8 = lax.broadcasted_iota(jnp.int32, (8, dk), 0)

    u = up_ref[...]
    u_hi, u_lo = _split_bf16(u)
    urow = lax.broadcasted_iota(jnp.int32, u.shape, 0)
    rhs = jnp.where(urow >= 2 * GA_W, u_lo, u_hi)
    bias = bias_ref[...]

    def gate_stage(ga_blk, put):
        z = _dot(ga_blk, rhs) + bias
        yield
        a = (jnp.minimum(z, 0.0) - jnp.log(1.0 + jnp.exp(-jnp.abs(z)))) * (1.0 / GLA_GATE_NORM)
        put(*_split_bf16(a))
        yield

    def decay_stage(a_hi, a_lo, k_blk, q_blk, msk, store):
        same, lower, tmat = msk
        nsub = a_hi.shape[0] // sub
        assert nsub <= 8
        pre = _dot(tmat, a_hi) + _dot(tmat, a_lo)
        yield
        a = a_hi.astype(F32) + a_lo.astype(F32)
        lasts = [pre[i * sub + sub - 1:i * sub + sub, :] for i in range(nsub)]
        tot = jnp.concatenate([jnp.broadcast_to(l, (sub, 2 * dk)) for l in lasts], axis=0)
        b_f = pre[:, :dk]
        ex_b = pre[:, dk:] - a[:, dk:]
        b_b = tot[:, dk:] - ex_b
        kf = k_blk.astype(F32)
        khf = (kf * jnp.exp(tot[:, :dk] - b_f)).astype(BF16)
        khb = (kf * jnp.exp(ex_b)).astype(BF16)

        def tile8(half):
            rows = [jnp.broadcast_to(l[:, half * dk:(half + 1) * dk], (8, dk)) for l in lasts]
            t = rows[nsub - 1]
            for i in range(nsub - 2, -1, -1):
                t = jnp.where(r8 == i, rows[i], t)
            return t

        dec = jnp.exp(jnp.concatenate([tile8(0), tile8(1), jnp.zeros((LANE - 16, dk), F32)], axis=0))
        out = dict(khf=khf, khb=khb, dec_t=dec.T)
        if q_blk is not None:
            qf32 = q_blk.astype(F32)
            out.update(qf=(qf32 * jnp.exp(b_f)).astype(BF16), qb=(qf32 * jnp.exp(b_b)).astype(BF16),
                       ktf=(kf * jnp.exp(-b_f)).astype(BF16), ktb=(kf * jnp.exp(-b_b)).astype(BF16))
        store(**out)
        yield

    def score_stage(r):
        same, lower, _ = blk_masks
        rows = pl.ds(r, blk)
        s_f = _dot_nt(qf_scr[rows, :], ktf_scr[rows, :])
        s_b = _dot_nt(qb_scr[rows, :], ktb_scr[rows, :])
        yield
        s = jnp.where(same, jnp.where(lower, s_f, s_b), 0.0).astype(BF16)
        yield
        o_scr[rows, :] = _dot(s, v_ref[rows, :])
        yield

    run = _run_staged

    def store_ctx(khf, khb, dec_t):
        khcf_scr[...] = khf
        khcb_scr[...] = khb
        decc_scr[...] = dec_t

    ctx_gate = []
    run(gate_stage(gac_ref[...], lambda hi, lo: ctx_gate.extend((hi, lo))))
    run(decay_stage(*ctx_gate, kc_ref[...], None, ctx_masks, store_ctx))

    def gate_block(t):
        rows = pl.ds(pl.multiple_of(t * blk, blk), blk)

        def put(a_hi, a_lo):
            ahi_scr[rows, :] = a_hi
            alo_scr[rows, :] = a_lo

        return gate_stage(ga_ref[rows, :], put)

    def decay_block(t):
        rows = pl.ds(pl.multiple_of(t * blk, blk), blk)

        def store(khf, khb, dec_t, qf, qb, ktf, ktb):
            khf_scr[rows, :] = khf
            khb_scr[rows, :] = khb
            qf_scr[rows, :] = qf
            qb_scr[rows, :] = qb
            ktf_scr[rows, :] = ktf
            ktb_scr[rows, :] = ktb
            dec_scr[t] = dec_t

        return decay_stage(ahi_scr[rows, :], alo_scr[rows, :], k_ref[rows, :], q_ref[rows, :],
                           blk_masks, store)

    def score_block(t):
        return score_stage(pl.multiple_of(t * blk, blk))

    run(gate_block(0))
    run(gate_block(1), decay_block(0))

    def prep_body(t, carry):
        run(gate_block(t + 2), decay_block(t + 1), score_block(t))
        return carry

    lax.fori_loop(0, n_blk - 2, prep_body, 0, unroll=PREP_UNROLL)
    run(decay_block(n_blk - 1), score_block(n_blk - 2))
    run(score_block(n_blk - 1))

    def scan_blk(direction, n_sub, dec_ts, kh_at, v_at, q_at=None, o_rows=None):
        s = s_scr[direction]
        order = range(n_sub) if direction == 0 else range(n_sub - 1, -1, -1)
        kv = {i: _dot_tn(kh_at(i), v_at(i)) for i in order}
        yield
        for i in order:
            c = direction * 8 + i % nsub
            col = dec_ts[i // nsub][:, c:c + 1]
            if q_at is not None:
                rows = o_rows(i)
                o_scr[rows, :] += _dot(q_at(i), s.astype(BF16))
            s = s * col + kv[i]
            yield
        s_scr[direction] = s

    s_scr[...] = jnp.zeros(s_scr.shape, F32)
    run(*[scan_blk(direction, ctx_rows // sub, [decc_scr[...]],
                   lambda i, khc=khc: khc[i * sub:(i + 1) * sub, :],
                   lambda i: vc_ref[i * sub:(i + 1) * sub, :])
          for direction, khc in ((0, khcf_scr), (1, khcb_scr))])

    sblk = GLA_SCAN_BLK
    per = sblk // blk
    n_sblk = n_blk // per

    fin_rows = 128

    def finalize(t):
        for tb in (t, n_sblk - 1 - t):
            for r0 in range(0, sblk, fin_rows):
                rows = pl.ds(pl.multiple_of(tb * sblk + r0, fin_rows), fin_rows)
                o = o_scr[rows, :]
                ms = jnp.mean(o * o, axis=-1, keepdims=True)
                nrm = o * (scale * lax.rsqrt(ms * (scale * scale) + EPS))
                g = g_ref[rows, :]
                o_ref[rows, :] = (nrm * gn_ref[...] * (g * _sigmoid(g)).astype(F32)).astype(o_ref.dtype)
                yield

    def scans(t):
        for direction, q_scr, kh_scr in ((0, qf_scr, khf_scr), (1, qb_scr, khb_scr)):
            tb = t if direction == 0 else n_sblk - 1 - t

            def rows(i, tb=tb):
                return pl.ds(pl.multiple_of(tb * sblk + i * sub, sub), sub)

            yield from scan_blk(direction, sblk // sub, [dec_scr[tb * per + j] for j in range(per)],
                                lambda i, kh_scr=kh_scr, rows=rows: kh_scr[rows(i), :],
                                lambda i, rows=rows: v_ref[rows(i), :],
                                lambda i, q_scr=q_scr, rows=rows: q_scr[rows(i), :],
                                rows)

    def scan_body(t, carry, fin_prev):
        if fin_prev:
            run(scans(t), finalize(t - 1))
        else:
            run(scans(t))
        return carry

    half = n_sblk // 2
    lax.fori_loop(0, half + 1, functools.partial(scan_body, fin_prev=False), 0, unroll=half + 1)
    lax.fori_loop(half + 1, n_sblk, functools.partial(scan_body, fin_prev=True), 0, unroll=n_sblk - half - 1)
    run(finalize(n_sblk - 1))


def _gla(p, ga, pc, gac, up_stack, bias2, gla_gn, casts, *, batch, seq, ctx_len):
    assert ctx_len % GLA_SUB == 0 and ctx_len <= GLA_BLK and GLA_SCAN_BLK % GLA_BLK == 0
    assert (seq // GLA_SCAN_BLK) % 2 == 0
    n_blk = seq // GLA_BLK
    kern = functools.partial(_gla_kernel, n_blk=n_blk, n_cast=len(casts))
    dk, dv = GLA_DK, GLA_DV
    cast_specs = [pl.BlockSpec(*_cast_plan(a, batch, GLA_HEADS)) for a in casts]
    return pl.pallas_call(
        kern,
        out_shape=(jax.ShapeDtypeStruct((batch * seq, GLA_VW), BF16),
                   *[jax.ShapeDtypeStruct(a.shape, BF16) for a in casts]),
        grid=(batch, GLA_HEADS),
        in_specs=[
            pl.BlockSpec((seq, dk), lambda b, h: (b, _DST["gq"] // dk + h)),
            pl.BlockSpec((seq, dk), lambda b, h: (b, _DST["gk"] // dk + h)),
            pl.BlockSpec((seq, dv), lambda b, h: (b, _DST["gv"] // dv + h)),
            pl.BlockSpec((seq, dv), lambda b, h: (b, _DST["gg"] // dv + h)),
            pl.BlockSpec((seq, LANE), lambda b, h: (b, 0)),
            pl.BlockSpec((ctx_len, dk), lambda b, h: (b, _DST["gk"] // dk + h)),
            pl.BlockSpec((ctx_len, dv), lambda b, h: (b, _DST["gv"] // dv + h)),
            pl.BlockSpec((ctx_len, LANE), lambda b, h: (b, 0)),
            pl.BlockSpec((LANE, 2 * dk), lambda b, h: (0, h)),
            pl.BlockSpec((1, 2 * dk), lambda b, h: (0, h)),
            pl.BlockSpec((1, dv), lambda b, h: (0, h)),
            *cast_specs,
        ],
        out_specs=(pl.BlockSpec((seq, dv), lambda b, h: (b, h)), *cast_specs),
        scratch_shapes=[
            pltpu.VMEM((seq, 2 * dk), BF16), pltpu.VMEM((seq, 2 * dk), BF16),
            pltpu.VMEM((seq, dk), BF16), pltpu.VMEM((seq, dk), BF16),
            pltpu.VMEM((seq, dk), BF16), pltpu.VMEM((seq, dk), BF16),
            pltpu.VMEM((seq, dk), BF16), pltpu.VMEM((seq, dk), BF16),
            pltpu.VMEM((ctx_len, dk), BF16), pltpu.VMEM((ctx_len, dk), BF16),
            pltpu.VMEM((n_blk, dk, LANE), F32), pltpu.VMEM((dk, LANE), F32),
            pltpu.VMEM((seq, dv), F32),
            pltpu.VMEM((2, dk, dv), F32),
        ],
        compiler_params=pltpu.CompilerParams(dimension_semantics=("arbitrary", "arbitrary"),
                                             vmem_limit_bytes=VMEM_LIMIT),
        name="gla",
    )(p, p, p, p, ga, pc, pc, gac, up_stack, bias2, gla_gn, *casts)


MXU_N = 256


POST_ROWS = 128


def _post_kernel(or_ref, og_ref, ga0_ref, ga1_ref, gb0_ref, gb1_ref, x_ref, wur_ref, wug_ref, wo_ref,
                 g1_ref, sh2_ref, sc2_ref, npost_ref, npre_ref, wsrc_ref, hs_ref, h2_ref, wdst_ref):
    wdst_ref[...] = wsrc_ref[...].astype(wdst_ref.dtype)
    half = ga0_ref.shape[1]
    tm = hs_ref.shape[0]
    groups = [slice(r0, r0 + POST_ROWS) for r0 in range(0, tm, POST_ROWS)]

    def up(rows):
        return _dot(or_ref[rows, :], wur_ref[...]), _dot(og_ref[rows, :], wug_ref[...])

    def out(rows, y_ret, y_gla):
        def merge(ga_r, gb_r, lo):
            return (_sigmoid(ga_r[rows, :].astype(F32)) * y_ret[:, lo:lo + half]
                    + _sigmoid(gb_r[rows, :].astype(F32)) * y_gla[:, lo:lo + half]).astype(BF16)

        merged = jnp.concatenate([merge(ga0_ref, gb0_ref, 0), merge(ga1_ref, gb1_ref, half)], axis=1)
        return _dot(merged, wo_ref[...])

    post_gain = g1_ref[...] * npost_ref[...]
    pre_gain = npre_ref[...] * (1.0 + sc2_ref[...])

    def finish(rows, y):
        hs = x_ref[rows, :] + _rms(y) * post_gain
        hs_ref[rows, :] = hs
        h2 = _rms(hs) * pre_gain + sh2_ref[...]
        h2_ref[rows, :] = h2.astype(h2_ref.dtype)

    ups = [up(rows) for rows in groups]
    ys = [out(rows, *u) for rows, u in zip(groups, ups)]
    for rows, y in zip(groups, ys):
        finish(rows, y)


def _post(o_r, o_g, p, x2d, w_up_ret, w_up_gla, w_out, mod3, npost, npre, w_later, *, tm, seq):
    m, d = x2d.shape
    tiles_per_seq = seq // tm
    const = dict(pipeline_mode=pl.Buffered(1))
    gw = d // 2
    n_steps = m // tm
    assert w_later.shape[0] % (16 * n_steps) == 0
    cast_spec = pl.BlockSpec((w_later.shape[0] // n_steps, w_later.shape[1]), lambda i: (i, 0))

    def modspec(chunk):
        return pl.BlockSpec((None, 1, d), lambda i: (i // tiles_per_seq, 0, chunk))

    def gatespec(name, part):
        return pl.BlockSpec((tm, gw), lambda i: (i, _DST[name] // gw + part))

    return pl.pallas_call(
        _post_kernel,
        out_shape=(jax.ShapeDtypeStruct((m, d), F32), jax.ShapeDtypeStruct((m, d), BF16),
                   jax.ShapeDtypeStruct(w_later.shape, BF16)),
        grid=(n_steps,),
        in_specs=[
            pl.BlockSpec((tm, RET_VW), lambda i: (i, 0)),
            pl.BlockSpec((tm, GLA_VW), lambda i: (i, 0)),
            gatespec("gate_a", 0), gatespec("gate_a", 1), gatespec("gate_b", 0), gatespec("gate_b", 1),
            pl.BlockSpec((tm, d), lambda i: (i, 0)),
            pl.BlockSpec((RET_VW, d), lambda i: (0, 0), **const),
            pl.BlockSpec((GLA_VW, d), lambda i: (0, 0), **const),
            pl.BlockSpec((d, d), lambda i: (0, 0), **const),
            modspec(2), modspec(3), modspec(4),
            pl.BlockSpec((1, d), lambda i: (0, 0)),
            pl.BlockSpec((1, d), lambda i: (0, 0)),
            cast_spec,
        ],
        out_specs=(pl.BlockSpec((tm, d), lambda i: (i, 0)),
                   pl.BlockSpec((tm, d), lambda i: (i, 0)),
                   cast_spec),
        compiler_params=pltpu.CompilerParams(dimension_semantics=("arbitrary",),
                                             vmem_limit_bytes=VMEM_LIMIT),
        name="post",
    )(o_r, o_g, p, p, p, p, x2d, w_up_ret, w_up_gla, w_out, mod3, mod3, mod3, npost, npre, w_later)


def _ffn_up_kernel(h_ref, wg_ref, wu_ref, wsrc_ref, a_ref, wdst_ref):
    wdst_ref[...] = wsrc_ref[...].astype(wdst_ref.dtype)
    h = h_ref[...]
    for lo in range(0, a_ref.shape[1], MXU_N):
        g = _dot(h, wg_ref[:, lo:lo + MXU_N])
        u = _dot(h, wu_ref[:, lo:lo + MXU_N])
        a_ref[:, lo:lo + MXU_N] = (g * _sigmoid(g) * u).astype(a_ref.dtype)


def _ffn_up(h2, wg, wu, w_later, *, tm, tf):
    m, d = h2.shape
    d_ff = wg.shape[1]
    grid = (m // tm, d_ff // tf)
    cast_spec = pl.BlockSpec(*_cast_plan(w_later, *grid))
    return pl.pallas_call(
        _ffn_up_kernel,
        out_shape=(jax.ShapeDtypeStruct((m, d_ff), BF16), jax.ShapeDtypeStruct(w_later.shape, BF16)),
        grid=grid,
        in_specs=[
            pl.BlockSpec((tm, d), lambda i, f: (i, 0)),
            pl.BlockSpec((d, tf), lambda i, f: (0, f)),
            pl.BlockSpec((d, tf), lambda i, f: (0, f)),
            cast_spec,
        ],
        out_specs=(pl.BlockSpec((tm, tf), lambda i, f: (i, f)), cast_spec),
        compiler_params=pltpu.CompilerParams(dimension_semantics=("arbitrary", "arbitrary"),
                                             vmem_limit_bytes=VMEM_LIMIT),
        name="ffn_up",
    )(h2, wg, wu, w_later)


def _ffn_down_kernel(a_ref, wd_ref, hs_ref, g2_ref, nw_ref, o_ref, y_scr, *, n_split):
    tm, d = o_ref.shape
    w = d // n_split
    ss = jnp.zeros((tm, 1), F32)
    for lo in range(0, d, w):
        y = _dot(a_ref[...], wd_ref[:, lo:lo + w])
        y_scr[:, lo:lo + w] = y
        ss = ss + jnp.sum(y * y, axis=-1, keepdims=True)
    inv = lax.rsqrt(ss * (1.0 / d) + EPS)
    o_ref[...] = hs_ref[...] + y_scr[...] * inv * (g2_ref[...] * nw_ref[...])


def _ffn_down(a, hs, wd, mod3, npost, *, tm, seq):
    m, d = hs.shape
    d_ff = wd.shape[0]
    tiles_per_seq = seq // tm
    kern = functools.partial(_ffn_down_kernel, n_split=4)
    return pl.pallas_call(
        kern,
        out_shape=jax.ShapeDtypeStruct((m, d), F32),
        grid=(m // tm,),
        in_specs=[
            pl.BlockSpec((tm, d_ff), lambda i: (i, 0)),
            pl.BlockSpec((d_ff, d), lambda i: (0, 0), pipeline_mode=pl.Buffered(1)),
            pl.BlockSpec((tm, d), lambda i: (i, 0)),
            pl.BlockSpec((None, 1, d), lambda i: (i // tiles_per_seq, 0, 5)),
            pl.BlockSpec((1, d), lambda i: (0, 0)),
        ],
        out_specs=pl.BlockSpec((tm, d), lambda i: (i, 0)),
        scratch_shapes=[pltpu.VMEM((tm, d), F32)],
        compiler_params=pltpu.CompilerParams(dimension_semantics=("arbitrary",),
                                             vmem_limit_bytes=VMEM_LIMIT),
        name="ffn_down",
    )(a, wd, hs, mod3, npost)


def _rope_tables(seq):
    f32 = np.float32
    rows = seq // GRID_W
    pos_r = np.repeat(np.arange(rows, dtype=f32), GRID_W)
    pos_c = np.tile(np.arange(GRID_W, dtype=f32), rows)
    n_f = RET_DK // 4
    inv = (f32(ROPE_BASE) ** (-np.arange(n_f, dtype=f32) / f32(n_f))).astype(f32)
    ang = np.concatenate([pos_r[:, None] * inv, pos_c[:, None] * inv], axis=-1).astype(f32)
    return jnp.asarray(np.cos(ang), F32), jnp.asarray(np.sin(ang), F32)


def _gate_map_layout(gla_a_up, gla_a_bias):
    r, h, dk = GLA_LOW_RANK, GLA_HEADS, GLA_DK
    u = jnp.zeros((GA_W, h, 2, dk), F32)
    u = u.at[:r, :, 0, :].set(gla_a_up[0].reshape(r, h, dk))
    u = u.at[r:, :, 1, :].set(gla_a_up[1].reshape(r, h, dk))
    u = u.reshape(GA_W, h * 2 * dk)
    up_stack = jnp.concatenate([u, u, u, jnp.zeros((LANE - 3 * GA_W, h * 2 * dk), F32)], axis=0)
    bias2 = jnp.stack([gla_a_bias[0].reshape(h, dk), gla_a_bias[1].reshape(h, dk)], axis=1)
    return up_stack, bias2.reshape(1, h * 2 * dk)


def _layer(h_state, c_rows, ctx2d, w_mod, b_mod, norm_mix_pre, norm_mix_post, norm_ffn_pre, norm_ffn_post,
           w_in, ret_decay, gla_a_up, gla_a_bias, ret_gn, gla_gn, w_up_ret, w_up_gla, w_out,
           ffn_w_gate, ffn_w_up, ffn_w_down, cos, sin, *, batch, seq, ctx_len):
    d = D_MODEL
    ga0 = _SRC["ga"][0]
    w_t = w_in.T
    w_g = w_t[ga0:ga0 + GA_W]
    w_ga = jnp.concatenate([w_g, w_g, w_g, jnp.zeros((LANE - 3 * GA_W, d), F32)], axis=0).astype(BF16)
    up_stack, bias2 = _gate_map_layout(gla_a_up, gla_a_bias)
    rd_b = jnp.broadcast_to(ret_decay.reshape(2, RET_HEADS, 1, 1), (2, RET_HEADS, 8, RET_DK))

    w_main, mod = _prologue(w_t, c_rows, w_mod, b_mod)
    mod3 = mod.reshape(mod.shape[0], 1, 6 * d)
    nw_pre = norm_mix_pre.reshape(1, d)

    tm = 1024
    p, ga, wur, wug, wo = _inproj(
        h_state, nw_pre, mod3, lambda i: i // (seq // tm), w_main, w_ga, cos, sin,
        tm=tm, tn=1024, ncols=N_MAIN, rope=True, seq_tiles=seq // tm,
        casts=(w_up_ret, w_up_gla, w_out))
    pc, gac = _inproj(ctx2d, nw_pre, mod3, lambda i: batch, w_main, w_ga, cos, sin,
                      tm=batch * ctx_len, tn=N_CTX // 2, ncols=N_CTX, rope=False, seq_tiles=1)

    (o_r,) = _retention(p, pc, rd_b, ret_gn.reshape(1, RET_VW), (), batch=batch, seq=seq, ctx_len=ctx_len)
    o_g, wu = _gla(p, ga, pc, gac, up_stack, bias2, gla_gn.reshape(1, GLA_VW), (ffn_w_up,),
                   batch=batch, seq=seq, ctx_len=ctx_len)

    hs, h2, wg = _post(o_r, o_g, p, h_state, wur, wug, wo,
                       mod3, norm_mix_post.reshape(1, d), norm_ffn_pre.reshape(1, d), ffn_w_gate,
                       tm=256, seq=seq)
    act, wd = _ffn_up(h2, wg, wu, ffn_w_down, tm=2048, tf=512)
    return _ffn_down(act, hs, wd, mod3, norm_ffn_post.reshape(1, d), tm=256, seq=seq)


def kernel(x, c, ctx, c_ctx, w_mod, b_mod, norm_mix_pre, norm_mix_post, norm_ffn_pre, norm_ffn_post,
           w_in, ret_decay, gla_a_up, gla_a_bias, ret_gn, gla_gn, w_up_ret, w_up_gla, w_out,
           ffn_w_gate, ffn_w_up, ffn_w_down):
    batch, seq, d = x.shape
    ctx_len = ctx.shape[1]
    depth = w_mod.shape[0]
    cos, sin = _rope_tables(seq)
    c_rows = jnp.zeros((16, d), F32).at[:batch].set(c.astype(F32)).at[batch].set(c_ctx.astype(F32))
    ctx2d = ctx.astype(F32).reshape(batch * ctx_len, d)
    h_state = x.astype(F32).reshape(batch * seq, d)
    for i in range(depth):
        h_state = _layer(h_state, c_rows, ctx2d, w_mod[i], b_mod[i], norm_mix_pre[i], norm_mix_post[i],
                         norm_ffn_pre[i], norm_ffn_post[i], w_in[i], ret_decay[i], gla_a_up[i], gla_a_bias[i],
                         ret_gn[i], gla_gn[i], w_up_ret[i], w_up_gla[i], w_out[i],
                         ffn_w_gate[i], ffn_w_up[i], ffn_w_down[i], cos, sin,
                         batch=batch, seq=seq, ctx_len=ctx_len)
    return h_state.reshape(batch, seq, d).astype(x.dtype)
```

```python
import functools

import jax
import jax.numpy as jnp
import numpy as np
from jax import lax
from jax.experimental import pallas as pl
from jax.experimental.pallas import tpu as pltpu

F32 = jnp.float32
BF16 = jnp.bfloat16

D_MODEL = 2048
GRID_W = 64
RET_HEADS = 4
RET_DK = 256
RET_DV = 256
GLA_HEADS = 4
GLA_DK = 128
GLA_DV = 256
GLA_LOW_RANK = 16
GLA_GATE_NORM = 16.0
ROPE_BASE = 10000.0
EPS = 1e-6

RET_QK = RET_HEADS * RET_DK
RET_VW = RET_HEADS * RET_DV
GLA_KW = GLA_HEADS * GLA_DK
GLA_VW = GLA_HEADS * GLA_DV
GA_W = 2 * GLA_LOW_RANK
LANE = 128

_SRC = {}
_off = 0
for _name, _w in (("rk", RET_QK), ("rv", RET_VW), ("gk", GLA_KW), ("gv", GLA_VW), ("ga", GA_W),
                  ("rq", RET_QK), ("rg", RET_VW), ("gq", GLA_KW), ("gg", GLA_VW),
                  ("gate_a", D_MODEL), ("gate_b", D_MODEL)):
    _SRC[_name] = (_off, _w)
    _off += _w

_ORDER = ("rk", "rv", "gv", "gk", "gq", "rq", "rg", "gg", "gate_a", "gate_b")
_DST = {}
_off = 0
for _name in _ORDER:
    _DST[_name] = _off
    _off += _SRC[_name][1]
N_MAIN = _off
N_CTX = _DST["gq"]
PACK_W = 512

VMEM_LIMIT = 56 * 1024 * 1024


def _dot(a, b):
    return jnp.dot(a, b, preferred_element_type=F32)


def _dot_nt(a, b):
    return lax.dot_general(a, b, (((1,), (1,)), ((), ())), preferred_element_type=F32)


def _dot_tn(a, b):
    return lax.dot_general(a, b, (((0,), (0,)), ((), ())), preferred_element_type=F32)


def _sigmoid(x):
    return 1.0 / (1.0 + jnp.exp(-x))


def _rms(x):
    return x * lax.rsqrt(jnp.mean(x * x, axis=-1, keepdims=True) + EPS)


def _run_staged(*stages):
    live = list(stages)
    while live:
        live = [g for g in live if next(g, live) is not live]


def _split_bf16(x):
    hi = x.astype(BF16)
    lo = (x - hi.astype(F32)).astype(BF16)
    return hi, lo


MOD_TN = 1024


MOD_BUFS = 3


def _prologue_kernel(a_idx_ref, n_idx_ref, shift_ref, a_ref, n_ref, c_ref, wm_hbm, bm_ref,
                     o_ref, mod_ref, wm_buf, wm_sem, *, n_mod):
    j = pl.program_id(0)

    def wm_copy(tile, slot):
        return pltpu.make_async_copy(wm_hbm.at[:, pl.ds(tile * MOD_TN, MOD_TN)], wm_buf.at[slot],
                                     wm_sem.at[slot])

    @pl.when(j == 0)
    def _():
        for t in range(min(MOD_BUFS, n_mod)):
            wm_copy(t, t).start()

    @pl.when(shift_ref[j] == 0)
    def _():
        o_ref[...] = a_ref[...].astype(o_ref.dtype)

    @pl.when(shift_ref[j] != 0)
    def _():
        o_ref[...] = jnp.concatenate([a_ref[GA_W:, :], n_ref[...]], axis=0).astype(o_ref.dtype)

    @pl.when(j < n_mod)
    def _():
        cf = c_ref[...]
        rows = cf.shape[0]
        s_hi, s_lo = _split_bf16(cf * _sigmoid(cf))
        slot = j % MOD_BUFS
        wm_copy(j, slot).wait()
        w_hi, w_lo = _split_bf16(wm_buf[slot])
        r = _dot(jnp.concatenate([s_hi, s_lo], axis=0), w_hi)
        mod_ref[...] = r[:rows] + r[rows:] + _dot(s_hi, w_lo) + bm_ref[...]

        @pl.when(j + MOD_BUFS < n_mod)
        def _():
            wm_copy(j + MOD_BUFS, slot).start()


def _prologue(w_t, c_rows, w_mod, b_mod):
    d = w_t.shape[1]
    rows = c_rows.shape[0]
    n_modcols = w_mod.shape[1]
    n_mod = n_modcols // MOD_TN
    a_idx, n_idx, shift = [], [], []
    for name in _ORDER:
        src, width = _SRC[name]
        for c in range(src, src + width, PACK_W):
            base = c - c % PACK_W
            assert c - base in (0, GA_W)
            a_idx.append(base // PACK_W)
            n_idx.append((base + PACK_W) // GA_W if c != base else (n_idx[-1] if n_idx else 0))
            shift.append(c - base)
    n_tiles = len(a_idx)
    assert n_mod <= n_tiles
    as_i32 = lambda v: jnp.asarray(v, jnp.int32)
    mod_tile = lambda j, a, n, s: (0, jnp.minimum(j, n_mod - 1))
    return pl.pallas_call(
        functools.partial(_prologue_kernel, n_mod=n_mod),
        out_shape=(jax.ShapeDtypeStruct((n_tiles * PACK_W, d), BF16),
                   jax.ShapeDtypeStruct((rows, n_modcols), F32)),
        grid_spec=pltpu.PrefetchScalarGridSpec(
            num_scalar_prefetch=3,
            grid=(n_tiles,),
            in_specs=[pl.BlockSpec((PACK_W, d), lambda j, a, n, s: (a[j], 0)),
                      pl.BlockSpec((GA_W, d), lambda j, a, n, s: (n[j], 0)),
                      pl.BlockSpec((rows, d), lambda j, a, n, s: (0, 0)),
                      pl.BlockSpec(memory_space=pl.ANY),
                      pl.BlockSpec((1, MOD_TN), mod_tile)],
            out_specs=(pl.BlockSpec((PACK_W, d), lambda j, a, n, s: (j, 0)),
                       pl.BlockSpec((rows, MOD_TN), mod_tile)),
            scratch_shapes=[pltpu.VMEM((MOD_BUFS, d, MOD_TN), F32),
                            pltpu.SemaphoreType.DMA((MOD_BUFS,))],
        ),
        compiler_params=pltpu.CompilerParams(dimension_semantics=("arbitrary",),
                                             vmem_limit_bytes=VMEM_LIMIT),
        name="prologue",
    )(as_i32(a_idx), as_i32(n_idx), as_i32(shift), w_t, w_t, c_rows, w_mod, b_mod.reshape(1, n_modcols))


def _inproj_kernel(x_ref, nw_ref, sh_ref, sc_ref, w_ref, wga_ref, cos_ref, sin_ref, *rest,
                   rope_tiles, tn, n_cast):
    cast_in = rest[:n_cast]
    o_ref, ga_ref = rest[n_cast:n_cast + 2]
    cast_out = rest[n_cast + 2:2 * n_cast + 2]
    h_scr = rest[2 * n_cast + 2]
    j = pl.program_id(1)

    def ride_along_casts():
        for src, dst in zip(cast_in, cast_out):
            dst[...] = src[...].astype(dst.dtype)

    @pl.when(j == 0)
    def _():
        h = _rms(x_ref[...]) * (nw_ref[...] * (1.0 + sc_ref[...])) + sh_ref[...]
        hb = h.astype(BF16)
        h_scr[...] = hb
        g3 = _dot_nt(hb, wga_ref[...])
        lane = lax.broadcasted_iota(jnp.int32, g3.shape, 1)
        resid = g3 - g3.astype(BF16).astype(F32)
        ga_ref[...] = jnp.where((lane >= GA_W) & (lane < 2 * GA_W), resid, g3).astype(ga_ref.dtype)

    if not rope_tiles:
        ride_along_casts()
        o_ref[...] = _dot_nt(h_scr[...], w_ref[...]).astype(o_ref.dtype)
        return

    is_rope = functools.reduce(jnp.logical_or, [j == t for t in rope_tiles])

    @pl.when(is_rope)
    def _():
        ride_along_casts()
        cos = cos_ref[...]
        sin = sin_ref[...]
        half = RET_DK // 2
        for hd in range(0, tn, RET_DK):
            acc = _dot_nt(h_scr[...], w_ref[hd:hd + RET_DK, :])
            t1 = acc[:, :half]
            t2 = acc[:, half:]
            o_ref[:, hd:hd + half] = (t1 * cos - t2 * sin).astype(o_ref.dtype)
            o_ref[:, hd + half:hd + RET_DK] = (t1 * sin + t2 * cos).astype(o_ref.dtype)

    @pl.when(jnp.logical_not(is_rope))
    def _():
        ride_along_casts()
        o_ref[...] = _dot_nt(h_scr[...], w_ref[...]).astype(o_ref.dtype)


def _cast_plan(arr, n_i, n_j):
    rows, cols = arr.shape

    def pieces(extent, want, align):
        n = want
        while extent % n or (extent // n) % align:
            n -= 1
        return n

    if rows % n_j == 0 and (rows // n_j) % 16 == 0:
        nr, nc = n_j, pieces(cols, n_i, LANE)
        index = lambda i, j, nc=nc: (j, jnp.minimum(i, nc - 1))
    else:
        nr, nc = pieces(rows, n_i, 16), pieces(cols, n_j, LANE)
        index = lambda i, j, nr=nr, nc=nc: (jnp.minimum(i, nr - 1), jnp.minimum(j, nc - 1))
    return (rows // nr, cols // nc), index


def _inproj(x2d, norm_w, mod3, mod_row_of_tile, w_main, w_ga, cos, sin, *, tm, tn, ncols, rope, seq_tiles,
            casts=()):
    m, d = x2d.shape
    rope_tiles = ()
    if rope:
        assert tn == RET_QK and _DST["rk"] % tn == 0 and _DST["rq"] % tn == 0
        rope_tiles = (_DST["rk"] // tn, _DST["rq"] // tn)
    grid = (m // tm, ncols // tn)
    cast_specs = [pl.BlockSpec(*_cast_plan(a, *grid)) for a in casts]
    kern = functools.partial(_inproj_kernel, rope_tiles=rope_tiles, tn=tn, n_cast=len(casts))
    return pl.pallas_call(
        kern,
        out_shape=(jax.ShapeDtypeStruct((m, ncols), BF16),
                   jax.ShapeDtypeStruct((m, LANE), BF16),
                   *[jax.ShapeDtypeStruct(a.shape, BF16) for a in casts]),
        grid=grid,
        in_specs=[
            pl.BlockSpec((tm, d), lambda i, j: (i, 0)),
            pl.BlockSpec((1, d), lambda i, j: (0, 0)),
            pl.BlockSpec((None, 1, d), lambda i, j: (mod_row_of_tile(i), 0, 0)),
            pl.BlockSpec((None, 1, d), lambda i, j: (mod_row_of_tile(i), 0, 1)),
            pl.BlockSpec((tn, d), lambda i, j: (j, 0)),
            pl.BlockSpec((LANE, d), lambda i, j: (0, 0)),
            pl.BlockSpec((tm, RET_DK // 2), lambda i, j: (i % seq_tiles, 0)),
            pl.BlockSpec((tm, RET_DK // 2), lambda i, j: (i % seq_tiles, 0)),
            *cast_specs,
        ],
        out_specs=(pl.BlockSpec((tm, tn), lambda i, j: (i, j)),
                   pl.BlockSpec((tm, LANE), lambda i, j: (i, 0)),
                   *cast_specs),
        scratch_shapes=[pltpu.VMEM((tm, d), BF16)],
        compiler_params=pltpu.CompilerParams(dimension_semantics=("arbitrary", "arbitrary"),
                                             vmem_limit_bytes=VMEM_LIMIT),
        name="inproj",
    )(x2d, norm_w, mod3, mod3, w_main, w_ga, cos, sin, *casts)


RET_CHUNK = 256


def _ride_along_casts(rest, n_cast):
    for src, dst in zip(rest[:n_cast], rest[n_cast + 1:2 * n_cast + 1]):
        dst[...] = src[...].astype(dst.dtype)
    return rest[n_cast], rest[2 * n_cast + 1:]


def _ret_kernel(q_ref, k_ref, v_ref, g_ref, kc_ref, vc_ref, rd_ref, gn_ref, *rest, n_chunks, n_cast):
    o_ref, (o_scr, sc_scr, s_scr, dm_scr, eq_scr, wk_scr, sd_scr) = _ride_along_casts(rest, n_cast)
    c_len = RET_CHUNK
    ii = lax.broadcasted_iota(jnp.int32, (c_len, c_len), 0)
    jj = lax.broadcasted_iota(jnp.int32, (c_len, c_len), 1)
    rowi = lax.broadcasted_iota(jnp.int32, (c_len, RET_DK), 0).astype(F32)
    scale = RET_DK ** -0.5

    lg_f = -jnp.exp(rd_ref[0][0:1, :])
    lg_b = -jnp.exp(rd_ref[1][0:1, :])
    dm_scr[...] = jnp.where(jj <= ii, jnp.exp((ii - jj).astype(F32) * lg_f),
                            jnp.exp((jj - ii).astype(F32) * lg_b))
    for direction, lg in ((0, lg_f), (1, lg_b)):
        if direction == 0:
            eq_scr[direction] = jnp.exp((rowi + 1.0) * lg)
            wk = jnp.exp((c_len - 1.0 - rowi) * lg)
        else:
            eq_scr[direction] = jnp.exp((c_len - rowi) * lg)
            wk = jnp.exp(rowi * lg)
        wk_scr[direction] = wk
        sd_scr[direction] = jnp.broadcast_to(jnp.exp(float(c_len) * lg), (8, RET_DV))
        kcw = (kc_ref[...].astype(F32) * wk).astype(BF16)
        s_scr[direction] = _dot_tn(kcw, vc_ref[...])

    def rows_of(c):
        return pl.ds(pl.multiple_of(c * c_len, c_len), c_len)

    def score_stage(c):
        rows = rows_of(c)
        s_qk = _dot_nt(q_ref[rows, :], k_ref[rows, :])
        yield
        sc_scr[rows, :] = (s_qk * dm_scr[...]).astype(BF16)
        yield

    def value_stage(c):
        rows = rows_of(c)
        o_scr[rows, :] = _dot(sc_scr[rows, :], v_ref[rows, :])
        yield

    _run_staged(score_stage(0))

    def intra_body(t, carry):
        _run_staged(score_stage(t + 1), value_stage(t))
        return carry

    lax.fori_loop(0, n_chunks - 1, intra_body, 0, unroll=n_chunks - 1)
    _run_staged(value_stage(n_chunks - 1))

    def scan_chunk(direction, c, done):
        rows = rows_of(c)
        q = q_ref[rows, :]
        k = k_ref[rows, :]
        v = v_ref[rows, :]
        s_state = s_scr[direction]
        kv = _dot_tn((k.astype(F32) * wk_scr[direction]).astype(BF16), v)
        o_state = _dot(q, s_state.astype(BF16))
        yield
        s_scr[direction] = sd_scr[direction][0:1, :] * s_state + kv
        tot = o_scr[rows, :] + eq_scr[direction] * o_state
        if not done:
            o_scr[rows, :] = tot
        else:
            mu = jnp.mean(tot, axis=-1, keepdims=True)
            cen = tot - mu
            var = jnp.mean(cen * cen, axis=-1, keepdims=True)
            nrm = cen * (scale * lax.rsqrt(var * (scale * scale) + EPS))
            g = g_ref[rows, :]
            o_ref[rows, :] = (nrm * gn_ref[...] * (g * _sigmoid(g)).astype(F32)).astype(o_ref.dtype)
        yield

    def body(t, carry, done):
        _run_staged(scan_chunk(0, t, done), scan_chunk(1, n_chunks - 1 - t, done))
        return carry

    lax.fori_loop(0, n_chunks // 2, functools.partial(body, done=False), 0, unroll=n_chunks // 2)
    lax.fori_loop(n_chunks // 2, n_chunks, functools.partial(body, done=True), 0, unroll=n_chunks // 2)


def _retention(p, pc, rd_b, ret_gn, casts, *, batch, seq, ctx_len):
    assert ctx_len == RET_CHUNK and (seq // RET_CHUNK) % 2 == 0
    n_chunks = seq // RET_CHUNK
    w = RET_DK
    kern = functools.partial(_ret_kernel, n_chunks=n_chunks, n_cast=len(casts))
    cast_specs = [pl.BlockSpec(*_cast_plan(a, batch, RET_HEADS)) for a in casts]

    def col(name):
        return _DST[name] // w

    return pl.pallas_call(
        kern,
        out_shape=(jax.ShapeDtypeStruct((batch * seq, RET_VW), BF16),
                   *[jax.ShapeDtypeStruct(a.shape, BF16) for a in casts]),
        grid=(batch, RET_HEADS),
        in_specs=[
            pl.BlockSpec((seq, w), lambda b, h: (b, col("rq") + h)),
            pl.BlockSpec((seq, w), lambda b, h: (b, col("rk") + h)),
            pl.BlockSpec((seq, w), lambda b, h: (b, col("rv") + h)),
            pl.BlockSpec((seq, w), lambda b, h: (b, col("rg") + h)),
            pl.BlockSpec((ctx_len, w), lambda b, h: (b, col("rk") + h)),
            pl.BlockSpec((ctx_len, w), lambda b, h: (b, col("rv") + h)),
            pl.BlockSpec((2, None, 8, w), lambda b, h: (0, h, 0, 0)),
            pl.BlockSpec((1, w), lambda b, h: (0, h)),
            *cast_specs,
        ],
        out_specs=(pl.BlockSpec((seq, w), lambda b, h: (b, h)), *cast_specs),
        scratch_shapes=[
            pltpu.VMEM((seq, RET_DV), F32),
            pltpu.VMEM((seq, RET_CHUNK), BF16),
            pltpu.VMEM((2, RET_DK, RET_DV), F32),
            pltpu.VMEM((RET_CHUNK, RET_CHUNK), F32),
            pltpu.VMEM((2, RET_CHUNK, RET_DV), F32),
            pltpu.VMEM((2, RET_CHUNK, RET_DK), F32),
            pltpu.VMEM((2, 8, RET_DV), F32),
        ],
        compiler_params=pltpu.CompilerParams(dimension_semantics=("arbitrary", "arbitrary"),
                                             vmem_limit_bytes=VMEM_LIMIT),
        name="retention",
    )(p, p, p, p, pc, pc, rd_b, ret_gn, *casts)


GLA_SUB = 64
GLA_BLK = 256
GLA_SCAN_BLK = 512
PREP_UNROLL = 14


def _gla_kernel(q_ref, k_ref, v_ref, g_ref, ga_ref, kc_ref, vc_ref, gac_ref, up_ref, bias_ref, gn_ref,
                *rest, n_blk, n_cast):
    o_ref, scratch = _ride_along_casts(rest, n_cast)
    (ahi_scr, alo_scr, qf_scr, qb_scr, ktf_scr, ktb_scr, khf_scr, khb_scr, khcf_scr, khcb_scr,
     dec_scr, decc_scr, o_scr, s_scr) = scratch
    blk, sub, dk = GLA_BLK, GLA_SUB, GLA_DK
    nsub = blk // sub
    shift = sub.bit_length() - 1
    scale = GLA_DK ** -0.5

    def masks(n):
        ii = lax.broadcasted_iota(jnp.int32, (n, n), 0)
        jj = lax.broadcasted_iota(jnp.int32, (n, n), 1)
        same = (ii >> shift) == (jj >> shift)
        lower = jj <= ii
        return same, lower, jnp.where(same & lower, 1.0, 0.0).astype(BF16)

    ctx_rows = gac_ref.shape[0]
    blk_masks = masks(blk)
    ctx_masks = blk_masks if ctx_rows == blk else masks(ctx_rows)
    r8 = lax.broadcasted_iota(jnp.int32, (8, dk), 0)

    u = up_ref[...]
    u_hi, u_lo = _split_bf16(u)
    urow = lax.broadcasted_iota(jnp.int32, u.shape, 0)
    rhs = jnp.where(urow >= 2 * GA_W, u_lo, u_hi)
    bias = bias_ref[...]

    def gate_stage(ga_blk, put):
        z = _dot(ga_blk, rhs) + bias
        yield
        a = (jnp.minimum(z, 0.0) - jnp.log(1.0 + jnp.exp(-jnp.abs(z)))) * (1.0 / GLA_GATE_NORM)
        put(*_split_bf16(a))
        yield

    def decay_stage(a_hi, a_lo, k_blk, q_blk, msk, store):
        same, lower, tmat = msk
        nsub = a_hi.shape[0] // sub
        assert nsub <= 8
        pre = _dot(tmat, a_hi) + _dot(tmat, a_lo)
        yield
        a = a_hi.astype(F32) + a_lo.astype(F32)
        lasts = [pre[i * sub + sub - 1:i * sub + sub, :] for i in range(nsub)]
        tot = jnp.concatenate([jnp.broadcast_to(l, (sub, 2 * dk)) for l in lasts], axis=0)
        b_f = pre[:, :dk]
        ex_b = pre[:, dk:] - a[:, dk:]
        b_b = tot[:, dk:] - ex_b
        kf = k_blk.astype(F32)
        khf = (kf * jnp.exp(tot[:, :dk] - b_f)).astype(BF16)
        khb = (kf * jnp.exp(ex_b)).astype(BF16)

        def tile8(half):
            rows = [jnp.broadcast_to(l[:, half * dk:(half + 1) * dk], (8, dk)) for l in lasts]
            t = rows[nsub - 1]
            for i in range(nsub - 2, -1, -1):
                t = jnp.where(r8 == i, rows[i], t)
            return t

        dec = jnp.exp(jnp.concatenate([tile8(0), tile8(1), jnp.zeros((LANE - 16, dk), F32)], axis=0))
        out = dict(khf=khf, khb=khb, dec_t=dec.T)
        if q_blk is not None:
            qf32 = q_blk.astype(F32)
            out.update(qf=(qf32 * jnp.exp(b_f)).astype(BF16), qb=(qf32 * jnp.exp(b_b)).astype(BF16),
                       ktf=(kf * jnp.exp(-b_f)).astype(BF16), ktb=(kf * jnp.exp(-b_b)).astype(BF16))
        store(**out)
        yield

    def score_stage(r):
        same, lower, _ = blk_masks
        rows = pl.ds(r, blk)
        s_f = _dot_nt(qf_scr[rows, :], ktf_scr[rows, :])
        s_b = _dot_nt(qb_scr[rows, :], ktb_scr[rows, :])
        yield
        s = jnp.where(same, jnp.where(lower, s_f, s_b), 0.0).astype(BF16)
        yield
        o_scr[rows, :] = _dot(s, v_ref[rows, :])
        yield

    run = _run_staged

    def store_ctx(khf, khb, dec_t):
        khcf_scr[...] = khf
        khcb_scr[...] = khb
        decc_scr[...] = dec_t

    ctx_gate = []
    run(gate_stage(gac_ref[...], lambda hi, lo: ctx_gate.extend((hi, lo))))
    run(decay_stage(*ctx_gate, kc_ref[...], None, ctx_masks, store_ctx))

    def gate_block(t):
        rows = pl.ds(pl.multiple_of(t * blk, blk), blk)

        def put(a_hi, a_lo):
            ahi_scr[rows, :] = a_hi
            alo_scr[rows, :] = a_lo

        return gate_stage(ga_ref[rows, :], put)

    def decay_block(t):
        rows = pl.ds(pl.multiple_of(t * blk, blk), blk)

        def store(khf, khb, dec_t, qf, qb, ktf, ktb):
            khf_scr[rows, :] = khf
            khb_scr[rows, :] = khb
            qf_scr[rows, :] = qf
            qb_scr[rows, :] = qb
            ktf_scr[rows, :] = ktf
            ktb_scr[rows, :] = ktb
            dec_scr[t] = dec_t

        return decay_stage(ahi_scr[rows, :], alo_scr[rows, :], k_ref[rows, :], q_ref[rows, :],
                           blk_masks, store)

    def score_block(t):
        return score_stage(pl.multiple_of(t * blk, blk))

    run(gate_block(0))
    run(gate_block(1), decay_block(0))

    def prep_body(t, carry):
        run(gate_block(t + 2), decay_block(t + 1), score_block(t))
        return carry

    lax.fori_loop(0, n_blk - 2, prep_body, 0, unroll=PREP_UNROLL)
    run(decay_block(n_blk - 1), score_block(n_blk - 2))
    run(score_block(n_blk - 1))

    def scan_blk(direction, n_sub, dec_ts, kh_at, v_at, q_at=None, o_rows=None):
        s = s_scr[direction]
        order = range(n_sub) if direction == 0 else range(n_sub - 1, -1, -1)
        kv = {i: _dot_tn(kh_at(i), v_at(i)) for i in order}
        yield
        for i in order:
            c = direction * 8 + i % nsub
            col = dec_ts[i // nsub][:, c:c + 1]
            if q_at is not None:
                rows = o_rows(i)
                o_scr[rows, :] += _dot(q_at(i), s.astype(BF16))
            s = s * col + kv[i]
            yield
        s_scr[direction] = s

    s_scr[...] = jnp.zeros(s_scr.shape, F32)
    run(*[scan_blk(direction, ctx_rows // sub, [decc_scr[...]],
                   lambda i, khc=khc: khc[i * sub:(i + 1) * sub, :],
                   lambda i: vc_ref[i * sub:(i + 1) * sub, :])
          for direction, khc in ((0, khcf_scr), (1, khcb_scr))])

    sblk = GLA_SCAN_BLK
    per = sblk // blk
    n_sblk = n_blk // per

    fin_rows = 128

    def finalize(t):
        for tb in (t, n_sblk - 1 - t):
            for r0 in range(0, sblk, fin_rows):
                rows = pl.ds(pl.multiple_of(tb * sblk + r0, fin_rows), fin_rows)
                o = o_scr[rows, :]
                ms = jnp.mean(o * o, axis=-1, keepdims=True)
                nrm = o * (scale * lax.rsqrt(ms * (scale * scale) + EPS))
                g = g_ref[rows, :]
                o_ref[rows, :] = (nrm * gn_ref[...] * (g * _sigmoid(g)).astype(F32)).astype(o_ref.dtype)
                yield

    def scans(t):
        for direction, q_scr, kh_scr in ((0, qf_scr, khf_scr), (1, qb_scr, khb_scr)):
            tb = t if direction == 0 else n_sblk - 1 - t

            def rows(i, tb=tb):
                return pl.ds(pl.multiple_of(tb * sblk + i * sub, sub), sub)

            yield from scan_blk(direction, sblk // sub, [dec_scr[tb * per + j] for j in range(per)],
                                lambda i, kh_scr=kh_scr, rows=rows: kh_scr[rows(i), :],
                                lambda i, rows=rows: v_ref[rows(i), :],
                                lambda i, q_scr=q_scr, rows=rows: q_scr[rows(i), :],
                                rows)

    def scan_body(t, carry, fin_prev):
        if fin_prev:
            run(scans(t), finalize(t - 1))
        else:
            run(scans(t))
        return carry

    half = n_sblk // 2
    lax.fori_loop(0, half + 1, functools.partial(scan_body, fin_prev=False), 0, unroll=half + 1)
    lax.fori_loop(half + 1, n_sblk, functools.partial(scan_body, fin_prev=True), 0, unroll=n_sblk - half - 1)
    run(finalize(n_sblk - 1))


def _gla(p, ga, pc, gac, up_stack, bias2, gla_gn, casts, *, batch, seq, ctx_len):
    assert ctx_len % GLA_SUB == 0 and ctx_len <= GLA_BLK and GLA_SCAN_BLK % GLA_BLK == 0
    assert (seq // GLA_SCAN_BLK) % 2 == 0
    n_blk = seq // GLA_BLK
    kern = functools.partial(_gla_kernel, n_blk=n_blk, n_cast=len(casts))
    dk, dv = GLA_DK, GLA_DV
    cast_specs = [pl.BlockSpec(*_cast_plan(a, batch, GLA_HEADS)) for a in casts]
    return pl.pallas_call(
        kern,
        out_shape=(jax.ShapeDtypeStruct((batch * seq, GLA_VW), BF16),
                   *[jax.ShapeDtypeStruct(a.shape, BF16) for a in casts]),
        grid=(batch, GLA_HEADS),
        in_specs=[
            pl.BlockSpec((seq, dk), lambda b, h: (b, _DST["gq"] // dk + h)),
            pl.BlockSpec((seq, dk), lambda b, h: (b, _DST["gk"] // dk + h)),
            pl.BlockSpec((seq, dv), lambda b, h: (b, _DST["gv"] // dv + h)),
            pl.BlockSpec((seq, dv), lambda b, h: (b, _DST["gg"] // dv + h)),
            pl.BlockSpec((seq, LANE), lambda b, h: (b, 0)),
            pl.BlockSpec((ctx_len, dk), lambda b, h: (b, _DST["gk"] // dk + h)),
            pl.BlockSpec((ctx_len, dv), lambda b, h: (b, _DST["gv"] // dv + h)),
            pl.BlockSpec((ctx_len, LANE), lambda b, h: (b, 0)),
            pl.BlockSpec((LANE, 2 * dk), lambda b, h: (0, h)),
            pl.BlockSpec((1, 2 * dk), lambda b, h: (0, h)),
            pl.BlockSpec((1, dv), lambda b, h: (0, h)),
            *cast_specs,
        ],
        out_specs=(pl.BlockSpec((seq, dv), lambda b, h: (b, h)), *cast_specs),
        scratch_shapes=[
            pltpu.VMEM((seq, 2 * dk), BF16), pltpu.VMEM((seq, 2 * dk), BF16),
            pltpu.VMEM((seq, dk), BF16), pltpu.VMEM((seq, dk), BF16),
            pltpu.VMEM((seq, dk), BF16), pltpu.VMEM((seq, dk), BF16),
            pltpu.VMEM((seq, dk), BF16), pltpu.VMEM((seq, dk), BF16),
            pltpu.VMEM((ctx_len, dk), BF16), pltpu.VMEM((ctx_len, dk), BF16),
            pltpu.VMEM((n_blk, dk, LANE), F32), pltpu.VMEM((dk, LANE), F32),
            pltpu.VMEM((seq, dv), F32),
            pltpu.VMEM((2, dk, dv), F32),
        ],
        compiler_params=pltpu.CompilerParams(dimension_semantics=("arbitrary", "arbitrary"),
                                             vmem_limit_bytes=VMEM_LIMIT),
        name="gla",
    )(p, p, p, p, ga, pc, pc, gac, up_stack, bias2, gla_gn, *casts)


MXU_N = 256


POST_ROWS = 128


def _post_kernel(or_ref, og_ref, ga0_ref, ga1_ref, gb0_ref, gb1_ref, x_ref, wur_ref, wug_ref, wo_ref,
                 g1_ref, sh2_ref, sc2_ref, npost_ref, npre_ref, wsrc_ref, hs_ref, h2_ref, wdst_ref):
    wdst_ref[...] = wsrc_ref[...].astype(wdst_ref.dtype)
    half = ga0_ref.shape[1]
    tm = hs_ref.shape[0]
    groups = [slice(r0, r0 + POST_ROWS) for r0 in range(0, tm, POST_ROWS)]

    def up(rows):
        return _dot(or_ref[rows, :], wur_ref[...]), _dot(og_ref[rows, :], wug_ref[...])

    def out(rows, y_ret, y_gla):
        def merge(ga_r, gb_r, lo):
            return (_sigmoid(ga_r[rows, :].astype(F32)) * y_ret[:, lo:lo + half]
                    + _sigmoid(gb_r[rows, :].astype(F32)) * y_gla[:, lo:lo + half]).astype(BF16)

        merged = jnp.concatenate([merge(ga0_ref, gb0_ref, 0), merge(ga1_ref, gb1_ref, half)], axis=1)
        return _dot(merged, wo_ref[...])

    post_gain = g1_ref[...] * npost_ref[...]
    pre_gain = npre_ref[...] * (1.0 + sc2_ref[...])

    def finish(rows, y):
        hs = x_ref[rows, :] + _rms(y) * post_gain
        hs_ref[rows, :] = hs
        h2 = _rms(hs) * pre_gain + sh2_ref[...]
        h2_ref[rows, :] = h2.astype(h2_ref.dtype)

    ups = [up(rows) for rows in groups]
    ys = [out(rows, *u) for rows, u in zip(groups, ups)]
    for rows, y in zip(groups, ys):
        finish(rows, y)


def _post(o_r, o_g, p, x2d, w_up_ret, w_up_gla, w_out, mod3, npost, npre, w_later, *, tm, seq):
    m, d = x2d.shape
    tiles_per_seq = seq // tm
    const = dict(pipeline_mode=pl.Buffered(1))
    gw = d // 2
    n_steps = m // tm
    assert w_later.shape[0] % (16 * n_steps) == 0
    cast_spec = pl.BlockSpec((w_later.shape[0] // n_steps, w_later.shape[1]), lambda i: (i, 0))

    def modspec(chunk):
        return pl.BlockSpec((None, 1, d), lambda i: (i // tiles_per_seq, 0, chunk))

    def gatespec(name, part):
        return pl.BlockSpec((tm, gw), lambda i: (i, _DST[name] // gw + part))

    return pl.pallas_call(
        _post_kernel,
        out_shape=(jax.ShapeDtypeStruct((m, d), F32), jax.ShapeDtypeStruct((m, d), BF16),
                   jax.ShapeDtypeStruct(w_later.shape, BF16)),
        grid=(n_steps,),
        in_specs=[
            pl.BlockSpec((tm, RET_VW), lambda i: (i, 0)),
            pl.BlockSpec((tm, GLA_VW), lambda i: (i, 0)),
            gatespec("gate_a", 0), gatespec("gate_a", 1), gatespec("gate_b", 0), gatespec("gate_b", 1),
            pl.BlockSpec((tm, d), lambda i: (i, 0)),
            pl.BlockSpec((RET_VW, d), lambda i: (0, 0), **const),
            pl.BlockSpec((GLA_VW, d), lambda i: (0, 0), **const),
            pl.BlockSpec((d, d), lambda i: (0, 0), **const),
            modspec(2), modspec(3), modspec(4),
            pl.BlockSpec((1, d), lambda i: (0, 0)),
            pl.BlockSpec((1, d), lambda i: (0, 0)),
            cast_spec,
        ],
        out_specs=(pl.BlockSpec((tm, d), lambda i: (i, 0)),
                   pl.BlockSpec((tm, d), lambda i: (i, 0)),
                   cast_spec),
        compiler_params=pltpu.CompilerParams(dimension_semantics=("arbitrary",),
                                             vmem_limit_bytes=VMEM_LIMIT),
        name="post",
    )(o_r, o_g, p, p, p, p, x2d, w_up_ret, w_up_gla, w_out, mod3, mod3, mod3, npost, npre, w_later)


def _ffn_up_kernel(h_ref, wg_ref, wu_ref, wsrc_ref, a_ref, wdst_ref):
    wdst_ref[...] = wsrc_ref[...].astype(wdst_ref.dtype)
    h = h_ref[...]
    for lo in range(0, a_ref.shape[1], MXU_N):
        g = _dot(h, wg_ref[:, lo:lo + MXU_N])
        u = _dot(h, wu_ref[:, lo:lo + MXU_N])
        a_ref[:, lo:lo + MXU_N] = (g * _sigmoid(g) * u).astype(a_ref.dtype)


def _ffn_up(h2, wg, wu, w_later, *, tm, tf):
    m, d = h2.shape
    d_ff = wg.shape[1]
    grid = (m // tm, d_ff // tf)
    cast_spec = pl.BlockSpec(*_cast_plan(w_later, *grid))
    return pl.pallas_call(
        _ffn_up_kernel,
        out_shape=(jax.ShapeDtypeStruct((m, d_ff), BF16), jax.ShapeDtypeStruct(w_later.shape, BF16)),
        grid=grid,
        in_specs=[
            pl.BlockSpec((tm, d), lambda i, f: (i, 0)),
            pl.BlockSpec((d, tf), lambda i, f: (0, f)),
            pl.BlockSpec((d, tf), lambda i, f: (0, f)),
            cast_spec,
        ],
        out_specs=(pl.BlockSpec((tm, tf), lambda i, f: (i, f)), cast_spec),
        compiler_params=pltpu.CompilerParams(dimension_semantics=("arbitrary", "arbitrary"),
                                             vmem_limit_bytes=VMEM_LIMIT),
        name="ffn_up",
    )(h2, wg, wu, w_later)


def _ffn_down_kernel(a_ref, wd_ref, hs_ref, g2_ref, nw_ref, o_ref, y_scr, *, n_split):
    tm, d = o_ref.shape
    w = d // n_split
    ss = jnp.zeros((tm, 1), F32)
    for lo in range(0, d, w):
        y = _dot(a_ref[...], wd_ref[:, lo:lo + w])
        y_scr[:, lo:lo + w] = y
        ss = ss + jnp.sum(y * y, axis=-1, keepdims=True)
    inv = lax.rsqrt(ss * (1.0 / d) + EPS)
    o_ref[...] = hs_ref[...] + y_scr[...] * inv * (g2_ref[...] * nw_ref[...])


def _ffn_down(a, hs, wd, mod3, npost, *, tm, seq):
    m, d = hs.shape
    d_ff = wd.shape[0]
    tiles_per_seq = seq // tm
    kern = functools.partial(_ffn_down_kernel, n_split=4)
    return pl.pallas_call(
        kern,
        out_shape=jax.ShapeDtypeStruct((m, d), F32),
        grid=(m // tm,),
        in_specs=[
            pl.BlockSpec((tm, d_ff), lambda i: (i, 0)),
            pl.BlockSpec((d_ff, d), lambda i: (0, 0), pipeline_mode=pl.Buffered(1)),
            pl.BlockSpec((tm, d), lambda i: (i, 0)),
            pl.BlockSpec((None, 1, d), lambda i: (i // tiles_per_seq, 0, 5)),
            pl.BlockSpec((1, d), lambda i: (0, 0)),
        ],
        out_specs=pl.BlockSpec((tm, d), lambda i: (i, 0)),
        scratch_shapes=[pltpu.VMEM((tm, d), F32)],
        compiler_params=pltpu.CompilerParams(dimension_semantics=("arbitrary",),
                                             vmem_limit_bytes=VMEM_LIMIT),
        name="ffn_down",
    )(a, wd, hs, mod3, npost)


def _rope_tables(seq):
    f32 = np.float32
    rows = seq // GRID_W
    pos_r = np.repeat(np.arange(rows, dtype=f32), GRID_W)
    pos_c = np.tile(np.arange(GRID_W, dtype=f32), rows)
    n_f = RET_DK // 4
    inv = (f32(ROPE_BASE) ** (-np.arange(n_f, dtype=f32) / f32(n_f))).astype(f32)
    ang = np.concatenate([pos_r[:, None] * inv, pos_c[:, None] * inv], axis=-1).astype(f32)
    return jnp.asarray(np.cos(ang), F32), jnp.asarray(np.sin(ang), F32)


def _gate_map_layout(gla_a_up, gla_a_bias):
    r, h, dk = GLA_LOW_RANK, GLA_HEADS, GLA_DK
    u = jnp.zeros((GA_W, h, 2, dk), F32)
    u = u.at[:r, :, 0, :].set(gla_a_up[0].reshape(r, h, dk))
    u = u.at[r:, :, 1, :].set(gla_a_up[1].reshape(r, h, dk))
    u = u.reshape(GA_W, h * 2 * dk)
    up_stack = jnp.concatenate([u, u, u, jnp.zeros((LANE - 3 * GA_W, h * 2 * dk), F32)], axis=0)
    bias2 = jnp.stack([gla_a_bias[0].reshape(h, dk), gla_a_bias[1].reshape(h, dk)], axis=1)
    return up_stack, bias2.reshape(1, h * 2 * dk)


def _layer(h_state, c_rows, ctx2d, w_mod, b_mod, norm_mix_pre, norm_mix_post, norm_ffn_pre, norm_ffn_post,
           w_in, ret_decay, gla_a_up, gla_a_bias, ret_gn, gla_gn, w_up_ret, w_up_gla, w_out,
           ffn_w_gate, ffn_w_up, ffn_w_down, cos, sin, *, batch, seq, ctx_len):
    d = D_MODEL
    ga0 = _SRC["ga"][0]
    w_t = w_in.T
    w_g = w_t[ga0:ga0 + GA_W]
    w_ga = jnp.concatenate([w_g, w_g, w_g, jnp.zeros((LANE - 3 * GA_W, d), F32)], axis=0).astype(BF16)
    up_stack, bias2 = _gate_map_layout(gla_a_up, gla_a_bias)
    rd_b = jnp.broadcast_to(ret_decay.reshape(2, RET_HEADS, 1, 1), (2, RET_HEADS, 8, RET_DK))

    w_main, mod = _prologue(w_t, c_rows, w_mod, b_mod)
    mod3 = mod.reshape(mod.shape[0], 1, 6 * d)
    nw_pre = norm_mix_pre.reshape(1, d)

    tm = 1024
    p, ga, wur, wug, wo = _inproj(
        h_state, nw_pre, mod3, lambda i: i // (seq // tm), w_main, w_ga, cos, sin,
        tm=tm, tn=1024, ncols=N_MAIN, rope=True, seq_tiles=seq // tm,
        casts=(w_up_ret, w_up_gla, w_out))
    pc, gac = _inproj(ctx2d, nw_pre, mod3, lambda i: batch, w_main, w_ga, cos, sin,
                      tm=batch * ctx_len, tn=N_CTX // 2, ncols=N_CTX, rope=False, seq_tiles=1)

    (o_r,) = _retention(p, pc, rd_b, ret_gn.reshape(1, RET_VW), (), batch=batch, seq=seq, ctx_len=ctx_len)
    o_g, wu = _gla(p, ga, pc, gac, up_stack, bias2, gla_gn.reshape(1, GLA_VW), (ffn_w_up,),
                   batch=batch, seq=seq, ctx_len=ctx_len)

    hs, h2, wg = _post(o_r, o_g, p, h_state, wur, wug, wo,
                       mod3, norm_mix_post.reshape(1, d), norm_ffn_pre.reshape(1, d), ffn_w_gate,
                       tm=256, seq=seq)
    act, wd = _ffn_up(h2, wg, wu, ffn_w_down, tm=1024, tf=512)
    return _ffn_down(act, hs, wd, mod3, norm_ffn_post.reshape(1, d), tm=256, seq=seq)


def kernel(x, c, ctx, c_ctx, w_mod, b_mod, norm_mix_pre, norm_mix_post, norm_ffn_pre, norm_ffn_post,
           w_in, ret_decay, gla_a_up, gla_a_bias, ret_gn, gla_gn, w_up_ret, w_up_gla, w_out,
           ffn_w_gate, ffn_w_up, ffn_w_down):
    batch, seq, d = x.shape
    ctx_len = ctx.shape[1]
    depth = w_mod.shape[0]
    cos, sin = _rope_tables(seq)
    c_rows = jnp.zeros((16, d), F32).at[:batch].set(c.astype(F32)).at[batch].set(c_ctx.astype(F32))
    ctx2d = ctx.astype(F32).reshape(batch * ctx_len, d)
    h_state = x.astype(F32).reshape(batch * seq, d)
    for i in range(depth):
        h_state = _layer(h_state, c_rows, ctx2d, w_mod[i], b_mod[i], norm_mix_pre[i], norm_mix_post[i],
                         norm_ffn_pre[i], norm_ffn_post[i], w_in[i], ret_decay[i], gla_a_up[i], gla_a_bias[i],
                         ret_gn[i], gla_gn[i], w_up_ret[i], w_up_gla[i], w_out[i],
                         ffn_w_gate[i], ffn_w_up[i], ffn_w_down[i], cos, sin,
                         batch=batch, seq=seq, ctx_len=ctx_len)
    return h_state.reshape(batch, seq, d).astype(x.dtype)
```
